```python
import math
import jax
import jax.numpy as jnp
from jax import lax
import numpy as np

D_MODEL = 1024
BATCH = 2
SEQ = 8192
DEPTH = 2

GRID_W = 64
CTX_LEN = 256
NORM_EPS = 1e-6
N_MOD = 6

D_MIX = D_MODEL
N_MIXERS = 4
GROUP_WIDTH = D_MIX // N_MIXERS

SSD_WIDTH = GROUP_WIDTH
SSD_HEAD_DIM = 64
SSD_HEADS = SSD_WIDTH // SSD_HEAD_DIM
SSD_GROUPS = 2
SSD_STATE = 128
SSD_CONV = 5
SSD_CHUNK = 128
SSD_CONV_CH = SSD_WIDTH + 2 * SSD_GROUPS * SSD_STATE
SSD_IN = SSD_WIDTH + SSD_CONV_CH + 2 * SSD_HEADS

GLA_WIDTH = GROUP_WIDTH
GLA_HEADS = 4
GLA_DV = GLA_WIDTH // GLA_HEADS
GLA_DK = GLA_DV // 2
GLA_QK = GLA_HEADS * GLA_DK
GLA_GATE_RANK = 16
GLA_TAU = 16.0
GLA_CHUNK = 64
GLA_IN = 2 * GLA_QK + 2 * GLA_WIDTH + 2 * GLA_GATE_RANK

MLA_WIDTH = GROUP_WIDTH
MLA_HEADS = 4
MLA_V = MLA_WIDTH // MLA_HEADS
MLA_NOPE = 64
MLA_ROPE = 32
MLA_Q_RANK = 256
MLA_KV_RANK = 128
MLA_QBLOCK = 128
MLA_SCALE = (MLA_NOPE + MLA_ROPE) ** -0.5
ROPE_BASE = 10000.0
MLA_IN = MLA_Q_RANK + MLA_KV_RANK + MLA_ROPE

S5_WIDTH = GROUP_WIDTH
S5_GROUP = 16
S5_NGROUPS = S5_WIDTH // S5_GROUP
S5_STATE = 64
S5_MAX_RE = -1e-4
S5_IN = S5_WIDTH

D_IN = SSD_IN + GLA_IN + MLA_IN + S5_IN

PEER_KEYS = 128
PEER_EXPERTS = PEER_KEYS * PEER_KEYS
PEER_HEADS = 8
PEER_TOPK = 16
PEER_DQ = 128
PEER_BLOCK = 128

kernel_name = 'hybrid_parallel_heads_peer_flow_block'


def _rmsnorm(x, g):
    xf = x.astype(jnp.float32)
    y = xf * lax.rsqrt(jnp.mean(xf * xf, axis=-1, keepdims=True) + NORM_EPS)
    return y.astype(x.dtype) * g


def _modulate(x, g, shift, scale):
    return _rmsnorm(x, g) * (1 + scale) + shift


def _split(t, sizes):
    out, start = [], 0
    for s in sizes:
        out.append(t[..., start:start + s])
        start += s
    return out


def _dwconv_centered(x, w, b):
    k = w.shape[0]
    left = k // 2
    y = lax.conv_general_dilated(x, w[:, None, :].astype(x.dtype), window_strides=(1,),
                                 padding=[(left, k - 1 - left)],
                                 dimension_numbers=('NWC', 'WIO', 'NWC'),
                                 feature_group_count=x.shape[-1])
    return y + b


def _bidirectional_with_prefix(scan_f, scan_b, ctx_seq, lat_seq, h0):
    flip = lambda seq: tuple(jnp.flip(a, axis=1) for a in seq)
    yc_f, hc_f = scan_f(ctx_seq, h0)
    yc_b, hc_b = scan_b(flip(ctx_seq), h0)
    yl_f, _ = scan_f(lat_seq, hc_f)
    yl_b, _ = scan_b(flip(lat_seq), hc_b)
    return yc_f + jnp.flip(yc_b, axis=1), yl_f + jnp.flip(yl_b, axis=1)


def _ssd_chunk_scan(x, dt, a_head, bm, cm, h0):
    bsz, length, nh, hp = x.shape
    ns = bm.shape[-1]
    nc = length // SSD_CHUNK
    a = (dt * a_head).reshape(bsz, nc, SSD_CHUNK, nh)
    xd = (x * dt[..., None]).reshape(bsz, nc, SSD_CHUNK, nh, hp)
    bm = bm.reshape(bsz, nc, SSD_CHUNK, nh, ns)
    cm = cm.reshape(bsz, nc, SSD_CHUNK, nh, ns)
    acum = jnp.cumsum(a, axis=2)
    lower = jnp.tril(jnp.ones((SSD_CHUNK, SSD_CHUNK), bool))
    seg = acum[:, :, :, None, :] - acum[:, :, None, :, :]
    decay = jnp.exp(jnp.where(lower[None, None, :, :, None], seg, -jnp.inf))
    scores = jnp.einsum('bcihn,bcjhn->bcijh', cm, bm) * decay
    y_diag = jnp.einsum('bcijh,bcjhp->bcihp', scores, xd)
    to_end = jnp.exp(acum[:, :, -1:, :] - acum)
    states = jnp.einsum('bcjh,bcjhn,bcjhp->bchpn', to_end, bm, xd)
    chunk_decay = jnp.exp(acum[:, :, -1, :])

    def step(h, inp):
        s_c, g_c = inp
        return g_c[:, :, None, None] * h + s_c, h

    h_fin, h_in = lax.scan(step, h0, (jnp.moveaxis(states, 1, 0), jnp.moveaxis(chunk_decay, 1, 0)))
    h_in = jnp.moveaxis(h_in, 0, 1)
    y_off = jnp.einsum('bcihn,bchpn->bcihp', cm, h_in) * jnp.exp(acum)[..., None]
    return (y_diag + y_off).reshape(bsz, length, nh, hp), h_fin


def _ssd_mixer(pc, pl, conv_w, conv_b, a_log, dt_bias, d_skip, norm_g, ctx_out):
    f32 = jnp.float32
    dtype = pl.dtype
    rep = SSD_HEADS // SSD_GROUPS

    def prep(p):
        bsz, length = p.shape[:2]
        z, xbc, dt_raw = _split(p, (SSD_WIDTH, SSD_CONV_CH, 2 * SSD_HEADS))
        xbc = jax.nn.silu(_dwconv_centered(xbc, conv_w, conv_b)).astype(f32)
        xs, bm, cm = _split(xbc, (SSD_WIDTH, SSD_GROUPS * SSD_STATE, SSD_GROUPS * SSD_STATE))
        xs = xs.reshape(bsz, length, SSD_HEADS, SSD_HEAD_DIM)
        bm = jnp.repeat(bm.reshape(bsz, length, SSD_GROUPS, SSD_STATE), rep, axis=2)
        cm = jnp.repeat(cm.reshape(bsz, length, SSD_GROUPS, SSD_STATE), rep, axis=2)
        dt_raw = dt_raw.astype(f32).reshape(bsz, length, 2, SSD_HEADS)
        return z, (xs, bm, cm, dt_raw)

    zc, seq_c = prep(pc)
    zl, seq_l = prep(pl)

    def make_scan(d):
        a_head = -jnp.exp(a_log[d].astype(f32))
        bias = dt_bias[d].astype(f32)

        def scan(seq, h0):
            xs, bm, cm, dt_raw = seq
            dt = jax.nn.softplus(dt_raw[:, :, d] + bias)
            return _ssd_chunk_scan(xs, dt, a_head, bm, cm, h0)
        return scan

    h0 = jnp.zeros((pl.shape[0], SSD_HEADS, SSD_HEAD_DIM, SSD_STATE), f32)
    yc, yl = _bidirectional_with_prefix(make_scan(0), make_scan(1), seq_c, seq_l, h0)

    def finish(y, xs, z):
        y = y + d_skip.astype(f32)[:, None] * xs
        y = y.reshape(y.shape[0], y.shape[1], SSD_WIDTH).astype(dtype) * jax.nn.silu(z)
        return _rmsnorm(y, norm_g)

    out_c = finish(yc, seq_c[0], zc) if ctx_out else None
    return out_c, finish(yl, seq_l[0], zl)


def _gla_chunk_scan(q, k, v, logg, s0):
    bsz, length, nh, _ = q.shape
    dv = v.shape[-1]
    nc = length // GLA_CHUNK
    rs = lambda t: t.reshape(bsz, nc, GLA_CHUNK, nh, t.shape[-1])
    q, k, v, logg = rs(q), rs(k), rs(v), rs(logg)
    b = jnp.cumsum(logg, axis=2)
    qe = q * jnp.exp(b)
    ke = k * jnp.exp(-b)
    lower = jnp.tril(jnp.ones((GLA_CHUNK, GLA_CHUNK), bool))
    att = jnp.where(lower, jnp.einsum('bcihd,bcjhd->bchij', qe, ke), 0.0)
    o_intra = jnp.einsum('bchij,bcjhv->bcihv', att, v)
    b_last = b[:, :, -1]
    kd = k * jnp.exp(b_last[:, :, None] - b)
    local = jnp.einsum('bcjhd,bcjhv->bchdv', kd, v)

    def step(s, inp):
        g_c, u_c = inp
        return g_c[..., None] * s + u_c, s

    s_fin, s_in = lax.scan(step, s0, (jnp.moveaxis(jnp.exp(b_last), 1, 0), jnp.moveaxis(local, 1, 0)))
    s_in = jnp.moveaxis(s_in, 0, 1)
    o_inter = jnp.einsum('bcihd,bchdv->bcihv', qe, s_in)
    return (o_intra + o_inter).reshape(bsz, length, nh, dv), s_fin


def _gla_mixer(pc, pl, gate_w, gate_b, norm_g, ctx_out):
    f32 = jnp.float32
    dtype = pl.dtype

    def prep(p):
        bsz, length = p.shape[:2]
        q, k, v, r, glr = _split(p, (GLA_QK, GLA_QK, GLA_WIDTH, GLA_WIDTH, 2 * GLA_GATE_RANK))
        hd = lambda t, dim: t.astype(f32).reshape(bsz, length, GLA_HEADS, dim)
        seq = (hd(q, GLA_DK) * GLA_DK ** -0.5, hd(k, GLA_DK), hd(v, GLA_DV),
               glr.astype(f32).reshape(bsz, length, 2, GLA_GATE_RANK))
        return r, seq

    rc, seq_c = prep(pc)
    rl, seq_l = prep(pl)

    def make_scan(d):
        w = gate_w[d].astype(f32)
        bias = gate_b[d].astype(f32)

        def scan(seq, s0):
            q, k, v, glr = seq
            logg = (jax.nn.log_sigmoid(glr[:, :, d] @ w + bias) / GLA_TAU).reshape(q.shape)
            return _gla_chunk_scan(q, k, v, logg, s0)
        return scan

    s0 = jnp.zeros((pl.shape[0], GLA_HEADS, GLA_DK, GLA_DV), f32)
    oc, ol = _bidirectional_with_prefix(make_scan(0), make_scan(1), seq_c, seq_l, s0)

    def finish(o, r):
        o = _rmsnorm(o.astype(dtype), norm_g)
        return o.reshape(o.shape[0], o.shape[1], GLA_WIDTH) * jax.nn.silu(r)

    out_c = finish(oc, rc) if ctx_out else None
    return out_c, finish(ol, rl)


def _axial_rope(length, dtype):
    rows = length // GRID_W
    row = jnp.repeat(jnp.arange(rows, dtype=jnp.float32), GRID_W)
    col = jnp.tile(jnp.arange(GRID_W, dtype=jnp.float32), rows)
    half = MLA_ROPE // 2
    inv = ROPE_BASE ** (-jnp.arange(0, half, 2, dtype=jnp.float32) / half)
    ang = jnp.concatenate([row[:, None] * inv, col[:, None] * inv], axis=-1)
    return jnp.cos(ang).astype(dtype), jnp.sin(ang).astype(dtype)


def _apply_rope(x, cos, sin):
    h = x.shape[-1] // 2
    x1, x2 = x[..., :h], x[..., h:]
    return jnp.concatenate([x1 * cos - x2 * sin, x1 * sin + x2 * cos], axis=-1)


def _attention(q, k, v):
    s = jnp.einsum('bqhd,bkhd->bhqk', q, k).astype(jnp.float32) * MLA_SCALE
    p = jax.nn.softmax(s, axis=-1).astype(v.dtype)
    return jnp.einsum('bhqk,bkhv->bqhv', p, v)


def _block_attention(q, k, v):
    bsz, length = q.shape[:2]
    nb = length // MLA_QBLOCK
    qb = jnp.moveaxis(q.reshape(bsz, nb, MLA_QBLOCK, q.shape[2], q.shape[3]), 1, 0)
    out = lax.map(lambda blk: _attention(blk, k, v), qb)
    return jnp.moveaxis(out, 0, 1).reshape(bsz, length, out.shape[3], out.shape[4])


def _mla_mixer(pc, pl, q_norm_g, w_uq, kv_norm_g, w_ukv, ctx_out):
    def project(p):
        bsz, length = p.shape[:2]
        cq, ckv, kr = _split(p, (MLA_Q_RANK, MLA_KV_RANK, MLA_ROPE))
        q = (_rmsnorm(cq, q_norm_g) @ w_uq).reshape(bsz, length, MLA_HEADS, MLA_NOPE + MLA_ROPE)
        kv = (_rmsnorm(ckv, kv_norm_g) @ w_ukv).reshape(bsz, length, MLA_HEADS, MLA_NOPE + MLA_V)
        return q, kv[..., :MLA_NOPE], kr, kv[..., MLA_NOPE:]

    def full_keys(k_nope, k_rope):
        shared = jnp.broadcast_to(k_rope[:, :, None, :], k_nope.shape[:3] + (MLA_ROPE,))
        return jnp.concatenate([k_nope, shared], axis=-1)

    qc, knc, krc, vc = project(pc)
    ql, knl, krl, vl = project(pl)
    cos, sin = _axial_rope(pl.shape[1], pl.dtype)
    ql = jnp.concatenate([ql[..., :MLA_NOPE],
                          _apply_rope(ql[..., MLA_NOPE:], cos[:, None, :], sin[:, None, :])], axis=-1)
    krl = _apply_rope(krl, cos, sin)
    kc = full_keys(knc, krc)
    kl = full_keys(knl, krl)
    k_all = jnp.concatenate([kl, kc], axis=1)
    v_all = jnp.concatenate([vl, vc], axis=1)
    yl = _block_attention(ql, k_all, v_all)
    yl = yl.reshape(yl.shape[0], yl.shape[1], MLA_WIDTH)
    if ctx_out:
        yc = _attention(qc, kc, vc)
        yc = yc.reshape(yc.shape[0], yc.shape[1], MLA_WIDTH)
    else:
        yc = None
    return yc, yl


def _linear_recurrence_op(e1, e2):
    a1, b1 = e1
    a2, b2 = e2
    return a1 * a2, a2 * b1 + b2


def _s5_mixer(pc, pl, a_re, a_im, log_dt, b_re, b_im, c_re, c_im, d_skip, glu_w, glu_b, ctx_out):
    f32 = jnp.float32
    dtype = pl.dtype
    bmat = b_re.astype(f32) + 1j * b_im.astype(f32)
    group = lambda p: p.astype(f32).reshape(p.shape[0], p.shape[1], S5_NGROUPS, S5_GROUP)
    uc, ul = group(pc), group(pl)

    def make_scan(d):
        lam = jnp.minimum(a_re[d].astype(f32), S5_MAX_RE) + 1j * a_im[d].astype(f32)
        step = jnp.exp(log_dt[d].astype(f32))[:, None]
        abar = jnp.exp(lam * step)
        bbar = ((abar - 1.0) / lam)[:, :, None] * bmat
        cmat = c_re[d].astype(f32) + 1j * c_im[d].astype(f32)

        def scan(seq, h0):
            (u,) = seq
            bu = jnp.einsum('blgc,gnc->blgn', u.astype(jnp.complex64), bbar)
            bu = bu.at[:, 0].add(abar * h0)
            _, h = lax.associative_scan(_linear_recurrence_op, (jnp.broadcast_to(abar, bu.shape), bu), axis=1)
            return jnp.einsum('blgn,gcn->blgc', h, cmat).real, h[:, -1]
        return scan

    h0 = jnp.zeros((pl.shape[0], S5_NGROUPS, S5_STATE), jnp.complex64)
    yc, yl = _bidirectional_with_prefix(make_scan(0), make_scan(1), (uc,), (ul,), h0)
    dmat = d_skip.astype(f32).reshape(S5_NGROUPS, S5_GROUP)

    def finish(y, u):
        y = (y + dmat * u).reshape(u.shape[0], u.shape[1], S5_WIDTH).astype(dtype)
        y = jax.nn.gelu(y, approximate=False)
        return y * jax.nn.sigmoid(y @ glu_w + glu_b)

    out_c = finish(yc, uc) if ctx_out else None
    return out_c, finish(yl, ul)


def _peer(xn, w_q, sub_keys, u_tab, v_tab):
    shape = xn.shape
    xt = xn.reshape(-1, shape[-1])
    n_tok = xt.shape[0]
    q = (xt @ w_q).reshape(n_tok, PEER_HEADS, 2, PEER_DQ // 2)
    s1 = jnp.einsum('thd,hnd->thn', q[:, :, 0], sub_keys[0]).astype(jnp.float32)
    s2 = jnp.einsum('thd,hnd->thn', q[:, :, 1], sub_keys[1]).astype(jnp.float32)
    v1, i1 = lax.top_k(s1, PEER_TOPK)
    v2, i2 = lax.top_k(s2, PEER_TOPK)
    cand = (v1[..., :, None] + v2[..., None, :]).reshape(n_tok, PEER_HEADS, PEER_TOPK * PEER_TOPK)
    cidx = (i1[..., :, None] * PEER_KEYS + i2[..., None, :]).reshape(n_tok, PEER_HEADS, PEER_TOPK * PEER_TOPK)
    best, pos = lax.top_k(cand, PEER_TOPK)
    experts = jnp.take_along_axis(cidx, pos, axis=-1).reshape(n_tok, PEER_HEADS * PEER_TOPK)
    gates = jax.nn.softmax(best, axis=-1).reshape(n_tok, PEER_HEADS * PEER_TOPK).astype(xn.dtype)
    nb = n_tok // PEER_BLOCK

    def block(args):
        xb, eb, gb = args
        hid = jnp.einsum('td,tkd->tk', xb, u_tab[eb])
        return jnp.einsum('tk,tkd->td', gb * jax.nn.gelu(hid, approximate=False), v_tab[eb])

    out = lax.map(block, (xt.reshape(nb, PEER_BLOCK, shape[-1]),
                          experts.reshape(nb, PEER_BLOCK, -1),
                          gates.reshape(nb, PEER_BLOCK, -1)))
    return out.reshape(shape)


def setup_inputs(seed: int = 0) -> dict:
    key = jax.random.key(seed)
    ks = iter(jax.random.split(key, 48))
    f32 = jnp.float32
    nd = DEPTH

    def nrm(shape, scale):
        return jax.random.normal(next(ks), shape, f32) * scale

    def gain(shape):
        return 1.0 + nrm(shape, 0.02)

    def unif(shape, lo, hi):
        return jax.random.uniform(next(ks), shape, f32, lo, hi)

    x = nrm((BATCH, SEQ, D_MODEL), 1.0)
    c = nrm((BATCH, D_MODEL), 1.0)
    ctx = nrm((BATCH, CTX_LEN, D_MODEL), 1.0)
    c_ctx = nrm((D_MODEL,), 1.0)
    ada_w = nrm((nd, D_MODEL, N_MOD * D_MODEL), 0.5 * D_MODEL ** -0.5)
    ada_b = nrm((nd, N_MOD * D_MODEL), 0.02)
    norm_mix_g = gain((nd, D_MODEL))
    norm_ffn_g = gain((nd, D_MODEL))
    w_in = nrm((nd, D_MODEL, D_IN), D_MODEL ** -0.5)
    w_out = nrm((nd, D_MIX, D_MODEL), D_MIX ** -0.5)
    ssd_conv_w = nrm((nd, SSD_CONV, SSD_CONV_CH), SSD_CONV ** -0.5)
    ssd_conv_b = nrm((nd, SSD_CONV_CH), 0.02)
    ssd_a_log = jnp.log(unif((nd, 2, SSD_HEADS), 1.0, 16.0))
    dt0 = jnp.exp(unif((nd, 2, SSD_HEADS), math.log(1e-3), math.log(1e-1)))
    ssd_dt_bias = dt0 + jnp.log(-jnp.expm1(-dt0))
    ssd_d = 1.0 + nrm((nd, SSD_HEADS), 0.1)
    ssd_norm_g = gain((nd, SSD_WIDTH))
    gla_gate_w = nrm((nd, 2, GLA_GATE_RANK, GLA_QK), GLA_GATE_RANK ** -0.5)
    gla_gate_b = nrm((nd, 2, GLA_QK), 0.1)
    gla_norm_g = gain((nd, GLA_DV))
    mla_q_norm_g = gain((nd, MLA_Q_RANK))
    mla_w_uq = nrm((nd, MLA_Q_RANK, MLA_HEADS * (MLA_NOPE + MLA_ROPE)), MLA_Q_RANK ** -0.5)
    mla_kv_norm_g = gain((nd, MLA_KV_RANK))
    mla_w_ukv = nrm((nd, MLA_KV_RANK, MLA_HEADS * (MLA_NOPE + MLA_V)), MLA_KV_RANK ** -0.5)
    s5_a_re = -0.5 + nrm((nd, 2, S5_NGROUPS, S5_STATE), 0.01)
    s5_a_im = jnp.pi * jnp.arange(S5_STATE, dtype=f32) + nrm((nd, 2, S5_NGROUPS, S5_STATE), 0.01)
    s5_log_dt = unif((nd, 2, S5_NGROUPS), math.log(1e-3), math.log(1e-1))
    s5_b_re = nrm((nd, S5_NGROUPS, S5_STATE, S5_GROUP), (2 * S5_GROUP) ** -0.5)
    s5_b_im = nrm((nd, S5_NGROUPS, S5_STATE, S5_GROUP), (2 * S5_GROUP) ** -0.5)
    s5_c_re = nrm((nd, 2, S5_NGROUPS, S5_GROUP, S5_STATE), (2 * S5_STATE) ** -0.5)
    s5_c_im = nrm((nd, 2, S5_NGROUPS, S5_GROUP, S5_STATE), (2 * S5_STATE) ** -0.5)
    s5_d = nrm((nd, S5_WIDTH), 1.0)
    s5_glu_w = nrm((nd, S5_WIDTH, S5_WIDTH), S5_WIDTH ** -0.5)
    s5_glu_b = nrm((nd, S5_WIDTH), 0.02)
    peer_w_q = nrm((nd, D_MODEL, PEER_HEADS * PEER_DQ), D_MODEL ** -0.5)
    peer_sub_keys = nrm((nd, 2, PEER_HEADS, PEER_KEYS, PEER_DQ // 2), (PEER_DQ // 2) ** -0.5)
    peer_u = nrm((nd, PEER_EXPERTS, D_MODEL), D_MODEL ** -0.5)
    peer_v = nrm((nd, PEER_EXPERTS, D_MODEL), 1.0)
    final_norm_g = gain((D_MODEL,))
    return {'x': x, 'c': c, 'ctx': ctx, 'c_ctx': c_ctx, 'ada_w': ada_w, 'ada_b': ada_b,
            'norm_mix_g': norm_mix_g, 'norm_ffn_g': norm_ffn_g, 'w_in': w_in, 'w_out': w_out,
            'ssd_conv_w': ssd_conv_w, 'ssd_conv_b': ssd_conv_b, 'ssd_a_log': ssd_a_log,
            'ssd_dt_bias': ssd_dt_bias, 'ssd_d': ssd_d, 'ssd_norm_g': ssd_norm_g,
            'gla_gate_w': gla_gate_w, 'gla_gate_b': gla_gate_b, 'gla_norm_g': gla_norm_g,
            'mla_q_norm_g': mla_q_norm_g, 'mla_w_uq': mla_w_uq, 'mla_kv_norm_g': mla_kv_norm_g,
            'mla_w_ukv': mla_w_ukv, 's5_a_re': s5_a_re, 's5_a_im': s5_a_im, 's5_log_dt': s5_log_dt,
            's5_b_re': s5_b_re, 's5_b_im': s5_b_im, 's5_c_re': s5_c_re, 's5_c_im': s5_c_im,
            's5_d': s5_d, 's5_glu_w': s5_glu_w, 's5_glu_b': s5_glu_b, 'peer_w_q': peer_w_q,
            'peer_sub_keys': peer_sub_keys, 'peer_u': peer_u, 'peer_v': peer_v,
            'final_norm_g': final_norm_g}


def reference(x, c, ctx, c_ctx, ada_w, ada_b, norm_mix_g, norm_ffn_g, w_in, w_out,
              ssd_conv_w, ssd_conv_b, ssd_a_log, ssd_dt_bias, ssd_d, ssd_norm_g,
              gla_gate_w, gla_gate_b, gla_norm_g, mla_q_norm_g, mla_w_uq, mla_kv_norm_g, mla_w_ukv,
              s5_a_re, s5_a_im, s5_log_dt, s5_b_re, s5_b_im, s5_c_re, s5_c_im, s5_d, s5_glu_w, s5_glu_b,
              peer_w_q, peer_sub_keys, peer_u, peer_v, final_norm_g):
    h_lat, h_ctx = x, ctx
    cond_lat = jax.nn.silu(c)[:, None, :]
    cond_ctx = jax.nn.silu(c_ctx)[None, None, :]
    splits = (SSD_IN, GLA_IN, MLA_IN, S5_IN)
    for i in range(DEPTH):
        ctx_out = i < DEPTH - 1
        mod_l = jnp.split(cond_lat @ ada_w[i] + ada_b[i], N_MOD, axis=-1)
        mod_c = jnp.split(cond_ctx @ ada_w[i] + ada_b[i], N_MOD, axis=-1)
        p_l = _modulate(h_lat, norm_mix_g[i], mod_l[0], mod_l[1]) @ w_in[i]
        p_c = _modulate(h_ctx, norm_mix_g[i], mod_c[0], mod_c[1]) @ w_in[i]
        ssd_pl, gla_pl, mla_pl, s5_pl = _split(p_l, splits)
        ssd_pc, gla_pc, mla_pc, s5_pc = _split(p_c, splits)
        ssd_c, ssd_l = _ssd_mixer(ssd_pc, ssd_pl, ssd_conv_w[i], ssd_conv_b[i], ssd_a_log[i],
                                  ssd_dt_bias[i], ssd_d[i], ssd_norm_g[i], ctx_out)
        gla_c, gla_l = _gla_mixer(gla_pc, gla_pl, gla_gate_w[i], gla_gate_b[i], gla_norm_g[i], ctx_out)
        mla_c, mla_l = _mla_mixer(mla_pc, mla_pl, mla_q_norm_g[i], mla_w_uq[i], mla_kv_norm_g[i],
                                  mla_w_ukv[i], ctx_out)
        s5_c, s5_l = _s5_mixer(s5_pc, s5_pl, s5_a_re[i], s5_a_im[i], s5_log_dt[i], s5_b_re[i], s5_b_im[i],
                               s5_c_re[i], s5_c_im[i], s5_d[i], s5_glu_w[i], s5_glu_b[i], ctx_out)
        mix_l = jnp.concatenate([ssd_l, gla_l, mla_l, s5_l], axis=-1) @ w_out[i]
        h_lat = h_lat + mod_l[2] * mix_l
        f_l = _modulate(h_lat, norm_ffn_g[i], mod_l[3], mod_l[4])
        h_lat = h_lat + mod_l[5] * _peer(f_l, peer_w_q[i], peer_sub_keys[i], peer_u[i], peer_v[i])
        if ctx_out:
            mix_c = jnp.concatenate([ssd_c, gla_c, mla_c, s5_c], axis=-1) @ w_out[i]
            h_ctx = h_ctx + mod_c[2] * mix_c
            f_c = _modulate(h_ctx, norm_ffn_g[i], mod_c[3], mod_c[4])
            h_ctx = h_ctx + mod_c[5] * _peer(f_c, peer_w_q[i], peer_sub_keys[i], peer_u[i], peer_v[i])
    return _rmsnorm(h_lat, final_norm_g)
```

```python
import math
import functools
import jax
import jax.numpy as jnp
from jax import lax
import numpy as np
from jax.experimental import pallas as pl
from jax.experimental.pallas import tpu as pltpu

D_MODEL = 1024
BATCH = 2
SEQ = 8192
DEPTH = 2

GRID_W = 64
CTX_LEN = 256
NORM_EPS = 1e-6
N_MOD = 6

D_MIX = D_MODEL
N_MIXERS = 4
GROUP_WIDTH = D_MIX // N_MIXERS

SSD_WIDTH = GROUP_WIDTH
SSD_HEAD_DIM = 64
SSD_HEADS = SSD_WIDTH // SSD_HEAD_DIM
SSD_GROUPS = 2
SSD_STATE = 128
SSD_CONV = 5
SSD_CHUNK = 128
SSD_CONV_CH = SSD_WIDTH + 2 * SSD_GROUPS * SSD_STATE
SSD_IN = SSD_WIDTH + SSD_CONV_CH + 2 * SSD_HEADS

GLA_WIDTH = GROUP_WIDTH
GLA_HEADS = 4
GLA_DV = GLA_WIDTH // GLA_HEADS
GLA_DK = GLA_DV // 2
GLA_QK = GLA_HEADS * GLA_DK
GLA_GATE_RANK = 16
GLA_TAU = 16.0
GLA_CHUNK = 64
GLA_IN = 2 * GLA_QK + 2 * GLA_WIDTH + 2 * GLA_GATE_RANK

MLA_WIDTH = GROUP_WIDTH
MLA_HEADS = 4
MLA_V = MLA_WIDTH // MLA_HEADS
MLA_NOPE = 64
MLA_ROPE = 32
MLA_Q_RANK = 256
MLA_KV_RANK = 128
MLA_QBLOCK = 128
MLA_SCALE = (MLA_NOPE + MLA_ROPE) ** -0.5
ROPE_BASE = 10000.0
MLA_IN = MLA_Q_RANK + MLA_KV_RANK + MLA_ROPE

S5_WIDTH = GROUP_WIDTH
S5_GROUP = 16
S5_NGROUPS = S5_WIDTH // S5_GROUP
S5_STATE = 64
S5_MAX_RE = -1e-4
S5_IN = S5_WIDTH

D_IN = SSD_IN + GLA_IN + MLA_IN + S5_IN

PEER_KEYS = 128
PEER_EXPERTS = PEER_KEYS * PEER_KEYS
PEER_HEADS = 8
PEER_TOPK = 16
PEER_DQ = 128
PEER_BLOCK = 128


def _rmsnorm(x, g):
    xf = x.astype(jnp.float32)
    y = xf * lax.rsqrt(jnp.mean(xf * xf, axis=-1, keepdims=True) + NORM_EPS)
    return y.astype(x.dtype) * g


def _modulate(x, g, shift, scale):
    return _rmsnorm(x, g) * (1 + scale) + shift


def _split(t, sizes):
    out, start = [], 0
    for s in sizes:
        out.append(t[..., start:start + s])
        start += s
    return out


def _dwconv_centered(x, w, b):
    k = w.shape[0]
    left = k // 2
    y = lax.conv_general_dilated(x, w[:, None, :].astype(x.dtype), window_strides=(1,),
                                 padding=[(left, k - 1 - left)],
                                 dimension_numbers=('NWC', 'WIO', 'NWC'),
                                 feature_group_count=x.shape[-1])
    return y + b


def _bidirectional_with_prefix(scan_f, scan_b, ctx_seq, lat_seq, h0):
    flip = lambda seq: tuple(jnp.flip(a, axis=1) for a in seq)
    yc_f, hc_f = scan_f(ctx_seq, h0)
    yc_b, hc_b = scan_b(flip(ctx_seq), h0)
    yl_f, _ = scan_f(lat_seq, hc_f)
    yl_b, _ = scan_b(flip(lat_seq), hc_b)
    return yc_f + jnp.flip(yc_b, axis=1), yl_f + jnp.flip(yl_b, axis=1)


def _ssd_chunk_scan(x, dt, a_head, bm, cm, h0):
    bsz, length, nh, hp = x.shape
    ns = bm.shape[-1]
    nc = length // SSD_CHUNK
    a = (dt * a_head).reshape(bsz, nc, SSD_CHUNK, nh)
    xd = (x * dt[..., None]).reshape(bsz, nc, SSD_CHUNK, nh, hp)
    bm = bm.reshape(bsz, nc, SSD_CHUNK, nh, ns)
    cm = cm.reshape(bsz, nc, SSD_CHUNK, nh, ns)
    acum = jnp.cumsum(a, axis=2)
    lower = jnp.tril(jnp.ones((SSD_CHUNK, SSD_CHUNK), bool))
    seg = acum[:, :, :, None, :] - acum[:, :, None, :, :]
    decay = jnp.exp(jnp.where(lower[None, None, :, :, None], seg, -jnp.inf))
    scores = jnp.einsum('bcihn,bcjhn->bcijh', cm, bm) * decay
    y_diag = jnp.einsum('bcijh,bcjhp->bcihp', scores, xd)
    to_end = jnp.exp(acum[:, :, -1:, :] - acum)
    states = jnp.einsum('bcjh,bcjhn,bcjhp->bchpn', to_end, bm, xd)
    chunk_decay = jnp.exp(acum[:, :, -1, :])

    def step(h, inp):
        s_c, g_c = inp
        return g_c[:, :, None, None] * h + s_c, h

    h_fin, h_in = lax.scan(step, h0, (jnp.moveaxis(states, 1, 0), jnp.moveaxis(chunk_decay, 1, 0)))
    h_in = jnp.moveaxis(h_in, 0, 1)
    y_off = jnp.einsum('bcihn,bchpn->bcihp', cm, h_in) * jnp.exp(acum)[..., None]
    return (y_diag + y_off).reshape(bsz, length, nh, hp), h_fin


def _ssd_mixer(pc, pl_, conv_w, conv_b, a_log, dt_bias, d_skip, norm_g, ctx_out):
    f32 = jnp.float32
    dtype = pl_.dtype
    rep = SSD_HEADS // SSD_GROUPS

    def prep(p):
        bsz, length = p.shape[:2]
        z, xbc, dt_raw = _split(p, (SSD_WIDTH, SSD_CONV_CH, 2 * SSD_HEADS))
        xbc = jax.nn.silu(_dwconv_centered(xbc, conv_w, conv_b)).astype(f32)
        xs, bm, cm = _split(xbc, (SSD_WIDTH, SSD_GROUPS * SSD_STATE, SSD_GROUPS * SSD_STATE))
        xs = xs.reshape(bsz, length, SSD_HEADS, SSD_HEAD_DIM)
        bm = jnp.repeat(bm.reshape(bsz, length, SSD_GROUPS, SSD_STATE), rep, axis=2)
        cm = jnp.repeat(cm.reshape(bsz, length, SSD_GROUPS, SSD_STATE), rep, axis=2)
        dt_raw = dt_raw.astype(f32).reshape(bsz, length, 2, SSD_HEADS)
        return z, (xs, bm, cm, dt_raw)

    zc, seq_c = prep(pc)
    zl, seq_l = prep(pl_)

    def make_scan(d):
        a_head = -jnp.exp(a_log[d].astype(f32))
        bias = dt_bias[d].astype(f32)

        def scan(seq, h0):
            xs, bm, cm, dt_raw = seq
            dt = jax.nn.softplus(dt_raw[:, :, d] + bias)
            return _ssd_chunk_scan(xs, dt, a_head, bm, cm, h0)
        return scan

    h0 = jnp.zeros((pl_.shape[0], SSD_HEADS, SSD_HEAD_DIM, SSD_STATE), f32)
    yc, yl = _bidirectional_with_prefix(make_scan(0), make_scan(1), seq_c, seq_l, h0)

    def finish(y, xs, z):
        y = y + d_skip.astype(f32)[:, None] * xs
        y = y.reshape(y.shape[0], y.shape[1], SSD_WIDTH).astype(dtype) * jax.nn.silu(z)
        return _rmsnorm(y, norm_g)

    out_c = finish(yc, seq_c[0], zc) if ctx_out else None
    return out_c, finish(yl, seq_l[0], zl)


def _gla_chunk_scan(q, k, v, logg, s0):
    bsz, length, nh, _ = q.shape
    dv = v.shape[-1]
    nc = length // GLA_CHUNK
    rs = lambda t: t.reshape(bsz, nc, GLA_CHUNK, nh, t.shape[-1])
    q, k, v, logg = rs(q), rs(k), rs(v), rs(logg)
    b = jnp.cumsum(logg, axis=2)
    qe = q * jnp.exp(b)
    ke = k * jnp.exp(-b)
    lower = jnp.tril(jnp.ones((GLA_CHUNK, GLA_CHUNK), bool))
    att = jnp.where(lower, jnp.einsum('bcihd,bcjhd->bchij', qe, ke), 0.0)
    o_intra = jnp.einsum('bchij,bcjhv->bcihv', att, v)
    b_last = b[:, :, -1]
    kd = k * jnp.exp(b_last[:, :, None] - b)
    local = jnp.einsum('bcjhd,bcjhv->bchdv', kd, v)

    def step(s, inp):
        g_c, u_c = inp
        return g_c[..., None] * s + u_c, s

    s_fin, s_in = lax.scan(step, s0, (jnp.moveaxis(jnp.exp(b_last), 1, 0), jnp.moveaxis(local, 1, 0)))
    s_in = jnp.moveaxis(s_in, 0, 1)
    o_inter = jnp.einsum('bcihd,bchdv->bcihv', qe, s_in)
    return (o_intra + o_inter).reshape(bsz, length, nh, dv), s_fin


def _gla_mixer(pc, pl_, gate_w, gate_b, norm_g, ctx_out):
    f32 = jnp.float32
    dtype = pl_.dtype

    def prep(p):
        bsz, length = p.shape[:2]
        q, k, v, r, glr = _split(p, (GLA_QK, GLA_QK, GLA_WIDTH, GLA_WIDTH, 2 * GLA_GATE_RANK))
        hd = lambda t, dim: t.astype(f32).reshape(bsz, length, GLA_HEADS, dim)
        seq = (hd(q, GLA_DK) * GLA_DK ** -0.5, hd(k, GLA_DK), hd(v, GLA_DV),
               glr.astype(f32).reshape(bsz, length, 2, GLA_GATE_RANK))
        return r, seq

    rc, seq_c = prep(pc)
    rl, seq_l = prep(pl_)

    def make_scan(d):
        w = gate_w[d].astype(f32)
        bias = gate_b[d].astype(f32)

        def scan(seq, s0):
            q, k, v, glr = seq
            logg = (jax.nn.log_sigmoid(glr[:, :, d] @ w + bias) / GLA_TAU).reshape(q.shape)
            return _gla_chunk_scan(q, k, v, logg, s0)
        return scan

    s0 = jnp.zeros((pl_.shape[0], GLA_HEADS, GLA_DK, GLA_DV), f32)
    oc, ol = _bidirectional_with_prefix(make_scan(0), make_scan(1), seq_c, seq_l, s0)

    def finish(o, r):
        o = _rmsnorm(o.astype(dtype), norm_g)
        return o.reshape(o.shape[0], o.shape[1], GLA_WIDTH) * jax.nn.silu(r)

    out_c = finish(oc, rc) if ctx_out else None
    return out_c, finish(ol, rl)


def _axial_rope(length, dtype):
    rows = length // GRID_W
    row = jnp.repeat(jnp.arange(rows, dtype=jnp.float32), GRID_W)
    col = jnp.tile(jnp.arange(GRID_W, dtype=jnp.float32), rows)
    half = MLA_ROPE // 2
    inv = ROPE_BASE ** (-jnp.arange(0, half, 2, dtype=jnp.float32) / half)
    ang = jnp.concatenate([row[:, None] * inv, col[:, None] * inv], axis=-1)
    return jnp.cos(ang).astype(dtype), jnp.sin(ang).astype(dtype)


def _apply_rope(x, cos, sin):
    h = x.shape[-1] // 2
    x1, x2 = x[..., :h], x[..., h:]
    return jnp.concatenate([x1 * cos - x2 * sin, x1 * sin + x2 * cos], axis=-1)


def _attention(q, k, v):
    s = jnp.einsum('bqhd,bkhd->bhqk', q, k).astype(jnp.float32) * MLA_SCALE
    p = jax.nn.softmax(s, axis=-1).astype(v.dtype)
    return jnp.einsum('bhqk,bkhv->bqhv', p, v)


def _block_attention(q, k, v):
    bsz, length = q.shape[:2]
    nb = length // MLA_QBLOCK
    qb = jnp.moveaxis(q.reshape(bsz, nb, MLA_QBLOCK, q.shape[2], q.shape[3]), 1, 0)
    out = lax.map(lambda blk: _attention(blk, k, v), qb)
    return jnp.moveaxis(out, 0, 1).reshape(bsz, length, out.shape[3], out.shape[4])


def _mla_mixer(pc, pl_, q_norm_g, w_uq, kv_norm_g, w_ukv, ctx_out):
    def project(p):
        bsz, length = p.shape[:2]
        cq, ckv, kr = _split(p, (MLA_Q_RANK, MLA_KV_RANK, MLA_ROPE))
        q = (_rmsnorm(cq, q_norm_g) @ w_uq).reshape(bsz, length, MLA_HEADS, MLA_NOPE + MLA_ROPE)
        kv = (_rmsnorm(ckv, kv_norm_g) @ w_ukv).reshape(bsz, length, MLA_HEADS, MLA_NOPE + MLA_V)
        return q, kv[..., :MLA_NOPE], kr, kv[..., MLA_NOPE:]

    def full_keys(k_nope, k_rope):
        shared = jnp.broadcast_to(k_rope[:, :, None, :], k_nope.shape[:3] + (MLA_ROPE,))
        return jnp.concatenate([k_nope, shared], axis=-1)

    qc, knc, krc, vc = project(pc)
    ql, knl, krl, vl = project(pl_)
    cos, sin = _axial_rope(pl_.shape[1], pl_.dtype)
    ql = jnp.concatenate([ql[..., :MLA_NOPE],
                          _apply_rope(ql[..., MLA_NOPE:], cos[:, None, :], sin[:, None, :])], axis=-1)
    krl = _apply_rope(krl, cos, sin)
    kc = full_keys(knc, krc)
    kl = full_keys(knl, krl)
    k_all = jnp.concatenate([kl, kc], axis=1)
    v_all = jnp.concatenate([vl, vc], axis=1)
    yl = _block_attention(ql, k_all, v_all)
    yl = yl.reshape(yl.shape[0], yl.shape[1], MLA_WIDTH)
    if ctx_out:
        yc = _attention(qc, kc, vc)
        yc = yc.reshape(yc.shape[0], yc.shape[1], MLA_WIDTH)
    else:
        yc = None
    return yc, yl


def _linear_recurrence_op(e1, e2):
    a1, b1 = e1
    a2, b2 = e2
    return a1 * a2, a2 * b1 + b2


def _s5_mixer(pc, pl_, a_re, a_im, log_dt, b_re, b_im, c_re, c_im, d_skip, glu_w, glu_b, ctx_out):
    f32 = jnp.float32
    dtype = pl_.dtype
    bmat = b_re.astype(f32) + 1j * b_im.astype(f32)
    group = lambda p: p.astype(f32).reshape(p.shape[0], p.shape[1], S5_NGROUPS, S5_GROUP)
    uc, ul = group(pc), group(pl_)

    def make_scan(d):
        lam = jnp.minimum(a_re[d].astype(f32), S5_MAX_RE) + 1j * a_im[d].astype(f32)
        step = jnp.exp(log_dt[d].astype(f32))[:, None]
        abar = jnp.exp(lam * step)
        bbar = ((abar - 1.0) / lam)[:, :, None] * bmat
        cmat = c_re[d].astype(f32) + 1j * c_im[d].astype(f32)

        def scan(seq, h0):
            (u,) = seq
            bu = jnp.einsum('blgc,gnc->blgn', u.astype(jnp.complex64), bbar)
            bu = bu.at[:, 0].add(abar * h0)
            _, h = lax.associative_scan(_linear_recurrence_op, (jnp.broadcast_to(abar, bu.shape), bu), axis=1)
            return jnp.einsum('blgn,gcn->blgc', h, cmat).real, h[:, -1]
        return scan

    h0 = jnp.zeros((pl_.shape[0], S5_NGROUPS, S5_STATE), jnp.complex64)
    yc, yl = _bidirectional_with_prefix(make_scan(0), make_scan(1), (uc,), (ul,), h0)
    dmat = d_skip.astype(f32).reshape(S5_NGROUPS, S5_GROUP)

    def finish(y, u):
        y = (y + dmat * u).reshape(u.shape[0], u.shape[1], S5_WIDTH).astype(dtype)
        y = jax.nn.gelu(y, approximate=False)
        return y * jax.nn.sigmoid(y @ glu_w + glu_b)

    out_c = finish(yc, uc) if ctx_out else None
    return out_c, finish(yl, ul)


def _peer(xn, w_q, sub_keys, u_tab, v_tab):
    shape = xn.shape
    xt = xn.reshape(-1, shape[-1])
    n_tok = xt.shape[0]
    q = (xt @ w_q).reshape(n_tok, PEER_HEADS, 2, PEER_DQ // 2)
    s1 = jnp.einsum('thd,hnd->thn', q[:, :, 0], sub_keys[0]).astype(jnp.float32)
    s2 = jnp.einsum('thd,hnd->thn', q[:, :, 1], sub_keys[1]).astype(jnp.float32)
    v1, i1 = lax.top_k(s1, PEER_TOPK)
    v2, i2 = lax.top_k(s2, PEER_TOPK)
    cand = (v1[..., :, None] + v2[..., None, :]).reshape(n_tok, PEER_HEADS, PEER_TOPK * PEER_TOPK)
    cidx = (i1[..., :, None] * PEER_KEYS + i2[..., None, :]).reshape(n_tok, PEER_HEADS, PEER_TOPK * PEER_TOPK)
    best, pos = lax.top_k(cand, PEER_TOPK)
    experts = jnp.take_along_axis(cidx, pos, axis=-1).reshape(n_tok, PEER_HEADS * PEER_TOPK)
    gates = jax.nn.softmax(best, axis=-1).reshape(n_tok, PEER_HEADS * PEER_TOPK).astype(xn.dtype)
    nb = n_tok // PEER_BLOCK

    def block(args):
        xb, eb, gb = args
        hid = jnp.einsum('td,tkd->tk', xb, u_tab[eb])
        return jnp.einsum('tk,tkd->td', gb * jax.nn.gelu(hid, approximate=False), v_tab[eb])

    out = lax.map(block, (xt.reshape(nb, PEER_BLOCK, shape[-1]),
                          experts.reshape(nb, PEER_BLOCK, -1),
                          gates.reshape(nb, PEER_BLOCK, -1)))
    return out.reshape(shape)


def _final_norm_kernel(x_ref, g_ref, o_ref):
    x = x_ref[...]
    ms = jnp.mean(x * x, axis=-1, keepdims=True)
    o_ref[...] = x * lax.rsqrt(ms + NORM_EPS) * g_ref[...]


def _final_norm(h, g):
    n = h.shape[0] * h.shape[1]
    x2 = h.reshape(n, D_MODEL)
    tm = 512
    out = pl.pallas_call(
        _final_norm_kernel,
        grid=(n // tm,),
        in_specs=[pl.BlockSpec((tm, D_MODEL), lambda i: (i, 0)),
                  pl.BlockSpec((1, D_MODEL), lambda i: (0, 0))],
        out_specs=pl.BlockSpec((tm, D_MODEL), lambda i: (i, 0)),
        out_shape=jax.ShapeDtypeStruct((n, D_MODEL), jnp.float32),
        name="final_norm",
    )(x2, g.reshape(1, D_MODEL))
    return out.reshape(h.shape)


def kernel(x, c, ctx, c_ctx, ada_w, ada_b, norm_mix_g, norm_ffn_g, w_in, w_out,
           ssd_conv_w, ssd_conv_b, ssd_a_log, ssd_dt_bias, ssd_d, ssd_norm_g,
           gla_gate_w, gla_gate_b, gla_norm_g, mla_q_norm_g, mla_w_uq, mla_kv_norm_g, mla_w_ukv,
           s5_a_re, s5_a_im, s5_log_dt, s5_b_re, s5_b_im, s5_c_re, s5_c_im, s5_d, s5_glu_w, s5_glu_b,
           peer_w_q, peer_sub_keys, peer_u, peer_v, final_norm_g):
    h_lat, h_ctx = x, ctx
    cond_lat = jax.nn.silu(c)[:, None, :]
    cond_ctx = jax.nn.silu(c_ctx)[None, None, :]
    splits = (SSD_IN, GLA_IN, MLA_IN, S5_IN)
    for i in range(DEPTH):
        ctx_out = i < DEPTH - 1
        mod_l = jnp.split(cond_lat @ ada_w[i] + ada_b[i], N_MOD, axis=-1)
        mod_c = jnp.split(cond_ctx @ ada_w[i] + ada_b[i], N_MOD, axis=-1)
        p_l = _modulate(h_lat, norm_mix_g[i], mod_l[0], mod_l[1]) @ w_in[i]
        p_c = _modulate(h_ctx, norm_mix_g[i], mod_c[0], mod_c[1]) @ w_in[i]
        ssd_pl, gla_pl, mla_pl, s5_pl = _split(p_l, splits)
        ssd_pc, gla_pc, mla_pc, s5_pc = _split(p_c, splits)
        ssd_c, ssd_l = _ssd_mixer(ssd_pc, ssd_pl, ssd_conv_w[i], ssd_conv_b[i], ssd_a_log[i],
                                  ssd_dt_bias[i], ssd_d[i], ssd_norm_g[i], ctx_out)
        gla_c, gla_l = _gla_mixer(gla_pc, gla_pl, gla_gate_w[i], gla_gate_b[i], gla_norm_g[i], ctx_out)
        mla_c, mla_l = _mla_mixer(mla_pc, mla_pl, mla_q_norm_g[i], mla_w_uq[i], mla_kv_norm_g[i],
                                  mla_w_ukv[i], ctx_out)
        s5_c, s5_l = _s5_mixer(s5_pc, s5_pl, s5_a_re[i], s5_a_im[i], s5_log_dt[i], s5_b_re[i], s5_b_im[i],
                               s5_c_re[i], s5_c_im[i], s5_d[i], s5_glu_w[i], s5_glu_b[i], ctx_out)
        mix_l = jnp.concatenate([ssd_l, gla_l, mla_l, s5_l], axis=-1) @ w_out[i]
        h_lat = h_lat + mod_l[2] * mix_l
        f_l = _modulate(h_lat, norm_ffn_g[i], mod_l[3], mod_l[4])
        h_lat = h_lat + mod_l[5] * _peer(f_l, peer_w_q[i], peer_sub_keys[i], peer_u[i], peer_v[i])
        if ctx_out:
            mix_c = jnp.concatenate([ssd_c, gla_c, mla_c, s5_c], axis=-1) @ w_out[i]
            h_ctx = h_ctx + mod_c[2] * mix_c
            f_c = _modulate(h_ctx, norm_ffn_g[i], mod_c[3], mod_c[4])
            h_ctx = h_ctx + mod_c[5] * _peer(f_c, peer_w_q[i], peer_sub_keys[i], peer_u[i], peer_v[i])
    return _final_norm(h_lat, final_norm_g)
```

```python
import math
import functools
import jax
import jax.numpy as jnp
from jax import lax
import numpy as np
from jax.experimental import pallas as pl
from jax.experimental.pallas import tpu as pltpu

D_MODEL = 1024
BATCH = 2
SEQ = 8192
DEPTH = 2

GRID_W = 64
CTX_LEN = 256
NORM_EPS = 1e-6
N_MOD = 6

D_MIX = D_MODEL
N_MIXERS = 4
GROUP_WIDTH = D_MIX // N_MIXERS

SSD_WIDTH = GROUP_WIDTH
SSD_HEAD_DIM = 64
SSD_HEADS = SSD_WIDTH // SSD_HEAD_DIM
SSD_GROUPS = 2
SSD_STATE = 128
SSD_CONV = 5
SSD_CHUNK = 128
SSD_CONV_CH = SSD_WIDTH + 2 * SSD_GROUPS * SSD_STATE
SSD_IN = SSD_WIDTH + SSD_CONV_CH + 2 * SSD_HEADS

GLA_WIDTH = GROUP_WIDTH
GLA_HEADS = 4
GLA_DV = GLA_WIDTH // GLA_HEADS
GLA_DK = GLA_DV // 2
GLA_QK = GLA_HEADS * GLA_DK
GLA_GATE_RANK = 16
GLA_TAU = 16.0
GLA_CHUNK = 64
GLA_IN = 2 * GLA_QK + 2 * GLA_WIDTH + 2 * GLA_GATE_RANK

MLA_WIDTH = GROUP_WIDTH
MLA_HEADS = 4
MLA_V = MLA_WIDTH // MLA_HEADS
MLA_NOPE = 64
MLA_ROPE = 32
MLA_Q_RANK = 256
MLA_KV_RANK = 128
MLA_QBLOCK = 128
MLA_SCALE = (MLA_NOPE + MLA_ROPE) ** -0.5
ROPE_BASE = 10000.0
MLA_IN = MLA_Q_RANK + MLA_KV_RANK + MLA_ROPE

S5_WIDTH = GROUP_WIDTH
S5_GROUP = 16
S5_NGROUPS = S5_WIDTH // S5_GROUP
S5_STATE = 64
S5_MAX_RE = -1e-4
S5_IN = S5_WIDTH

D_IN = SSD_IN + GLA_IN + MLA_IN + S5_IN

PEER_KEYS = 128
PEER_EXPERTS = PEER_KEYS * PEER_KEYS
PEER_HEADS = 8
PEER_TOPK = 16
PEER_DQ = 128
PEER_BLOCK = 128


def _rmsnorm(x, g):
    xf = x.astype(jnp.float32)
    y = xf * lax.rsqrt(jnp.mean(xf * xf, axis=-1, keepdims=True) + NORM_EPS)
    return y.astype(x.dtype) * g


def _modulate(x, g, shift, scale):
    return _rmsnorm(x, g) * (1 + scale) + shift


def _split(t, sizes):
    out, start = [], 0
    for s in sizes:
        out.append(t[..., start:start + s])
        start += s
    return out


def _dwconv_centered(x, w, b):
    k = w.shape[0]
    left = k // 2
    y = lax.conv_general_dilated(x, w[:, None, :].astype(x.dtype), window_strides=(1,),
                                 padding=[(left, k - 1 - left)],
                                 dimension_numbers=('NWC', 'WIO', 'NWC'),
                                 feature_group_count=x.shape[-1])
    return y + b


def _bidirectional_with_prefix(scan_f, scan_b, ctx_seq, lat_seq, h0):
    flip = lambda seq: tuple(jnp.flip(a, axis=1) for a in seq)
    yc_f, hc_f = scan_f(ctx_seq, h0)
    yc_b, hc_b = scan_b(flip(ctx_seq), h0)
    yl_f, _ = scan_f(lat_seq, hc_f)
    yl_b, _ = scan_b(flip(lat_seq), hc_b)
    return yc_f + jnp.flip(yc_b, axis=1), yl_f + jnp.flip(yl_b, axis=1)


def _ssd_chunk_scan(x, dt, a_head, bm, cm, h0):
    bsz, length, nh, hp = x.shape
    ns = bm.shape[-1]
    nc = length // SSD_CHUNK
    a = (dt * a_head).reshape(bsz, nc, SSD_CHUNK, nh)
    xd = (x * dt[..., None]).reshape(bsz, nc, SSD_CHUNK, nh, hp)
    bm = bm.reshape(bsz, nc, SSD_CHUNK, nh, ns)
    cm = cm.reshape(bsz, nc, SSD_CHUNK, nh, ns)
    acum = jnp.cumsum(a, axis=2)
    lower = jnp.tril(jnp.ones((SSD_CHUNK, SSD_CHUNK), bool))
    seg = acum[:, :, :, None, :] - acum[:, :, None, :, :]
    decay = jnp.exp(jnp.where(lower[None, None, :, :, None], seg, -jnp.inf))
    scores = jnp.einsum('bcihn,bcjhn->bcijh', cm, bm) * decay
    y_diag = jnp.einsum('bcijh,bcjhp->bcihp', scores, xd)
    to_end = jnp.exp(acum[:, :, -1:, :] - acum)
    states = jnp.einsum('bcjh,bcjhn,bcjhp->bchpn', to_end, bm, xd)
    chunk_decay = jnp.exp(acum[:, :, -1, :])

    def step(h, inp):
        s_c, g_c = inp
        return g_c[:, :, None, None] * h + s_c, h

    h_fin, h_in = lax.scan(step, h0, (jnp.moveaxis(states, 1, 0), jnp.moveaxis(chunk_decay, 1, 0)))
    h_in = jnp.moveaxis(h_in, 0, 1)
    y_off = jnp.einsum('bcihn,bchpn->bcihp', cm, h_in) * jnp.exp(acum)[..., None]
    return (y_diag + y_off).reshape(bsz, length, nh, hp), h_fin


def _ssd_mixer(pc, pl_, conv_w, conv_b, a_log, dt_bias, d_skip, norm_g, ctx_out):
    f32 = jnp.float32
    dtype = pl_.dtype
    rep = SSD_HEADS // SSD_GROUPS

    def prep(p):
        bsz, length = p.shape[:2]
        z, xbc, dt_raw = _split(p, (SSD_WIDTH, SSD_CONV_CH, 2 * SSD_HEADS))
        xbc = jax.nn.silu(_dwconv_centered(xbc, conv_w, conv_b)).astype(f32)
        xs, bm, cm = _split(xbc, (SSD_WIDTH, SSD_GROUPS * SSD_STATE, SSD_GROUPS * SSD_STATE))
        xs = xs.reshape(bsz, length, SSD_HEADS, SSD_HEAD_DIM)
        bm = jnp.repeat(bm.reshape(bsz, length, SSD_GROUPS, SSD_STATE), rep, axis=2)
        cm = jnp.repeat(cm.reshape(bsz, length, SSD_GROUPS, SSD_STATE), rep, axis=2)
        dt_raw = dt_raw.astype(f32).reshape(bsz, length, 2, SSD_HEADS)
        return z, (xs, bm, cm, dt_raw)

    zc, seq_c = prep(pc)
    zl, seq_l = prep(pl_)

    def make_scan(d):
        a_head = -jnp.exp(a_log[d].astype(f32))
        bias = dt_bias[d].astype(f32)

        def scan(seq, h0):
            xs, bm, cm, dt_raw = seq
            dt = jax.nn.softplus(dt_raw[:, :, d] + bias)
            return _ssd_chunk_scan(xs, dt, a_head, bm, cm, h0)
        return scan

    h0 = jnp.zeros((pl_.shape[0], SSD_HEADS, SSD_HEAD_DIM, SSD_STATE), f32)
    yc, yl = _bidirectional_with_prefix(make_scan(0), make_scan(1), seq_c, seq_l, h0)

    def finish(y, xs, z):
        y = y + d_skip.astype(f32)[:, None] * xs
        y = y.reshape(y.shape[0], y.shape[1], SSD_WIDTH).astype(dtype) * jax.nn.silu(z)
        return _rmsnorm(y, norm_g)

    out_c = finish(yc, seq_c[0], zc) if ctx_out else None
    return out_c, finish(yl, seq_l[0], zl)


def _gla_chunk_scan(q, k, v, logg, s0):
    bsz, length, nh, _ = q.shape
    dv = v.shape[-1]
    nc = length // GLA_CHUNK
    rs = lambda t: t.reshape(bsz, nc, GLA_CHUNK, nh, t.shape[-1])
    q, k, v, logg = rs(q), rs(k), rs(v), rs(logg)
    b = jnp.cumsum(logg, axis=2)
    qe = q * jnp.exp(b)
    ke = k * jnp.exp(-b)
    lower = jnp.tril(jnp.ones((GLA_CHUNK, GLA_CHUNK), bool))
    att = jnp.where(lower, jnp.einsum('bcihd,bcjhd->bchij', qe, ke), 0.0)
    o_intra = jnp.einsum('bchij,bcjhv->bcihv', att, v)
    b_last = b[:, :, -1]
    kd = k * jnp.exp(b_last[:, :, None] - b)
    local = jnp.einsum('bcjhd,bcjhv->bchdv', kd, v)

    def step(s, inp):
        g_c, u_c = inp
        return g_c[..., None] * s + u_c, s

    s_fin, s_in = lax.scan(step, s0, (jnp.moveaxis(jnp.exp(b_last), 1, 0), jnp.moveaxis(local, 1, 0)))
    s_in = jnp.moveaxis(s_in, 0, 1)
    o_inter = jnp.einsum('bcihd,bchdv->bcihv', qe, s_in)
    return (o_intra + o_inter).reshape(bsz, length, nh, dv), s_fin


def _gla_mixer(pc, pl_, gate_w, gate_b, norm_g, ctx_out):
    f32 = jnp.float32
    dtype = pl_.dtype

    def prep(p):
        bsz, length = p.shape[:2]
        q, k, v, r, glr = _split(p, (GLA_QK, GLA_QK, GLA_WIDTH, GLA_WIDTH, 2 * GLA_GATE_RANK))
        hd = lambda t, dim: t.astype(f32).reshape(bsz, length, GLA_HEADS, dim)
        seq = (hd(q, GLA_DK) * GLA_DK ** -0.5, hd(k, GLA_DK), hd(v, GLA_DV),
               glr.astype(f32).reshape(bsz, length, 2, GLA_GATE_RANK))
        return r, seq

    rc, seq_c = prep(pc)
    rl, seq_l = prep(pl_)

    def make_scan(d):
        w = gate_w[d].astype(f32)
        bias = gate_b[d].astype(f32)

        def scan(seq, s0):
            q, k, v, glr = seq
            logg = (jax.nn.log_sigmoid(glr[:, :, d] @ w + bias) / GLA_TAU).reshape(q.shape)
            return _gla_chunk_scan(q, k, v, logg, s0)
        return scan

    s0 = jnp.zeros((pl_.shape[0], GLA_HEADS, GLA_DK, GLA_DV), f32)
    oc, ol = _bidirectional_with_prefix(make_scan(0), make_scan(1), seq_c, seq_l, s0)

    def finish(o, r):
        o = _rmsnorm(o.astype(dtype), norm_g)
        return o.reshape(o.shape[0], o.shape[1], GLA_WIDTH) * jax.nn.silu(r)

    out_c = finish(oc, rc) if ctx_out else None
    return out_c, finish(ol, rl)


def _axial_rope(length, dtype):
    rows = length // GRID_W
    row = jnp.repeat(jnp.arange(rows, dtype=jnp.float32), GRID_W)
    col = jnp.tile(jnp.arange(GRID_W, dtype=jnp.float32), rows)
    half = MLA_ROPE // 2
    inv = ROPE_BASE ** (-jnp.arange(0, half, 2, dtype=jnp.float32) / half)
    ang = jnp.concatenate([row[:, None] * inv, col[:, None] * inv], axis=-1)
    return jnp.cos(ang).astype(dtype), jnp.sin(ang).astype(dtype)


def _apply_rope(x, cos, sin):
    h = x.shape[-1] // 2
    x1, x2 = x[..., :h], x[..., h:]
    return jnp.concatenate([x1 * cos - x2 * sin, x1 * sin + x2 * cos], axis=-1)


def _attention(q, k, v):
    s = jnp.einsum('bqhd,bkhd->bhqk', q, k).astype(jnp.float32) * MLA_SCALE
    p = jax.nn.softmax(s, axis=-1).astype(v.dtype)
    return jnp.einsum('bhqk,bkhv->bqhv', p, v)


def _block_attention(q, k, v):
    bsz, length = q.shape[:2]
    nb = length // MLA_QBLOCK
    qb = jnp.moveaxis(q.reshape(bsz, nb, MLA_QBLOCK, q.shape[2], q.shape[3]), 1, 0)
    out = lax.map(lambda blk: _attention(blk, k, v), qb)
    return jnp.moveaxis(out, 0, 1).reshape(bsz, length, out.shape[3], out.shape[4])


def _mla_mixer(pc, pl_, q_norm_g, w_uq, kv_norm_g, w_ukv, ctx_out):
    def project(p):
        bsz, length = p.shape[:2]
        cq, ckv, kr = _split(p, (MLA_Q_RANK, MLA_KV_RANK, MLA_ROPE))
        q = (_rmsnorm(cq, q_norm_g) @ w_uq).reshape(bsz, length, MLA_HEADS, MLA_NOPE + MLA_ROPE)
        kv = (_rmsnorm(ckv, kv_norm_g) @ w_ukv).reshape(bsz, length, MLA_HEADS, MLA_NOPE + MLA_V)
        return q, kv[..., :MLA_NOPE], kr, kv[..., MLA_NOPE:]

    def full_keys(k_nope, k_rope):
        shared = jnp.broadcast_to(k_rope[:, :, None, :], k_nope.shape[:3] + (MLA_ROPE,))
        return jnp.concatenate([k_nope, shared], axis=-1)

    qc, knc, krc, vc = project(pc)
    ql, knl, krl, vl = project(pl_)
    cos, sin = _axial_rope(pl_.shape[1], pl_.dtype)
    ql = jnp.concatenate([ql[..., :MLA_NOPE],
                          _apply_rope(ql[..., MLA_NOPE:], cos[:, None, :], sin[:, None, :])], axis=-1)
    krl = _apply_rope(krl, cos, sin)
    kc = full_keys(knc, krc)
    kl = full_keys(knl, krl)
    k_all = jnp.concatenate([kl, kc], axis=1)
    v_all = jnp.concatenate([vl, vc], axis=1)
    yl = _block_attention(ql, k_all, v_all)
    yl = yl.reshape(yl.shape[0], yl.shape[1], MLA_WIDTH)
    if ctx_out:
        yc = _attention(qc, kc, vc)
        yc = yc.reshape(yc.shape[0], yc.shape[1], MLA_WIDTH)
    else:
        yc = None
    return yc, yl


def _linear_recurrence_op(e1, e2):
    a1, b1 = e1
    a2, b2 = e2
    return a1 * a2, a2 * b1 + b2


def _s5_mixer(pc, pl_, a_re, a_im, log_dt, b_re, b_im, c_re, c_im, d_skip, glu_w, glu_b, ctx_out):
    f32 = jnp.float32
    dtype = pl_.dtype
    bmat = b_re.astype(f32) + 1j * b_im.astype(f32)
    group = lambda p: p.astype(f32).reshape(p.shape[0], p.shape[1], S5_NGROUPS, S5_GROUP)
    uc, ul = group(pc), group(pl_)

    def make_scan(d):
        lam = jnp.minimum(a_re[d].astype(f32), S5_MAX_RE) + 1j * a_im[d].astype(f32)
        step = jnp.exp(log_dt[d].astype(f32))[:, None]
        abar = jnp.exp(lam * step)
        bbar = ((abar - 1.0) / lam)[:, :, None] * bmat
        cmat = c_re[d].astype(f32) + 1j * c_im[d].astype(f32)

        def scan(seq, h0):
            (u,) = seq
            bu = jnp.einsum('blgc,gnc->blgn', u.astype(jnp.complex64), bbar)
            bu = bu.at[:, 0].add(abar * h0)
            _, h = lax.associative_scan(_linear_recurrence_op, (jnp.broadcast_to(abar, bu.shape), bu), axis=1)
            return jnp.einsum('blgn,gcn->blgc', h, cmat).real, h[:, -1]
        return scan

    h0 = jnp.zeros((pl_.shape[0], S5_NGROUPS, S5_STATE), jnp.complex64)
    yc, yl = _bidirectional_with_prefix(make_scan(0), make_scan(1), (uc,), (ul,), h0)
    dmat = d_skip.astype(f32).reshape(S5_NGROUPS, S5_GROUP)

    def finish(y, u):
        y = (y + dmat * u).reshape(u.shape[0], u.shape[1], S5_WIDTH).astype(dtype)
        y = jax.nn.gelu(y, approximate=False)
        return y * jax.nn.sigmoid(y @ glu_w + glu_b)

    out_c = finish(yc, uc) if ctx_out else None
    return out_c, finish(yl, ul)


def _peer(xn, w_q, sub_keys, u_tab, v_tab):
    shape = xn.shape
    xt = xn.reshape(-1, shape[-1])
    n_tok = xt.shape[0]
    q = (xt @ w_q).reshape(n_tok, PEER_HEADS, 2, PEER_DQ // 2)
    s1 = jnp.einsum('thd,hnd->thn', q[:, :, 0], sub_keys[0]).astype(jnp.float32)
    s2 = jnp.einsum('thd,hnd->thn', q[:, :, 1], sub_keys[1]).astype(jnp.float32)
    v1, i1 = lax.top_k(s1, PEER_TOPK)
    v2, i2 = lax.top_k(s2, PEER_TOPK)
    cand = (v1[..., :, None] + v2[..., None, :]).reshape(n_tok, PEER_HEADS, PEER_TOPK * PEER_TOPK)
    cidx = (i1[..., :, None] * PEER_KEYS + i2[..., None, :]).reshape(n_tok, PEER_HEADS, PEER_TOPK * PEER_TOPK)
    best, pos = lax.top_k(cand, PEER_TOPK)
    experts = jnp.take_along_axis(cidx, pos, axis=-1).reshape(n_tok, PEER_HEADS * PEER_TOPK)
    gates = jax.nn.softmax(best, axis=-1).reshape(n_tok, PEER_HEADS * PEER_TOPK).astype(xn.dtype)
    nb = n_tok // PEER_BLOCK

    def block(args):
        xb, eb, gb = args
        hid = jnp.einsum('td,tkd->tk', xb, u_tab[eb])
        return jnp.einsum('tk,tkd->td', gb * jax.nn.gelu(hid, approximate=False), v_tab[eb])

    out = lax.map(block, (xt.reshape(nb, PEER_BLOCK, shape[-1]),
                          experts.reshape(nb, PEER_BLOCK, -1),
                          gates.reshape(nb, PEER_BLOCK, -1)))
    return out.reshape(shape)


PEER_ROUTE_TOKENS = 256
PEER_GATE_TOKENS = 256
PEER_DENSE_TOKENS = 1024
PEER_DENSE_EXPERTS = 512
PEER_SLOTS = PEER_HEADS * PEER_TOPK
_NT_DIMS = (((1,), (1,)), ((), ()))


def _split_bf16(x):
    hi = x.astype(jnp.bfloat16)
    lo = (x - hi.astype(jnp.float32)).astype(jnp.bfloat16)
    return hi, lo


def _dot3(a_hi, a_lo, b_hi, b_lo, dims):
    f = lambda a, b: lax.dot_general(a, b, dims, preferred_element_type=jnp.float32)
    return f(a_hi, b_hi) + f(a_hi, b_lo) + f(a_lo, b_hi)


def _gelu_erf(x):
    return 0.5 * x * (1.0 + lax.erf(x * (2.0 ** -0.5)))


def _modulated_norm(x, g, shift, scale):
    ms = jnp.mean(x * x, axis=-1, keepdims=True)
    return x * lax.rsqrt(ms + NORM_EPS) * g * (1.0 + scale) + shift


def _topk_rows(s, k):
    n_rows = s.shape[0]
    rows = lax.broadcasted_iota(jnp.int32, s.shape, 0)
    vals, idxs = [], []
    for _ in range(k):
        m = jnp.max(s, axis=0, keepdims=True)
        idx = jnp.min(jnp.where(s == m, rows, n_rows), axis=0, keepdims=True)
        vals.append(m)
        idxs.append(idx)
        s = jnp.where(rows == idx, -jnp.inf, s)
    return jnp.concatenate(vals, axis=0), jnp.concatenate(idxs, axis=0)


def _select_rows(pos, table):
    out = jnp.zeros(pos.shape, table.dtype)
    for r in range(table.shape[0]):
        out = jnp.where(pos == r, table[r:r + 1, :], out)
    return out


def _peer_route_kernel(h_ref, g_ref, shift_ref, scale_ref, wq_hi_ref, wq_lo_ref, k_hi_ref, k_lo_ref,
                       xn_ref, i1_ref, i2_ref, gate_ref, q_scr):
    xn = _modulated_norm(h_ref[...], g_ref[...], shift_ref[0], scale_ref[0])
    xn_ref[...] = xn.astype(jnp.bfloat16)
    x_hi, x_lo = _split_bf16(xn)
    q_scr[...] = _dot3(wq_hi_ref[...], wq_lo_ref[...], x_hi, x_lo, _NT_DIMS)
    half = PEER_DQ // 2

    def head_body(h, carry):
        base = pl.multiple_of(h * PEER_DQ, PEER_DQ)
        tops = []
        for j in range(2):
            qq = q_scr[pl.ds(base + j * half, half), :]
            q_hi, q_lo = _split_bf16(qq)
            mm = (((1,), (0,)), ((), ()))
            s = _dot3(k_hi_ref[j, h], k_lo_ref[j, h], q_hi, q_lo, mm)
            tops.append(_topk_rows(s, PEER_TOPK))
        (v1, i1), (v2, i2) = tops
        cand = jnp.concatenate([v1[a:a + 1, :] + v2 for a in range(PEER_TOPK)], axis=0)
        best, pos = _topk_rows(cand, PEER_TOPK)
        e = jnp.exp(best - best[0:1, :])
        gates = e / jnp.sum(e, axis=0, keepdims=True)
        row0 = pl.multiple_of(h * PEER_TOPK, PEER_TOPK)
        i1_ref[pl.ds(row0, PEER_TOPK), :] = _select_rows(pos >> (PEER_TOPK.bit_length() - 1), i1)
        i2_ref[pl.ds(row0, PEER_TOPK), :] = _select_rows(pos & (PEER_TOPK - 1), i2)
        gate_ref[pl.ds(row0, PEER_TOPK), :] = gates
        return carry

    lax.fori_loop(0, PEER_HEADS, head_body, 0)


def _peer_gate_kernel(i1_ref, i2_ref, gate_ref, g_ref):
    rows = lax.broadcasted_iota(jnp.int32, (PEER_KEYS, PEER_SLOTS), 0)

    def token_body(t, carry):
        a = i1_ref[pl.ds(t, 1), :]
        b = i2_ref[pl.ds(t, 1), :]
        w = gate_ref[pl.ds(t, 1), :]
        lhs = jnp.where(rows == a, w, 0.0).astype(jnp.bfloat16)
        rhs = jnp.where(rows == b, 1.0, 0.0).astype(jnp.bfloat16)
        gt = lax.dot_general(lhs, rhs, _NT_DIMS, preferred_element_type=jnp.float32)
        g_ref[t] = gt.astype(jnp.bfloat16)
        return carry

    lax.fori_loop(0, g_ref.shape[0], token_body, 0)


def _peer_dense_kernel(xn_ref, u_ref, v_ref, gmat_ref, h_ref, mod_ref, o_ref, acc_ref):
    j = pl.program_id(1)

    @pl.when(j == 0)
    def _():
        acc_ref[...] = jnp.zeros_like(acc_ref)

    hid = lax.dot_general(xn_ref[...], u_ref[...], _NT_DIMS, preferred_element_type=jnp.float32)
    y = gmat_ref[...].astype(jnp.float32) * _gelu_erf(hid)
    acc_ref[...] += jnp.dot(y.astype(jnp.bfloat16), v_ref[...], preferred_element_type=jnp.float32)

    @pl.when(j == pl.num_programs(1) - 1)
    def _():
        o_ref[...] = h_ref[...] + mod_ref[0] * acc_ref[...]


def _peer_layer(h, norm_g, shift, scale, gate_mod, wq_t_hi, wq_t_lo, keys_hi, keys_lo, u_bf, v_bf):
    n, d = h.shape
    nb = shift.shape[0]
    rows_per_batch = n // nb
    tr = min(PEER_ROUTE_TOKENS, rows_per_batch)
    f32 = jnp.float32
    full = lambda *shape: pl.BlockSpec(shape, lambda i: (0,) * len(shape))
    per_batch = lambda t: pl.BlockSpec((1, 1, d), lambda i: (i * t // rows_per_batch, 0, 0))
    xn, i1, i2, gate = pl.pallas_call(
        _peer_route_kernel,
        grid=(n // tr,),
        in_specs=[pl.BlockSpec((tr, d), lambda i: (i, 0)), full(1, d), per_batch(tr), per_batch(tr),
                  full(PEER_HEADS * PEER_DQ, d), full(PEER_HEADS * PEER_DQ, d),
                  full(2, PEER_HEADS, PEER_KEYS, PEER_DQ // 2), full(2, PEER_HEADS, PEER_KEYS, PEER_DQ // 2)],
        out_specs=[pl.BlockSpec((tr, d), lambda i: (i, 0))] + [pl.BlockSpec((PEER_SLOTS, tr), lambda i: (0, i))] * 3,
        out_shape=[jax.ShapeDtypeStruct((n, d), jnp.bfloat16),
                   jax.ShapeDtypeStruct((PEER_SLOTS, n), jnp.int32),
                   jax.ShapeDtypeStruct((PEER_SLOTS, n), jnp.int32),
                   jax.ShapeDtypeStruct((PEER_SLOTS, n), f32)],
        scratch_shapes=[pltpu.VMEM((PEER_HEADS * PEER_DQ, tr), f32)],
        compiler_params=pltpu.CompilerParams(dimension_semantics=("arbitrary",)),
        name="peer_route",
    )(h, norm_g.reshape(1, d), shift, scale, wq_t_hi, wq_t_lo, keys_hi, keys_lo)

    tg = min(PEER_GATE_TOKENS, n)
    slot_spec = pl.BlockSpec((tg, PEER_SLOTS), lambda i: (i, 0))
    gmat = pl.pallas_call(
        _peer_gate_kernel,
        grid=(n // tg,),
        in_specs=[slot_spec, slot_spec, slot_spec],
        out_specs=pl.BlockSpec((tg, PEER_KEYS, PEER_KEYS), lambda i: (i, 0, 0)),
        out_shape=jax.ShapeDtypeStruct((n, PEER_KEYS, PEER_KEYS), jnp.bfloat16),
        compiler_params=pltpu.CompilerParams(dimension_semantics=("arbitrary",)),
        name="peer_gate",
    )(i1.T, i2.T, gate.T)
    gmat = gmat.reshape(n, PEER_EXPERTS)

    tm = min(PEER_DENSE_TOKENS, rows_per_batch)
    te = PEER_DENSE_EXPERTS
    return pl.pallas_call(
        _peer_dense_kernel,
        grid=(n // tm, PEER_EXPERTS // te),
        in_specs=[pl.BlockSpec((tm, d), lambda i, j: (i, 0)),
                  pl.BlockSpec((te, d), lambda i, j: (j, 0)),
                  pl.BlockSpec((te, d), lambda i, j: (j, 0)),
                  pl.BlockSpec((tm, te), lambda i, j: (i, j)),
                  pl.BlockSpec((tm, d), lambda i, j: (i, 0)),
                  pl.BlockSpec((1, 1, d), lambda i, j: (i * tm // rows_per_batch, 0, 0))],
        out_specs=pl.BlockSpec((tm, d), lambda i, j: (i, 0)),
        out_shape=jax.ShapeDtypeStruct((n, d), f32),
        scratch_shapes=[pltpu.VMEM((tm, d), f32)],
        compiler_params=pltpu.CompilerParams(dimension_semantics=("arbitrary", "arbitrary")),
        name="peer_dense",
    )(xn, u_bf, v_bf, gmat, h, gate_mod)


def _final_norm_kernel(x_ref, g_ref, o_ref):
    x = x_ref[...]
    ms = jnp.mean(x * x, axis=-1, keepdims=True)
    o_ref[...] = x * lax.rsqrt(ms + NORM_EPS) * g_ref[...]


def _final_norm(h, g):
    n = h.shape[0] * h.shape[1]
    x2 = h.reshape(n, D_MODEL)
    tm = 512
    out = pl.pallas_call(
        _final_norm_kernel,
        grid=(n // tm,),
        in_specs=[pl.BlockSpec((tm, D_MODEL), lambda i: (i, 0)),
                  pl.BlockSpec((1, D_MODEL), lambda i: (0, 0))],
        out_specs=pl.BlockSpec((tm, D_MODEL), lambda i: (i, 0)),
        out_shape=jax.ShapeDtypeStruct((n, D_MODEL), jnp.float32),
        name="final_norm",
    )(x2, g.reshape(1, D_MODEL))
    return out.reshape(h.shape)


def kernel(x, c, ctx, c_ctx, ada_w, ada_b, norm_mix_g, norm_ffn_g, w_in, w_out,
           ssd_conv_w, ssd_conv_b, ssd_a_log, ssd_dt_bias, ssd_d, ssd_norm_g,
           gla_gate_w, gla_gate_b, gla_norm_g, mla_q_norm_g, mla_w_uq, mla_kv_norm_g, mla_w_ukv,
           s5_a_re, s5_a_im, s5_log_dt, s5_b_re, s5_b_im, s5_c_re, s5_c_im, s5_d, s5_glu_w, s5_glu_b,
           peer_w_q, peer_sub_keys, peer_u, peer_v, final_norm_g):
    h_lat, h_ctx = x, ctx
    cond_lat = jax.nn.silu(c)[:, None, :]
    cond_ctx = jax.nn.silu(c_ctx)[None, None, :]
    splits = (SSD_IN, GLA_IN, MLA_IN, S5_IN)
    for i in range(DEPTH):
        ctx_out = i < DEPTH - 1
        mod_l = jnp.split(cond_lat @ ada_w[i] + ada_b[i], N_MOD, axis=-1)
        mod_c = jnp.split(cond_ctx @ ada_w[i] + ada_b[i], N_MOD, axis=-1)
        p_l = _modulate(h_lat, norm_mix_g[i], mod_l[0], mod_l[1]) @ w_in[i]
        p_c = _modulate(h_ctx, norm_mix_g[i], mod_c[0], mod_c[1]) @ w_in[i]
        ssd_pl, gla_pl, mla_pl, s5_pl = _split(p_l, splits)
        ssd_pc, gla_pc, mla_pc, s5_pc = _split(p_c, splits)
        ssd_c, ssd_l = _ssd_mixer(ssd_pc, ssd_pl, ssd_conv_w[i], ssd_conv_b[i], ssd_a_log[i],
                                  ssd_dt_bias[i], ssd_d[i], ssd_norm_g[i], ctx_out)
        gla_c, gla_l = _gla_mixer(gla_pc, gla_pl, gla_gate_w[i], gla_gate_b[i], gla_norm_g[i], ctx_out)
        mla_c, mla_l = _mla_mixer(mla_pc, mla_pl, mla_q_norm_g[i], mla_w_uq[i], mla_kv_norm_g[i],
                                  mla_w_ukv[i], ctx_out)
        s5_c, s5_l = _s5_mixer(s5_pc, s5_pl, s5_a_re[i], s5_a_im[i], s5_log_dt[i], s5_b_re[i], s5_b_im[i],
                               s5_c_re[i], s5_c_im[i], s5_d[i], s5_glu_w[i], s5_glu_b[i], ctx_out)
        mix_l = jnp.concatenate([ssd_l, gla_l, mla_l, s5_l], axis=-1) @ w_out[i]
        h_lat = h_lat + mod_l[2] * mix_l
        wq_t_hi, wq_t_lo = _split_bf16(peer_w_q[i].T)
        keys_hi, keys_lo = _split_bf16(peer_sub_keys[i])
        u_bf, v_bf = peer_u[i].astype(jnp.bfloat16), peer_v[i].astype(jnp.bfloat16)
        peer = functools.partial(_peer_layer, norm_g=norm_ffn_g[i], wq_t_hi=wq_t_hi, wq_t_lo=wq_t_lo,
                                 keys_hi=keys_hi, keys_lo=keys_lo, u_bf=u_bf, v_bf=v_bf)
        h_lat = peer(h_lat.reshape(-1, D_MODEL), shift=mod_l[3], scale=mod_l[4],
                     gate_mod=mod_l[5]).reshape(h_lat.shape)
        if ctx_out:
            mix_c = jnp.concatenate([ssd_c, gla_c, mla_c, s5_c], axis=-1) @ w_out[i]
            h_ctx = h_ctx + mod_c[2] * mix_c
            h_ctx = peer(h_ctx.reshape(-1, D_MODEL), shift=mod_c[3], scale=mod_c[4],
                         gate_mod=mod_c[5]).reshape(h_ctx.shape)
    return _final_norm(h_lat, final_norm_g)
```

```python
import functools
import jax
import jax.numpy as jnp
from jax import lax
import numpy as np
from jax.experimental import pallas as pl
from jax.experimental.pallas import tpu as pltpu

D_MODEL = 1024
DEPTH = 2
GRID_W = 64
NORM_EPS = 1e-6
N_MOD = 6

GROUP_WIDTH = D_MODEL // 4

SSD_WIDTH = GROUP_WIDTH
SSD_HEAD_DIM = 64
SSD_HEADS = SSD_WIDTH // SSD_HEAD_DIM
SSD_GROUPS = 2
SSD_STATE = 128
SSD_CONV = 5
SSD_CHUNK = 128
SSD_CONV_CH = SSD_WIDTH + 2 * SSD_GROUPS * SSD_STATE
SSD_IN = SSD_WIDTH + SSD_CONV_CH + 2 * SSD_HEADS

GLA_WIDTH = GROUP_WIDTH
GLA_HEADS = 4
GLA_DV = GLA_WIDTH // GLA_HEADS
GLA_DK = GLA_DV // 2
GLA_QK = GLA_HEADS * GLA_DK
GLA_GATE_RANK = 16
GLA_TAU = 16.0
GLA_CHUNK = 64
GLA_IN = 2 * GLA_QK + 2 * GLA_WIDTH + 2 * GLA_GATE_RANK

MLA_WIDTH = GROUP_WIDTH
MLA_HEADS = 4
MLA_V = MLA_WIDTH // MLA_HEADS
MLA_NOPE = 64
MLA_ROPE = 32
MLA_Q_RANK = 256
MLA_KV_RANK = 128
MLA_SCALE = (MLA_NOPE + MLA_ROPE) ** -0.5
ROPE_BASE = 10000.0
MLA_IN = MLA_Q_RANK + MLA_KV_RANK + MLA_ROPE

S5_WIDTH = GROUP_WIDTH
S5_GROUP = 16
S5_NGROUPS = S5_WIDTH // S5_GROUP
S5_STATE = 64
S5_MAX_RE = -1e-4
S5_IN = S5_WIDTH
S5_CHUNK = 16
S5_PAIRS = S5_NGROUPS // 2

PEER_KEYS = 128
PEER_EXPERTS = PEER_KEYS * PEER_KEYS
PEER_HEADS = 8
PEER_TOPK = 16
PEER_DQ = 128

LANES = 128
ROW_TILE = 256

F32 = jnp.float32
BF16 = jnp.bfloat16

COL_XS, COL_BM, COL_CM, COL_Z = 0, 256, 512, 768
COL_GLA_V, COL_GLA_R, COL_CQ, COL_S5 = 1024, 1280, 1536, 1792
COL_GLA_Q, COL_GLA_K, COL_CKV, COL_DT, COL_GLR, COL_KR, COL_KRROT = 2048, 2176, 2304, 2432, 2560, 2688, 2816
P_COLS = 2944

_NN_DIMS = (((1,), (0,)), ((), ()))
_NT_DIMS = (((1,), (1,)), ((), ()))
_TN_DIMS = (((0,), (0,)), ((), ()))


def _mm(a, b, dims=_NN_DIMS):
    return lax.dot_general(a, b, dims, preferred_element_type=F32)


def _split_bf16(x):
    hi = x.astype(BF16)
    lo = (x - hi.astype(F32)).astype(BF16)
    return hi, lo


def _split3_bf16(x):
    p1 = x.astype(BF16)
    r1 = x - p1.astype(F32)
    p2 = r1.astype(BF16)
    p3 = (r1 - p2.astype(F32)).astype(BF16)
    return p1, p2, p3


def _dot3(a_hi, a_lo, b_hi, b_lo, dims):
    return _mm(a_hi, b_hi, dims) + _mm(a_hi, b_lo, dims) + _mm(a_lo, b_hi, dims)


def _gelu_erf(x):
    return 0.5 * x * (1.0 + lax.erf(x * (2.0 ** -0.5)))


def _silu(x):
    return x * jax.nn.sigmoid(x)


def _softplus(x):
    return jnp.maximum(x, 0.0) + jnp.log1p(jnp.exp(-jnp.abs(x)))


def _log_sigmoid(x):
    return jnp.minimum(x, 0.0) - jnp.log1p(jnp.exp(-jnp.abs(x)))


def _rms(x, g):
    return x * lax.rsqrt(jnp.mean(x * x, axis=-1, keepdims=True) + NORM_EPS) * g


def _modulated_norm(x, g, shift, scale):
    return _rms(x, g) * (1.0 + scale) + shift


def _causal_mask(n, reverse):
    ri = lax.broadcasted_iota(jnp.int32, (n, n), 0)
    ci = lax.broadcasted_iota(jnp.int32, (n, n), 1)
    return (ci >= ri) if reverse else (ci <= ri)


def _scan_chunk(s, n_lat, n_ctx, reverse):
    if reverse:
        return n_lat + n_ctx - 1 - s
    return jnp.where(s < n_ctx, n_lat + s, s - n_ctx)


def _inproj_kernel(h_ref, g_ref, shift_ref, scale_ref, w_ref, o_ref):
    xn = _modulated_norm(h_ref[0], g_ref[...], shift_ref[0], scale_ref[0])
    o_ref[0] = _mm(xn.astype(BF16), w_ref[...])


def _inproj(hcomb, norm_g, shift_tab, scale_tab, w_pad, n_lat_tiles):
    b, r, d = hcomb.shape
    mod_spec = pl.BlockSpec((1, 1, d), lambda i, t: (jnp.where(t < n_lat_tiles, i, b), 0, 0))
    return pl.pallas_call(
        _inproj_kernel,
        grid=(b, r // ROW_TILE),
        in_specs=[pl.BlockSpec((1, ROW_TILE, d), lambda i, t: (i, t, 0)),
                  pl.BlockSpec((1, d), lambda i, t: (0, 0)), mod_spec, mod_spec,
                  pl.BlockSpec((d, P_COLS), lambda i, t: (0, 0))],
        out_specs=pl.BlockSpec((1, ROW_TILE, P_COLS), lambda i, t: (i, t, 0)),
        out_shape=jax.ShapeDtypeStruct((b, r, P_COLS), F32),
        compiler_params=pltpu.CompilerParams(dimension_semantics=("arbitrary", "arbitrary"),
                                             vmem_limit_bytes=48 * 2 ** 20),
        name="inproj",
    )(hcomb, norm_g.reshape(1, d), shift_tab, scale_tab, w_pad)


def _pack_w_in(w):
    o_ssd, o_gla, o_mla, o_s5 = 0, SSD_IN, SSD_IN + GLA_IN, SSD_IN + GLA_IN + MLA_IN
    out = jnp.zeros((w.shape[0], P_COLS), F32)
    put = lambda out, col, src, width: out.at[:, col:col + width].set(w[:, src:src + width])
    out = put(out, COL_Z, o_ssd, SSD_WIDTH)
    out = put(out, COL_XS, o_ssd + SSD_WIDTH, SSD_CONV_CH)
    out = put(out, COL_DT, o_ssd + SSD_WIDTH + SSD_CONV_CH, 2 * SSD_HEADS)
    out = put(out, COL_GLA_Q, o_gla, GLA_QK)
    out = put(out, COL_GLA_K, o_gla + GLA_QK, GLA_QK)
    out = put(out, COL_GLA_V, o_gla + 2 * GLA_QK, GLA_WIDTH)
    out = put(out, COL_GLA_R, o_gla + 2 * GLA_QK + GLA_WIDTH, GLA_WIDTH)
    out = put(out, COL_GLR, o_gla + 2 * GLA_QK + 2 * GLA_WIDTH, 2 * GLA_GATE_RANK)
    out = put(out, COL_CQ, o_mla, MLA_Q_RANK)
    out = put(out, COL_CKV, o_mla + MLA_Q_RANK, MLA_KV_RANK)
    o_kr = o_mla + MLA_Q_RANK + MLA_KV_RANK
    half = MLA_ROPE // 2
    out = put(out, COL_KR + MLA_NOPE, o_kr, MLA_ROPE)
    out = out.at[:, COL_KRROT + MLA_NOPE:COL_KRROT + MLA_NOPE + half].set(-w[:, o_kr + half:o_kr + MLA_ROPE])
    out = out.at[:, COL_KRROT + MLA_NOPE + half:COL_KRROT + MLA_NOPE + MLA_ROPE].set(w[:, o_kr:o_kr + half])
    out = put(out, COL_S5, o_s5, S5_WIDTH)
    return out.astype(BF16)


def _ssd_prep_kernel(x_ref, prev_ref, next_ref, dt_ref, w_ref, b_ref, bias_ref, xbc_ref, dtc_ref, dtt_ref,
                     *, n_lat_tiles):
    t = pl.program_id(1)
    x = x_ref[0]
    halo = prev_ref.shape[1]
    prev = jnp.where(jnp.logical_and(t > 0, t < n_lat_tiles), prev_ref[0], 0.0)
    nxt = jnp.where(t < n_lat_tiles - 1, next_ref[0], 0.0)
    ext = jnp.concatenate([prev, x, nxt], axis=0)
    rows = ext.shape[0]
    left = SSD_CONV // 2
    acc = jnp.zeros_like(x) + b_ref[...]
    for k in range(SSD_CONV):
        shifted = ext if k == left else pltpu.roll(ext, (left - k) % rows, 0)
        acc = acc + w_ref[k:k + 1, :] * shifted[halo:halo + x.shape[0]]
    xbc_ref[0] = _silu(acc)
    dt = _softplus(dt_ref[0] + bias_ref[...])
    dtc_ref[0] = dt
    dtt_ref[0] = dt.T[:dtt_ref.shape[1]]


def _ssd_scan_kernel(*refs, direction, n_lat, n_ctx, has_prev):
    if has_prev:
        xbc_ref, dtc_ref, dtt_ref, ahr_ref, ahc_ref, yprev_ref, y_ref, state_ref = refs
    else:
        xbc_ref, dtc_ref, dtt_ref, ahr_ref, ahc_ref, y_ref, state_ref = refs
    reverse = direction == 1
    s = pl.program_id(1)

    @pl.when(s == 0)
    def _():
        state_ref[...] = jnp.zeros_like(state_ref)

    q = SSD_CHUNK
    mask = _causal_mask(q, reverse)
    tri = jnp.where(mask, 1.0, 0.0).astype(BF16)
    xbc = xbc_ref[0]
    xs, bm, cm = xbc[:, :SSD_WIDTH], xbc[:, SSD_WIDTH:SSD_WIDTH + 256], xbc[:, SSD_WIDTH + 256:]
    dtc = dtc_ref[0]
    a_col = dtc * ahr_ref[...]
    a_row = dtt_ref[0] * ahc_ref[...]
    acum_col = sum(_mm(tri, part) for part in _split3_bf16(a_col))
    acum_row = sum(_mm(part, tri, _NT_DIMS) for part in _split3_bf16(a_row))
    end = 0 if reverse else q - 1
    bm_bf, cm_bf = bm.astype(BF16), cm.astype(BF16)
    ys = []
    cb = {}
    for h in range(SSD_HEADS):
        g = h // (SSD_HEADS // SSD_GROUPS)
        gs = slice(g * SSD_STATE, (g + 1) * SSD_STATE)
        if g not in cb:
            cb[g] = _mm(cm_bf[:, gs], bm_bf[:, gs], _NT_DIMS)
        ch = direction * SSD_HEADS + h
        ac = acum_col[:, ch:ch + 1]
        ar = acum_row[ch:ch + 1, :]
        decay = jnp.exp(jnp.where(mask, ac - ar, -jnp.inf))
        xd = xs[:, h * SSD_HEAD_DIM:(h + 1) * SSD_HEAD_DIM] * dtc[:, ch:ch + 1]
        y_diag = _mm((cb[g] * decay).astype(BF16), xd.astype(BF16))
        a_end = ac[end:end + 1, :]
        st_local = _mm((xd * jnp.exp(a_end - ac)).astype(BF16), bm_bf[:, gs], _TN_DIMS)
        hs = state_ref[h]
        y_off = jnp.exp(ac) * _mm(cm_bf[:, gs], hs.astype(BF16), _NT_DIMS)
        state_ref[h] = jnp.exp(a_end) * hs + st_local
        ys.append(y_diag + y_off)
    y = jnp.concatenate(ys, axis=1)
    if has_prev:
        y = y + yprev_ref[0]
    y_ref[0] = y


def _ssd_mixer(p, conv_w, conv_b, a_log, dt_bias, n_lat_tiles):
    b, r, _ = p.shape
    nt = r // ROW_TILE
    halo = 8
    hb = ROW_TILE // halo
    w8 = jnp.zeros((8, SSD_CONV_CH), F32).at[:SSD_CONV].set(conv_w)
    bias = jnp.zeros((1, LANES), F32).at[0, :2 * SSD_HEADS].set(dt_bias.reshape(-1))
    xbc, dtc, dtt = pl.pallas_call(
        functools.partial(_ssd_prep_kernel, n_lat_tiles=n_lat_tiles),
        grid=(b, nt),
        in_specs=[pl.BlockSpec((1, ROW_TILE, SSD_CONV_CH), lambda i, t: (i, t, 0)),
                  pl.BlockSpec((1, halo, SSD_CONV_CH), lambda i, t: (i, jnp.maximum(t * hb - 1, 0), 0)),
                  pl.BlockSpec((1, halo, SSD_CONV_CH), lambda i, t: (i, jnp.minimum((t + 1) * hb, nt * hb - 1), 0)),
                  pl.BlockSpec((1, ROW_TILE, LANES), lambda i, t: (i, t, COL_DT // LANES)),
                  pl.BlockSpec((8, SSD_CONV_CH), lambda i, t: (0, 0)),
                  pl.BlockSpec((1, SSD_CONV_CH), lambda i, t: (0, 0)),
                  pl.BlockSpec((1, LANES), lambda i, t: (0, 0))],
        out_specs=[pl.BlockSpec((1, ROW_TILE, SSD_CONV_CH), lambda i, t: (i, t, 0)),
                   pl.BlockSpec((1, ROW_TILE, LANES), lambda i, t: (i, t, 0)),
                   pl.BlockSpec((1, 8, ROW_TILE), lambda i, t: (i, 0, t))],
        out_shape=[jax.ShapeDtypeStruct((b, r, SSD_CONV_CH), F32),
                   jax.ShapeDtypeStruct((b, r, LANES), F32),
                   jax.ShapeDtypeStruct((b, 8, r), F32)],
        compiler_params=pltpu.CompilerParams(dimension_semantics=("arbitrary", "arbitrary")),
        name="ssd_prep",
    )(p, p, p, p, w8, conv_b.reshape(1, -1), bias)

    a_head = -jnp.exp(a_log.astype(F32)).reshape(-1)
    ahr = jnp.zeros((1, LANES), F32).at[0, :2 * SSD_HEADS].set(a_head)
    ahc = a_head.reshape(2 * SSD_HEADS, 1)
    n_lat = n_lat_tiles * ROW_TILE // SSD_CHUNK
    n_ctx = r // SSD_CHUNK - n_lat
    y = None
    for direction in (0, 1):
        cidx = functools.partial(_scan_chunk, n_lat=n_lat, n_ctx=n_ctx, reverse=direction == 1)
        in_specs = [pl.BlockSpec((1, SSD_CHUNK, SSD_CONV_CH), lambda i, s: (i, cidx(s), 0)),
                    pl.BlockSpec((1, SSD_CHUNK, LANES), lambda i, s: (i, cidx(s), 0)),
                    pl.BlockSpec((1, 8, SSD_CHUNK), lambda i, s: (i, 0, cidx(s))),
                    pl.BlockSpec((1, LANES), lambda i, s: (0, 0)),
                    pl.BlockSpec((2 * SSD_HEADS, 1), lambda i, s: (0, 0))]
        args = [xbc, dtc, dtt, ahr, ahc]
        y_spec = pl.BlockSpec((1, SSD_CHUNK, SSD_WIDTH), lambda i, s: (i, cidx(s), 0))
        if y is not None:
            in_specs.append(y_spec)
            args.append(y)
        y = pl.pallas_call(
            functools.partial(_ssd_scan_kernel, direction=direction, n_lat=n_lat, n_ctx=n_ctx,
                              has_prev=y is not None),
            grid=(b, n_lat + n_ctx),
            in_specs=in_specs,
            out_specs=y_spec,
            out_shape=jax.ShapeDtypeStruct((b, r, SSD_WIDTH), F32),
            scratch_shapes=[pltpu.VMEM((SSD_HEADS, SSD_HEAD_DIM, SSD_STATE), F32)],
            compiler_params=pltpu.CompilerParams(dimension_semantics=("arbitrary", "arbitrary")),
            name=f"ssd_scan_{direction}",
        )(*args)
    return y, xbc


def _gla_scan_kernel(*refs, direction, has_prev):
    if has_prev:
        q_ref, k_ref, v_ref, glr_ref, wg_ref, bias_ref, oprev_ref, o_ref, st_ref = refs
    else:
        q_ref, k_ref, v_ref, glr_ref, wg_ref, bias_ref, o_ref, st_ref = refs
    reverse = direction == 1
    s = pl.program_id(1)

    @pl.when(s == 0)
    def _():
        st_ref[...] = jnp.zeros_like(st_ref)

    n = GLA_CHUNK
    mask = _causal_mask(n, reverse)
    tri = jnp.where(mask, 1.0, 0.0).astype(BF16)
    g_hi, g_lo = _split_bf16(glr_ref[0])
    logits = _dot3(g_hi, g_lo, wg_ref[0], wg_ref[1], _NN_DIMS) + bias_ref[...]
    logg = _log_sigmoid(logits) * (1.0 / GLA_TAU)
    bcum = sum(_mm(tri, part) for part in _split3_bf16(logg))
    end = 0 if reverse else n - 1
    b_end = bcum[end:end + 1, :]
    q, k, v = q_ref[0], k_ref[0], v_ref[0]
    qe = q * jnp.exp(bcum) * (GLA_DK ** -0.5)
    ke = (k * jnp.exp(-bcum)).astype(BF16)
    kd = k * jnp.exp(b_end - bcum)
    decay_end = jnp.exp(b_end)
    lane_head = lax.broadcasted_iota(jnp.int32, (1, GLA_QK), 1) >> (GLA_DK.bit_length() - 1)
    outs = []
    for h in range(GLA_HEADS):
        hm = lane_head == h
        qh = jnp.where(hm, qe, 0.0).astype(BF16)
        att = jnp.where(mask, _mm(qh, ke, _NT_DIMS), 0.0)
        vh = v[:, h * GLA_DV:(h + 1) * GLA_DV].astype(BF16)
        st = st_ref[h]
        o_h = _mm(att.astype(BF16), vh) + _mm(qh, st.astype(BF16), _NT_DIMS)
        local = _mm(vh, jnp.where(hm, kd, 0.0).astype(BF16), _TN_DIMS)
        st_ref[h] = st * decay_end + local
        outs.append(o_h)
    o = jnp.concatenate(outs, axis=1)
    if has_prev:
        o = o + oprev_ref[0]
    o_ref[0] = o


def _gla_mixer(p, gate_w, gate_b, n_lat_tiles):
    b, r, _ = p.shape
    n_lat = n_lat_tiles * ROW_TILE // GLA_CHUNK
    n_ctx = r // GLA_CHUNK - n_lat
    o = None
    for direction in (0, 1):
        cidx = functools.partial(_scan_chunk, n_lat=n_lat, n_ctx=n_ctx, reverse=direction == 1)
        wg = jnp.zeros((LANES, GLA_QK), F32).at[direction * GLA_GATE_RANK:(direction + 1) * GLA_GATE_RANK].set(
            gate_w[direction])
        wg = jnp.stack(_split_bf16(wg))
        blk = lambda width, col: pl.BlockSpec((1, GLA_CHUNK, width), lambda i, s: (i, cidx(s), col // width))
        in_specs = [blk(GLA_QK, COL_GLA_Q), blk(GLA_QK, COL_GLA_K), blk(GLA_WIDTH, COL_GLA_V), blk(LANES, COL_GLR),
                    pl.BlockSpec((2, LANES, GLA_QK), lambda i, s: (0, 0, 0)),
                    pl.BlockSpec((1, GLA_QK), lambda i, s: (0, 0))]
        args = [p, p, p, p, wg, gate_b[direction].reshape(1, -1)]
        o_spec = pl.BlockSpec((1, GLA_CHUNK, GLA_WIDTH), lambda i, s: (i, cidx(s), 0))
        if o is not None:
            in_specs.append(o_spec)
            args.append(o)
        o = pl.pallas_call(
            functools.partial(_gla_scan_kernel, direction=direction, has_prev=o is not None),
            grid=(b, n_lat + n_ctx),
            in_specs=in_specs,
            out_specs=o_spec,
            out_shape=jax.ShapeDtypeStruct((b, r, GLA_WIDTH), F32),
            scratch_shapes=[pltpu.VMEM((GLA_HEADS, GLA_DV, GLA_QK), F32)],
            compiler_params=pltpu.CompilerParams(dimension_semantics=("arbitrary", "arbitrary")),
            name=f"gla_scan_{direction}",
        )(*args)
    return o


MLA_Q_TILE = 1024
MLA_K_TILE = 256


def _mla_prep_kernel(cq_ref, ckv_ref, kr_ref, krrot_ref, onec_ref, sinr_ref, gq_ref, gkv_ref,
                     wq_ref, wqr_ref, wk_ref, wv_ref, q_ref, k_ref, v_ref):
    qn = _rms(cq_ref[0], gq_ref[...]).astype(BF16)
    kvn = _rms(ckv_ref[0], gkv_ref[...]).astype(BF16)
    onec, sinr = onec_ref[...], sinr_ref[...]
    k_rope = kr_ref[0] * onec + krrot_ref[0] * sinr
    ones_lane = jnp.where(lax.broadcasted_iota(jnp.int32, (1, LANES), 1) == MLA_V, 1.0, 0.0)
    for h in range(MLA_HEADS):
        qh = _mm(qn, wq_ref[h]) * onec + _mm(qn, wqr_ref[h]) * sinr
        q_ref[0, h] = (qh * MLA_SCALE).astype(BF16)
        k_ref[0, h] = (_mm(kvn, wk_ref[h]) + k_rope).astype(BF16)
        v_ref[0, h] = (_mm(kvn, wv_ref[h]) + ones_lane).astype(BF16)


def _mla_attn_kernel(q_ref, k_ref, v_ref, o_ref, m_ref, acc_ref):
    j = pl.program_id(2)

    @pl.when(j == 0)
    def _():
        m_ref[...] = jnp.full_like(m_ref, -jnp.inf)
        acc_ref[...] = jnp.zeros_like(acc_ref)

    reps = k_ref.shape[2] // LANES
    for h in range(MLA_HEADS):
        s = _mm(q_ref[0, h], k_ref[0, h], _NT_DIMS)
        m_prev = m_ref[h]
        m_new = jnp.maximum(m_prev, jnp.max(s, axis=1, keepdims=True))
        p = jnp.exp(s - jnp.concatenate([m_new] * reps, axis=1))
        acc_ref[h] = jnp.exp(m_prev - m_new) * acc_ref[h] + _mm(p.astype(BF16), v_ref[0, h])
        m_ref[h] = m_new

    @pl.when(j == pl.num_programs(2) - 1)
    def _():
        outs = []
        for h in range(MLA_HEADS):
            acc = acc_ref[h]
            outs.append(acc[:, :MLA_V] / acc[:, MLA_V:MLA_V + 1])
        o_ref[0] = jnp.concatenate(outs, axis=1)


def _rope_tables(n_lat, n_rows):
    rows = n_lat // GRID_W
    row = jnp.repeat(jnp.arange(rows, dtype=F32), GRID_W)
    col = jnp.tile(jnp.arange(GRID_W, dtype=F32), rows)
    half = MLA_ROPE // 2
    inv = ROPE_BASE ** (-jnp.arange(0, half, 2, dtype=F32) / half)
    ang = jnp.concatenate([row[:, None] * inv, col[:, None] * inv], axis=-1)
    cos = jnp.concatenate([jnp.cos(ang), jnp.ones((n_rows - n_lat, half), F32)], axis=0)
    sin = jnp.concatenate([jnp.sin(ang), jnp.zeros((n_rows - n_lat, half), F32)], axis=0)
    pad = jnp.zeros((n_rows, LANES - MLA_NOPE - MLA_ROPE), F32)
    onec = jnp.concatenate([jnp.ones((n_rows, MLA_NOPE), F32), cos, cos, pad], axis=1)
    sinr = jnp.concatenate([jnp.zeros((n_rows, MLA_NOPE), F32), sin, sin, pad], axis=1)
    return onec, sinr


def _mla_weights(w_uq, w_ukv):
    dqk = MLA_NOPE + MLA_ROPE
    half = MLA_ROPE // 2
    wq = w_uq.reshape(MLA_Q_RANK, MLA_HEADS, dqk).transpose(1, 0, 2)
    rot = jnp.concatenate([jnp.zeros_like(wq[..., :MLA_NOPE]), -wq[..., MLA_NOPE + half:], wq[..., MLA_NOPE:MLA_NOPE + half]],
                          axis=-1)
    padq = lambda w: jnp.pad(w, ((0, 0), (0, 0), (0, LANES - dqk))).astype(BF16)
    wkv = w_ukv.reshape(MLA_KV_RANK, MLA_HEADS, MLA_NOPE + MLA_V).transpose(1, 0, 2)
    padk = lambda w: jnp.pad(w, ((0, 0), (0, 0), (0, LANES - w.shape[-1]))).astype(BF16)
    return padq(wq), padq(rot), padk(wkv[..., :MLA_NOPE]), padk(wkv[..., MLA_NOPE:])


def _mla_attention(q, k, v, q_tile, q_off, n_q, k_off, n_k):
    b = q.shape[0]
    kt = MLA_K_TILE
    return pl.pallas_call(
        _mla_attn_kernel,
        grid=(b, n_q, n_k),
        in_specs=[pl.BlockSpec((1, MLA_HEADS, q_tile, LANES), lambda i, a, j: (i, 0, q_off + a, 0)),
                  pl.BlockSpec((1, MLA_HEADS, kt, LANES), lambda i, a, j: (i, 0, k_off + j, 0)),
                  pl.BlockSpec((1, MLA_HEADS, kt, LANES), lambda i, a, j: (i, 0, k_off + j, 0))],
        out_specs=pl.BlockSpec((1, q_tile, MLA_WIDTH), lambda i, a, j: (i, a, 0)),
        out_shape=jax.ShapeDtypeStruct((b, n_q * q_tile, MLA_WIDTH), F32),
        scratch_shapes=[pltpu.VMEM((MLA_HEADS, q_tile, LANES), F32), pltpu.VMEM((MLA_HEADS, q_tile, LANES), F32)],
        compiler_params=pltpu.CompilerParams(dimension_semantics=("arbitrary", "arbitrary", "arbitrary")),
        name="mla_attn",
    )(q, k, v)


def _mla_mixer(p, q_norm_g, w_uq, kv_norm_g, w_ukv, n_lat_tiles, ctx_out):
    b, r, _ = p.shape
    nt = r // ROW_TILE
    n_lat = n_lat_tiles * ROW_TILE
    onec, sinr = _rope_tables(n_lat, r)
    wq, wqr, wk, wv = _mla_weights(w_uq, w_ukv)
    blk = lambda width, col: pl.BlockSpec((1, ROW_TILE, width), lambda i, t: (i, t, col // width))
    tab = pl.BlockSpec((ROW_TILE, LANES), lambda i, t: (t, 0))
    full = lambda *shape: pl.BlockSpec(shape, lambda i, t: (0,) * len(shape))
    head_out = pl.BlockSpec((1, MLA_HEADS, ROW_TILE, LANES), lambda i, t: (i, 0, t, 0))
    q, k, v = pl.pallas_call(
        _mla_prep_kernel,
        grid=(b, nt),
        in_specs=[blk(MLA_Q_RANK, COL_CQ), blk(LANES, COL_CKV), blk(LANES, COL_KR), blk(LANES, COL_KRROT), tab, tab,
                  full(1, MLA_Q_RANK), full(1, MLA_KV_RANK),
                  full(MLA_HEADS, MLA_Q_RANK, LANES), full(MLA_HEADS, MLA_Q_RANK, LANES),
                  full(MLA_HEADS, MLA_KV_RANK, LANES), full(MLA_HEADS, MLA_KV_RANK, LANES)],
        out_specs=[head_out] * 3,
        out_shape=[jax.ShapeDtypeStruct((b, MLA_HEADS, r, LANES), BF16)] * 3,
        compiler_params=pltpu.CompilerParams(dimension_semantics=("arbitrary", "arbitrary")),
        name="mla_prep",
    )(p, p, p, p, onec, sinr, q_norm_g.reshape(1, -1), kv_norm_g.reshape(1, -1), wq, wqr, wk, wv)
    q_tile = min(MLA_Q_TILE, n_lat)
    y_lat = _mla_attention(q, k, v, q_tile, 0, n_lat // q_tile, 0, r // MLA_K_TILE)
    y_ctx = None
    if ctx_out:
        n_ctx = r - n_lat
        y_ctx = _mla_attention(q, k, v, n_ctx, n_lat // n_ctx, 1, n_lat // MLA_K_TILE, n_ctx // MLA_K_TILE)
    return y_lat, y_ctx


def _s5_matrices(a_re, a_im, log_dt, b_re, b_im, c_re, c_im):
    q, ng, ns, nc = S5_CHUNK, S5_NGROUPS, S5_STATE, S5_GROUP
    lam = jnp.minimum(a_re.astype(F32), S5_MAX_RE) + 1j * a_im.astype(F32)
    step = jnp.exp(log_dt.astype(F32))[..., None]
    abar = jnp.exp(lam * step)
    bmat = b_re.astype(F32) + 1j * b_im.astype(F32)
    bbar = ((abar - 1.0) / lam)[..., None] * bmat
    cmat = c_re.astype(F32) + 1j * c_im.astype(F32)
    pw = jnp.exp((lam * step)[..., None] * jnp.arange(q + 1, dtype=F32))
    kern = jnp.einsum('dgcn,dgnl,dgnk->dglck', cmat, pw[..., :q], bbar).real
    ii = jnp.arange(q)
    lag_f = ii[None, :] - ii[:, None]
    gather = lambda kd, lag: jnp.where((lag >= 0)[None, :, :, None, None], kd[:, jnp.clip(lag, 0, q - 1)], 0.0)
    t_f = gather(kern[0], lag_f).transpose(0, 1, 4, 2, 3)
    t_b = gather(kern[1], -lag_f).transpose(0, 1, 4, 2, 3)
    t_sum = (t_f + t_b).reshape(ng, q * nc, q * nc)
    pw_f = pw[0][..., q - 1 - ii]
    pw_b = pw[1][..., ii]
    wst = lambda pwd, bb: jnp.einsum('gnj,gnc->gjcn', pwd, bb).reshape(ng, q * nc, ns)
    wst_f, wst_b = wst(pw_f, bbar[0]), wst(pw_b, bbar[1])
    wout = lambda pwd, cm: jnp.einsum('gcn,gni->gnic', cm, pwd).reshape(ng, ns, q * nc)
    wo_f, wo_b = wout(pw[0][..., ii + 1], cmat[0]), wout(pw[1][..., q - ii], cmat[1])
    aq = pw[..., q]

    def pair_cols(x):
        x = x.reshape(S5_PAIRS, 2, x.shape[1], x.shape[2])
        z = jnp.zeros_like(x[:, 0])
        return jnp.concatenate([jnp.concatenate([x[:, 0], z], axis=2), jnp.concatenate([z, x[:, 1]], axis=2)], axis=1)

    w_local = jnp.concatenate([pair_cols(wst_f.real), pair_cols(wst_f.imag),
                               pair_cols(wst_b.real), pair_cols(wst_b.imag)], axis=2)
    w_out = jnp.concatenate([pair_cols(t_sum), pair_cols(wo_f.real), pair_cols(-wo_f.imag),
                             pair_cols(wo_b.real), pair_cols(-wo_b.imag)], axis=1)
    aq_pair = aq.reshape(2, S5_PAIRS, 2 * ns)
    aq_tab = jnp.concatenate([aq_pair[0].real, aq_pair[0].imag, aq_pair[1].real, aq_pair[1].imag], axis=1)
    return w_local.astype(BF16), w_out.astype(BF16), aq_tab.reshape(S5_PAIRS, 1, 8 * ns).astype(F32)


def _s5_local_kernel(u_ref, w_ref, s_ref):
    s_ref[...] = _mm(u_ref[0], w_ref[0])


def _s5_scan_kernel(s_ref, aq_ref, hs_ref, *, n_lat, n_ctx, nb):
    w = 2 * S5_STATE
    aq = aq_ref[0]
    a = [aq[:, i * w:(i + 1) * w] for i in range(4)]
    zero = jnp.zeros((nb, w), F32)
    slab = 8
    cps = slab // nb

    def run_slab(s_re, s_im, a_re, a_im, h_re, h_im, order):
        ent_re, ent_im = [None] * cps, [None] * cps
        for c in order:
            ent_re[c], ent_im[c] = h_re, h_im
            rows = slice(c * nb, (c + 1) * nb)
            h_re, h_im = a_re * h_re - a_im * h_im + s_re[rows], a_re * h_im + a_im * h_re + s_im[rows]
        return jnp.concatenate(ent_re, axis=0), jnp.concatenate(ent_im, axis=0), h_re, h_im

    def body(kk, carry):
        f_re, f_im, b_re, b_im = carry
        rf = pl.multiple_of(_scan_chunk(kk, n_lat // cps, n_ctx // cps, False) * slab, slab)
        rb = pl.multiple_of(_scan_chunk(kk, n_lat // cps, n_ctx // cps, True) * slab, slab)
        e_re, e_im, f_re, f_im = run_slab(s_ref[pl.ds(rf, slab), 0:w], s_ref[pl.ds(rf, slab), w:2 * w],
                                          a[0], a[1], f_re, f_im, range(cps))
        hs_ref[pl.ds(rf, slab), 0:w] = e_re
        hs_ref[pl.ds(rf, slab), w:2 * w] = e_im
        e_re, e_im, b_re, b_im = run_slab(s_ref[pl.ds(rb, slab), 2 * w:3 * w], s_ref[pl.ds(rb, slab), 3 * w:4 * w],
                                          a[2], a[3], b_re, b_im, range(cps - 1, -1, -1))
        hs_ref[pl.ds(rb, slab), 2 * w:3 * w] = e_re
        hs_ref[pl.ds(rb, slab), 3 * w:4 * w] = e_im
        return f_re, f_im, b_re, b_im

    lax.fori_loop(0, (n_lat + n_ctx) // cps, body, (zero, zero, zero, zero))


def _s5_out_kernel(u_ref, hs_ref, w_ref, y_ref):
    k_u = u_ref.shape[2]
    y_ref[0] = _mm(u_ref[0], w_ref[0, :k_u]) + _mm(hs_ref[...].astype(BF16), w_ref[0, k_u:])


def _s5_mixer(p, a_re, a_im, log_dt, b_re, b_im, c_re, c_im, n_lat_tiles):
    b, r, _ = p.shape
    q, nc = S5_CHUNK, S5_GROUP
    n_chunks = r // q
    m = n_chunks * b
    w_local, w_out, aq_tab = _s5_matrices(a_re, a_im, log_dt, b_re, b_im, c_re, c_im)
    u = p[:, :, COL_S5:COL_S5 + S5_WIDTH]
    u_pairs = u.reshape(b, n_chunks, q, S5_PAIRS, 2, nc).transpose(3, 1, 0, 4, 2, 5).reshape(S5_PAIRS, m, 2 * q * nc)
    u_pairs = u_pairs.astype(BF16)
    kw = 2 * q * nc
    cp = pltpu.CompilerParams(dimension_semantics=("arbitrary",))
    s_loc = pl.pallas_call(
        _s5_local_kernel,
        grid=(S5_PAIRS,),
        in_specs=[pl.BlockSpec((1, m, kw), lambda g: (g, 0, 0)), pl.BlockSpec((1, kw, kw), lambda g: (g, 0, 0))],
        out_specs=pl.BlockSpec((m, kw), lambda g: (0, g)),
        out_shape=jax.ShapeDtypeStruct((m, S5_PAIRS * kw), F32),
        compiler_params=cp, name="s5_local",
    )(u_pairs, w_local)
    n_lat = n_lat_tiles * ROW_TILE // q
    hs = pl.pallas_call(
        functools.partial(_s5_scan_kernel, n_lat=n_lat, n_ctx=n_chunks - n_lat, nb=b),
        grid=(S5_PAIRS,),
        in_specs=[pl.BlockSpec((m, kw), lambda g: (0, g)), pl.BlockSpec((1, 1, kw), lambda g: (g, 0, 0))],
        out_specs=pl.BlockSpec((m, kw), lambda g: (0, g)),
        out_shape=jax.ShapeDtypeStruct((m, S5_PAIRS * kw), F32),
        compiler_params=cp, name="s5_scan",
    )(s_loc, aq_tab)
    y_pairs = pl.pallas_call(
        _s5_out_kernel,
        grid=(S5_PAIRS,),
        in_specs=[pl.BlockSpec((1, m, kw), lambda g: (g, 0, 0)), pl.BlockSpec((m, kw), lambda g: (0, g)),
                  pl.BlockSpec((1, 2 * kw, kw), lambda g: (g, 0, 0))],
        out_specs=pl.BlockSpec((1, m, kw), lambda g: (g, 0, 0)),
        out_shape=jax.ShapeDtypeStruct((S5_PAIRS, m, kw), F32),
        compiler_params=cp, name="s5_out",
    )(u_pairs, hs, w_out)
    y = y_pairs.reshape(S5_PAIRS, n_chunks, b, 2, q, nc).transpose(2, 1, 4, 0, 3, 5)
    return y.reshape(b, r, S5_WIDTH)


def _post_kernel(h_ref, xs_ref, z_ref, r_ref, u_ref, ssd_ref, gla_ref, mla_ref, s5_ref,
                 ssd_d_ref, ssd_g_ref, gla_g_ref, s5_d_ref, glu_w_ref, glu_b_ref, w_out_ref, mod_ref, o_ref):
    y = ssd_ref[0] + ssd_d_ref[...] * xs_ref[0]
    ssd = _rms(y * _silu(z_ref[0]), ssd_g_ref[...])
    o = gla_ref[0]
    lane_head = lax.broadcasted_iota(jnp.int32, (1, GLA_WIDTH), 1) >> (GLA_DV.bit_length() - 1)
    ms = jnp.zeros_like(o)
    for h in range(GLA_HEADS):
        oh = o[:, h * GLA_DV:(h + 1) * GLA_DV]
        ms = jnp.where(lane_head == h, jnp.mean(oh * oh, axis=-1, keepdims=True), ms)
    gla = o * lax.rsqrt(ms + NORM_EPS) * gla_g_ref[...] * _silu(r_ref[0])
    y5 = _gelu_erf(s5_ref[0] + s5_d_ref[...] * u_ref[0])
    s5 = y5 * jax.nn.sigmoid(_mm(y5.astype(BF16), glu_w_ref[...]) + glu_b_ref[...])
    mix_in = jnp.concatenate([ssd, gla, mla_ref[0], s5], axis=1).astype(BF16)
    o_ref[0] = h_ref[0] + mod_ref[0] * _mm(mix_in, w_out_ref[...])


def _post(h, p, ssd_xbc, ssd_y, gla_o, mla_y, s5_y, ssd_d, ssd_norm_g, gla_norm_g, s5_d, glu_w, glu_b, w_out, mod,
          row_off, mla_off):
    b, rows, d = h.shape
    w = GROUP_WIDTH
    pblk = lambda col: pl.BlockSpec((1, ROW_TILE, w), lambda i, t: (i, row_off + t, col // w))
    yblk = pl.BlockSpec((1, ROW_TILE, w), lambda i, t: (i, row_off + t, 0))
    full = lambda *shape: pl.BlockSpec(shape, lambda i, t: (0,) * len(shape))
    vec = lambda x: x.reshape(1, -1).astype(F32)
    n_mod = mod.shape[0]
    return pl.pallas_call(
        _post_kernel,
        grid=(b, rows // ROW_TILE),
        in_specs=[pl.BlockSpec((1, ROW_TILE, d), lambda i, t: (i, t, 0)),
                  yblk, pblk(COL_Z), pblk(COL_GLA_R), pblk(COL_S5), yblk, yblk,
                  pl.BlockSpec((1, ROW_TILE, w), lambda i, t: (i, mla_off + t, 0)), yblk,
                  full(1, w), full(1, w), full(1, w), full(1, w), full(w, w), full(1, w), full(d, d),
                  pl.BlockSpec((1, 1, d), lambda i, t: (jnp.minimum(i, n_mod - 1), 0, 0))],
        out_specs=pl.BlockSpec((1, ROW_TILE, d), lambda i, t: (i, t, 0)),
        out_shape=jax.ShapeDtypeStruct((b, rows, d), F32),
        compiler_params=pltpu.CompilerParams(dimension_semantics=("arbitrary", "arbitrary")),
        name="mix_post",
    )(h, ssd_xbc, p, p, p, ssd_y, gla_o, mla_y, s5_y,
      vec(jnp.repeat(ssd_d, SSD_HEAD_DIM)), vec(ssd_norm_g), vec(jnp.tile(gla_norm_g, GLA_HEADS)), vec(s5_d),
      glu_w.astype(BF16), vec(glu_b), w_out.astype(BF16), mod)


PEER_ROUTE_TOKENS = 256
PEER_GATE_TOKENS = 256
PEER_DENSE_TOKENS = 1024
PEER_DENSE_EXPERTS = 512
PEER_SLOTS = PEER_HEADS * PEER_TOPK


def _topk_rows(s, k):
    n_rows = s.shape[0]
    rows = lax.broadcasted_iota(jnp.int32, s.shape, 0)
    vals, idxs = [], []
    for _ in range(k):
        m = jnp.max(s, axis=0, keepdims=True)
        idx = jnp.min(jnp.where(s == m, rows, n_rows), axis=0, keepdims=True)
        vals.append(m)
        idxs.append(idx)
        s = jnp.where(rows == idx, -jnp.inf, s)
    return jnp.concatenate(vals, axis=0), jnp.concatenate(idxs, axis=0)


def _select_rows(pos, table):
    out = jnp.zeros(pos.shape, table.dtype)
    for r in range(table.shape[0]):
        out = jnp.where(pos == r, table[r:r + 1, :], out)
    return out


def _peer_route_kernel(h_ref, g_ref, shift_ref, scale_ref, wq_hi_ref, wq_lo_ref, k_hi_ref, k_lo_ref,
                       xn_ref, i1_ref, i2_ref, gate_ref, q_scr):
    xn = _modulated_norm(h_ref[...], g_ref[...], shift_ref[0], scale_ref[0])
    xn_ref[...] = xn.astype(BF16)
    x_hi, x_lo = _split_bf16(xn)
    q_scr[...] = _dot3(wq_hi_ref[...], wq_lo_ref[...], x_hi, x_lo, _NT_DIMS)
    half = PEER_DQ // 2

    def head_body(h, carry):
        base = pl.multiple_of(h * PEER_DQ, PEER_DQ)
        tops = []
        for j in range(2):
            qq = q_scr[pl.ds(base + j * half, half), :]
            q_hi, q_lo = _split_bf16(qq)
            s = _dot3(k_hi_ref[j, h], k_lo_ref[j, h], q_hi, q_lo, _NN_DIMS)
            tops.append(_topk_rows(s, PEER_TOPK))
        (v1, i1), (v2, i2) = tops
        cand = jnp.concatenate([v1[a:a + 1, :] + v2 for a in range(PEER_TOPK)], axis=0)
        best, pos = _topk_rows(cand, PEER_TOPK)
        e = jnp.exp(best - best[0:1, :])
        gates = e / jnp.sum(e, axis=0, keepdims=True)
        row0 = pl.multiple_of(h * PEER_TOPK, PEER_TOPK)
        i1_ref[pl.ds(row0, PEER_TOPK), :] = _select_rows(pos >> (PEER_TOPK.bit_length() - 1), i1)
        i2_ref[pl.ds(row0, PEER_TOPK), :] = _select_rows(pos & (PEER_TOPK - 1), i2)
        gate_ref[pl.ds(row0, PEER_TOPK), :] = gates
        return carry

    lax.fori_loop(0, PEER_HEADS, head_body, 0)


def _peer_gate_kernel(i1_ref, i2_ref, gate_ref, g_ref):
    rows = lax.broadcasted_iota(jnp.int32, (PEER_KEYS, PEER_SLOTS), 0)

    def token_body(t, carry):
        a = i1_ref[pl.ds(t, 1), :]
        b = i2_ref[pl.ds(t, 1), :]
        w = gate_ref[pl.ds(t, 1), :]
        lhs = jnp.where(rows == a, w, 0.0).astype(BF16)
        rhs = jnp.where(rows == b, 1.0, 0.0).astype(BF16)
        g_ref[t] = _mm(lhs, rhs, _NT_DIMS).astype(BF16)
        return carry

    lax.fori_loop(0, g_ref.shape[0], token_body, 0)


def _peer_dense_kernel(xn_ref, u_ref, v_ref, gmat_ref, h_ref, mod_ref, o_ref, acc_ref):
    j = pl.program_id(1)

    @pl.when(j == 0)
    def _():
        acc_ref[...] = jnp.zeros_like(acc_ref)

    hid = _mm(xn_ref[...], u_ref[...], _NT_DIMS)
    y = gmat_ref[...].astype(F32) * _gelu_erf(hid)
    acc_ref[...] += _mm(y.astype(BF16), v_ref[...])

    @pl.when(j == pl.num_programs(1) - 1)
    def _():
        o_ref[...] = h_ref[...] + mod_ref[0] * acc_ref[...]


def _peer_layer(h, norm_g, shift, scale, gate_mod, wq_t_hi, wq_t_lo, keys_hi, keys_lo, u_bf, v_bf):
    n, d = h.shape
    nb = shift.shape[0]
    rows_per_batch = n // nb
    tr = min(PEER_ROUTE_TOKENS, rows_per_batch)
    full = lambda *shape: pl.BlockSpec(shape, lambda i: (0,) * len(shape))
    per_batch = lambda t: pl.BlockSpec((1, 1, d), lambda i: (i * t // rows_per_batch, 0, 0))
    xn, i1, i2, gate = pl.pallas_call(
        _peer_route_kernel,
        grid=(n // tr,),
        in_specs=[pl.BlockSpec((tr, d), lambda i: (i, 0)), full(1, d), per_batch(tr), per_batch(tr),
                  full(PEER_HEADS * PEER_DQ, d), full(PEER_HEADS * PEER_DQ, d),
                  full(2, PEER_HEADS, PEER_KEYS, PEER_DQ // 2), full(2, PEER_HEADS, PEER_KEYS, PEER_DQ // 2)],
        out_specs=[pl.BlockSpec((tr, d), lambda i: (i, 0))] + [pl.BlockSpec((PEER_SLOTS, tr), lambda i: (0, i))] * 3,
        out_shape=[jax.ShapeDtypeStruct((n, d), BF16),
                   jax.ShapeDtypeStruct((PEER_SLOTS, n), jnp.int32),
                   jax.ShapeDtypeStruct((PEER_SLOTS, n), jnp.int32),
                   jax.ShapeDtypeStruct((PEER_SLOTS, n), F32)],
        scratch_shapes=[pltpu.VMEM((PEER_HEADS * PEER_DQ, tr), F32)],
        compiler_params=pltpu.CompilerParams(dimension_semantics=("arbitrary",)),
        name="peer_route",
    )(h, norm_g.reshape(1, d), shift, scale, wq_t_hi, wq_t_lo, keys_hi, keys_lo)

    tg = min(PEER_GATE_TOKENS, n)
    slot_spec = pl.BlockSpec((tg, PEER_SLOTS), lambda i: (i, 0))
    gmat = pl.pallas_call(
        _peer_gate_kernel,
        grid=(n // tg,),
        in_specs=[slot_spec, slot_spec, slot_spec],
        out_specs=pl.BlockSpec((tg, PEER_KEYS, PEER_KEYS), lambda i: (i, 0, 0)),
        out_shape=jax.ShapeDtypeStruct((n, PEER_KEYS, PEER_KEYS), BF16),
        compiler_params=pltpu.CompilerParams(dimension_semantics=("arbitrary",)),
        name="peer_gate",
    )(i1.T, i2.T, gate.T)
    gmat = gmat.reshape(n, PEER_EXPERTS)

    tm = min(PEER_DENSE_TOKENS, rows_per_batch)
    te = PEER_DENSE_EXPERTS
    return pl.pallas_call(
        _peer_dense_kernel,
        grid=(n // tm, PEER_EXPERTS // te),
        in_specs=[pl.BlockSpec((tm, d), lambda i, j: (i, 0)),
                  pl.BlockSpec((te, d), lambda i, j: (j, 0)),
                  pl.BlockSpec((te, d), lambda i, j: (j, 0)),
                  pl.BlockSpec((tm, te), lambda i, j: (i, j)),
                  pl.BlockSpec((tm, d), lambda i, j: (i, 0)),
                  pl.BlockSpec((1, 1, d), lambda i, j: (i * tm // rows_per_batch, 0, 0))],
        out_specs=pl.BlockSpec((tm, d), lambda i, j: (i, 0)),
        out_shape=jax.ShapeDtypeStruct((n, d), F32),
        scratch_shapes=[pltpu.VMEM((tm, d), F32)],
        compiler_params=pltpu.CompilerParams(dimension_semantics=("arbitrary", "arbitrary")),
        name="peer_dense",
    )(xn, u_bf, v_bf, gmat, h, gate_mod)


def _final_norm_kernel(x_ref, g_ref, o_ref):
    o_ref[...] = _rms(x_ref[...], g_ref[...])


def _final_norm(h, g):
    n = h.shape[0] * h.shape[1]
    x2 = h.reshape(n, D_MODEL)
    tm = 512
    out = pl.pallas_call(
        _final_norm_kernel,
        grid=(n // tm,),
        in_specs=[pl.BlockSpec((tm, D_MODEL), lambda i: (i, 0)),
                  pl.BlockSpec((1, D_MODEL), lambda i: (0, 0))],
        out_specs=pl.BlockSpec((tm, D_MODEL), lambda i: (i, 0)),
        out_shape=jax.ShapeDtypeStruct((n, D_MODEL), F32),
        name="final_norm",
    )(x2, g.reshape(1, D_MODEL))
    return out.reshape(h.shape)


def _mix_layer(h_lat, h_ctx, mod_l, mod_c, norm_g, w_in, w_out, ssd, gla, mla, s5, ctx_out):
    b, n_lat, d = h_lat.shape
    n_lat_tiles = n_lat // ROW_TILE
    hcomb = jnp.concatenate([h_lat, h_ctx], axis=1)
    tab = lambda k: jnp.concatenate([mod_l[k], mod_c[k]], axis=0)
    p = _inproj(hcomb, norm_g, tab(0), tab(1), _pack_w_in(w_in), n_lat_tiles)
    ssd_y, ssd_xbc = _ssd_mixer(p, ssd["conv_w"], ssd["conv_b"], ssd["a_log"], ssd["dt_bias"], n_lat_tiles)
    gla_o = _gla_mixer(p, gla["gate_w"], gla["gate_b"], n_lat_tiles)
    mla_lat, mla_ctx = _mla_mixer(p, mla["q_norm_g"], mla["w_uq"], mla["kv_norm_g"], mla["w_ukv"], n_lat_tiles, ctx_out)
    s5_y = _s5_mixer(p, s5["a_re"], s5["a_im"], s5["log_dt"], s5["b_re"], s5["b_im"], s5["c_re"], s5["c_im"],
                     n_lat_tiles)
    post = functools.partial(_post, p=p, ssd_xbc=ssd_xbc, ssd_y=ssd_y, gla_o=gla_o, s5_y=s5_y, ssd_d=ssd["d"],
                             ssd_norm_g=ssd["norm_g"], gla_norm_g=gla["norm_g"], s5_d=s5["d"],
                             glu_w=s5["glu_w"], glu_b=s5["glu_b"], w_out=w_out)
    new_lat = post(h_lat, mla_y=mla_lat, mod=mod_l[2], row_off=0, mla_off=0)
    new_ctx = None
    if ctx_out:
        new_ctx = post(h_ctx, mla_y=mla_ctx, mod=mod_c[2], row_off=n_lat_tiles, mla_off=0)
    return new_lat, new_ctx


def kernel(x, c, ctx, c_ctx, ada_w, ada_b, norm_mix_g, norm_ffn_g, w_in, w_out,
           ssd_conv_w, ssd_conv_b, ssd_a_log, ssd_dt_bias, ssd_d, ssd_norm_g,
           gla_gate_w, gla_gate_b, gla_norm_g, mla_q_norm_g, mla_w_uq, mla_kv_norm_g, mla_w_ukv,
           s5_a_re, s5_a_im, s5_log_dt, s5_b_re, s5_b_im, s5_c_re, s5_c_im, s5_d, s5_glu_w, s5_glu_b,
           peer_w_q, peer_sub_keys, peer_u, peer_v, final_norm_g):
    h_lat, h_ctx = x, ctx
    cond_lat = jax.nn.silu(c)[:, None, :]
    cond_ctx = jax.nn.silu(c_ctx)[None, None, :]
    for i in range(DEPTH):
        ctx_out = i < DEPTH - 1
        mod_l = jnp.split(cond_lat @ ada_w[i] + ada_b[i], N_MOD, axis=-1)
        mod_c = jnp.split(cond_ctx @ ada_w[i] + ada_b[i], N_MOD, axis=-1)
        ssd = dict(conv_w=ssd_conv_w[i], conv_b=ssd_conv_b[i], a_log=ssd_a_log[i], dt_bias=ssd_dt_bias[i],
                   d=ssd_d[i], norm_g=ssd_norm_g[i])
        gla = dict(gate_w=gla_gate_w[i], gate_b=gla_gate_b[i], norm_g=gla_norm_g[i])
        mla = dict(q_norm_g=mla_q_norm_g[i], w_uq=mla_w_uq[i], kv_norm_g=mla_kv_norm_g[i], w_ukv=mla_w_ukv[i])
        s5 = dict(a_re=s5_a_re[i], a_im=s5_a_im[i], log_dt=s5_log_dt[i], b_re=s5_b_re[i], b_im=s5_b_im[i],
                  c_re=s5_c_re[i], c_im=s5_c_im[i], d=s5_d[i], glu_w=s5_glu_w[i], glu_b=s5_glu_b[i])
        h_lat, h_ctx_new = _mix_layer(h_lat, h_ctx, mod_l, mod_c, norm_mix_g[i], w_in[i], w_out[i],
                                      ssd, gla, mla, s5, ctx_out)
        wq_t_hi, wq_t_lo = _split_bf16(peer_w_q[i].T)
        keys_hi, keys_lo = _split_bf16(peer_sub_keys[i])
        u_bf, v_bf = peer_u[i].astype(BF16), peer_v[i].astype(BF16)
        peer = functools.partial(_peer_layer, norm_g=norm_ffn_g[i], wq_t_hi=wq_t_hi, wq_t_lo=wq_t_lo,
                                 keys_hi=keys_hi, keys_lo=keys_lo, u_bf=u_bf, v_bf=v_bf)
        h_lat = peer(h_lat.reshape(-1, D_MODEL), shift=mod_l[3], scale=mod_l[4],
                     gate_mod=mod_l[5]).reshape(h_lat.shape)
        if ctx_out:
            h_ctx = peer(h_ctx_new.reshape(-1, D_MODEL), shift=mod_c[3], scale=mod_c[4],
                         gate_mod=mod_c[5]).reshape(h_ctx.shape)
    return _final_norm(h_lat, final_norm_g)
```

```python
import functools
import jax
import jax.numpy as jnp
from jax import lax
import numpy as np
from jax.experimental import pallas as pl
from jax.experimental.pallas import tpu as pltpu

D_MODEL = 1024
DEPTH = 2
GRID_W = 64
NORM_EPS = 1e-6
N_MOD = 6

GROUP_WIDTH = D_MODEL // 4

SSD_WIDTH = GROUP_WIDTH
SSD_HEAD_DIM = 64
SSD_HEADS = SSD_WIDTH // SSD_HEAD_DIM
SSD_GROUPS = 2
SSD_STATE = 128
SSD_CONV = 5
SSD_CHUNK = 128
SSD_CONV_CH = SSD_WIDTH + 2 * SSD_GROUPS * SSD_STATE
SSD_IN = SSD_WIDTH + SSD_CONV_CH + 2 * SSD_HEADS

GLA_WIDTH = GROUP_WIDTH
GLA_HEADS = 4
GLA_DV = GLA_WIDTH // GLA_HEADS
GLA_DK = GLA_DV // 2
GLA_QK = GLA_HEADS * GLA_DK
GLA_GATE_RANK = 16
GLA_TAU = 16.0
GLA_CHUNK = 64
GLA_IN = 2 * GLA_QK + 2 * GLA_WIDTH + 2 * GLA_GATE_RANK

MLA_WIDTH = GROUP_WIDTH
MLA_HEADS = 4
MLA_V = MLA_WIDTH // MLA_HEADS
MLA_NOPE = 64
MLA_ROPE = 32
MLA_Q_RANK = 256
MLA_KV_RANK = 128
MLA_SCALE = (MLA_NOPE + MLA_ROPE) ** -0.5
ROPE_BASE = 10000.0
MLA_IN = MLA_Q_RANK + MLA_KV_RANK + MLA_ROPE

S5_WIDTH = GROUP_WIDTH
S5_GROUP = 16
S5_NGROUPS = S5_WIDTH // S5_GROUP
S5_STATE = 64
S5_MAX_RE = -1e-4
S5_IN = S5_WIDTH
S5_CHUNK = 16
S5_PAIRS = S5_NGROUPS // 2

PEER_KEYS = 128
PEER_EXPERTS = PEER_KEYS * PEER_KEYS
PEER_HEADS = 8
PEER_TOPK = 16
PEER_DQ = 128

LANES = 128
ROW_TILE = 256

F32 = jnp.float32
BF16 = jnp.bfloat16

COL_XS, COL_BM, COL_CM, COL_Z = 0, 256, 512, 768
COL_GLA_V, COL_GLA_R, COL_CQ, COL_S5 = 1024, 1280, 1536, 1792
COL_GLA_Q, COL_GLA_K, COL_CKV, COL_DT, COL_GLR, COL_KR, COL_KRROT = 2048, 2176, 2304, 2432, 2560, 2688, 2816
P_COLS = 2944

_NN_DIMS = (((1,), (0,)), ((), ()))
_NT_DIMS = (((1,), (1,)), ((), ()))
_TN_DIMS = (((0,), (0,)), ((), ()))


def _mm(a, b, dims=_NN_DIMS):
    return lax.dot_general(a, b, dims, preferred_element_type=F32)


def _split_bf16(x):
    hi = x.astype(BF16)
    lo = (x - hi.astype(F32)).astype(BF16)
    return hi, lo


def _split3_bf16(x):
    p1 = x.astype(BF16)
    r1 = x - p1.astype(F32)
    p2 = r1.astype(BF16)
    p3 = (r1 - p2.astype(F32)).astype(BF16)
    return p1, p2, p3


def _dot3(a_hi, a_lo, b_hi, b_lo, dims):
    return _mm(a_hi, b_hi, dims) + _mm(a_hi, b_lo, dims) + _mm(a_lo, b_hi, dims)


def _gelu_erf(x):
    return 0.5 * x * (1.0 + lax.erf(x * (2.0 ** -0.5)))


def _silu(x):
    return x * jax.nn.sigmoid(x)


def _softplus(x):
    return jnp.maximum(x, 0.0) + jnp.log1p(jnp.exp(-jnp.abs(x)))


def _log_sigmoid(x):
    return jnp.minimum(x, 0.0) - jnp.log1p(jnp.exp(-jnp.abs(x)))


def _rms(x, g):
    return x * lax.rsqrt(jnp.mean(x * x, axis=-1, keepdims=True) + NORM_EPS) * g


def _modulated_norm(x, g, shift, scale):
    return _rms(x, g) * (1.0 + scale) + shift


def _causal_mask(n, reverse):
    ri = lax.broadcasted_iota(jnp.int32, (n, n), 0)
    ci = lax.broadcasted_iota(jnp.int32, (n, n), 1)
    return (ci >= ri) if reverse else (ci <= ri)


def _scan_chunk(s, n_lat, n_ctx, reverse):
    if reverse:
        return n_lat + n_ctx - 1 - s
    return jnp.where(s < n_ctx, n_lat + s, s - n_ctx)


def _inproj_kernel(h_ref, g_ref, shift_ref, scale_ref, w_ref, o_ref):
    xn = _modulated_norm(h_ref[0], g_ref[...], shift_ref[0], scale_ref[0])
    o_ref[0] = _mm(xn.astype(BF16), w_ref[...])


def _inproj(hcomb, norm_g, shift_tab, scale_tab, w_pad, n_lat_tiles):
    b, r, d = hcomb.shape
    mod_spec = pl.BlockSpec((1, 1, d), lambda i, t: (jnp.where(t < n_lat_tiles, i, b), 0, 0))
    return pl.pallas_call(
        _inproj_kernel,
        grid=(b, r // ROW_TILE),
        in_specs=[pl.BlockSpec((1, ROW_TILE, d), lambda i, t: (i, t, 0)),
                  pl.BlockSpec((1, d), lambda i, t: (0, 0)), mod_spec, mod_spec,
                  pl.BlockSpec((d, P_COLS), lambda i, t: (0, 0))],
        out_specs=pl.BlockSpec((1, ROW_TILE, P_COLS), lambda i, t: (i, t, 0)),
        out_shape=jax.ShapeDtypeStruct((b, r, P_COLS), F32),
        compiler_params=pltpu.CompilerParams(dimension_semantics=("arbitrary", "arbitrary"),
                                             vmem_limit_bytes=48 * 2 ** 20),
        name="inproj",
    )(hcomb, norm_g.reshape(1, d), shift_tab, scale_tab, w_pad)


def _pack_w_in(w):
    o_ssd, o_gla, o_mla, o_s5 = 0, SSD_IN, SSD_IN + GLA_IN, SSD_IN + GLA_IN + MLA_IN
    out = jnp.zeros((w.shape[0], P_COLS), F32)
    put = lambda out, col, src, width: out.at[:, col:col + width].set(w[:, src:src + width])
    out = put(out, COL_Z, o_ssd, SSD_WIDTH)
    out = put(out, COL_XS, o_ssd + SSD_WIDTH, SSD_CONV_CH)
    out = put(out, COL_DT, o_ssd + SSD_WIDTH + SSD_CONV_CH, 2 * SSD_HEADS)
    out = put(out, COL_GLA_Q, o_gla, GLA_QK)
    out = put(out, COL_GLA_K, o_gla + GLA_QK, GLA_QK)
    out = put(out, COL_GLA_V, o_gla + 2 * GLA_QK, GLA_WIDTH)
    out = put(out, COL_GLA_R, o_gla + 2 * GLA_QK + GLA_WIDTH, GLA_WIDTH)
    out = put(out, COL_GLR, o_gla + 2 * GLA_QK + 2 * GLA_WIDTH, 2 * GLA_GATE_RANK)
    out = put(out, COL_CQ, o_mla, MLA_Q_RANK)
    out = put(out, COL_CKV, o_mla + MLA_Q_RANK, MLA_KV_RANK)
    o_kr = o_mla + MLA_Q_RANK + MLA_KV_RANK
    half = MLA_ROPE // 2
    out = put(out, COL_KR + MLA_NOPE, o_kr, MLA_ROPE)
    out = out.at[:, COL_KRROT + MLA_NOPE:COL_KRROT + MLA_NOPE + half].set(-w[:, o_kr + half:o_kr + MLA_ROPE])
    out = out.at[:, COL_KRROT + MLA_NOPE + half:COL_KRROT + MLA_NOPE + MLA_ROPE].set(w[:, o_kr:o_kr + half])
    out = put(out, COL_S5, o_s5, S5_WIDTH)
    return out.astype(BF16)


def _ssd_prep_kernel(x_ref, prev_ref, next_ref, dt_ref, w_ref, b_ref, bias_ref, xbc_ref, dtc_ref, dtt_ref,
                     *, n_lat_tiles):
    t = pl.program_id(1)
    x = x_ref[0]
    halo = prev_ref.shape[1]
    prev = jnp.where(jnp.logical_and(t > 0, t < n_lat_tiles), prev_ref[0], 0.0)
    nxt = jnp.where(t < n_lat_tiles - 1, next_ref[0], 0.0)
    ext = jnp.concatenate([prev, x, nxt], axis=0)
    rows = ext.shape[0]
    left = SSD_CONV // 2
    acc = jnp.zeros_like(x) + b_ref[...]
    for k in range(SSD_CONV):
        shifted = ext if k == left else pltpu.roll(ext, (left - k) % rows, 0)
        acc = acc + w_ref[k:k + 1, :] * shifted[halo:halo + x.shape[0]]
    xbc_ref[0] = _silu(acc)
    dt = _softplus(dt_ref[0] + bias_ref[...])
    dtc_ref[0] = dt
    dtt_ref[0] = dt.T[:dtt_ref.shape[1]]


def _ssd_scan_kernel(*refs, direction, n_lat, n_ctx, has_prev):
    if has_prev:
        xbc_ref, dtc_ref, dtt_ref, ahr_ref, ahc_ref, yprev_ref, y_ref, state_ref = refs
    else:
        xbc_ref, dtc_ref, dtt_ref, ahr_ref, ahc_ref, y_ref, state_ref = refs
    reverse = direction == 1
    s = pl.program_id(1)

    @pl.when(s == 0)
    def _():
        state_ref[...] = jnp.zeros_like(state_ref)

    q = SSD_CHUNK
    mask = _causal_mask(q, reverse)
    tri = jnp.where(mask, 1.0, 0.0).astype(BF16)
    xbc = xbc_ref[0]
    xs, bm, cm = xbc[:, :SSD_WIDTH], xbc[:, SSD_WIDTH:SSD_WIDTH + 256], xbc[:, SSD_WIDTH + 256:]
    dtc = dtc_ref[0]
    a_col = dtc * ahr_ref[...]
    a_row = dtt_ref[0] * ahc_ref[...]
    acum_col = sum(_mm(tri, part) for part in _split3_bf16(a_col))
    acum_row = sum(_mm(part, tri, _NT_DIMS) for part in _split3_bf16(a_row))
    end = 0 if reverse else q - 1
    bm_bf, cm_bf = bm.astype(BF16), cm.astype(BF16)
    ys = []
    cb = {}
    for h in range(SSD_HEADS):
        g = h // (SSD_HEADS // SSD_GROUPS)
        gs = slice(g * SSD_STATE, (g + 1) * SSD_STATE)
        if g not in cb:
            cb[g] = _mm(cm_bf[:, gs], bm_bf[:, gs], _NT_DIMS)
        ch = direction * SSD_HEADS + h
        ac = acum_col[:, ch:ch + 1]
        ar = acum_row[ch:ch + 1, :]
        decay = jnp.exp(jnp.where(mask, ac - ar, -jnp.inf))
        xd = xs[:, h * SSD_HEAD_DIM:(h + 1) * SSD_HEAD_DIM] * dtc[:, ch:ch + 1]
        y_diag = _mm((cb[g] * decay).astype(BF16), xd.astype(BF16))
        a_end = ac[end:end + 1, :]
        st_local = _mm((xd * jnp.exp(a_end - ac)).astype(BF16), bm_bf[:, gs], _TN_DIMS)
        hs = state_ref[h]
        y_off = jnp.exp(ac) * _mm(cm_bf[:, gs], hs.astype(BF16), _NT_DIMS)
        state_ref[h] = jnp.exp(a_end) * hs + st_local
        ys.append(y_diag + y_off)
    y = jnp.concatenate(ys, axis=1)
    if has_prev:
        y = y + yprev_ref[0]
    y_ref[0] = y


def _ssd_mixer(p, conv_w, conv_b, a_log, dt_bias, n_lat_tiles):
    b, r, _ = p.shape
    nt = r // ROW_TILE
    halo = 8
    hb = ROW_TILE // halo
    w8 = jnp.zeros((8, SSD_CONV_CH), F32).at[:SSD_CONV].set(conv_w)
    bias = jnp.zeros((1, LANES), F32).at[0, :2 * SSD_HEADS].set(dt_bias.reshape(-1))
    xbc, dtc, dtt = pl.pallas_call(
        functools.partial(_ssd_prep_kernel, n_lat_tiles=n_lat_tiles),
        grid=(b, nt),
        in_specs=[pl.BlockSpec((1, ROW_TILE, SSD_CONV_CH), lambda i, t: (i, t, 0)),
                  pl.BlockSpec((1, halo, SSD_CONV_CH), lambda i, t: (i, jnp.maximum(t * hb - 1, 0), 0)),
                  pl.BlockSpec((1, halo, SSD_CONV_CH), lambda i, t: (i, jnp.minimum((t + 1) * hb, nt * hb - 1), 0)),
                  pl.BlockSpec((1, ROW_TILE, LANES), lambda i, t: (i, t, COL_DT // LANES)),
                  pl.BlockSpec((8, SSD_CONV_CH), lambda i, t: (0, 0)),
                  pl.BlockSpec((1, SSD_CONV_CH), lambda i, t: (0, 0)),
                  pl.BlockSpec((1, LANES), lambda i, t: (0, 0))],
        out_specs=[pl.BlockSpec((1, ROW_TILE, SSD_CONV_CH), lambda i, t: (i, t, 0)),
                   pl.BlockSpec((1, ROW_TILE, LANES), lambda i, t: (i, t, 0)),
                   pl.BlockSpec((1, 8, ROW_TILE), lambda i, t: (i, 0, t))],
        out_shape=[jax.ShapeDtypeStruct((b, r, SSD_CONV_CH), F32),
                   jax.ShapeDtypeStruct((b, r, LANES), F32),
                   jax.ShapeDtypeStruct((b, 8, r), F32)],
        compiler_params=pltpu.CompilerParams(dimension_semantics=("arbitrary", "arbitrary")),
        name="ssd_prep",
    )(p, p, p, p, w8, conv_b.reshape(1, -1), bias)

    a_head = -jnp.exp(a_log.astype(F32)).reshape(-1)
    ahr = jnp.zeros((1, LANES), F32).at[0, :2 * SSD_HEADS].set(a_head)
    ahc = a_head.reshape(2 * SSD_HEADS, 1)
    n_lat = n_lat_tiles * ROW_TILE // SSD_CHUNK
    n_ctx = r // SSD_CHUNK - n_lat
    y = None
    for direction in (0, 1):
        cidx = functools.partial(_scan_chunk, n_lat=n_lat, n_ctx=n_ctx, reverse=direction == 1)
        in_specs = [pl.BlockSpec((1, SSD_CHUNK, SSD_CONV_CH), lambda i, s: (i, cidx(s), 0)),
                    pl.BlockSpec((1, SSD_CHUNK, LANES), lambda i, s: (i, cidx(s), 0)),
                    pl.BlockSpec((1, 8, SSD_CHUNK), lambda i, s: (i, 0, cidx(s))),
                    pl.BlockSpec((1, LANES), lambda i, s: (0, 0)),
                    pl.BlockSpec((2 * SSD_HEADS, 1), lambda i, s: (0, 0))]
        args = [xbc, dtc, dtt, ahr, ahc]
        y_spec = pl.BlockSpec((1, SSD_CHUNK, SSD_WIDTH), lambda i, s: (i, cidx(s), 0))
        if y is not None:
            in_specs.append(y_spec)
            args.append(y)
        y = pl.pallas_call(
            functools.partial(_ssd_scan_kernel, direction=direction, n_lat=n_lat, n_ctx=n_ctx,
                              has_prev=y is not None),
            grid=(b, n_lat + n_ctx),
            in_specs=in_specs,
            out_specs=y_spec,
            out_shape=jax.ShapeDtypeStruct((b, r, SSD_WIDTH), F32),
            scratch_shapes=[pltpu.VMEM((SSD_HEADS, SSD_HEAD_DIM, SSD_STATE), F32)],
            compiler_params=pltpu.CompilerParams(dimension_semantics=("arbitrary", "arbitrary")),
            name=f"ssd_scan_{direction}",
        )(*args)
    return y, xbc


def _gla_scan_kernel(*refs, direction, has_prev):
    if has_prev:
        q_ref, k_ref, v_ref, glr_ref, wg_ref, bias_ref, oprev_ref, o_ref, st_ref = refs
    else:
        q_ref, k_ref, v_ref, glr_ref, wg_ref, bias_ref, o_ref, st_ref = refs
    reverse = direction == 1
    s = pl.program_id(1)

    @pl.when(s == 0)
    def _():
        st_ref[...] = jnp.zeros_like(st_ref)

    n = GLA_CHUNK
    mask = _causal_mask(n, reverse)
    tri = jnp.where(mask, 1.0, 0.0).astype(BF16)
    g_hi, g_lo = _split_bf16(glr_ref[0])
    logits = _dot3(g_hi, g_lo, wg_ref[0], wg_ref[1], _NN_DIMS) + bias_ref[...]
    logg = _log_sigmoid(logits) * (1.0 / GLA_TAU)
    bcum = sum(_mm(tri, part) for part in _split3_bf16(logg))
    end = 0 if reverse else n - 1
    b_end = bcum[end:end + 1, :]
    q, k, v = q_ref[0], k_ref[0], v_ref[0]
    qe = q * jnp.exp(bcum) * (GLA_DK ** -0.5)
    ke = (k * jnp.exp(-bcum)).astype(BF16)
    kd = k * jnp.exp(b_end - bcum)
    decay_end = jnp.exp(b_end)
    lane_head = lax.broadcasted_iota(jnp.int32, (1, GLA_QK), 1) >> (GLA_DK.bit_length() - 1)
    outs = []
    for h in range(GLA_HEADS):
        hm = lane_head == h
        qh = jnp.where(hm, qe, 0.0).astype(BF16)
        att = jnp.where(mask, _mm(qh, ke, _NT_DIMS), 0.0)
        vh = v[:, h * GLA_DV:(h + 1) * GLA_DV].astype(BF16)
        st = st_ref[h]
        o_h = _mm(att.astype(BF16), vh) + _mm(qh, st.astype(BF16), _NT_DIMS)
        local = _mm(vh, jnp.where(hm, kd, 0.0).astype(BF16), _TN_DIMS)
        st_ref[h] = st * decay_end + local
        outs.append(o_h)
    o = jnp.concatenate(outs, axis=1)
    if has_prev:
        o = o + oprev_ref[0]
    o_ref[0] = o


def _gla_mixer(p, gate_w, gate_b, n_lat_tiles):
    b, r, _ = p.shape
    n_lat = n_lat_tiles * ROW_TILE // GLA_CHUNK
    n_ctx = r // GLA_CHUNK - n_lat
    o = None
    for direction in (0, 1):
        cidx = functools.partial(_scan_chunk, n_lat=n_lat, n_ctx=n_ctx, reverse=direction == 1)
        wg = jnp.zeros((LANES, GLA_QK), F32).at[direction * GLA_GATE_RANK:(direction + 1) * GLA_GATE_RANK].set(
            gate_w[direction])
        wg = jnp.stack(_split_bf16(wg))
        blk = lambda width, col: pl.BlockSpec((1, GLA_CHUNK, width), lambda i, s: (i, cidx(s), col // width))
        in_specs = [blk(GLA_QK, COL_GLA_Q), blk(GLA_QK, COL_GLA_K), blk(GLA_WIDTH, COL_GLA_V), blk(LANES, COL_GLR),
                    pl.BlockSpec((2, LANES, GLA_QK), lambda i, s: (0, 0, 0)),
                    pl.BlockSpec((1, GLA_QK), lambda i, s: (0, 0))]
        args = [p, p, p, p, wg, gate_b[direction].reshape(1, -1)]
        o_spec = pl.BlockSpec((1, GLA_CHUNK, GLA_WIDTH), lambda i, s: (i, cidx(s), 0))
        if o is not None:
            in_specs.append(o_spec)
            args.append(o)
        o = pl.pallas_call(
            functools.partial(_gla_scan_kernel, direction=direction, has_prev=o is not None),
            grid=(b, n_lat + n_ctx),
            in_specs=in_specs,
            out_specs=o_spec,
            out_shape=jax.ShapeDtypeStruct((b, r, GLA_WIDTH), F32),
            scratch_shapes=[pltpu.VMEM((GLA_HEADS, GLA_DV, GLA_QK), F32)],
            compiler_params=pltpu.CompilerParams(dimension_semantics=("arbitrary", "arbitrary")),
            name=f"gla_scan_{direction}",
        )(*args)
    return o


MLA_Q_TILE = 1024
MLA_K_TILE = 256


def _mla_prep_kernel(cq_ref, ckv_ref, kr_ref, krrot_ref, onec_ref, sinr_ref, gq_ref, gkv_ref,
                     wq_ref, wqr_ref, wk_ref, wv_ref, q_ref, k_ref, v_ref):
    qn = _rms(cq_ref[0], gq_ref[...]).astype(BF16)
    kvn = _rms(ckv_ref[0], gkv_ref[...]).astype(BF16)
    onec, sinr = onec_ref[...], sinr_ref[...]
    k_rope = kr_ref[0] * onec + krrot_ref[0] * sinr
    ones_lane = jnp.where(lax.broadcasted_iota(jnp.int32, (1, LANES), 1) == MLA_V, 1.0, 0.0)
    for h in range(MLA_HEADS):
        qh = _mm(qn, wq_ref[h]) * onec + _mm(qn, wqr_ref[h]) * sinr
        q_ref[0, h] = (qh * MLA_SCALE).astype(BF16)
        k_ref[0, h] = (_mm(kvn, wk_ref[h]) + k_rope).astype(BF16)
        v_ref[0, h] = (_mm(kvn, wv_ref[h]) + ones_lane).astype(BF16)


def _mla_attn_kernel(q_ref, k_ref, v_ref, o_ref, m_ref, acc_ref):
    j = pl.program_id(2)

    @pl.when(j == 0)
    def _():
        m_ref[...] = jnp.full_like(m_ref, -jnp.inf)
        acc_ref[...] = jnp.zeros_like(acc_ref)

    reps = k_ref.shape[2] // LANES
    for h in range(MLA_HEADS):
        s = _mm(q_ref[0, h], k_ref[0, h], _NT_DIMS)
        m_prev = m_ref[h]
        m_new = jnp.maximum(m_prev, jnp.max(s, axis=1, keepdims=True))
        p = jnp.exp(s - jnp.concatenate([m_new] * reps, axis=1))
        acc_ref[h] = jnp.exp(m_prev - m_new) * acc_ref[h] + _mm(p.astype(BF16), v_ref[0, h])
        m_ref[h] = m_new

    @pl.when(j == pl.num_programs(2) - 1)
    def _():
        outs = []
        for h in range(MLA_HEADS):
            acc = acc_ref[h]
            outs.append(acc[:, :MLA_V] / acc[:, MLA_V:MLA_V + 1])
        o_ref[0] = jnp.concatenate(outs, axis=1)


def _rope_tables(n_lat, n_rows):
    rows = n_lat // GRID_W
    row = jnp.repeat(jnp.arange(rows, dtype=F32), GRID_W)
    col = jnp.tile(jnp.arange(GRID_W, dtype=F32), rows)
    half = MLA_ROPE // 2
    inv = ROPE_BASE ** (-jnp.arange(0, half, 2, dtype=F32) / half)
    ang = jnp.concatenate([row[:, None] * inv, col[:, None] * inv], axis=-1)
    cos = jnp.concatenate([jnp.cos(ang), jnp.ones((n_rows - n_lat, half), F32)], axis=0)
    sin = jnp.concatenate([jnp.sin(ang), jnp.zeros((n_rows - n_lat, half), F32)], axis=0)
    pad = jnp.zeros((n_rows, LANES - MLA_NOPE - MLA_ROPE), F32)
    onec = jnp.concatenate([jnp.ones((n_rows, MLA_NOPE), F32), cos, cos, pad], axis=1)
    sinr = jnp.concatenate([jnp.zeros((n_rows, MLA_NOPE), F32), sin, sin, pad], axis=1)
    return onec, sinr


def _mla_weights(w_uq, w_ukv):
    dqk = MLA_NOPE + MLA_ROPE
    half = MLA_ROPE // 2
    wq = w_uq.reshape(MLA_Q_RANK, MLA_HEADS, dqk).transpose(1, 0, 2)
    rot = jnp.concatenate([jnp.zeros_like(wq[..., :MLA_NOPE]), -wq[..., MLA_NOPE + half:], wq[..., MLA_NOPE:MLA_NOPE + half]],
                          axis=-1)
    padq = lambda w: jnp.pad(w, ((0, 0), (0, 0), (0, LANES - dqk))).astype(BF16)
    wkv = w_ukv.reshape(MLA_KV_RANK, MLA_HEADS, MLA_NOPE + MLA_V).transpose(1, 0, 2)
    padk = lambda w: jnp.pad(w, ((0, 0), (0, 0), (0, LANES - w.shape[-1]))).astype(BF16)
    return padq(wq), padq(rot), padk(wkv[..., :MLA_NOPE]), padk(wkv[..., MLA_NOPE:])


def _mla_attention(q, k, v, q_tile, q_off, n_q, k_off, n_k):
    b = q.shape[0]
    kt = MLA_K_TILE
    return pl.pallas_call(
        _mla_attn_kernel,
        grid=(b, n_q, n_k),
        in_specs=[pl.BlockSpec((1, MLA_HEADS, q_tile, LANES), lambda i, a, j: (i, 0, q_off + a, 0)),
                  pl.BlockSpec((1, MLA_HEADS, kt, LANES), lambda i, a, j: (i, 0, k_off + j, 0)),
                  pl.BlockSpec((1, MLA_HEADS, kt, LANES), lambda i, a, j: (i, 0, k_off + j, 0))],
        out_specs=pl.BlockSpec((1, q_tile, MLA_WIDTH), lambda i, a, j: (i, a, 0)),
        out_shape=jax.ShapeDtypeStruct((b, n_q * q_tile, MLA_WIDTH), F32),
        scratch_shapes=[pltpu.VMEM((MLA_HEADS, q_tile, LANES), F32), pltpu.VMEM((MLA_HEADS, q_tile, LANES), F32)],
        compiler_params=pltpu.CompilerParams(dimension_semantics=("arbitrary", "arbitrary", "arbitrary")),
        name="mla_attn",
    )(q, k, v)


def _mla_mixer(p, q_norm_g, w_uq, kv_norm_g, w_ukv, n_lat_tiles, ctx_out):
    b, r, _ = p.shape
    nt = r // ROW_TILE
    n_lat = n_lat_tiles * ROW_TILE
    onec, sinr = _rope_tables(n_lat, r)
    wq, wqr, wk, wv = _mla_weights(w_uq, w_ukv)
    blk = lambda width, col: pl.BlockSpec((1, ROW_TILE, width), lambda i, t: (i, t, col // width))
    tab = pl.BlockSpec((ROW_TILE, LANES), lambda i, t: (t, 0))
    full = lambda *shape: pl.BlockSpec(shape, lambda i, t: (0,) * len(shape))
    head_out = pl.BlockSpec((1, MLA_HEADS, ROW_TILE, LANES), lambda i, t: (i, 0, t, 0))
    q, k, v = pl.pallas_call(
        _mla_prep_kernel,
        grid=(b, nt),
        in_specs=[blk(MLA_Q_RANK, COL_CQ), blk(LANES, COL_CKV), blk(LANES, COL_KR), blk(LANES, COL_KRROT), tab, tab,
                  full(1, MLA_Q_RANK), full(1, MLA_KV_RANK),
                  full(MLA_HEADS, MLA_Q_RANK, LANES), full(MLA_HEADS, MLA_Q_RANK, LANES),
                  full(MLA_HEADS, MLA_KV_RANK, LANES), full(MLA_HEADS, MLA_KV_RANK, LANES)],
        out_specs=[head_out] * 3,
        out_shape=[jax.ShapeDtypeStruct((b, MLA_HEADS, r, LANES), BF16)] * 3,
        compiler_params=pltpu.CompilerParams(dimension_semantics=("arbitrary", "arbitrary")),
        name="mla_prep",
    )(p, p, p, p, onec, sinr, q_norm_g.reshape(1, -1), kv_norm_g.reshape(1, -1), wq, wqr, wk, wv)
    q_tile = min(MLA_Q_TILE, n_lat)
    y_lat = _mla_attention(q, k, v, q_tile, 0, n_lat // q_tile, 0, r // MLA_K_TILE)
    y_ctx = None
    if ctx_out:
        n_ctx = r - n_lat
        y_ctx = _mla_attention(q, k, v, n_ctx, n_lat // n_ctx, 1, n_lat // MLA_K_TILE, n_ctx // MLA_K_TILE)
    return y_lat, y_ctx


def _s5_matrices(a_re, a_im, log_dt, b_re, b_im, c_re, c_im):
    q, ng, ns, nc = S5_CHUNK, S5_NGROUPS, S5_STATE, S5_GROUP
    lam = jnp.minimum(a_re.astype(F32), S5_MAX_RE) + 1j * a_im.astype(F32)
    step = jnp.exp(log_dt.astype(F32))[..., None]
    abar = jnp.exp(lam * step)
    bmat = b_re.astype(F32) + 1j * b_im.astype(F32)
    bbar = ((abar - 1.0) / lam)[..., None] * bmat
    cmat = c_re.astype(F32) + 1j * c_im.astype(F32)
    pw = jnp.exp((lam * step)[..., None] * jnp.arange(q + 1, dtype=F32))
    kern = jnp.einsum('dgcn,dgnl,dgnk->dglck', cmat, pw[..., :q], bbar).real
    ii = jnp.arange(q)
    lag_f = ii[None, :] - ii[:, None]
    gather = lambda kd, lag: jnp.where((lag >= 0)[None, :, :, None, None], kd[:, jnp.clip(lag, 0, q - 1)], 0.0)
    t_f = gather(kern[0], lag_f).transpose(0, 1, 4, 2, 3)
    t_b = gather(kern[1], -lag_f).transpose(0, 1, 4, 2, 3)
    t_sum = (t_f + t_b).reshape(ng, q * nc, q * nc)
    pw_f = pw[0][..., q - 1 - ii]
    pw_b = pw[1][..., ii]
    wst = lambda pwd, bb: jnp.einsum('gnj,gnc->gjcn', pwd, bb).reshape(ng, q * nc, ns)
    wst_f, wst_b = wst(pw_f, bbar[0]), wst(pw_b, bbar[1])
    wout = lambda pwd, cm: jnp.einsum('gcn,gni->gnic', cm, pwd).reshape(ng, ns, q * nc)
    wo_f, wo_b = wout(pw[0][..., ii + 1], cmat[0]), wout(pw[1][..., q - ii], cmat[1])
    aq = pw[..., q]

    def pair_cols(x):
        x = x.reshape(S5_PAIRS, 2, x.shape[1], x.shape[2])
        z = jnp.zeros_like(x[:, 0])
        return jnp.concatenate([jnp.concatenate([x[:, 0], z], axis=2), jnp.concatenate([z, x[:, 1]], axis=2)], axis=1)

    w_local = jnp.concatenate([pair_cols(wst_f.real), pair_cols(wst_f.imag),
                               pair_cols(wst_b.real), pair_cols(wst_b.imag)], axis=2)
    w_out = jnp.concatenate([pair_cols(t_sum), pair_cols(wo_f.real), pair_cols(-wo_f.imag),
                             pair_cols(wo_b.real), pair_cols(-wo_b.imag)], axis=1)
    aq_pair = aq.reshape(2, S5_PAIRS, 2 * ns)
    aq_tab = jnp.concatenate([aq_pair[0].real, aq_pair[0].imag, aq_pair[1].real, aq_pair[1].imag], axis=1)
    return w_local.astype(BF16), w_out.astype(BF16), aq_tab.reshape(S5_PAIRS, 1, 8 * ns).astype(F32)


def _s5_local_kernel(u_ref, w_ref, s_ref):
    s_ref[...] = _mm(u_ref[0], w_ref[0])


def _s5_scan_kernel(s_ref, aq_ref, hs_ref, *, n_lat, n_ctx, nb):
    w = 2 * S5_STATE
    aq = aq_ref[0]
    a = [aq[:, i * w:(i + 1) * w] for i in range(4)]
    zero = jnp.zeros((nb, w), F32)
    slab = 8
    cps = slab // nb

    def run_slab(s_re, s_im, a_re, a_im, h_re, h_im, order):
        ent_re, ent_im = [None] * cps, [None] * cps
        for c in order:
            ent_re[c], ent_im[c] = h_re, h_im
            rows = slice(c * nb, (c + 1) * nb)
            h_re, h_im = a_re * h_re - a_im * h_im + s_re[rows], a_re * h_im + a_im * h_re + s_im[rows]
        return jnp.concatenate(ent_re, axis=0), jnp.concatenate(ent_im, axis=0), h_re, h_im

    def body(kk, carry):
        f_re, f_im, b_re, b_im = carry
        rf = pl.multiple_of(_scan_chunk(kk, n_lat // cps, n_ctx // cps, False) * slab, slab)
        rb = pl.multiple_of(_scan_chunk(kk, n_lat // cps, n_ctx // cps, True) * slab, slab)
        e_re, e_im, f_re, f_im = run_slab(s_ref[pl.ds(rf, slab), 0:w], s_ref[pl.ds(rf, slab), w:2 * w],
                                          a[0], a[1], f_re, f_im, range(cps))
        hs_ref[pl.ds(rf, slab), 0:w] = e_re
        hs_ref[pl.ds(rf, slab), w:2 * w] = e_im
        e_re, e_im, b_re, b_im = run_slab(s_ref[pl.ds(rb, slab), 2 * w:3 * w], s_ref[pl.ds(rb, slab), 3 * w:4 * w],
                                          a[2], a[3], b_re, b_im, range(cps - 1, -1, -1))
        hs_ref[pl.ds(rb, slab), 2 * w:3 * w] = e_re
        hs_ref[pl.ds(rb, slab), 3 * w:4 * w] = e_im
        return f_re, f_im, b_re, b_im

    lax.fori_loop(0, (n_lat + n_ctx) // cps, body, (zero, zero, zero, zero))


def _s5_out_kernel(u_ref, hs_ref, w_ref, y_ref):
    k_u = u_ref.shape[2]
    y_ref[0] = _mm(u_ref[0], w_ref[0, :k_u]) + _mm(hs_ref[...].astype(BF16), w_ref[0, k_u:])


def _s5_mixer(p, a_re, a_im, log_dt, b_re, b_im, c_re, c_im, n_lat_tiles):
    b, r, _ = p.shape
    q, nc = S5_CHUNK, S5_GROUP
    n_chunks = r // q
    m = n_chunks * b
    w_local, w_out, aq_tab = _s5_matrices(a_re, a_im, log_dt, b_re, b_im, c_re, c_im)
    u = p[:, :, COL_S5:COL_S5 + S5_WIDTH]
    u_pairs = u.reshape(b, n_chunks, q, S5_PAIRS, 2, nc).transpose(3, 1, 0, 4, 2, 5).reshape(S5_PAIRS, m, 2 * q * nc)
    u_pairs = u_pairs.astype(BF16)
    kw = 2 * q * nc
    cp = pltpu.CompilerParams(dimension_semantics=("arbitrary",))
    s_loc = pl.pallas_call(
        _s5_local_kernel,
        grid=(S5_PAIRS,),
        in_specs=[pl.BlockSpec((1, m, kw), lambda g: (g, 0, 0)), pl.BlockSpec((1, kw, kw), lambda g: (g, 0, 0))],
        out_specs=pl.BlockSpec((m, kw), lambda g: (0, g)),
        out_shape=jax.ShapeDtypeStruct((m, S5_PAIRS * kw), F32),
        compiler_params=cp, name="s5_local",
    )(u_pairs, w_local)
    n_lat = n_lat_tiles * ROW_TILE // q
    hs = pl.pallas_call(
        functools.partial(_s5_scan_kernel, n_lat=n_lat, n_ctx=n_chunks - n_lat, nb=b),
        grid=(S5_PAIRS,),
        in_specs=[pl.BlockSpec((m, kw), lambda g: (0, g)), pl.BlockSpec((1, 1, kw), lambda g: (g, 0, 0))],
        out_specs=pl.BlockSpec((m, kw), lambda g: (0, g)),
        out_shape=jax.ShapeDtypeStruct((m, S5_PAIRS * kw), F32),
        compiler_params=cp, name="s5_scan",
    )(s_loc, aq_tab)
    y_pairs = pl.pallas_call(
        _s5_out_kernel,
        grid=(S5_PAIRS,),
        in_specs=[pl.BlockSpec((1, m, kw), lambda g: (g, 0, 0)), pl.BlockSpec((m, kw), lambda g: (0, g)),
                  pl.BlockSpec((1, 2 * kw, kw), lambda g: (g, 0, 0))],
        out_specs=pl.BlockSpec((1, m, kw), lambda g: (g, 0, 0)),
        out_shape=jax.ShapeDtypeStruct((S5_PAIRS, m, kw), F32),
        compiler_params=cp, name="s5_out",
    )(u_pairs, hs, w_out)
    y = y_pairs.reshape(S5_PAIRS, n_chunks, b, 2, q, nc).transpose(2, 1, 4, 0, 3, 5)
    return y.reshape(b, r, S5_WIDTH)


def _post_kernel(h_ref, xs_ref, z_ref, r_ref, u_ref, ssd_ref, gla_ref, mla_ref, s5_ref,
                 ssd_d_ref, ssd_g_ref, gla_g_ref, s5_d_ref, glu_w_ref, glu_b_ref, w_out_ref, mod_ref, o_ref):
    y = ssd_ref[0] + ssd_d_ref[...] * xs_ref[0]
    ssd = _rms(y * _silu(z_ref[0]), ssd_g_ref[...])
    o = gla_ref[0]
    lane_head = lax.broadcasted_iota(jnp.int32, (1, GLA_WIDTH), 1) >> (GLA_DV.bit_length() - 1)
    ms = jnp.zeros_like(o)
    for h in range(GLA_HEADS):
        oh = o[:, h * GLA_DV:(h + 1) * GLA_DV]
        ms = jnp.where(lane_head == h, jnp.mean(oh * oh, axis=-1, keepdims=True), ms)
    gla = o * lax.rsqrt(ms + NORM_EPS) * gla_g_ref[...] * _silu(r_ref[0])
    y5 = _gelu_erf(s5_ref[0] + s5_d_ref[...] * u_ref[0])
    s5 = y5 * jax.nn.sigmoid(_mm(y5.astype(BF16), glu_w_ref[...]) + glu_b_ref[...])
    mix_in = jnp.concatenate([ssd, gla, mla_ref[0], s5], axis=1).astype(BF16)
    o_ref[0] = h_ref[0] + mod_ref[0] * _mm(mix_in, w_out_ref[...])


def _post(h, p, ssd_xbc, ssd_y, gla_o, mla_y, s5_y, ssd_d, ssd_norm_g, gla_norm_g, s5_d, glu_w, glu_b, w_out, mod,
          row_off, mla_off):
    b, rows, d = h.shape
    w = GROUP_WIDTH
    pblk = lambda col: pl.BlockSpec((1, ROW_TILE, w), lambda i, t: (i, row_off + t, col // w))
    yblk = pl.BlockSpec((1, ROW_TILE, w), lambda i, t: (i, row_off + t, 0))
    full = lambda *shape: pl.BlockSpec(shape, lambda i, t: (0,) * len(shape))
    vec = lambda x: x.reshape(1, -1).astype(F32)
    n_mod = mod.shape[0]
    return pl.pallas_call(
        _post_kernel,
        grid=(b, rows // ROW_TILE),
        in_specs=[pl.BlockSpec((1, ROW_TILE, d), lambda i, t: (i, t, 0)),
                  yblk, pblk(COL_Z), pblk(COL_GLA_R), pblk(COL_S5), yblk, yblk,
                  pl.BlockSpec((1, ROW_TILE, w), lambda i, t: (i, mla_off + t, 0)), yblk,
                  full(1, w), full(1, w), full(1, w), full(1, w), full(w, w), full(1, w), full(d, d),
                  pl.BlockSpec((1, 1, d), lambda i, t: (jnp.minimum(i, n_mod - 1), 0, 0))],
        out_specs=pl.BlockSpec((1, ROW_TILE, d), lambda i, t: (i, t, 0)),
        out_shape=jax.ShapeDtypeStruct((b, rows, d), F32),
        compiler_params=pltpu.CompilerParams(dimension_semantics=("arbitrary", "arbitrary")),
        name="mix_post",
    )(h, ssd_xbc, p, p, p, ssd_y, gla_o, mla_y, s5_y,
      vec(jnp.repeat(ssd_d, SSD_HEAD_DIM)), vec(ssd_norm_g), vec(jnp.tile(gla_norm_g, GLA_HEADS)), vec(s5_d),
      glu_w.astype(BF16), vec(glu_b), w_out.astype(BF16), mod)


PEER_ROUTE_TOKENS = 256
PEER_GATE_TOKENS = 256
PEER_GATE_UNROLL = 8
PEER_GATE_SUBLANES = 8
PEER_DENSE_TOKENS = 512
PEER_DENSE_EXPERTS = 2 * PEER_GATE_SUBLANES * PEER_KEYS
PEER_SLOTS = PEER_HEADS * PEER_TOPK


def _topk_rows(s, k):
    n_rows = s.shape[0]
    rows = lax.broadcasted_iota(jnp.int32, s.shape, 0)
    vals, idxs = [], []
    for _ in range(k):
        m = jnp.max(s, axis=0, keepdims=True)
        idx = jnp.min(jnp.where(s == m, rows, n_rows), axis=0, keepdims=True)
        vals.append(m)
        idxs.append(idx)
        s = jnp.where(rows == idx, -jnp.inf, s)
    return jnp.concatenate(vals, axis=0), jnp.concatenate(idxs, axis=0)


def _select_rows(pos, table):
    out = jnp.zeros(pos.shape, table.dtype)
    for r in range(table.shape[0]):
        out = jnp.where(pos == r, table[r:r + 1, :], out)
    return out


def _peer_route_kernel(h_ref, g_ref, shift_ref, scale_ref, wq_hi_ref, wq_lo_ref, k_hi_ref, k_lo_ref,
                       xn_ref, i1_ref, i2_ref, gate_ref, q_scr):
    xn = _modulated_norm(h_ref[...], g_ref[...], shift_ref[0], scale_ref[0])
    xn_ref[...] = xn.astype(BF16)
    x_hi, x_lo = _split_bf16(xn)
    q_scr[...] = _dot3(wq_hi_ref[...], wq_lo_ref[...], x_hi, x_lo, _NT_DIMS)
    half = PEER_DQ // 2

    def head_body(h, carry):
        base = pl.multiple_of(h * PEER_DQ, PEER_DQ)
        tops = []
        for j in range(2):
            qq = q_scr[pl.ds(base + j * half, half), :]
            q_hi, q_lo = _split_bf16(qq)
            s = _dot3(k_hi_ref[j, h], k_lo_ref[j, h], q_hi, q_lo, _NN_DIMS)
            tops.append(_topk_rows(s, PEER_TOPK))
        (v1, i1), (v2, i2) = tops
        cand = jnp.concatenate([v1[a:a + 1, :] + v2 for a in range(PEER_TOPK)], axis=0)
        best, pos = _topk_rows(cand, PEER_TOPK)
        e = jnp.exp(best - best[0:1, :])
        gates = e / jnp.sum(e, axis=0, keepdims=True)
        row0 = pl.multiple_of(h * PEER_TOPK, PEER_TOPK)
        i1_ref[pl.ds(row0, PEER_TOPK), :] = _select_rows(pos >> (PEER_TOPK.bit_length() - 1), i1)
        i2_ref[pl.ds(row0, PEER_TOPK), :] = _select_rows(pos & (PEER_TOPK - 1), i2)
        gate_ref[pl.ds(row0, PEER_TOPK), :] = gates
        return carry

    lax.fori_loop(0, PEER_HEADS, head_body, 0)


def _bf16_bits(x):
    return pltpu.bitcast(x.astype(BF16).astype(F32), jnp.uint32)


def _peer_gate_kernel(i1_ref, i2_ref, gate_ref, g_ref):
    rows = lax.broadcasted_iota(jnp.int32, (PEER_KEYS, PEER_SLOTS), 0)
    sub = PEER_GATE_SUBLANES

    def token_body(t, carry):
        a = i1_ref[pl.ds(t, 1), :]
        b = i2_ref[pl.ds(t, 1), :]
        w = gate_ref[pl.ds(t, 1), :]
        lhs = jnp.where(rows == a, w, 0.0).astype(BF16)
        rhs = jnp.where(rows == b, 1.0, 0.0).astype(BF16)
        gt = _mm(lhs, rhs, _NT_DIMS)
        groups = PEER_KEYS // (2 * sub)
        lo = jnp.concatenate([gt[2 * sub * g:2 * sub * g + sub] for g in range(groups)], axis=0)
        hi = jnp.concatenate([gt[2 * sub * g + sub:2 * sub * (g + 1)] for g in range(groups)], axis=0)
        g_ref[t] = (_bf16_bits(lo) >> 16) | _bf16_bits(hi)
        return carry

    lax.fori_loop(0, g_ref.shape[0], token_body, 0, unroll=PEER_GATE_UNROLL)


def _peer_dense_kernel(xn_ref, u_ref, v_ref, gpk_ref, h_ref, mod_ref, o_ref, acc_ref):
    j = pl.program_id(1)

    @pl.when(j == 0)
    def _():
        acc_ref[...] = jnp.zeros_like(acc_ref)

    sub = PEER_GATE_SUBLANES
    xn = xn_ref[...]
    for half in range(2):
        rows = slice(half * sub * PEER_KEYS, (half + 1) * sub * PEER_KEYS)
        hid = _gelu_erf(_mm(xn, u_ref[rows, :], _NT_DIMS))
        ys = []
        for r in range(sub):
            word = gpk_ref[:, r, :]
            bits = (word << 16) if half == 0 else (word & jnp.uint32(0xFFFF0000))
            ys.append(pltpu.bitcast(bits, F32) * hid[:, r * PEER_KEYS:(r + 1) * PEER_KEYS])
        y = jnp.concatenate(ys, axis=1).astype(BF16)
        acc_ref[...] += _mm(y, v_ref[rows, :])

    @pl.when(j == pl.num_programs(1) - 1)
    def _():
        o_ref[...] = h_ref[...] + mod_ref[0] * acc_ref[...]


def _peer_layer(h, norm_g, shift, scale, gate_mod, wq_t_hi, wq_t_lo, keys_hi, keys_lo, u_bf, v_bf):
    n, d = h.shape
    nb = shift.shape[0]
    rows_per_batch = n // nb
    tr = min(PEER_ROUTE_TOKENS, rows_per_batch)
    full = lambda *shape: pl.BlockSpec(shape, lambda i: (0,) * len(shape))
    per_batch = lambda t: pl.BlockSpec((1, 1, d), lambda i: (i * t // rows_per_batch, 0, 0))
    xn, i1, i2, gate = pl.pallas_call(
        _peer_route_kernel,
        grid=(n // tr,),
        in_specs=[pl.BlockSpec((tr, d), lambda i: (i, 0)), full(1, d), per_batch(tr), per_batch(tr),
                  full(PEER_HEADS * PEER_DQ, d), full(PEER_HEADS * PEER_DQ, d),
                  full(2, PEER_HEADS, PEER_KEYS, PEER_DQ // 2), full(2, PEER_HEADS, PEER_KEYS, PEER_DQ // 2)],
        out_specs=[pl.BlockSpec((tr, d), lambda i: (i, 0))] + [pl.BlockSpec((PEER_SLOTS, tr), lambda i: (0, i))] * 3,
        out_shape=[jax.ShapeDtypeStruct((n, d), BF16),
                   jax.ShapeDtypeStruct((PEER_SLOTS, n), jnp.int32),
                   jax.ShapeDtypeStruct((PEER_SLOTS, n), jnp.int32),
                   jax.ShapeDtypeStruct((PEER_SLOTS, n), F32)],
        scratch_shapes=[pltpu.VMEM((PEER_HEADS * PEER_DQ, tr), F32)],
        compiler_params=pltpu.CompilerParams(dimension_semantics=("arbitrary",)),
        name="peer_route",
    )(h, norm_g.reshape(1, d), shift, scale, wq_t_hi, wq_t_lo, keys_hi, keys_lo)

    tg = min(PEER_GATE_TOKENS, n)
    slot_spec = pl.BlockSpec((tg, PEER_SLOTS), lambda i: (i, 0))
    gmat = pl.pallas_call(
        _peer_gate_kernel,
        grid=(n // tg,),
        in_specs=[slot_spec, slot_spec, slot_spec],
        out_specs=pl.BlockSpec((tg, PEER_KEYS // 2, PEER_KEYS), lambda i: (i, 0, 0)),
        out_shape=jax.ShapeDtypeStruct((n, PEER_KEYS // 2, PEER_KEYS), jnp.uint32),
        compiler_params=pltpu.CompilerParams(dimension_semantics=("arbitrary",)),
        name="peer_gate",
    )(i1.T, i2.T, gate.T)

    tm = min(PEER_DENSE_TOKENS, rows_per_batch)
    te = PEER_DENSE_EXPERTS
    return pl.pallas_call(
        _peer_dense_kernel,
        grid=(n // tm, PEER_EXPERTS // te),
        in_specs=[pl.BlockSpec((tm, d), lambda i, j: (i, 0)),
                  pl.BlockSpec((te, d), lambda i, j: (j, 0)),
                  pl.BlockSpec((te, d), lambda i, j: (j, 0)),
                  pl.BlockSpec((tm, PEER_GATE_SUBLANES, PEER_KEYS), lambda i, j: (i, j, 0)),
                  pl.BlockSpec((tm, d), lambda i, j: (i, 0)),
                  pl.BlockSpec((1, 1, d), lambda i, j: (i * tm // rows_per_batch, 0, 0))],
        out_specs=pl.BlockSpec((tm, d), lambda i, j: (i, 0)),
        out_shape=jax.ShapeDtypeStruct((n, d), F32),
        scratch_shapes=[pltpu.VMEM((tm, d), F32)],
        compiler_params=pltpu.CompilerParams(dimension_semantics=("arbitrary", "arbitrary"),
                                             vmem_limit_bytes=52 * 2 ** 20),
        name="peer_dense",
    )(xn, u_bf, v_bf, gmat, h, gate_mod)


def _final_norm_kernel(x_ref, g_ref, o_ref):
    o_ref[...] = _rms(x_ref[...], g_ref[...])


def _final_norm(h, g):
    n = h.shape[0] * h.shape[1]
    x2 = h.reshape(n, D_MODEL)
    tm = 512
    out = pl.pallas_call(
        _final_norm_kernel,
        grid=(n // tm,),
        in_specs=[pl.BlockSpec((tm, D_MODEL), lambda i: (i, 0)),
                  pl.BlockSpec((1, D_MODEL), lambda i: (0, 0))],
        out_specs=pl.BlockSpec((tm, D_MODEL), lambda i: (i, 0)),
        out_shape=jax.ShapeDtypeStruct((n, D_MODEL), F32),
        name="final_norm",
    )(x2, g.reshape(1, D_MODEL))
    return out.reshape(h.shape)


def _mix_layer(h_lat, h_ctx, mod_l, mod_c, norm_g, w_in, w_out, ssd, gla, mla, s5, ctx_out):
    b, n_lat, d = h_lat.shape
    n_lat_tiles = n_lat // ROW_TILE
    hcomb = jnp.concatenate([h_lat, h_ctx], axis=1)
    tab = lambda k: jnp.concatenate([mod_l[k], mod_c[k]], axis=0)
    p = _inproj(hcomb, norm_g, tab(0), tab(1), _pack_w_in(w_in), n_lat_tiles)
    ssd_y, ssd_xbc = _ssd_mixer(p, ssd["conv_w"], ssd["conv_b"], ssd["a_log"], ssd["dt_bias"], n_lat_tiles)
    gla_o = _gla_mixer(p, gla["gate_w"], gla["gate_b"], n_lat_tiles)
    mla_lat, mla_ctx = _mla_mixer(p, mla["q_norm_g"], mla["w_uq"], mla["kv_norm_g"], mla["w_ukv"], n_lat_tiles, ctx_out)
    s5_y = _s5_mixer(p, s5["a_re"], s5["a_im"], s5["log_dt"], s5["b_re"], s5["b_im"], s5["c_re"], s5["c_im"],
                     n_lat_tiles)
    post = functools.partial(_post, p=p, ssd_xbc=ssd_xbc, ssd_y=ssd_y, gla_o=gla_o, s5_y=s5_y, ssd_d=ssd["d"],
                             ssd_norm_g=ssd["norm_g"], gla_norm_g=gla["norm_g"], s5_d=s5["d"],
                             glu_w=s5["glu_w"], glu_b=s5["glu_b"], w_out=w_out)
    new_lat = post(h_lat, mla_y=mla_lat, mod=mod_l[2], row_off=0, mla_off=0)
    new_ctx = None
    if ctx_out:
        new_ctx = post(h_ctx, mla_y=mla_ctx, mod=mod_c[2], row_off=n_lat_tiles, mla_off=0)
    return new_lat, new_ctx


def kernel(x, c, ctx, c_ctx, ada_w, ada_b, norm_mix_g, norm_ffn_g, w_in, w_out,
           ssd_conv_w, ssd_conv_b, ssd_a_log, ssd_dt_bias, ssd_d, ssd_norm_g,
           gla_gate_w, gla_gate_b, gla_norm_g, mla_q_norm_g, mla_w_uq, mla_kv_norm_g, mla_w_ukv,
           s5_a_re, s5_a_im, s5_log_dt, s5_b_re, s5_b_im, s5_c_re, s5_c_im, s5_d, s5_glu_w, s5_glu_b,
           peer_w_q, peer_sub_keys, peer_u, peer_v, final_norm_g):
    h_lat, h_ctx = x, ctx
    cond_lat = jax.nn.silu(c)[:, None, :]
    cond_ctx = jax.nn.silu(c_ctx)[None, None, :]
    for i in range(DEPTH):
        ctx_out = i < DEPTH - 1
        mod_l = jnp.split(cond_lat @ ada_w[i] + ada_b[i], N_MOD, axis=-1)
        mod_c = jnp.split(cond_ctx @ ada_w[i] + ada_b[i], N_MOD, axis=-1)
        ssd = dict(conv_w=ssd_conv_w[i], conv_b=ssd_conv_b[i], a_log=ssd_a_log[i], dt_bias=ssd_dt_bias[i],
                   d=ssd_d[i], norm_g=ssd_norm_g[i])
        gla = dict(gate_w=gla_gate_w[i], gate_b=gla_gate_b[i], norm_g=gla_norm_g[i])
        mla = dict(q_norm_g=mla_q_norm_g[i], w_uq=mla_w_uq[i], kv_norm_g=mla_kv_norm_g[i], w_ukv=mla_w_ukv[i])
        s5 = dict(a_re=s5_a_re[i], a_im=s5_a_im[i], log_dt=s5_log_dt[i], b_re=s5_b_re[i], b_im=s5_b_im[i],
                  c_re=s5_c_re[i], c_im=s5_c_im[i], d=s5_d[i], glu_w=s5_glu_w[i], glu_b=s5_glu_b[i])
        h_lat, h_ctx_new = _mix_layer(h_lat, h_ctx, mod_l, mod_c, norm_mix_g[i], w_in[i], w_out[i],
                                      ssd, gla, mla, s5, ctx_out)
        wq_t_hi, wq_t_lo = _split_bf16(peer_w_q[i].T)
        keys_hi, keys_lo = _split_bf16(peer_sub_keys[i])
        u_bf, v_bf = peer_u[i].astype(BF16), peer_v[i].astype(BF16)
        peer = functools.partial(_peer_layer, norm_g=norm_ffn_g[i], wq_t_hi=wq_t_hi, wq_t_lo=wq_t_lo,
                                 keys_hi=keys_hi, keys_lo=keys_lo, u_bf=u_bf, v_bf=v_bf)
        h_lat = peer(h_lat.reshape(-1, D_MODEL), shift=mod_l[3], scale=mod_l[4],
                     gate_mod=mod_l[5]).reshape(h_lat.shape)
        if ctx_out:
            h_ctx = peer(h_ctx_new.reshape(-1, D_MODEL), shift=mod_c[3], scale=mod_c[4],
                         gate_mod=mod_c[5]).reshape(h_ctx.shape)
    return _final_norm(h_lat, final_norm_g)
```

```python
import functools
import jax
import jax.numpy as jnp
from jax import lax
import numpy as np
from jax.experimental import pallas as pl
from jax.experimental.pallas import tpu as pltpu

D_MODEL = 1024
DEPTH = 2
GRID_W = 64
NORM_EPS = 1e-6
N_MOD = 6

GROUP_WIDTH = D_MODEL // 4

SSD_WIDTH = GROUP_WIDTH
SSD_HEAD_DIM = 64
SSD_HEADS = SSD_WIDTH // SSD_HEAD_DIM
SSD_GROUPS = 2
SSD_STATE = 128
SSD_CONV = 5
SSD_CHUNK = 128
SSD_CONV_CH = SSD_WIDTH + 2 * SSD_GROUPS * SSD_STATE
SSD_IN = SSD_WIDTH + SSD_CONV_CH + 2 * SSD_HEADS

GLA_WIDTH = GROUP_WIDTH
GLA_HEADS = 4
GLA_DV = GLA_WIDTH // GLA_HEADS
GLA_DK = GLA_DV // 2
GLA_QK = GLA_HEADS * GLA_DK
GLA_GATE_RANK = 16
GLA_TAU = 16.0
GLA_CHUNK = 64
GLA_IN = 2 * GLA_QK + 2 * GLA_WIDTH + 2 * GLA_GATE_RANK

MLA_WIDTH = GROUP_WIDTH
MLA_HEADS = 4
MLA_V = MLA_WIDTH // MLA_HEADS
MLA_NOPE = 64
MLA_ROPE = 32
MLA_Q_RANK = 256
MLA_KV_RANK = 128
MLA_SCALE = (MLA_NOPE + MLA_ROPE) ** -0.5
ROPE_BASE = 10000.0
MLA_IN = MLA_Q_RANK + MLA_KV_RANK + MLA_ROPE

S5_WIDTH = GROUP_WIDTH
S5_GROUP = 16
S5_NGROUPS = S5_WIDTH // S5_GROUP
S5_STATE = 64
S5_MAX_RE = -1e-4
S5_IN = S5_WIDTH
S5_CHUNK = 16
S5_PAIRS = S5_NGROUPS // 2

PEER_KEYS = 128
PEER_EXPERTS = PEER_KEYS * PEER_KEYS
PEER_HEADS = 8
PEER_TOPK = 16
PEER_DQ = 128

LANES = 128
ROW_TILE = 256

F32 = jnp.float32
BF16 = jnp.bfloat16

COL_XS, COL_BM, COL_CM, COL_Z = 0, 256, 512, 768
COL_GLA_V, COL_GLA_R, COL_CQ, COL_S5 = 1024, 1280, 1536, 1792
COL_GLA_Q, COL_GLA_K, COL_CKV, COL_DT, COL_GLR, COL_KR, COL_KRROT = 2048, 2176, 2304, 2432, 2560, 2688, 2816
P_COLS = 2944

_NN_DIMS = (((1,), (0,)), ((), ()))
_NT_DIMS = (((1,), (1,)), ((), ()))
_TN_DIMS = (((0,), (0,)), ((), ()))


def _mm(a, b, dims=_NN_DIMS):
    return lax.dot_general(a, b, dims, preferred_element_type=F32)


def _split_bf16(x):
    hi = x.astype(BF16)
    lo = (x - hi.astype(F32)).astype(BF16)
    return hi, lo


def _split3_bf16(x):
    p1 = x.astype(BF16)
    r1 = x - p1.astype(F32)
    p2 = r1.astype(BF16)
    p3 = (r1 - p2.astype(F32)).astype(BF16)
    return p1, p2, p3


def _dot3(a_hi, a_lo, b_hi, b_lo, dims):
    return _mm(a_hi, b_hi, dims) + _mm(a_hi, b_lo, dims) + _mm(a_lo, b_hi, dims)


def _gelu_erf(x):
    return 0.5 * x * (1.0 + lax.erf(x * (2.0 ** -0.5)))


def _silu(x):
    return x * jax.nn.sigmoid(x)


def _softplus(x):
    return jnp.maximum(x, 0.0) + jnp.log1p(jnp.exp(-jnp.abs(x)))


def _log_sigmoid(x):
    return jnp.minimum(x, 0.0) - jnp.log1p(jnp.exp(-jnp.abs(x)))


def _rms(x, g):
    return x * lax.rsqrt(jnp.mean(x * x, axis=-1, keepdims=True) + NORM_EPS) * g


def _modulated_norm(x, g, shift, scale):
    return _rms(x, g) * (1.0 + scale) + shift


def _causal_mask(n, reverse):
    ri = lax.broadcasted_iota(jnp.int32, (n, n), 0)
    ci = lax.broadcasted_iota(jnp.int32, (n, n), 1)
    return (ci >= ri) if reverse else (ci <= ri)


def _scan_chunk(s, n_lat, n_ctx, reverse):
    if reverse:
        return n_lat + n_ctx - 1 - s
    return jnp.where(s < n_ctx, n_lat + s, s - n_ctx)


def _inproj_kernel(h_ref, g_ref, shift_ref, scale_ref, w_ref, o_ref):
    xn = _modulated_norm(h_ref[0], g_ref[...], shift_ref[0], scale_ref[0])
    o_ref[0] = _mm(xn.astype(BF16), w_ref[...])


def _inproj(hcomb, norm_g, shift_tab, scale_tab, w_pad, n_lat_tiles):
    b, r, d = hcomb.shape
    mod_spec = pl.BlockSpec((1, 1, d), lambda i, t: (jnp.where(t < n_lat_tiles, i, b), 0, 0))
    return pl.pallas_call(
        _inproj_kernel,
        grid=(b, r // ROW_TILE),
        in_specs=[pl.BlockSpec((1, ROW_TILE, d), lambda i, t: (i, t, 0)),
                  pl.BlockSpec((1, d), lambda i, t: (0, 0)), mod_spec, mod_spec,
                  pl.BlockSpec((d, P_COLS), lambda i, t: (0, 0))],
        out_specs=pl.BlockSpec((1, ROW_TILE, P_COLS), lambda i, t: (i, t, 0)),
        out_shape=jax.ShapeDtypeStruct((b, r, P_COLS), F32),
        compiler_params=pltpu.CompilerParams(dimension_semantics=("arbitrary", "arbitrary"),
                                             vmem_limit_bytes=48 * 2 ** 20),
        name="inproj",
    )(hcomb, norm_g.reshape(1, d), shift_tab, scale_tab, w_pad)


def _pack_w_in(w):
    o_ssd, o_gla, o_mla, o_s5 = 0, SSD_IN, SSD_IN + GLA_IN, SSD_IN + GLA_IN + MLA_IN
    out = jnp.zeros((w.shape[0], P_COLS), F32)
    put = lambda out, col, src, width: out.at[:, col:col + width].set(w[:, src:src + width])
    out = put(out, COL_Z, o_ssd, SSD_WIDTH)
    out = put(out, COL_XS, o_ssd + SSD_WIDTH, SSD_CONV_CH)
    out = put(out, COL_DT, o_ssd + SSD_WIDTH + SSD_CONV_CH, 2 * SSD_HEADS)
    out = put(out, COL_GLA_Q, o_gla, GLA_QK)
    out = put(out, COL_GLA_K, o_gla + GLA_QK, GLA_QK)
    out = put(out, COL_GLA_V, o_gla + 2 * GLA_QK, GLA_WIDTH)
    out = put(out, COL_GLA_R, o_gla + 2 * GLA_QK + GLA_WIDTH, GLA_WIDTH)
    out = put(out, COL_GLR, o_gla + 2 * GLA_QK + 2 * GLA_WIDTH, 2 * GLA_GATE_RANK)
    out = put(out, COL_CQ, o_mla, MLA_Q_RANK)
    out = put(out, COL_CKV, o_mla + MLA_Q_RANK, MLA_KV_RANK)
    o_kr = o_mla + MLA_Q_RANK + MLA_KV_RANK
    half = MLA_ROPE // 2
    out = put(out, COL_KR + MLA_NOPE, o_kr, MLA_ROPE)
    out = out.at[:, COL_KRROT + MLA_NOPE:COL_KRROT + MLA_NOPE + half].set(-w[:, o_kr + half:o_kr + MLA_ROPE])
    out = out.at[:, COL_KRROT + MLA_NOPE + half:COL_KRROT + MLA_NOPE + MLA_ROPE].set(w[:, o_kr:o_kr + half])
    out = put(out, COL_S5, o_s5, S5_WIDTH)
    return out.astype(BF16)


def _ssd_prep_kernel(x_ref, prev_ref, next_ref, dt_ref, w_ref, b_ref, bias_ref, xbc_ref, dtc_ref, dtt_ref,
                     *, n_lat_tiles):
    t = pl.program_id(1)
    x = x_ref[0]
    halo = prev_ref.shape[1]
    prev = jnp.where(jnp.logical_and(t > 0, t < n_lat_tiles), prev_ref[0], 0.0)
    nxt = jnp.where(t < n_lat_tiles - 1, next_ref[0], 0.0)
    ext = jnp.concatenate([prev, x, nxt], axis=0)
    rows = ext.shape[0]
    left = SSD_CONV // 2
    acc = jnp.zeros_like(x) + b_ref[...]
    for k in range(SSD_CONV):
        shifted = ext if k == left else pltpu.roll(ext, (left - k) % rows, 0)
        acc = acc + w_ref[k:k + 1, :] * shifted[halo:halo + x.shape[0]]
    xbc_ref[0] = _silu(acc)
    dt = _softplus(dt_ref[0] + bias_ref[...])
    dtc_ref[0] = dt
    dtt_ref[0] = dt.T[:dtt_ref.shape[1]]


def _ssd_scan_kernel(*refs, direction, n_lat, n_ctx, has_prev):
    if has_prev:
        xbc_ref, dtc_ref, dtt_ref, ahr_ref, ahc_ref, yprev_ref, y_ref, state_ref = refs
    else:
        xbc_ref, dtc_ref, dtt_ref, ahr_ref, ahc_ref, y_ref, state_ref = refs
    reverse = direction == 1
    s = pl.program_id(1)

    @pl.when(s == 0)
    def _():
        state_ref[...] = jnp.zeros_like(state_ref)

    q = SSD_CHUNK
    mask = _causal_mask(q, reverse)
    tri = jnp.where(mask, 1.0, 0.0).astype(BF16)
    xbc = xbc_ref[0]
    xs, bm, cm = xbc[:, :SSD_WIDTH], xbc[:, SSD_WIDTH:SSD_WIDTH + 256], xbc[:, SSD_WIDTH + 256:]
    dtc = dtc_ref[0]
    a_col = dtc * ahr_ref[...]
    a_row = dtt_ref[0] * ahc_ref[...]
    acum_col = sum(_mm(tri, part) for part in _split3_bf16(a_col))
    acum_row = sum(_mm(part, tri, _NT_DIMS) for part in _split3_bf16(a_row))
    end = 0 if reverse else q - 1
    bm_bf, cm_bf = bm.astype(BF16), cm.astype(BF16)
    ys = []
    cb = {}
    for h in range(SSD_HEADS):
        g = h // (SSD_HEADS // SSD_GROUPS)
        gs = slice(g * SSD_STATE, (g + 1) * SSD_STATE)
        if g not in cb:
            cb[g] = _mm(cm_bf[:, gs], bm_bf[:, gs], _NT_DIMS)
        ch = direction * SSD_HEADS + h
        ac = acum_col[:, ch:ch + 1]
        ar = acum_row[ch:ch + 1, :]
        decay = jnp.exp(jnp.where(mask, ac - ar, -jnp.inf))
        xd = xs[:, h * SSD_HEAD_DIM:(h + 1) * SSD_HEAD_DIM] * dtc[:, ch:ch + 1]
        y_diag = _mm((cb[g] * decay).astype(BF16), xd.astype(BF16))
        a_end = ac[end:end + 1, :]
        st_local = _mm((xd * jnp.exp(a_end - ac)).astype(BF16), bm_bf[:, gs], _TN_DIMS)
        hs = state_ref[h]
        y_off = jnp.exp(ac) * _mm(cm_bf[:, gs], hs.astype(BF16), _NT_DIMS)
        state_ref[h] = jnp.exp(a_end) * hs + st_local
        ys.append(y_diag + y_off)
    y = jnp.concatenate(ys, axis=1)
    if has_prev:
        y = y + yprev_ref[0]
    y_ref[0] = y


def _ssd_mixer(p, conv_w, conv_b, a_log, dt_bias, n_lat_tiles):
    b, r, _ = p.shape
    nt = r // ROW_TILE
    halo = 8
    hb = ROW_TILE // halo
    w8 = jnp.zeros((8, SSD_CONV_CH), F32).at[:SSD_CONV].set(conv_w)
    bias = jnp.zeros((1, LANES), F32).at[0, :2 * SSD_HEADS].set(dt_bias.reshape(-1))
    xbc, dtc, dtt = pl.pallas_call(
        functools.partial(_ssd_prep_kernel, n_lat_tiles=n_lat_tiles),
        grid=(b, nt),
        in_specs=[pl.BlockSpec((1, ROW_TILE, SSD_CONV_CH), lambda i, t: (i, t, 0)),
                  pl.BlockSpec((1, halo, SSD_CONV_CH), lambda i, t: (i, jnp.maximum(t * hb - 1, 0), 0)),
                  pl.BlockSpec((1, halo, SSD_CONV_CH), lambda i, t: (i, jnp.minimum((t + 1) * hb, nt * hb - 1), 0)),
                  pl.BlockSpec((1, ROW_TILE, LANES), lambda i, t: (i, t, COL_DT // LANES)),
                  pl.BlockSpec((8, SSD_CONV_CH), lambda i, t: (0, 0)),
                  pl.BlockSpec((1, SSD_CONV_CH), lambda i, t: (0, 0)),
                  pl.BlockSpec((1, LANES), lambda i, t: (0, 0))],
        out_specs=[pl.BlockSpec((1, ROW_TILE, SSD_CONV_CH), lambda i, t: (i, t, 0)),
                   pl.BlockSpec((1, ROW_TILE, LANES), lambda i, t: (i, t, 0)),
                   pl.BlockSpec((1, 8, ROW_TILE), lambda i, t: (i, 0, t))],
        out_shape=[jax.ShapeDtypeStruct((b, r, SSD_CONV_CH), F32),
                   jax.ShapeDtypeStruct((b, r, LANES), F32),
                   jax.ShapeDtypeStruct((b, 8, r), F32)],
        compiler_params=pltpu.CompilerParams(dimension_semantics=("arbitrary", "arbitrary")),
        name="ssd_prep",
    )(p, p, p, p, w8, conv_b.reshape(1, -1), bias)

    a_head = -jnp.exp(a_log.astype(F32)).reshape(-1)
    ahr = jnp.zeros((1, LANES), F32).at[0, :2 * SSD_HEADS].set(a_head)
    ahc = a_head.reshape(2 * SSD_HEADS, 1)
    n_lat = n_lat_tiles * ROW_TILE // SSD_CHUNK
    n_ctx = r // SSD_CHUNK - n_lat
    y = None
    for direction in (0, 1):
        cidx = functools.partial(_scan_chunk, n_lat=n_lat, n_ctx=n_ctx, reverse=direction == 1)
        in_specs = [pl.BlockSpec((1, SSD_CHUNK, SSD_CONV_CH), lambda i, s: (i, cidx(s), 0)),
                    pl.BlockSpec((1, SSD_CHUNK, LANES), lambda i, s: (i, cidx(s), 0)),
                    pl.BlockSpec((1, 8, SSD_CHUNK), lambda i, s: (i, 0, cidx(s))),
                    pl.BlockSpec((1, LANES), lambda i, s: (0, 0)),
                    pl.BlockSpec((2 * SSD_HEADS, 1), lambda i, s: (0, 0))]
        args = [xbc, dtc, dtt, ahr, ahc]
        y_spec = pl.BlockSpec((1, SSD_CHUNK, SSD_WIDTH), lambda i, s: (i, cidx(s), 0))
        if y is not None:
            in_specs.append(y_spec)
            args.append(y)
        y = pl.pallas_call(
            functools.partial(_ssd_scan_kernel, direction=direction, n_lat=n_lat, n_ctx=n_ctx,
                              has_prev=y is not None),
            grid=(b, n_lat + n_ctx),
            in_specs=in_specs,
            out_specs=y_spec,
            out_shape=jax.ShapeDtypeStruct((b, r, SSD_WIDTH), F32),
            scratch_shapes=[pltpu.VMEM((SSD_HEADS, SSD_HEAD_DIM, SSD_STATE), F32)],
            compiler_params=pltpu.CompilerParams(dimension_semantics=("arbitrary", "arbitrary")),
            name=f"ssd_scan_{direction}",
        )(*args)
    return y, xbc


def _gla_scan_kernel(*refs, direction, has_prev):
    if has_prev:
        q_ref, k_ref, v_ref, glr_ref, wg_ref, bias_ref, oprev_ref, o_ref, st_ref = refs
    else:
        q_ref, k_ref, v_ref, glr_ref, wg_ref, bias_ref, o_ref, st_ref = refs
    reverse = direction == 1
    s = pl.program_id(1)

    @pl.when(s == 0)
    def _():
        st_ref[...] = jnp.zeros_like(st_ref)

    n = GLA_CHUNK
    mask = _causal_mask(n, reverse)
    tri = jnp.where(mask, 1.0, 0.0).astype(BF16)
    g_hi, g_lo = _split_bf16(glr_ref[0])
    logits = _dot3(g_hi, g_lo, wg_ref[0], wg_ref[1], _NN_DIMS) + bias_ref[...]
    logg = _log_sigmoid(logits) * (1.0 / GLA_TAU)
    bcum = sum(_mm(tri, part) for part in _split3_bf16(logg))
    end = 0 if reverse else n - 1
    b_end = bcum[end:end + 1, :]
    q, k, v = q_ref[0], k_ref[0], v_ref[0]
    qe = q * jnp.exp(bcum) * (GLA_DK ** -0.5)
    ke = (k * jnp.exp(-bcum)).astype(BF16)
    kd = k * jnp.exp(b_end - bcum)
    decay_end = jnp.exp(b_end)
    lane_head = lax.broadcasted_iota(jnp.int32, (1, GLA_QK), 1) >> (GLA_DK.bit_length() - 1)
    outs = []
    for h in range(GLA_HEADS):
        hm = lane_head == h
        qh = jnp.where(hm, qe, 0.0).astype(BF16)
        att = jnp.where(mask, _mm(qh, ke, _NT_DIMS), 0.0)
        vh = v[:, h * GLA_DV:(h + 1) * GLA_DV].astype(BF16)
        st = st_ref[h]
        o_h = _mm(att.astype(BF16), vh) + _mm(qh, st.astype(BF16), _NT_DIMS)
        local = _mm(vh, jnp.where(hm, kd, 0.0).astype(BF16), _TN_DIMS)
        st_ref[h] = st * decay_end + local
        outs.append(o_h)
    o = jnp.concatenate(outs, axis=1)
    if has_prev:
        o = o + oprev_ref[0]
    o_ref[0] = o


def _gla_mixer(p, gate_w, gate_b, n_lat_tiles):
    b, r, _ = p.shape
    n_lat = n_lat_tiles * ROW_TILE // GLA_CHUNK
    n_ctx = r // GLA_CHUNK - n_lat
    o = None
    for direction in (0, 1):
        cidx = functools.partial(_scan_chunk, n_lat=n_lat, n_ctx=n_ctx, reverse=direction == 1)
        wg = jnp.zeros((LANES, GLA_QK), F32).at[direction * GLA_GATE_RANK:(direction + 1) * GLA_GATE_RANK].set(
            gate_w[direction])
        wg = jnp.stack(_split_bf16(wg))
        blk = lambda width, col: pl.BlockSpec((1, GLA_CHUNK, width), lambda i, s: (i, cidx(s), col // width))
        in_specs = [blk(GLA_QK, COL_GLA_Q), blk(GLA_QK, COL_GLA_K), blk(GLA_WIDTH, COL_GLA_V), blk(LANES, COL_GLR),
                    pl.BlockSpec((2, LANES, GLA_QK), lambda i, s: (0, 0, 0)),
                    pl.BlockSpec((1, GLA_QK), lambda i, s: (0, 0))]
        args = [p, p, p, p, wg, gate_b[direction].reshape(1, -1)]
        o_spec = pl.BlockSpec((1, GLA_CHUNK, GLA_WIDTH), lambda i, s: (i, cidx(s), 0))
        if o is not None:
            in_specs.append(o_spec)
            args.append(o)
        o = pl.pallas_call(
            functools.partial(_gla_scan_kernel, direction=direction, has_prev=o is not None),
            grid=(b, n_lat + n_ctx),
            in_specs=in_specs,
            out_specs=o_spec,
            out_shape=jax.ShapeDtypeStruct((b, r, GLA_WIDTH), F32),
            scratch_shapes=[pltpu.VMEM((GLA_HEADS, GLA_DV, GLA_QK), F32)],
            compiler_params=pltpu.CompilerParams(dimension_semantics=("arbitrary", "arbitrary")),
            name=f"gla_scan_{direction}",
        )(*args)
    return o


MLA_Q_TILE = 1024
MLA_K_TILE = 256


def _mla_prep_kernel(cq_ref, ckv_ref, kr_ref, krrot_ref, onec_ref, sinr_ref, gq_ref, gkv_ref,
                     wq_ref, wqr_ref, wk_ref, wv_ref, q_ref, k_ref, v_ref):
    qn = _rms(cq_ref[0], gq_ref[...]).astype(BF16)
    kvn = _rms(ckv_ref[0], gkv_ref[...]).astype(BF16)
    onec, sinr = onec_ref[...], sinr_ref[...]
    k_rope = kr_ref[0] * onec + krrot_ref[0] * sinr
    ones_lane = jnp.where(lax.broadcasted_iota(jnp.int32, (1, LANES), 1) == MLA_V, 1.0, 0.0)
    for h in range(MLA_HEADS):
        qh = _mm(qn, wq_ref[h]) * onec + _mm(qn, wqr_ref[h]) * sinr
        q_ref[0, h] = (qh * MLA_SCALE).astype(BF16)
        k_ref[0, h] = (_mm(kvn, wk_ref[h]) + k_rope).astype(BF16)
        v_ref[0, h] = (_mm(kvn, wv_ref[h]) + ones_lane).astype(BF16)


def _mla_attn_kernel(q_ref, k_ref, v_ref, o_ref, m_ref, acc_ref):
    j = pl.program_id(2)

    @pl.when(j == 0)
    def _():
        m_ref[...] = jnp.full_like(m_ref, -jnp.inf)
        acc_ref[...] = jnp.zeros_like(acc_ref)

    reps = k_ref.shape[2] // LANES
    for h in range(MLA_HEADS):
        s = _mm(q_ref[0, h], k_ref[0, h], _NT_DIMS)
        m_prev = m_ref[h]
        m_new = jnp.maximum(m_prev, jnp.max(s, axis=1, keepdims=True))
        p = jnp.exp(s - jnp.concatenate([m_new] * reps, axis=1))
        acc_ref[h] = jnp.exp(m_prev - m_new) * acc_ref[h] + _mm(p.astype(BF16), v_ref[0, h])
        m_ref[h] = m_new

    @pl.when(j == pl.num_programs(2) - 1)
    def _():
        outs = []
        for h in range(MLA_HEADS):
            acc = acc_ref[h]
            outs.append(acc[:, :MLA_V] / acc[:, MLA_V:MLA_V + 1])
        o_ref[0] = jnp.concatenate(outs, axis=1)


def _rope_tables(n_lat, n_rows):
    rows = n_lat // GRID_W
    row = jnp.repeat(jnp.arange(rows, dtype=F32), GRID_W)
    col = jnp.tile(jnp.arange(GRID_W, dtype=F32), rows)
    half = MLA_ROPE // 2
    inv = ROPE_BASE ** (-jnp.arange(0, half, 2, dtype=F32) / half)
    ang = jnp.concatenate([row[:, None] * inv, col[:, None] * inv], axis=-1)
    cos = jnp.concatenate([jnp.cos(ang), jnp.ones((n_rows - n_lat, half), F32)], axis=0)
    sin = jnp.concatenate([jnp.sin(ang), jnp.zeros((n_rows - n_lat, half), F32)], axis=0)
    pad = jnp.zeros((n_rows, LANES - MLA_NOPE - MLA_ROPE), F32)
    onec = jnp.concatenate([jnp.ones((n_rows, MLA_NOPE), F32), cos, cos, pad], axis=1)
    sinr = jnp.concatenate([jnp.zeros((n_rows, MLA_NOPE), F32), sin, sin, pad], axis=1)
    return onec, sinr


def _mla_weights(w_uq, w_ukv):
    dqk = MLA_NOPE + MLA_ROPE
    half = MLA_ROPE // 2
    wq = w_uq.reshape(MLA_Q_RANK, MLA_HEADS, dqk).transpose(1, 0, 2)
    rot = jnp.concatenate([jnp.zeros_like(wq[..., :MLA_NOPE]), -wq[..., MLA_NOPE + half:], wq[..., MLA_NOPE:MLA_NOPE + half]],
                          axis=-1)
    padq = lambda w: jnp.pad(w, ((0, 0), (0, 0), (0, LANES - dqk))).astype(BF16)
    wkv = w_ukv.reshape(MLA_KV_RANK, MLA_HEADS, MLA_NOPE + MLA_V).transpose(1, 0, 2)
    padk = lambda w: jnp.pad(w, ((0, 0), (0, 0), (0, LANES - w.shape[-1]))).astype(BF16)
    return padq(wq), padq(rot), padk(wkv[..., :MLA_NOPE]), padk(wkv[..., MLA_NOPE:])


def _mla_attention(q, k, v, q_tile, q_off, n_q, k_off, n_k):
    b = q.shape[0]
    kt = MLA_K_TILE
    return pl.pallas_call(
        _mla_attn_kernel,
        grid=(b, n_q, n_k),
        in_specs=[pl.BlockSpec((1, MLA_HEADS, q_tile, LANES), lambda i, a, j: (i, 0, q_off + a, 0)),
                  pl.BlockSpec((1, MLA_HEADS, kt, LANES), lambda i, a, j: (i, 0, k_off + j, 0)),
                  pl.BlockSpec((1, MLA_HEADS, kt, LANES), lambda i, a, j: (i, 0, k_off + j, 0))],
        out_specs=pl.BlockSpec((1, q_tile, MLA_WIDTH), lambda i, a, j: (i, a, 0)),
        out_shape=jax.ShapeDtypeStruct((b, n_q * q_tile, MLA_WIDTH), F32),
        scratch_shapes=[pltpu.VMEM((MLA_HEADS, q_tile, LANES), F32), pltpu.VMEM((MLA_HEADS, q_tile, LANES), F32)],
        compiler_params=pltpu.CompilerParams(dimension_semantics=("arbitrary", "arbitrary", "arbitrary")),
        name="mla_attn",
    )(q, k, v)


def _mla_mixer(p, q_norm_g, w_uq, kv_norm_g, w_ukv, n_lat_tiles, ctx_out):
    b, r, _ = p.shape
    nt = r // ROW_TILE
    n_lat = n_lat_tiles * ROW_TILE
    onec, sinr = _rope_tables(n_lat, r)
    wq, wqr, wk, wv = _mla_weights(w_uq, w_ukv)
    blk = lambda width, col: pl.BlockSpec((1, ROW_TILE, width), lambda i, t: (i, t, col // width))
    tab = pl.BlockSpec((ROW_TILE, LANES), lambda i, t: (t, 0))
    full = lambda *shape: pl.BlockSpec(shape, lambda i, t: (0,) * len(shape))
    head_out = pl.BlockSpec((1, MLA_HEADS, ROW_TILE, LANES), lambda i, t: (i, 0, t, 0))
    q, k, v = pl.pallas_call(
        _mla_prep_kernel,
        grid=(b, nt),
        in_specs=[blk(MLA_Q_RANK, COL_CQ), blk(LANES, COL_CKV), blk(LANES, COL_KR), blk(LANES, COL_KRROT), tab, tab,
                  full(1, MLA_Q_RANK), full(1, MLA_KV_RANK),
                  full(MLA_HEADS, MLA_Q_RANK, LANES), full(MLA_HEADS, MLA_Q_RANK, LANES),
                  full(MLA_HEADS, MLA_KV_RANK, LANES), full(MLA_HEADS, MLA_KV_RANK, LANES)],
        out_specs=[head_out] * 3,
        out_shape=[jax.ShapeDtypeStruct((b, MLA_HEADS, r, LANES), BF16)] * 3,
        compiler_params=pltpu.CompilerParams(dimension_semantics=("arbitrary", "arbitrary")),
        name="mla_prep",
    )(p, p, p, p, onec, sinr, q_norm_g.reshape(1, -1), kv_norm_g.reshape(1, -1), wq, wqr, wk, wv)
    q_tile = min(MLA_Q_TILE, n_lat)
    y_lat = _mla_attention(q, k, v, q_tile, 0, n_lat // q_tile, 0, r // MLA_K_TILE)
    y_ctx = None
    if ctx_out:
        n_ctx = r - n_lat
        y_ctx = _mla_attention(q, k, v, n_ctx, n_lat // n_ctx, 1, n_lat // MLA_K_TILE, n_ctx // MLA_K_TILE)
    return y_lat, y_ctx


def _s5_matrices(a_re, a_im, log_dt, b_re, b_im, c_re, c_im):
    q, ng, ns, nc = S5_CHUNK, S5_NGROUPS, S5_STATE, S5_GROUP
    lam = jnp.minimum(a_re.astype(F32), S5_MAX_RE) + 1j * a_im.astype(F32)
    step = jnp.exp(log_dt.astype(F32))[..., None]
    abar = jnp.exp(lam * step)
    bmat = b_re.astype(F32) + 1j * b_im.astype(F32)
    bbar = ((abar - 1.0) / lam)[..., None] * bmat
    cmat = c_re.astype(F32) + 1j * c_im.astype(F32)
    pw = jnp.exp((lam * step)[..., None] * jnp.arange(q + 1, dtype=F32))
    kern = jnp.einsum('dgcn,dgnl,dgnk->dglck', cmat, pw[..., :q], bbar).real
    ii = jnp.arange(q)
    lag_f = ii[None, :] - ii[:, None]
    gather = lambda kd, lag: jnp.where((lag >= 0)[None, :, :, None, None], kd[:, jnp.clip(lag, 0, q - 1)], 0.0)
    t_f = gather(kern[0], lag_f).transpose(0, 1, 4, 2, 3)
    t_b = gather(kern[1], -lag_f).transpose(0, 1, 4, 2, 3)
    t_sum = (t_f + t_b).reshape(ng, q * nc, q * nc)
    pw_f = pw[0][..., q - 1 - ii]
    pw_b = pw[1][..., ii]
    wst = lambda pwd, bb: jnp.einsum('gnj,gnc->gjcn', pwd, bb).reshape(ng, q * nc, ns)
    wst_f, wst_b = wst(pw_f, bbar[0]), wst(pw_b, bbar[1])
    wout = lambda pwd, cm: jnp.einsum('gcn,gni->gnic', cm, pwd).reshape(ng, ns, q * nc)
    wo_f, wo_b = wout(pw[0][..., ii + 1], cmat[0]), wout(pw[1][..., q - ii], cmat[1])
    aq = pw[..., q]

    def pair_cols(x):
        x = x.reshape(S5_PAIRS, 2, x.shape[1], x.shape[2])
        z = jnp.zeros_like(x[:, 0])
        return jnp.concatenate([jnp.concatenate([x[:, 0], z], axis=2), jnp.concatenate([z, x[:, 1]], axis=2)], axis=1)

    w_local = jnp.concatenate([pair_cols(wst_f.real), pair_cols(wst_f.imag),
                               pair_cols(wst_b.real), pair_cols(wst_b.imag)], axis=2)
    w_out = jnp.concatenate([pair_cols(t_sum), pair_cols(wo_f.real), pair_cols(-wo_f.imag),
                             pair_cols(wo_b.real), pair_cols(-wo_b.imag)], axis=1)
    aq_pair = aq.reshape(2, S5_PAIRS, 2 * ns)
    aq_tab = jnp.concatenate([aq_pair[0].real, aq_pair[0].imag, aq_pair[1].real, aq_pair[1].imag], axis=1)
    return w_local.astype(BF16), w_out.astype(BF16), aq_tab.reshape(S5_PAIRS, 1, 8 * ns).astype(F32)


def _s5_local_kernel(u_ref, w_ref, s_ref):
    s_ref[...] = _mm(u_ref[0], w_ref[0])


def _s5_scan_kernel(s_ref, aq_ref, hs_ref, *, n_lat, n_ctx, nb):
    w = 2 * S5_STATE
    aq = aq_ref[0]
    a = [aq[:, i * w:(i + 1) * w] for i in range(4)]
    zero = jnp.zeros((nb, w), F32)
    slab = 8
    cps = slab // nb

    def run_slab(s_re, s_im, a_re, a_im, h_re, h_im, order):
        ent_re, ent_im = [None] * cps, [None] * cps
        for c in order:
            ent_re[c], ent_im[c] = h_re, h_im
            rows = slice(c * nb, (c + 1) * nb)
            h_re, h_im = a_re * h_re - a_im * h_im + s_re[rows], a_re * h_im + a_im * h_re + s_im[rows]
        return jnp.concatenate(ent_re, axis=0), jnp.concatenate(ent_im, axis=0), h_re, h_im

    def body(kk, carry):
        f_re, f_im, b_re, b_im = carry
        rf = pl.multiple_of(_scan_chunk(kk, n_lat // cps, n_ctx // cps, False) * slab, slab)
        rb = pl.multiple_of(_scan_chunk(kk, n_lat // cps, n_ctx // cps, True) * slab, slab)
        e_re, e_im, f_re, f_im = run_slab(s_ref[pl.ds(rf, slab), 0:w], s_ref[pl.ds(rf, slab), w:2 * w],
                                          a[0], a[1], f_re, f_im, range(cps))
        hs_ref[pl.ds(rf, slab), 0:w] = e_re
        hs_ref[pl.ds(rf, slab), w:2 * w] = e_im
        e_re, e_im, b_re, b_im = run_slab(s_ref[pl.ds(rb, slab), 2 * w:3 * w], s_ref[pl.ds(rb, slab), 3 * w:4 * w],
                                          a[2], a[3], b_re, b_im, range(cps - 1, -1, -1))
        hs_ref[pl.ds(rb, slab), 2 * w:3 * w] = e_re
        hs_ref[pl.ds(rb, slab), 3 * w:4 * w] = e_im
        return f_re, f_im, b_re, b_im

    lax.fori_loop(0, (n_lat + n_ctx) // cps, body, (zero, zero, zero, zero))


def _s5_out_kernel(u_ref, hs_ref, w_ref, y_ref):
    k_u = u_ref.shape[2]
    y_ref[0] = _mm(u_ref[0], w_ref[0, :k_u]) + _mm(hs_ref[...].astype(BF16), w_ref[0, k_u:])


def _s5_mixer(p, a_re, a_im, log_dt, b_re, b_im, c_re, c_im, n_lat_tiles):
    b, r, _ = p.shape
    q, nc = S5_CHUNK, S5_GROUP
    n_chunks = r // q
    m = n_chunks * b
    w_local, w_out, aq_tab = _s5_matrices(a_re, a_im, log_dt, b_re, b_im, c_re, c_im)
    u = p[:, :, COL_S5:COL_S5 + S5_WIDTH]
    u_pairs = u.reshape(b, n_chunks, q, S5_PAIRS, 2, nc).transpose(3, 1, 0, 4, 2, 5).reshape(S5_PAIRS, m, 2 * q * nc)
    u_pairs = u_pairs.astype(BF16)
    kw = 2 * q * nc
    cp = pltpu.CompilerParams(dimension_semantics=("arbitrary",))
    s_loc = pl.pallas_call(
        _s5_local_kernel,
        grid=(S5_PAIRS,),
        in_specs=[pl.BlockSpec((1, m, kw), lambda g: (g, 0, 0)), pl.BlockSpec((1, kw, kw), lambda g: (g, 0, 0))],
        out_specs=pl.BlockSpec((m, kw), lambda g: (0, g)),
        out_shape=jax.ShapeDtypeStruct((m, S5_PAIRS * kw), F32),
        compiler_params=cp, name="s5_local",
    )(u_pairs, w_local)
    n_lat = n_lat_tiles * ROW_TILE // q
    hs = pl.pallas_call(
        functools.partial(_s5_scan_kernel, n_lat=n_lat, n_ctx=n_chunks - n_lat, nb=b),
        grid=(S5_PAIRS,),
        in_specs=[pl.BlockSpec((m, kw), lambda g: (0, g)), pl.BlockSpec((1, 1, kw), lambda g: (g, 0, 0))],
        out_specs=pl.BlockSpec((m, kw), lambda g: (0, g)),
        out_shape=jax.ShapeDtypeStruct((m, S5_PAIRS * kw), F32),
        compiler_params=cp, name="s5_scan",
    )(s_loc, aq_tab)
    y_pairs = pl.pallas_call(
        _s5_out_kernel,
        grid=(S5_PAIRS,),
        in_specs=[pl.BlockSpec((1, m, kw), lambda g: (g, 0, 0)), pl.BlockSpec((m, kw), lambda g: (0, g)),
                  pl.BlockSpec((1, 2 * kw, kw), lambda g: (g, 0, 0))],
        out_specs=pl.BlockSpec((1, m, kw), lambda g: (g, 0, 0)),
        out_shape=jax.ShapeDtypeStruct((S5_PAIRS, m, kw), F32),
        compiler_params=cp, name="s5_out",
    )(u_pairs, hs, w_out)
    y = y_pairs.reshape(S5_PAIRS, n_chunks, b, 2, q, nc).transpose(2, 1, 4, 0, 3, 5)
    return y.reshape(b, r, S5_WIDTH)


def _post_kernel(h_ref, xs_ref, z_ref, r_ref, u_ref, ssd_ref, gla_ref, mla_ref, s5_ref,
                 ssd_d_ref, ssd_g_ref, gla_g_ref, s5_d_ref, glu_w_ref, glu_b_ref, w_out_ref, mod_ref, o_ref):
    y = ssd_ref[0] + ssd_d_ref[...] * xs_ref[0]
    ssd = _rms(y * _silu(z_ref[0]), ssd_g_ref[...])
    o = gla_ref[0]
    lane_head = lax.broadcasted_iota(jnp.int32, (1, GLA_WIDTH), 1) >> (GLA_DV.bit_length() - 1)
    ms = jnp.zeros_like(o)
    for h in range(GLA_HEADS):
        oh = o[:, h * GLA_DV:(h + 1) * GLA_DV]
        ms = jnp.where(lane_head == h, jnp.mean(oh * oh, axis=-1, keepdims=True), ms)
    gla = o * lax.rsqrt(ms + NORM_EPS) * gla_g_ref[...] * _silu(r_ref[0])
    y5 = _gelu_erf(s5_ref[0] + s5_d_ref[...] * u_ref[0])
    s5 = y5 * jax.nn.sigmoid(_mm(y5.astype(BF16), glu_w_ref[...]) + glu_b_ref[...])
    mix_in = jnp.concatenate([ssd, gla, mla_ref[0], s5], axis=1).astype(BF16)
    o_ref[0] = h_ref[0] + mod_ref[0] * _mm(mix_in, w_out_ref[...])


def _post(h, p, ssd_xbc, ssd_y, gla_o, mla_y, s5_y, ssd_d, ssd_norm_g, gla_norm_g, s5_d, glu_w, glu_b, w_out, mod,
          row_off, mla_off):
    b, rows, d = h.shape
    w = GROUP_WIDTH
    pblk = lambda col: pl.BlockSpec((1, ROW_TILE, w), lambda i, t: (i, row_off + t, col // w))
    yblk = pl.BlockSpec((1, ROW_TILE, w), lambda i, t: (i, row_off + t, 0))
    full = lambda *shape: pl.BlockSpec(shape, lambda i, t: (0,) * len(shape))
    vec = lambda x: x.reshape(1, -1).astype(F32)
    n_mod = mod.shape[0]
    return pl.pallas_call(
        _post_kernel,
        grid=(b, rows // ROW_TILE),
        in_specs=[pl.BlockSpec((1, ROW_TILE, d), lambda i, t: (i, t, 0)),
                  yblk, pblk(COL_Z), pblk(COL_GLA_R), pblk(COL_S5), yblk, yblk,
                  pl.BlockSpec((1, ROW_TILE, w), lambda i, t: (i, mla_off + t, 0)), yblk,
                  full(1, w), full(1, w), full(1, w), full(1, w), full(w, w), full(1, w), full(d, d),
                  pl.BlockSpec((1, 1, d), lambda i, t: (jnp.minimum(i, n_mod - 1), 0, 0))],
        out_specs=pl.BlockSpec((1, ROW_TILE, d), lambda i, t: (i, t, 0)),
        out_shape=jax.ShapeDtypeStruct((b, rows, d), F32),
        compiler_params=pltpu.CompilerParams(dimension_semantics=("arbitrary", "arbitrary")),
        name="mix_post",
    )(h, ssd_xbc, p, p, p, ssd_y, gla_o, mla_y, s5_y,
      vec(jnp.repeat(ssd_d, SSD_HEAD_DIM)), vec(ssd_norm_g), vec(jnp.tile(gla_norm_g, GLA_HEADS)), vec(s5_d),
      glu_w.astype(BF16), vec(glu_b), w_out.astype(BF16), mod)


PEER_ROUTE_TOKENS = 256
PEER_GATE_TOKENS = 256
PEER_GATE_UNROLL = 8
PEER_GATE_SUBLANES = 8
PEER_DENSE_TOKENS = 512
PEER_DENSE_EXPERTS = 2 * PEER_GATE_SUBLANES * PEER_KEYS
PEER_SLOTS = PEER_HEADS * PEER_TOPK


def _topk_rows(s, k):
    n_rows = s.shape[0]
    rows = lax.broadcasted_iota(jnp.int32, s.shape, 0)
    vals, idxs = [], []
    for _ in range(k):
        m = jnp.max(s, axis=0, keepdims=True)
        idx = jnp.min(jnp.where(s == m, rows, n_rows), axis=0, keepdims=True)
        vals.append(m)
        idxs.append(idx)
        s = jnp.where(rows == idx, -jnp.inf, s)
    return jnp.concatenate(vals, axis=0), jnp.concatenate(idxs, axis=0)


def _select_rows(pos, table):
    out = jnp.zeros(pos.shape, table.dtype)
    for r in range(table.shape[0]):
        out = jnp.where(pos == r, table[r:r + 1, :], out)
    return out


def _peer_route_kernel(h_ref, g_ref, shift_ref, scale_ref, wq_hi_ref, wq_lo_ref, k_hi_ref, k_lo_ref,
                       xn_ref, i1_ref, i2_ref, gate_ref, q_scr):
    xn = _modulated_norm(h_ref[...], g_ref[...], shift_ref[0], scale_ref[0])
    xn_ref[...] = xn.astype(BF16)
    x_hi, x_lo = _split_bf16(xn)
    q_scr[...] = _dot3(wq_hi_ref[...], wq_lo_ref[...], x_hi, x_lo, _NT_DIMS)
    half = PEER_DQ // 2

    def head_body(h, carry):
        base = pl.multiple_of(h * PEER_DQ, PEER_DQ)
        tops = []
        for j in range(2):
            qq = q_scr[pl.ds(base + j * half, half), :]
            q_hi, q_lo = _split_bf16(qq)
            s = _dot3(k_hi_ref[j, h], k_lo_ref[j, h], q_hi, q_lo, _NN_DIMS)
            tops.append(_topk_rows(s, PEER_TOPK))
        (v1, i1), (v2, i2) = tops
        pieces = [v1[a:a + 1, :] + v2[:PEER_TOPK // (a + 1), :] for a in range(PEER_TOPK)]
        n_cand = sum(PEER_TOPK // (a + 1) for a in range(PEER_TOPK))
        pad = -n_cand % 8
        cand = jnp.concatenate(pieces + [jnp.full((pad, v1.shape[1]), -jnp.inf, F32)], axis=0)
        best, pos = _topk_rows(cand, PEER_TOPK)
        e = jnp.exp(best - best[0:1, :])
        gates = e / jnp.sum(e, axis=0, keepdims=True)
        a_idx = jnp.zeros_like(pos)
        start = jnp.zeros_like(pos)
        first = 0
        for a in range(1, PEER_TOPK):
            width = PEER_TOPK // a
            first += width
            reached = pos >= first
            a_idx = a_idx + jnp.where(reached, 1, 0)
            start = start + jnp.where(reached, width, 0)
        row0 = pl.multiple_of(h * PEER_TOPK, PEER_TOPK)
        i1_ref[pl.ds(row0, PEER_TOPK), :] = _select_rows(a_idx, i1)
        i2_ref[pl.ds(row0, PEER_TOPK), :] = _select_rows(pos - start, i2)
        gate_ref[pl.ds(row0, PEER_TOPK), :] = gates
        return carry

    lax.fori_loop(0, PEER_HEADS, head_body, 0)


def _bf16_bits(x):
    return pltpu.bitcast(x.astype(BF16).astype(F32), jnp.uint32)


def _peer_gate_kernel(i1_ref, i2_ref, gate_ref, g_ref):
    rows = lax.broadcasted_iota(jnp.int32, (PEER_KEYS, PEER_SLOTS), 0)
    sub = PEER_GATE_SUBLANES

    def token_body(t, carry):
        a = i1_ref[pl.ds(t, 1), :]
        b = i2_ref[pl.ds(t, 1), :]
        w = gate_ref[pl.ds(t, 1), :]
        lhs = jnp.where(rows == a, w, 0.0).astype(BF16)
        rhs = jnp.where(rows == b, 1.0, 0.0).astype(BF16)
        gt = _mm(lhs, rhs, _NT_DIMS)
        row0 = pl.multiple_of(t * sub, sub)
        for g in range(PEER_KEYS // (2 * sub)):
            lo = gt[2 * sub * g:2 * sub * g + sub]
            hi = gt[2 * sub * g + sub:2 * sub * (g + 1)]
            g_ref[g, pl.ds(row0, sub), :] = (_bf16_bits(lo) >> 16) | _bf16_bits(hi)
        return carry

    lax.fori_loop(0, i1_ref.shape[0], token_body, 0, unroll=PEER_GATE_UNROLL)


def _peer_dense_kernel(xn_ref, u_ref, v_ref, gpk_ref, h_ref, mod_ref, o_ref, acc_ref):
    j = pl.program_id(1)

    @pl.when(j == 0)
    def _():
        acc_ref[...] = jnp.zeros_like(acc_ref)

    sub = PEER_GATE_SUBLANES
    xn = xn_ref[...]
    tokens = xn.shape[0]
    words = [gpk_ref[0, pl.ds(r, tokens, stride=sub), :] for r in range(sub)]
    for half in range(2):
        rows = slice(half * sub * PEER_KEYS, (half + 1) * sub * PEER_KEYS)
        hid = _gelu_erf(_mm(xn, u_ref[rows, :], _NT_DIMS))
        ys = []
        for r in range(sub):
            bits = (words[r] << 16) if half == 0 else (words[r] & jnp.uint32(0xFFFF0000))
            ys.append(pltpu.bitcast(bits, F32) * hid[:, r * PEER_KEYS:(r + 1) * PEER_KEYS])
        y = jnp.concatenate(ys, axis=1).astype(BF16)
        acc_ref[...] += _mm(y, v_ref[rows, :])

    @pl.when(j == pl.num_programs(1) - 1)
    def _():
        o_ref[...] = h_ref[...] + mod_ref[0] * acc_ref[...]


def _peer_layer(h, norm_g, shift, scale, gate_mod, wq_t_hi, wq_t_lo, keys_hi, keys_lo, u_bf, v_bf):
    n, d = h.shape
    nb = shift.shape[0]
    rows_per_batch = n // nb
    tr = min(PEER_ROUTE_TOKENS, rows_per_batch)
    full = lambda *shape: pl.BlockSpec(shape, lambda i: (0,) * len(shape))
    per_batch = lambda t: pl.BlockSpec((1, 1, d), lambda i: (i * t // rows_per_batch, 0, 0))
    xn, i1, i2, gate = pl.pallas_call(
        _peer_route_kernel,
        grid=(n // tr,),
        in_specs=[pl.BlockSpec((tr, d), lambda i: (i, 0)), full(1, d), per_batch(tr), per_batch(tr),
                  full(PEER_HEADS * PEER_DQ, d), full(PEER_HEADS * PEER_DQ, d),
                  full(2, PEER_HEADS, PEER_KEYS, PEER_DQ // 2), full(2, PEER_HEADS, PEER_KEYS, PEER_DQ // 2)],
        out_specs=[pl.BlockSpec((tr, d), lambda i: (i, 0))] + [pl.BlockSpec((PEER_SLOTS, tr), lambda i: (0, i))] * 3,
        out_shape=[jax.ShapeDtypeStruct((n, d), BF16),
                   jax.ShapeDtypeStruct((PEER_SLOTS, n), jnp.int32),
                   jax.ShapeDtypeStruct((PEER_SLOTS, n), jnp.int32),
                   jax.ShapeDtypeStruct((PEER_SLOTS, n), F32)],
        scratch_shapes=[pltpu.VMEM((PEER_HEADS * PEER_DQ, tr), F32)],
        compiler_params=pltpu.CompilerParams(dimension_semantics=("arbitrary",)),
        name="peer_route",
    )(h, norm_g.reshape(1, d), shift, scale, wq_t_hi, wq_t_lo, keys_hi, keys_lo)

    tg = min(PEER_GATE_TOKENS, n)
    n_planes = PEER_KEYS // (2 * PEER_GATE_SUBLANES)
    slot_spec = pl.BlockSpec((tg, PEER_SLOTS), lambda i: (i, 0))
    gmat = pl.pallas_call(
        _peer_gate_kernel,
        grid=(n // tg,),
        in_specs=[slot_spec, slot_spec, slot_spec],
        out_specs=pl.BlockSpec((n_planes, tg * PEER_GATE_SUBLANES, PEER_KEYS), lambda i: (0, i, 0)),
        out_shape=jax.ShapeDtypeStruct((n_planes, n * PEER_GATE_SUBLANES, PEER_KEYS), jnp.uint32),
        compiler_params=pltpu.CompilerParams(dimension_semantics=("arbitrary",)),
        name="peer_gate",
    )(i1.T, i2.T, gate.T)

    tm = min(PEER_DENSE_TOKENS, rows_per_batch)
    te = PEER_DENSE_EXPERTS
    return pl.pallas_call(
        _peer_dense_kernel,
        grid=(n // tm, PEER_EXPERTS // te),
        in_specs=[pl.BlockSpec((tm, d), lambda i, j: (i, 0)),
                  pl.BlockSpec((te, d), lambda i, j: (j, 0)),
                  pl.BlockSpec((te, d), lambda i, j: (j, 0)),
                  pl.BlockSpec((1, tm * PEER_GATE_SUBLANES, PEER_KEYS), lambda i, j: (j, i, 0)),
                  pl.BlockSpec((tm, d), lambda i, j: (i, 0)),
                  pl.BlockSpec((1, 1, d), lambda i, j: (i * tm // rows_per_batch, 0, 0))],
        out_specs=pl.BlockSpec((tm, d), lambda i, j: (i, 0)),
        out_shape=jax.ShapeDtypeStruct((n, d), F32),
        scratch_shapes=[pltpu.VMEM((tm, d), F32)],
        compiler_params=pltpu.CompilerParams(dimension_semantics=("arbitrary", "arbitrary"),
                                             vmem_limit_bytes=52 * 2 ** 20),
        name="peer_dense",
    )(xn, u_bf, v_bf, gmat, h, gate_mod)


def _final_norm_kernel(x_ref, g_ref, o_ref):
    o_ref[...] = _rms(x_ref[...], g_ref[...])


def _final_norm(h, g):
    n = h.shape[0] * h.shape[1]
    x2 = h.reshape(n, D_MODEL)
    tm = 512
    out = pl.pallas_call(
        _final_norm_kernel,
        grid=(n // tm,),
        in_specs=[pl.BlockSpec((tm, D_MODEL), lambda i: (i, 0)),
                  pl.BlockSpec((1, D_MODEL), lambda i: (0, 0))],
        out_specs=pl.BlockSpec((tm, D_MODEL), lambda i: (i, 0)),
        out_shape=jax.ShapeDtypeStruct((n, D_MODEL), F32),
        name="final_norm",
    )(x2, g.reshape(1, D_MODEL))
    return out.reshape(h.shape)


def _mix_layer(h_lat, h_ctx, mod_l, mod_c, norm_g, w_in, w_out, ssd, gla, mla, s5, ctx_out):
    b, n_lat, d = h_lat.shape
    n_lat_tiles = n_lat // ROW_TILE
    hcomb = jnp.concatenate([h_lat, h_ctx], axis=1)
    tab = lambda k: jnp.concatenate([mod_l[k], mod_c[k]], axis=0)
    p = _inproj(hcomb, norm_g, tab(0), tab(1), _pack_w_in(w_in), n_lat_tiles)
    ssd_y, ssd_xbc = _ssd_mixer(p, ssd["conv_w"], ssd["conv_b"], ssd["a_log"], ssd["dt_bias"], n_lat_tiles)
    gla_o = _gla_mixer(p, gla["gate_w"], gla["gate_b"], n_lat_tiles)
    mla_lat, mla_ctx = _mla_mixer(p, mla["q_norm_g"], mla["w_uq"], mla["kv_norm_g"], mla["w_ukv"], n_lat_tiles, ctx_out)
    s5_y = _s5_mixer(p, s5["a_re"], s5["a_im"], s5["log_dt"], s5["b_re"], s5["b_im"], s5["c_re"], s5["c_im"],
                     n_lat_tiles)
    post = functools.partial(_post, p=p, ssd_xbc=ssd_xbc, ssd_y=ssd_y, gla_o=gla_o, s5_y=s5_y, ssd_d=ssd["d"],
                             ssd_norm_g=ssd["norm_g"], gla_norm_g=gla["norm_g"], s5_d=s5["d"],
                             glu_w=s5["glu_w"], glu_b=s5["glu_b"], w_out=w_out)
    new_lat = post(h_lat, mla_y=mla_lat, mod=mod_l[2], row_off=0, mla_off=0)
    new_ctx = None
    if ctx_out:
        new_ctx = post(h_ctx, mla_y=mla_ctx, mod=mod_c[2], row_off=n_lat_tiles, mla_off=0)
    return new_lat, new_ctx


def kernel(x, c, ctx, c_ctx, ada_w, ada_b, norm_mix_g, norm_ffn_g, w_in, w_out,
           ssd_conv_w, ssd_conv_b, ssd_a_log, ssd_dt_bias, ssd_d, ssd_norm_g,
           gla_gate_w, gla_gate_b, gla_norm_g, mla_q_norm_g, mla_w_uq, mla_kv_norm_g, mla_w_ukv,
           s5_a_re, s5_a_im, s5_log_dt, s5_b_re, s5_b_im, s5_c_re, s5_c_im, s5_d, s5_glu_w, s5_glu_b,
           peer_w_q, peer_sub_keys, peer_u, peer_v, final_norm_g):
    h_lat, h_ctx = x, ctx
    cond_lat = jax.nn.silu(c)[:, None, :]
    cond_ctx = jax.nn.silu(c_ctx)[None, None, :]
    for i in range(DEPTH):
        ctx_out = i < DEPTH - 1
        mod_l = jnp.split(cond_lat @ ada_w[i] + ada_b[i], N_MOD, axis=-1)
        mod_c = jnp.split(cond_ctx @ ada_w[i] + ada_b[i], N_MOD, axis=-1)
        ssd = dict(conv_w=ssd_conv_w[i], conv_b=ssd_conv_b[i], a_log=ssd_a_log[i], dt_bias=ssd_dt_bias[i],
                   d=ssd_d[i], norm_g=ssd_norm_g[i])
        gla = dict(gate_w=gla_gate_w[i], gate_b=gla_gate_b[i], norm_g=gla_norm_g[i])
        mla = dict(q_norm_g=mla_q_norm_g[i], w_uq=mla_w_uq[i], kv_norm_g=mla_kv_norm_g[i], w_ukv=mla_w_ukv[i])
        s5 = dict(a_re=s5_a_re[i], a_im=s5_a_im[i], log_dt=s5_log_dt[i], b_re=s5_b_re[i], b_im=s5_b_im[i],
                  c_re=s5_c_re[i], c_im=s5_c_im[i], d=s5_d[i], glu_w=s5_glu_w[i], glu_b=s5_glu_b[i])
        h_lat, h_ctx_new = _mix_layer(h_lat, h_ctx, mod_l, mod_c, norm_mix_g[i], w_in[i], w_out[i],
                                      ssd, gla, mla, s5, ctx_out)
        wq_t_hi, wq_t_lo = _split_bf16(peer_w_q[i].T)
        keys_hi, keys_lo = _split_bf16(peer_sub_keys[i])
        u_bf, v_bf = peer_u[i].astype(BF16), peer_v[i].astype(BF16)
        peer = functools.partial(_peer_layer, norm_g=norm_ffn_g[i], wq_t_hi=wq_t_hi, wq_t_lo=wq_t_lo,
                                 keys_hi=keys_hi, keys_lo=keys_lo, u_bf=u_bf, v_bf=v_bf)
        h_lat = peer(h_lat.reshape(-1, D_MODEL), shift=mod_l[3], scale=mod_l[4],
                     gate_mod=mod_l[5]).reshape(h_lat.shape)
        if ctx_out:
            h_ctx = peer(h_ctx_new.reshape(-1, D_MODEL), shift=mod_c[3], scale=mod_c[4],
                         gate_mod=mod_c[5]).reshape(h_ctx.shape)
    return _final_norm(h_lat, final_norm_g)
```

```python
import functools
import jax
import jax.numpy as jnp
from jax import lax
import numpy as np
from jax.experimental import pallas as pl
from jax.experimental.pallas import tpu as pltpu

D_MODEL = 1024
DEPTH = 2
GRID_W = 64
NORM_EPS = 1e-6
N_MOD = 6

GROUP_WIDTH = D_MODEL // 4

SSD_WIDTH = GROUP_WIDTH
SSD_HEAD_DIM = 64
SSD_HEADS = SSD_WIDTH // SSD_HEAD_DIM
SSD_GROUPS = 2
SSD_STATE = 128
SSD_CONV = 5
SSD_CHUNK = 128
SSD_CONV_CH = SSD_WIDTH + 2 * SSD_GROUPS * SSD_STATE
SSD_IN = SSD_WIDTH + SSD_CONV_CH + 2 * SSD_HEADS

GLA_WIDTH = GROUP_WIDTH
GLA_HEADS = 4
GLA_DV = GLA_WIDTH // GLA_HEADS
GLA_DK = GLA_DV // 2
GLA_QK = GLA_HEADS * GLA_DK
GLA_GATE_RANK = 16
GLA_TAU = 16.0
GLA_CHUNK = 64
GLA_IN = 2 * GLA_QK + 2 * GLA_WIDTH + 2 * GLA_GATE_RANK

MLA_WIDTH = GROUP_WIDTH
MLA_HEADS = 4
MLA_V = MLA_WIDTH // MLA_HEADS
MLA_NOPE = 64
MLA_ROPE = 32
MLA_Q_RANK = 256
MLA_KV_RANK = 128
MLA_SCALE = (MLA_NOPE + MLA_ROPE) ** -0.5
ROPE_BASE = 10000.0
MLA_IN = MLA_Q_RANK + MLA_KV_RANK + MLA_ROPE

S5_WIDTH = GROUP_WIDTH
S5_GROUP = 16
S5_NGROUPS = S5_WIDTH // S5_GROUP
S5_STATE = 64
S5_MAX_RE = -1e-4
S5_IN = S5_WIDTH
S5_CHUNK = 16
S5_PAIRS = S5_NGROUPS // 2

PEER_KEYS = 128
PEER_EXPERTS = PEER_KEYS * PEER_KEYS
PEER_HEADS = 8
PEER_TOPK = 16
PEER_DQ = 128

LANES = 128
ROW_TILE = 256

F32 = jnp.float32
BF16 = jnp.bfloat16

COL_XS, COL_BM, COL_CM, COL_Z = 0, 256, 512, 768
COL_GLA_V, COL_GLA_R, COL_CQ, COL_S5 = 1024, 1280, 1536, 1792
COL_GLA_Q, COL_GLA_K, COL_CKV, COL_DT, COL_GLR, COL_KR, COL_KRROT = 2048, 2176, 2304, 2432, 2560, 2688, 2816
P_COLS = 2944

_NN_DIMS = (((1,), (0,)), ((), ()))
_NT_DIMS = (((1,), (1,)), ((), ()))
_TN_DIMS = (((0,), (0,)), ((), ()))


def _mm(a, b, dims=_NN_DIMS):
    return lax.dot_general(a, b, dims, preferred_element_type=F32)


def _split_bf16(x):
    hi = x.astype(BF16)
    lo = (x - hi.astype(F32)).astype(BF16)
    return hi, lo


def _split3_bf16(x):
    p1 = x.astype(BF16)
    r1 = x - p1.astype(F32)
    p2 = r1.astype(BF16)
    p3 = (r1 - p2.astype(F32)).astype(BF16)
    return p1, p2, p3


def _dot3(a_hi, a_lo, b_hi, b_lo, dims):
    return _mm(a_hi, b_hi, dims) + _mm(a_hi, b_lo, dims) + _mm(a_lo, b_hi, dims)


def _gelu_erf(x):
    return 0.5 * x * (1.0 + lax.erf(x * (2.0 ** -0.5)))


def _silu(x):
    return x * jax.nn.sigmoid(x)


def _softplus(x):
    return jnp.maximum(x, 0.0) + jnp.log1p(jnp.exp(-jnp.abs(x)))


def _log_sigmoid(x):
    return jnp.minimum(x, 0.0) - jnp.log1p(jnp.exp(-jnp.abs(x)))


def _rms(x, g):
    return x * lax.rsqrt(jnp.mean(x * x, axis=-1, keepdims=True) + NORM_EPS) * g


def _modulated_norm(x, g, shift, scale):
    return _rms(x, g) * (1.0 + scale) + shift


def _causal_mask(n, reverse):
    ri = lax.broadcasted_iota(jnp.int32, (n, n), 0)
    ci = lax.broadcasted_iota(jnp.int32, (n, n), 1)
    return (ci >= ri) if reverse else (ci <= ri)


def _scan_chunk(s, n_lat, n_ctx, reverse):
    if reverse:
        return n_lat + n_ctx - 1 - s
    return jnp.where(s < n_ctx, n_lat + s, s - n_ctx)


def _inproj_kernel(h_ref, g_ref, shift_ref, scale_ref, w_ref, o_ref):
    xn = _modulated_norm(h_ref[0], g_ref[...], shift_ref[0], scale_ref[0])
    o_ref[0] = _mm(xn.astype(BF16), w_ref[...])


def _inproj(hcomb, norm_g, shift_tab, scale_tab, w_pad, n_lat_tiles):
    b, r, d = hcomb.shape
    mod_spec = pl.BlockSpec((1, 1, d), lambda i, t: (jnp.where(t < n_lat_tiles, i, b), 0, 0))
    return pl.pallas_call(
        _inproj_kernel,
        grid=(b, r // ROW_TILE),
        in_specs=[pl.BlockSpec((1, ROW_TILE, d), lambda i, t: (i, t, 0)),
                  pl.BlockSpec((1, d), lambda i, t: (0, 0)), mod_spec, mod_spec,
                  pl.BlockSpec((d, P_COLS), lambda i, t: (0, 0))],
        out_specs=pl.BlockSpec((1, ROW_TILE, P_COLS), lambda i, t: (i, t, 0)),
        out_shape=jax.ShapeDtypeStruct((b, r, P_COLS), F32),
        compiler_params=pltpu.CompilerParams(dimension_semantics=("arbitrary", "arbitrary"),
                                             vmem_limit_bytes=48 * 2 ** 20),
        name="inproj",
    )(hcomb, norm_g.reshape(1, d), shift_tab, scale_tab, w_pad)


def _pack_w_in(w):
    o_ssd, o_gla, o_mla, o_s5 = 0, SSD_IN, SSD_IN + GLA_IN, SSD_IN + GLA_IN + MLA_IN
    out = jnp.zeros((w.shape[0], P_COLS), F32)
    put = lambda out, col, src, width: out.at[:, col:col + width].set(w[:, src:src + width])
    out = put(out, COL_Z, o_ssd, SSD_WIDTH)
    out = put(out, COL_XS, o_ssd + SSD_WIDTH, SSD_CONV_CH)
    out = put(out, COL_DT, o_ssd + SSD_WIDTH + SSD_CONV_CH, 2 * SSD_HEADS)
    out = put(out, COL_GLA_Q, o_gla, GLA_QK)
    out = put(out, COL_GLA_K, o_gla + GLA_QK, GLA_QK)
    out = put(out, COL_GLA_V, o_gla + 2 * GLA_QK, GLA_WIDTH)
    out = put(out, COL_GLA_R, o_gla + 2 * GLA_QK + GLA_WIDTH, GLA_WIDTH)
    out = put(out, COL_GLR, o_gla + 2 * GLA_QK + 2 * GLA_WIDTH, 2 * GLA_GATE_RANK)
    out = put(out, COL_CQ, o_mla, MLA_Q_RANK)
    out = put(out, COL_CKV, o_mla + MLA_Q_RANK, MLA_KV_RANK)
    o_kr = o_mla + MLA_Q_RANK + MLA_KV_RANK
    half = MLA_ROPE // 2
    out = put(out, COL_KR + MLA_NOPE, o_kr, MLA_ROPE)
    out = out.at[:, COL_KRROT + MLA_NOPE:COL_KRROT + MLA_NOPE + half].set(-w[:, o_kr + half:o_kr + MLA_ROPE])
    out = out.at[:, COL_KRROT + MLA_NOPE + half:COL_KRROT + MLA_NOPE + MLA_ROPE].set(w[:, o_kr:o_kr + half])
    out = put(out, COL_S5, o_s5, S5_WIDTH)
    return out.astype(BF16)


def _ssd_prep_kernel(x_ref, prev_ref, next_ref, dt_ref, w_ref, b_ref, bias_ref, xbc_ref, dtc_ref, dtt_ref,
                     *, n_lat_tiles):
    t = pl.program_id(1)
    x = x_ref[0]
    halo = prev_ref.shape[1]
    prev = jnp.where(jnp.logical_and(t > 0, t < n_lat_tiles), prev_ref[0], 0.0)
    nxt = jnp.where(t < n_lat_tiles - 1, next_ref[0], 0.0)
    ext = jnp.concatenate([prev, x, nxt], axis=0)
    rows = ext.shape[0]
    left = SSD_CONV // 2
    acc = jnp.zeros_like(x) + b_ref[...]
    for k in range(SSD_CONV):
        shifted = ext if k == left else pltpu.roll(ext, (left - k) % rows, 0)
        acc = acc + w_ref[k:k + 1, :] * shifted[halo:halo + x.shape[0]]
    xbc_ref[0] = _silu(acc)
    dt = _softplus(dt_ref[0] + bias_ref[...])
    dtc_ref[0] = dt
    dtt_ref[0] = dt.T[:dtt_ref.shape[1]]


def _ssd_scan_kernel(*refs, direction, n_lat, n_ctx, has_prev):
    if has_prev:
        xbc_ref, dtc_ref, dtt_ref, ahr_ref, ahc_ref, yprev_ref, y_ref, state_ref = refs
    else:
        xbc_ref, dtc_ref, dtt_ref, ahr_ref, ahc_ref, y_ref, state_ref = refs
    reverse = direction == 1
    s = pl.program_id(1)

    @pl.when(s == 0)
    def _():
        state_ref[...] = jnp.zeros_like(state_ref)

    q = SSD_CHUNK
    mask = _causal_mask(q, reverse)
    tri = jnp.where(mask, 1.0, 0.0).astype(BF16)
    xbc = xbc_ref[0]
    xs, bm, cm = xbc[:, :SSD_WIDTH], xbc[:, SSD_WIDTH:SSD_WIDTH + 256], xbc[:, SSD_WIDTH + 256:]
    dtc = dtc_ref[0]
    a_col = dtc * ahr_ref[...]
    a_row = dtt_ref[0] * ahc_ref[...]
    acum_col = sum(_mm(tri, part) for part in _split3_bf16(a_col))
    acum_row = sum(_mm(part, tri, _NT_DIMS) for part in _split3_bf16(a_row))
    end = 0 if reverse else q - 1
    bm_bf, cm_bf = bm.astype(BF16), cm.astype(BF16)
    ys = []
    cb = {}
    for h in range(SSD_HEADS):
        g = h // (SSD_HEADS // SSD_GROUPS)
        gs = slice(g * SSD_STATE, (g + 1) * SSD_STATE)
        if g not in cb:
            cb[g] = _mm(cm_bf[:, gs], bm_bf[:, gs], _NT_DIMS)
        ch = direction * SSD_HEADS + h
        ac = acum_col[:, ch:ch + 1]
        ar = acum_row[ch:ch + 1, :]
        decay = jnp.exp(jnp.where(mask, ac - ar, -jnp.inf))
        xd = xs[:, h * SSD_HEAD_DIM:(h + 1) * SSD_HEAD_DIM] * dtc[:, ch:ch + 1]
        y_diag = _mm((cb[g] * decay).astype(BF16), xd.astype(BF16))
        a_end = ac[end:end + 1, :]
        st_local = _mm((xd * jnp.exp(a_end - ac)).astype(BF16), bm_bf[:, gs], _TN_DIMS)
        hs = state_ref[h]
        y_off = jnp.exp(ac) * _mm(cm_bf[:, gs], hs.astype(BF16), _NT_DIMS)
        state_ref[h] = jnp.exp(a_end) * hs + st_local
        ys.append(y_diag + y_off)
    y = jnp.concatenate(ys, axis=1)
    if has_prev:
        y = y + yprev_ref[0]
    y_ref[0] = y


def _ssd_mixer(p, conv_w, conv_b, a_log, dt_bias, n_lat_tiles):
    b, r, _ = p.shape
    nt = r // ROW_TILE
    halo = 8
    hb = ROW_TILE // halo
    w8 = jnp.zeros((8, SSD_CONV_CH), F32).at[:SSD_CONV].set(conv_w)
    bias = jnp.zeros((1, LANES), F32).at[0, :2 * SSD_HEADS].set(dt_bias.reshape(-1))
    xbc, dtc, dtt = pl.pallas_call(
        functools.partial(_ssd_prep_kernel, n_lat_tiles=n_lat_tiles),
        grid=(b, nt),
        in_specs=[pl.BlockSpec((1, ROW_TILE, SSD_CONV_CH), lambda i, t: (i, t, 0)),
                  pl.BlockSpec((1, halo, SSD_CONV_CH), lambda i, t: (i, jnp.maximum(t * hb - 1, 0), 0)),
                  pl.BlockSpec((1, halo, SSD_CONV_CH), lambda i, t: (i, jnp.minimum((t + 1) * hb, nt * hb - 1), 0)),
                  pl.BlockSpec((1, ROW_TILE, LANES), lambda i, t: (i, t, COL_DT // LANES)),
                  pl.BlockSpec((8, SSD_CONV_CH), lambda i, t: (0, 0)),
                  pl.BlockSpec((1, SSD_CONV_CH), lambda i, t: (0, 0)),
                  pl.BlockSpec((1, LANES), lambda i, t: (0, 0))],
        out_specs=[pl.BlockSpec((1, ROW_TILE, SSD_CONV_CH), lambda i, t: (i, t, 0)),
                   pl.BlockSpec((1, ROW_TILE, LANES), lambda i, t: (i, t, 0)),
                   pl.BlockSpec((1, 8, ROW_TILE), lambda i, t: (i, 0, t))],
        out_shape=[jax.ShapeDtypeStruct((b, r, SSD_CONV_CH), F32),
                   jax.ShapeDtypeStruct((b, r, LANES), F32),
                   jax.ShapeDtypeStruct((b, 8, r), F32)],
        compiler_params=pltpu.CompilerParams(dimension_semantics=("arbitrary", "arbitrary")),
        name="ssd_prep",
    )(p, p, p, p, w8, conv_b.reshape(1, -1), bias)

    a_head = -jnp.exp(a_log.astype(F32)).reshape(-1)
    ahr = jnp.zeros((1, LANES), F32).at[0, :2 * SSD_HEADS].set(a_head)
    ahc = a_head.reshape(2 * SSD_HEADS, 1)
    n_lat = n_lat_tiles * ROW_TILE // SSD_CHUNK
    n_ctx = r // SSD_CHUNK - n_lat
    y = None
    for direction in (0, 1):
        cidx = functools.partial(_scan_chunk, n_lat=n_lat, n_ctx=n_ctx, reverse=direction == 1)
        in_specs = [pl.BlockSpec((1, SSD_CHUNK, SSD_CONV_CH), lambda i, s: (i, cidx(s), 0)),
                    pl.BlockSpec((1, SSD_CHUNK, LANES), lambda i, s: (i, cidx(s), 0)),
                    pl.BlockSpec((1, 8, SSD_CHUNK), lambda i, s: (i, 0, cidx(s))),
                    pl.BlockSpec((1, LANES), lambda i, s: (0, 0)),
                    pl.BlockSpec((2 * SSD_HEADS, 1), lambda i, s: (0, 0))]
        args = [xbc, dtc, dtt, ahr, ahc]
        y_spec = pl.BlockSpec((1, SSD_CHUNK, SSD_WIDTH), lambda i, s: (i, cidx(s), 0))
        if y is not None:
            in_specs.append(y_spec)
            args.append(y)
        y = pl.pallas_call(
            functools.partial(_ssd_scan_kernel, direction=direction, n_lat=n_lat, n_ctx=n_ctx,
                              has_prev=y is not None),
            grid=(b, n_lat + n_ctx),
            in_specs=in_specs,
            out_specs=y_spec,
            out_shape=jax.ShapeDtypeStruct((b, r, SSD_WIDTH), F32),
            scratch_shapes=[pltpu.VMEM((SSD_HEADS, SSD_HEAD_DIM, SSD_STATE), F32)],
            compiler_params=pltpu.CompilerParams(dimension_semantics=("arbitrary", "arbitrary")),
            name=f"ssd_scan_{direction}",
        )(*args)
    return y, xbc


def _gla_scan_kernel(*refs, direction, has_prev):
    if has_prev:
        q_ref, k_ref, v_ref, glr_ref, wg_ref, bias_ref, oprev_ref, o_ref, st_ref = refs
    else:
        q_ref, k_ref, v_ref, glr_ref, wg_ref, bias_ref, o_ref, st_ref = refs
    reverse = direction == 1
    s = pl.program_id(1)

    @pl.when(s == 0)
    def _():
        st_ref[...] = jnp.zeros_like(st_ref)

    n = GLA_CHUNK
    mask = _causal_mask(n, reverse)
    tri = jnp.where(mask, 1.0, 0.0).astype(BF16)
    g_hi, g_lo = _split_bf16(glr_ref[0])
    logits = _dot3(g_hi, g_lo, wg_ref[0], wg_ref[1], _NN_DIMS) + bias_ref[...]
    logg = _log_sigmoid(logits) * (1.0 / GLA_TAU)
    bcum = sum(_mm(tri, part) for part in _split3_bf16(logg))
    end = 0 if reverse else n - 1
    b_end = bcum[end:end + 1, :]
    q, k, v = q_ref[0], k_ref[0], v_ref[0]
    qe = q * jnp.exp(bcum) * (GLA_DK ** -0.5)
    ke = (k * jnp.exp(-bcum)).astype(BF16)
    kd = k * jnp.exp(b_end - bcum)
    decay_end = jnp.exp(b_end)
    lane_head = lax.broadcasted_iota(jnp.int32, (1, GLA_QK), 1) >> (GLA_DK.bit_length() - 1)
    outs = []
    for h in range(GLA_HEADS):
        hm = lane_head == h
        qh = jnp.where(hm, qe, 0.0).astype(BF16)
        att = jnp.where(mask, _mm(qh, ke, _NT_DIMS), 0.0)
        vh = v[:, h * GLA_DV:(h + 1) * GLA_DV].astype(BF16)
        st = st_ref[h]
        o_h = _mm(att.astype(BF16), vh) + _mm(qh, st.astype(BF16), _NT_DIMS)
        local = _mm(vh, jnp.where(hm, kd, 0.0).astype(BF16), _TN_DIMS)
        st_ref[h] = st * decay_end + local
        outs.append(o_h)
    o = jnp.concatenate(outs, axis=1)
    if has_prev:
        o = o + oprev_ref[0]
    o_ref[0] = o


def _gla_mixer(p, gate_w, gate_b, n_lat_tiles):
    b, r, _ = p.shape
    n_lat = n_lat_tiles * ROW_TILE // GLA_CHUNK
    n_ctx = r // GLA_CHUNK - n_lat
    o = None
    for direction in (0, 1):
        cidx = functools.partial(_scan_chunk, n_lat=n_lat, n_ctx=n_ctx, reverse=direction == 1)
        wg = jnp.zeros((LANES, GLA_QK), F32).at[direction * GLA_GATE_RANK:(direction + 1) * GLA_GATE_RANK].set(
            gate_w[direction])
        wg = jnp.stack(_split_bf16(wg))
        blk = lambda width, col: pl.BlockSpec((1, GLA_CHUNK, width), lambda i, s: (i, cidx(s), col // width))
        in_specs = [blk(GLA_QK, COL_GLA_Q), blk(GLA_QK, COL_GLA_K), blk(GLA_WIDTH, COL_GLA_V), blk(LANES, COL_GLR),
                    pl.BlockSpec((2, LANES, GLA_QK), lambda i, s: (0, 0, 0)),
                    pl.BlockSpec((1, GLA_QK), lambda i, s: (0, 0))]
        args = [p, p, p, p, wg, gate_b[direction].reshape(1, -1)]
        o_spec = pl.BlockSpec((1, GLA_CHUNK, GLA_WIDTH), lambda i, s: (i, cidx(s), 0))
        if o is not None:
            in_specs.append(o_spec)
            args.append(o)
        o = pl.pallas_call(
            functools.partial(_gla_scan_kernel, direction=direction, has_prev=o is not None),
            grid=(b, n_lat + n_ctx),
            in_specs=in_specs,
            out_specs=o_spec,
            out_shape=jax.ShapeDtypeStruct((b, r, GLA_WIDTH), F32),
            scratch_shapes=[pltpu.VMEM((GLA_HEADS, GLA_DV, GLA_QK), F32)],
            compiler_params=pltpu.CompilerParams(dimension_semantics=("arbitrary", "arbitrary")),
            name=f"gla_scan_{direction}",
        )(*args)
    return o


MLA_Q_TILE = 1024
MLA_K_TILE = 256


def _mla_prep_kernel(cq_ref, ckv_ref, kr_ref, krrot_ref, onec_ref, sinr_ref, gq_ref, gkv_ref,
                     wq_ref, wqr_ref, wk_ref, wv_ref, q_ref, k_ref, v_ref):
    qn = _rms(cq_ref[0], gq_ref[...]).astype(BF16)
    kvn = _rms(ckv_ref[0], gkv_ref[...]).astype(BF16)
    onec, sinr = onec_ref[...], sinr_ref[...]
    k_rope = kr_ref[0] * onec + krrot_ref[0] * sinr
    ones_lane = jnp.where(lax.broadcasted_iota(jnp.int32, (1, LANES), 1) == MLA_V, 1.0, 0.0)
    for h in range(MLA_HEADS):
        qh = _mm(qn, wq_ref[h]) * onec + _mm(qn, wqr_ref[h]) * sinr
        q_ref[0, h] = (qh * MLA_SCALE).astype(BF16)
        k_ref[0, h] = (_mm(kvn, wk_ref[h]) + k_rope).astype(BF16)
        v_ref[0, h] = (_mm(kvn, wv_ref[h]) + ones_lane).astype(BF16)


def _mla_attn_kernel(q_ref, k_ref, v_ref, o_ref, m_ref, acc_ref):
    j = pl.program_id(2)

    @pl.when(j == 0)
    def _():
        m_ref[...] = jnp.full_like(m_ref, -jnp.inf)
        acc_ref[...] = jnp.zeros_like(acc_ref)

    reps = k_ref.shape[2] // LANES
    for h in range(MLA_HEADS):
        s = _mm(q_ref[0, h], k_ref[0, h], _NT_DIMS)
        m_prev = m_ref[h]
        m_new = jnp.maximum(m_prev, jnp.max(s, axis=1, keepdims=True))
        p = jnp.exp(s - jnp.concatenate([m_new] * reps, axis=1))
        acc_ref[h] = jnp.exp(m_prev - m_new) * acc_ref[h] + _mm(p.astype(BF16), v_ref[0, h])
        m_ref[h] = m_new

    @pl.when(j == pl.num_programs(2) - 1)
    def _():
        outs = []
        for h in range(MLA_HEADS):
            acc = acc_ref[h]
            outs.append(acc[:, :MLA_V] / acc[:, MLA_V:MLA_V + 1])
        o_ref[0] = jnp.concatenate(outs, axis=1)


def _rope_tables(n_lat, n_rows):
    rows = n_lat // GRID_W
    row = jnp.repeat(jnp.arange(rows, dtype=F32), GRID_W)
    col = jnp.tile(jnp.arange(GRID_W, dtype=F32), rows)
    half = MLA_ROPE // 2
    inv = ROPE_BASE ** (-jnp.arange(0, half, 2, dtype=F32) / half)
    ang = jnp.concatenate([row[:, None] * inv, col[:, None] * inv], axis=-1)
    cos = jnp.concatenate([jnp.cos(ang), jnp.ones((n_rows - n_lat, half), F32)], axis=0)
    sin = jnp.concatenate([jnp.sin(ang), jnp.zeros((n_rows - n_lat, half), F32)], axis=0)
    pad = jnp.zeros((n_rows, LANES - MLA_NOPE - MLA_ROPE), F32)
    onec = jnp.concatenate([jnp.ones((n_rows, MLA_NOPE), F32), cos, cos, pad], axis=1)
    sinr = jnp.concatenate([jnp.zeros((n_rows, MLA_NOPE), F32), sin, sin, pad], axis=1)
    return onec, sinr


def _mla_weights(w_uq, w_ukv):
    dqk = MLA_NOPE + MLA_ROPE
    half = MLA_ROPE // 2
    wq = w_uq.reshape(MLA_Q_RANK, MLA_HEADS, dqk).transpose(1, 0, 2)
    rot = jnp.concatenate([jnp.zeros_like(wq[..., :MLA_NOPE]), -wq[..., MLA_NOPE + half:], wq[..., MLA_NOPE:MLA_NOPE + half]],
                          axis=-1)
    padq = lambda w: jnp.pad(w, ((0, 0), (0, 0), (0, LANES - dqk))).astype(BF16)
    wkv = w_ukv.reshape(MLA_KV_RANK, MLA_HEADS, MLA_NOPE + MLA_V).transpose(1, 0, 2)
    padk = lambda w: jnp.pad(w, ((0, 0), (0, 0), (0, LANES - w.shape[-1]))).astype(BF16)
    return padq(wq), padq(rot), padk(wkv[..., :MLA_NOPE]), padk(wkv[..., MLA_NOPE:])


def _mla_attention(q, k, v, q_tile, q_off, n_q, k_off, n_k):
    b = q.shape[0]
    kt = MLA_K_TILE
    return pl.pallas_call(
        _mla_attn_kernel,
        grid=(b, n_q, n_k),
        in_specs=[pl.BlockSpec((1, MLA_HEADS, q_tile, LANES), lambda i, a, j: (i, 0, q_off + a, 0)),
                  pl.BlockSpec((1, MLA_HEADS, kt, LANES), lambda i, a, j: (i, 0, k_off + j, 0)),
                  pl.BlockSpec((1, MLA_HEADS, kt, LANES), lambda i, a, j: (i, 0, k_off + j, 0))],
        out_specs=pl.BlockSpec((1, q_tile, MLA_WIDTH), lambda i, a, j: (i, a, 0)),
        out_shape=jax.ShapeDtypeStruct((b, n_q * q_tile, MLA_WIDTH), F32),
        scratch_shapes=[pltpu.VMEM((MLA_HEADS, q_tile, LANES), F32), pltpu.VMEM((MLA_HEADS, q_tile, LANES), F32)],
        compiler_params=pltpu.CompilerParams(dimension_semantics=("arbitrary", "arbitrary", "arbitrary")),
        name="mla_attn",
    )(q, k, v)


def _mla_mixer(p, q_norm_g, w_uq, kv_norm_g, w_ukv, n_lat_tiles, ctx_out):
    b, r, _ = p.shape
    nt = r // ROW_TILE
    n_lat = n_lat_tiles * ROW_TILE
    onec, sinr = _rope_tables(n_lat, r)
    wq, wqr, wk, wv = _mla_weights(w_uq, w_ukv)
    blk = lambda width, col: pl.BlockSpec((1, ROW_TILE, width), lambda i, t: (i, t, col // width))
    tab = pl.BlockSpec((ROW_TILE, LANES), lambda i, t: (t, 0))
    full = lambda *shape: pl.BlockSpec(shape, lambda i, t: (0,) * len(shape))
    head_out = pl.BlockSpec((1, MLA_HEADS, ROW_TILE, LANES), lambda i, t: (i, 0, t, 0))
    q, k, v = pl.pallas_call(
        _mla_prep_kernel,
        grid=(b, nt),
        in_specs=[blk(MLA_Q_RANK, COL_CQ), blk(LANES, COL_CKV), blk(LANES, COL_KR), blk(LANES, COL_KRROT), tab, tab,
                  full(1, MLA_Q_RANK), full(1, MLA_KV_RANK),
                  full(MLA_HEADS, MLA_Q_RANK, LANES), full(MLA_HEADS, MLA_Q_RANK, LANES),
                  full(MLA_HEADS, MLA_KV_RANK, LANES), full(MLA_HEADS, MLA_KV_RANK, LANES)],
        out_specs=[head_out] * 3,
        out_shape=[jax.ShapeDtypeStruct((b, MLA_HEADS, r, LANES), BF16)] * 3,
        compiler_params=pltpu.CompilerParams(dimension_semantics=("arbitrary", "arbitrary")),
        name="mla_prep",
    )(p, p, p, p, onec, sinr, q_norm_g.reshape(1, -1), kv_norm_g.reshape(1, -1), wq, wqr, wk, wv)
    q_tile = min(MLA_Q_TILE, n_lat)
    y_lat = _mla_attention(q, k, v, q_tile, 0, n_lat // q_tile, 0, r // MLA_K_TILE)
    y_ctx = None
    if ctx_out:
        n_ctx = r - n_lat
        y_ctx = _mla_attention(q, k, v, n_ctx, n_lat // n_ctx, 1, n_lat // MLA_K_TILE, n_ctx // MLA_K_TILE)
    return y_lat, y_ctx


def _s5_matrices(a_re, a_im, log_dt, b_re, b_im, c_re, c_im):
    q, ng, ns, nc = S5_CHUNK, S5_NGROUPS, S5_STATE, S5_GROUP
    lam = jnp.minimum(a_re.astype(F32), S5_MAX_RE) + 1j * a_im.astype(F32)
    step = jnp.exp(log_dt.astype(F32))[..., None]
    abar = jnp.exp(lam * step)
    bmat = b_re.astype(F32) + 1j * b_im.astype(F32)
    bbar = ((abar - 1.0) / lam)[..., None] * bmat
    cmat = c_re.astype(F32) + 1j * c_im.astype(F32)
    pw = jnp.exp((lam * step)[..., None] * jnp.arange(q + 1, dtype=F32))
    kern = jnp.einsum('dgcn,dgnl,dgnk->dglck', cmat, pw[..., :q], bbar).real
    ii = jnp.arange(q)
    lag_f = ii[None, :] - ii[:, None]
    gather = lambda kd, lag: jnp.where((lag >= 0)[None, :, :, None, None], kd[:, jnp.clip(lag, 0, q - 1)], 0.0)
    t_f = gather(kern[0], lag_f).transpose(0, 1, 4, 2, 3)
    t_b = gather(kern[1], -lag_f).transpose(0, 1, 4, 2, 3)
    t_sum = (t_f + t_b).reshape(ng, q * nc, q * nc)
    pw_f = pw[0][..., q - 1 - ii]
    pw_b = pw[1][..., ii]
    wst = lambda pwd, bb: jnp.einsum('gnj,gnc->gjcn', pwd, bb).reshape(ng, q * nc, ns)
    wst_f, wst_b = wst(pw_f, bbar[0]), wst(pw_b, bbar[1])
    wout = lambda pwd, cm: jnp.einsum('gcn,gni->gnic', cm, pwd).reshape(ng, ns, q * nc)
    wo_f, wo_b = wout(pw[0][..., ii + 1], cmat[0]), wout(pw[1][..., q - ii], cmat[1])
    aq = pw[..., q]

    def pair_cols(x):
        x = x.reshape(S5_PAIRS, 2, x.shape[1], x.shape[2])
        z = jnp.zeros_like(x[:, 0])
        return jnp.concatenate([jnp.concatenate([x[:, 0], z], axis=2), jnp.concatenate([z, x[:, 1]], axis=2)], axis=1)

    w_local = jnp.concatenate([pair_cols(wst_f.real), pair_cols(wst_f.imag),
                               pair_cols(wst_b.real), pair_cols(wst_b.imag)], axis=2)
    w_out = jnp.concatenate([pair_cols(t_sum), pair_cols(wo_f.real), pair_cols(-wo_f.imag),
                             pair_cols(wo_b.real), pair_cols(-wo_b.imag)], axis=1)
    aq_pair = aq.reshape(2, S5_PAIRS, 2 * ns)
    aq_tab = jnp.concatenate([aq_pair[0].real, aq_pair[0].imag, aq_pair[1].real, aq_pair[1].imag], axis=1)
    eye = jnp.eye(S5_PAIRS, dtype=BF16)
    kw = 2 * q * nc
    wl = w_local.astype(BF16).reshape(S5_PAIRS, 2, q, nc, kw)
    w_local_big = jnp.einsum('pajck,pq->jpacqk', wl, eye).reshape(q * S5_WIDTH, S5_PAIRS * kw)
    wu = w_out[:, :kw].astype(BF16).reshape(S5_PAIRS, 2, q, nc, 2, q, nc)
    w_u_big = jnp.einsum('pajcbid,pq->jpaciqbd', wu, eye).reshape(q * S5_WIDTH, q * S5_WIDTH)
    wh = w_out[:, kw:].astype(BF16).reshape(S5_PAIRS, kw, 2, q, nc)
    w_h_big = jnp.einsum('pkbid,pq->pkiqbd', wh, eye).reshape(S5_PAIRS * kw, q * S5_WIDTH)
    def half_major(x, axis):
        shp = x.shape
        x = x.reshape(shp[:axis] + (q, S5_WIDTH // LANES, LANES) + shp[axis + 1:])
        return jnp.swapaxes(x, axis, axis + 1).reshape(shp)

    w_local_big = half_major(w_local_big, 0)
    w_u_big = half_major(half_major(w_u_big, 0), 1)
    w_h_big = half_major(w_h_big, 1)
    return w_local_big, w_u_big, w_h_big, aq_tab.reshape(S5_PAIRS, 1, 8 * ns).astype(F32)


def _s5_pack_kernel(u_ref, o_ref):
    n = o_ref.shape[1]
    for j in range(S5_CHUNK):
        o_ref[0, :, j * LANES:(j + 1) * LANES] = u_ref[0, pl.ds(j, n, stride=S5_CHUNK), :].astype(BF16)


def _s5_unpack_kernel(y_ref, o_ref):
    n = y_ref.shape[1]
    for i in range(S5_CHUNK):
        o_ref[0, pl.ds(i, n, stride=S5_CHUNK), :] = y_ref[0, :, i * LANES:(i + 1) * LANES]


def _s5_local_kernel(u_ref, w_ref, s_ref):
    s_ref[0] = _mm(u_ref[0], w_ref[...])


def _s5_scan_kernel(s3_ref, aq_ref, hs3_ref, *, n_lat, n_ctx, nb):
    s_ref, hs_ref = s3_ref.at[0], hs3_ref.at[0]
    w = 2 * S5_STATE
    aq = aq_ref[0]
    a = [aq[:, i * w:(i + 1) * w] for i in range(4)]
    zero = jnp.zeros((nb, w), F32)
    slab = 8
    cps = slab // nb

    def run_slab(s_re, s_im, a_re, a_im, h_re, h_im, order):
        ent_re, ent_im = [None] * cps, [None] * cps
        for c in order:
            ent_re[c], ent_im[c] = h_re, h_im
            rows = slice(c * nb, (c + 1) * nb)
            h_re, h_im = a_re * h_re - a_im * h_im + s_re[rows], a_re * h_im + a_im * h_re + s_im[rows]
        return jnp.concatenate(ent_re, axis=0), jnp.concatenate(ent_im, axis=0), h_re, h_im

    def body(kk, carry):
        f_re, f_im, b_re, b_im = carry
        rf = pl.multiple_of(_scan_chunk(kk, n_lat // cps, n_ctx // cps, False) * slab, slab)
        rb = pl.multiple_of(_scan_chunk(kk, n_lat // cps, n_ctx // cps, True) * slab, slab)
        e_re, e_im, f_re, f_im = run_slab(s_ref[pl.ds(rf, slab), 0:w], s_ref[pl.ds(rf, slab), w:2 * w],
                                          a[0], a[1], f_re, f_im, range(cps))
        hs_ref[pl.ds(rf, slab), 0:w] = e_re
        hs_ref[pl.ds(rf, slab), w:2 * w] = e_im
        e_re, e_im, b_re, b_im = run_slab(s_ref[pl.ds(rb, slab), 2 * w:3 * w], s_ref[pl.ds(rb, slab), 3 * w:4 * w],
                                          a[2], a[3], b_re, b_im, range(cps - 1, -1, -1))
        hs_ref[pl.ds(rb, slab), 2 * w:3 * w] = e_re
        hs_ref[pl.ds(rb, slab), 3 * w:4 * w] = e_im
        return f_re, f_im, b_re, b_im

    lax.fori_loop(0, (n_lat + n_ctx) // cps, body, (zero, zero, zero, zero))


def _s5_out_kernel(u_ref, hs_ref, wu_ref, wh_ref, y_ref):
    y_ref[0] = _mm(u_ref[0], wu_ref[...]) + _mm(hs_ref[0].astype(BF16), wh_ref[...])


def _s5_mixer(p, a_re, a_im, log_dt, b_re, b_im, c_re, c_im, n_lat_tiles):
    b, r, _ = p.shape
    q = S5_CHUNK
    n_chunks = r // q
    cols = q * S5_WIDTH
    kw = 2 * q * S5_GROUP
    w_local, w_u, w_h, aq_tab = _s5_matrices(a_re, a_im, log_dt, b_re, b_im, c_re, c_im)
    cp2 = pltpu.CompilerParams(dimension_semantics=("arbitrary", "arbitrary"), vmem_limit_bytes=48 * 2 ** 20)
    cpt = ROW_TILE // q
    halves = S5_WIDTH // LANES
    cp3 = pltpu.CompilerParams(dimension_semantics=("arbitrary", "arbitrary", "arbitrary"))
    chunk_rows = pl.BlockSpec((1, cpt, q * LANES), lambda i, t, hf: (i, t, hf))
    u_big = pl.pallas_call(
        _s5_pack_kernel,
        grid=(b, r // ROW_TILE, halves),
        in_specs=[pl.BlockSpec((1, ROW_TILE, LANES), lambda i, t, hf: (i, t, COL_S5 // LANES + hf))],
        out_specs=chunk_rows,
        out_shape=jax.ShapeDtypeStruct((b, n_chunks, cols), BF16),
        compiler_params=cp3, name="s5_pack",
    )(p)
    all_chunks = pl.BlockSpec((1, n_chunks, cols), lambda i, g: (i, 0, 0))
    col_tile = lambda width: pl.BlockSpec((1, n_chunks, width), lambda i, g: (i, 0, g))
    s_loc = pl.pallas_call(
        _s5_local_kernel,
        grid=(b, cols // kw),
        in_specs=[all_chunks, pl.BlockSpec((cols, kw), lambda i, g: (0, g))],
        out_specs=col_tile(kw),
        out_shape=jax.ShapeDtypeStruct((b, n_chunks, cols), F32),
        compiler_params=cp2, name="s5_local",
    )(u_big, w_local)
    n_lat = n_lat_tiles * ROW_TILE // q
    hs = pl.pallas_call(
        functools.partial(_s5_scan_kernel, n_lat=n_lat, n_ctx=n_chunks - n_lat, nb=1),
        grid=(b, S5_PAIRS),
        in_specs=[col_tile(kw), pl.BlockSpec((1, 1, kw), lambda i, g: (g, 0, 0))],
        out_specs=col_tile(kw),
        out_shape=jax.ShapeDtypeStruct((b, n_chunks, cols), F32),
        compiler_params=cp2, name="s5_scan",
    )(s_loc, aq_tab)
    nt = S5_WIDTH
    y_big = pl.pallas_call(
        _s5_out_kernel,
        grid=(b, cols // nt),
        in_specs=[all_chunks, all_chunks, pl.BlockSpec((cols, nt), lambda i, g: (0, g)),
                  pl.BlockSpec((cols, nt), lambda i, g: (0, g))],
        out_specs=col_tile(nt),
        out_shape=jax.ShapeDtypeStruct((b, n_chunks, cols), F32),
        compiler_params=cp2, name="s5_out",
    )(u_big, hs, w_u, w_h)
    return pl.pallas_call(
        _s5_unpack_kernel,
        grid=(b, r // ROW_TILE, halves),
        in_specs=[chunk_rows],
        out_specs=pl.BlockSpec((1, ROW_TILE, LANES), lambda i, t, hf: (i, t, hf)),
        out_shape=jax.ShapeDtypeStruct((b, r, S5_WIDTH), F32),
        compiler_params=cp3, name="s5_unpack",
    )(y_big)


def _post_kernel(h_ref, xs_ref, z_ref, r_ref, u_ref, ssd_ref, gla_ref, mla_ref, s5_ref,
                 ssd_d_ref, ssd_g_ref, gla_g_ref, s5_d_ref, glu_w_ref, glu_b_ref, w_out_ref, mod_ref, o_ref):
    y = ssd_ref[0] + ssd_d_ref[...] * xs_ref[0]
    ssd = _rms(y * _silu(z_ref[0]), ssd_g_ref[...])
    o = gla_ref[0]
    lane_head = lax.broadcasted_iota(jnp.int32, (1, GLA_WIDTH), 1) >> (GLA_DV.bit_length() - 1)
    ms = jnp.zeros_like(o)
    for h in range(GLA_HEADS):
        oh = o[:, h * GLA_DV:(h + 1) * GLA_DV]
        ms = jnp.where(lane_head == h, jnp.mean(oh * oh, axis=-1, keepdims=True), ms)
    gla = o * lax.rsqrt(ms + NORM_EPS) * gla_g_ref[...] * _silu(r_ref[0])
    y5 = _gelu_erf(s5_ref[0] + s5_d_ref[...] * u_ref[0])
    s5 = y5 * jax.nn.sigmoid(_mm(y5.astype(BF16), glu_w_ref[...]) + glu_b_ref[...])
    mix_in = jnp.concatenate([ssd, gla, mla_ref[0], s5], axis=1).astype(BF16)
    o_ref[0] = h_ref[0] + mod_ref[0] * _mm(mix_in, w_out_ref[...])


def _post(h, p, ssd_xbc, ssd_y, gla_o, mla_y, s5_y, ssd_d, ssd_norm_g, gla_norm_g, s5_d, glu_w, glu_b, w_out, mod,
          row_off, mla_off):
    b, rows, d = h.shape
    w = GROUP_WIDTH
    pblk = lambda col: pl.BlockSpec((1, ROW_TILE, w), lambda i, t: (i, row_off + t, col // w))
    yblk = pl.BlockSpec((1, ROW_TILE, w), lambda i, t: (i, row_off + t, 0))
    full = lambda *shape: pl.BlockSpec(shape, lambda i, t: (0,) * len(shape))
    vec = lambda x: x.reshape(1, -1).astype(F32)
    n_mod = mod.shape[0]
    return pl.pallas_call(
        _post_kernel,
        grid=(b, rows // ROW_TILE),
        in_specs=[pl.BlockSpec((1, ROW_TILE, d), lambda i, t: (i, t, 0)),
                  yblk, pblk(COL_Z), pblk(COL_GLA_R), pblk(COL_S5), yblk, yblk,
                  pl.BlockSpec((1, ROW_TILE, w), lambda i, t: (i, mla_off + t, 0)), yblk,
                  full(1, w), full(1, w), full(1, w), full(1, w), full(w, w), full(1, w), full(d, d),
                  pl.BlockSpec((1, 1, d), lambda i, t: (jnp.minimum(i, n_mod - 1), 0, 0))],
        out_specs=pl.BlockSpec((1, ROW_TILE, d), lambda i, t: (i, t, 0)),
        out_shape=jax.ShapeDtypeStruct((b, rows, d), F32),
        compiler_params=pltpu.CompilerParams(dimension_semantics=("arbitrary", "arbitrary")),
        name="mix_post",
    )(h, ssd_xbc, p, p, p, ssd_y, gla_o, mla_y, s5_y,
      vec(jnp.repeat(ssd_d, SSD_HEAD_DIM)), vec(ssd_norm_g), vec(jnp.tile(gla_norm_g, GLA_HEADS)), vec(s5_d),
      glu_w.astype(BF16), vec(glu_b), w_out.astype(BF16), mod)


PEER_ROUTE_TOKENS = 256
PEER_GATE_TOKENS = 256
PEER_GATE_UNROLL = 8
PEER_GATE_SUBLANES = 8
PEER_DENSE_TOKENS = 512
PEER_DENSE_EXPERTS = 2 * PEER_GATE_SUBLANES * PEER_KEYS
PEER_SLOTS = PEER_HEADS * PEER_TOPK


def _topk_rows(s, k):
    n_rows = s.shape[0]
    rows = lax.broadcasted_iota(jnp.int32, s.shape, 0)
    vals, idxs = [], []
    for _ in range(k):
        m = jnp.max(s, axis=0, keepdims=True)
        idx = jnp.min(jnp.where(s == m, rows, n_rows), axis=0, keepdims=True)
        vals.append(m)
        idxs.append(idx)
        s = jnp.where(rows == idx, -jnp.inf, s)
    return jnp.concatenate(vals, axis=0), jnp.concatenate(idxs, axis=0)


def _select_rows(pos, table):
    out = jnp.zeros(pos.shape, table.dtype)
    for r in range(table.shape[0]):
        out = jnp.where(pos == r, table[r:r + 1, :], out)
    return out


def _peer_route_kernel(h_ref, g_ref, shift_ref, scale_ref, wq_hi_ref, wq_lo_ref, k_hi_ref, k_lo_ref,
                       xn_ref, i1_ref, i2_ref, gate_ref, q_scr, slot_scr):
    xn = _modulated_norm(h_ref[...], g_ref[...], shift_ref[0], scale_ref[0])
    xn_ref[...] = xn.astype(BF16)
    x_hi, x_lo = _split_bf16(xn)
    q_scr[...] = _dot3(wq_hi_ref[...], wq_lo_ref[...], x_hi, x_lo, _NT_DIMS)
    half = PEER_DQ // 2

    def head_body(h, carry):
        base = pl.multiple_of(h * PEER_DQ, PEER_DQ)
        tops = []
        for j in range(2):
            qq = q_scr[pl.ds(base + j * half, half), :]
            q_hi, q_lo = _split_bf16(qq)
            s = _dot3(k_hi_ref[j, h], k_lo_ref[j, h], q_hi, q_lo, _NN_DIMS)
            tops.append(_topk_rows(s, PEER_TOPK))
        (v1, i1), (v2, i2) = tops
        pieces = [v1[a:a + 1, :] + v2[:PEER_TOPK // (a + 1), :] for a in range(PEER_TOPK)]
        n_cand = sum(PEER_TOPK // (a + 1) for a in range(PEER_TOPK))
        pad = -n_cand % 8
        cand = jnp.concatenate(pieces + [jnp.full((pad, v1.shape[1]), -jnp.inf, F32)], axis=0)
        best, pos = _topk_rows(cand, PEER_TOPK)
        e = jnp.exp(best - best[0:1, :])
        gates = e / jnp.sum(e, axis=0, keepdims=True)
        a_idx = jnp.zeros_like(pos)
        start = jnp.zeros_like(pos)
        first = 0
        for a in range(1, PEER_TOPK):
            width = PEER_TOPK // a
            first += width
            reached = pos >= first
            a_idx = a_idx + jnp.where(reached, 1, 0)
            start = start + jnp.where(reached, width, 0)
        row0 = pl.multiple_of(h * PEER_TOPK, PEER_TOPK)
        slot_scr[0, pl.ds(row0, PEER_TOPK), :] = _select_rows(a_idx, i1).astype(F32)
        slot_scr[1, pl.ds(row0, PEER_TOPK), :] = _select_rows(pos - start, i2).astype(F32)
        slot_scr[2, pl.ds(row0, PEER_TOPK), :] = gates
        return carry

    lax.fori_loop(0, PEER_HEADS, head_body, 0)
    i1_ref[...] = slot_scr[0].T.astype(jnp.int32)
    i2_ref[...] = slot_scr[1].T.astype(jnp.int32)
    gate_ref[...] = slot_scr[2].T


def _bf16_bits(x):
    return pltpu.bitcast(x.astype(BF16).astype(F32), jnp.uint32)


def _peer_gate_kernel(i1_ref, i2_ref, gate_ref, g_ref):
    rows = lax.broadcasted_iota(jnp.int32, (PEER_KEYS, PEER_SLOTS), 0)
    sub = PEER_GATE_SUBLANES

    def token_body(t, carry):
        a = i1_ref[pl.ds(t, 1), :]
        b = i2_ref[pl.ds(t, 1), :]
        w = gate_ref[pl.ds(t, 1), :]
        lhs = jnp.where(rows == a, w, 0.0).astype(BF16)
        rhs = jnp.where(rows == b, 1.0, 0.0).astype(BF16)
        gt = _mm(lhs, rhs, _NT_DIMS)
        row0 = pl.multiple_of(t * sub, sub)
        for g in range(PEER_KEYS // (2 * sub)):
            lo = gt[2 * sub * g:2 * sub * g + sub]
            hi = gt[2 * sub * g + sub:2 * sub * (g + 1)]
            g_ref[g, pl.ds(row0, sub), :] = (_bf16_bits(lo) >> 16) | _bf16_bits(hi)
        return carry

    lax.fori_loop(0, i1_ref.shape[0], token_body, 0, unroll=PEER_GATE_UNROLL)


def _peer_dense_kernel(xn_ref, u_ref, v_ref, gpk_ref, h_ref, mod_ref, o_ref, acc_ref):
    j = pl.program_id(1)

    @pl.when(j == 0)
    def _():
        acc_ref[...] = jnp.zeros_like(acc_ref)

    sub = PEER_GATE_SUBLANES
    xn = xn_ref[...]
    tokens = xn.shape[0]
    words = [gpk_ref[0, pl.ds(r, tokens, stride=sub), :] for r in range(sub)]
    for half in range(2):
        rows = slice(half * sub * PEER_KEYS, (half + 1) * sub * PEER_KEYS)
        hid = _gelu_erf(_mm(xn, u_ref[rows, :], _NT_DIMS))
        ys = []
        for r in range(sub):
            bits = (words[r] << 16) if half == 0 else (words[r] & jnp.uint32(0xFFFF0000))
            ys.append(pltpu.bitcast(bits, F32) * hid[:, r * PEER_KEYS:(r + 1) * PEER_KEYS])
        y = jnp.concatenate(ys, axis=1).astype(BF16)
        acc_ref[...] += _mm(y, v_ref[rows, :])

    @pl.when(j == pl.num_programs(1) - 1)
    def _():
        o_ref[...] = h_ref[...] + mod_ref[0] * acc_ref[...]


def _peer_layer(h, norm_g, shift, scale, gate_mod, wq_t_hi, wq_t_lo, keys_hi, keys_lo, u_bf, v_bf):
    n, d = h.shape
    nb = shift.shape[0]
    rows_per_batch = n // nb
    tr = min(PEER_ROUTE_TOKENS, rows_per_batch)
    full = lambda *shape: pl.BlockSpec(shape, lambda i: (0,) * len(shape))
    per_batch = lambda t: pl.BlockSpec((1, 1, d), lambda i: (i * t // rows_per_batch, 0, 0))
    xn, i1, i2, gate = pl.pallas_call(
        _peer_route_kernel,
        grid=(n // tr,),
        in_specs=[pl.BlockSpec((tr, d), lambda i: (i, 0)), full(1, d), per_batch(tr), per_batch(tr),
                  full(PEER_HEADS * PEER_DQ, d), full(PEER_HEADS * PEER_DQ, d),
                  full(2, PEER_HEADS, PEER_KEYS, PEER_DQ // 2), full(2, PEER_HEADS, PEER_KEYS, PEER_DQ // 2)],
        out_specs=[pl.BlockSpec((tr, d), lambda i: (i, 0))] + [pl.BlockSpec((tr, PEER_SLOTS), lambda i: (i, 0))] * 3,
        out_shape=[jax.ShapeDtypeStruct((n, d), BF16),
                   jax.ShapeDtypeStruct((n, PEER_SLOTS), jnp.int32),
                   jax.ShapeDtypeStruct((n, PEER_SLOTS), jnp.int32),
                   jax.ShapeDtypeStruct((n, PEER_SLOTS), F32)],
        scratch_shapes=[pltpu.VMEM((PEER_HEADS * PEER_DQ, tr), F32), pltpu.VMEM((3, PEER_SLOTS, tr), F32)],
        compiler_params=pltpu.CompilerParams(dimension_semantics=("arbitrary",)),
        name="peer_route",
    )(h, norm_g.reshape(1, d), shift, scale, wq_t_hi, wq_t_lo, keys_hi, keys_lo)

    tg = min(PEER_GATE_TOKENS, n)
    n_planes = PEER_KEYS // (2 * PEER_GATE_SUBLANES)
    slot_spec = pl.BlockSpec((tg, PEER_SLOTS), lambda i: (i, 0))
    gmat = pl.pallas_call(
        _peer_gate_kernel,
        grid=(n // tg,),
        in_specs=[slot_spec, slot_spec, slot_spec],
        out_specs=pl.BlockSpec((n_planes, tg * PEER_GATE_SUBLANES, PEER_KEYS), lambda i: (0, i, 0)),
        out_shape=jax.ShapeDtypeStruct((n_planes, n * PEER_GATE_SUBLANES, PEER_KEYS), jnp.uint32),
        compiler_params=pltpu.CompilerParams(dimension_semantics=("arbitrary",)),
        name="peer_gate",
    )(i1, i2, gate)

    tm = min(PEER_DENSE_TOKENS, rows_per_batch)
    te = PEER_DENSE_EXPERTS
    return pl.pallas_call(
        _peer_dense_kernel,
        grid=(n // tm, PEER_EXPERTS // te),
        in_specs=[pl.BlockSpec((tm, d), lambda i, j: (i, 0)),
                  pl.BlockSpec((te, d), lambda i, j: (j, 0)),
                  pl.BlockSpec((te, d), lambda i, j: (j, 0)),
                  pl.BlockSpec((1, tm * PEER_GATE_SUBLANES, PEER_KEYS), lambda i, j: (j, i, 0)),
                  pl.BlockSpec((tm, d), lambda i, j: (i, 0)),
                  pl.BlockSpec((1, 1, d), lambda i, j: (i * tm // rows_per_batch, 0, 0))],
        out_specs=pl.BlockSpec((tm, d), lambda i, j: (i, 0)),
        out_shape=jax.ShapeDtypeStruct((n, d), F32),
        scratch_shapes=[pltpu.VMEM((tm, d), F32)],
        compiler_params=pltpu.CompilerParams(dimension_semantics=("arbitrary", "arbitrary"),
                                             vmem_limit_bytes=52 * 2 ** 20),
        name="peer_dense",
    )(xn, u_bf, v_bf, gmat, h, gate_mod)


def _final_norm_kernel(x_ref, g_ref, o_ref):
    o_ref[...] = _rms(x_ref[...], g_ref[...])


def _final_norm(h, g):
    n = h.shape[0] * h.shape[1]
    x2 = h.reshape(n, D_MODEL)
    tm = 512
    out = pl.pallas_call(
        _final_norm_kernel,
        grid=(n // tm,),
        in_specs=[pl.BlockSpec((tm, D_MODEL), lambda i: (i, 0)),
                  pl.BlockSpec((1, D_MODEL), lambda i: (0, 0))],
        out_specs=pl.BlockSpec((tm, D_MODEL), lambda i: (i, 0)),
        out_shape=jax.ShapeDtypeStruct((n, D_MODEL), F32),
        name="final_norm",
    )(x2, g.reshape(1, D_MODEL))
    return out.reshape(h.shape)


def _mix_layer(h_lat, h_ctx, mod_l, mod_c, norm_g, w_in, w_out, ssd, gla, mla, s5, ctx_out):
    b, n_lat, d = h_lat.shape
    n_lat_tiles = n_lat // ROW_TILE
    hcomb = jnp.concatenate([h_lat, h_ctx], axis=1)
    tab = lambda k: jnp.concatenate([mod_l[k], mod_c[k]], axis=0)
    p = _inproj(hcomb, norm_g, tab(0), tab(1), _pack_w_in(w_in), n_lat_tiles)
    ssd_y, ssd_xbc = _ssd_mixer(p, ssd["conv_w"], ssd["conv_b"], ssd["a_log"], ssd["dt_bias"], n_lat_tiles)
    gla_o = _gla_mixer(p, gla["gate_w"], gla["gate_b"], n_lat_tiles)
    mla_lat, mla_ctx = _mla_mixer(p, mla["q_norm_g"], mla["w_uq"], mla["kv_norm_g"], mla["w_ukv"], n_lat_tiles, ctx_out)
    s5_y = _s5_mixer(p, s5["a_re"], s5["a_im"], s5["log_dt"], s5["b_re"], s5["b_im"], s5["c_re"], s5["c_im"],
                     n_lat_tiles)
    post = functools.partial(_post, p=p, ssd_xbc=ssd_xbc, ssd_y=ssd_y, gla_o=gla_o, s5_y=s5_y, ssd_d=ssd["d"],
                             ssd_norm_g=ssd["norm_g"], gla_norm_g=gla["norm_g"], s5_d=s5["d"],
                             glu_w=s5["glu_w"], glu_b=s5["glu_b"], w_out=w_out)
    new_lat = post(h_lat, mla_y=mla_lat, mod=mod_l[2], row_off=0, mla_off=0)
    new_ctx = None
    if ctx_out:
        new_ctx = post(h_ctx, mla_y=mla_ctx, mod=mod_c[2], row_off=n_lat_tiles, mla_off=0)
    return new_lat, new_ctx


def kernel(x, c, ctx, c_ctx, ada_w, ada_b, norm_mix_g, norm_ffn_g, w_in, w_out,
           ssd_conv_w, ssd_conv_b, ssd_a_log, ssd_dt_bias, ssd_d, ssd_norm_g,
           gla_gate_w, gla_gate_b, gla_norm_g, mla_q_norm_g, mla_w_uq, mla_kv_norm_g, mla_w_ukv,
           s5_a_re, s5_a_im, s5_log_dt, s5_b_re, s5_b_im, s5_c_re, s5_c_im, s5_d, s5_glu_w, s5_glu_b,
           peer_w_q, peer_sub_keys, peer_u, peer_v, final_norm_g):
    h_lat, h_ctx = x, ctx
    cond_lat = jax.nn.silu(c)[:, None, :]
    cond_ctx = jax.nn.silu(c_ctx)[None, None, :]
    for i in range(DEPTH):
        ctx_out = i < DEPTH - 1
        mod_l = jnp.split(cond_lat @ ada_w[i] + ada_b[i], N_MOD, axis=-1)
        mod_c = jnp.split(cond_ctx @ ada_w[i] + ada_b[i], N_MOD, axis=-1)
        ssd = dict(conv_w=ssd_conv_w[i], conv_b=ssd_conv_b[i], a_log=ssd_a_log[i], dt_bias=ssd_dt_bias[i],
                   d=ssd_d[i], norm_g=ssd_norm_g[i])
        gla = dict(gate_w=gla_gate_w[i], gate_b=gla_gate_b[i], norm_g=gla_norm_g[i])
        mla = dict(q_norm_g=mla_q_norm_g[i], w_uq=mla_w_uq[i], kv_norm_g=mla_kv_norm_g[i], w_ukv=mla_w_ukv[i])
        s5 = dict(a_re=s5_a_re[i], a_im=s5_a_im[i], log_dt=s5_log_dt[i], b_re=s5_b_re[i], b_im=s5_b_im[i],
                  c_re=s5_c_re[i], c_im=s5_c_im[i], d=s5_d[i], glu_w=s5_glu_w[i], glu_b=s5_glu_b[i])
        h_lat, h_ctx_new = _mix_layer(h_lat, h_ctx, mod_l, mod_c, norm_mix_g[i], w_in[i], w_out[i],
                                      ssd, gla, mla, s5, ctx_out)
        wq_t_hi, wq_t_lo = _split_bf16(peer_w_q[i].T)
        keys_hi, keys_lo = _split_bf16(peer_sub_keys[i])
        u_bf, v_bf = peer_u[i].astype(BF16), peer_v[i].astype(BF16)
        peer = functools.partial(_peer_layer, norm_g=norm_ffn_g[i], wq_t_hi=wq_t_hi, wq_t_lo=wq_t_lo,
                                 keys_hi=keys_hi, keys_lo=keys_lo, u_bf=u_bf, v_bf=v_bf)
        h_lat = peer(h_lat.reshape(-1, D_MODEL), shift=mod_l[3], scale=mod_l[4],
                     gate_mod=mod_l[5]).reshape(h_lat.shape)
        if ctx_out:
            h_ctx = peer(h_ctx_new.reshape(-1, D_MODEL), shift=mod_c[3], scale=mod_c[4],
                         gate_mod=mod_c[5]).reshape(h_ctx.shape)
    return _final_norm(h_lat, final_norm_g)
```

```python
import functools
import jax
import jax.numpy as jnp
from jax import lax
import numpy as np
from jax.experimental import pallas as pl
from jax.experimental.pallas import tpu as pltpu

D_MODEL = 1024
DEPTH = 2
GRID_W = 64
NORM_EPS = 1e-6
N_MOD = 6

GROUP_WIDTH = D_MODEL // 4

SSD_WIDTH = GROUP_WIDTH
SSD_HEAD_DIM = 64
SSD_HEADS = SSD_WIDTH // SSD_HEAD_DIM
SSD_GROUPS = 2
SSD_STATE = 128
SSD_CONV = 5
SSD_CHUNK = 128
SSD_CONV_CH = SSD_WIDTH + 2 * SSD_GROUPS * SSD_STATE
SSD_IN = SSD_WIDTH + SSD_CONV_CH + 2 * SSD_HEADS

GLA_WIDTH = GROUP_WIDTH
GLA_HEADS = 4
GLA_DV = GLA_WIDTH // GLA_HEADS
GLA_DK = GLA_DV // 2
GLA_QK = GLA_HEADS * GLA_DK
GLA_GATE_RANK = 16
GLA_TAU = 16.0
GLA_CHUNK = 64
GLA_IN = 2 * GLA_QK + 2 * GLA_WIDTH + 2 * GLA_GATE_RANK

MLA_WIDTH = GROUP_WIDTH
MLA_HEADS = 4
MLA_V = MLA_WIDTH // MLA_HEADS
MLA_NOPE = 64
MLA_ROPE = 32
MLA_Q_RANK = 256
MLA_KV_RANK = 128
MLA_SCALE = (MLA_NOPE + MLA_ROPE) ** -0.5
ROPE_BASE = 10000.0
MLA_IN = MLA_Q_RANK + MLA_KV_RANK + MLA_ROPE

S5_WIDTH = GROUP_WIDTH
S5_GROUP = 16
S5_NGROUPS = S5_WIDTH // S5_GROUP
S5_STATE = 64
S5_MAX_RE = -1e-4
S5_IN = S5_WIDTH
S5_CHUNK = 16
S5_PAIRS = S5_NGROUPS // 2

PEER_KEYS = 128
PEER_EXPERTS = PEER_KEYS * PEER_KEYS
PEER_HEADS = 8
PEER_TOPK = 16
PEER_DQ = 128

LANES = 128
ROW_TILE = 256

F32 = jnp.float32
BF16 = jnp.bfloat16

COL_XS, COL_BM, COL_CM, COL_Z = 0, 256, 512, 768
COL_GLA_V, COL_GLA_R, COL_CQ, COL_S5 = 1024, 1280, 1536, 1792
COL_GLA_Q, COL_GLA_K, COL_CKV, COL_DT, COL_GLR, COL_KR, COL_KRROT = 2048, 2176, 2304, 2432, 2560, 2688, 2816
P_COLS = 2944

_NN_DIMS = (((1,), (0,)), ((), ()))
_NT_DIMS = (((1,), (1,)), ((), ()))
_TN_DIMS = (((0,), (0,)), ((), ()))


def _mm(a, b, dims=_NN_DIMS):
    return lax.dot_general(a, b, dims, preferred_element_type=F32)


def _split_bf16(x):
    hi = x.astype(BF16)
    lo = (x - hi.astype(F32)).astype(BF16)
    return hi, lo


def _split3_bf16(x):
    p1 = x.astype(BF16)
    r1 = x - p1.astype(F32)
    p2 = r1.astype(BF16)
    p3 = (r1 - p2.astype(F32)).astype(BF16)
    return p1, p2, p3


def _dot3(a_hi, a_lo, b_hi, b_lo, dims):
    return _mm(a_hi, b_hi, dims) + _mm(a_hi, b_lo, dims) + _mm(a_lo, b_hi, dims)


def _gelu_erf(x):
    return 0.5 * x * (1.0 + lax.erf(x * (2.0 ** -0.5)))


def _silu(x):
    return x * jax.nn.sigmoid(x)


def _softplus(x):
    return jnp.maximum(x, 0.0) + jnp.log1p(jnp.exp(-jnp.abs(x)))


def _log_sigmoid(x):
    return jnp.minimum(x, 0.0) - jnp.log1p(jnp.exp(-jnp.abs(x)))


def _rms(x, g):
    return x * lax.rsqrt(jnp.mean(x * x, axis=-1, keepdims=True) + NORM_EPS) * g


def _modulated_norm(x, g, shift, scale):
    return _rms(x, g) * (1.0 + scale) + shift


def _causal_mask(n, reverse):
    ri = lax.broadcasted_iota(jnp.int32, (n, n), 0)
    ci = lax.broadcasted_iota(jnp.int32, (n, n), 1)
    return (ci >= ri) if reverse else (ci <= ri)


def _scan_chunk(s, n_lat, n_ctx, reverse):
    if reverse:
        return n_lat + n_ctx - 1 - s
    return jnp.where(s < n_ctx, n_lat + s, s - n_ctx)


def _inproj_kernel(h_ref, g_ref, shift_ref, scale_ref, w_ref, o_ref):
    xn = _modulated_norm(h_ref[0], g_ref[...], shift_ref[0], scale_ref[0])
    o_ref[0] = _mm(xn.astype(BF16), w_ref[...])


def _inproj(hcomb, norm_g, shift_tab, scale_tab, w_pad, n_lat_tiles):
    b, r, d = hcomb.shape
    mod_spec = pl.BlockSpec((1, 1, d), lambda i, t: (jnp.where(t < n_lat_tiles, i, b), 0, 0))
    return pl.pallas_call(
        _inproj_kernel,
        grid=(b, r // ROW_TILE),
        in_specs=[pl.BlockSpec((1, ROW_TILE, d), lambda i, t: (i, t, 0)),
                  pl.BlockSpec((1, d), lambda i, t: (0, 0)), mod_spec, mod_spec,
                  pl.BlockSpec((d, P_COLS), lambda i, t: (0, 0))],
        out_specs=pl.BlockSpec((1, ROW_TILE, P_COLS), lambda i, t: (i, t, 0)),
        out_shape=jax.ShapeDtypeStruct((b, r, P_COLS), F32),
        compiler_params=pltpu.CompilerParams(dimension_semantics=("arbitrary", "arbitrary"),
                                             vmem_limit_bytes=48 * 2 ** 20),
        name="inproj",
    )(hcomb, norm_g.reshape(1, d), shift_tab, scale_tab, w_pad)


def _pack_w_in(w):
    o_ssd, o_gla, o_mla, o_s5 = 0, SSD_IN, SSD_IN + GLA_IN, SSD_IN + GLA_IN + MLA_IN
    out = jnp.zeros((w.shape[0], P_COLS), F32)
    put = lambda out, col, src, width: out.at[:, col:col + width].set(w[:, src:src + width])
    out = put(out, COL_Z, o_ssd, SSD_WIDTH)
    out = put(out, COL_XS, o_ssd + SSD_WIDTH, SSD_CONV_CH)
    out = put(out, COL_DT, o_ssd + SSD_WIDTH + SSD_CONV_CH, 2 * SSD_HEADS)
    out = put(out, COL_GLA_Q, o_gla, GLA_QK)
    out = put(out, COL_GLA_K, o_gla + GLA_QK, GLA_QK)
    out = put(out, COL_GLA_V, o_gla + 2 * GLA_QK, GLA_WIDTH)
    out = put(out, COL_GLA_R, o_gla + 2 * GLA_QK + GLA_WIDTH, GLA_WIDTH)
    out = put(out, COL_GLR, o_gla + 2 * GLA_QK + 2 * GLA_WIDTH, 2 * GLA_GATE_RANK)
    out = put(out, COL_CQ, o_mla, MLA_Q_RANK)
    out = put(out, COL_CKV, o_mla + MLA_Q_RANK, MLA_KV_RANK)
    o_kr = o_mla + MLA_Q_RANK + MLA_KV_RANK
    half = MLA_ROPE // 2
    out = put(out, COL_KR + MLA_NOPE, o_kr, MLA_ROPE)
    out = out.at[:, COL_KRROT + MLA_NOPE:COL_KRROT + MLA_NOPE + half].set(-w[:, o_kr + half:o_kr + MLA_ROPE])
    out = out.at[:, COL_KRROT + MLA_NOPE + half:COL_KRROT + MLA_NOPE + MLA_ROPE].set(w[:, o_kr:o_kr + half])
    out = put(out, COL_S5, o_s5, S5_WIDTH)
    return out.astype(BF16)


def _ssd_prep_kernel(x_ref, prev_ref, next_ref, dt_ref, w_ref, b_ref, bias_ref, xbc_ref, dtc_ref, dtt_ref,
                     *, n_lat_tiles):
    t = pl.program_id(1)
    x = x_ref[0]
    halo = prev_ref.shape[1]
    prev = jnp.where(jnp.logical_and(t > 0, t < n_lat_tiles), prev_ref[0], 0.0)
    nxt = jnp.where(t < n_lat_tiles - 1, next_ref[0], 0.0)
    ext = jnp.concatenate([prev, x, nxt], axis=0)
    rows = ext.shape[0]
    left = SSD_CONV // 2
    acc = jnp.zeros_like(x) + b_ref[...]
    for k in range(SSD_CONV):
        shifted = ext if k == left else pltpu.roll(ext, (left - k) % rows, 0)
        acc = acc + w_ref[k:k + 1, :] * shifted[halo:halo + x.shape[0]]
    xbc_ref[0] = _silu(acc)
    dt = _softplus(dt_ref[0] + bias_ref[...])
    dtc_ref[0] = dt
    dtt_ref[0] = dt.T[:dtt_ref.shape[1]]


def _ssd_scan_kernel(*refs, direction, n_lat, n_ctx, has_prev):
    if has_prev:
        xbc_ref, dtc_ref, dtt_ref, ahr_ref, ahc_ref, yprev_ref, y_ref, state_ref = refs
    else:
        xbc_ref, dtc_ref, dtt_ref, ahr_ref, ahc_ref, y_ref, state_ref = refs
    reverse = direction == 1
    s = pl.program_id(1)

    @pl.when(s == 0)
    def _():
        state_ref[...] = jnp.zeros_like(state_ref)

    q = SSD_CHUNK
    mask = _causal_mask(q, reverse)
    tri = jnp.where(mask, 1.0, 0.0).astype(BF16)
    xbc = xbc_ref[0]
    xs, bm, cm = xbc[:, :SSD_WIDTH], xbc[:, SSD_WIDTH:SSD_WIDTH + 256], xbc[:, SSD_WIDTH + 256:]
    dtc = dtc_ref[0]
    a_col = dtc * ahr_ref[...]
    a_row = dtt_ref[0] * ahc_ref[...]
    acum_col = sum(_mm(tri, part) for part in _split3_bf16(a_col))
    acum_row = sum(_mm(part, tri, _NT_DIMS) for part in _split3_bf16(a_row))
    end = 0 if reverse else q - 1
    bm_bf, cm_bf = bm.astype(BF16), cm.astype(BF16)
    ys = []
    cb = {}
    for h in range(SSD_HEADS):
        g = h // (SSD_HEADS // SSD_GROUPS)
        gs = slice(g * SSD_STATE, (g + 1) * SSD_STATE)
        if g not in cb:
            cb[g] = _mm(cm_bf[:, gs], bm_bf[:, gs], _NT_DIMS)
        ch = direction * SSD_HEADS + h
        ac = acum_col[:, ch:ch + 1]
        ar = acum_row[ch:ch + 1, :]
        decay = jnp.exp(jnp.where(mask, ac - ar, -jnp.inf))
        xd = xs[:, h * SSD_HEAD_DIM:(h + 1) * SSD_HEAD_DIM] * dtc[:, ch:ch + 1]
        y_diag = _mm((cb[g] * decay).astype(BF16), xd.astype(BF16))
        a_end = ac[end:end + 1, :]
        st_local = _mm((xd * jnp.exp(a_end - ac)).astype(BF16), bm_bf[:, gs], _TN_DIMS)
        hs = state_ref[h]
        y_off = jnp.exp(ac) * _mm(cm_bf[:, gs], hs.astype(BF16), _NT_DIMS)
        state_ref[h] = jnp.exp(a_end) * hs + st_local
        ys.append(y_diag + y_off)
    y = jnp.concatenate(ys, axis=1)
    if has_prev:
        y = y + yprev_ref[0]
    y_ref[0] = y


def _ssd_mixer(p, conv_w, conv_b, a_log, dt_bias, n_lat_tiles):
    b, r, _ = p.shape
    nt = r // ROW_TILE
    halo = 8
    hb = ROW_TILE // halo
    w8 = jnp.zeros((8, SSD_CONV_CH), F32).at[:SSD_CONV].set(conv_w)
    bias = jnp.zeros((1, LANES), F32).at[0, :2 * SSD_HEADS].set(dt_bias.reshape(-1))
    xbc, dtc, dtt = pl.pallas_call(
        functools.partial(_ssd_prep_kernel, n_lat_tiles=n_lat_tiles),
        grid=(b, nt),
        in_specs=[pl.BlockSpec((1, ROW_TILE, SSD_CONV_CH), lambda i, t: (i, t, 0)),
                  pl.BlockSpec((1, halo, SSD_CONV_CH), lambda i, t: (i, jnp.maximum(t * hb - 1, 0), 0)),
                  pl.BlockSpec((1, halo, SSD_CONV_CH), lambda i, t: (i, jnp.minimum((t + 1) * hb, nt * hb - 1), 0)),
                  pl.BlockSpec((1, ROW_TILE, LANES), lambda i, t: (i, t, COL_DT // LANES)),
                  pl.BlockSpec((8, SSD_CONV_CH), lambda i, t: (0, 0)),
                  pl.BlockSpec((1, SSD_CONV_CH), lambda i, t: (0, 0)),
                  pl.BlockSpec((1, LANES), lambda i, t: (0, 0))],
        out_specs=[pl.BlockSpec((1, ROW_TILE, SSD_CONV_CH), lambda i, t: (i, t, 0)),
                   pl.BlockSpec((1, ROW_TILE, LANES), lambda i, t: (i, t, 0)),
                   pl.BlockSpec((1, 8, ROW_TILE), lambda i, t: (i, 0, t))],
        out_shape=[jax.ShapeDtypeStruct((b, r, SSD_CONV_CH), F32),
                   jax.ShapeDtypeStruct((b, r, LANES), F32),
                   jax.ShapeDtypeStruct((b, 8, r), F32)],
        compiler_params=pltpu.CompilerParams(dimension_semantics=("arbitrary", "arbitrary")),
        name="ssd_prep",
    )(p, p, p, p, w8, conv_b.reshape(1, -1), bias)

    a_head = -jnp.exp(a_log.astype(F32)).reshape(-1)
    ahr = jnp.zeros((1, LANES), F32).at[0, :2 * SSD_HEADS].set(a_head)
    ahc = a_head.reshape(2 * SSD_HEADS, 1)
    n_lat = n_lat_tiles * ROW_TILE // SSD_CHUNK
    n_ctx = r // SSD_CHUNK - n_lat
    y = None
    for direction in (0, 1):
        cidx = functools.partial(_scan_chunk, n_lat=n_lat, n_ctx=n_ctx, reverse=direction == 1)
        in_specs = [pl.BlockSpec((1, SSD_CHUNK, SSD_CONV_CH), lambda i, s: (i, cidx(s), 0)),
                    pl.BlockSpec((1, SSD_CHUNK, LANES), lambda i, s: (i, cidx(s), 0)),
                    pl.BlockSpec((1, 8, SSD_CHUNK), lambda i, s: (i, 0, cidx(s))),
                    pl.BlockSpec((1, LANES), lambda i, s: (0, 0)),
                    pl.BlockSpec((2 * SSD_HEADS, 1), lambda i, s: (0, 0))]
        args = [xbc, dtc, dtt, ahr, ahc]
        y_spec = pl.BlockSpec((1, SSD_CHUNK, SSD_WIDTH), lambda i, s: (i, cidx(s), 0))
        if y is not None:
            in_specs.append(y_spec)
            args.append(y)
        y = pl.pallas_call(
            functools.partial(_ssd_scan_kernel, direction=direction, n_lat=n_lat, n_ctx=n_ctx,
                              has_prev=y is not None),
            grid=(b, n_lat + n_ctx),
            in_specs=in_specs,
            out_specs=y_spec,
            out_shape=jax.ShapeDtypeStruct((b, r, SSD_WIDTH), F32),
            scratch_shapes=[pltpu.VMEM((SSD_HEADS, SSD_HEAD_DIM, SSD_STATE), F32)],
            compiler_params=pltpu.CompilerParams(dimension_semantics=("arbitrary", "arbitrary")),
            name=f"ssd_scan_{direction}",
        )(*args)
    return y, xbc


def _gla_scan_kernel(*refs, direction, has_prev):
    if has_prev:
        q_ref, k_ref, v_ref, glr_ref, wg_ref, bias_ref, oprev_ref, o_ref, st_ref = refs
    else:
        q_ref, k_ref, v_ref, glr_ref, wg_ref, bias_ref, o_ref, st_ref = refs
    reverse = direction == 1
    s = pl.program_id(1)

    @pl.when(s == 0)
    def _():
        st_ref[...] = jnp.zeros_like(st_ref)

    n = GLA_CHUNK
    mask = _causal_mask(n, reverse)
    tri = jnp.where(mask, 1.0, 0.0).astype(BF16)
    g_hi, g_lo = _split_bf16(glr_ref[0])
    logits = _dot3(g_hi, g_lo, wg_ref[0], wg_ref[1], _NN_DIMS) + bias_ref[...]
    logg = _log_sigmoid(logits) * (1.0 / GLA_TAU)
    bcum = sum(_mm(tri, part) for part in _split3_bf16(logg))
    end = 0 if reverse else n - 1
    b_end = bcum[end:end + 1, :]
    q, k, v = q_ref[0], k_ref[0], v_ref[0]
    qe = q * jnp.exp(bcum) * (GLA_DK ** -0.5)
    ke = (k * jnp.exp(-bcum)).astype(BF16)
    kd = k * jnp.exp(b_end - bcum)
    decay_end = jnp.exp(b_end)
    lane_head = lax.broadcasted_iota(jnp.int32, (1, GLA_QK), 1) >> (GLA_DK.bit_length() - 1)
    outs = []
    for h in range(GLA_HEADS):
        hm = lane_head == h
        qh = jnp.where(hm, qe, 0.0).astype(BF16)
        att = jnp.where(mask, _mm(qh, ke, _NT_DIMS), 0.0)
        vh = v[:, h * GLA_DV:(h + 1) * GLA_DV].astype(BF16)
        st = st_ref[h]
        o_h = _mm(att.astype(BF16), vh) + _mm(qh, st.astype(BF16), _NT_DIMS)
        local = _mm(vh, jnp.where(hm, kd, 0.0).astype(BF16), _TN_DIMS)
        st_ref[h] = st * decay_end + local
        outs.append(o_h)
    o = jnp.concatenate(outs, axis=1)
    if has_prev:
        o = o + oprev_ref[0]
    o_ref[0] = o


def _gla_mixer(p, gate_w, gate_b, n_lat_tiles):
    b, r, _ = p.shape
    n_lat = n_lat_tiles * ROW_TILE // GLA_CHUNK
    n_ctx = r // GLA_CHUNK - n_lat
    o = None
    for direction in (0, 1):
        cidx = functools.partial(_scan_chunk, n_lat=n_lat, n_ctx=n_ctx, reverse=direction == 1)
        wg = jnp.zeros((LANES, GLA_QK), F32).at[direction * GLA_GATE_RANK:(direction + 1) * GLA_GATE_RANK].set(
            gate_w[direction])
        wg = jnp.stack(_split_bf16(wg))
        blk = lambda width, col: pl.BlockSpec((1, GLA_CHUNK, width), lambda i, s: (i, cidx(s), col // width))
        in_specs = [blk(GLA_QK, COL_GLA_Q), blk(GLA_QK, COL_GLA_K), blk(GLA_WIDTH, COL_GLA_V), blk(LANES, COL_GLR),
                    pl.BlockSpec((2, LANES, GLA_QK), lambda i, s: (0, 0, 0)),
                    pl.BlockSpec((1, GLA_QK), lambda i, s: (0, 0))]
        args = [p, p, p, p, wg, gate_b[direction].reshape(1, -1)]
        o_spec = pl.BlockSpec((1, GLA_CHUNK, GLA_WIDTH), lambda i, s: (i, cidx(s), 0))
        if o is not None:
            in_specs.append(o_spec)
            args.append(o)
        o = pl.pallas_call(
            functools.partial(_gla_scan_kernel, direction=direction, has_prev=o is not None),
            grid=(b, n_lat + n_ctx),
            in_specs=in_specs,
            out_specs=o_spec,
            out_shape=jax.ShapeDtypeStruct((b, r, GLA_WIDTH), F32),
            scratch_shapes=[pltpu.VMEM((GLA_HEADS, GLA_DV, GLA_QK), F32)],
            compiler_params=pltpu.CompilerParams(dimension_semantics=("arbitrary", "arbitrary")),
            name=f"gla_scan_{direction}",
        )(*args)
    return o


MLA_Q_TILE = 1024
MLA_K_TILE = 256


def _mla_prep_kernel(cq_ref, ckv_ref, kr_ref, krrot_ref, onec_ref, sinr_ref, gq_ref, gkv_ref,
                     wq_ref, wqr_ref, wk_ref, wv_ref, q_ref, k_ref, v_ref):
    qn = _rms(cq_ref[0], gq_ref[...]).astype(BF16)
    kvn = _rms(ckv_ref[0], gkv_ref[...]).astype(BF16)
    onec, sinr = onec_ref[...], sinr_ref[...]
    k_rope = kr_ref[0] * onec + krrot_ref[0] * sinr
    ones_lane = jnp.where(lax.broadcasted_iota(jnp.int32, (1, LANES), 1) == MLA_V, 1.0, 0.0)
    for h in range(MLA_HEADS):
        qh = _mm(qn, wq_ref[h]) * onec + _mm(qn, wqr_ref[h]) * sinr
        q_ref[0, h] = (qh * MLA_SCALE).astype(BF16)
        k_ref[0, h] = (_mm(kvn, wk_ref[h]) + k_rope).astype(BF16)
        v_ref[0, h] = (_mm(kvn, wv_ref[h]) + ones_lane).astype(BF16)


def _mla_attn_kernel(q_ref, k_ref, v_ref, o_ref, m_ref, acc_ref):
    j = pl.program_id(2)

    @pl.when(j == 0)
    def _():
        m_ref[...] = jnp.full_like(m_ref, -jnp.inf)
        acc_ref[...] = jnp.zeros_like(acc_ref)

    reps = k_ref.shape[2] // LANES
    for h in range(MLA_HEADS):
        s = _mm(q_ref[0, h], k_ref[0, h], _NT_DIMS)
        m_prev = m_ref[h]
        m_new = jnp.maximum(m_prev, jnp.max(s, axis=1, keepdims=True))
        p = jnp.exp(s - jnp.concatenate([m_new] * reps, axis=1))
        acc_ref[h] = jnp.exp(m_prev - m_new) * acc_ref[h] + _mm(p.astype(BF16), v_ref[0, h])
        m_ref[h] = m_new

    @pl.when(j == pl.num_programs(2) - 1)
    def _():
        outs = []
        for h in range(MLA_HEADS):
            acc = acc_ref[h]
            outs.append(acc[:, :MLA_V] / acc[:, MLA_V:MLA_V + 1])
        o_ref[0] = jnp.concatenate(outs, axis=1)


def _rope_tables(n_lat, n_rows):
    rows = n_lat // GRID_W
    row = jnp.repeat(jnp.arange(rows, dtype=F32), GRID_W)
    col = jnp.tile(jnp.arange(GRID_W, dtype=F32), rows)
    half = MLA_ROPE // 2
    inv = ROPE_BASE ** (-jnp.arange(0, half, 2, dtype=F32) / half)
    ang = jnp.concatenate([row[:, None] * inv, col[:, None] * inv], axis=-1)
    cos = jnp.concatenate([jnp.cos(ang), jnp.ones((n_rows - n_lat, half), F32)], axis=0)
    sin = jnp.concatenate([jnp.sin(ang), jnp.zeros((n_rows - n_lat, half), F32)], axis=0)
    pad = jnp.zeros((n_rows, LANES - MLA_NOPE - MLA_ROPE), F32)
    onec = jnp.concatenate([jnp.ones((n_rows, MLA_NOPE), F32), cos, cos, pad], axis=1)
    sinr = jnp.concatenate([jnp.zeros((n_rows, MLA_NOPE), F32), sin, sin, pad], axis=1)
    return onec, sinr


def _mla_weights(w_uq, w_ukv):
    dqk = MLA_NOPE + MLA_ROPE
    half = MLA_ROPE // 2
    wq = w_uq.reshape(MLA_Q_RANK, MLA_HEADS, dqk).transpose(1, 0, 2)
    rot = jnp.concatenate([jnp.zeros_like(wq[..., :MLA_NOPE]), -wq[..., MLA_NOPE + half:], wq[..., MLA_NOPE:MLA_NOPE + half]],
                          axis=-1)
    padq = lambda w: jnp.pad(w, ((0, 0), (0, 0), (0, LANES - dqk))).astype(BF16)
    wkv = w_ukv.reshape(MLA_KV_RANK, MLA_HEADS, MLA_NOPE + MLA_V).transpose(1, 0, 2)
    padk = lambda w: jnp.pad(w, ((0, 0), (0, 0), (0, LANES - w.shape[-1]))).astype(BF16)
    return padq(wq), padq(rot), padk(wkv[..., :MLA_NOPE]), padk(wkv[..., MLA_NOPE:])


def _mla_attention(q, k, v, q_tile, q_off, n_q, k_off, n_k):
    b = q.shape[0]
    kt = MLA_K_TILE
    return pl.pallas_call(
        _mla_attn_kernel,
        grid=(b, n_q, n_k),
        in_specs=[pl.BlockSpec((1, MLA_HEADS, q_tile, LANES), lambda i, a, j: (i, 0, q_off + a, 0)),
                  pl.BlockSpec((1, MLA_HEADS, kt, LANES), lambda i, a, j: (i, 0, k_off + j, 0)),
                  pl.BlockSpec((1, MLA_HEADS, kt, LANES), lambda i, a, j: (i, 0, k_off + j, 0))],
        out_specs=pl.BlockSpec((1, q_tile, MLA_WIDTH), lambda i, a, j: (i, a, 0)),
        out_shape=jax.ShapeDtypeStruct((b, n_q * q_tile, MLA_WIDTH), F32),
        scratch_shapes=[pltpu.VMEM((MLA_HEADS, q_tile, LANES), F32), pltpu.VMEM((MLA_HEADS, q_tile, LANES), F32)],
        compiler_params=pltpu.CompilerParams(dimension_semantics=("arbitrary", "arbitrary", "arbitrary")),
        name="mla_attn",
    )(q, k, v)


def _mla_mixer(p, q_norm_g, w_uq, kv_norm_g, w_ukv, n_lat_tiles, ctx_out):
    b, r, _ = p.shape
    nt = r // ROW_TILE
    n_lat = n_lat_tiles * ROW_TILE
    onec, sinr = _rope_tables(n_lat, r)
    wq, wqr, wk, wv = _mla_weights(w_uq, w_ukv)
    blk = lambda width, col: pl.BlockSpec((1, ROW_TILE, width), lambda i, t: (i, t, col // width))
    tab = pl.BlockSpec((ROW_TILE, LANES), lambda i, t: (t, 0))
    full = lambda *shape: pl.BlockSpec(shape, lambda i, t: (0,) * len(shape))
    head_out = pl.BlockSpec((1, MLA_HEADS, ROW_TILE, LANES), lambda i, t: (i, 0, t, 0))
    q, k, v = pl.pallas_call(
        _mla_prep_kernel,
        grid=(b, nt),
        in_specs=[blk(MLA_Q_RANK, COL_CQ), blk(LANES, COL_CKV), blk(LANES, COL_KR), blk(LANES, COL_KRROT), tab, tab,
                  full(1, MLA_Q_RANK), full(1, MLA_KV_RANK),
                  full(MLA_HEADS, MLA_Q_RANK, LANES), full(MLA_HEADS, MLA_Q_RANK, LANES),
                  full(MLA_HEADS, MLA_KV_RANK, LANES), full(MLA_HEADS, MLA_KV_RANK, LANES)],
        out_specs=[head_out] * 3,
        out_shape=[jax.ShapeDtypeStruct((b, MLA_HEADS, r, LANES), BF16)] * 3,
        compiler_params=pltpu.CompilerParams(dimension_semantics=("arbitrary", "arbitrary")),
        name="mla_prep",
    )(p, p, p, p, onec, sinr, q_norm_g.reshape(1, -1), kv_norm_g.reshape(1, -1), wq, wqr, wk, wv)
    q_tile = min(MLA_Q_TILE, n_lat)
    y_lat = _mla_attention(q, k, v, q_tile, 0, n_lat // q_tile, 0, r // MLA_K_TILE)
    y_ctx = None
    if ctx_out:
        n_ctx = r - n_lat
        y_ctx = _mla_attention(q, k, v, n_ctx, n_lat // n_ctx, 1, n_lat // MLA_K_TILE, n_ctx // MLA_K_TILE)
    return y_lat, y_ctx


def _s5_matrices(a_re, a_im, log_dt, b_re, b_im, c_re, c_im):
    q, ng, ns, nc = S5_CHUNK, S5_NGROUPS, S5_STATE, S5_GROUP
    lam = jnp.minimum(a_re.astype(F32), S5_MAX_RE) + 1j * a_im.astype(F32)
    step = jnp.exp(log_dt.astype(F32))[..., None]
    abar = jnp.exp(lam * step)
    bmat = b_re.astype(F32) + 1j * b_im.astype(F32)
    bbar = ((abar - 1.0) / lam)[..., None] * bmat
    cmat = c_re.astype(F32) + 1j * c_im.astype(F32)
    pw = jnp.exp((lam * step)[..., None] * jnp.arange(q + 1, dtype=F32))
    kern = jnp.einsum('dgcn,dgnl,dgnk->dglck', cmat, pw[..., :q], bbar).real
    ii = jnp.arange(q)
    lag_f = ii[None, :] - ii[:, None]
    gather = lambda kd, lag: jnp.where((lag >= 0)[None, :, :, None, None], kd[:, jnp.clip(lag, 0, q - 1)], 0.0)
    t_f = gather(kern[0], lag_f).transpose(0, 1, 4, 2, 3)
    t_b = gather(kern[1], -lag_f).transpose(0, 1, 4, 2, 3)
    t_sum = (t_f + t_b).reshape(ng, q * nc, q * nc)
    pw_f = pw[0][..., q - 1 - ii]
    pw_b = pw[1][..., ii]
    wst = lambda pwd, bb: jnp.einsum('gnj,gnc->gjcn', pwd, bb).reshape(ng, q * nc, ns)
    wst_f, wst_b = wst(pw_f, bbar[0]), wst(pw_b, bbar[1])
    wout = lambda pwd, cm: jnp.einsum('gcn,gni->gnic', cm, pwd).reshape(ng, ns, q * nc)
    wo_f, wo_b = wout(pw[0][..., ii + 1], cmat[0]), wout(pw[1][..., q - ii], cmat[1])
    aq = pw[..., q]

    def pair_cols(x):
        x = x.reshape(S5_PAIRS, 2, x.shape[1], x.shape[2])
        z = jnp.zeros_like(x[:, 0])
        return jnp.concatenate([jnp.concatenate([x[:, 0], z], axis=2), jnp.concatenate([z, x[:, 1]], axis=2)], axis=1)

    w_local = jnp.concatenate([pair_cols(wst_f.real), pair_cols(wst_f.imag),
                               pair_cols(wst_b.real), pair_cols(wst_b.imag)], axis=2)
    w_out = jnp.concatenate([pair_cols(t_sum), pair_cols(wo_f.real), pair_cols(-wo_f.imag),
                             pair_cols(wo_b.real), pair_cols(-wo_b.imag)], axis=1)
    aq_pair = aq.reshape(2, S5_PAIRS, 2 * ns)
    aq_tab = jnp.concatenate([aq_pair[0].real, aq_pair[0].imag, aq_pair[1].real, aq_pair[1].imag], axis=1)
    return w_local.astype(BF16), w_out.astype(BF16), aq_tab.reshape(S5_PAIRS, 1, 8 * ns).astype(F32)


def _s5_perm():
    cols = S5_CHUNK * S5_WIDTH
    c = jnp.arange(cols, dtype=jnp.int32)
    cc, j = c % S5_GROUP, (c // S5_GROUP) % S5_CHUNK
    g = c // (S5_GROUP * S5_CHUNK)
    per_half = LANES // S5_GROUP
    src = (g // per_half) * (S5_CHUNK * LANES) + j * LANES + (g % per_half) * S5_GROUP + cc
    return jnp.where(c[:, None] == src[None, :], 1.0, 0.0).astype(BF16)


def _s5_pack_kernel(u_ref, o_ref):
    n = o_ref.shape[1]
    for j in range(S5_CHUNK):
        o_ref[0, :, j * LANES:(j + 1) * LANES] = u_ref[0, pl.ds(j, n, stride=S5_CHUNK), :].astype(BF16)


def _s5_unpack_kernel(y_ref, o_ref):
    n = y_ref.shape[1]
    for i in range(S5_CHUNK):
        o_ref[0, pl.ds(i, n, stride=S5_CHUNK), :] = y_ref[0, :, i * LANES:(i + 1) * LANES]


def _s5_local_kernel(u_ref, perm_ref, w_ref, up_ref, s_ref):
    up = _mm(u_ref[0], perm_ref[...]).astype(BF16)
    up_ref[0] = up
    s_ref[0] = _mm(up, w_ref[0])


def _s5_scan_kernel(s3_ref, aq_ref, hs3_ref, *, n_lat, n_ctx, nb):
    s_ref, hs_ref = s3_ref.at[0], hs3_ref.at[0]
    w = 2 * S5_STATE
    aq = aq_ref[0]
    a = [aq[:, i * w:(i + 1) * w] for i in range(4)]
    zero = jnp.zeros((nb, w), F32)
    slab = 8
    cps = slab // nb

    def run_slab(s_re, s_im, a_re, a_im, h_re, h_im, order):
        ent_re, ent_im = [None] * cps, [None] * cps
        for c in order:
            ent_re[c], ent_im[c] = h_re, h_im
            rows = slice(c * nb, (c + 1) * nb)
            h_re, h_im = a_re * h_re - a_im * h_im + s_re[rows], a_re * h_im + a_im * h_re + s_im[rows]
        return jnp.concatenate(ent_re, axis=0), jnp.concatenate(ent_im, axis=0), h_re, h_im

    def body(kk, carry):
        f_re, f_im, b_re, b_im = carry
        rf = pl.multiple_of(_scan_chunk(kk, n_lat // cps, n_ctx // cps, False) * slab, slab)
        rb = pl.multiple_of(_scan_chunk(kk, n_lat // cps, n_ctx // cps, True) * slab, slab)
        e_re, e_im, f_re, f_im = run_slab(s_ref[pl.ds(rf, slab), 0:w], s_ref[pl.ds(rf, slab), w:2 * w],
                                          a[0], a[1], f_re, f_im, range(cps))
        hs_ref[pl.ds(rf, slab), 0:w] = e_re
        hs_ref[pl.ds(rf, slab), w:2 * w] = e_im
        e_re, e_im, b_re, b_im = run_slab(s_ref[pl.ds(rb, slab), 2 * w:3 * w], s_ref[pl.ds(rb, slab), 3 * w:4 * w],
                                          a[2], a[3], b_re, b_im, range(cps - 1, -1, -1))
        hs_ref[pl.ds(rb, slab), 2 * w:3 * w] = e_re
        hs_ref[pl.ds(rb, slab), 3 * w:4 * w] = e_im
        return f_re, f_im, b_re, b_im

    lax.fori_loop(0, (n_lat + n_ctx) // cps, body, (zero, zero, zero, zero))


def _s5_out_kernel(up_ref, hs_ref, w_ref, perm_ref, y_ref):
    @pl.when(pl.program_id(1) == 0)
    def _():
        y_ref[...] = jnp.zeros_like(y_ref)

    kw = up_ref.shape[2]
    y_pair = _mm(up_ref[0], w_ref[0, :kw]) + _mm(hs_ref[0].astype(BF16), w_ref[0, kw:])
    y_hi, y_lo = _split_bf16(y_pair)
    y_ref[0] += _mm(y_hi, perm_ref[...], _NT_DIMS) + _mm(y_lo, perm_ref[...], _NT_DIMS)


def _s5_mixer(p, a_re, a_im, log_dt, b_re, b_im, c_re, c_im, n_lat_tiles):
    b, r, _ = p.shape
    q = S5_CHUNK
    n_chunks = r // q
    cols = q * S5_WIDTH
    kw = 2 * q * S5_GROUP
    w_local, w_out, aq_tab = _s5_matrices(a_re, a_im, log_dt, b_re, b_im, c_re, c_im)
    perm = _s5_perm()
    cp2 = pltpu.CompilerParams(dimension_semantics=("arbitrary", "arbitrary"), vmem_limit_bytes=48 * 2 ** 20)
    cpt = ROW_TILE // q
    halves = S5_WIDTH // LANES
    cp3 = pltpu.CompilerParams(dimension_semantics=("arbitrary", "arbitrary", "arbitrary"))
    chunk_rows = pl.BlockSpec((1, cpt, q * LANES), lambda i, t, hf: (i, t, hf))
    u_big = pl.pallas_call(
        _s5_pack_kernel,
        grid=(b, r // ROW_TILE, halves),
        in_specs=[pl.BlockSpec((1, ROW_TILE, LANES), lambda i, t, hf: (i, t, COL_S5 // LANES + hf))],
        out_specs=chunk_rows,
        out_shape=jax.ShapeDtypeStruct((b, n_chunks, cols), BF16),
        compiler_params=cp3, name="s5_pack",
    )(p)
    all_chunks = pl.BlockSpec((1, n_chunks, cols), lambda i, g: (i, 0, 0))
    col_tile = lambda width: pl.BlockSpec((1, n_chunks, width), lambda i, g: (i, 0, g))
    perm_cols = pl.BlockSpec((cols, kw), lambda i, g: (0, g))
    u_pairs, s_loc = pl.pallas_call(
        _s5_local_kernel,
        grid=(b, S5_PAIRS),
        in_specs=[all_chunks, perm_cols, pl.BlockSpec((1, kw, kw), lambda i, g: (g, 0, 0))],
        out_specs=[col_tile(kw), col_tile(kw)],
        out_shape=[jax.ShapeDtypeStruct((b, n_chunks, cols), BF16), jax.ShapeDtypeStruct((b, n_chunks, cols), F32)],
        compiler_params=cp2, name="s5_local",
    )(u_big, perm, w_local)
    n_lat = n_lat_tiles * ROW_TILE // q
    hs = pl.pallas_call(
        functools.partial(_s5_scan_kernel, n_lat=n_lat, n_ctx=n_chunks - n_lat, nb=1),
        grid=(b, S5_PAIRS),
        in_specs=[col_tile(kw), pl.BlockSpec((1, 1, kw), lambda i, g: (g, 0, 0))],
        out_specs=col_tile(kw),
        out_shape=jax.ShapeDtypeStruct((b, n_chunks, cols), F32),
        compiler_params=cp2, name="s5_scan",
    )(s_loc, aq_tab)
    y_big = pl.pallas_call(
        _s5_out_kernel,
        grid=(b, S5_PAIRS),
        in_specs=[col_tile(kw), col_tile(kw), pl.BlockSpec((1, 2 * kw, kw), lambda i, g: (g, 0, 0)), perm_cols],
        out_specs=all_chunks,
        out_shape=jax.ShapeDtypeStruct((b, n_chunks, cols), F32),
        compiler_params=cp2, name="s5_out",
    )(u_pairs, hs, w_out, perm)
    return pl.pallas_call(
        _s5_unpack_kernel,
        grid=(b, r // ROW_TILE, halves),
        in_specs=[chunk_rows],
        out_specs=pl.BlockSpec((1, ROW_TILE, LANES), lambda i, t, hf: (i, t, hf)),
        out_shape=jax.ShapeDtypeStruct((b, r, S5_WIDTH), F32),
        compiler_params=cp3, name="s5_unpack",
    )(y_big)


def _post_kernel(h_ref, xs_ref, z_ref, r_ref, u_ref, ssd_ref, gla_ref, mla_ref, s5_ref,
                 ssd_d_ref, ssd_g_ref, gla_g_ref, s5_d_ref, glu_w_ref, glu_b_ref, w_out_ref, mod_ref, o_ref):
    y = ssd_ref[0] + ssd_d_ref[...] * xs_ref[0]
    ssd = _rms(y * _silu(z_ref[0]), ssd_g_ref[...])
    o = gla_ref[0]
    lane_head = lax.broadcasted_iota(jnp.int32, (1, GLA_WIDTH), 1) >> (GLA_DV.bit_length() - 1)
    ms = jnp.zeros_like(o)
    for h in range(GLA_HEADS):
        oh = o[:, h * GLA_DV:(h + 1) * GLA_DV]
        ms = jnp.where(lane_head == h, jnp.mean(oh * oh, axis=-1, keepdims=True), ms)
    gla = o * lax.rsqrt(ms + NORM_EPS) * gla_g_ref[...] * _silu(r_ref[0])
    y5 = _gelu_erf(s5_ref[0] + s5_d_ref[...] * u_ref[0])
    s5 = y5 * jax.nn.sigmoid(_mm(y5.astype(BF16), glu_w_ref[...]) + glu_b_ref[...])
    mix_in = jnp.concatenate([ssd, gla, mla_ref[0], s5], axis=1).astype(BF16)
    o_ref[0] = h_ref[0] + mod_ref[0] * _mm(mix_in, w_out_ref[...])


def _post(h, p, ssd_xbc, ssd_y, gla_o, mla_y, s5_y, ssd_d, ssd_norm_g, gla_norm_g, s5_d, glu_w, glu_b, w_out, mod,
          row_off, mla_off):
    b, rows, d = h.shape
    w = GROUP_WIDTH
    pblk = lambda col: pl.BlockSpec((1, ROW_TILE, w), lambda i, t: (i, row_off + t, col // w))
    yblk = pl.BlockSpec((1, ROW_TILE, w), lambda i, t: (i, row_off + t, 0))
    full = lambda *shape: pl.BlockSpec(shape, lambda i, t: (0,) * len(shape))
    vec = lambda x: x.reshape(1, -1).astype(F32)
    n_mod = mod.shape[0]
    return pl.pallas_call(
        _post_kernel,
        grid=(b, rows // ROW_TILE),
        in_specs=[pl.BlockSpec((1, ROW_TILE, d), lambda i, t: (i, t, 0)),
                  yblk, pblk(COL_Z), pblk(COL_GLA_R), pblk(COL_S5), yblk, yblk,
                  pl.BlockSpec((1, ROW_TILE, w), lambda i, t: (i, mla_off + t, 0)), yblk,
                  full(1, w), full(1, w), full(1, w), full(1, w), full(w, w), full(1, w), full(d, d),
                  pl.BlockSpec((1, 1, d), lambda i, t: (jnp.minimum(i, n_mod - 1), 0, 0))],
        out_specs=pl.BlockSpec((1, ROW_TILE, d), lambda i, t: (i, t, 0)),
        out_shape=jax.ShapeDtypeStruct((b, rows, d), F32),
        compiler_params=pltpu.CompilerParams(dimension_semantics=("arbitrary", "arbitrary")),
        name="mix_post",
    )(h, ssd_xbc, p, p, p, ssd_y, gla_o, mla_y, s5_y,
      vec(jnp.repeat(ssd_d, SSD_HEAD_DIM)), vec(ssd_norm_g), vec(jnp.tile(gla_norm_g, GLA_HEADS)), vec(s5_d),
      glu_w.astype(BF16), vec(glu_b), w_out.astype(BF16), mod)


PEER_ROUTE_TOKENS = 256
PEER_GATE_TOKENS = 256
PEER_GATE_UNROLL = 8
PEER_GATE_SUBLANES = 8
PEER_DENSE_TOKENS = 512
PEER_DENSE_EXPERTS = 2 * PEER_GATE_SUBLANES * PEER_KEYS
PEER_SLOTS = PEER_HEADS * PEER_TOPK


def _topk_rows(s, k):
    n_rows = s.shape[0]
    rows = lax.broadcasted_iota(jnp.int32, s.shape, 0)
    vals, idxs = [], []
    for _ in range(k):
        m = jnp.max(s, axis=0, keepdims=True)
        idx = jnp.min(jnp.where(s == m, rows, n_rows), axis=0, keepdims=True)
        vals.append(m)
        idxs.append(idx)
        s = jnp.where(rows == idx, -jnp.inf, s)
    return jnp.concatenate(vals, axis=0), jnp.concatenate(idxs, axis=0)


def _select_rows(pos, table):
    out = jnp.zeros(pos.shape, table.dtype)
    for r in range(table.shape[0]):
        out = jnp.where(pos == r, table[r:r + 1, :], out)
    return out


def _peer_route_kernel(h_ref, g_ref, shift_ref, scale_ref, wq_hi_ref, wq_lo_ref, k_hi_ref, k_lo_ref,
                       xn_ref, i1_ref, i2_ref, gate_ref, q_scr, slot_scr):
    xn = _modulated_norm(h_ref[...], g_ref[...], shift_ref[0], scale_ref[0])
    xn_ref[...] = xn.astype(BF16)
    x_hi, x_lo = _split_bf16(xn)
    q_scr[...] = _dot3(wq_hi_ref[...], wq_lo_ref[...], x_hi, x_lo, _NT_DIMS)
    half = PEER_DQ // 2

    def head_body(h, carry):
        base = pl.multiple_of(h * PEER_DQ, PEER_DQ)
        tops = []
        for j in range(2):
            qq = q_scr[pl.ds(base + j * half, half), :]
            q_hi, q_lo = _split_bf16(qq)
            s = _dot3(k_hi_ref[j, h], k_lo_ref[j, h], q_hi, q_lo, _NN_DIMS)
            tops.append(_topk_rows(s, PEER_TOPK))
        (v1, i1), (v2, i2) = tops
        pieces = [v1[a:a + 1, :] + v2[:PEER_TOPK // (a + 1), :] for a in range(PEER_TOPK)]
        n_cand = sum(PEER_TOPK // (a + 1) for a in range(PEER_TOPK))
        pad = -n_cand % 8
        cand = jnp.concatenate(pieces + [jnp.full((pad, v1.shape[1]), -jnp.inf, F32)], axis=0)
        best, pos = _topk_rows(cand, PEER_TOPK)
        e = jnp.exp(best - best[0:1, :])
        gates = e / jnp.sum(e, axis=0, keepdims=True)
        a_idx = jnp.zeros_like(pos)
        start = jnp.zeros_like(pos)
        first = 0
        for a in range(1, PEER_TOPK):
            width = PEER_TOPK // a
            first += width
            reached = pos >= first
            a_idx = a_idx + jnp.where(reached, 1, 0)
            start = start + jnp.where(reached, width, 0)
        row0 = pl.multiple_of(h * PEER_TOPK, PEER_TOPK)
        slot_scr[0, pl.ds(row0, PEER_TOPK), :] = _select_rows(a_idx, i1).astype(F32)
        slot_scr[1, pl.ds(row0, PEER_TOPK), :] = _select_rows(pos - start, i2).astype(F32)
        slot_scr[2, pl.ds(row0, PEER_TOPK), :] = gates
        return carry

    lax.fori_loop(0, PEER_HEADS, head_body, 0)
    i1_ref[...] = slot_scr[0].T.astype(jnp.int32)
    i2_ref[...] = slot_scr[1].T.astype(jnp.int32)
    gate_ref[...] = slot_scr[2].T


def _bf16_bits(x):
    return pltpu.bitcast(x.astype(BF16).astype(F32), jnp.uint32)


def _peer_gate_kernel(i1_ref, i2_ref, gate_ref, g_ref):
    rows = lax.broadcasted_iota(jnp.int32, (PEER_KEYS, PEER_SLOTS), 0)
    sub = PEER_GATE_SUBLANES

    def token_body(t, carry):
        a = i1_ref[pl.ds(t, 1), :]
        b = i2_ref[pl.ds(t, 1), :]
        w = gate_ref[pl.ds(t, 1), :]
        lhs = jnp.where(rows == a, w, 0.0).astype(BF16)
        rhs = jnp.where(rows == b, 1.0, 0.0).astype(BF16)
        gt = _mm(lhs, rhs, _NT_DIMS)
        row0 = pl.multiple_of(t * sub, sub)
        for g in range(PEER_KEYS // (2 * sub)):
            lo = gt[2 * sub * g:2 * sub * g + sub]
            hi = gt[2 * sub * g + sub:2 * sub * (g + 1)]
            g_ref[g, pl.ds(row0, sub), :] = (_bf16_bits(lo) >> 16) | _bf16_bits(hi)
        return carry

    lax.fori_loop(0, i1_ref.shape[0], token_body, 0, unroll=PEER_GATE_UNROLL)


def _peer_dense_kernel(xn_ref, u_ref, v_ref, gpk_ref, h_ref, mod_ref, o_ref, acc_ref):
    j = pl.program_id(1)

    @pl.when(j == 0)
    def _():
        acc_ref[...] = jnp.zeros_like(acc_ref)

    sub = PEER_GATE_SUBLANES
    xn = xn_ref[...]
    tokens = xn.shape[0]
    words = [gpk_ref[0, pl.ds(r, tokens, stride=sub), :] for r in range(sub)]
    for half in range(2):
        rows = slice(half * sub * PEER_KEYS, (half + 1) * sub * PEER_KEYS)
        hid = _gelu_erf(_mm(xn, u_ref[rows, :], _NT_DIMS))
        ys = []
        for r in range(sub):
            bits = (words[r] << 16) if half == 0 else (words[r] & jnp.uint32(0xFFFF0000))
            ys.append(pltpu.bitcast(bits, F32) * hid[:, r * PEER_KEYS:(r + 1) * PEER_KEYS])
        y = jnp.concatenate(ys, axis=1).astype(BF16)
        acc_ref[...] += _mm(y, v_ref[rows, :])

    @pl.when(j == pl.num_programs(1) - 1)
    def _():
        o_ref[...] = h_ref[...] + mod_ref[0] * acc_ref[...]


def _peer_layer(h, norm_g, shift, scale, gate_mod, wq_t_hi, wq_t_lo, keys_hi, keys_lo, u_bf, v_bf):
    n, d = h.shape
    nb = shift.shape[0]
    rows_per_batch = n // nb
    tr = min(PEER_ROUTE_TOKENS, rows_per_batch)
    full = lambda *shape: pl.BlockSpec(shape, lambda i: (0,) * len(shape))
    per_batch = lambda t: pl.BlockSpec((1, 1, d), lambda i: (i * t // rows_per_batch, 0, 0))
    xn, i1, i2, gate = pl.pallas_call(
        _peer_route_kernel,
        grid=(n // tr,),
        in_specs=[pl.BlockSpec((tr, d), lambda i: (i, 0)), full(1, d), per_batch(tr), per_batch(tr),
                  full(PEER_HEADS * PEER_DQ, d), full(PEER_HEADS * PEER_DQ, d),
                  full(2, PEER_HEADS, PEER_KEYS, PEER_DQ // 2), full(2, PEER_HEADS, PEER_KEYS, PEER_DQ // 2)],
        out_specs=[pl.BlockSpec((tr, d), lambda i: (i, 0))] + [pl.BlockSpec((tr, PEER_SLOTS), lambda i: (i, 0))] * 3,
        out_shape=[jax.ShapeDtypeStruct((n, d), BF16),
                   jax.ShapeDtypeStruct((n, PEER_SLOTS), jnp.int32),
                   jax.ShapeDtypeStruct((n, PEER_SLOTS), jnp.int32),
                   jax.ShapeDtypeStruct((n, PEER_SLOTS), F32)],
        scratch_shapes=[pltpu.VMEM((PEER_HEADS * PEER_DQ, tr), F32), pltpu.VMEM((3, PEER_SLOTS, tr), F32)],
        compiler_params=pltpu.CompilerParams(dimension_semantics=("arbitrary",)),
        name="peer_route",
    )(h, norm_g.reshape(1, d), shift, scale, wq_t_hi, wq_t_lo, keys_hi, keys_lo)

    tg = min(PEER_GATE_TOKENS, n)
    n_planes = PEER_KEYS // (2 * PEER_GATE_SUBLANES)
    slot_spec = pl.BlockSpec((tg, PEER_SLOTS), lambda i: (i, 0))
    gmat = pl.pallas_call(
        _peer_gate_kernel,
        grid=(n // tg,),
        in_specs=[slot_spec, slot_spec, slot_spec],
        out_specs=pl.BlockSpec((n_planes, tg * PEER_GATE_SUBLANES, PEER_KEYS), lambda i: (0, i, 0)),
        out_shape=jax.ShapeDtypeStruct((n_planes, n * PEER_GATE_SUBLANES, PEER_KEYS), jnp.uint32),
        compiler_params=pltpu.CompilerParams(dimension_semantics=("arbitrary",)),
        name="peer_gate",
    )(i1, i2, gate)

    tm = min(PEER_DENSE_TOKENS, rows_per_batch)
    te = PEER_DENSE_EXPERTS
    return pl.pallas_call(
        _peer_dense_kernel,
        grid=(n // tm, PEER_EXPERTS // te),
        in_specs=[pl.BlockSpec((tm, d), lambda i, j: (i, 0)),
                  pl.BlockSpec((te, d), lambda i, j: (j, 0)),
                  pl.BlockSpec((te, d), lambda i, j: (j, 0)),
                  pl.BlockSpec((1, tm * PEER_GATE_SUBLANES, PEER_KEYS), lambda i, j: (j, i, 0)),
                  pl.BlockSpec((tm, d), lambda i, j: (i, 0)),
                  pl.BlockSpec((1, 1, d), lambda i, j: (i * tm // rows_per_batch, 0, 0))],
        out_specs=pl.BlockSpec((tm, d), lambda i, j: (i, 0)),
        out_shape=jax.ShapeDtypeStruct((n, d), F32),
        scratch_shapes=[pltpu.VMEM((tm, d), F32)],
        compiler_params=pltpu.CompilerParams(dimension_semantics=("arbitrary", "arbitrary"),
                                             vmem_limit_bytes=52 * 2 ** 20),
        name="peer_dense",
    )(xn, u_bf, v_bf, gmat, h, gate_mod)


def _final_norm_kernel(x_ref, g_ref, o_ref):
    o_ref[...] = _rms(x_ref[...], g_ref[...])


def _final_norm(h, g):
    n = h.shape[0] * h.shape[1]
    x2 = h.reshape(n, D_MODEL)
    tm = 512
    out = pl.pallas_call(
        _final_norm_kernel,
        grid=(n // tm,),
        in_specs=[pl.BlockSpec((tm, D_MODEL), lambda i: (i, 0)),
                  pl.BlockSpec((1, D_MODEL), lambda i: (0, 0))],
        out_specs=pl.BlockSpec((tm, D_MODEL), lambda i: (i, 0)),
        out_shape=jax.ShapeDtypeStruct((n, D_MODEL), F32),
        name="final_norm",
    )(x2, g.reshape(1, D_MODEL))
    return out.reshape(h.shape)


def _mix_layer(h_lat, h_ctx, mod_l, mod_c, norm_g, w_in, w_out, ssd, gla, mla, s5, ctx_out):
    b, n_lat, d = h_lat.shape
    n_lat_tiles = n_lat // ROW_TILE
    hcomb = jnp.concatenate([h_lat, h_ctx], axis=1)
    tab = lambda k: jnp.concatenate([mod_l[k], mod_c[k]], axis=0)
    p = _inproj(hcomb, norm_g, tab(0), tab(1), _pack_w_in(w_in), n_lat_tiles)
    ssd_y, ssd_xbc = _ssd_mixer(p, ssd["conv_w"], ssd["conv_b"], ssd["a_log"], ssd["dt_bias"], n_lat_tiles)
    gla_o = _gla_mixer(p, gla["gate_w"], gla["gate_b"], n_lat_tiles)
    mla_lat, mla_ctx = _mla_mixer(p, mla["q_norm_g"], mla["w_uq"], mla["kv_norm_g"], mla["w_ukv"], n_lat_tiles, ctx_out)
    s5_y = _s5_mixer(p, s5["a_re"], s5["a_im"], s5["log_dt"], s5["b_re"], s5["b_im"], s5["c_re"], s5["c_im"],
                     n_lat_tiles)
    post = functools.partial(_post, p=p, ssd_xbc=ssd_xbc, ssd_y=ssd_y, gla_o=gla_o, s5_y=s5_y, ssd_d=ssd["d"],
                             ssd_norm_g=ssd["norm_g"], gla_norm_g=gla["norm_g"], s5_d=s5["d"],
                             glu_w=s5["glu_w"], glu_b=s5["glu_b"], w_out=w_out)
    new_lat = post(h_lat, mla_y=mla_lat, mod=mod_l[2], row_off=0, mla_off=0)
    new_ctx = None
    if ctx_out:
        new_ctx = post(h_ctx, mla_y=mla_ctx, mod=mod_c[2], row_off=n_lat_tiles, mla_off=0)
    return new_lat, new_ctx


def kernel(x, c, ctx, c_ctx, ada_w, ada_b, norm_mix_g, norm_ffn_g, w_in, w_out,
           ssd_conv_w, ssd_conv_b, ssd_a_log, ssd_dt_bias, ssd_d, ssd_norm_g,
           gla_gate_w, gla_gate_b, gla_norm_g, mla_q_norm_g, mla_w_uq, mla_kv_norm_g, mla_w_ukv,
           s5_a_re, s5_a_im, s5_log_dt, s5_b_re, s5_b_im, s5_c_re, s5_c_im, s5_d, s5_glu_w, s5_glu_b,
           peer_w_q, peer_sub_keys, peer_u, peer_v, final_norm_g):
    h_lat, h_ctx = x, ctx
    cond_lat = jax.nn.silu(c)[:, None, :]
    cond_ctx = jax.nn.silu(c_ctx)[None, None, :]
    for i in range(DEPTH):
        ctx_out = i < DEPTH - 1
        mod_l = jnp.split(cond_lat @ ada_w[i] + ada_b[i], N_MOD, axis=-1)
        mod_c = jnp.split(cond_ctx @ ada_w[i] + ada_b[i], N_MOD, axis=-1)
        ssd = dict(conv_w=ssd_conv_w[i], conv_b=ssd_conv_b[i], a_log=ssd_a_log[i], dt_bias=ssd_dt_bias[i],
                   d=ssd_d[i], norm_g=ssd_norm_g[i])
        gla = dict(gate_w=gla_gate_w[i], gate_b=gla_gate_b[i], norm_g=gla_norm_g[i])
        mla = dict(q_norm_g=mla_q_norm_g[i], w_uq=mla_w_uq[i], kv_norm_g=mla_kv_norm_g[i], w_ukv=mla_w_ukv[i])
        s5 = dict(a_re=s5_a_re[i], a_im=s5_a_im[i], log_dt=s5_log_dt[i], b_re=s5_b_re[i], b_im=s5_b_im[i],
                  c_re=s5_c_re[i], c_im=s5_c_im[i], d=s5_d[i], glu_w=s5_glu_w[i], glu_b=s5_glu_b[i])
        h_lat, h_ctx_new = _mix_layer(h_lat, h_ctx, mod_l, mod_c, norm_mix_g[i], w_in[i], w_out[i],
                                      ssd, gla, mla, s5, ctx_out)
        wq_t_hi, wq_t_lo = _split_bf16(peer_w_q[i].T)
        keys_hi, keys_lo = _split_bf16(peer_sub_keys[i])
        u_bf, v_bf = peer_u[i].astype(BF16), peer_v[i].astype(BF16)
        peer = functools.partial(_peer_layer, norm_g=norm_ffn_g[i], wq_t_hi=wq_t_hi, wq_t_lo=wq_t_lo,
                                 keys_hi=keys_hi, keys_lo=keys_lo, u_bf=u_bf, v_bf=v_bf)
        h_lat = peer(h_lat.reshape(-1, D_MODEL), shift=mod_l[3], scale=mod_l[4],
                     gate_mod=mod_l[5]).reshape(h_lat.shape)
        if ctx_out:
            h_ctx = peer(h_ctx_new.reshape(-1, D_MODEL), shift=mod_c[3], scale=mod_c[4],
                         gate_mod=mod_c[5]).reshape(h_ctx.shape)
    return _final_norm(h_lat, final_norm_g)
```

```python
import functools
import jax
import jax.numpy as jnp
from jax import lax
import numpy as np
from jax.experimental import pallas as pl
from jax.experimental.pallas import tpu as pltpu

D_MODEL = 1024
DEPTH = 2
GRID_W = 64
NORM_EPS = 1e-6
N_MOD = 6

GROUP_WIDTH = D_MODEL // 4

SSD_WIDTH = GROUP_WIDTH
SSD_HEAD_DIM = 64
SSD_HEADS = SSD_WIDTH // SSD_HEAD_DIM
SSD_GROUPS = 2
SSD_STATE = 128
SSD_CONV = 5
SSD_CHUNK = 128
SSD_CONV_CH = SSD_WIDTH + 2 * SSD_GROUPS * SSD_STATE
SSD_IN = SSD_WIDTH + SSD_CONV_CH + 2 * SSD_HEADS

GLA_WIDTH = GROUP_WIDTH
GLA_HEADS = 4
GLA_DV = GLA_WIDTH // GLA_HEADS
GLA_DK = GLA_DV // 2
GLA_QK = GLA_HEADS * GLA_DK
GLA_GATE_RANK = 16
GLA_TAU = 16.0
GLA_CHUNK = 64
GLA_IN = 2 * GLA_QK + 2 * GLA_WIDTH + 2 * GLA_GATE_RANK

MLA_WIDTH = GROUP_WIDTH
MLA_HEADS = 4
MLA_V = MLA_WIDTH // MLA_HEADS
MLA_NOPE = 64
MLA_ROPE = 32
MLA_Q_RANK = 256
MLA_KV_RANK = 128
MLA_SCALE = (MLA_NOPE + MLA_ROPE) ** -0.5
ROPE_BASE = 10000.0
MLA_IN = MLA_Q_RANK + MLA_KV_RANK + MLA_ROPE

S5_WIDTH = GROUP_WIDTH
S5_GROUP = 16
S5_NGROUPS = S5_WIDTH // S5_GROUP
S5_STATE = 64
S5_MAX_RE = -1e-4
S5_IN = S5_WIDTH
S5_CHUNK = 16
S5_PAIRS = S5_NGROUPS // 2

PEER_KEYS = 128
PEER_EXPERTS = PEER_KEYS * PEER_KEYS
PEER_HEADS = 8
PEER_TOPK = 16
PEER_DQ = 128

LANES = 128
ROW_TILE = 256

F32 = jnp.float32
BF16 = jnp.bfloat16

COL_XS, COL_BM, COL_CM, COL_Z = 0, 256, 512, 768
COL_GLA_V, COL_GLA_R, COL_CQ, COL_S5 = 1024, 1280, 1536, 1792
COL_GLA_Q, COL_GLA_K, COL_CKV, COL_DT, COL_GLR, COL_KR, COL_KRROT = 2048, 2176, 2304, 2432, 2560, 2688, 2816
P_COLS = 2944

_NN_DIMS = (((1,), (0,)), ((), ()))
_NT_DIMS = (((1,), (1,)), ((), ()))
_TN_DIMS = (((0,), (0,)), ((), ()))


def _mm(a, b, dims=_NN_DIMS):
    return lax.dot_general(a, b, dims, preferred_element_type=F32)


def _split_bf16(x):
    hi = x.astype(BF16)
    lo = (x - hi.astype(F32)).astype(BF16)
    return hi, lo


def _split3_bf16(x):
    p1 = x.astype(BF16)
    r1 = x - p1.astype(F32)
    p2 = r1.astype(BF16)
    p3 = (r1 - p2.astype(F32)).astype(BF16)
    return p1, p2, p3


def _dot3(a_hi, a_lo, b_hi, b_lo, dims):
    return _mm(a_hi, b_hi, dims) + _mm(a_hi, b_lo, dims) + _mm(a_lo, b_hi, dims)


def _gelu_erf(x):
    return 0.5 * x * (1.0 + lax.erf(x * (2.0 ** -0.5)))


def _silu(x):
    return x * jax.nn.sigmoid(x)


def _softplus(x):
    return jnp.maximum(x, 0.0) + jnp.log1p(jnp.exp(-jnp.abs(x)))


def _log_sigmoid(x):
    return jnp.minimum(x, 0.0) - jnp.log1p(jnp.exp(-jnp.abs(x)))


def _rms(x, g):
    return x * lax.rsqrt(jnp.mean(x * x, axis=-1, keepdims=True) + NORM_EPS) * g


def _modulated_norm(x, g, shift, scale):
    return _rms(x, g) * (1.0 + scale) + shift


def _causal_mask(n, reverse):
    ri = lax.broadcasted_iota(jnp.int32, (n, n), 0)
    ci = lax.broadcasted_iota(jnp.int32, (n, n), 1)
    return (ci >= ri) if reverse else (ci <= ri)


def _scan_chunk(s, n_lat, n_ctx, reverse):
    if reverse:
        return n_lat + n_ctx - 1 - s
    return jnp.where(s < n_ctx, n_lat + s, s - n_ctx)


def _inproj_kernel(h_ref, g_ref, shift_ref, scale_ref, w_ref, o_ref):
    xn = _modulated_norm(h_ref[0], g_ref[...], shift_ref[0], scale_ref[0])
    o_ref[0] = _mm(xn.astype(BF16), w_ref[...])


def _inproj(hcomb, norm_g, shift_tab, scale_tab, w_pad, n_lat_tiles):
    b, r, d = hcomb.shape
    mod_spec = pl.BlockSpec((1, 1, d), lambda i, t: (jnp.where(t < n_lat_tiles, i, b), 0, 0))
    return pl.pallas_call(
        _inproj_kernel,
        grid=(b, r // ROW_TILE),
        in_specs=[pl.BlockSpec((1, ROW_TILE, d), lambda i, t: (i, t, 0)),
                  pl.BlockSpec((1, d), lambda i, t: (0, 0)), mod_spec, mod_spec,
                  pl.BlockSpec((d, P_COLS), lambda i, t: (0, 0))],
        out_specs=pl.BlockSpec((1, ROW_TILE, P_COLS), lambda i, t: (i, t, 0)),
        out_shape=jax.ShapeDtypeStruct((b, r, P_COLS), F32),
        compiler_params=pltpu.CompilerParams(dimension_semantics=("arbitrary", "arbitrary"),
                                             vmem_limit_bytes=48 * 2 ** 20),
        name="inproj",
    )(hcomb, norm_g.reshape(1, d), shift_tab, scale_tab, w_pad)


def _pack_w_in(w):
    o_ssd, o_gla, o_mla, o_s5 = 0, SSD_IN, SSD_IN + GLA_IN, SSD_IN + GLA_IN + MLA_IN
    out = jnp.zeros((w.shape[0], P_COLS), F32)
    put = lambda out, col, src, width: out.at[:, col:col + width].set(w[:, src:src + width])
    out = put(out, COL_Z, o_ssd, SSD_WIDTH)
    out = put(out, COL_XS, o_ssd + SSD_WIDTH, SSD_CONV_CH)
    out = put(out, COL_DT, o_ssd + SSD_WIDTH + SSD_CONV_CH, 2 * SSD_HEADS)
    out = put(out, COL_GLA_Q, o_gla, GLA_QK)
    out = put(out, COL_GLA_K, o_gla + GLA_QK, GLA_QK)
    out = put(out, COL_GLA_V, o_gla + 2 * GLA_QK, GLA_WIDTH)
    out = put(out, COL_GLA_R, o_gla + 2 * GLA_QK + GLA_WIDTH, GLA_WIDTH)
    out = put(out, COL_GLR, o_gla + 2 * GLA_QK + 2 * GLA_WIDTH, 2 * GLA_GATE_RANK)
    out = put(out, COL_CQ, o_mla, MLA_Q_RANK)
    out = put(out, COL_CKV, o_mla + MLA_Q_RANK, MLA_KV_RANK)
    o_kr = o_mla + MLA_Q_RANK + MLA_KV_RANK
    half = MLA_ROPE // 2
    out = put(out, COL_KR + MLA_NOPE, o_kr, MLA_ROPE)
    out = out.at[:, COL_KRROT + MLA_NOPE:COL_KRROT + MLA_NOPE + half].set(-w[:, o_kr + half:o_kr + MLA_ROPE])
    out = out.at[:, COL_KRROT + MLA_NOPE + half:COL_KRROT + MLA_NOPE + MLA_ROPE].set(w[:, o_kr:o_kr + half])
    out = put(out, COL_S5, o_s5, S5_WIDTH)
    return out.astype(BF16)


def _ssd_prep_kernel(x_ref, prev_ref, next_ref, dt_ref, w_ref, b_ref, bias_ref, xbc_ref, dtc_ref, dtt_ref,
                     *, n_lat_tiles):
    t = pl.program_id(1)
    x = x_ref[0]
    halo = prev_ref.shape[1]
    prev = jnp.where(jnp.logical_and(t > 0, t < n_lat_tiles), prev_ref[0], 0.0)
    nxt = jnp.where(t < n_lat_tiles - 1, next_ref[0], 0.0)
    ext = jnp.concatenate([prev, x, nxt], axis=0)
    rows = ext.shape[0]
    left = SSD_CONV // 2
    acc = jnp.zeros_like(x) + b_ref[...]
    for k in range(SSD_CONV):
        shifted = ext if k == left else pltpu.roll(ext, (left - k) % rows, 0)
        acc = acc + w_ref[k:k + 1, :] * shifted[halo:halo + x.shape[0]]
    xbc_ref[0] = _silu(acc)
    dt = _softplus(dt_ref[0] + bias_ref[...])
    dtc_ref[0] = dt
    dtt_ref[0] = dt.T[:dtt_ref.shape[1]]


def _ssd_scan_kernel(xbc_f_ref, dtc_f_ref, dtt_f_ref, xbc_b_ref, dtc_b_ref, dtt_b_ref, ahr_ref, ahc_ref,
                     yf_ref, yb_ref, state_ref):
    @pl.when(pl.program_id(1) == 0)
    def _():
        state_ref[...] = jnp.zeros_like(state_ref)

    _ssd_chunk(xbc_f_ref, dtc_f_ref, dtt_f_ref, ahr_ref, ahc_ref, yf_ref, state_ref.at[0], 0)
    _ssd_chunk(xbc_b_ref, dtc_b_ref, dtt_b_ref, ahr_ref, ahc_ref, yb_ref, state_ref.at[1], 1)


def _ssd_chunk(xbc_ref, dtc_ref, dtt_ref, ahr_ref, ahc_ref, y_ref, state_ref, direction):
    reverse = direction == 1
    q = SSD_CHUNK
    mask = _causal_mask(q, reverse)
    tri = jnp.where(mask, 1.0, 0.0).astype(BF16)
    xbc = xbc_ref[0]
    xs, bm, cm = xbc[:, :SSD_WIDTH], xbc[:, SSD_WIDTH:SSD_WIDTH + 256], xbc[:, SSD_WIDTH + 256:]
    dtc = dtc_ref[0]
    a_col = dtc * ahr_ref[...]
    a_row = dtt_ref[0] * ahc_ref[...]
    acum_col = sum(_mm(tri, part) for part in _split3_bf16(a_col))
    acum_row = sum(_mm(part, tri, _NT_DIMS) for part in _split3_bf16(a_row))
    end = 0 if reverse else q - 1
    bm_bf, cm_bf = bm.astype(BF16), cm.astype(BF16)
    ys = []
    cb = {}
    for h in range(SSD_HEADS):
        g = h // (SSD_HEADS // SSD_GROUPS)
        gs = slice(g * SSD_STATE, (g + 1) * SSD_STATE)
        if g not in cb:
            cb[g] = _mm(cm_bf[:, gs], bm_bf[:, gs], _NT_DIMS)
        ch = direction * SSD_HEADS + h
        ac = acum_col[:, ch:ch + 1]
        ar = acum_row[ch:ch + 1, :]
        decay = jnp.exp(jnp.where(mask, ac - ar, -jnp.inf))
        xd = xs[:, h * SSD_HEAD_DIM:(h + 1) * SSD_HEAD_DIM] * dtc[:, ch:ch + 1]
        y_diag = _mm((cb[g] * decay).astype(BF16), xd.astype(BF16))
        a_end = ac[end:end + 1, :]
        st_local = _mm((xd * jnp.exp(a_end - ac)).astype(BF16), bm_bf[:, gs], _TN_DIMS)
        hs = state_ref[h]
        y_off = jnp.exp(ac) * _mm(cm_bf[:, gs], hs.astype(BF16), _NT_DIMS)
        state_ref[h] = jnp.exp(a_end) * hs + st_local
        ys.append(y_diag + y_off)
    y_ref[0] = jnp.concatenate(ys, axis=1)


def _ssd_mixer(p, conv_w, conv_b, a_log, dt_bias, n_lat_tiles):
    b, r, _ = p.shape
    nt = r // ROW_TILE
    halo = 8
    hb = ROW_TILE // halo
    w8 = jnp.zeros((8, SSD_CONV_CH), F32).at[:SSD_CONV].set(conv_w)
    bias = jnp.zeros((1, LANES), F32).at[0, :2 * SSD_HEADS].set(dt_bias.reshape(-1))
    xbc, dtc, dtt = pl.pallas_call(
        functools.partial(_ssd_prep_kernel, n_lat_tiles=n_lat_tiles),
        grid=(b, nt),
        in_specs=[pl.BlockSpec((1, ROW_TILE, SSD_CONV_CH), lambda i, t: (i, t, 0)),
                  pl.BlockSpec((1, halo, SSD_CONV_CH), lambda i, t: (i, jnp.maximum(t * hb - 1, 0), 0)),
                  pl.BlockSpec((1, halo, SSD_CONV_CH), lambda i, t: (i, jnp.minimum((t + 1) * hb, nt * hb - 1), 0)),
                  pl.BlockSpec((1, ROW_TILE, LANES), lambda i, t: (i, t, COL_DT // LANES)),
                  pl.BlockSpec((8, SSD_CONV_CH), lambda i, t: (0, 0)),
                  pl.BlockSpec((1, SSD_CONV_CH), lambda i, t: (0, 0)),
                  pl.BlockSpec((1, LANES), lambda i, t: (0, 0))],
        out_specs=[pl.BlockSpec((1, ROW_TILE, SSD_CONV_CH), lambda i, t: (i, t, 0)),
                   pl.BlockSpec((1, ROW_TILE, LANES), lambda i, t: (i, t, 0)),
                   pl.BlockSpec((1, 8, ROW_TILE), lambda i, t: (i, 0, t))],
        out_shape=[jax.ShapeDtypeStruct((b, r, SSD_CONV_CH), F32),
                   jax.ShapeDtypeStruct((b, r, LANES), F32),
                   jax.ShapeDtypeStruct((b, 8, r), F32)],
        compiler_params=pltpu.CompilerParams(dimension_semantics=("arbitrary", "arbitrary")),
        name="ssd_prep",
    )(p, p, p, p, w8, conv_b.reshape(1, -1), bias)

    a_head = -jnp.exp(a_log.astype(F32)).reshape(-1)
    ahr = jnp.zeros((1, LANES), F32).at[0, :2 * SSD_HEADS].set(a_head)
    ahc = a_head.reshape(2 * SSD_HEADS, 1)
    n_lat = n_lat_tiles * ROW_TILE // SSD_CHUNK
    n_ctx = r // SSD_CHUNK - n_lat
    in_specs, y_specs = [], []
    for reverse in (False, True):
        cidx = functools.partial(_scan_chunk, n_lat=n_lat, n_ctx=n_ctx, reverse=reverse)
        in_specs += [pl.BlockSpec((1, SSD_CHUNK, SSD_CONV_CH), lambda i, s, cidx=cidx: (i, cidx(s), 0)),
                     pl.BlockSpec((1, SSD_CHUNK, LANES), lambda i, s, cidx=cidx: (i, cidx(s), 0)),
                     pl.BlockSpec((1, 8, SSD_CHUNK), lambda i, s, cidx=cidx: (i, 0, cidx(s)))]
        y_specs.append(pl.BlockSpec((1, SSD_CHUNK, SSD_WIDTH), lambda i, s, cidx=cidx: (i, cidx(s), 0)))
    in_specs += [pl.BlockSpec((1, LANES), lambda i, s: (0, 0)), pl.BlockSpec((2 * SSD_HEADS, 1), lambda i, s: (0, 0))]
    y_f, y_b = pl.pallas_call(
        _ssd_scan_kernel,
        grid=(b, n_lat + n_ctx),
        in_specs=in_specs,
        out_specs=y_specs,
        out_shape=[jax.ShapeDtypeStruct((b, r, SSD_WIDTH), F32)] * 2,
        scratch_shapes=[pltpu.VMEM((2, SSD_HEADS, SSD_HEAD_DIM, SSD_STATE), F32)],
        compiler_params=pltpu.CompilerParams(dimension_semantics=("arbitrary", "arbitrary")),
        name="ssd_scan",
    )(xbc, dtc, dtt, xbc, dtc, dtt, ahr, ahc)
    return y_f, y_b, xbc


def _gla_scan_kernel(qf_ref, kf_ref, vf_ref, glrf_ref, qb_ref, kb_ref, vb_ref, glrb_ref, wg_ref, bias_ref,
                     of_ref, ob_ref, st_ref):
    @pl.when(pl.program_id(1) == 0)
    def _():
        st_ref[...] = jnp.zeros_like(st_ref)

    _gla_chunk(qf_ref, kf_ref, vf_ref, glrf_ref, wg_ref.at[0], bias_ref.at[0], of_ref, st_ref.at[0], False)
    _gla_chunk(qb_ref, kb_ref, vb_ref, glrb_ref, wg_ref.at[1], bias_ref.at[1], ob_ref, st_ref.at[1], True)


def _gla_chunk(q_ref, k_ref, v_ref, glr_ref, wg_ref, bias_ref, o_ref, st_ref, reverse):
    n = GLA_CHUNK
    mask = _causal_mask(n, reverse)
    tri = jnp.where(mask, 1.0, 0.0).astype(BF16)
    g_hi, g_lo = _split_bf16(glr_ref[0])
    logits = _dot3(g_hi, g_lo, wg_ref[0], wg_ref[1], _NN_DIMS) + bias_ref[...]
    logg = _log_sigmoid(logits) * (1.0 / GLA_TAU)
    bcum = sum(_mm(tri, part) for part in _split3_bf16(logg))
    end = 0 if reverse else n - 1
    b_end = bcum[end:end + 1, :]
    q, k, v = q_ref[0], k_ref[0], v_ref[0]
    qe = q * jnp.exp(bcum) * (GLA_DK ** -0.5)
    ke = (k * jnp.exp(-bcum)).astype(BF16)
    kd = k * jnp.exp(b_end - bcum)
    decay_end = jnp.exp(b_end)
    lane_head = lax.broadcasted_iota(jnp.int32, (1, GLA_QK), 1) >> (GLA_DK.bit_length() - 1)
    outs = []
    for h in range(GLA_HEADS):
        hm = lane_head == h
        qh = jnp.where(hm, qe, 0.0).astype(BF16)
        att = jnp.where(mask, _mm(qh, ke, _NT_DIMS), 0.0)
        vh = v[:, h * GLA_DV:(h + 1) * GLA_DV].astype(BF16)
        st = st_ref[h]
        o_h = _mm(att.astype(BF16), vh) + _mm(qh, st.astype(BF16), _NT_DIMS)
        local = _mm(vh, jnp.where(hm, kd, 0.0).astype(BF16), _TN_DIMS)
        st_ref[h] = st * decay_end + local
        outs.append(o_h)
    o_ref[0] = jnp.concatenate(outs, axis=1)


def _gla_mixer(p, gate_w, gate_b, n_lat_tiles):
    b, r, _ = p.shape
    n_lat = n_lat_tiles * ROW_TILE // GLA_CHUNK
    n_ctx = r // GLA_CHUNK - n_lat
    in_specs, o_specs, wgs = [], [], []
    for direction in (0, 1):
        cidx = functools.partial(_scan_chunk, n_lat=n_lat, n_ctx=n_ctx, reverse=direction == 1)
        wg = jnp.zeros((LANES, GLA_QK), F32).at[direction * GLA_GATE_RANK:(direction + 1) * GLA_GATE_RANK].set(
            gate_w[direction])
        wgs.append(jnp.stack(_split_bf16(wg)))
        blk = lambda width, col, cidx=cidx: pl.BlockSpec((1, GLA_CHUNK, width),
                                                         lambda i, s: (i, cidx(s), col // width))
        in_specs += [blk(GLA_QK, COL_GLA_Q), blk(GLA_QK, COL_GLA_K), blk(GLA_WIDTH, COL_GLA_V), blk(LANES, COL_GLR)]
        o_specs.append(pl.BlockSpec((1, GLA_CHUNK, GLA_WIDTH), lambda i, s, cidx=cidx: (i, cidx(s), 0)))
    in_specs += [pl.BlockSpec((2, 2, LANES, GLA_QK), lambda i, s: (0, 0, 0, 0)),
                 pl.BlockSpec((2, 1, GLA_QK), lambda i, s: (0, 0, 0))]
    return pl.pallas_call(
        _gla_scan_kernel,
        grid=(b, n_lat + n_ctx),
        in_specs=in_specs,
        out_specs=o_specs,
        out_shape=[jax.ShapeDtypeStruct((b, r, GLA_WIDTH), F32)] * 2,
        scratch_shapes=[pltpu.VMEM((2, GLA_HEADS, GLA_DV, GLA_QK), F32)],
        compiler_params=pltpu.CompilerParams(dimension_semantics=("arbitrary", "arbitrary")),
        name="gla_scan",
    )(p, p, p, p, p, p, p, p, jnp.stack(wgs), gate_b.reshape(2, 1, GLA_QK))


MLA_Q_TILE = 1024
MLA_K_TILES = (768, 256)


def _mla_prep_kernel(cq_ref, ckv_ref, kr_ref, krrot_ref, onec_ref, sinr_ref, gq_ref, gkv_ref,
                     wq_ref, wqr_ref, wk_ref, wv_ref, q_ref, k_ref, v_ref):
    qn = _rms(cq_ref[0], gq_ref[...]).astype(BF16)
    kvn = _rms(ckv_ref[0], gkv_ref[...]).astype(BF16)
    onec, sinr = onec_ref[...], sinr_ref[...]
    k_rope = kr_ref[0] * onec + krrot_ref[0] * sinr
    ones_lane = jnp.where(lax.broadcasted_iota(jnp.int32, (1, LANES), 1) == MLA_V, 1.0, 0.0)
    for h in range(MLA_HEADS):
        qh = _mm(qn, wq_ref[h]) * onec + _mm(qn, wqr_ref[h]) * sinr
        q_ref[0, h] = (qh * MLA_SCALE).astype(BF16)
        k_ref[0, h] = (_mm(kvn, wk_ref[h]) + k_rope).astype(BF16)
        v_ref[0, h] = (_mm(kvn, wv_ref[h]) + ones_lane).astype(BF16)


def _mla_attn_kernel(q_ref, k_ref, v_ref, o_ref, m_ref, acc_ref):
    j = pl.program_id(2)

    @pl.when(j == 0)
    def _():
        m_ref[...] = jnp.full_like(m_ref, -jnp.inf)
        acc_ref[...] = jnp.zeros_like(acc_ref)

    reps = k_ref.shape[2] // LANES
    for h in range(MLA_HEADS):
        s = _mm(q_ref[0, h], k_ref[0, h], _NT_DIMS)
        m_prev = m_ref[h]
        m_new = jnp.maximum(m_prev, jnp.max(s, axis=1, keepdims=True))
        p = jnp.exp(s - jnp.concatenate([m_new] * reps, axis=1))
        acc_ref[h] = jnp.exp(m_prev - m_new) * acc_ref[h] + _mm(p.astype(BF16), v_ref[0, h])
        m_ref[h] = m_new

    @pl.when(j == pl.num_programs(2) - 1)
    def _():
        outs = []
        for h in range(MLA_HEADS):
            acc = acc_ref[h]
            outs.append(acc[:, :MLA_V] / acc[:, MLA_V:MLA_V + 1])
        o_ref[0] = jnp.concatenate(outs, axis=1)


def _rope_tables(n_lat, n_rows):
    rows = n_lat // GRID_W
    row = jnp.repeat(jnp.arange(rows, dtype=F32), GRID_W)
    col = jnp.tile(jnp.arange(GRID_W, dtype=F32), rows)
    half = MLA_ROPE // 2
    inv = ROPE_BASE ** (-jnp.arange(0, half, 2, dtype=F32) / half)
    ang = jnp.concatenate([row[:, None] * inv, col[:, None] * inv], axis=-1)
    cos = jnp.concatenate([jnp.cos(ang), jnp.ones((n_rows - n_lat, half), F32)], axis=0)
    sin = jnp.concatenate([jnp.sin(ang), jnp.zeros((n_rows - n_lat, half), F32)], axis=0)
    pad = jnp.zeros((n_rows, LANES - MLA_NOPE - MLA_ROPE), F32)
    onec = jnp.concatenate([jnp.ones((n_rows, MLA_NOPE), F32), cos, cos, pad], axis=1)
    sinr = jnp.concatenate([jnp.zeros((n_rows, MLA_NOPE), F32), sin, sin, pad], axis=1)
    return onec, sinr


def _mla_weights(w_uq, w_ukv):
    dqk = MLA_NOPE + MLA_ROPE
    half = MLA_ROPE // 2
    wq = w_uq.reshape(MLA_Q_RANK, MLA_HEADS, dqk).transpose(1, 0, 2)
    rot = jnp.concatenate([jnp.zeros_like(wq[..., :MLA_NOPE]), -wq[..., MLA_NOPE + half:], wq[..., MLA_NOPE:MLA_NOPE + half]],
                          axis=-1)
    padq = lambda w: jnp.pad(w, ((0, 0), (0, 0), (0, LANES - dqk))).astype(BF16)
    wkv = w_ukv.reshape(MLA_KV_RANK, MLA_HEADS, MLA_NOPE + MLA_V).transpose(1, 0, 2)
    padk = lambda w: jnp.pad(w, ((0, 0), (0, 0), (0, LANES - w.shape[-1]))).astype(BF16)
    return padq(wq), padq(rot), padk(wkv[..., :MLA_NOPE]), padk(wkv[..., MLA_NOPE:])


def _mla_attention(q, k, v, q_tile, q_off, n_q, kt, k_off, n_k):
    b = q.shape[0]
    return pl.pallas_call(
        _mla_attn_kernel,
        grid=(b, n_q, n_k),
        in_specs=[pl.BlockSpec((1, MLA_HEADS, q_tile, LANES), lambda i, a, j: (i, 0, q_off + a, 0)),
                  pl.BlockSpec((1, MLA_HEADS, kt, LANES), lambda i, a, j: (i, 0, k_off + j, 0)),
                  pl.BlockSpec((1, MLA_HEADS, kt, LANES), lambda i, a, j: (i, 0, k_off + j, 0))],
        out_specs=pl.BlockSpec((1, q_tile, MLA_WIDTH), lambda i, a, j: (i, a, 0)),
        out_shape=jax.ShapeDtypeStruct((b, n_q * q_tile, MLA_WIDTH), F32),
        scratch_shapes=[pltpu.VMEM((MLA_HEADS, q_tile, LANES), F32), pltpu.VMEM((MLA_HEADS, q_tile, LANES), F32)],
        compiler_params=pltpu.CompilerParams(dimension_semantics=("arbitrary", "arbitrary", "arbitrary")),
        name="mla_attn",
    )(q, k, v)


def _mla_mixer(p, q_norm_g, w_uq, kv_norm_g, w_ukv, n_lat_tiles, ctx_out):
    b, r, _ = p.shape
    nt = r // ROW_TILE
    n_lat = n_lat_tiles * ROW_TILE
    onec, sinr = _rope_tables(n_lat, r)
    wq, wqr, wk, wv = _mla_weights(w_uq, w_ukv)
    blk = lambda width, col: pl.BlockSpec((1, ROW_TILE, width), lambda i, t: (i, t, col // width))
    tab = pl.BlockSpec((ROW_TILE, LANES), lambda i, t: (t, 0))
    full = lambda *shape: pl.BlockSpec(shape, lambda i, t: (0,) * len(shape))
    head_out = pl.BlockSpec((1, MLA_HEADS, ROW_TILE, LANES), lambda i, t: (i, 0, t, 0))
    q, k, v = pl.pallas_call(
        _mla_prep_kernel,
        grid=(b, nt),
        in_specs=[blk(MLA_Q_RANK, COL_CQ), blk(LANES, COL_CKV), blk(LANES, COL_KR), blk(LANES, COL_KRROT), tab, tab,
                  full(1, MLA_Q_RANK), full(1, MLA_KV_RANK),
                  full(MLA_HEADS, MLA_Q_RANK, LANES), full(MLA_HEADS, MLA_Q_RANK, LANES),
                  full(MLA_HEADS, MLA_KV_RANK, LANES), full(MLA_HEADS, MLA_KV_RANK, LANES)],
        out_specs=[head_out] * 3,
        out_shape=[jax.ShapeDtypeStruct((b, MLA_HEADS, r, LANES), BF16)] * 3,
        compiler_params=pltpu.CompilerParams(dimension_semantics=("arbitrary", "arbitrary")),
        name="mla_prep",
    )(p, p, p, p, onec, sinr, q_norm_g.reshape(1, -1), kv_norm_g.reshape(1, -1), wq, wqr, wk, wv)
    q_tile = min(MLA_Q_TILE, n_lat)
    k_tile = next(t for t in MLA_K_TILES if r % t == 0)
    y_lat = _mla_attention(q, k, v, q_tile, 0, n_lat // q_tile, k_tile, 0, r // k_tile)
    y_ctx = None
    if ctx_out:
        n_ctx = r - n_lat
        y_ctx = _mla_attention(q, k, v, n_ctx, n_lat // n_ctx, 1, n_ctx, n_lat // n_ctx, 1)
    return y_lat, y_ctx


def _s5_matrices(a_re, a_im, log_dt, b_re, b_im, c_re, c_im):
    q, ng, ns, nc = S5_CHUNK, S5_NGROUPS, S5_STATE, S5_GROUP
    lam = jnp.minimum(a_re.astype(F32), S5_MAX_RE) + 1j * a_im.astype(F32)
    step = jnp.exp(log_dt.astype(F32))[..., None]
    abar = jnp.exp(lam * step)
    bmat = b_re.astype(F32) + 1j * b_im.astype(F32)
    bbar = ((abar - 1.0) / lam)[..., None] * bmat
    cmat = c_re.astype(F32) + 1j * c_im.astype(F32)
    pw = jnp.exp((lam * step)[..., None] * jnp.arange(q + 1, dtype=F32))
    kern = jnp.einsum('dgcn,dgnl,dgnk->dglck', cmat, pw[..., :q], bbar).real
    ii = jnp.arange(q)
    lag_f = ii[None, :] - ii[:, None]
    gather = lambda kd, lag: jnp.where((lag >= 0)[None, :, :, None, None], kd[:, jnp.clip(lag, 0, q - 1)], 0.0)
    t_f = gather(kern[0], lag_f).transpose(0, 1, 4, 2, 3)
    t_b = gather(kern[1], -lag_f).transpose(0, 1, 4, 2, 3)
    t_sum = (t_f + t_b).reshape(ng, q * nc, q * nc)
    pw_f = pw[0][..., q - 1 - ii]
    pw_b = pw[1][..., ii]
    wst = lambda pwd, bb: jnp.einsum('gnj,gnc->gjcn', pwd, bb).reshape(ng, q * nc, ns)
    wst_f, wst_b = wst(pw_f, bbar[0]), wst(pw_b, bbar[1])
    wout = lambda pwd, cm: jnp.einsum('gcn,gni->gnic', cm, pwd).reshape(ng, ns, q * nc)
    wo_f, wo_b = wout(pw[0][..., ii + 1], cmat[0]), wout(pw[1][..., q - ii], cmat[1])
    aq = pw[..., q]

    def pair_cols(x):
        x = x.reshape(S5_PAIRS, 2, x.shape[1], x.shape[2])
        z = jnp.zeros_like(x[:, 0])
        return jnp.concatenate([jnp.concatenate([x[:, 0], z], axis=2), jnp.concatenate([z, x[:, 1]], axis=2)], axis=1)

    w_local = jnp.concatenate([pair_cols(wst_f.real), pair_cols(wst_f.imag),
                               pair_cols(wst_b.real), pair_cols(wst_b.imag)], axis=2)
    w_out = jnp.concatenate([pair_cols(t_sum), pair_cols(wo_f.real), pair_cols(-wo_f.imag),
                             pair_cols(wo_b.real), pair_cols(-wo_b.imag)], axis=1)
    aq_pair = aq.reshape(2, S5_PAIRS, 2 * ns)
    aq_tab = jnp.concatenate([aq_pair[0].real, aq_pair[0].imag, aq_pair[1].real, aq_pair[1].imag], axis=1)
    return w_local.astype(BF16), w_out.astype(BF16), aq_tab.reshape(S5_PAIRS, 1, 8 * ns).astype(F32)


def _s5_perm():
    cols = S5_CHUNK * S5_WIDTH
    c = jnp.arange(cols, dtype=jnp.int32)
    cc, j = c % S5_GROUP, (c // S5_GROUP) % S5_CHUNK
    g = c // (S5_GROUP * S5_CHUNK)
    per_half = LANES // S5_GROUP
    src = (g // per_half) * (S5_CHUNK * LANES) + j * LANES + (g % per_half) * S5_GROUP + cc
    return jnp.where(c[:, None] == src[None, :], 1.0, 0.0).astype(BF16)


def _s5_pack_kernel(u_ref, o_ref):
    n = o_ref.shape[1]
    for j in range(S5_CHUNK):
        o_ref[0, :, j * LANES:(j + 1) * LANES] = u_ref[0, pl.ds(j, n, stride=S5_CHUNK), :].astype(BF16)


def _s5_unpack_kernel(y_ref, o_ref):
    n = y_ref.shape[1]
    for i in range(S5_CHUNK):
        o_ref[0, pl.ds(i, n, stride=S5_CHUNK), :] = y_ref[0, :, i * LANES:(i + 1) * LANES]


def _s5_local_kernel(u_ref, perm_ref, w_ref, up_ref, s_ref):
    up = _mm(u_ref[0], perm_ref[...]).astype(BF16)
    up_ref[0] = up
    s_ref[0] = _mm(up, w_ref[0])


def _s5_scan_kernel(s3_ref, aq_ref, hs3_ref, *, n_lat, n_ctx, nb):
    s_ref, hs_ref = s3_ref.at[0], hs3_ref.at[0]
    w = 2 * S5_STATE
    aq = aq_ref[0]
    a = [aq[:, i * w:(i + 1) * w] for i in range(4)]
    zero = jnp.zeros((nb, w), F32)
    slab = 8
    cps = slab // nb

    def run_slab(s_re, s_im, a_re, a_im, h_re, h_im, order):
        ent_re, ent_im = [None] * cps, [None] * cps
        for c in order:
            ent_re[c], ent_im[c] = h_re, h_im
            rows = slice(c * nb, (c + 1) * nb)
            h_re, h_im = a_re * h_re - a_im * h_im + s_re[rows], a_re * h_im + a_im * h_re + s_im[rows]
        return jnp.concatenate(ent_re, axis=0), jnp.concatenate(ent_im, axis=0), h_re, h_im

    def body(kk, carry):
        f_re, f_im, b_re, b_im = carry
        rf = pl.multiple_of(_scan_chunk(kk, n_lat // cps, n_ctx // cps, False) * slab, slab)
        rb = pl.multiple_of(_scan_chunk(kk, n_lat // cps, n_ctx // cps, True) * slab, slab)
        e_re, e_im, f_re, f_im = run_slab(s_ref[pl.ds(rf, slab), 0:w], s_ref[pl.ds(rf, slab), w:2 * w],
                                          a[0], a[1], f_re, f_im, range(cps))
        hs_ref[pl.ds(rf, slab), 0:w] = e_re
        hs_ref[pl.ds(rf, slab), w:2 * w] = e_im
        e_re, e_im, b_re, b_im = run_slab(s_ref[pl.ds(rb, slab), 2 * w:3 * w], s_ref[pl.ds(rb, slab), 3 * w:4 * w],
                                          a[2], a[3], b_re, b_im, range(cps - 1, -1, -1))
        hs_ref[pl.ds(rb, slab), 2 * w:3 * w] = e_re
        hs_ref[pl.ds(rb, slab), 3 * w:4 * w] = e_im
        return f_re, f_im, b_re, b_im

    lax.fori_loop(0, (n_lat + n_ctx) // cps, body, (zero, zero, zero, zero))


def _s5_out_kernel(up_ref, hs_ref, w_ref, perm_ref, y_ref):
    @pl.when(pl.program_id(1) == 0)
    def _():
        y_ref[...] = jnp.zeros_like(y_ref)

    kw = up_ref.shape[2]
    y_pair = _mm(up_ref[0], w_ref[0, :kw]) + _mm(hs_ref[0].astype(BF16), w_ref[0, kw:])
    y_hi, y_lo = _split_bf16(y_pair)
    y_ref[0] += _mm(y_hi, perm_ref[...], _NT_DIMS) + _mm(y_lo, perm_ref[...], _NT_DIMS)


def _s5_mixer(p, a_re, a_im, log_dt, b_re, b_im, c_re, c_im, n_lat_tiles):
    b, r, _ = p.shape
    q = S5_CHUNK
    n_chunks = r // q
    cols = q * S5_WIDTH
    kw = 2 * q * S5_GROUP
    w_local, w_out, aq_tab = _s5_matrices(a_re, a_im, log_dt, b_re, b_im, c_re, c_im)
    perm = _s5_perm()
    cp2 = pltpu.CompilerParams(dimension_semantics=("arbitrary", "arbitrary"), vmem_limit_bytes=48 * 2 ** 20)
    cpt = ROW_TILE // q
    halves = S5_WIDTH // LANES
    cp3 = pltpu.CompilerParams(dimension_semantics=("arbitrary", "arbitrary", "arbitrary"))
    chunk_rows = pl.BlockSpec((1, cpt, q * LANES), lambda i, t, hf: (i, t, hf))
    u_big = pl.pallas_call(
        _s5_pack_kernel,
        grid=(b, r // ROW_TILE, halves),
        in_specs=[pl.BlockSpec((1, ROW_TILE, LANES), lambda i, t, hf: (i, t, COL_S5 // LANES + hf))],
        out_specs=chunk_rows,
        out_shape=jax.ShapeDtypeStruct((b, n_chunks, cols), BF16),
        compiler_params=cp3, name="s5_pack",
    )(p)
    all_chunks = pl.BlockSpec((1, n_chunks, cols), lambda i, g: (i, 0, 0))
    col_tile = lambda width: pl.BlockSpec((1, n_chunks, width), lambda i, g: (i, 0, g))
    perm_cols = pl.BlockSpec((cols, kw), lambda i, g: (0, g))
    u_pairs, s_loc = pl.pallas_call(
        _s5_local_kernel,
        grid=(b, S5_PAIRS),
        in_specs=[all_chunks, perm_cols, pl.BlockSpec((1, kw, kw), lambda i, g: (g, 0, 0))],
        out_specs=[col_tile(kw), col_tile(kw)],
        out_shape=[jax.ShapeDtypeStruct((b, n_chunks, cols), BF16), jax.ShapeDtypeStruct((b, n_chunks, cols), F32)],
        compiler_params=cp2, name="s5_local",
    )(u_big, perm, w_local)
    n_lat = n_lat_tiles * ROW_TILE // q
    hs = pl.pallas_call(
        functools.partial(_s5_scan_kernel, n_lat=n_lat, n_ctx=n_chunks - n_lat, nb=1),
        grid=(b, S5_PAIRS),
        in_specs=[col_tile(kw), pl.BlockSpec((1, 1, kw), lambda i, g: (g, 0, 0))],
        out_specs=col_tile(kw),
        out_shape=jax.ShapeDtypeStruct((b, n_chunks, cols), F32),
        compiler_params=cp2, name="s5_scan",
    )(s_loc, aq_tab)
    y_big = pl.pallas_call(
        _s5_out_kernel,
        grid=(b, S5_PAIRS),
        in_specs=[col_tile(kw), col_tile(kw), pl.BlockSpec((1, 2 * kw, kw), lambda i, g: (g, 0, 0)), perm_cols],
        out_specs=all_chunks,
        out_shape=jax.ShapeDtypeStruct((b, n_chunks, cols), F32),
        compiler_params=cp2, name="s5_out",
    )(u_pairs, hs, w_out, perm)
    return pl.pallas_call(
        _s5_unpack_kernel,
        grid=(b, r // ROW_TILE, halves),
        in_specs=[chunk_rows],
        out_specs=pl.BlockSpec((1, ROW_TILE, LANES), lambda i, t, hf: (i, t, hf)),
        out_shape=jax.ShapeDtypeStruct((b, r, S5_WIDTH), F32),
        compiler_params=cp3, name="s5_unpack",
    )(y_big)


def _post_kernel(h_ref, xs_ref, z_ref, r_ref, u_ref, ssd_ref, ssd_b_ref, gla_ref, gla_b_ref, mla_ref, s5_ref,
                 ssd_d_ref, ssd_g_ref, gla_g_ref, s5_d_ref, glu_w_ref, glu_b_ref, w_out_ref, mod_ref, o_ref):
    y = ssd_ref[0] + ssd_b_ref[0] + ssd_d_ref[...] * xs_ref[0]
    ssd = _rms(y * _silu(z_ref[0]), ssd_g_ref[...])
    o = gla_ref[0] + gla_b_ref[0]
    lane_head = lax.broadcasted_iota(jnp.int32, (1, GLA_WIDTH), 1) >> (GLA_DV.bit_length() - 1)
    ms = jnp.zeros_like(o)
    for h in range(GLA_HEADS):
        oh = o[:, h * GLA_DV:(h + 1) * GLA_DV]
        ms = jnp.where(lane_head == h, jnp.mean(oh * oh, axis=-1, keepdims=True), ms)
    gla = o * lax.rsqrt(ms + NORM_EPS) * gla_g_ref[...] * _silu(r_ref[0])
    y5 = _gelu_erf(s5_ref[0] + s5_d_ref[...] * u_ref[0])
    s5 = y5 * jax.nn.sigmoid(_mm(y5.astype(BF16), glu_w_ref[...]) + glu_b_ref[...])
    mix_in = jnp.concatenate([ssd, gla, mla_ref[0], s5], axis=1).astype(BF16)
    o_ref[0] = h_ref[0] + mod_ref[0] * _mm(mix_in, w_out_ref[...])


def _post(h, p, ssd_xbc, ssd_y, ssd_yb, gla_o, gla_ob, mla_y, s5_y, ssd_d, ssd_norm_g, gla_norm_g, s5_d, glu_w, glu_b, w_out, mod,
          row_off, mla_off):
    b, rows, d = h.shape
    w = GROUP_WIDTH
    pblk = lambda col: pl.BlockSpec((1, ROW_TILE, w), lambda i, t: (i, row_off + t, col // w))
    yblk = pl.BlockSpec((1, ROW_TILE, w), lambda i, t: (i, row_off + t, 0))
    full = lambda *shape: pl.BlockSpec(shape, lambda i, t: (0,) * len(shape))
    vec = lambda x: x.reshape(1, -1).astype(F32)
    n_mod = mod.shape[0]
    return pl.pallas_call(
        _post_kernel,
        grid=(b, rows // ROW_TILE),
        in_specs=[pl.BlockSpec((1, ROW_TILE, d), lambda i, t: (i, t, 0)),
                  yblk, pblk(COL_Z), pblk(COL_GLA_R), pblk(COL_S5), yblk, yblk, yblk, yblk,
                  pl.BlockSpec((1, ROW_TILE, w), lambda i, t: (i, mla_off + t, 0)), yblk,
                  full(1, w), full(1, w), full(1, w), full(1, w), full(w, w), full(1, w), full(d, d),
                  pl.BlockSpec((1, 1, d), lambda i, t: (jnp.minimum(i, n_mod - 1), 0, 0))],
        out_specs=pl.BlockSpec((1, ROW_TILE, d), lambda i, t: (i, t, 0)),
        out_shape=jax.ShapeDtypeStruct((b, rows, d), F32),
        compiler_params=pltpu.CompilerParams(dimension_semantics=("arbitrary", "arbitrary")),
        name="mix_post",
    )(h, ssd_xbc, p, p, p, ssd_y, ssd_yb, gla_o, gla_ob, mla_y, s5_y,
      vec(jnp.repeat(ssd_d, SSD_HEAD_DIM)), vec(ssd_norm_g), vec(jnp.tile(gla_norm_g, GLA_HEADS)), vec(s5_d),
      glu_w.astype(BF16), vec(glu_b), w_out.astype(BF16), mod)


PEER_ROUTE_TOKENS = 256
PEER_GATE_TOKENS = 256
PEER_GATE_UNROLL = 16
PEER_GATE_SUBLANES = 8
PEER_DENSE_TOKENS = 512
PEER_DENSE_EXPERTS = 2 * PEER_GATE_SUBLANES * PEER_KEYS
PEER_SLOTS = PEER_HEADS * PEER_TOPK


def _topk_rows(s, k):
    n_rows = s.shape[0]
    rows = lax.broadcasted_iota(jnp.int32, s.shape, 0)
    vals, idxs = [], []
    for _ in range(k):
        m = jnp.max(s, axis=0, keepdims=True)
        idx = jnp.min(jnp.where(s == m, rows, n_rows), axis=0, keepdims=True)
        vals.append(m)
        idxs.append(idx)
        s = jnp.where(rows == idx, -jnp.inf, s)
    return jnp.concatenate(vals, axis=0), jnp.concatenate(idxs, axis=0)


def _select_rows(pos, table):
    out = jnp.zeros(pos.shape, table.dtype)
    for r in range(table.shape[0]):
        out = jnp.where(pos == r, table[r:r + 1, :], out)
    return out


def _peer_route_kernel(h_ref, g_ref, shift_ref, scale_ref, wq_hi_ref, wq_lo_ref, k_hi_ref, k_lo_ref,
                       xn_ref, i1_ref, i2_ref, gate_ref, q_scr, slot_scr):
    xn = _modulated_norm(h_ref[...], g_ref[...], shift_ref[0], scale_ref[0])
    xn_ref[...] = xn.astype(BF16)
    x_hi, x_lo = _split_bf16(xn)
    q_scr[...] = _dot3(wq_hi_ref[...], wq_lo_ref[...], x_hi, x_lo, _NT_DIMS)
    half = PEER_DQ // 2

    def head_body(h, carry):
        base = pl.multiple_of(h * PEER_DQ, PEER_DQ)
        tops = []
        for j in range(2):
            qq = q_scr[pl.ds(base + j * half, half), :]
            q_hi, q_lo = _split_bf16(qq)
            s = _dot3(k_hi_ref[j, h], k_lo_ref[j, h], q_hi, q_lo, _NN_DIMS)
            tops.append(_topk_rows(s, PEER_TOPK))
        (v1, i1), (v2, i2) = tops
        pieces = [v1[a:a + 1, :] + v2[:PEER_TOPK // (a + 1), :] for a in range(PEER_TOPK)]
        n_cand = sum(PEER_TOPK // (a + 1) for a in range(PEER_TOPK))
        pad = -n_cand % 8
        cand = jnp.concatenate(pieces + [jnp.full((pad, v1.shape[1]), -jnp.inf, F32)], axis=0)
        best, pos = _topk_rows(cand, PEER_TOPK)
        e = jnp.exp(best - best[0:1, :])
        gates = e / jnp.sum(e, axis=0, keepdims=True)
        a_idx = jnp.zeros_like(pos)
        start = jnp.zeros_like(pos)
        first = 0
        for a in range(1, PEER_TOPK):
            width = PEER_TOPK // a
            first += width
            reached = pos >= first
            a_idx = a_idx + jnp.where(reached, 1, 0)
            start = start + jnp.where(reached, width, 0)
        row0 = pl.multiple_of(h * PEER_TOPK, PEER_TOPK)
        slot_scr[0, pl.ds(row0, PEER_TOPK), :] = _select_rows(a_idx, i1).astype(F32)
        slot_scr[1, pl.ds(row0, PEER_TOPK), :] = _select_rows(pos - start, i2).astype(F32)
        slot_scr[2, pl.ds(row0, PEER_TOPK), :] = gates
        return carry

    lax.fori_loop(0, PEER_HEADS, head_body, 0)
    i1_ref[...] = slot_scr[0].T.astype(jnp.int32)
    i2_ref[...] = slot_scr[1].T.astype(jnp.int32)
    gate_ref[...] = slot_scr[2].T


def _bf16_bits(x):
    return pltpu.bitcast(x.astype(BF16).astype(F32), jnp.uint32)


def _peer_gate_kernel(i1_ref, i2_ref, gate_ref, g_ref):
    rows = lax.broadcasted_iota(jnp.int32, (PEER_KEYS, PEER_SLOTS), 0)
    sub = PEER_GATE_SUBLANES

    def token_body(t, carry):
        a = i1_ref[pl.ds(t, 1), :]
        b = i2_ref[pl.ds(t, 1), :]
        w = gate_ref[pl.ds(t, 1), :]
        lhs = jnp.where(rows == a, w, 0.0).astype(BF16)
        rhs = jnp.where(rows == b, 1.0, 0.0).astype(BF16)
        gt = _mm(lhs, rhs, _NT_DIMS)
        row0 = pl.multiple_of(t * sub, sub)
        for g in range(PEER_KEYS // (2 * sub)):
            lo = gt[2 * sub * g:2 * sub * g + sub]
            hi = gt[2 * sub * g + sub:2 * sub * (g + 1)]
            g_ref[g, pl.ds(row0, sub), :] = (_bf16_bits(lo) >> 16) | _bf16_bits(hi)
        return carry

    lax.fori_loop(0, i1_ref.shape[0], token_body, 0, unroll=PEER_GATE_UNROLL)


def _peer_dense_kernel(xn_ref, u_ref, v_ref, gpk_ref, h_ref, mod_ref, o_ref, acc_ref):
    j = pl.program_id(1)

    @pl.when(j == 0)
    def _():
        acc_ref[...] = jnp.zeros_like(acc_ref)

    sub = PEER_GATE_SUBLANES
    xn = xn_ref[...]
    tokens = xn.shape[0]
    words = [gpk_ref[0, pl.ds(r, tokens, stride=sub), :] for r in range(sub)]
    for half in range(2):
        rows = slice(half * sub * PEER_KEYS, (half + 1) * sub * PEER_KEYS)
        hid = _gelu_erf(_mm(xn, u_ref[rows, :], _NT_DIMS))
        ys = []
        for r in range(sub):
            bits = (words[r] << 16) if half == 0 else (words[r] & jnp.uint32(0xFFFF0000))
            ys.append(pltpu.bitcast(bits, F32) * hid[:, r * PEER_KEYS:(r + 1) * PEER_KEYS])
        y = jnp.concatenate(ys, axis=1).astype(BF16)
        acc_ref[...] += _mm(y, v_ref[rows, :])

    @pl.when(j == pl.num_programs(1) - 1)
    def _():
        o_ref[...] = h_ref[...] + mod_ref[0] * acc_ref[...]


def _peer_layer(h, norm_g, shift, scale, gate_mod, wq_t_hi, wq_t_lo, keys_hi, keys_lo, u_bf, v_bf):
    n, d = h.shape
    nb = shift.shape[0]
    rows_per_batch = n // nb
    tr = min(PEER_ROUTE_TOKENS, rows_per_batch)
    full = lambda *shape: pl.BlockSpec(shape, lambda i: (0,) * len(shape))
    per_batch = lambda t: pl.BlockSpec((1, 1, d), lambda i: (i * t // rows_per_batch, 0, 0))
    xn, i1, i2, gate = pl.pallas_call(
        _peer_route_kernel,
        grid=(n // tr,),
        in_specs=[pl.BlockSpec((tr, d), lambda i: (i, 0)), full(1, d), per_batch(tr), per_batch(tr),
                  full(PEER_HEADS * PEER_DQ, d), full(PEER_HEADS * PEER_DQ, d),
                  full(2, PEER_HEADS, PEER_KEYS, PEER_DQ // 2), full(2, PEER_HEADS, PEER_KEYS, PEER_DQ // 2)],
        out_specs=[pl.BlockSpec((tr, d), lambda i: (i, 0))] + [pl.BlockSpec((tr, PEER_SLOTS), lambda i: (i, 0))] * 3,
        out_shape=[jax.ShapeDtypeStruct((n, d), BF16),
                   jax.ShapeDtypeStruct((n, PEER_SLOTS), jnp.int32),
                   jax.ShapeDtypeStruct((n, PEER_SLOTS), jnp.int32),
                   jax.ShapeDtypeStruct((n, PEER_SLOTS), F32)],
        scratch_shapes=[pltpu.VMEM((PEER_HEADS * PEER_DQ, tr), F32), pltpu.VMEM((3, PEER_SLOTS, tr), F32)],
        compiler_params=pltpu.CompilerParams(dimension_semantics=("arbitrary",)),
        name="peer_route",
    )(h, norm_g.reshape(1, d), shift, scale, wq_t_hi, wq_t_lo, keys_hi, keys_lo)

    tg = min(PEER_GATE_TOKENS, n)
    n_planes = PEER_KEYS // (2 * PEER_GATE_SUBLANES)
    slot_spec = pl.BlockSpec((tg, PEER_SLOTS), lambda i: (i, 0))
    gmat = pl.pallas_call(
        _peer_gate_kernel,
        grid=(n // tg,),
        in_specs=[slot_spec, slot_spec, slot_spec],
        out_specs=pl.BlockSpec((n_planes, tg * PEER_GATE_SUBLANES, PEER_KEYS), lambda i: (0, i, 0)),
        out_shape=jax.ShapeDtypeStruct((n_planes, n * PEER_GATE_SUBLANES, PEER_KEYS), jnp.uint32),
        compiler_params=pltpu.CompilerParams(dimension_semantics=("arbitrary",)),
        name="peer_gate",
    )(i1, i2, gate)

    tm = min(PEER_DENSE_TOKENS, rows_per_batch)
    te = PEER_DENSE_EXPERTS
    return pl.pallas_call(
        _peer_dense_kernel,
        grid=(n // tm, PEER_EXPERTS // te),
        in_specs=[pl.BlockSpec((tm, d), lambda i, j: (i, 0)),
                  pl.BlockSpec((te, d), lambda i, j: (j, 0)),
                  pl.BlockSpec((te, d), lambda i, j: (j, 0)),
                  pl.BlockSpec((1, tm * PEER_GATE_SUBLANES, PEER_KEYS), lambda i, j: (j, i, 0)),
                  pl.BlockSpec((tm, d), lambda i, j: (i, 0)),
                  pl.BlockSpec((1, 1, d), lambda i, j: (i * tm // rows_per_batch, 0, 0))],
        out_specs=pl.BlockSpec((tm, d), lambda i, j: (i, 0)),
        out_shape=jax.ShapeDtypeStruct((n, d), F32),
        scratch_shapes=[pltpu.VMEM((tm, d), F32)],
        compiler_params=pltpu.CompilerParams(dimension_semantics=("arbitrary", "arbitrary"),
                                             vmem_limit_bytes=52 * 2 ** 20),
        name="peer_dense",
    )(xn, u_bf, v_bf, gmat, h, gate_mod)


def _final_norm_kernel(x_ref, g_ref, o_ref):
    o_ref[...] = _rms(x_ref[...], g_ref[...])


def _final_norm(h, g):
    n = h.shape[0] * h.shape[1]
    x2 = h.reshape(n, D_MODEL)
    tm = 512
    out = pl.pallas_call(
        _final_norm_kernel,
        grid=(n // tm,),
        in_specs=[pl.BlockSpec((tm, D_MODEL), lambda i: (i, 0)),
                  pl.BlockSpec((1, D_MODEL), lambda i: (0, 0))],
        out_specs=pl.BlockSpec((tm, D_MODEL), lambda i: (i, 0)),
        out_shape=jax.ShapeDtypeStruct((n, D_MODEL), F32),
        name="final_norm",
    )(x2, g.reshape(1, D_MODEL))
    return out.reshape(h.shape)


def _mix_layer(h_lat, h_ctx, mod_l, mod_c, norm_g, w_in, w_out, ssd, gla, mla, s5, ctx_out):
    b, n_lat, d = h_lat.shape
    n_lat_tiles = n_lat // ROW_TILE
    hcomb = jnp.concatenate([h_lat, h_ctx], axis=1)
    tab = lambda k: jnp.concatenate([mod_l[k], mod_c[k]], axis=0)
    p = _inproj(hcomb, norm_g, tab(0), tab(1), _pack_w_in(w_in), n_lat_tiles)
    ssd_y, ssd_yb, ssd_xbc = _ssd_mixer(p, ssd["conv_w"], ssd["conv_b"], ssd["a_log"], ssd["dt_bias"], n_lat_tiles)
    gla_o, gla_ob = _gla_mixer(p, gla["gate_w"], gla["gate_b"], n_lat_tiles)
    mla_lat, mla_ctx = _mla_mixer(p, mla["q_norm_g"], mla["w_uq"], mla["kv_norm_g"], mla["w_ukv"], n_lat_tiles, ctx_out)
    s5_y = _s5_mixer(p, s5["a_re"], s5["a_im"], s5["log_dt"], s5["b_re"], s5["b_im"], s5["c_re"], s5["c_im"],
                     n_lat_tiles)
    post = functools.partial(_post, p=p, ssd_xbc=ssd_xbc, ssd_y=ssd_y, ssd_yb=ssd_yb, gla_o=gla_o, gla_ob=gla_ob,
                             s5_y=s5_y, ssd_d=ssd["d"],
                             ssd_norm_g=ssd["norm_g"], gla_norm_g=gla["norm_g"], s5_d=s5["d"],
                             glu_w=s5["glu_w"], glu_b=s5["glu_b"], w_out=w_out)
    new_lat = post(h_lat, mla_y=mla_lat, mod=mod_l[2], row_off=0, mla_off=0)
    new_ctx = None
    if ctx_out:
        new_ctx = post(h_ctx, mla_y=mla_ctx, mod=mod_c[2], row_off=n_lat_tiles, mla_off=0)
    return new_lat, new_ctx


def kernel(x, c, ctx, c_ctx, ada_w, ada_b, norm_mix_g, norm_ffn_g, w_in, w_out,
           ssd_conv_w, ssd_conv_b, ssd_a_log, ssd_dt_bias, ssd_d, ssd_norm_g,
           gla_gate_w, gla_gate_b, gla_norm_g, mla_q_norm_g, mla_w_uq, mla_kv_norm_g, mla_w_ukv,
           s5_a_re, s5_a_im, s5_log_dt, s5_b_re, s5_b_im, s5_c_re, s5_c_im, s5_d, s5_glu_w, s5_glu_b,
           peer_w_q, peer_sub_keys, peer_u, peer_v, final_norm_g):
    h_lat, h_ctx = x, ctx
    cond_lat = jax.nn.silu(c)[:, None, :]
    cond_ctx = jax.nn.silu(c_ctx)[None, None, :]
    for i in range(DEPTH):
        ctx_out = i < DEPTH - 1
        mod_l = jnp.split(cond_lat @ ada_w[i] + ada_b[i], N_MOD, axis=-1)
        mod_c = jnp.split(cond_ctx @ ada_w[i] + ada_b[i], N_MOD, axis=-1)
        ssd = dict(conv_w=ssd_conv_w[i], conv_b=ssd_conv_b[i], a_log=ssd_a_log[i], dt_bias=ssd_dt_bias[i],
                   d=ssd_d[i], norm_g=ssd_norm_g[i])
        gla = dict(gate_w=gla_gate_w[i], gate_b=gla_gate_b[i], norm_g=gla_norm_g[i])
        mla = dict(q_norm_g=mla_q_norm_g[i], w_uq=mla_w_uq[i], kv_norm_g=mla_kv_norm_g[i], w_ukv=mla_w_ukv[i])
        s5 = dict(a_re=s5_a_re[i], a_im=s5_a_im[i], log_dt=s5_log_dt[i], b_re=s5_b_re[i], b_im=s5_b_im[i],
                  c_re=s5_c_re[i], c_im=s5_c_im[i], d=s5_d[i], glu_w=s5_glu_w[i], glu_b=s5_glu_b[i])
        h_lat, h_ctx_new = _mix_layer(h_lat, h_ctx, mod_l, mod_c, norm_mix_g[i], w_in[i], w_out[i],
                                      ssd, gla, mla, s5, ctx_out)
        wq_t_hi, wq_t_lo = _split_bf16(peer_w_q[i].T)
        keys_hi, keys_lo = _split_bf16(peer_sub_keys[i])
        u_bf, v_bf = peer_u[i].astype(BF16), peer_v[i].astype(BF16)
        peer = functools.partial(_peer_layer, norm_g=norm_ffn_g[i], wq_t_hi=wq_t_hi, wq_t_lo=wq_t_lo,
                                 keys_hi=keys_hi, keys_lo=keys_lo, u_bf=u_bf, v_bf=v_bf)
        h_lat = peer(h_lat.reshape(-1, D_MODEL), shift=mod_l[3], scale=mod_l[4],
                     gate_mod=mod_l[5]).reshape(h_lat.shape)
        if ctx_out:
            h_ctx = peer(h_ctx_new.reshape(-1, D_MODEL), shift=mod_c[3], scale=mod_c[4],
                         gate_mod=mod_c[5]).reshape(h_ctx.shape)
    return _final_norm(h_lat, final_norm_g)
```

```python
import functools
import jax
import jax.numpy as jnp
from jax import lax
import numpy as np
from jax.experimental import pallas as pl
from jax.experimental.pallas import tpu as pltpu

D_MODEL = 1024
DEPTH = 2
GRID_W = 64
NORM_EPS = 1e-6
N_MOD = 6

GROUP_WIDTH = D_MODEL // 4

SSD_WIDTH = GROUP_WIDTH
SSD_HEAD_DIM = 64
SSD_HEADS = SSD_WIDTH // SSD_HEAD_DIM
SSD_GROUPS = 2
SSD_STATE = 128
SSD_CONV = 5
SSD_CHUNK = 128
SSD_CONV_CH = SSD_WIDTH + 2 * SSD_GROUPS * SSD_STATE
SSD_IN = SSD_WIDTH + SSD_CONV_CH + 2 * SSD_HEADS

GLA_WIDTH = GROUP_WIDTH
GLA_HEADS = 4
GLA_DV = GLA_WIDTH // GLA_HEADS
GLA_DK = GLA_DV // 2
GLA_QK = GLA_HEADS * GLA_DK
GLA_GATE_RANK = 16
GLA_TAU = 16.0
GLA_CHUNK = 64
GLA_IN = 2 * GLA_QK + 2 * GLA_WIDTH + 2 * GLA_GATE_RANK

MLA_WIDTH = GROUP_WIDTH
MLA_HEADS = 4
MLA_V = MLA_WIDTH // MLA_HEADS
MLA_NOPE = 64
MLA_ROPE = 32
MLA_Q_RANK = 256
MLA_KV_RANK = 128
MLA_SCALE = (MLA_NOPE + MLA_ROPE) ** -0.5
ROPE_BASE = 10000.0
MLA_IN = MLA_Q_RANK + MLA_KV_RANK + MLA_ROPE

S5_WIDTH = GROUP_WIDTH
S5_GROUP = 16
S5_NGROUPS = S5_WIDTH // S5_GROUP
S5_STATE = 64
S5_MAX_RE = -1e-4
S5_IN = S5_WIDTH
S5_CHUNK = 16
S5_PAIRS = S5_NGROUPS // 2

PEER_KEYS = 128
PEER_EXPERTS = PEER_KEYS * PEER_KEYS
PEER_HEADS = 8
PEER_TOPK = 16
PEER_DQ = 128

LANES = 128
ROW_TILE = 256

F32 = jnp.float32
BF16 = jnp.bfloat16

COL_XS, COL_BM, COL_CM, COL_Z = 0, 256, 512, 768
COL_GLA_V, COL_GLA_R, COL_CQ, COL_S5 = 1024, 1280, 1536, 1792
COL_GLA_Q, COL_GLA_K, COL_CKV, COL_DT, COL_GLR, COL_KR, COL_KRROT = 2048, 2176, 2304, 2432, 2560, 2688, 2816
P_COLS = 2944

_NN_DIMS = (((1,), (0,)), ((), ()))
_NT_DIMS = (((1,), (1,)), ((), ()))
_TN_DIMS = (((0,), (0,)), ((), ()))


def _mm(a, b, dims=_NN_DIMS):
    return lax.dot_general(a, b, dims, preferred_element_type=F32)


def _split_bf16(x):
    hi = x.astype(BF16)
    lo = (x - hi.astype(F32)).astype(BF16)
    return hi, lo


def _split3_bf16(x):
    p1 = x.astype(BF16)
    r1 = x - p1.astype(F32)
    p2 = r1.astype(BF16)
    p3 = (r1 - p2.astype(F32)).astype(BF16)
    return p1, p2, p3


def _dot3(a_hi, a_lo, b_hi, b_lo, dims):
    return _mm(a_hi, b_hi, dims) + _mm(a_hi, b_lo, dims) + _mm(a_lo, b_hi, dims)


def _gelu_erf(x):
    return 0.5 * x * (1.0 + lax.erf(x * (2.0 ** -0.5)))


def _silu(x):
    return x * jax.nn.sigmoid(x)


def _softplus(x):
    return jnp.maximum(x, 0.0) + jnp.log1p(jnp.exp(-jnp.abs(x)))


def _log_sigmoid(x):
    return jnp.minimum(x, 0.0) - jnp.log1p(jnp.exp(-jnp.abs(x)))


def _rms(x, g):
    return x * lax.rsqrt(jnp.mean(x * x, axis=-1, keepdims=True) + NORM_EPS) * g


def _modulated_norm(x, g, shift, scale):
    return _rms(x, g) * (1.0 + scale) + shift


def _causal_mask(n, reverse):
    ri = lax.broadcasted_iota(jnp.int32, (n, n), 0)
    ci = lax.broadcasted_iota(jnp.int32, (n, n), 1)
    return (ci >= ri) if reverse else (ci <= ri)


def _scan_chunk(s, n_lat, n_ctx, reverse):
    if reverse:
        return n_lat + n_ctx - 1 - s
    return jnp.where(s < n_ctx, n_lat + s, s - n_ctx)


def _inproj_kernel(h_ref, g_ref, shift_ref, scale_ref, w_ref, o_ref):
    xn = _modulated_norm(h_ref[0], g_ref[...], shift_ref[0], scale_ref[0])
    o_ref[0] = _mm(xn.astype(BF16), w_ref[...])


def _inproj(hcomb, norm_g, shift_tab, scale_tab, w_pad, n_lat_tiles):
    b, r, d = hcomb.shape
    mod_spec = pl.BlockSpec((1, 1, d), lambda i, t: (jnp.where(t < n_lat_tiles, i, b), 0, 0))
    return pl.pallas_call(
        _inproj_kernel,
        grid=(b, r // ROW_TILE),
        in_specs=[pl.BlockSpec((1, ROW_TILE, d), lambda i, t: (i, t, 0)),
                  pl.BlockSpec((1, d), lambda i, t: (0, 0)), mod_spec, mod_spec,
                  pl.BlockSpec((d, P_COLS), lambda i, t: (0, 0))],
        out_specs=pl.BlockSpec((1, ROW_TILE, P_COLS), lambda i, t: (i, t, 0)),
        out_shape=jax.ShapeDtypeStruct((b, r, P_COLS), F32),
        compiler_params=pltpu.CompilerParams(dimension_semantics=("arbitrary", "arbitrary"),
                                             vmem_limit_bytes=48 * 2 ** 20),
        name="inproj",
    )(hcomb, norm_g.reshape(1, d), shift_tab, scale_tab, w_pad)


def _pack_w_in(w):
    o_ssd, o_gla, o_mla, o_s5 = 0, SSD_IN, SSD_IN + GLA_IN, SSD_IN + GLA_IN + MLA_IN
    out = jnp.zeros((w.shape[0], P_COLS), F32)
    put = lambda out, col, src, width: out.at[:, col:col + width].set(w[:, src:src + width])
    out = put(out, COL_Z, o_ssd, SSD_WIDTH)
    out = put(out, COL_XS, o_ssd + SSD_WIDTH, SSD_CONV_CH)
    out = put(out, COL_DT, o_ssd + SSD_WIDTH + SSD_CONV_CH, 2 * SSD_HEADS)
    out = put(out, COL_GLA_Q, o_gla, GLA_QK)
    out = put(out, COL_GLA_K, o_gla + GLA_QK, GLA_QK)
    out = put(out, COL_GLA_V, o_gla + 2 * GLA_QK, GLA_WIDTH)
    out = put(out, COL_GLA_R, o_gla + 2 * GLA_QK + GLA_WIDTH, GLA_WIDTH)
    out = put(out, COL_GLR, o_gla + 2 * GLA_QK + 2 * GLA_WIDTH, 2 * GLA_GATE_RANK)
    out = put(out, COL_CQ, o_mla, MLA_Q_RANK)
    out = put(out, COL_CKV, o_mla + MLA_Q_RANK, MLA_KV_RANK)
    o_kr = o_mla + MLA_Q_RANK + MLA_KV_RANK
    half = MLA_ROPE // 2
    out = put(out, COL_KR + MLA_NOPE, o_kr, MLA_ROPE)
    out = out.at[:, COL_KRROT + MLA_NOPE:COL_KRROT + MLA_NOPE + half].set(-w[:, o_kr + half:o_kr + MLA_ROPE])
    out = out.at[:, COL_KRROT + MLA_NOPE + half:COL_KRROT + MLA_NOPE + MLA_ROPE].set(w[:, o_kr:o_kr + half])
    out = put(out, COL_S5, o_s5, S5_WIDTH)
    return out.astype(BF16)


def _ssd_prep_kernel(x_ref, prev_ref, next_ref, dt_ref, w_ref, b_ref, bias_ref, xbc_ref, dtc_ref, dtt_ref,
                     *, n_lat_tiles):
    t = pl.program_id(1)
    x = x_ref[0]
    halo = prev_ref.shape[1]
    prev = jnp.where(jnp.logical_and(t > 0, t < n_lat_tiles), prev_ref[0], 0.0)
    nxt = jnp.where(t < n_lat_tiles - 1, next_ref[0], 0.0)
    ext = jnp.concatenate([prev, x, nxt], axis=0)
    rows = ext.shape[0]
    left = SSD_CONV // 2
    acc = jnp.zeros_like(x) + b_ref[...]
    for k in range(SSD_CONV):
        shifted = ext if k == left else pltpu.roll(ext, (left - k) % rows, 0)
        acc = acc + w_ref[k:k + 1, :] * shifted[halo:halo + x.shape[0]]
    xbc_ref[0] = _silu(acc)
    dt = _softplus(dt_ref[0] + bias_ref[...])
    dtc_ref[0] = dt
    dtt_ref[0] = dt.T[:dtt_ref.shape[1]]


def _ssd_scan_kernel(xbc_f_ref, dtc_f_ref, dtt_f_ref, xbc_b_ref, dtc_b_ref, dtt_b_ref, ahr_ref, ahc_ref,
                     yf_ref, yb_ref, state_ref):
    @pl.when(pl.program_id(1) == 0)
    def _():
        state_ref[...] = jnp.zeros_like(state_ref)

    _ssd_chunk(xbc_f_ref, dtc_f_ref, dtt_f_ref, ahr_ref, ahc_ref, yf_ref, state_ref.at[0], 0)
    _ssd_chunk(xbc_b_ref, dtc_b_ref, dtt_b_ref, ahr_ref, ahc_ref, yb_ref, state_ref.at[1], 1)


def _ssd_chunk(xbc_ref, dtc_ref, dtt_ref, ahr_ref, ahc_ref, y_ref, state_ref, direction):
    reverse = direction == 1
    q = SSD_CHUNK
    mask = _causal_mask(q, reverse)
    tri = jnp.where(mask, 1.0, 0.0).astype(BF16)
    xbc = xbc_ref[0]
    xs, bm, cm = xbc[:, :SSD_WIDTH], xbc[:, SSD_WIDTH:SSD_WIDTH + 256], xbc[:, SSD_WIDTH + 256:]
    dtc = dtc_ref[0]
    a_col = dtc * ahr_ref[...]
    a_row = dtt_ref[0] * ahc_ref[...]
    acum_col = sum(_mm(tri, part) for part in _split3_bf16(a_col))
    acum_row = sum(_mm(part, tri, _NT_DIMS) for part in _split3_bf16(a_row))
    end = 0 if reverse else q - 1
    bm_bf, cm_bf = bm.astype(BF16), cm.astype(BF16)
    ys = []
    cb = {}
    for h in range(SSD_HEADS):
        g = h // (SSD_HEADS // SSD_GROUPS)
        gs = slice(g * SSD_STATE, (g + 1) * SSD_STATE)
        if g not in cb:
            cb[g] = _mm(cm_bf[:, gs], bm_bf[:, gs], _NT_DIMS)
        ch = direction * SSD_HEADS + h
        ac = acum_col[:, ch:ch + 1]
        ar = acum_row[ch:ch + 1, :]
        decay = jnp.exp(jnp.where(mask, ac - ar, -jnp.inf))
        xd = xs[:, h * SSD_HEAD_DIM:(h + 1) * SSD_HEAD_DIM] * dtc[:, ch:ch + 1]
        y_diag = _mm((cb[g] * decay).astype(BF16), xd.astype(BF16))
        a_end = ac[end:end + 1, :]
        st_local = _mm((xd * jnp.exp(a_end - ac)).astype(BF16), bm_bf[:, gs], _TN_DIMS)
        hs = state_ref[h]
        y_off = jnp.exp(ac) * _mm(cm_bf[:, gs], hs.astype(BF16), _NT_DIMS)
        state_ref[h] = jnp.exp(a_end) * hs + st_local
        ys.append(y_diag + y_off)
    y_ref[0] = jnp.concatenate(ys, axis=1)


def _ssd_mixer(p, conv_w, conv_b, a_log, dt_bias, n_lat_tiles):
    b, r, _ = p.shape
    nt = r // ROW_TILE
    halo = 8
    hb = ROW_TILE // halo
    w8 = jnp.zeros((8, SSD_CONV_CH), F32).at[:SSD_CONV].set(conv_w)
    bias = jnp.zeros((1, LANES), F32).at[0, :2 * SSD_HEADS].set(dt_bias.reshape(-1))
    xbc, dtc, dtt = pl.pallas_call(
        functools.partial(_ssd_prep_kernel, n_lat_tiles=n_lat_tiles),
        grid=(b, nt),
        in_specs=[pl.BlockSpec((1, ROW_TILE, SSD_CONV_CH), lambda i, t: (i, t, 0)),
                  pl.BlockSpec((1, halo, SSD_CONV_CH), lambda i, t: (i, jnp.maximum(t * hb - 1, 0), 0)),
                  pl.BlockSpec((1, halo, SSD_CONV_CH), lambda i, t: (i, jnp.minimum((t + 1) * hb, nt * hb - 1), 0)),
                  pl.BlockSpec((1, ROW_TILE, LANES), lambda i, t: (i, t, COL_DT // LANES)),
                  pl.BlockSpec((8, SSD_CONV_CH), lambda i, t: (0, 0)),
                  pl.BlockSpec((1, SSD_CONV_CH), lambda i, t: (0, 0)),
                  pl.BlockSpec((1, LANES), lambda i, t: (0, 0))],
        out_specs=[pl.BlockSpec((1, ROW_TILE, SSD_CONV_CH), lambda i, t: (i, t, 0)),
                   pl.BlockSpec((1, ROW_TILE, LANES), lambda i, t: (i, t, 0)),
                   pl.BlockSpec((1, 8, ROW_TILE), lambda i, t: (i, 0, t))],
        out_shape=[jax.ShapeDtypeStruct((b, r, SSD_CONV_CH), F32),
                   jax.ShapeDtypeStruct((b, r, LANES), F32),
                   jax.ShapeDtypeStruct((b, 8, r), F32)],
        compiler_params=pltpu.CompilerParams(dimension_semantics=("arbitrary", "arbitrary")),
        name="ssd_prep",
    )(p, p, p, p, w8, conv_b.reshape(1, -1), bias)

    a_head = -jnp.exp(a_log.astype(F32)).reshape(-1)
    ahr = jnp.zeros((1, LANES), F32).at[0, :2 * SSD_HEADS].set(a_head)
    ahc = a_head.reshape(2 * SSD_HEADS, 1)
    n_lat = n_lat_tiles * ROW_TILE // SSD_CHUNK
    n_ctx = r // SSD_CHUNK - n_lat
    in_specs, y_specs = [], []
    for reverse in (False, True):
        cidx = functools.partial(_scan_chunk, n_lat=n_lat, n_ctx=n_ctx, reverse=reverse)
        in_specs += [pl.BlockSpec((1, SSD_CHUNK, SSD_CONV_CH), lambda i, s, cidx=cidx: (i, cidx(s), 0)),
                     pl.BlockSpec((1, SSD_CHUNK, LANES), lambda i, s, cidx=cidx: (i, cidx(s), 0)),
                     pl.BlockSpec((1, 8, SSD_CHUNK), lambda i, s, cidx=cidx: (i, 0, cidx(s)))]
        y_specs.append(pl.BlockSpec((1, SSD_CHUNK, SSD_WIDTH), lambda i, s, cidx=cidx: (i, cidx(s), 0)))
    in_specs += [pl.BlockSpec((1, LANES), lambda i, s: (0, 0)), pl.BlockSpec((2 * SSD_HEADS, 1), lambda i, s: (0, 0))]
    y_f, y_b = pl.pallas_call(
        _ssd_scan_kernel,
        grid=(b, n_lat + n_ctx),
        in_specs=in_specs,
        out_specs=y_specs,
        out_shape=[jax.ShapeDtypeStruct((b, r, SSD_WIDTH), F32)] * 2,
        scratch_shapes=[pltpu.VMEM((2, SSD_HEADS, SSD_HEAD_DIM, SSD_STATE), F32)],
        compiler_params=pltpu.CompilerParams(dimension_semantics=("arbitrary", "arbitrary")),
        name="ssd_scan",
    )(xbc, dtc, dtt, xbc, dtc, dtt, ahr, ahc)
    return y_f, y_b, xbc


def _gla_scan_kernel(qf_ref, kf_ref, vf_ref, glrf_ref, qb_ref, kb_ref, vb_ref, glrb_ref, wg_ref, bias_ref,
                     of_ref, ob_ref, st_ref):
    @pl.when(pl.program_id(1) == 0)
    def _():
        st_ref[...] = jnp.zeros_like(st_ref)

    _gla_chunk(qf_ref, kf_ref, vf_ref, glrf_ref, wg_ref.at[0], bias_ref.at[0], of_ref, st_ref.at[0], False)
    _gla_chunk(qb_ref, kb_ref, vb_ref, glrb_ref, wg_ref.at[1], bias_ref.at[1], ob_ref, st_ref.at[1], True)


def _gla_chunk(q_ref, k_ref, v_ref, glr_ref, wg_ref, bias_ref, o_ref, st_ref, reverse):
    n = GLA_CHUNK
    mask = _causal_mask(n, reverse)
    tri = jnp.where(mask, 1.0, 0.0).astype(BF16)
    g_hi, g_lo = _split_bf16(glr_ref[0])
    logits = _dot3(g_hi, g_lo, wg_ref[0], wg_ref[1], _NN_DIMS) + bias_ref[...]
    logg = _log_sigmoid(logits) * (1.0 / GLA_TAU)
    bcum = sum(_mm(tri, part) for part in _split3_bf16(logg))
    end = 0 if reverse else n - 1
    b_end = bcum[end:end + 1, :]
    q, k, v = q_ref[0], k_ref[0], v_ref[0]
    qe = q * jnp.exp(bcum) * (GLA_DK ** -0.5)
    ke = (k * jnp.exp(-bcum)).astype(BF16)
    kd = k * jnp.exp(b_end - bcum)
    decay_end = jnp.exp(b_end)
    lane_head = lax.broadcasted_iota(jnp.int32, (1, GLA_QK), 1) >> (GLA_DK.bit_length() - 1)
    outs = []
    for h in range(GLA_HEADS):
        hm = lane_head == h
        qh = jnp.where(hm, qe, 0.0).astype(BF16)
        att = jnp.where(mask, _mm(qh, ke, _NT_DIMS), 0.0)
        vh = v[:, h * GLA_DV:(h + 1) * GLA_DV].astype(BF16)
        st = st_ref[h]
        o_h = _mm(att.astype(BF16), vh) + _mm(qh, st.astype(BF16), _NT_DIMS)
        local = _mm(vh, jnp.where(hm, kd, 0.0).astype(BF16), _TN_DIMS)
        st_ref[h] = st * decay_end + local
        outs.append(o_h)
    o_ref[0] = jnp.concatenate(outs, axis=1)


def _gla_mixer(p, gate_w, gate_b, n_lat_tiles):
    b, r, _ = p.shape
    n_lat = n_lat_tiles * ROW_TILE // GLA_CHUNK
    n_ctx = r // GLA_CHUNK - n_lat
    in_specs, o_specs, wgs = [], [], []
    for direction in (0, 1):
        cidx = functools.partial(_scan_chunk, n_lat=n_lat, n_ctx=n_ctx, reverse=direction == 1)
        wg = jnp.zeros((LANES, GLA_QK), F32).at[direction * GLA_GATE_RANK:(direction + 1) * GLA_GATE_RANK].set(
            gate_w[direction])
        wgs.append(jnp.stack(_split_bf16(wg)))
        blk = lambda width, col, cidx=cidx: pl.BlockSpec((1, GLA_CHUNK, width),
                                                         lambda i, s: (i, cidx(s), col // width))
        in_specs += [blk(GLA_QK, COL_GLA_Q), blk(GLA_QK, COL_GLA_K), blk(GLA_WIDTH, COL_GLA_V), blk(LANES, COL_GLR)]
        o_specs.append(pl.BlockSpec((1, GLA_CHUNK, GLA_WIDTH), lambda i, s, cidx=cidx: (i, cidx(s), 0)))
    in_specs += [pl.BlockSpec((2, 2, LANES, GLA_QK), lambda i, s: (0, 0, 0, 0)),
                 pl.BlockSpec((2, 1, GLA_QK), lambda i, s: (0, 0, 0))]
    return pl.pallas_call(
        _gla_scan_kernel,
        grid=(b, n_lat + n_ctx),
        in_specs=in_specs,
        out_specs=o_specs,
        out_shape=[jax.ShapeDtypeStruct((b, r, GLA_WIDTH), F32)] * 2,
        scratch_shapes=[pltpu.VMEM((2, GLA_HEADS, GLA_DV, GLA_QK), F32)],
        compiler_params=pltpu.CompilerParams(dimension_semantics=("arbitrary", "arbitrary")),
        name="gla_scan",
    )(p, p, p, p, p, p, p, p, jnp.stack(wgs), gate_b.reshape(2, 1, GLA_QK))


MLA_Q_TILE = 1024
MLA_K_TILES = (768, 256)


def _mla_prep_kernel(cq_ref, ckv_ref, kr_ref, krrot_ref, onec_ref, sinr_ref, gq_ref, gkv_ref,
                     wq_ref, wqr_ref, wk_ref, wv_ref, q_ref, k_ref, v_ref):
    qn = _rms(cq_ref[0], gq_ref[...]).astype(BF16)
    kvn = _rms(ckv_ref[0], gkv_ref[...]).astype(BF16)
    onec, sinr = onec_ref[...], sinr_ref[...]
    k_rope = kr_ref[0] * onec + krrot_ref[0] * sinr
    ones_lane = jnp.where(lax.broadcasted_iota(jnp.int32, (1, LANES), 1) == MLA_V, 1.0, 0.0)
    for h in range(MLA_HEADS):
        qh = _mm(qn, wq_ref[h]) * onec + _mm(qn, wqr_ref[h]) * sinr
        q_ref[0, h] = (qh * MLA_SCALE).astype(BF16)
        k_ref[0, h] = (_mm(kvn, wk_ref[h]) + k_rope).astype(BF16)
        v_ref[0, h] = (_mm(kvn, wv_ref[h]) + ones_lane).astype(BF16)


def _mla_attn_kernel(q_ref, k_ref, v_ref, o_ref, m_ref, acc_ref):
    j = pl.program_id(2)

    @pl.when(j == 0)
    def _():
        m_ref[...] = jnp.full_like(m_ref, -jnp.inf)
        acc_ref[...] = jnp.zeros_like(acc_ref)

    reps = k_ref.shape[2] // LANES
    for h in range(MLA_HEADS):
        s = _mm(q_ref[0, h], k_ref[0, h], _NT_DIMS)
        m_prev = m_ref[h]
        m_new = jnp.maximum(m_prev, jnp.max(s, axis=1, keepdims=True))
        p = jnp.exp((s - jnp.concatenate([m_new] * reps, axis=1)).astype(BF16))
        acc_ref[h] = jnp.exp(m_prev - m_new) * acc_ref[h] + _mm(p, v_ref[0, h])
        m_ref[h] = m_new

    @pl.when(j == pl.num_programs(2) - 1)
    def _():
        outs = []
        for h in range(MLA_HEADS):
            acc = acc_ref[h]
            outs.append(acc[:, :MLA_V] / acc[:, MLA_V:MLA_V + 1])
        o_ref[0] = jnp.concatenate(outs, axis=1)


def _rope_tables(n_lat, n_rows):
    rows = n_lat // GRID_W
    row = jnp.repeat(jnp.arange(rows, dtype=F32), GRID_W)
    col = jnp.tile(jnp.arange(GRID_W, dtype=F32), rows)
    half = MLA_ROPE // 2
    inv = ROPE_BASE ** (-jnp.arange(0, half, 2, dtype=F32) / half)
    ang = jnp.concatenate([row[:, None] * inv, col[:, None] * inv], axis=-1)
    cos = jnp.concatenate([jnp.cos(ang), jnp.ones((n_rows - n_lat, half), F32)], axis=0)
    sin = jnp.concatenate([jnp.sin(ang), jnp.zeros((n_rows - n_lat, half), F32)], axis=0)
    pad = jnp.zeros((n_rows, LANES - MLA_NOPE - MLA_ROPE), F32)
    onec = jnp.concatenate([jnp.ones((n_rows, MLA_NOPE), F32), cos, cos, pad], axis=1)
    sinr = jnp.concatenate([jnp.zeros((n_rows, MLA_NOPE), F32), sin, sin, pad], axis=1)
    return onec, sinr


def _mla_weights(w_uq, w_ukv):
    dqk = MLA_NOPE + MLA_ROPE
    half = MLA_ROPE // 2
    wq = w_uq.reshape(MLA_Q_RANK, MLA_HEADS, dqk).transpose(1, 0, 2)
    rot = jnp.concatenate([jnp.zeros_like(wq[..., :MLA_NOPE]), -wq[..., MLA_NOPE + half:], wq[..., MLA_NOPE:MLA_NOPE + half]],
                          axis=-1)
    padq = lambda w: jnp.pad(w, ((0, 0), (0, 0), (0, LANES - dqk))).astype(BF16)
    wkv = w_ukv.reshape(MLA_KV_RANK, MLA_HEADS, MLA_NOPE + MLA_V).transpose(1, 0, 2)
    padk = lambda w: jnp.pad(w, ((0, 0), (0, 0), (0, LANES - w.shape[-1]))).astype(BF16)
    return padq(wq), padq(rot), padk(wkv[..., :MLA_NOPE]), padk(wkv[..., MLA_NOPE:])


def _mla_attention(q, k, v, q_tile, q_off, n_q, kt, k_off, n_k):
    b = q.shape[0]
    return pl.pallas_call(
        _mla_attn_kernel,
        grid=(b, n_q, n_k),
        in_specs=[pl.BlockSpec((1, MLA_HEADS, q_tile, LANES), lambda i, a, j: (i, 0, q_off + a, 0)),
                  pl.BlockSpec((1, MLA_HEADS, kt, LANES), lambda i, a, j: (i, 0, k_off + j, 0)),
                  pl.BlockSpec((1, MLA_HEADS, kt, LANES), lambda i, a, j: (i, 0, k_off + j, 0))],
        out_specs=pl.BlockSpec((1, q_tile, MLA_WIDTH), lambda i, a, j: (i, a, 0)),
        out_shape=jax.ShapeDtypeStruct((b, n_q * q_tile, MLA_WIDTH), F32),
        scratch_shapes=[pltpu.VMEM((MLA_HEADS, q_tile, LANES), F32), pltpu.VMEM((MLA_HEADS, q_tile, LANES), F32)],
        compiler_params=pltpu.CompilerParams(dimension_semantics=("arbitrary", "arbitrary", "arbitrary")),
        name="mla_attn",
    )(q, k, v)


def _mla_mixer(p, q_norm_g, w_uq, kv_norm_g, w_ukv, n_lat_tiles, ctx_out):
    b, r, _ = p.shape
    nt = r // ROW_TILE
    n_lat = n_lat_tiles * ROW_TILE
    onec, sinr = _rope_tables(n_lat, r)
    wq, wqr, wk, wv = _mla_weights(w_uq, w_ukv)
    blk = lambda width, col: pl.BlockSpec((1, ROW_TILE, width), lambda i, t: (i, t, col // width))
    tab = pl.BlockSpec((ROW_TILE, LANES), lambda i, t: (t, 0))
    full = lambda *shape: pl.BlockSpec(shape, lambda i, t: (0,) * len(shape))
    head_out = pl.BlockSpec((1, MLA_HEADS, ROW_TILE, LANES), lambda i, t: (i, 0, t, 0))
    q, k, v = pl.pallas_call(
        _mla_prep_kernel,
        grid=(b, nt),
        in_specs=[blk(MLA_Q_RANK, COL_CQ), blk(LANES, COL_CKV), blk(LANES, COL_KR), blk(LANES, COL_KRROT), tab, tab,
                  full(1, MLA_Q_RANK), full(1, MLA_KV_RANK),
                  full(MLA_HEADS, MLA_Q_RANK, LANES), full(MLA_HEADS, MLA_Q_RANK, LANES),
                  full(MLA_HEADS, MLA_KV_RANK, LANES), full(MLA_HEADS, MLA_KV_RANK, LANES)],
        out_specs=[head_out] * 3,
        out_shape=[jax.ShapeDtypeStruct((b, MLA_HEADS, r, LANES), BF16)] * 3,
        compiler_params=pltpu.CompilerParams(dimension_semantics=("arbitrary", "arbitrary")),
        name="mla_prep",
    )(p, p, p, p, onec, sinr, q_norm_g.reshape(1, -1), kv_norm_g.reshape(1, -1), wq, wqr, wk, wv)
    q_tile = min(MLA_Q_TILE, n_lat)
    k_tile = next(t for t in MLA_K_TILES if r % t == 0)
    y_lat = _mla_attention(q, k, v, q_tile, 0, n_lat // q_tile, k_tile, 0, r // k_tile)
    y_ctx = None
    if ctx_out:
        n_ctx = r - n_lat
        y_ctx = _mla_attention(q, k, v, n_ctx, n_lat // n_ctx, 1, n_ctx, n_lat // n_ctx, 1)
    return y_lat, y_ctx


def _s5_matrices(a_re, a_im, log_dt, b_re, b_im, c_re, c_im):
    q, ng, ns, nc = S5_CHUNK, S5_NGROUPS, S5_STATE, S5_GROUP
    lam = jnp.minimum(a_re.astype(F32), S5_MAX_RE) + 1j * a_im.astype(F32)
    step = jnp.exp(log_dt.astype(F32))[..., None]
    abar = jnp.exp(lam * step)
    bmat = b_re.astype(F32) + 1j * b_im.astype(F32)
    bbar = ((abar - 1.0) / lam)[..., None] * bmat
    cmat = c_re.astype(F32) + 1j * c_im.astype(F32)
    pw = jnp.exp((lam * step)[..., None] * jnp.arange(q + 1, dtype=F32))
    kern = jnp.einsum('dgcn,dgnl,dgnk->dglck', cmat, pw[..., :q], bbar).real
    ii = jnp.arange(q)
    lag_f = ii[None, :] - ii[:, None]
    gather = lambda kd, lag: jnp.where((lag >= 0)[None, :, :, None, None], kd[:, jnp.clip(lag, 0, q - 1)], 0.0)
    t_f = gather(kern[0], lag_f).transpose(0, 1, 4, 2, 3)
    t_b = gather(kern[1], -lag_f).transpose(0, 1, 4, 2, 3)
    t_sum = (t_f + t_b).reshape(ng, q * nc, q * nc)
    pw_f = pw[0][..., q - 1 - ii]
    pw_b = pw[1][..., ii]
    wst = lambda pwd, bb: jnp.einsum('gnj,gnc->gjcn', pwd, bb).reshape(ng, q * nc, ns)
    wst_f, wst_b = wst(pw_f, bbar[0]), wst(pw_b, bbar[1])
    wout = lambda pwd, cm: jnp.einsum('gcn,gni->gnic', cm, pwd).reshape(ng, ns, q * nc)
    wo_f, wo_b = wout(pw[0][..., ii + 1], cmat[0]), wout(pw[1][..., q - ii], cmat[1])
    aq = pw[..., q]

    def pair_cols(x):
        x = x.reshape(S5_PAIRS, 2, x.shape[1], x.shape[2])
        z = jnp.zeros_like(x[:, 0])
        return jnp.concatenate([jnp.concatenate([x[:, 0], z], axis=2), jnp.concatenate([z, x[:, 1]], axis=2)], axis=1)

    w_local = jnp.concatenate([pair_cols(wst_f.real), pair_cols(wst_f.imag),
                               pair_cols(wst_b.real), pair_cols(wst_b.imag)], axis=2)
    w_out = jnp.concatenate([pair_cols(t_sum), pair_cols(wo_f.real), pair_cols(-wo_f.imag),
                             pair_cols(wo_b.real), pair_cols(-wo_b.imag)], axis=1)
    aq_pair = aq.reshape(2, S5_PAIRS, 2 * ns)
    aq_tab = jnp.concatenate([aq_pair[0].real, aq_pair[0].imag, aq_pair[1].real, aq_pair[1].imag], axis=1)
    return w_local.astype(BF16), w_out.astype(BF16), aq_tab.reshape(S5_PAIRS, 1, 8 * ns).astype(F32)


def _s5_perm():
    cols = S5_CHUNK * S5_WIDTH
    c = jnp.arange(cols, dtype=jnp.int32)
    cc, j = c % S5_GROUP, (c // S5_GROUP) % S5_CHUNK
    g = c // (S5_GROUP * S5_CHUNK)
    per_half = LANES // S5_GROUP
    src = (g // per_half) * (S5_CHUNK * LANES) + j * LANES + (g % per_half) * S5_GROUP + cc
    return jnp.where(c[:, None] == src[None, :], 1.0, 0.0).astype(BF16)


def _s5_pack_kernel(u_ref, o_ref):
    n = o_ref.shape[1]
    for j in range(S5_CHUNK):
        o_ref[0, :, j * LANES:(j + 1) * LANES] = u_ref[0, pl.ds(j, n, stride=S5_CHUNK), :].astype(BF16)


def _s5_unpack_kernel(y_ref, o_ref):
    n = y_ref.shape[1]
    for i in range(S5_CHUNK):
        o_ref[0, pl.ds(i, n, stride=S5_CHUNK), :] = y_ref[0, :, i * LANES:(i + 1) * LANES]


def _s5_local_kernel(u_ref, perm_ref, w_ref, up_ref, s_ref):
    up = _mm(u_ref[0], perm_ref[...]).astype(BF16)
    up_ref[0] = up
    s_ref[0] = _mm(up, w_ref[0])


def _s5_scan_kernel(s3_ref, aq_ref, hs3_ref, *, n_lat, n_ctx, nb):
    s_ref, hs_ref = s3_ref.at[0], hs3_ref.at[0]
    w = 2 * S5_STATE
    aq = aq_ref[0]
    a = [aq[:, i * w:(i + 1) * w] for i in range(4)]
    zero = jnp.zeros((nb, w), F32)
    slab = 8
    cps = slab // nb

    def run_slab(s_re, s_im, a_re, a_im, h_re, h_im, order):
        ent_re, ent_im = [None] * cps, [None] * cps
        for c in order:
            ent_re[c], ent_im[c] = h_re, h_im
            rows = slice(c * nb, (c + 1) * nb)
            h_re, h_im = a_re * h_re - a_im * h_im + s_re[rows], a_re * h_im + a_im * h_re + s_im[rows]
        return jnp.concatenate(ent_re, axis=0), jnp.concatenate(ent_im, axis=0), h_re, h_im

    def body(kk, carry):
        f_re, f_im, b_re, b_im = carry
        rf = pl.multiple_of(_scan_chunk(kk, n_lat // cps, n_ctx // cps, False) * slab, slab)
        rb = pl.multiple_of(_scan_chunk(kk, n_lat // cps, n_ctx // cps, True) * slab, slab)
        e_re, e_im, f_re, f_im = run_slab(s_ref[pl.ds(rf, slab), 0:w], s_ref[pl.ds(rf, slab), w:2 * w],
                                          a[0], a[1], f_re, f_im, range(cps))
        hs_ref[pl.ds(rf, slab), 0:w] = e_re
        hs_ref[pl.ds(rf, slab), w:2 * w] = e_im
        e_re, e_im, b_re, b_im = run_slab(s_ref[pl.ds(rb, slab), 2 * w:3 * w], s_ref[pl.ds(rb, slab), 3 * w:4 * w],
                                          a[2], a[3], b_re, b_im, range(cps - 1, -1, -1))
        hs_ref[pl.ds(rb, slab), 2 * w:3 * w] = e_re
        hs_ref[pl.ds(rb, slab), 3 * w:4 * w] = e_im
        return f_re, f_im, b_re, b_im

    lax.fori_loop(0, (n_lat + n_ctx) // cps, body, (zero, zero, zero, zero))


def _s5_out_kernel(up_ref, hs_ref, w_ref, perm_ref, y_ref):
    @pl.when(pl.program_id(1) == 0)
    def _():
        y_ref[...] = jnp.zeros_like(y_ref)

    kw = up_ref.shape[2]
    y_pair = _mm(up_ref[0], w_ref[0, :kw]) + _mm(hs_ref[0].astype(BF16), w_ref[0, kw:])
    y_hi, y_lo = _split_bf16(y_pair)
    y_ref[0] += _mm(y_hi, perm_ref[...], _NT_DIMS) + _mm(y_lo, perm_ref[...], _NT_DIMS)


def _s5_mixer(p, a_re, a_im, log_dt, b_re, b_im, c_re, c_im, n_lat_tiles):
    b, r, _ = p.shape
    q = S5_CHUNK
    n_chunks = r // q
    cols = q * S5_WIDTH
    kw = 2 * q * S5_GROUP
    w_local, w_out, aq_tab = _s5_matrices(a_re, a_im, log_dt, b_re, b_im, c_re, c_im)
    perm = _s5_perm()
    cp2 = pltpu.CompilerParams(dimension_semantics=("arbitrary", "arbitrary"), vmem_limit_bytes=48 * 2 ** 20)
    cpt = ROW_TILE // q
    halves = S5_WIDTH // LANES
    cp3 = pltpu.CompilerParams(dimension_semantics=("arbitrary", "arbitrary", "arbitrary"))
    chunk_rows = pl.BlockSpec((1, cpt, q * LANES), lambda i, t, hf: (i, t, hf))
    u_big = pl.pallas_call(
        _s5_pack_kernel,
        grid=(b, r // ROW_TILE, halves),
        in_specs=[pl.BlockSpec((1, ROW_TILE, LANES), lambda i, t, hf: (i, t, COL_S5 // LANES + hf))],
        out_specs=chunk_rows,
        out_shape=jax.ShapeDtypeStruct((b, n_chunks, cols), BF16),
        compiler_params=cp3, name="s5_pack",
    )(p)
    all_chunks = pl.BlockSpec((1, n_chunks, cols), lambda i, g: (i, 0, 0))
    col_tile = lambda width: pl.BlockSpec((1, n_chunks, width), lambda i, g: (i, 0, g))
    perm_cols = pl.BlockSpec((cols, kw), lambda i, g: (0, g))
    u_pairs, s_loc = pl.pallas_call(
        _s5_local_kernel,
        grid=(b, S5_PAIRS),
        in_specs=[all_chunks, perm_cols, pl.BlockSpec((1, kw, kw), lambda i, g: (g, 0, 0))],
        out_specs=[col_tile(kw), col_tile(kw)],
        out_shape=[jax.ShapeDtypeStruct((b, n_chunks, cols), BF16), jax.ShapeDtypeStruct((b, n_chunks, cols), F32)],
        compiler_params=cp2, name="s5_local",
    )(u_big, perm, w_local)
    n_lat = n_lat_tiles * ROW_TILE // q
    hs = pl.pallas_call(
        functools.partial(_s5_scan_kernel, n_lat=n_lat, n_ctx=n_chunks - n_lat, nb=1),
        grid=(b, S5_PAIRS),
        in_specs=[col_tile(kw), pl.BlockSpec((1, 1, kw), lambda i, g: (g, 0, 0))],
        out_specs=col_tile(kw),
        out_shape=jax.ShapeDtypeStruct((b, n_chunks, cols), F32),
        compiler_params=cp2, name="s5_scan",
    )(s_loc, aq_tab)
    y_big = pl.pallas_call(
        _s5_out_kernel,
        grid=(b, S5_PAIRS),
        in_specs=[col_tile(kw), col_tile(kw), pl.BlockSpec((1, 2 * kw, kw), lambda i, g: (g, 0, 0)), perm_cols],
        out_specs=all_chunks,
        out_shape=jax.ShapeDtypeStruct((b, n_chunks, cols), F32),
        compiler_params=cp2, name="s5_out",
    )(u_pairs, hs, w_out, perm)
    return pl.pallas_call(
        _s5_unpack_kernel,
        grid=(b, r // ROW_TILE, halves),
        in_specs=[chunk_rows],
        out_specs=pl.BlockSpec((1, ROW_TILE, LANES), lambda i, t, hf: (i, t, hf)),
        out_shape=jax.ShapeDtypeStruct((b, r, S5_WIDTH), F32),
        compiler_params=cp3, name="s5_unpack",
    )(y_big)


def _post_kernel(h_ref, xs_ref, z_ref, r_ref, u_ref, ssd_ref, ssd_b_ref, gla_ref, gla_b_ref, mla_ref, s5_ref,
                 ssd_d_ref, ssd_g_ref, gla_g_ref, s5_d_ref, glu_w_ref, glu_b_ref, w_out_ref, mod_ref, o_ref):
    y = ssd_ref[0] + ssd_b_ref[0] + ssd_d_ref[...] * xs_ref[0]
    ssd = _rms(y * _silu(z_ref[0]), ssd_g_ref[...])
    o = gla_ref[0] + gla_b_ref[0]
    lane_head = lax.broadcasted_iota(jnp.int32, (1, GLA_WIDTH), 1) >> (GLA_DV.bit_length() - 1)
    ms = jnp.zeros_like(o)
    for h in range(GLA_HEADS):
        oh = o[:, h * GLA_DV:(h + 1) * GLA_DV]
        ms = jnp.where(lane_head == h, jnp.mean(oh * oh, axis=-1, keepdims=True), ms)
    gla = o * lax.rsqrt(ms + NORM_EPS) * gla_g_ref[...] * _silu(r_ref[0])
    y5 = _gelu_erf(s5_ref[0] + s5_d_ref[...] * u_ref[0])
    s5 = y5 * jax.nn.sigmoid(_mm(y5.astype(BF16), glu_w_ref[...]) + glu_b_ref[...])
    mix_in = jnp.concatenate([ssd, gla, mla_ref[0], s5], axis=1).astype(BF16)
    o_ref[0] = h_ref[0] + mod_ref[0] * _mm(mix_in, w_out_ref[...])


def _post(h, p, ssd_xbc, ssd_y, ssd_yb, gla_o, gla_ob, mla_y, s5_y, ssd_d, ssd_norm_g, gla_norm_g, s5_d, glu_w, glu_b, w_out, mod,
          row_off, mla_off):
    b, rows, d = h.shape
    w = GROUP_WIDTH
    pblk = lambda col: pl.BlockSpec((1, ROW_TILE, w), lambda i, t: (i, row_off + t, col // w))
    yblk = pl.BlockSpec((1, ROW_TILE, w), lambda i, t: (i, row_off + t, 0))
    full = lambda *shape: pl.BlockSpec(shape, lambda i, t: (0,) * len(shape))
    vec = lambda x: x.reshape(1, -1).astype(F32)
    n_mod = mod.shape[0]
    return pl.pallas_call(
        _post_kernel,
        grid=(b, rows // ROW_TILE),
        in_specs=[pl.BlockSpec((1, ROW_TILE, d), lambda i, t: (i, t, 0)),
                  yblk, pblk(COL_Z), pblk(COL_GLA_R), pblk(COL_S5), yblk, yblk, yblk, yblk,
                  pl.BlockSpec((1, ROW_TILE, w), lambda i, t: (i, mla_off + t, 0)), yblk,
                  full(1, w), full(1, w), full(1, w), full(1, w), full(w, w), full(1, w), full(d, d),
                  pl.BlockSpec((1, 1, d), lambda i, t: (jnp.minimum(i, n_mod - 1), 0, 0))],
        out_specs=pl.BlockSpec((1, ROW_TILE, d), lambda i, t: (i, t, 0)),
        out_shape=jax.ShapeDtypeStruct((b, rows, d), F32),
        compiler_params=pltpu.CompilerParams(dimension_semantics=("arbitrary", "arbitrary")),
        name="mix_post",
    )(h, ssd_xbc, p, p, p, ssd_y, ssd_yb, gla_o, gla_ob, mla_y, s5_y,
      vec(jnp.repeat(ssd_d, SSD_HEAD_DIM)), vec(ssd_norm_g), vec(jnp.tile(gla_norm_g, GLA_HEADS)), vec(s5_d),
      glu_w.astype(BF16), vec(glu_b), w_out.astype(BF16), mod)


PEER_ROUTE_TOKENS = 256
PEER_GATE_TOKENS = 256
PEER_GATE_UNROLL = 16
PEER_GATE_SUBLANES = 8
PEER_DENSE_TOKENS = 512
PEER_DENSE_EXPERTS = 2 * PEER_GATE_SUBLANES * PEER_KEYS
PEER_SLOTS = PEER_HEADS * PEER_TOPK


def _topk_rows(s, k):
    n_rows = s.shape[0]
    rows = lax.broadcasted_iota(jnp.int32, s.shape, 0)
    vals, idxs = [], []
    for _ in range(k):
        m = jnp.max(s, axis=0, keepdims=True)
        idx = jnp.min(jnp.where(s == m, rows, n_rows), axis=0, keepdims=True)
        vals.append(m)
        idxs.append(idx)
        s = jnp.where(rows == idx, -jnp.inf, s)
    return jnp.concatenate(vals, axis=0), jnp.concatenate(idxs, axis=0)


def _select_rows(pos, table):
    out = jnp.zeros(pos.shape, table.dtype)
    for r in range(table.shape[0]):
        out = jnp.where(pos == r, table[r:r + 1, :], out)
    return out


def _peer_route_kernel(h_ref, g_ref, shift_ref, scale_ref, wq_hi_ref, wq_lo_ref, k_hi_ref, k_lo_ref,
                       xn_ref, i1_ref, i2_ref, gate_ref, q_scr, slot_scr):
    xn = _modulated_norm(h_ref[...], g_ref[...], shift_ref[0], scale_ref[0])
    xn_ref[...] = xn.astype(BF16)
    x_hi, x_lo = _split_bf16(xn)
    q_scr[...] = _dot3(wq_hi_ref[...], wq_lo_ref[...], x_hi, x_lo, _NT_DIMS)
    half = PEER_DQ // 2

    def head_body(h, carry):
        base = pl.multiple_of(h * PEER_DQ, PEER_DQ)
        tops = []
        for j in range(2):
            qq = q_scr[pl.ds(base + j * half, half), :]
            q_hi, q_lo = _split_bf16(qq)
            s = _dot3(k_hi_ref[j, h], k_lo_ref[j, h], q_hi, q_lo, _NN_DIMS)
            tops.append(_topk_rows(s, PEER_TOPK))
        (v1, i1), (v2, i2) = tops
        pieces = [v1[a:a + 1, :] + v2[:PEER_TOPK // (a + 1), :] for a in range(PEER_TOPK)]
        n_cand = sum(PEER_TOPK // (a + 1) for a in range(PEER_TOPK))
        pad = -n_cand % 8
        cand = jnp.concatenate(pieces + [jnp.full((pad, v1.shape[1]), -jnp.inf, F32)], axis=0)
        best, pos = _topk_rows(cand, PEER_TOPK)
        e = jnp.exp(best - best[0:1, :])
        gates = e / jnp.sum(e, axis=0, keepdims=True)
        a_idx = jnp.zeros_like(pos)
        start = jnp.zeros_like(pos)
        first = 0
        for a in range(1, PEER_TOPK):
            width = PEER_TOPK // a
            first += width
            reached = pos >= first
            a_idx = a_idx + jnp.where(reached, 1, 0)
            start = start + jnp.where(reached, width, 0)
        row0 = pl.multiple_of(h * PEER_TOPK, PEER_TOPK)
        slot_scr[0, pl.ds(row0, PEER_TOPK), :] = _select_rows(a_idx, i1).astype(F32)
        slot_scr[1, pl.ds(row0, PEER_TOPK), :] = _select_rows(pos - start, i2).astype(F32)
        slot_scr[2, pl.ds(row0, PEER_TOPK), :] = gates
        return carry

    lax.fori_loop(0, PEER_HEADS, head_body, 0)
    i1_ref[...] = slot_scr[0].T.astype(jnp.int32)
    i2_ref[...] = slot_scr[1].T.astype(jnp.int32)
    gate_ref[...] = slot_scr[2].T


def _bf16_bits(x):
    return pltpu.bitcast(x.astype(BF16).astype(F32), jnp.uint32)


def _peer_gate_kernel(i1_ref, i2_ref, gate_ref, g_ref):
    rows = lax.broadcasted_iota(jnp.int32, (PEER_KEYS, PEER_SLOTS), 0)
    sub = PEER_GATE_SUBLANES

    def token_body(t, carry):
        a = i1_ref[pl.ds(t, 1), :]
        b = i2_ref[pl.ds(t, 1), :]
        w = gate_ref[pl.ds(t, 1), :]
        lhs = jnp.where(rows == a, w, 0.0).astype(BF16)
        rhs = jnp.where(rows == b, 1.0, 0.0).astype(BF16)
        gt = _mm(lhs, rhs, _NT_DIMS)
        row0 = pl.multiple_of(t * sub, sub)
        for g in range(PEER_KEYS // (2 * sub)):
            lo = gt[2 * sub * g:2 * sub * g + sub]
            hi = gt[2 * sub * g + sub:2 * sub * (g + 1)]
            g_ref[g, pl.ds(row0, sub), :] = (_bf16_bits(lo) >> 16) | _bf16_bits(hi)
        return carry

    lax.fori_loop(0, i1_ref.shape[0], token_body, 0, unroll=PEER_GATE_UNROLL)


def _peer_dense_kernel(xn_ref, u_ref, v_ref, gpk_ref, h_ref, mod_ref, o_ref, acc_ref):
    j = pl.program_id(1)

    @pl.when(j == 0)
    def _():
        acc_ref[...] = jnp.zeros_like(acc_ref)

    sub = PEER_GATE_SUBLANES
    xn = xn_ref[...]
    tokens = xn.shape[0]
    words = [gpk_ref[0, pl.ds(r, tokens, stride=sub), :] for r in range(sub)]
    for half in range(2):
        rows = slice(half * sub * PEER_KEYS, (half + 1) * sub * PEER_KEYS)
        hid = _gelu_erf(_mm(xn, u_ref[rows, :], _NT_DIMS))
        ys = []
        for r in range(sub):
            bits = (words[r] << 16) if half == 0 else (words[r] & jnp.uint32(0xFFFF0000))
            ys.append(pltpu.bitcast(bits, F32) * hid[:, r * PEER_KEYS:(r + 1) * PEER_KEYS])
        y = jnp.concatenate(ys, axis=1).astype(BF16)
        acc_ref[...] += _mm(y, v_ref[rows, :])

    @pl.when(j == pl.num_programs(1) - 1)
    def _():
        o_ref[...] = h_ref[...] + mod_ref[0] * acc_ref[...]


def _peer_layer(h, norm_g, shift, scale, gate_mod, wq_t_hi, wq_t_lo, keys_hi, keys_lo, u_bf, v_bf):
    n, d = h.shape
    nb = shift.shape[0]
    rows_per_batch = n // nb
    tr = min(PEER_ROUTE_TOKENS, rows_per_batch)
    full = lambda *shape: pl.BlockSpec(shape, lambda i: (0,) * len(shape))
    per_batch = lambda t: pl.BlockSpec((1, 1, d), lambda i: (i * t // rows_per_batch, 0, 0))
    xn, i1, i2, gate = pl.pallas_call(
        _peer_route_kernel,
        grid=(n // tr,),
        in_specs=[pl.BlockSpec((tr, d), lambda i: (i, 0)), full(1, d), per_batch(tr), per_batch(tr),
                  full(PEER_HEADS * PEER_DQ, d), full(PEER_HEADS * PEER_DQ, d),
                  full(2, PEER_HEADS, PEER_KEYS, PEER_DQ // 2), full(2, PEER_HEADS, PEER_KEYS, PEER_DQ // 2)],
        out_specs=[pl.BlockSpec((tr, d), lambda i: (i, 0))] + [pl.BlockSpec((tr, PEER_SLOTS), lambda i: (i, 0))] * 3,
        out_shape=[jax.ShapeDtypeStruct((n, d), BF16),
                   jax.ShapeDtypeStruct((n, PEER_SLOTS), jnp.int32),
                   jax.ShapeDtypeStruct((n, PEER_SLOTS), jnp.int32),
                   jax.ShapeDtypeStruct((n, PEER_SLOTS), F32)],
        scratch_shapes=[pltpu.VMEM((PEER_HEADS * PEER_DQ, tr), F32), pltpu.VMEM((3, PEER_SLOTS, tr), F32)],
        compiler_params=pltpu.CompilerParams(dimension_semantics=("arbitrary",)),
        name="peer_route",
    )(h, norm_g.reshape(1, d), shift, scale, wq_t_hi, wq_t_lo, keys_hi, keys_lo)

    tg = min(PEER_GATE_TOKENS, n)
    n_planes = PEER_KEYS // (2 * PEER_GATE_SUBLANES)
    slot_spec = pl.BlockSpec((tg, PEER_SLOTS), lambda i: (i, 0))
    gmat = pl.pallas_call(
        _peer_gate_kernel,
        grid=(n // tg,),
        in_specs=[slot_spec, slot_spec, slot_spec],
        out_specs=pl.BlockSpec((n_planes, tg * PEER_GATE_SUBLANES, PEER_KEYS), lambda i: (0, i, 0)),
        out_shape=jax.ShapeDtypeStruct((n_planes, n * PEER_GATE_SUBLANES, PEER_KEYS), jnp.uint32),
        compiler_params=pltpu.CompilerParams(dimension_semantics=("arbitrary",)),
        name="peer_gate",
    )(i1, i2, gate)

    tm = min(PEER_DENSE_TOKENS, rows_per_batch)
    te = PEER_DENSE_EXPERTS
    return pl.pallas_call(
        _peer_dense_kernel,
        grid=(n // tm, PEER_EXPERTS // te),
        in_specs=[pl.BlockSpec((tm, d), lambda i, j: (i, 0)),
                  pl.BlockSpec((te, d), lambda i, j: (j, 0)),
                  pl.BlockSpec((te, d), lambda i, j: (j, 0)),
                  pl.BlockSpec((1, tm * PEER_GATE_SUBLANES, PEER_KEYS), lambda i, j: (j, i, 0)),
                  pl.BlockSpec((tm, d), lambda i, j: (i, 0)),
                  pl.BlockSpec((1, 1, d), lambda i, j: (i * tm // rows_per_batch, 0, 0))],
        out_specs=pl.BlockSpec((tm, d), lambda i, j: (i, 0)),
        out_shape=jax.ShapeDtypeStruct((n, d), F32),
        scratch_shapes=[pltpu.VMEM((tm, d), F32)],
        compiler_params=pltpu.CompilerParams(dimension_semantics=("arbitrary", "arbitrary"),
                                             vmem_limit_bytes=52 * 2 ** 20),
        name="peer_dense",
    )(xn, u_bf, v_bf, gmat, h, gate_mod)


def _final_norm_kernel(x_ref, g_ref, o_ref):
    o_ref[...] = _rms(x_ref[...], g_ref[...])


def _final_norm(h, g):
    n = h.shape[0] * h.shape[1]
    x2 = h.reshape(n, D_MODEL)
    tm = 512
    out = pl.pallas_call(
        _final_norm_kernel,
        grid=(n // tm,),
        in_specs=[pl.BlockSpec((tm, D_MODEL), lambda i: (i, 0)),
                  pl.BlockSpec((1, D_MODEL), lambda i: (0, 0))],
        out_specs=pl.BlockSpec((tm, D_MODEL), lambda i: (i, 0)),
        out_shape=jax.ShapeDtypeStruct((n, D_MODEL), F32),
        name="final_norm",
    )(x2, g.reshape(1, D_MODEL))
    return out.reshape(h.shape)


def _mix_layer(h_lat, h_ctx, mod_l, mod_c, norm_g, w_in, w_out, ssd, gla, mla, s5, ctx_out):
    b, n_lat, d = h_lat.shape
    n_lat_tiles = n_lat // ROW_TILE
    hcomb = jnp.concatenate([h_lat, h_ctx], axis=1)
    tab = lambda k: jnp.concatenate([mod_l[k], mod_c[k]], axis=0)
    p = _inproj(hcomb, norm_g, tab(0), tab(1), _pack_w_in(w_in), n_lat_tiles)
    ssd_y, ssd_yb, ssd_xbc = _ssd_mixer(p, ssd["conv_w"], ssd["conv_b"], ssd["a_log"], ssd["dt_bias"], n_lat_tiles)
    gla_o, gla_ob = _gla_mixer(p, gla["gate_w"], gla["gate_b"], n_lat_tiles)
    mla_lat, mla_ctx = _mla_mixer(p, mla["q_norm_g"], mla["w_uq"], mla["kv_norm_g"], mla["w_ukv"], n_lat_tiles, ctx_out)
    s5_y = _s5_mixer(p, s5["a_re"], s5["a_im"], s5["log_dt"], s5["b_re"], s5["b_im"], s5["c_re"], s5["c_im"],
                     n_lat_tiles)
    post = functools.partial(_post, p=p, ssd_xbc=ssd_xbc, ssd_y=ssd_y, ssd_yb=ssd_yb, gla_o=gla_o, gla_ob=gla_ob,
                             s5_y=s5_y, ssd_d=ssd["d"],
                             ssd_norm_g=ssd["norm_g"], gla_norm_g=gla["norm_g"], s5_d=s5["d"],
                             glu_w=s5["glu_w"], glu_b=s5["glu_b"], w_out=w_out)
    new_lat = post(h_lat, mla_y=mla_lat, mod=mod_l[2], row_off=0, mla_off=0)
    new_ctx = None
    if ctx_out:
        new_ctx = post(h_ctx, mla_y=mla_ctx, mod=mod_c[2], row_off=n_lat_tiles, mla_off=0)
    return new_lat, new_ctx


def kernel(x, c, ctx, c_ctx, ada_w, ada_b, norm_mix_g, norm_ffn_g, w_in, w_out,
           ssd_conv_w, ssd_conv_b, ssd_a_log, ssd_dt_bias, ssd_d, ssd_norm_g,
           gla_gate_w, gla_gate_b, gla_norm_g, mla_q_norm_g, mla_w_uq, mla_kv_norm_g, mla_w_ukv,
           s5_a_re, s5_a_im, s5_log_dt, s5_b_re, s5_b_im, s5_c_re, s5_c_im, s5_d, s5_glu_w, s5_glu_b,
           peer_w_q, peer_sub_keys, peer_u, peer_v, final_norm_g):
    h_lat, h_ctx = x, ctx
    cond_lat = jax.nn.silu(c)[:, None, :]
    cond_ctx = jax.nn.silu(c_ctx)[None, None, :]
    for i in range(DEPTH):
        ctx_out = i < DEPTH - 1
        mod_l = jnp.split(cond_lat @ ada_w[i] + ada_b[i], N_MOD, axis=-1)
        mod_c = jnp.split(cond_ctx @ ada_w[i] + ada_b[i], N_MOD, axis=-1)
        ssd = dict(conv_w=ssd_conv_w[i], conv_b=ssd_conv_b[i], a_log=ssd_a_log[i], dt_bias=ssd_dt_bias[i],
                   d=ssd_d[i], norm_g=ssd_norm_g[i])
        gla = dict(gate_w=gla_gate_w[i], gate_b=gla_gate_b[i], norm_g=gla_norm_g[i])
        mla = dict(q_norm_g=mla_q_norm_g[i], w_uq=mla_w_uq[i], kv_norm_g=mla_kv_norm_g[i], w_ukv=mla_w_ukv[i])
        s5 = dict(a_re=s5_a_re[i], a_im=s5_a_im[i], log_dt=s5_log_dt[i], b_re=s5_b_re[i], b_im=s5_b_im[i],
                  c_re=s5_c_re[i], c_im=s5_c_im[i], d=s5_d[i], glu_w=s5_glu_w[i], glu_b=s5_glu_b[i])
        h_lat, h_ctx_new = _mix_layer(h_lat, h_ctx, mod_l, mod_c, norm_mix_g[i], w_in[i], w_out[i],
                                      ssd, gla, mla, s5, ctx_out)
        wq_t_hi, wq_t_lo = _split_bf16(peer_w_q[i].T)
        keys_hi, keys_lo = _split_bf16(peer_sub_keys[i])
        u_bf, v_bf = peer_u[i].astype(BF16), peer_v[i].astype(BF16)
        peer = functools.partial(_peer_layer, norm_g=norm_ffn_g[i], wq_t_hi=wq_t_hi, wq_t_lo=wq_t_lo,
                                 keys_hi=keys_hi, keys_lo=keys_lo, u_bf=u_bf, v_bf=v_bf)
        h_lat = peer(h_lat.reshape(-1, D_MODEL), shift=mod_l[3], scale=mod_l[4],
                     gate_mod=mod_l[5]).reshape(h_lat.shape)
        if ctx_out:
            h_ctx = peer(h_ctx_new.reshape(-1, D_MODEL), shift=mod_c[3], scale=mod_c[4],
                         gate_mod=mod_c[5]).reshape(h_ctx.shape)
    return _final_norm(h_lat, final_norm_g)
```

```python
import functools
import jax
import jax.numpy as jnp
from jax import lax
import numpy as np
from jax.experimental import pallas as pl
from jax.experimental.pallas import tpu as pltpu

D_MODEL = 1024
DEPTH = 2
GRID_W = 64
NORM_EPS = 1e-6
N_MOD = 6

GROUP_WIDTH = D_MODEL // 4

SSD_WIDTH = GROUP_WIDTH
SSD_HEAD_DIM = 64
SSD_HEADS = SSD_WIDTH // SSD_HEAD_DIM
SSD_GROUPS = 2
SSD_STATE = 128
SSD_CONV = 5
SSD_CHUNK = 128
SSD_CONV_CH = SSD_WIDTH + 2 * SSD_GROUPS * SSD_STATE
SSD_IN = SSD_WIDTH + SSD_CONV_CH + 2 * SSD_HEADS

GLA_WIDTH = GROUP_WIDTH
GLA_HEADS = 4
GLA_DV = GLA_WIDTH // GLA_HEADS
GLA_DK = GLA_DV // 2
GLA_QK = GLA_HEADS * GLA_DK
GLA_GATE_RANK = 16
GLA_TAU = 16.0
GLA_CHUNK = 64
GLA_IN = 2 * GLA_QK + 2 * GLA_WIDTH + 2 * GLA_GATE_RANK

MLA_WIDTH = GROUP_WIDTH
MLA_HEADS = 4
MLA_V = MLA_WIDTH // MLA_HEADS
MLA_NOPE = 64
MLA_ROPE = 32
MLA_Q_RANK = 256
MLA_KV_RANK = 128
MLA_SCALE = (MLA_NOPE + MLA_ROPE) ** -0.5
ROPE_BASE = 10000.0
MLA_IN = MLA_Q_RANK + MLA_KV_RANK + MLA_ROPE

S5_WIDTH = GROUP_WIDTH
S5_GROUP = 16
S5_NGROUPS = S5_WIDTH // S5_GROUP
S5_STATE = 64
S5_MAX_RE = -1e-4
S5_IN = S5_WIDTH
S5_CHUNK = 16
S5_PAIRS = S5_NGROUPS // 2
S5_PACK_ROWS = (768, 256)

PEER_KEYS = 128
PEER_EXPERTS = PEER_KEYS * PEER_KEYS
PEER_HEADS = 8
PEER_TOPK = 16
PEER_DQ = 128

LANES = 128
ROW_TILE = 256
SCAN_STEP_ROWS = ROW_TILE

F32 = jnp.float32
BF16 = jnp.bfloat16

COL_XS, COL_BM, COL_CM, COL_Z = 0, 256, 512, 768
COL_GLA_V, COL_GLA_R, COL_CQ, COL_S5 = 1024, 1280, 1536, 1792
COL_GLA_Q, COL_GLA_K, COL_CKV, COL_DT, COL_GLR, COL_KR, COL_KRROT = 2048, 2176, 2304, 2432, 2560, 2688, 2816
P_COLS = 2944

_NN_DIMS = (((1,), (0,)), ((), ()))
_NT_DIMS = (((1,), (1,)), ((), ()))
_TN_DIMS = (((0,), (0,)), ((), ()))


def _mm(a, b, dims=_NN_DIMS):
    return lax.dot_general(a, b, dims, preferred_element_type=F32)


def _split_bf16(x):
    hi = x.astype(BF16)
    lo = (x - hi.astype(F32)).astype(BF16)
    return hi, lo


def _split3_bf16(x):
    p1 = x.astype(BF16)
    r1 = x - p1.astype(F32)
    p2 = r1.astype(BF16)
    p3 = (r1 - p2.astype(F32)).astype(BF16)
    return p1, p2, p3


def _dot3(a_hi, a_lo, b_hi, b_lo, dims):
    return _mm(a_hi, b_hi, dims) + _mm(a_hi, b_lo, dims) + _mm(a_lo, b_hi, dims)


def _gelu_erf(x):
    return 0.5 * x * (1.0 + lax.erf(x * (2.0 ** -0.5)))


def _silu(x):
    return x * jax.nn.sigmoid(x)


def _softplus(x):
    return jnp.maximum(x, 0.0) + jnp.log1p(jnp.exp(-jnp.abs(x)))


def _log_sigmoid(x):
    return jnp.minimum(x, 0.0) - jnp.log1p(jnp.exp(-jnp.abs(x)))


def _rms(x, g):
    return x * lax.rsqrt(jnp.mean(x * x, axis=-1, keepdims=True) + NORM_EPS) * g


def _modulated_norm(x, g, shift, scale):
    return _rms(x, g) * (1.0 + scale) + shift


def _causal_mask(n, reverse):
    ri = lax.broadcasted_iota(jnp.int32, (n, n), 0)
    ci = lax.broadcasted_iota(jnp.int32, (n, n), 1)
    return (ci >= ri) if reverse else (ci <= ri)


def _scan_chunk(s, n_lat, n_ctx, reverse):
    if reverse:
        return n_lat + n_ctx - 1 - s
    return jnp.where(s < n_ctx, n_lat + s, s - n_ctx)


def _inproj_kernel(h_ref, g_ref, shift_ref, scale_ref, w_ref, o_ref):
    xn = _modulated_norm(h_ref[0], g_ref[...], shift_ref[0], scale_ref[0])
    o_ref[0] = _mm(xn.astype(BF16), w_ref[...])


def _inproj(hcomb, norm_g, shift_tab, scale_tab, w_pad, n_lat_tiles):
    b, r, d = hcomb.shape
    mod_spec = pl.BlockSpec((1, 1, d), lambda i, t: (jnp.where(t < n_lat_tiles, i, b), 0, 0))
    return pl.pallas_call(
        _inproj_kernel,
        grid=(b, r // ROW_TILE),
        in_specs=[pl.BlockSpec((1, ROW_TILE, d), lambda i, t: (i, t, 0)),
                  pl.BlockSpec((1, d), lambda i, t: (0, 0)), mod_spec, mod_spec,
                  pl.BlockSpec((d, P_COLS), lambda i, t: (0, 0))],
        out_specs=pl.BlockSpec((1, ROW_TILE, P_COLS), lambda i, t: (i, t, 0)),
        out_shape=jax.ShapeDtypeStruct((b, r, P_COLS), F32),
        compiler_params=pltpu.CompilerParams(dimension_semantics=("arbitrary", "arbitrary"),
                                             vmem_limit_bytes=48 * 2 ** 20),
        name="inproj",
    )(hcomb, norm_g.reshape(1, d), shift_tab, scale_tab, w_pad)


def _pack_w_in(w):
    o_ssd, o_gla, o_mla, o_s5 = 0, SSD_IN, SSD_IN + GLA_IN, SSD_IN + GLA_IN + MLA_IN
    out = jnp.zeros((w.shape[0], P_COLS), F32)
    put = lambda out, col, src, width: out.at[:, col:col + width].set(w[:, src:src + width])
    out = put(out, COL_Z, o_ssd, SSD_WIDTH)
    out = put(out, COL_XS, o_ssd + SSD_WIDTH, SSD_CONV_CH)
    out = put(out, COL_DT, o_ssd + SSD_WIDTH + SSD_CONV_CH, 2 * SSD_HEADS)
    out = put(out, COL_GLA_Q, o_gla, GLA_QK)
    out = put(out, COL_GLA_K, o_gla + GLA_QK, GLA_QK)
    out = put(out, COL_GLA_V, o_gla + 2 * GLA_QK, GLA_WIDTH)
    out = put(out, COL_GLA_R, o_gla + 2 * GLA_QK + GLA_WIDTH, GLA_WIDTH)
    out = put(out, COL_GLR, o_gla + 2 * GLA_QK + 2 * GLA_WIDTH, 2 * GLA_GATE_RANK)
    out = put(out, COL_CQ, o_mla, MLA_Q_RANK)
    out = put(out, COL_CKV, o_mla + MLA_Q_RANK, MLA_KV_RANK)
    o_kr = o_mla + MLA_Q_RANK + MLA_KV_RANK
    half = MLA_ROPE // 2
    out = put(out, COL_KR + MLA_NOPE, o_kr, MLA_ROPE)
    out = out.at[:, COL_KRROT + MLA_NOPE:COL_KRROT + MLA_NOPE + half].set(-w[:, o_kr + half:o_kr + MLA_ROPE])
    out = out.at[:, COL_KRROT + MLA_NOPE + half:COL_KRROT + MLA_NOPE + MLA_ROPE].set(w[:, o_kr:o_kr + half])
    out = put(out, COL_S5, o_s5, S5_WIDTH)
    return out.astype(BF16)


def _ssd_prep_kernel(x_ref, prev_ref, next_ref, dt_ref, w_ref, b_ref, bias_ref, xbc_ref, dtc_ref, dtt_ref,
                     *, n_lat_tiles):
    t = pl.program_id(1)
    x = x_ref[0]
    halo = prev_ref.shape[1]
    prev = jnp.where(jnp.logical_and(t > 0, t < n_lat_tiles), prev_ref[0], 0.0)
    nxt = jnp.where(t < n_lat_tiles - 1, next_ref[0], 0.0)
    ext = jnp.concatenate([prev, x, nxt], axis=0)
    rows = ext.shape[0]
    left = SSD_CONV // 2
    acc = jnp.zeros_like(x) + b_ref[...]
    for k in range(SSD_CONV):
        shifted = ext if k == left else pltpu.roll(ext, (left - k) % rows, 0)
        acc = acc + w_ref[k:k + 1, :] * shifted[halo:halo + x.shape[0]]
    xbc_ref[0] = _silu(acc)
    dt = _softplus(dt_ref[0] + bias_ref[...])
    dtc_ref[0] = dt
    dtt_ref[0] = dt.T[:dtt_ref.shape[1]]


def _ssd_scan_kernel(xbc_f_ref, dtc_f_ref, dtt_f_ref, xbc_b_ref, dtc_b_ref, dtt_b_ref, ahr_ref, ahc_ref,
                     yf_ref, yb_ref, state_ref):
    @pl.when(pl.program_id(1) == 0)
    def _():
        state_ref[...] = jnp.zeros_like(state_ref)

    q = SSD_CHUNK
    n_sub = xbc_f_ref.shape[1] // q
    for direction, (xbc_ref, dtc_ref, dtt_ref, y_ref) in enumerate(
            ((xbc_f_ref, dtc_f_ref, dtt_f_ref, yf_ref), (xbc_b_ref, dtc_b_ref, dtt_b_ref, yb_ref))):
        for c in (range(n_sub) if direction == 0 else range(n_sub - 1, -1, -1)):
            rows = slice(c * q, (c + 1) * q)
            y_ref[0, rows] = _ssd_chunk(xbc_ref[0, rows], dtc_ref[0, rows], dtt_ref[0, :, rows], ahr_ref, ahc_ref,
                                        state_ref.at[direction], direction)


def _ssd_chunk(xbc, dtc, dtt, ahr_ref, ahc_ref, state_ref, direction):
    reverse = direction == 1
    q = SSD_CHUNK
    mask = _causal_mask(q, reverse)
    tri = jnp.where(mask, 1.0, 0.0).astype(BF16)
    xs, bm, cm = xbc[:, :SSD_WIDTH], xbc[:, SSD_WIDTH:SSD_WIDTH + 256], xbc[:, SSD_WIDTH + 256:]
    a_col = dtc * ahr_ref[...]
    a_row = dtt * ahc_ref[...]
    acum_col = sum(_mm(tri, part) for part in _split3_bf16(a_col))
    acum_row = sum(_mm(part, tri, _NT_DIMS) for part in _split3_bf16(a_row))
    end = 0 if reverse else q - 1
    bm_bf, cm_bf = bm.astype(BF16), cm.astype(BF16)
    ys = []
    cb = {}
    for h in range(SSD_HEADS):
        g = h // (SSD_HEADS // SSD_GROUPS)
        gs = slice(g * SSD_STATE, (g + 1) * SSD_STATE)
        if g not in cb:
            cb[g] = _mm(cm_bf[:, gs], bm_bf[:, gs], _NT_DIMS)
        ch = direction * SSD_HEADS + h
        ac = acum_col[:, ch:ch + 1]
        ar = acum_row[ch:ch + 1, :]
        decay = jnp.exp(jnp.where(mask, ac - ar, -jnp.inf))
        xd = xs[:, h * SSD_HEAD_DIM:(h + 1) * SSD_HEAD_DIM] * dtc[:, ch:ch + 1]
        y_diag = _mm((cb[g] * decay).astype(BF16), xd.astype(BF16))
        a_end = ac[end:end + 1, :]
        st_local = _mm((xd * jnp.exp(a_end - ac)).astype(BF16), bm_bf[:, gs], _TN_DIMS)
        hs = state_ref[h]
        y_off = jnp.exp(ac) * _mm(cm_bf[:, gs], hs.astype(BF16), _NT_DIMS)
        state_ref[h] = jnp.exp(a_end) * hs + st_local
        ys.append(y_diag + y_off)
    return jnp.concatenate(ys, axis=1)


def _ssd_mixer(p, conv_w, conv_b, a_log, dt_bias, n_lat_tiles):
    b, r, _ = p.shape
    nt = r // ROW_TILE
    halo = 8
    hb = ROW_TILE // halo
    w8 = jnp.zeros((8, SSD_CONV_CH), F32).at[:SSD_CONV].set(conv_w)
    bias = jnp.zeros((1, LANES), F32).at[0, :2 * SSD_HEADS].set(dt_bias.reshape(-1))
    xbc, dtc, dtt = pl.pallas_call(
        functools.partial(_ssd_prep_kernel, n_lat_tiles=n_lat_tiles),
        grid=(b, nt),
        in_specs=[pl.BlockSpec((1, ROW_TILE, SSD_CONV_CH), lambda i, t: (i, t, 0)),
                  pl.BlockSpec((1, halo, SSD_CONV_CH), lambda i, t: (i, jnp.maximum(t * hb - 1, 0), 0)),
                  pl.BlockSpec((1, halo, SSD_CONV_CH), lambda i, t: (i, jnp.minimum((t + 1) * hb, nt * hb - 1), 0)),
                  pl.BlockSpec((1, ROW_TILE, LANES), lambda i, t: (i, t, COL_DT // LANES)),
                  pl.BlockSpec((8, SSD_CONV_CH), lambda i, t: (0, 0)),
                  pl.BlockSpec((1, SSD_CONV_CH), lambda i, t: (0, 0)),
                  pl.BlockSpec((1, LANES), lambda i, t: (0, 0))],
        out_specs=[pl.BlockSpec((1, ROW_TILE, SSD_CONV_CH), lambda i, t: (i, t, 0)),
                   pl.BlockSpec((1, ROW_TILE, LANES), lambda i, t: (i, t, 0)),
                   pl.BlockSpec((1, 8, ROW_TILE), lambda i, t: (i, 0, t))],
        out_shape=[jax.ShapeDtypeStruct((b, r, SSD_CONV_CH), F32),
                   jax.ShapeDtypeStruct((b, r, LANES), F32),
                   jax.ShapeDtypeStruct((b, 8, r), F32)],
        compiler_params=pltpu.CompilerParams(dimension_semantics=("arbitrary", "arbitrary")),
        name="ssd_prep",
    )(p, p, p, p, w8, conv_b.reshape(1, -1), bias)

    a_head = -jnp.exp(a_log.astype(F32)).reshape(-1)
    ahr = jnp.zeros((1, LANES), F32).at[0, :2 * SSD_HEADS].set(a_head)
    ahc = a_head.reshape(2 * SSD_HEADS, 1)
    blk = SCAN_STEP_ROWS
    n_lat = n_lat_tiles * ROW_TILE // blk
    n_ctx = r // blk - n_lat
    in_specs, y_specs = [], []
    for reverse in (False, True):
        cidx = functools.partial(_scan_chunk, n_lat=n_lat, n_ctx=n_ctx, reverse=reverse)
        in_specs += [pl.BlockSpec((1, blk, SSD_CONV_CH), lambda i, s, cidx=cidx: (i, cidx(s), 0)),
                     pl.BlockSpec((1, blk, LANES), lambda i, s, cidx=cidx: (i, cidx(s), 0)),
                     pl.BlockSpec((1, 8, blk), lambda i, s, cidx=cidx: (i, 0, cidx(s)))]
        y_specs.append(pl.BlockSpec((1, blk, SSD_WIDTH), lambda i, s, cidx=cidx: (i, cidx(s), 0)))
    in_specs += [pl.BlockSpec((1, LANES), lambda i, s: (0, 0)), pl.BlockSpec((2 * SSD_HEADS, 1), lambda i, s: (0, 0))]
    y_f, y_b = pl.pallas_call(
        _ssd_scan_kernel,
        grid=(b, n_lat + n_ctx),
        in_specs=in_specs,
        out_specs=y_specs,
        out_shape=[jax.ShapeDtypeStruct((b, r, SSD_WIDTH), F32)] * 2,
        scratch_shapes=[pltpu.VMEM((2, SSD_HEADS, SSD_HEAD_DIM, SSD_STATE), F32)],
        compiler_params=pltpu.CompilerParams(dimension_semantics=("arbitrary", "arbitrary")),
        name="ssd_scan",
    )(xbc, dtc, dtt, xbc, dtc, dtt, ahr, ahc)
    return y_f, y_b, xbc


def _gla_scan_kernel(qf_ref, kf_ref, vf_ref, glrf_ref, qb_ref, kb_ref, vb_ref, glrb_ref, wg_ref, bias_ref,
                     of_ref, ob_ref, st_ref):
    @pl.when(pl.program_id(1) == 0)
    def _():
        st_ref[...] = jnp.zeros_like(st_ref)

    n = GLA_CHUNK
    n_sub = qf_ref.shape[1] // n
    for direction, (q_ref, k_ref, v_ref, glr_ref, o_ref) in enumerate(
            ((qf_ref, kf_ref, vf_ref, glrf_ref, of_ref), (qb_ref, kb_ref, vb_ref, glrb_ref, ob_ref))):
        for c in (range(n_sub) if direction == 0 else range(n_sub - 1, -1, -1)):
            rows = slice(c * n, (c + 1) * n)
            o_ref[0, rows] = _gla_chunk(q_ref[0, rows], k_ref[0, rows], v_ref[0, rows], glr_ref[0, rows],
                                        wg_ref.at[direction], bias_ref.at[direction], st_ref.at[direction],
                                        direction == 1)


def _gla_chunk(q, k, v, glr, wg_ref, bias_ref, st_ref, reverse):
    n = GLA_CHUNK
    mask = _causal_mask(n, reverse)
    tri = jnp.where(mask, 1.0, 0.0).astype(BF16)
    g_hi, g_lo = _split_bf16(glr)
    logits = _dot3(g_hi, g_lo, wg_ref[0], wg_ref[1], _NN_DIMS) + bias_ref[...]
    logg = _log_sigmoid(logits) * (1.0 / GLA_TAU)
    bcum = sum(_mm(tri, part) for part in _split3_bf16(logg))
    end = 0 if reverse else n - 1
    b_end = bcum[end:end + 1, :]
    qe = q * jnp.exp(bcum) * (GLA_DK ** -0.5)
    ke = (k * jnp.exp(-bcum)).astype(BF16)
    kd = k * jnp.exp(b_end - bcum)
    decay_end = jnp.exp(b_end)
    lane_head = lax.broadcasted_iota(jnp.int32, (1, GLA_QK), 1) >> (GLA_DK.bit_length() - 1)
    outs = []
    for h in range(GLA_HEADS):
        hm = lane_head == h
        qh = jnp.where(hm, qe, 0.0).astype(BF16)
        att = jnp.where(mask, _mm(qh, ke, _NT_DIMS), 0.0)
        vh = v[:, h * GLA_DV:(h + 1) * GLA_DV].astype(BF16)
        st = st_ref[h]
        o_h = _mm(att.astype(BF16), vh) + _mm(qh, st.astype(BF16), _NT_DIMS)
        local = _mm(vh, jnp.where(hm, kd, 0.0).astype(BF16), _TN_DIMS)
        st_ref[h] = st * decay_end + local
        outs.append(o_h)
    return jnp.concatenate(outs, axis=1)


def _gla_mixer(p, gate_w, gate_b, n_lat_tiles):
    b, r, _ = p.shape
    rows = SCAN_STEP_ROWS
    n_lat = n_lat_tiles * ROW_TILE // rows
    n_ctx = r // rows - n_lat
    in_specs, o_specs, wgs = [], [], []
    for direction in (0, 1):
        cidx = functools.partial(_scan_chunk, n_lat=n_lat, n_ctx=n_ctx, reverse=direction == 1)
        wg = jnp.zeros((LANES, GLA_QK), F32).at[direction * GLA_GATE_RANK:(direction + 1) * GLA_GATE_RANK].set(
            gate_w[direction])
        wgs.append(jnp.stack(_split_bf16(wg)))
        blk = lambda width, col, cidx=cidx: pl.BlockSpec((1, rows, width), lambda i, s: (i, cidx(s), col // width))
        in_specs += [blk(GLA_QK, COL_GLA_Q), blk(GLA_QK, COL_GLA_K), blk(GLA_WIDTH, COL_GLA_V), blk(LANES, COL_GLR)]
        o_specs.append(pl.BlockSpec((1, rows, GLA_WIDTH), lambda i, s, cidx=cidx: (i, cidx(s), 0)))
    in_specs += [pl.BlockSpec((2, 2, LANES, GLA_QK), lambda i, s: (0, 0, 0, 0)),
                 pl.BlockSpec((2, 1, GLA_QK), lambda i, s: (0, 0, 0))]
    return pl.pallas_call(
        _gla_scan_kernel,
        grid=(b, n_lat + n_ctx),
        in_specs=in_specs,
        out_specs=o_specs,
        out_shape=[jax.ShapeDtypeStruct((b, r, GLA_WIDTH), F32)] * 2,
        scratch_shapes=[pltpu.VMEM((2, GLA_HEADS, GLA_DV, GLA_QK), F32)],
        compiler_params=pltpu.CompilerParams(dimension_semantics=("arbitrary", "arbitrary")),
        name="gla_scan",
    )(p, p, p, p, p, p, p, p, jnp.stack(wgs), gate_b.reshape(2, 1, GLA_QK))


MLA_Q_TILE = 1024
MLA_K_TILES = (768, 256)


def _mla_prep_kernel(cq_ref, ckv_ref, kr_ref, krrot_ref, onec_ref, sinr_ref, gq_ref, gkv_ref,
                     wq_ref, wqr_ref, wk_ref, wv_ref, q_ref, k_ref, v_ref):
    qn = _rms(cq_ref[0], gq_ref[...]).astype(BF16)
    kvn = _rms(ckv_ref[0], gkv_ref[...]).astype(BF16)
    onec, sinr = onec_ref[...], sinr_ref[...]
    k_rope = kr_ref[0] * onec + krrot_ref[0] * sinr
    ones_lane = jnp.where(lax.broadcasted_iota(jnp.int32, (1, LANES), 1) == MLA_V, 1.0, 0.0)
    for h in range(MLA_HEADS):
        qh = _mm(qn, wq_ref[h]) * onec + _mm(qn, wqr_ref[h]) * sinr
        q_ref[0, h] = (qh * MLA_SCALE).astype(BF16)
        k_ref[0, h] = (_mm(kvn, wk_ref[h]) + k_rope).astype(BF16)
        v_ref[0, h] = (_mm(kvn, wv_ref[h]) + ones_lane).astype(BF16)


def _mla_attn_kernel(q_ref, k_ref, v_ref, o_ref, m_ref, acc_ref):
    j = pl.program_id(2)

    @pl.when(j == 0)
    def _():
        m_ref[...] = jnp.full_like(m_ref, -jnp.inf)
        acc_ref[...] = jnp.zeros_like(acc_ref)

    reps = k_ref.shape[2] // LANES
    for h in range(MLA_HEADS):
        s = _mm(q_ref[0, h], k_ref[0, h], _NT_DIMS)
        m_prev = m_ref[h]
        m_new = jnp.maximum(m_prev, jnp.max(s, axis=1, keepdims=True))
        p = jnp.exp((s - jnp.concatenate([m_new] * reps, axis=1)).astype(BF16))
        acc_ref[h] = jnp.exp(m_prev - m_new) * acc_ref[h] + _mm(p, v_ref[0, h])
        m_ref[h] = m_new

    @pl.when(j == pl.num_programs(2) - 1)
    def _():
        outs = []
        for h in range(MLA_HEADS):
            acc = acc_ref[h]
            outs.append(acc[:, :MLA_V] / acc[:, MLA_V:MLA_V + 1])
        o_ref[0] = jnp.concatenate(outs, axis=1)


def _rope_tables(n_lat, n_rows):
    rows = n_lat // GRID_W
    row = jnp.repeat(jnp.arange(rows, dtype=F32), GRID_W)
    col = jnp.tile(jnp.arange(GRID_W, dtype=F32), rows)
    half = MLA_ROPE // 2
    inv = ROPE_BASE ** (-jnp.arange(0, half, 2, dtype=F32) / half)
    ang = jnp.concatenate([row[:, None] * inv, col[:, None] * inv], axis=-1)
    cos = jnp.concatenate([jnp.cos(ang), jnp.ones((n_rows - n_lat, half), F32)], axis=0)
    sin = jnp.concatenate([jnp.sin(ang), jnp.zeros((n_rows - n_lat, half), F32)], axis=0)
    pad = jnp.zeros((n_rows, LANES - MLA_NOPE - MLA_ROPE), F32)
    onec = jnp.concatenate([jnp.ones((n_rows, MLA_NOPE), F32), cos, cos, pad], axis=1)
    sinr = jnp.concatenate([jnp.zeros((n_rows, MLA_NOPE), F32), sin, sin, pad], axis=1)
    return onec, sinr


def _mla_weights(w_uq, w_ukv):
    dqk = MLA_NOPE + MLA_ROPE
    half = MLA_ROPE // 2
    wq = w_uq.reshape(MLA_Q_RANK, MLA_HEADS, dqk).transpose(1, 0, 2)
    rot = jnp.concatenate([jnp.zeros_like(wq[..., :MLA_NOPE]), -wq[..., MLA_NOPE + half:], wq[..., MLA_NOPE:MLA_NOPE + half]],
                          axis=-1)
    padq = lambda w: jnp.pad(w, ((0, 0), (0, 0), (0, LANES - dqk))).astype(BF16)
    wkv = w_ukv.reshape(MLA_KV_RANK, MLA_HEADS, MLA_NOPE + MLA_V).transpose(1, 0, 2)
    padk = lambda w: jnp.pad(w, ((0, 0), (0, 0), (0, LANES - w.shape[-1]))).astype(BF16)
    return padq(wq), padq(rot), padk(wkv[..., :MLA_NOPE]), padk(wkv[..., MLA_NOPE:])


def _mla_attention(q, k, v, q_tile, q_off, n_q, kt, k_off, n_k):
    b = q.shape[0]
    return pl.pallas_call(
        _mla_attn_kernel,
        grid=(b, n_q, n_k),
        in_specs=[pl.BlockSpec((1, MLA_HEADS, q_tile, LANES), lambda i, a, j: (i, 0, q_off + a, 0)),
                  pl.BlockSpec((1, MLA_HEADS, kt, LANES), lambda i, a, j: (i, 0, k_off + j, 0)),
                  pl.BlockSpec((1, MLA_HEADS, kt, LANES), lambda i, a, j: (i, 0, k_off + j, 0))],
        out_specs=pl.BlockSpec((1, q_tile, MLA_WIDTH), lambda i, a, j: (i, a, 0)),
        out_shape=jax.ShapeDtypeStruct((b, n_q * q_tile, MLA_WIDTH), F32),
        scratch_shapes=[pltpu.VMEM((MLA_HEADS, q_tile, LANES), F32), pltpu.VMEM((MLA_HEADS, q_tile, LANES), F32)],
        compiler_params=pltpu.CompilerParams(dimension_semantics=("arbitrary", "arbitrary", "arbitrary")),
        name="mla_attn",
    )(q, k, v)


def _mla_mixer(p, q_norm_g, w_uq, kv_norm_g, w_ukv, n_lat_tiles, ctx_out):
    b, r, _ = p.shape
    nt = r // ROW_TILE
    n_lat = n_lat_tiles * ROW_TILE
    onec, sinr = _rope_tables(n_lat, r)
    wq, wqr, wk, wv = _mla_weights(w_uq, w_ukv)
    blk = lambda width, col: pl.BlockSpec((1, ROW_TILE, width), lambda i, t: (i, t, col // width))
    tab = pl.BlockSpec((ROW_TILE, LANES), lambda i, t: (t, 0))
    full = lambda *shape: pl.BlockSpec(shape, lambda i, t: (0,) * len(shape))
    head_out = pl.BlockSpec((1, MLA_HEADS, ROW_TILE, LANES), lambda i, t: (i, 0, t, 0))
    q, k, v = pl.pallas_call(
        _mla_prep_kernel,
        grid=(b, nt),
        in_specs=[blk(MLA_Q_RANK, COL_CQ), blk(LANES, COL_CKV), blk(LANES, COL_KR), blk(LANES, COL_KRROT), tab, tab,
                  full(1, MLA_Q_RANK), full(1, MLA_KV_RANK),
                  full(MLA_HEADS, MLA_Q_RANK, LANES), full(MLA_HEADS, MLA_Q_RANK, LANES),
                  full(MLA_HEADS, MLA_KV_RANK, LANES), full(MLA_HEADS, MLA_KV_RANK, LANES)],
        out_specs=[head_out] * 3,
        out_shape=[jax.ShapeDtypeStruct((b, MLA_HEADS, r, LANES), BF16)] * 3,
        compiler_params=pltpu.CompilerParams(dimension_semantics=("arbitrary", "arbitrary")),
        name="mla_prep",
    )(p, p, p, p, onec, sinr, q_norm_g.reshape(1, -1), kv_norm_g.reshape(1, -1), wq, wqr, wk, wv)
    q_tile = min(MLA_Q_TILE, n_lat)
    k_tile = next(t for t in MLA_K_TILES if r % t == 0)
    y_lat = _mla_attention(q, k, v, q_tile, 0, n_lat // q_tile, k_tile, 0, r // k_tile)
    y_ctx = None
    if ctx_out:
        n_ctx = r - n_lat
        y_ctx = _mla_attention(q, k, v, n_ctx, n_lat // n_ctx, 1, n_ctx, n_lat // n_ctx, 1)
    return y_lat, y_ctx


def _s5_matrices(a_re, a_im, log_dt, b_re, b_im, c_re, c_im):
    q, ng, ns, nc = S5_CHUNK, S5_NGROUPS, S5_STATE, S5_GROUP
    lam = jnp.minimum(a_re.astype(F32), S5_MAX_RE) + 1j * a_im.astype(F32)
    step = jnp.exp(log_dt.astype(F32))[..., None]
    abar = jnp.exp(lam * step)
    bmat = b_re.astype(F32) + 1j * b_im.astype(F32)
    bbar = ((abar - 1.0) / lam)[..., None] * bmat
    cmat = c_re.astype(F32) + 1j * c_im.astype(F32)
    pw = jnp.exp((lam * step)[..., None] * jnp.arange(q + 1, dtype=F32))
    kern = jnp.einsum('dgcn,dgnl,dgnk->dglck', cmat, pw[..., :q], bbar).real
    ii = jnp.arange(q)
    lag_f = ii[None, :] - ii[:, None]
    gather = lambda kd, lag: jnp.where((lag >= 0)[None, :, :, None, None], kd[:, jnp.clip(lag, 0, q - 1)], 0.0)
    t_f = gather(kern[0], lag_f).transpose(0, 1, 4, 2, 3)
    t_b = gather(kern[1], -lag_f).transpose(0, 1, 4, 2, 3)
    t_sum = (t_f + t_b).reshape(ng, q * nc, q * nc)
    pw_f = pw[0][..., q - 1 - ii]
    pw_b = pw[1][..., ii]
    wst = lambda pwd, bb: jnp.einsum('gnj,gnc->gjcn', pwd, bb).reshape(ng, q * nc, ns)
    wst_f, wst_b = wst(pw_f, bbar[0]), wst(pw_b, bbar[1])
    wout = lambda pwd, cm: jnp.einsum('gcn,gni->gnic', cm, pwd).reshape(ng, ns, q * nc)
    wo_f, wo_b = wout(pw[0][..., ii + 1], cmat[0]), wout(pw[1][..., q - ii], cmat[1])
    aq = pw[..., q]

    def pair_cols(x):
        x = x.reshape(S5_PAIRS, 2, x.shape[1], x.shape[2])
        z = jnp.zeros_like(x[:, 0])
        return jnp.concatenate([jnp.concatenate([x[:, 0], z], axis=2), jnp.concatenate([z, x[:, 1]], axis=2)], axis=1)

    w_local = jnp.concatenate([pair_cols(wst_f.real), pair_cols(wst_f.imag),
                               pair_cols(wst_b.real), pair_cols(wst_b.imag)], axis=2)
    w_out = jnp.concatenate([pair_cols(t_sum), pair_cols(wo_f.real), pair_cols(-wo_f.imag),
                             pair_cols(wo_b.real), pair_cols(-wo_b.imag)], axis=1)
    aq_pair = aq.reshape(2, S5_PAIRS, 2 * ns)
    aq_tab = jnp.concatenate([aq_pair[0].real, aq_pair[0].imag, aq_pair[1].real, aq_pair[1].imag], axis=1)
    return w_local.astype(BF16), w_out.astype(BF16), aq_tab.reshape(S5_PAIRS, 1, 8 * ns).astype(F32)


def _s5_perm():
    cols = S5_CHUNK * S5_WIDTH
    c = jnp.arange(cols, dtype=jnp.int32)
    cc, j = c % S5_GROUP, (c // S5_GROUP) % S5_CHUNK
    g = c // (S5_GROUP * S5_CHUNK)
    per_half = LANES // S5_GROUP
    src = (g // per_half) * (S5_CHUNK * LANES) + j * LANES + (g % per_half) * S5_GROUP + cc
    return jnp.where(c[:, None] == src[None, :], 1.0, 0.0).astype(BF16)


def _s5_pack_kernel(u_ref, o_ref):
    n = o_ref.shape[1]
    for j in range(S5_CHUNK):
        o_ref[0, :, j * LANES:(j + 1) * LANES] = u_ref[0, pl.ds(j, n, stride=S5_CHUNK), :].astype(BF16)


def _s5_unpack_kernel(y_ref, o_ref):
    n = y_ref.shape[1]
    for i in range(S5_CHUNK):
        o_ref[0, pl.ds(i, n, stride=S5_CHUNK), :] = y_ref[0, :, i * LANES:(i + 1) * LANES]


def _s5_local_kernel(u_ref, perm_ref, w_ref, up_ref, s_ref):
    up = _mm(u_ref[0], perm_ref[...]).astype(BF16)
    up_ref[0] = up
    s_ref[0] = _mm(up, w_ref[0])


def _s5_scan_kernel(s3_ref, aq_ref, hs3_ref, *, n_lat, n_ctx, nb):
    s_ref, hs_ref = s3_ref.at[0], hs3_ref.at[0]
    w = 2 * S5_STATE
    aq = aq_ref[0]
    a = [aq[:, i * w:(i + 1) * w] for i in range(4)]
    zero = jnp.zeros((nb, w), F32)
    slab = 8
    cps = slab // nb

    def run_slab(s_re, s_im, a_re, a_im, h_re, h_im, order):
        ent_re, ent_im = [None] * cps, [None] * cps
        for c in order:
            ent_re[c], ent_im[c] = h_re, h_im
            rows = slice(c * nb, (c + 1) * nb)
            h_re, h_im = a_re * h_re - a_im * h_im + s_re[rows], a_re * h_im + a_im * h_re + s_im[rows]
        return jnp.concatenate(ent_re, axis=0), jnp.concatenate(ent_im, axis=0), h_re, h_im

    def body(kk, carry):
        f_re, f_im, b_re, b_im = carry
        rf = pl.multiple_of(_scan_chunk(kk, n_lat // cps, n_ctx // cps, False) * slab, slab)
        rb = pl.multiple_of(_scan_chunk(kk, n_lat // cps, n_ctx // cps, True) * slab, slab)
        e_re, e_im, f_re, f_im = run_slab(s_ref[pl.ds(rf, slab), 0:w], s_ref[pl.ds(rf, slab), w:2 * w],
                                          a[0], a[1], f_re, f_im, range(cps))
        hs_ref[pl.ds(rf, slab), 0:w] = e_re
        hs_ref[pl.ds(rf, slab), w:2 * w] = e_im
        e_re, e_im, b_re, b_im = run_slab(s_ref[pl.ds(rb, slab), 2 * w:3 * w], s_ref[pl.ds(rb, slab), 3 * w:4 * w],
                                          a[2], a[3], b_re, b_im, range(cps - 1, -1, -1))
        hs_ref[pl.ds(rb, slab), 2 * w:3 * w] = e_re
        hs_ref[pl.ds(rb, slab), 3 * w:4 * w] = e_im
        return f_re, f_im, b_re, b_im

    lax.fori_loop(0, (n_lat + n_ctx) // cps, body, (zero, zero, zero, zero))


def _s5_out_kernel(up_ref, hs_ref, w_ref, perm_ref, y_ref):
    @pl.when(pl.program_id(1) == 0)
    def _():
        y_ref[...] = jnp.zeros_like(y_ref)

    kw = up_ref.shape[2]
    y_pair = _mm(up_ref[0], w_ref[0, :kw]) + _mm(hs_ref[0].astype(BF16), w_ref[0, kw:])
    y_hi, y_lo = _split_bf16(y_pair)
    y_ref[0] += _mm(y_hi, perm_ref[...], _NT_DIMS) + _mm(y_lo, perm_ref[...], _NT_DIMS)


def _s5_mixer(p, a_re, a_im, log_dt, b_re, b_im, c_re, c_im, n_lat_tiles):
    b, r, _ = p.shape
    q = S5_CHUNK
    n_chunks = r // q
    cols = q * S5_WIDTH
    kw = 2 * q * S5_GROUP
    w_local, w_out, aq_tab = _s5_matrices(a_re, a_im, log_dt, b_re, b_im, c_re, c_im)
    perm = _s5_perm()
    cp2 = pltpu.CompilerParams(dimension_semantics=("arbitrary", "arbitrary"), vmem_limit_bytes=48 * 2 ** 20)
    pack_rows = next(t for t in S5_PACK_ROWS if r % t == 0)
    cpt = pack_rows // q
    halves = S5_WIDTH // LANES
    cp3 = pltpu.CompilerParams(dimension_semantics=("arbitrary", "arbitrary", "arbitrary"))
    chunk_rows = pl.BlockSpec((1, cpt, q * LANES), lambda i, t, hf: (i, t, hf))
    u_big = pl.pallas_call(
        _s5_pack_kernel,
        grid=(b, r // pack_rows, halves),
        in_specs=[pl.BlockSpec((1, pack_rows, LANES), lambda i, t, hf: (i, t, COL_S5 // LANES + hf))],
        out_specs=chunk_rows,
        out_shape=jax.ShapeDtypeStruct((b, n_chunks, cols), BF16),
        compiler_params=cp3, name="s5_pack",
    )(p)
    all_chunks = pl.BlockSpec((1, n_chunks, cols), lambda i, g: (i, 0, 0))
    col_tile = lambda width: pl.BlockSpec((1, n_chunks, width), lambda i, g: (i, 0, g))
    perm_cols = pl.BlockSpec((cols, kw), lambda i, g: (0, g))
    u_pairs, s_loc = pl.pallas_call(
        _s5_local_kernel,
        grid=(b, S5_PAIRS),
        in_specs=[all_chunks, perm_cols, pl.BlockSpec((1, kw, kw), lambda i, g: (g, 0, 0))],
        out_specs=[col_tile(kw), col_tile(kw)],
        out_shape=[jax.ShapeDtypeStruct((b, n_chunks, cols), BF16), jax.ShapeDtypeStruct((b, n_chunks, cols), F32)],
        compiler_params=cp2, name="s5_local",
    )(u_big, perm, w_local)
    n_lat = n_lat_tiles * ROW_TILE // q
    hs = pl.pallas_call(
        functools.partial(_s5_scan_kernel, n_lat=n_lat, n_ctx=n_chunks - n_lat, nb=1),
        grid=(b, S5_PAIRS),
        in_specs=[col_tile(kw), pl.BlockSpec((1, 1, kw), lambda i, g: (g, 0, 0))],
        out_specs=col_tile(kw),
        out_shape=jax.ShapeDtypeStruct((b, n_chunks, cols), F32),
        compiler_params=cp2, name="s5_scan",
    )(s_loc, aq_tab)
    y_big = pl.pallas_call(
        _s5_out_kernel,
        grid=(b, S5_PAIRS),
        in_specs=[col_tile(kw), col_tile(kw), pl.BlockSpec((1, 2 * kw, kw), lambda i, g: (g, 0, 0)), perm_cols],
        out_specs=all_chunks,
        out_shape=jax.ShapeDtypeStruct((b, n_chunks, cols), F32),
        compiler_params=cp2, name="s5_out",
    )(u_pairs, hs, w_out, perm)
    return pl.pallas_call(
        _s5_unpack_kernel,
        grid=(b, r // pack_rows, halves),
        in_specs=[chunk_rows],
        out_specs=pl.BlockSpec((1, pack_rows, LANES), lambda i, t, hf: (i, t, hf)),
        out_shape=jax.ShapeDtypeStruct((b, r, S5_WIDTH), F32),
        compiler_params=cp3, name="s5_unpack",
    )(y_big)


def _post_kernel(h_ref, xs_ref, z_ref, r_ref, u_ref, ssd_ref, ssd_b_ref, gla_ref, gla_b_ref, mla_ref, s5_ref,
                 ssd_d_ref, ssd_g_ref, gla_g_ref, s5_d_ref, glu_w_ref, glu_b_ref, w_out_ref, mod_ref, o_ref):
    y = ssd_ref[0] + ssd_b_ref[0] + ssd_d_ref[...] * xs_ref[0]
    ssd = _rms(y * _silu(z_ref[0]), ssd_g_ref[...])
    o = gla_ref[0] + gla_b_ref[0]
    lane_head = lax.broadcasted_iota(jnp.int32, (1, GLA_WIDTH), 1) >> (GLA_DV.bit_length() - 1)
    ms = jnp.zeros_like(o)
    for h in range(GLA_HEADS):
        oh = o[:, h * GLA_DV:(h + 1) * GLA_DV]
        ms = jnp.where(lane_head == h, jnp.mean(oh * oh, axis=-1, keepdims=True), ms)
    gla = o * lax.rsqrt(ms + NORM_EPS) * gla_g_ref[...] * _silu(r_ref[0])
    y5 = _gelu_erf(s5_ref[0] + s5_d_ref[...] * u_ref[0])
    s5 = y5 * jax.nn.sigmoid(_mm(y5.astype(BF16), glu_w_ref[...]) + glu_b_ref[...])
    mix_in = jnp.concatenate([ssd, gla, mla_ref[0], s5], axis=1).astype(BF16)
    o_ref[0] = h_ref[0] + mod_ref[0] * _mm(mix_in, w_out_ref[...])


def _post(h, p, ssd_xbc, ssd_y, ssd_yb, gla_o, gla_ob, mla_y, s5_y, ssd_d, ssd_norm_g, gla_norm_g, s5_d, glu_w, glu_b, w_out, mod,
          row_off, mla_off):
    b, rows, d = h.shape
    w = GROUP_WIDTH
    pblk = lambda col: pl.BlockSpec((1, ROW_TILE, w), lambda i, t: (i, row_off + t, col // w))
    yblk = pl.BlockSpec((1, ROW_TILE, w), lambda i, t: (i, row_off + t, 0))
    full = lambda *shape: pl.BlockSpec(shape, lambda i, t: (0,) * len(shape))
    vec = lambda x: x.reshape(1, -1).astype(F32)
    n_mod = mod.shape[0]
    return pl.pallas_call(
        _post_kernel,
        grid=(b, rows // ROW_TILE),
        in_specs=[pl.BlockSpec((1, ROW_TILE, d), lambda i, t: (i, t, 0)),
                  yblk, pblk(COL_Z), pblk(COL_GLA_R), pblk(COL_S5), yblk, yblk, yblk, yblk,
                  pl.BlockSpec((1, ROW_TILE, w), lambda i, t: (i, mla_off + t, 0)), yblk,
                  full(1, w), full(1, w), full(1, w), full(1, w), full(w, w), full(1, w), full(d, d),
                  pl.BlockSpec((1, 1, d), lambda i, t: (jnp.minimum(i, n_mod - 1), 0, 0))],
        out_specs=pl.BlockSpec((1, ROW_TILE, d), lambda i, t: (i, t, 0)),
        out_shape=jax.ShapeDtypeStruct((b, rows, d), F32),
        compiler_params=pltpu.CompilerParams(dimension_semantics=("arbitrary", "arbitrary")),
        name="mix_post",
    )(h, ssd_xbc, p, p, p, ssd_y, ssd_yb, gla_o, gla_ob, mla_y, s5_y,
      vec(jnp.repeat(ssd_d, SSD_HEAD_DIM)), vec(ssd_norm_g), vec(jnp.tile(gla_norm_g, GLA_HEADS)), vec(s5_d),
      glu_w.astype(BF16), vec(glu_b), w_out.astype(BF16), mod)


PEER_ROUTE_TOKENS = 256
PEER_GATE_TOKENS = 256
PEER_GATE_UNROLL = 16
PEER_GATE_SUBLANES = 8
PEER_DENSE_TOKENS = 512
PEER_DENSE_EXPERTS = 2 * PEER_GATE_SUBLANES * PEER_KEYS
PEER_SLOTS = PEER_HEADS * PEER_TOPK


def _topk_rows(s, k):
    n_rows = s.shape[0]
    rows = lax.broadcasted_iota(jnp.int32, s.shape, 0)
    vals, idxs = [], []
    for _ in range(k):
        m = jnp.max(s, axis=0, keepdims=True)
        idx = jnp.min(jnp.where(s == m, rows, n_rows), axis=0, keepdims=True)
        vals.append(m)
        idxs.append(idx)
        s = jnp.where(rows == idx, -jnp.inf, s)
    return jnp.concatenate(vals, axis=0), jnp.concatenate(idxs, axis=0)


def _select_rows(pos, table):
    out = jnp.zeros(pos.shape, table.dtype)
    for r in range(table.shape[0]):
        out = jnp.where(pos == r, table[r:r + 1, :], out)
    return out


def _peer_route_kernel(h_ref, g_ref, shift_ref, scale_ref, wq_hi_ref, wq_lo_ref, k_hi_ref, k_lo_ref,
                       xn_ref, i1_ref, i2_ref, gate_ref, q_scr, slot_scr):
    xn = _modulated_norm(h_ref[...], g_ref[...], shift_ref[0], scale_ref[0])
    xn_ref[...] = xn.astype(BF16)
    x_hi, x_lo = _split_bf16(xn)
    q_scr[...] = _dot3(wq_hi_ref[...], wq_lo_ref[...], x_hi, x_lo, _NT_DIMS)
    half = PEER_DQ // 2

    def head_body(h, carry):
        base = pl.multiple_of(h * PEER_DQ, PEER_DQ)
        tops = []
        for j in range(2):
            qq = q_scr[pl.ds(base + j * half, half), :]
            q_hi, q_lo = _split_bf16(qq)
            s = _dot3(k_hi_ref[j, h], k_lo_ref[j, h], q_hi, q_lo, _NN_DIMS)
            tops.append(_topk_rows(s, PEER_TOPK))
        (v1, i1), (v2, i2) = tops
        pieces = [v1[a:a + 1, :] + v2[:PEER_TOPK // (a + 1), :] for a in range(PEER_TOPK)]
        n_cand = sum(PEER_TOPK // (a + 1) for a in range(PEER_TOPK))
        pad = -n_cand % 8
        cand = jnp.concatenate(pieces + [jnp.full((pad, v1.shape[1]), -jnp.inf, F32)], axis=0)
        best, pos = _topk_rows(cand, PEER_TOPK)
        e = jnp.exp(best - best[0:1, :])
        gates = e / jnp.sum(e, axis=0, keepdims=True)
        a_idx = jnp.zeros_like(pos)
        start = jnp.zeros_like(pos)
        first = 0
        for a in range(1, PEER_TOPK):
            width = PEER_TOPK // a
            first += width
            reached = pos >= first
            a_idx = a_idx + jnp.where(reached, 1, 0)
            start = start + jnp.where(reached, width, 0)
        row0 = pl.multiple_of(h * PEER_TOPK, PEER_TOPK)
        slot_scr[0, pl.ds(row0, PEER_TOPK), :] = _select_rows(a_idx, i1).astype(F32)
        slot_scr[1, pl.ds(row0, PEER_TOPK), :] = _select_rows(pos - start, i2).astype(F32)
        slot_scr[2, pl.ds(row0, PEER_TOPK), :] = gates
        return carry

    lax.fori_loop(0, PEER_HEADS, head_body, 0)
    i1_ref[...] = slot_scr[0].T.astype(jnp.int32)
    i2_ref[...] = slot_scr[1].T.astype(jnp.int32)
    gate_ref[...] = slot_scr[2].T


def _bf16_bits(x):
    return pltpu.bitcast(x.astype(BF16).astype(F32), jnp.uint32)


def _peer_gate_kernel(i1_ref, i2_ref, gate_ref, g_ref):
    rows = lax.broadcasted_iota(jnp.int32, (PEER_KEYS, PEER_SLOTS), 0)
    sub = PEER_GATE_SUBLANES

    def token_body(t, carry):
        a = i1_ref[pl.ds(t, 1), :]
        b = i2_ref[pl.ds(t, 1), :]
        w = gate_ref[pl.ds(t, 1), :]
        lhs = jnp.where(rows == a, w, 0.0).astype(BF16)
        rhs = jnp.where(rows == b, 1.0, 0.0).astype(BF16)
        gt = _mm(lhs, rhs, _NT_DIMS)
        row0 = pl.multiple_of(t * sub, sub)
        for g in range(PEER_KEYS // (2 * sub)):
            lo = gt[2 * sub * g:2 * sub * g + sub]
            hi = gt[2 * sub * g + sub:2 * sub * (g + 1)]
            g_ref[g, pl.ds(row0, sub), :] = (_bf16_bits(lo) >> 16) | _bf16_bits(hi)
        return carry

    lax.fori_loop(0, i1_ref.shape[0], token_body, 0, unroll=PEER_GATE_UNROLL)


def _peer_dense_kernel(xn_ref, u_ref, v_ref, gpk_ref, h_ref, mod_ref, o_ref, acc_ref):
    j = pl.program_id(1)

    @pl.when(j == 0)
    def _():
        acc_ref[...] = jnp.zeros_like(acc_ref)

    sub = PEER_GATE_SUBLANES
    xn = xn_ref[...]
    tokens = xn.shape[0]
    words = [gpk_ref[0, pl.ds(r, tokens, stride=sub), :] for r in range(sub)]
    for half in range(2):
        rows = slice(half * sub * PEER_KEYS, (half + 1) * sub * PEER_KEYS)
        hid = _gelu_erf(_mm(xn, u_ref[rows, :], _NT_DIMS))
        ys = []
        for r in range(sub):
            bits = (words[r] << 16) if half == 0 else (words[r] & jnp.uint32(0xFFFF0000))
            ys.append(pltpu.bitcast(bits, F32) * hid[:, r * PEER_KEYS:(r + 1) * PEER_KEYS])
        y = jnp.concatenate(ys, axis=1).astype(BF16)
        acc_ref[...] += _mm(y, v_ref[rows, :])

    @pl.when(j == pl.num_programs(1) - 1)
    def _():
        o_ref[...] = h_ref[...] + mod_ref[0] * acc_ref[...]


def _peer_layer(h, norm_g, shift, scale, gate_mod, wq_t_hi, wq_t_lo, keys_hi, keys_lo, u_bf, v_bf):
    n, d = h.shape
    nb = shift.shape[0]
    rows_per_batch = n // nb
    tr = min(PEER_ROUTE_TOKENS, rows_per_batch)
    full = lambda *shape: pl.BlockSpec(shape, lambda i: (0,) * len(shape))
    per_batch = lambda t: pl.BlockSpec((1, 1, d), lambda i: (i * t // rows_per_batch, 0, 0))
    xn, i1, i2, gate = pl.pallas_call(
        _peer_route_kernel,
        grid=(n // tr,),
        in_specs=[pl.BlockSpec((tr, d), lambda i: (i, 0)), full(1, d), per_batch(tr), per_batch(tr),
                  full(PEER_HEADS * PEER_DQ, d), full(PEER_HEADS * PEER_DQ, d),
                  full(2, PEER_HEADS, PEER_KEYS, PEER_DQ // 2), full(2, PEER_HEADS, PEER_KEYS, PEER_DQ // 2)],
        out_specs=[pl.BlockSpec((tr, d), lambda i: (i, 0))] + [pl.BlockSpec((tr, PEER_SLOTS), lambda i: (i, 0))] * 3,
        out_shape=[jax.ShapeDtypeStruct((n, d), BF16),
                   jax.ShapeDtypeStruct((n, PEER_SLOTS), jnp.int32),
                   jax.ShapeDtypeStruct((n, PEER_SLOTS), jnp.int32),
                   jax.ShapeDtypeStruct((n, PEER_SLOTS), F32)],
        scratch_shapes=[pltpu.VMEM((PEER_HEADS * PEER_DQ, tr), F32), pltpu.VMEM((3, PEER_SLOTS, tr), F32)],
        compiler_params=pltpu.CompilerParams(dimension_semantics=("arbitrary",)),
        name="peer_route",
    )(h, norm_g.reshape(1, d), shift, scale, wq_t_hi, wq_t_lo, keys_hi, keys_lo)

    tg = min(PEER_GATE_TOKENS, n)
    n_planes = PEER_KEYS // (2 * PEER_GATE_SUBLANES)
    slot_spec = pl.BlockSpec((tg, PEER_SLOTS), lambda i: (i, 0))
    gmat = pl.pallas_call(
        _peer_gate_kernel,
        grid=(n // tg,),
        in_specs=[slot_spec, slot_spec, slot_spec],
        out_specs=pl.BlockSpec((n_planes, tg * PEER_GATE_SUBLANES, PEER_KEYS), lambda i: (0, i, 0)),
        out_shape=jax.ShapeDtypeStruct((n_planes, n * PEER_GATE_SUBLANES, PEER_KEYS), jnp.uint32),
        compiler_params=pltpu.CompilerParams(dimension_semantics=("arbitrary",)),
        name="peer_gate",
    )(i1, i2, gate)

    tm = min(PEER_DENSE_TOKENS, rows_per_batch)
    te = PEER_DENSE_EXPERTS
    return pl.pallas_call(
        _peer_dense_kernel,
        grid=(n // tm, PEER_EXPERTS // te),
        in_specs=[pl.BlockSpec((tm, d), lambda i, j: (i, 0)),
                  pl.BlockSpec((te, d), lambda i, j: (j, 0)),
                  pl.BlockSpec((te, d), lambda i, j: (j, 0)),
                  pl.BlockSpec((1, tm * PEER_GATE_SUBLANES, PEER_KEYS), lambda i, j: (j, i, 0)),
                  pl.BlockSpec((tm, d), lambda i, j: (i, 0)),
                  pl.BlockSpec((1, 1, d), lambda i, j: (i * tm // rows_per_batch, 0, 0))],
        out_specs=pl.BlockSpec((tm, d), lambda i, j: (i, 0)),
        out_shape=jax.ShapeDtypeStruct((n, d), F32),
        scratch_shapes=[pltpu.VMEM((tm, d), F32)],
        compiler_params=pltpu.CompilerParams(dimension_semantics=("arbitrary", "arbitrary"),
                                             vmem_limit_bytes=52 * 2 ** 20),
        name="peer_dense",
    )(xn, u_bf, v_bf, gmat, h, gate_mod)


def _final_norm_kernel(x_ref, g_ref, o_ref):
    o_ref[...] = _rms(x_ref[...], g_ref[...])


def _final_norm(h, g):
    n = h.shape[0] * h.shape[1]
    x2 = h.reshape(n, D_MODEL)
    tm = 512
    out = pl.pallas_call(
        _final_norm_kernel,
        grid=(n // tm,),
        in_specs=[pl.BlockSpec((tm, D_MODEL), lambda i: (i, 0)),
                  pl.BlockSpec((1, D_MODEL), lambda i: (0, 0))],
        out_specs=pl.BlockSpec((tm, D_MODEL), lambda i: (i, 0)),
        out_shape=jax.ShapeDtypeStruct((n, D_MODEL), F32),
        name="final_norm",
    )(x2, g.reshape(1, D_MODEL))
    return out.reshape(h.shape)


def _mix_layer(h_lat, h_ctx, mod_l, mod_c, norm_g, w_in, w_out, ssd, gla, mla, s5, ctx_out):
    b, n_lat, d = h_lat.shape
    n_lat_tiles = n_lat // ROW_TILE
    hcomb = jnp.concatenate([h_lat, h_ctx], axis=1)
    tab = lambda k: jnp.concatenate([mod_l[k], mod_c[k]], axis=0)
    p = _inproj(hcomb, norm_g, tab(0), tab(1), _pack_w_in(w_in), n_lat_tiles)
    ssd_y, ssd_yb, ssd_xbc = _ssd_mixer(p, ssd["conv_w"], ssd["conv_b"], ssd["a_log"], ssd["dt_bias"], n_lat_tiles)
    gla_o, gla_ob = _gla_mixer(p, gla["gate_w"], gla["gate_b"], n_lat_tiles)
    mla_lat, mla_ctx = _mla_mixer(p, mla["q_norm_g"], mla["w_uq"], mla["kv_norm_g"], mla["w_ukv"], n_lat_tiles, ctx_out)
    s5_y = _s5_mixer(p, s5["a_re"], s5["a_im"], s5["log_dt"], s5["b_re"], s5["b_im"], s5["c_re"], s5["c_im"],
                     n_lat_tiles)
    post = functools.partial(_post, p=p, ssd_xbc=ssd_xbc, ssd_y=ssd_y, ssd_yb=ssd_yb, gla_o=gla_o, gla_ob=gla_ob,
                             s5_y=s5_y, ssd_d=ssd["d"],
                             ssd_norm_g=ssd["norm_g"], gla_norm_g=gla["norm_g"], s5_d=s5["d"],
                             glu_w=s5["glu_w"], glu_b=s5["glu_b"], w_out=w_out)
    new_lat = post(h_lat, mla_y=mla_lat, mod=mod_l[2], row_off=0, mla_off=0)
    new_ctx = None
    if ctx_out:
        new_ctx = post(h_ctx, mla_y=mla_ctx, mod=mod_c[2], row_off=n_lat_tiles, mla_off=0)
    return new_lat, new_ctx


def kernel(x, c, ctx, c_ctx, ada_w, ada_b, norm_mix_g, norm_ffn_g, w_in, w_out,
           ssd_conv_w, ssd_conv_b, ssd_a_log, ssd_dt_bias, ssd_d, ssd_norm_g,
           gla_gate_w, gla_gate_b, gla_norm_g, mla_q_norm_g, mla_w_uq, mla_kv_norm_g, mla_w_ukv,
           s5_a_re, s5_a_im, s5_log_dt, s5_b_re, s5_b_im, s5_c_re, s5_c_im, s5_d, s5_glu_w, s5_glu_b,
           peer_w_q, peer_sub_keys, peer_u, peer_v, final_norm_g):
    h_lat, h_ctx = x, ctx
    cond_lat = jax.nn.silu(c)[:, None, :]
    cond_ctx = jax.nn.silu(c_ctx)[None, None, :]
    for i in range(DEPTH):
        ctx_out = i < DEPTH - 1
        mod_l = jnp.split(cond_lat @ ada_w[i] + ada_b[i], N_MOD, axis=-1)
        mod_c = jnp.split(cond_ctx @ ada_w[i] + ada_b[i], N_MOD, axis=-1)
        ssd = dict(conv_w=ssd_conv_w[i], conv_b=ssd_conv_b[i], a_log=ssd_a_log[i], dt_bias=ssd_dt_bias[i],
                   d=ssd_d[i], norm_g=ssd_norm_g[i])
        gla = dict(gate_w=gla_gate_w[i], gate_b=gla_gate_b[i], norm_g=gla_norm_g[i])
        mla = dict(q_norm_g=mla_q_norm_g[i], w_uq=mla_w_uq[i], kv_norm_g=mla_kv_norm_g[i], w_ukv=mla_w_ukv[i])
        s5 = dict(a_re=s5_a_re[i], a_im=s5_a_im[i], log_dt=s5_log_dt[i], b_re=s5_b_re[i], b_im=s5_b_im[i],
                  c_re=s5_c_re[i], c_im=s5_c_im[i], d=s5_d[i], glu_w=s5_glu_w[i], glu_b=s5_glu_b[i])
        h_lat, h_ctx_new = _mix_layer(h_lat, h_ctx, mod_l, mod_c, norm_mix_g[i], w_in[i], w_out[i],
                                      ssd, gla, mla, s5, ctx_out)
        wq_t_hi, wq_t_lo = _split_bf16(peer_w_q[i].T)
        keys_hi, keys_lo = _split_bf16(peer_sub_keys[i])
        u_bf, v_bf = peer_u[i].astype(BF16), peer_v[i].astype(BF16)
        peer = functools.partial(_peer_layer, norm_g=norm_ffn_g[i], wq_t_hi=wq_t_hi, wq_t_lo=wq_t_lo,
                                 keys_hi=keys_hi, keys_lo=keys_lo, u_bf=u_bf, v_bf=v_bf)
        h_lat = peer(h_lat.reshape(-1, D_MODEL), shift=mod_l[3], scale=mod_l[4],
                     gate_mod=mod_l[5]).reshape(h_lat.shape)
        if ctx_out:
            h_ctx = peer(h_ctx_new.reshape(-1, D_MODEL), shift=mod_c[3], scale=mod_c[4],
                         gate_mod=mod_c[5]).reshape(h_ctx.shape)
    return _final_norm(h_lat, final_norm_g)
```

```python
import functools
import jax
import jax.numpy as jnp
from jax import lax
import numpy as np
from jax.experimental import pallas as pl
from jax.experimental.pallas import tpu as pltpu

D_MODEL = 1024
DEPTH = 2
GRID_W = 64
NORM_EPS = 1e-6
N_MOD = 6

GROUP_WIDTH = D_MODEL // 4

SSD_WIDTH = GROUP_WIDTH
SSD_HEAD_DIM = 64
SSD_HEADS = SSD_WIDTH // SSD_HEAD_DIM
SSD_GROUPS = 2
SSD_STATE = 128
SSD_CONV = 5
SSD_CHUNK = 128
SSD_CONV_CH = SSD_WIDTH + 2 * SSD_GROUPS * SSD_STATE
SSD_IN = SSD_WIDTH + SSD_CONV_CH + 2 * SSD_HEADS

GLA_WIDTH = GROUP_WIDTH
GLA_HEADS = 4
GLA_DV = GLA_WIDTH // GLA_HEADS
GLA_DK = GLA_DV // 2
GLA_QK = GLA_HEADS * GLA_DK
GLA_GATE_RANK = 16
GLA_TAU = 16.0
GLA_CHUNK = 64
GLA_IN = 2 * GLA_QK + 2 * GLA_WIDTH + 2 * GLA_GATE_RANK

MLA_WIDTH = GROUP_WIDTH
MLA_HEADS = 4
MLA_V = MLA_WIDTH // MLA_HEADS
MLA_NOPE = 64
MLA_ROPE = 32
MLA_Q_RANK = 256
MLA_KV_RANK = 128
MLA_SCALE = (MLA_NOPE + MLA_ROPE) ** -0.5
ROPE_BASE = 10000.0
MLA_IN = MLA_Q_RANK + MLA_KV_RANK + MLA_ROPE

S5_WIDTH = GROUP_WIDTH
S5_GROUP = 16
S5_NGROUPS = S5_WIDTH // S5_GROUP
S5_STATE = 64
S5_MAX_RE = -1e-4
S5_IN = S5_WIDTH
S5_CHUNK = 16
S5_PAIRS = S5_NGROUPS // 2
S5_PACK_ROWS = (768, 256)

PEER_KEYS = 128
PEER_EXPERTS = PEER_KEYS * PEER_KEYS
PEER_HEADS = 8
PEER_TOPK = 16
PEER_DQ = 128

LANES = 128
ROW_TILE = 256
SCAN_STEP_ROWS = ROW_TILE

F32 = jnp.float32
BF16 = jnp.bfloat16

COL_XS, COL_BM, COL_CM, COL_Z = 0, 256, 512, 768
COL_GLA_V, COL_GLA_R, COL_CQ, COL_S5 = 1024, 1280, 1536, 1792
COL_GLA_Q, COL_GLA_K, COL_CKV, COL_DT, COL_GLR, COL_KR, COL_KRROT = 2048, 2176, 2304, 2432, 2560, 2688, 2816
P_COLS = 2944

_NN_DIMS = (((1,), (0,)), ((), ()))
_NT_DIMS = (((1,), (1,)), ((), ()))
_TN_DIMS = (((0,), (0,)), ((), ()))


def _mm(a, b, dims=_NN_DIMS):
    return lax.dot_general(a, b, dims, preferred_element_type=F32)


def _split_bf16(x):
    hi = x.astype(BF16)
    lo = (x - hi.astype(F32)).astype(BF16)
    return hi, lo


def _split3_bf16(x):
    p1 = x.astype(BF16)
    r1 = x - p1.astype(F32)
    p2 = r1.astype(BF16)
    p3 = (r1 - p2.astype(F32)).astype(BF16)
    return p1, p2, p3


def _dot3(a_hi, a_lo, b_hi, b_lo, dims):
    return _mm(a_hi, b_hi, dims) + _mm(a_hi, b_lo, dims) + _mm(a_lo, b_hi, dims)


def _gelu_erf(x):
    return 0.5 * x * (1.0 + lax.erf(x * (2.0 ** -0.5)))


def _silu(x):
    return x * jax.nn.sigmoid(x)


def _softplus(x):
    return jnp.maximum(x, 0.0) + jnp.log1p(jnp.exp(-jnp.abs(x)))


def _log_sigmoid(x):
    return jnp.minimum(x, 0.0) - jnp.log1p(jnp.exp(-jnp.abs(x)))


def _rms(x, g):
    return x * lax.rsqrt(jnp.mean(x * x, axis=-1, keepdims=True) + NORM_EPS) * g


def _modulated_norm(x, g, shift, scale):
    return _rms(x, g) * (1.0 + scale) + shift


def _causal_mask(n, reverse):
    ri = lax.broadcasted_iota(jnp.int32, (n, n), 0)
    ci = lax.broadcasted_iota(jnp.int32, (n, n), 1)
    return (ci >= ri) if reverse else (ci <= ri)


def _scan_chunk(s, n_lat, n_ctx, reverse):
    if reverse:
        return n_lat + n_ctx - 1 - s
    return jnp.where(s < n_ctx, n_lat + s, s - n_ctx)


def _inproj_kernel(h_ref, g_ref, shift_ref, scale_ref, w_ref, o_ref):
    xn = _modulated_norm(h_ref[0], g_ref[...], shift_ref[0], scale_ref[0])
    o_ref[0] = _mm(xn.astype(BF16), w_ref[...])


def _inproj(hcomb, norm_g, shift_tab, scale_tab, w_pad, n_lat_tiles):
    b, r, d = hcomb.shape
    mod_spec = pl.BlockSpec((1, 1, d), lambda i, t: (jnp.where(t < n_lat_tiles, i, b), 0, 0))
    return pl.pallas_call(
        _inproj_kernel,
        grid=(b, r // ROW_TILE),
        in_specs=[pl.BlockSpec((1, ROW_TILE, d), lambda i, t: (i, t, 0)),
                  pl.BlockSpec((1, d), lambda i, t: (0, 0)), mod_spec, mod_spec,
                  pl.BlockSpec((d, P_COLS), lambda i, t: (0, 0))],
        out_specs=pl.BlockSpec((1, ROW_TILE, P_COLS), lambda i, t: (i, t, 0)),
        out_shape=jax.ShapeDtypeStruct((b, r, P_COLS), F32),
        compiler_params=pltpu.CompilerParams(dimension_semantics=("arbitrary", "arbitrary"),
                                             vmem_limit_bytes=48 * 2 ** 20),
        name="inproj",
    )(hcomb, norm_g.reshape(1, d), shift_tab, scale_tab, w_pad)


def _pack_w_in(w):
    o_ssd, o_gla, o_mla, o_s5 = 0, SSD_IN, SSD_IN + GLA_IN, SSD_IN + GLA_IN + MLA_IN
    out = jnp.zeros((w.shape[0], P_COLS), F32)
    put = lambda out, col, src, width: out.at[:, col:col + width].set(w[:, src:src + width])
    out = put(out, COL_Z, o_ssd, SSD_WIDTH)
    out = put(out, COL_XS, o_ssd + SSD_WIDTH, SSD_CONV_CH)
    out = put(out, COL_DT, o_ssd + SSD_WIDTH + SSD_CONV_CH, 2 * SSD_HEADS)
    out = put(out, COL_GLA_Q, o_gla, GLA_QK)
    out = put(out, COL_GLA_K, o_gla + GLA_QK, GLA_QK)
    out = put(out, COL_GLA_V, o_gla + 2 * GLA_QK, GLA_WIDTH)
    out = put(out, COL_GLA_R, o_gla + 2 * GLA_QK + GLA_WIDTH, GLA_WIDTH)
    out = put(out, COL_GLR, o_gla + 2 * GLA_QK + 2 * GLA_WIDTH, 2 * GLA_GATE_RANK)
    out = put(out, COL_CQ, o_mla, MLA_Q_RANK)
    out = put(out, COL_CKV, o_mla + MLA_Q_RANK, MLA_KV_RANK)
    o_kr = o_mla + MLA_Q_RANK + MLA_KV_RANK
    half = MLA_ROPE // 2
    out = put(out, COL_KR + MLA_NOPE, o_kr, MLA_ROPE)
    out = out.at[:, COL_KRROT + MLA_NOPE:COL_KRROT + MLA_NOPE + half].set(-w[:, o_kr + half:o_kr + MLA_ROPE])
    out = out.at[:, COL_KRROT + MLA_NOPE + half:COL_KRROT + MLA_NOPE + MLA_ROPE].set(w[:, o_kr:o_kr + half])
    out = put(out, COL_S5, o_s5, S5_WIDTH)
    return out.astype(BF16)


def _ssd_prep_kernel(x_ref, prev_ref, next_ref, dt_ref, w_ref, b_ref, bias_ref, xbc_ref, dtc_ref, dtt_ref,
                     *, n_lat_tiles):
    t = pl.program_id(1)
    x = x_ref[0]
    halo = prev_ref.shape[1]
    prev = jnp.where(jnp.logical_and(t > 0, t < n_lat_tiles), prev_ref[0], 0.0)
    nxt = jnp.where(t < n_lat_tiles - 1, next_ref[0], 0.0)
    ext = jnp.concatenate([prev, x, nxt], axis=0)
    rows = ext.shape[0]
    left = SSD_CONV // 2
    acc = jnp.zeros_like(x) + b_ref[...]
    for k in range(SSD_CONV):
        shifted = ext if k == left else pltpu.roll(ext, (left - k) % rows, 0)
        acc = acc + w_ref[k:k + 1, :] * shifted[halo:halo + x.shape[0]]
    xbc_ref[0] = _silu(acc)
    dt = _softplus(dt_ref[0] + bias_ref[...])
    dtc_ref[0] = dt
    dtt_ref[0] = dt.T[:dtt_ref.shape[1]]


def _ssd_scan_kernel(xbc_f_ref, dtc_f_ref, dtt_f_ref, xbc_b_ref, dtc_b_ref, dtt_b_ref, ahr_ref, ahc_ref,
                     yf_ref, yb_ref, state_ref):
    @pl.when(pl.program_id(1) == 0)
    def _():
        state_ref[...] = jnp.zeros_like(state_ref)

    q = SSD_CHUNK
    n_sub = xbc_f_ref.shape[1] // q
    for direction, (xbc_ref, dtc_ref, dtt_ref, y_ref) in enumerate(
            ((xbc_f_ref, dtc_f_ref, dtt_f_ref, yf_ref), (xbc_b_ref, dtc_b_ref, dtt_b_ref, yb_ref))):
        for c in (range(n_sub) if direction == 0 else range(n_sub - 1, -1, -1)):
            rows = slice(c * q, (c + 1) * q)
            y_ref[0, rows] = _ssd_chunk(xbc_ref[0, rows], dtc_ref[0, rows], dtt_ref[0, :, rows], ahr_ref, ahc_ref,
                                        state_ref.at[direction], direction)


def _ssd_chunk(xbc, dtc, dtt, ahr_ref, ahc_ref, state_ref, direction):
    reverse = direction == 1
    q = SSD_CHUNK
    mask = _causal_mask(q, reverse)
    tri = jnp.where(mask, 1.0, 0.0).astype(BF16)
    xs, bm, cm = xbc[:, :SSD_WIDTH], xbc[:, SSD_WIDTH:SSD_WIDTH + 256], xbc[:, SSD_WIDTH + 256:]
    a_col = dtc * ahr_ref[...]
    a_row = dtt * ahc_ref[...]
    acum_col = sum(_mm(tri, part) for part in _split3_bf16(a_col))
    acum_row = sum(_mm(part, tri, _NT_DIMS) for part in _split3_bf16(a_row))
    end = 0 if reverse else q - 1
    bm_bf, cm_bf = bm.astype(BF16), cm.astype(BF16)
    ys = []
    cb = {}
    for h in range(SSD_HEADS):
        g = h // (SSD_HEADS // SSD_GROUPS)
        gs = slice(g * SSD_STATE, (g + 1) * SSD_STATE)
        if g not in cb:
            cb[g] = _mm(cm_bf[:, gs], bm_bf[:, gs], _NT_DIMS)
        ch = direction * SSD_HEADS + h
        ac = acum_col[:, ch:ch + 1]
        ar = acum_row[ch:ch + 1, :]
        decay = jnp.exp(jnp.where(mask, ac - ar, -jnp.inf))
        xd = xs[:, h * SSD_HEAD_DIM:(h + 1) * SSD_HEAD_DIM] * dtc[:, ch:ch + 1]
        y_diag = _mm((cb[g] * decay).astype(BF16), xd.astype(BF16))
        a_end = ac[end:end + 1, :]
        st_local = _mm((xd * jnp.exp(a_end - ac)).astype(BF16), bm_bf[:, gs], _TN_DIMS)
        hs = state_ref[h]
        y_off = jnp.exp(ac) * _mm(cm_bf[:, gs], hs.astype(BF16), _NT_DIMS)
        state_ref[h] = jnp.exp(a_end) * hs + st_local
        ys.append(y_diag + y_off)
    return jnp.concatenate(ys, axis=1)


def _ssd_mixer(p, conv_w, conv_b, a_log, dt_bias, n_lat_tiles):
    b, r, _ = p.shape
    nt = r // ROW_TILE
    halo = 8
    hb = ROW_TILE // halo
    w8 = jnp.zeros((8, SSD_CONV_CH), F32).at[:SSD_CONV].set(conv_w)
    bias = jnp.zeros((1, LANES), F32).at[0, :2 * SSD_HEADS].set(dt_bias.reshape(-1))
    xbc, dtc, dtt = pl.pallas_call(
        functools.partial(_ssd_prep_kernel, n_lat_tiles=n_lat_tiles),
        grid=(b, nt),
        in_specs=[pl.BlockSpec((1, ROW_TILE, SSD_CONV_CH), lambda i, t: (i, t, 0)),
                  pl.BlockSpec((1, halo, SSD_CONV_CH), lambda i, t: (i, jnp.maximum(t * hb - 1, 0), 0)),
                  pl.BlockSpec((1, halo, SSD_CONV_CH), lambda i, t: (i, jnp.minimum((t + 1) * hb, nt * hb - 1), 0)),
                  pl.BlockSpec((1, ROW_TILE, LANES), lambda i, t: (i, t, COL_DT // LANES)),
                  pl.BlockSpec((8, SSD_CONV_CH), lambda i, t: (0, 0)),
                  pl.BlockSpec((1, SSD_CONV_CH), lambda i, t: (0, 0)),
                  pl.BlockSpec((1, LANES), lambda i, t: (0, 0))],
        out_specs=[pl.BlockSpec((1, ROW_TILE, SSD_CONV_CH), lambda i, t: (i, t, 0)),
                   pl.BlockSpec((1, ROW_TILE, LANES), lambda i, t: (i, t, 0)),
                   pl.BlockSpec((1, 8, ROW_TILE), lambda i, t: (i, 0, t))],
        out_shape=[jax.ShapeDtypeStruct((b, r, SSD_CONV_CH), F32),
                   jax.ShapeDtypeStruct((b, r, LANES), F32),
                   jax.ShapeDtypeStruct((b, 8, r), F32)],
        compiler_params=pltpu.CompilerParams(dimension_semantics=("arbitrary", "arbitrary")),
        name="ssd_prep",
    )(p, p, p, p, w8, conv_b.reshape(1, -1), bias)

    a_head = -jnp.exp(a_log.astype(F32)).reshape(-1)
    ahr = jnp.zeros((1, LANES), F32).at[0, :2 * SSD_HEADS].set(a_head)
    ahc = a_head.reshape(2 * SSD_HEADS, 1)
    blk = SCAN_STEP_ROWS
    n_lat = n_lat_tiles * ROW_TILE // blk
    n_ctx = r // blk - n_lat
    in_specs, y_specs = [], []
    for reverse in (False, True):
        cidx = functools.partial(_scan_chunk, n_lat=n_lat, n_ctx=n_ctx, reverse=reverse)
        in_specs += [pl.BlockSpec((1, blk, SSD_CONV_CH), lambda i, s, cidx=cidx: (i, cidx(s), 0)),
                     pl.BlockSpec((1, blk, LANES), lambda i, s, cidx=cidx: (i, cidx(s), 0)),
                     pl.BlockSpec((1, 8, blk), lambda i, s, cidx=cidx: (i, 0, cidx(s)))]
        y_specs.append(pl.BlockSpec((1, blk, SSD_WIDTH), lambda i, s, cidx=cidx: (i, cidx(s), 0)))
    in_specs += [pl.BlockSpec((1, LANES), lambda i, s: (0, 0)), pl.BlockSpec((2 * SSD_HEADS, 1), lambda i, s: (0, 0))]
    y_f, y_b = pl.pallas_call(
        _ssd_scan_kernel,
        grid=(b, n_lat + n_ctx),
        in_specs=in_specs,
        out_specs=y_specs,
        out_shape=[jax.ShapeDtypeStruct((b, r, SSD_WIDTH), F32)] * 2,
        scratch_shapes=[pltpu.VMEM((2, SSD_HEADS, SSD_HEAD_DIM, SSD_STATE), F32)],
        compiler_params=pltpu.CompilerParams(dimension_semantics=("arbitrary", "arbitrary")),
        name="ssd_scan",
    )(xbc, dtc, dtt, xbc, dtc, dtt, ahr, ahc)
    return y_f, y_b, xbc


def _gla_scan_kernel(qf_ref, kf_ref, vf_ref, glrf_ref, qb_ref, kb_ref, vb_ref, glrb_ref, wg_ref, bias_ref,
                     of_ref, ob_ref, st_ref):
    @pl.when(pl.program_id(1) == 0)
    def _():
        st_ref[...] = jnp.zeros_like(st_ref)

    n = GLA_CHUNK
    n_sub = qf_ref.shape[1] // n
    for direction, (q_ref, k_ref, v_ref, glr_ref, o_ref) in enumerate(
            ((qf_ref, kf_ref, vf_ref, glrf_ref, of_ref), (qb_ref, kb_ref, vb_ref, glrb_ref, ob_ref))):
        for c in (range(n_sub) if direction == 0 else range(n_sub - 1, -1, -1)):
            rows = slice(c * n, (c + 1) * n)
            o_ref[0, rows] = _gla_chunk(q_ref[0, rows], k_ref[0, rows], v_ref[0, rows], glr_ref[0, rows],
                                        wg_ref.at[direction], bias_ref.at[direction], st_ref.at[direction],
                                        direction == 1)


def _gla_chunk(q, k, v, glr, wg_ref, bias_ref, st_ref, reverse):
    n = GLA_CHUNK
    mask = _causal_mask(n, reverse)
    tri = jnp.where(mask, 1.0, 0.0).astype(BF16)
    g_hi, g_lo = _split_bf16(glr)
    logits = _dot3(g_hi, g_lo, wg_ref[0], wg_ref[1], _NN_DIMS) + bias_ref[...]
    logg = _log_sigmoid(logits) * (1.0 / GLA_TAU)
    bcum = sum(_mm(tri, part) for part in _split3_bf16(logg))
    end = 0 if reverse else n - 1
    b_end = bcum[end:end + 1, :]
    qe = q * jnp.exp(bcum) * (GLA_DK ** -0.5)
    ke = (k * jnp.exp(-bcum)).astype(BF16)
    kd = k * jnp.exp(b_end - bcum)
    decay_end = jnp.exp(b_end)
    lane_head = lax.broadcasted_iota(jnp.int32, (1, GLA_QK), 1) >> (GLA_DK.bit_length() - 1)
    outs = []
    for h in range(GLA_HEADS):
        hm = lane_head == h
        qh = jnp.where(hm, qe, 0.0).astype(BF16)
        att = jnp.where(mask, _mm(qh, ke, _NT_DIMS), 0.0)
        vh = v[:, h * GLA_DV:(h + 1) * GLA_DV].astype(BF16)
        st = st_ref[h]
        o_h = _mm(att.astype(BF16), vh) + _mm(qh, st.astype(BF16), _NT_DIMS)
        local = _mm(vh, jnp.where(hm, kd, 0.0).astype(BF16), _TN_DIMS)
        st_ref[h] = st * decay_end + local
        outs.append(o_h)
    return jnp.concatenate(outs, axis=1)


def _gla_mixer(p, gate_w, gate_b, n_lat_tiles):
    b, r, _ = p.shape
    rows = SCAN_STEP_ROWS
    n_lat = n_lat_tiles * ROW_TILE // rows
    n_ctx = r // rows - n_lat
    in_specs, o_specs, wgs = [], [], []
    for direction in (0, 1):
        cidx = functools.partial(_scan_chunk, n_lat=n_lat, n_ctx=n_ctx, reverse=direction == 1)
        wg = jnp.zeros((LANES, GLA_QK), F32).at[direction * GLA_GATE_RANK:(direction + 1) * GLA_GATE_RANK].set(
            gate_w[direction])
        wgs.append(jnp.stack(_split_bf16(wg)))
        blk = lambda width, col, cidx=cidx: pl.BlockSpec((1, rows, width), lambda i, s: (i, cidx(s), col // width))
        in_specs += [blk(GLA_QK, COL_GLA_Q), blk(GLA_QK, COL_GLA_K), blk(GLA_WIDTH, COL_GLA_V), blk(LANES, COL_GLR)]
        o_specs.append(pl.BlockSpec((1, rows, GLA_WIDTH), lambda i, s, cidx=cidx: (i, cidx(s), 0)))
    in_specs += [pl.BlockSpec((2, 2, LANES, GLA_QK), lambda i, s: (0, 0, 0, 0)),
                 pl.BlockSpec((2, 1, GLA_QK), lambda i, s: (0, 0, 0))]
    return pl.pallas_call(
        _gla_scan_kernel,
        grid=(b, n_lat + n_ctx),
        in_specs=in_specs,
        out_specs=o_specs,
        out_shape=[jax.ShapeDtypeStruct((b, r, GLA_WIDTH), F32)] * 2,
        scratch_shapes=[pltpu.VMEM((2, GLA_HEADS, GLA_DV, GLA_QK), F32)],
        compiler_params=pltpu.CompilerParams(dimension_semantics=("arbitrary", "arbitrary")),
        name="gla_scan",
    )(p, p, p, p, p, p, p, p, jnp.stack(wgs), gate_b.reshape(2, 1, GLA_QK))


MLA_Q_TILE = 1024
MLA_K_TILES = (768, 256)


def _mla_prep_kernel(cq_ref, ckv_ref, kr_ref, krrot_ref, onec_ref, sinr_ref, gq_ref, gkv_ref,
                     wq_ref, wqr_ref, wk_ref, wv_ref, q_ref, k_ref, v_ref):
    qn = _rms(cq_ref[0], gq_ref[...]).astype(BF16)
    kvn = _rms(ckv_ref[0], gkv_ref[...]).astype(BF16)
    onec, sinr = onec_ref[...], sinr_ref[...]
    k_rope = kr_ref[0] * onec + krrot_ref[0] * sinr
    ones_lane = jnp.where(lax.broadcasted_iota(jnp.int32, (1, LANES), 1) == MLA_V, 1.0, 0.0)
    for h in range(MLA_HEADS):
        qh = _mm(qn, wq_ref[h]) * onec + _mm(qn, wqr_ref[h]) * sinr
        q_ref[0, h] = (qh * MLA_SCALE).astype(BF16)
        k_ref[0, h] = (_mm(kvn, wk_ref[h]) + k_rope).astype(BF16)
        v_ref[0, h] = (_mm(kvn, wv_ref[h]) + ones_lane).astype(BF16)


def _mla_attn_kernel(q_ref, k_ref, v_ref, o_ref, m_ref, acc_ref):
    j = pl.program_id(2)

    @pl.when(j == 0)
    def _():
        m_ref[...] = jnp.full_like(m_ref, -jnp.inf)
        acc_ref[...] = jnp.zeros_like(acc_ref)

    reps = k_ref.shape[2] // LANES
    for h in range(MLA_HEADS):
        s = _mm(q_ref[0, h], k_ref[0, h], _NT_DIMS)
        m_prev = m_ref[h]
        m_new = jnp.maximum(m_prev, jnp.max(s, axis=1, keepdims=True))
        p = jnp.exp((s - jnp.concatenate([m_new] * reps, axis=1)).astype(BF16))
        acc_ref[h] = jnp.exp(m_prev - m_new) * acc_ref[h] + _mm(p, v_ref[0, h])
        m_ref[h] = m_new

    @pl.when(j == pl.num_programs(2) - 1)
    def _():
        outs = []
        for h in range(MLA_HEADS):
            acc = acc_ref[h]
            outs.append(acc[:, :MLA_V] / acc[:, MLA_V:MLA_V + 1])
        o_ref[0] = jnp.concatenate(outs, axis=1)


def _rope_tables(n_lat, n_rows):
    rows = n_lat // GRID_W
    row = jnp.repeat(jnp.arange(rows, dtype=F32), GRID_W)
    col = jnp.tile(jnp.arange(GRID_W, dtype=F32), rows)
    half = MLA_ROPE // 2
    inv = ROPE_BASE ** (-jnp.arange(0, half, 2, dtype=F32) / half)
    ang = jnp.concatenate([row[:, None] * inv, col[:, None] * inv], axis=-1)
    cos = jnp.concatenate([jnp.cos(ang), jnp.ones((n_rows - n_lat, half), F32)], axis=0)
    sin = jnp.concatenate([jnp.sin(ang), jnp.zeros((n_rows - n_lat, half), F32)], axis=0)
    pad = jnp.zeros((n_rows, LANES - MLA_NOPE - MLA_ROPE), F32)
    onec = jnp.concatenate([jnp.ones((n_rows, MLA_NOPE), F32), cos, cos, pad], axis=1)
    sinr = jnp.concatenate([jnp.zeros((n_rows, MLA_NOPE), F32), sin, sin, pad], axis=1)
    return onec, sinr


def _mla_weights(w_uq, w_ukv):
    dqk = MLA_NOPE + MLA_ROPE
    half = MLA_ROPE // 2
    wq = w_uq.reshape(MLA_Q_RANK, MLA_HEADS, dqk).transpose(1, 0, 2)
    rot = jnp.concatenate([jnp.zeros_like(wq[..., :MLA_NOPE]), -wq[..., MLA_NOPE + half:], wq[..., MLA_NOPE:MLA_NOPE + half]],
                          axis=-1)
    padq = lambda w: jnp.pad(w, ((0, 0), (0, 0), (0, LANES - dqk))).astype(BF16)
    wkv = w_ukv.reshape(MLA_KV_RANK, MLA_HEADS, MLA_NOPE + MLA_V).transpose(1, 0, 2)
    padk = lambda w: jnp.pad(w, ((0, 0), (0, 0), (0, LANES - w.shape[-1]))).astype(BF16)
    return padq(wq), padq(rot), padk(wkv[..., :MLA_NOPE]), padk(wkv[..., MLA_NOPE:])


def _mla_attention(q, k, v, q_tile, q_off, n_q, kt, k_off, n_k):
    b = q.shape[0]
    return pl.pallas_call(
        _mla_attn_kernel,
        grid=(b, n_q, n_k),
        in_specs=[pl.BlockSpec((1, MLA_HEADS, q_tile, LANES), lambda i, a, j: (i, 0, q_off + a, 0)),
                  pl.BlockSpec((1, MLA_HEADS, kt, LANES), lambda i, a, j: (i, 0, k_off + j, 0)),
                  pl.BlockSpec((1, MLA_HEADS, kt, LANES), lambda i, a, j: (i, 0, k_off + j, 0))],
        out_specs=pl.BlockSpec((1, q_tile, MLA_WIDTH), lambda i, a, j: (i, a, 0)),
        out_shape=jax.ShapeDtypeStruct((b, n_q * q_tile, MLA_WIDTH), F32),
        scratch_shapes=[pltpu.VMEM((MLA_HEADS, q_tile, LANES), F32), pltpu.VMEM((MLA_HEADS, q_tile, LANES), F32)],
        compiler_params=pltpu.CompilerParams(dimension_semantics=("arbitrary", "arbitrary", "arbitrary")),
        name="mla_attn",
    )(q, k, v)


def _mla_mixer(p, q_norm_g, w_uq, kv_norm_g, w_ukv, n_lat_tiles, ctx_out):
    b, r, _ = p.shape
    nt = r // ROW_TILE
    n_lat = n_lat_tiles * ROW_TILE
    onec, sinr = _rope_tables(n_lat, r)
    wq, wqr, wk, wv = _mla_weights(w_uq, w_ukv)
    blk = lambda width, col: pl.BlockSpec((1, ROW_TILE, width), lambda i, t: (i, t, col // width))
    tab = pl.BlockSpec((ROW_TILE, LANES), lambda i, t: (t, 0))
    full = lambda *shape: pl.BlockSpec(shape, lambda i, t: (0,) * len(shape))
    head_out = pl.BlockSpec((1, MLA_HEADS, ROW_TILE, LANES), lambda i, t: (i, 0, t, 0))
    q, k, v = pl.pallas_call(
        _mla_prep_kernel,
        grid=(b, nt),
        in_specs=[blk(MLA_Q_RANK, COL_CQ), blk(LANES, COL_CKV), blk(LANES, COL_KR), blk(LANES, COL_KRROT), tab, tab,
                  full(1, MLA_Q_RANK), full(1, MLA_KV_RANK),
                  full(MLA_HEADS, MLA_Q_RANK, LANES), full(MLA_HEADS, MLA_Q_RANK, LANES),
                  full(MLA_HEADS, MLA_KV_RANK, LANES), full(MLA_HEADS, MLA_KV_RANK, LANES)],
        out_specs=[head_out] * 3,
        out_shape=[jax.ShapeDtypeStruct((b, MLA_HEADS, r, LANES), BF16)] * 3,
        compiler_params=pltpu.CompilerParams(dimension_semantics=("arbitrary", "arbitrary")),
        name="mla_prep",
    )(p, p, p, p, onec, sinr, q_norm_g.reshape(1, -1), kv_norm_g.reshape(1, -1), wq, wqr, wk, wv)
    q_tile = min(MLA_Q_TILE, n_lat)
    k_tile = next(t for t in MLA_K_TILES if r % t == 0)
    y_lat = _mla_attention(q, k, v, q_tile, 0, n_lat // q_tile, k_tile, 0, r // k_tile)
    y_ctx = None
    if ctx_out:
        n_ctx = r - n_lat
        y_ctx = _mla_attention(q, k, v, n_ctx, n_lat // n_ctx, 1, n_ctx, n_lat // n_ctx, 1)
    return y_lat, y_ctx


def _s5_matrices(a_re, a_im, log_dt, b_re, b_im, c_re, c_im):
    q, ng, ns, nc = S5_CHUNK, S5_NGROUPS, S5_STATE, S5_GROUP
    lam = jnp.minimum(a_re.astype(F32), S5_MAX_RE) + 1j * a_im.astype(F32)
    step = jnp.exp(log_dt.astype(F32))[..., None]
    abar = jnp.exp(lam * step)
    bmat = b_re.astype(F32) + 1j * b_im.astype(F32)
    bbar = ((abar - 1.0) / lam)[..., None] * bmat
    cmat = c_re.astype(F32) + 1j * c_im.astype(F32)
    pw = jnp.exp((lam * step)[..., None] * jnp.arange(q + 1, dtype=F32))
    kern = jnp.einsum('dgcn,dgnl,dgnk->dglck', cmat, pw[..., :q], bbar).real
    ii = jnp.arange(q)
    lag_f = ii[None, :] - ii[:, None]
    gather = lambda kd, lag: jnp.where((lag >= 0)[None, :, :, None, None], kd[:, jnp.clip(lag, 0, q - 1)], 0.0)
    t_f = gather(kern[0], lag_f).transpose(0, 1, 4, 2, 3)
    t_b = gather(kern[1], -lag_f).transpose(0, 1, 4, 2, 3)
    t_sum = (t_f + t_b).reshape(ng, q * nc, q * nc)
    pw_f = pw[0][..., q - 1 - ii]
    pw_b = pw[1][..., ii]
    wst = lambda pwd, bb: jnp.einsum('gnj,gnc->gjcn', pwd, bb).reshape(ng, q * nc, ns)
    wst_f, wst_b = wst(pw_f, bbar[0]), wst(pw_b, bbar[1])
    wout = lambda pwd, cm: jnp.einsum('gcn,gni->gnic', cm, pwd).reshape(ng, ns, q * nc)
    wo_f, wo_b = wout(pw[0][..., ii + 1], cmat[0]), wout(pw[1][..., q - ii], cmat[1])
    aq = pw[..., q]

    def pair_cols(x):
        x = x.reshape(S5_PAIRS, 2, x.shape[1], x.shape[2])
        z = jnp.zeros_like(x[:, 0])
        return jnp.concatenate([jnp.concatenate([x[:, 0], z], axis=2), jnp.concatenate([z, x[:, 1]], axis=2)], axis=1)

    w_local = jnp.concatenate([pair_cols(wst_f.real), pair_cols(wst_f.imag),
                               pair_cols(wst_b.real), pair_cols(wst_b.imag)], axis=2)
    w_out = jnp.concatenate([pair_cols(t_sum), pair_cols(wo_f.real), pair_cols(-wo_f.imag),
                             pair_cols(wo_b.real), pair_cols(-wo_b.imag)], axis=1)
    aq_pair = aq.reshape(2, S5_PAIRS, 2 * ns)
    aq_tab = jnp.concatenate([aq_pair[0].real, aq_pair[0].imag, aq_pair[1].real, aq_pair[1].imag], axis=1)
    return w_local.astype(BF16), w_out.astype(BF16), aq_tab.reshape(S5_PAIRS, 1, 8 * ns).astype(F32)


def _s5_perm():
    cols = S5_CHUNK * S5_WIDTH
    c = jnp.arange(cols, dtype=jnp.int32)
    cc, j = c % S5_GROUP, (c // S5_GROUP) % S5_CHUNK
    g = c // (S5_GROUP * S5_CHUNK)
    per_half = LANES // S5_GROUP
    src = (g // per_half) * (S5_CHUNK * LANES) + j * LANES + (g % per_half) * S5_GROUP + cc
    return jnp.where(c[:, None] == src[None, :], 1.0, 0.0).astype(BF16)


def _s5_pack_kernel(u_ref, o_ref):
    n = o_ref.shape[1]
    for j in range(S5_CHUNK):
        o_ref[0, :, j * LANES:(j + 1) * LANES] = u_ref[0, pl.ds(j, n, stride=S5_CHUNK), :].astype(BF16)


def _s5_unpack_kernel(y_ref, o_ref):
    n = y_ref.shape[1]
    for i in range(S5_CHUNK):
        o_ref[0, pl.ds(i, n, stride=S5_CHUNK), :] = y_ref[0, :, i * LANES:(i + 1) * LANES]


def _s5_local_kernel(u_ref, perm_ref, w_ref, up_ref, s_ref):
    up = _mm(u_ref[0], perm_ref[...]).astype(BF16)
    up_ref[0] = up
    s_ref[0] = _mm(up, w_ref[0])


def _s5_scan_kernel(s3_ref, aq_ref, hs3_ref, *, n_lat, n_ctx, nb):
    s_ref, hs_ref = s3_ref.at[0], hs3_ref.at[0]
    w = 2 * S5_STATE
    aq = aq_ref[0]
    a = [aq[:, i * w:(i + 1) * w] for i in range(4)]
    zero = jnp.zeros((nb, w), F32)
    slab = 8
    cps = slab // nb

    def run_slab(s_re, s_im, a_re, a_im, h_re, h_im, order):
        ent_re, ent_im = [None] * cps, [None] * cps
        for c in order:
            ent_re[c], ent_im[c] = h_re, h_im
            rows = slice(c * nb, (c + 1) * nb)
            h_re, h_im = a_re * h_re - a_im * h_im + s_re[rows], a_re * h_im + a_im * h_re + s_im[rows]
        return jnp.concatenate(ent_re, axis=0), jnp.concatenate(ent_im, axis=0), h_re, h_im

    def body(kk, carry):
        f_re, f_im, b_re, b_im = carry
        rf = pl.multiple_of(_scan_chunk(kk, n_lat // cps, n_ctx // cps, False) * slab, slab)
        rb = pl.multiple_of(_scan_chunk(kk, n_lat // cps, n_ctx // cps, True) * slab, slab)
        e_re, e_im, f_re, f_im = run_slab(s_ref[pl.ds(rf, slab), 0:w], s_ref[pl.ds(rf, slab), w:2 * w],
                                          a[0], a[1], f_re, f_im, range(cps))
        hs_ref[pl.ds(rf, slab), 0:w] = e_re
        hs_ref[pl.ds(rf, slab), w:2 * w] = e_im
        e_re, e_im, b_re, b_im = run_slab(s_ref[pl.ds(rb, slab), 2 * w:3 * w], s_ref[pl.ds(rb, slab), 3 * w:4 * w],
                                          a[2], a[3], b_re, b_im, range(cps - 1, -1, -1))
        hs_ref[pl.ds(rb, slab), 2 * w:3 * w] = e_re
        hs_ref[pl.ds(rb, slab), 3 * w:4 * w] = e_im
        return f_re, f_im, b_re, b_im

    lax.fori_loop(0, (n_lat + n_ctx) // cps, body, (zero, zero, zero, zero))


def _s5_out_kernel(up_ref, hs_ref, w_ref, perm_ref, y_ref):
    @pl.when(pl.program_id(1) == 0)
    def _():
        y_ref[...] = jnp.zeros_like(y_ref)

    kw = up_ref.shape[2]
    y_pair = _mm(up_ref[0], w_ref[0, :kw]) + _mm(hs_ref[0].astype(BF16), w_ref[0, kw:])
    y_hi, y_lo = _split_bf16(y_pair)
    y_ref[0] += _mm(y_hi, perm_ref[...], _NT_DIMS) + _mm(y_lo, perm_ref[...], _NT_DIMS)


def _s5_mixer(p, a_re, a_im, log_dt, b_re, b_im, c_re, c_im, n_lat_tiles):
    b, r, _ = p.shape
    q = S5_CHUNK
    n_chunks = r // q
    cols = q * S5_WIDTH
    kw = 2 * q * S5_GROUP
    w_local, w_out, aq_tab = _s5_matrices(a_re, a_im, log_dt, b_re, b_im, c_re, c_im)
    perm = _s5_perm()
    cp2 = pltpu.CompilerParams(dimension_semantics=("arbitrary", "arbitrary"), vmem_limit_bytes=48 * 2 ** 20)
    pack_rows = next(t for t in S5_PACK_ROWS if r % t == 0)
    cpt = pack_rows // q
    halves = S5_WIDTH // LANES
    cp3 = pltpu.CompilerParams(dimension_semantics=("arbitrary", "arbitrary", "arbitrary"))
    chunk_rows = pl.BlockSpec((1, cpt, q * LANES), lambda i, t, hf: (i, t, hf))
    u_big = pl.pallas_call(
        _s5_pack_kernel,
        grid=(b, r // pack_rows, halves),
        in_specs=[pl.BlockSpec((1, pack_rows, LANES), lambda i, t, hf: (i, t, COL_S5 // LANES + hf))],
        out_specs=chunk_rows,
        out_shape=jax.ShapeDtypeStruct((b, n_chunks, cols), BF16),
        compiler_params=cp3, name="s5_pack",
    )(p)
    all_chunks = pl.BlockSpec((1, n_chunks, cols), lambda i, g: (i, 0, 0))
    col_tile = lambda width: pl.BlockSpec((1, n_chunks, width), lambda i, g: (i, 0, g))
    perm_cols = pl.BlockSpec((cols, kw), lambda i, g: (0, g))
    u_pairs, s_loc = pl.pallas_call(
        _s5_local_kernel,
        grid=(b, S5_PAIRS),
        in_specs=[all_chunks, perm_cols, pl.BlockSpec((1, kw, kw), lambda i, g: (g, 0, 0))],
        out_specs=[col_tile(kw), col_tile(kw)],
        out_shape=[jax.ShapeDtypeStruct((b, n_chunks, cols), BF16), jax.ShapeDtypeStruct((b, n_chunks, cols), F32)],
        compiler_params=cp2, name="s5_local",
    )(u_big, perm, w_local)
    n_lat = n_lat_tiles * ROW_TILE // q
    hs = pl.pallas_call(
        functools.partial(_s5_scan_kernel, n_lat=n_lat, n_ctx=n_chunks - n_lat, nb=1),
        grid=(b, S5_PAIRS),
        in_specs=[col_tile(kw), pl.BlockSpec((1, 1, kw), lambda i, g: (g, 0, 0))],
        out_specs=col_tile(kw),
        out_shape=jax.ShapeDtypeStruct((b, n_chunks, cols), F32),
        compiler_params=cp2, name="s5_scan",
    )(s_loc, aq_tab)
    y_big = pl.pallas_call(
        _s5_out_kernel,
        grid=(b, S5_PAIRS),
        in_specs=[col_tile(kw), col_tile(kw), pl.BlockSpec((1, 2 * kw, kw), lambda i, g: (g, 0, 0)), perm_cols],
        out_specs=all_chunks,
        out_shape=jax.ShapeDtypeStruct((b, n_chunks, cols), F32),
        compiler_params=cp2, name="s5_out",
    )(u_pairs, hs, w_out, perm)
    return pl.pallas_call(
        _s5_unpack_kernel,
        grid=(b, r // pack_rows, halves),
        in_specs=[chunk_rows],
        out_specs=pl.BlockSpec((1, pack_rows, LANES), lambda i, t, hf: (i, t, hf)),
        out_shape=jax.ShapeDtypeStruct((b, r, S5_WIDTH), F32),
        compiler_params=cp3, name="s5_unpack",
    )(y_big)


def _post_kernel(h_ref, xs_ref, z_ref, r_ref, u_ref, ssd_ref, ssd_b_ref, gla_ref, gla_b_ref, mla_ref, s5_ref,
                 ssd_d_ref, ssd_g_ref, gla_g_ref, s5_d_ref, glu_w_ref, glu_b_ref, w_out_ref, mod_ref, o_ref):
    y = ssd_ref[0] + ssd_b_ref[0] + ssd_d_ref[...] * xs_ref[0]
    ssd = _rms(y * _silu(z_ref[0]), ssd_g_ref[...])
    o = gla_ref[0] + gla_b_ref[0]
    lane_head = lax.broadcasted_iota(jnp.int32, (1, GLA_WIDTH), 1) >> (GLA_DV.bit_length() - 1)
    ms = jnp.zeros_like(o)
    for h in range(GLA_HEADS):
        oh = o[:, h * GLA_DV:(h + 1) * GLA_DV]
        ms = jnp.where(lane_head == h, jnp.mean(oh * oh, axis=-1, keepdims=True), ms)
    gla = o * lax.rsqrt(ms + NORM_EPS) * gla_g_ref[...] * _silu(r_ref[0])
    y5 = _gelu_erf(s5_ref[0] + s5_d_ref[...] * u_ref[0])
    s5 = y5 * jax.nn.sigmoid(_mm(y5.astype(BF16), glu_w_ref[...]) + glu_b_ref[...])
    mix_in = jnp.concatenate([ssd, gla, mla_ref[0], s5], axis=1).astype(BF16)
    o_ref[0] = h_ref[0] + mod_ref[0] * _mm(mix_in, w_out_ref[...])


def _post(h, p, ssd_xbc, ssd_y, ssd_yb, gla_o, gla_ob, mla_y, s5_y, ssd_d, ssd_norm_g, gla_norm_g, s5_d, glu_w, glu_b, w_out, mod,
          row_off, mla_off):
    b, rows, d = h.shape
    w = GROUP_WIDTH
    pblk = lambda col: pl.BlockSpec((1, ROW_TILE, w), lambda i, t: (i, row_off + t, col // w))
    yblk = pl.BlockSpec((1, ROW_TILE, w), lambda i, t: (i, row_off + t, 0))
    full = lambda *shape: pl.BlockSpec(shape, lambda i, t: (0,) * len(shape))
    vec = lambda x: x.reshape(1, -1).astype(F32)
    n_mod = mod.shape[0]
    return pl.pallas_call(
        _post_kernel,
        grid=(b, rows // ROW_TILE),
        in_specs=[pl.BlockSpec((1, ROW_TILE, d), lambda i, t: (i, t, 0)),
                  yblk, pblk(COL_Z), pblk(COL_GLA_R), pblk(COL_S5), yblk, yblk, yblk, yblk,
                  pl.BlockSpec((1, ROW_TILE, w), lambda i, t: (i, mla_off + t, 0)), yblk,
                  full(1, w), full(1, w), full(1, w), full(1, w), full(w, w), full(1, w), full(d, d),
                  pl.BlockSpec((1, 1, d), lambda i, t: (jnp.minimum(i, n_mod - 1), 0, 0))],
        out_specs=pl.BlockSpec((1, ROW_TILE, d), lambda i, t: (i, t, 0)),
        out_shape=jax.ShapeDtypeStruct((b, rows, d), F32),
        compiler_params=pltpu.CompilerParams(dimension_semantics=("arbitrary", "arbitrary")),
        name="mix_post",
    )(h, ssd_xbc, p, p, p, ssd_y, ssd_yb, gla_o, gla_ob, mla_y, s5_y,
      vec(jnp.repeat(ssd_d, SSD_HEAD_DIM)), vec(ssd_norm_g), vec(jnp.tile(gla_norm_g, GLA_HEADS)), vec(s5_d),
      glu_w.astype(BF16), vec(glu_b), w_out.astype(BF16), mod)


PEER_ROUTE_TOKENS = 256
PEER_GATE_TOKENS = 256
PEER_GATE_UNROLL = 16
PEER_GATE_SUBLANES = 8
PEER_DENSE_TOKENS = 1024
PEER_DENSE_EXPERTS = 2 * PEER_GATE_SUBLANES * PEER_KEYS
PEER_SLOTS = PEER_HEADS * PEER_TOPK


def _topk_rows(s, k):
    n_rows = s.shape[0]
    rows = lax.broadcasted_iota(jnp.int32, s.shape, 0)
    vals, idxs = [], []
    for _ in range(k):
        m = jnp.max(s, axis=0, keepdims=True)
        idx = jnp.min(jnp.where(s == m, rows, n_rows), axis=0, keepdims=True)
        vals.append(m)
        idxs.append(idx)
        s = jnp.where(rows == idx, -jnp.inf, s)
    return jnp.concatenate(vals, axis=0), jnp.concatenate(idxs, axis=0)


def _select_rows(pos, table):
    out = jnp.zeros(pos.shape, table.dtype)
    for r in range(table.shape[0]):
        out = jnp.where(pos == r, table[r:r + 1, :], out)
    return out


def _peer_route_kernel(h_ref, g_ref, shift_ref, scale_ref, wq_hi_ref, wq_lo_ref, k_hi_ref, k_lo_ref,
                       xn_ref, i1_ref, i2_ref, gate_ref, q_scr, slot_scr):
    xn = _modulated_norm(h_ref[...], g_ref[...], shift_ref[0], scale_ref[0])
    xn_ref[...] = xn.astype(BF16)
    x_hi, x_lo = _split_bf16(xn)
    q_scr[...] = _dot3(wq_hi_ref[...], wq_lo_ref[...], x_hi, x_lo, _NT_DIMS)
    half = PEER_DQ // 2

    def head_body(h, carry):
        base = pl.multiple_of(h * PEER_DQ, PEER_DQ)
        tops = []
        for j in range(2):
            qq = q_scr[pl.ds(base + j * half, half), :]
            q_hi, q_lo = _split_bf16(qq)
            s = _dot3(k_hi_ref[j, h], k_lo_ref[j, h], q_hi, q_lo, _NN_DIMS)
            tops.append(_topk_rows(s, PEER_TOPK))
        (v1, i1), (v2, i2) = tops
        pieces = [v1[a:a + 1, :] + v2[:PEER_TOPK // (a + 1), :] for a in range(PEER_TOPK)]
        n_cand = sum(PEER_TOPK // (a + 1) for a in range(PEER_TOPK))
        pad = -n_cand % 8
        cand = jnp.concatenate(pieces + [jnp.full((pad, v1.shape[1]), -jnp.inf, F32)], axis=0)
        best, pos = _topk_rows(cand, PEER_TOPK)
        e = jnp.exp(best - best[0:1, :])
        gates = e / jnp.sum(e, axis=0, keepdims=True)
        a_idx = jnp.zeros_like(pos)
        start = jnp.zeros_like(pos)
        first = 0
        for a in range(1, PEER_TOPK):
            width = PEER_TOPK // a
            first += width
            reached = pos >= first
            a_idx = a_idx + jnp.where(reached, 1, 0)
            start = start + jnp.where(reached, width, 0)
        row0 = pl.multiple_of(h * PEER_TOPK, PEER_TOPK)
        slot_scr[0, pl.ds(row0, PEER_TOPK), :] = _select_rows(a_idx, i1).astype(F32)
        slot_scr[1, pl.ds(row0, PEER_TOPK), :] = _select_rows(pos - start, i2).astype(F32)
        slot_scr[2, pl.ds(row0, PEER_TOPK), :] = gates
        return carry

    lax.fori_loop(0, PEER_HEADS, head_body, 0)
    i1_ref[...] = slot_scr[0].T.astype(jnp.int32)
    i2_ref[...] = slot_scr[1].T.astype(jnp.int32)
    gate_ref[...] = slot_scr[2].T


def _bf16_bits(x):
    return pltpu.bitcast(x.astype(BF16).astype(F32), jnp.uint32)


def _peer_gate_kernel(i1_ref, i2_ref, gate_ref, g_ref):
    rows = lax.broadcasted_iota(jnp.int32, (PEER_KEYS, PEER_SLOTS), 0)
    sub = PEER_GATE_SUBLANES

    def token_body(t, carry):
        a = i1_ref[pl.ds(t, 1), :]
        b = i2_ref[pl.ds(t, 1), :]
        w = gate_ref[pl.ds(t, 1), :]
        lhs = jnp.where(rows == a, w, 0.0).astype(BF16)
        rhs = jnp.where(rows == b, 1.0, 0.0).astype(BF16)
        gt = _mm(lhs, rhs, _NT_DIMS)
        row0 = pl.multiple_of(t * sub, sub)
        for g in range(PEER_KEYS // (2 * sub)):
            lo = gt[2 * sub * g:2 * sub * g + sub]
            hi = gt[2 * sub * g + sub:2 * sub * (g + 1)]
            g_ref[g, pl.ds(row0, sub), :] = (_bf16_bits(lo) >> 16) | _bf16_bits(hi)
        return carry

    lax.fori_loop(0, i1_ref.shape[0], token_body, 0, unroll=PEER_GATE_UNROLL)


def _peer_dense_kernel(xn_ref, u_ref, v_ref, gpk_ref, h_ref, mod_ref, o_ref):
    acc_ref = o_ref
    j = pl.program_id(1)

    @pl.when(j == 0)
    def _():
        acc_ref[...] = jnp.zeros_like(acc_ref)

    sub = PEER_GATE_SUBLANES
    xn = xn_ref[...]
    tokens = xn.shape[0]
    words = [gpk_ref[0, pl.ds(r, tokens, stride=sub), :] for r in range(sub)]
    for half in range(2):
        rows = slice(half * sub * PEER_KEYS, (half + 1) * sub * PEER_KEYS)
        hid = _gelu_erf(_mm(xn, u_ref[rows, :], _NT_DIMS))
        ys = []
        for r in range(sub):
            bits = (words[r] << 16) if half == 0 else (words[r] & jnp.uint32(0xFFFF0000))
            ys.append(pltpu.bitcast(bits, F32) * hid[:, r * PEER_KEYS:(r + 1) * PEER_KEYS])
        y = jnp.concatenate(ys, axis=1).astype(BF16)
        acc_ref[...] += _mm(y, v_ref[rows, :])

    @pl.when(j == pl.num_programs(1) - 1)
    def _():
        o_ref[...] = h_ref[...] + mod_ref[0] * acc_ref[...]


def _peer_layer(h, norm_g, shift, scale, gate_mod, wq_t_hi, wq_t_lo, keys_hi, keys_lo, u_bf, v_bf):
    n, d = h.shape
    nb = shift.shape[0]
    rows_per_batch = n // nb
    tr = min(PEER_ROUTE_TOKENS, rows_per_batch)
    full = lambda *shape: pl.BlockSpec(shape, lambda i: (0,) * len(shape))
    per_batch = lambda t: pl.BlockSpec((1, 1, d), lambda i: (i * t // rows_per_batch, 0, 0))
    xn, i1, i2, gate = pl.pallas_call(
        _peer_route_kernel,
        grid=(n // tr,),
        in_specs=[pl.BlockSpec((tr, d), lambda i: (i, 0)), full(1, d), per_batch(tr), per_batch(tr),
                  full(PEER_HEADS * PEER_DQ, d), full(PEER_HEADS * PEER_DQ, d),
                  full(2, PEER_HEADS, PEER_KEYS, PEER_DQ // 2), full(2, PEER_HEADS, PEER_KEYS, PEER_DQ // 2)],
        out_specs=[pl.BlockSpec((tr, d), lambda i: (i, 0))] + [pl.BlockSpec((tr, PEER_SLOTS), lambda i: (i, 0))] * 3,
        out_shape=[jax.ShapeDtypeStruct((n, d), BF16),
                   jax.ShapeDtypeStruct((n, PEER_SLOTS), jnp.int32),
                   jax.ShapeDtypeStruct((n, PEER_SLOTS), jnp.int32),
                   jax.ShapeDtypeStruct((n, PEER_SLOTS), F32)],
        scratch_shapes=[pltpu.VMEM((PEER_HEADS * PEER_DQ, tr), F32), pltpu.VMEM((3, PEER_SLOTS, tr), F32)],
        compiler_params=pltpu.CompilerParams(dimension_semantics=("arbitrary",)),
        name="peer_route",
    )(h, norm_g.reshape(1, d), shift, scale, wq_t_hi, wq_t_lo, keys_hi, keys_lo)

    tg = min(PEER_GATE_TOKENS, n)
    n_planes = PEER_KEYS // (2 * PEER_GATE_SUBLANES)
    slot_spec = pl.BlockSpec((tg, PEER_SLOTS), lambda i: (i, 0))
    gmat = pl.pallas_call(
        _peer_gate_kernel,
        grid=(n // tg,),
        in_specs=[slot_spec, slot_spec, slot_spec],
        out_specs=pl.BlockSpec((n_planes, tg * PEER_GATE_SUBLANES, PEER_KEYS), lambda i: (0, i, 0)),
        out_shape=jax.ShapeDtypeStruct((n_planes, n * PEER_GATE_SUBLANES, PEER_KEYS), jnp.uint32),
        compiler_params=pltpu.CompilerParams(dimension_semantics=("arbitrary",)),
        name="peer_gate",
    )(i1, i2, gate)

    tm = min(PEER_DENSE_TOKENS, rows_per_batch)
    te = PEER_DENSE_EXPERTS
    return pl.pallas_call(
        _peer_dense_kernel,
        grid=(n // tm, PEER_EXPERTS // te),
        in_specs=[pl.BlockSpec((tm, d), lambda i, j: (i, 0), pipeline_mode=pl.Buffered(1)),
                  pl.BlockSpec((te, d), lambda i, j: (j, 0)),
                  pl.BlockSpec((te, d), lambda i, j: (j, 0)),
                  pl.BlockSpec((1, tm * PEER_GATE_SUBLANES, PEER_KEYS), lambda i, j: (j, i, 0)),
                  pl.BlockSpec((tm, d), lambda i, j: (i, 0), pipeline_mode=pl.Buffered(1)),
                  pl.BlockSpec((1, 1, d), lambda i, j: (i * tm // rows_per_batch, 0, 0))],
        out_specs=pl.BlockSpec((tm, d), lambda i, j: (i, 0)),
        out_shape=jax.ShapeDtypeStruct((n, d), F32),
        compiler_params=pltpu.CompilerParams(dimension_semantics=("arbitrary", "arbitrary"),
                                             vmem_limit_bytes=58 * 2 ** 20),
        name="peer_dense",
    )(xn, u_bf, v_bf, gmat, h, gate_mod)


def _final_norm_kernel(x_ref, g_ref, o_ref):
    o_ref[...] = _rms(x_ref[...], g_ref[...])


def _final_norm(h, g):
    n = h.shape[0] * h.shape[1]
    x2 = h.reshape(n, D_MODEL)
    tm = 512
    out = pl.pallas_call(
        _final_norm_kernel,
        grid=(n // tm,),
        in_specs=[pl.BlockSpec((tm, D_MODEL), lambda i: (i, 0)),
                  pl.BlockSpec((1, D_MODEL), lambda i: (0, 0))],
        out_specs=pl.BlockSpec((tm, D_MODEL), lambda i: (i, 0)),
        out_shape=jax.ShapeDtypeStruct((n, D_MODEL), F32),
        name="final_norm",
    )(x2, g.reshape(1, D_MODEL))
    return out.reshape(h.shape)


def _mix_layer(h_lat, h_ctx, mod_l, mod_c, norm_g, w_in, w_out, ssd, gla, mla, s5, ctx_out):
    b, n_lat, d = h_lat.shape
    n_lat_tiles = n_lat // ROW_TILE
    hcomb = jnp.concatenate([h_lat, h_ctx], axis=1)
    tab = lambda k: jnp.concatenate([mod_l[k], mod_c[k]], axis=0)
    p = _inproj(hcomb, norm_g, tab(0), tab(1), _pack_w_in(w_in), n_lat_tiles)
    ssd_y, ssd_yb, ssd_xbc = _ssd_mixer(p, ssd["conv_w"], ssd["conv_b"], ssd["a_log"], ssd["dt_bias"], n_lat_tiles)
    gla_o, gla_ob = _gla_mixer(p, gla["gate_w"], gla["gate_b"], n_lat_tiles)
    mla_lat, mla_ctx = _mla_mixer(p, mla["q_norm_g"], mla["w_uq"], mla["kv_norm_g"], mla["w_ukv"], n_lat_tiles, ctx_out)
    s5_y = _s5_mixer(p, s5["a_re"], s5["a_im"], s5["log_dt"], s5["b_re"], s5["b_im"], s5["c_re"], s5["c_im"],
                     n_lat_tiles)
    post = functools.partial(_post, p=p, ssd_xbc=ssd_xbc, ssd_y=ssd_y, ssd_yb=ssd_yb, gla_o=gla_o, gla_ob=gla_ob,
                             s5_y=s5_y, ssd_d=ssd["d"],
                             ssd_norm_g=ssd["norm_g"], gla_norm_g=gla["norm_g"], s5_d=s5["d"],
                             glu_w=s5["glu_w"], glu_b=s5["glu_b"], w_out=w_out)
    new_lat = post(h_lat, mla_y=mla_lat, mod=mod_l[2], row_off=0, mla_off=0)
    new_ctx = None
    if ctx_out:
        new_ctx = post(h_ctx, mla_y=mla_ctx, mod=mod_c[2], row_off=n_lat_tiles, mla_off=0)
    return new_lat, new_ctx


def kernel(x, c, ctx, c_ctx, ada_w, ada_b, norm_mix_g, norm_ffn_g, w_in, w_out,
           ssd_conv_w, ssd_conv_b, ssd_a_log, ssd_dt_bias, ssd_d, ssd_norm_g,
           gla_gate_w, gla_gate_b, gla_norm_g, mla_q_norm_g, mla_w_uq, mla_kv_norm_g, mla_w_ukv,
           s5_a_re, s5_a_im, s5_log_dt, s5_b_re, s5_b_im, s5_c_re, s5_c_im, s5_d, s5_glu_w, s5_glu_b,
           peer_w_q, peer_sub_keys, peer_u, peer_v, final_norm_g):
    h_lat, h_ctx = x, ctx
    cond_lat = jax.nn.silu(c)[:, None, :]
    cond_ctx = jax.nn.silu(c_ctx)[None, None, :]
    for i in range(DEPTH):
        ctx_out = i < DEPTH - 1
        mod_l = jnp.split(cond_lat @ ada_w[i] + ada_b[i], N_MOD, axis=-1)
        mod_c = jnp.split(cond_ctx @ ada_w[i] + ada_b[i], N_MOD, axis=-1)
        ssd = dict(conv_w=ssd_conv_w[i], conv_b=ssd_conv_b[i], a_log=ssd_a_log[i], dt_bias=ssd_dt_bias[i],
                   d=ssd_d[i], norm_g=ssd_norm_g[i])
        gla = dict(gate_w=gla_gate_w[i], gate_b=gla_gate_b[i], norm_g=gla_norm_g[i])
        mla = dict(q_norm_g=mla_q_norm_g[i], w_uq=mla_w_uq[i], kv_norm_g=mla_kv_norm_g[i], w_ukv=mla_w_ukv[i])
        s5 = dict(a_re=s5_a_re[i], a_im=s5_a_im[i], log_dt=s5_log_dt[i], b_re=s5_b_re[i], b_im=s5_b_im[i],
                  c_re=s5_c_re[i], c_im=s5_c_im[i], d=s5_d[i], glu_w=s5_glu_w[i], glu_b=s5_glu_b[i])
        h_lat, h_ctx_new = _mix_layer(h_lat, h_ctx, mod_l, mod_c, norm_mix_g[i], w_in[i], w_out[i],
                                      ssd, gla, mla, s5, ctx_out)
        wq_t_hi, wq_t_lo = _split_bf16(peer_w_q[i].T)
        keys_hi, keys_lo = _split_bf16(peer_sub_keys[i])
        u_bf, v_bf = peer_u[i].astype(BF16), peer_v[i].astype(BF16)
        peer = functools.partial(_peer_layer, norm_g=norm_ffn_g[i], wq_t_hi=wq_t_hi, wq_t_lo=wq_t_lo,
                                 keys_hi=keys_hi, keys_lo=keys_lo, u_bf=u_bf, v_bf=v_bf)
        h_lat = peer(h_lat.reshape(-1, D_MODEL), shift=mod_l[3], scale=mod_l[4],
                     gate_mod=mod_l[5]).reshape(h_lat.shape)
        if ctx_out:
            h_ctx = peer(h_ctx_new.reshape(-1, D_MODEL), shift=mod_c[3], scale=mod_c[4],
                         gate_mod=mod_c[5]).reshape(h_ctx.shape)
    return _final_norm(h_lat, final_norm_g)
```

```python
import functools
import jax
import jax.numpy as jnp
from jax import lax
import numpy as np
from jax.experimental import pallas as pl
from jax.experimental.pallas import tpu as pltpu

D_MODEL = 1024
DEPTH = 2
GRID_W = 64
NORM_EPS = 1e-6
N_MOD = 6

GROUP_WIDTH = D_MODEL // 4

SSD_WIDTH = GROUP_WIDTH
SSD_HEAD_DIM = 64
SSD_HEADS = SSD_WIDTH // SSD_HEAD_DIM
SSD_GROUPS = 2
SSD_STATE = 128
SSD_CONV = 5
SSD_CHUNK = 128
SSD_CONV_CH = SSD_WIDTH + 2 * SSD_GROUPS * SSD_STATE
SSD_IN = SSD_WIDTH + SSD_CONV_CH + 2 * SSD_HEADS

GLA_WIDTH = GROUP_WIDTH
GLA_HEADS = 4
GLA_DV = GLA_WIDTH // GLA_HEADS
GLA_DK = GLA_DV // 2
GLA_QK = GLA_HEADS * GLA_DK
GLA_GATE_RANK = 16
GLA_TAU = 16.0
GLA_CHUNK = 64
GLA_IN = 2 * GLA_QK + 2 * GLA_WIDTH + 2 * GLA_GATE_RANK

MLA_WIDTH = GROUP_WIDTH
MLA_HEADS = 4
MLA_V = MLA_WIDTH // MLA_HEADS
MLA_NOPE = 64
MLA_ROPE = 32
MLA_Q_RANK = 256
MLA_KV_RANK = 128
MLA_SCALE = (MLA_NOPE + MLA_ROPE) ** -0.5
ROPE_BASE = 10000.0
MLA_IN = MLA_Q_RANK + MLA_KV_RANK + MLA_ROPE

S5_WIDTH = GROUP_WIDTH
S5_GROUP = 16
S5_NGROUPS = S5_WIDTH // S5_GROUP
S5_STATE = 64
S5_MAX_RE = -1e-4
S5_IN = S5_WIDTH
S5_CHUNK = 16
S5_PAIRS = S5_NGROUPS // 2
S5_PACK_ROWS = (768, 256)

PEER_KEYS = 128
PEER_EXPERTS = PEER_KEYS * PEER_KEYS
PEER_HEADS = 8
PEER_TOPK = 16
PEER_DQ = 128

LANES = 128
ROW_TILE = 256
SCAN_STEP_ROWS = ROW_TILE

F32 = jnp.float32
BF16 = jnp.bfloat16

COL_XS, COL_BM, COL_CM, COL_Z = 0, 256, 512, 768
COL_GLA_V, COL_GLA_R, COL_CQ, COL_S5 = 1024, 1280, 1536, 1792
COL_GLA_Q, COL_GLA_K, COL_CKV, COL_DT, COL_GLR, COL_KR, COL_KRROT = 2048, 2176, 2304, 2432, 2560, 2688, 2816
P_COLS = 2944

_NN_DIMS = (((1,), (0,)), ((), ()))
_NT_DIMS = (((1,), (1,)), ((), ()))
_TN_DIMS = (((0,), (0,)), ((), ()))


def _mm(a, b, dims=_NN_DIMS):
    return lax.dot_general(a, b, dims, preferred_element_type=F32)


def _split_bf16(x):
    hi = x.astype(BF16)
    lo = (x - hi.astype(F32)).astype(BF16)
    return hi, lo


def _split3_bf16(x):
    p1 = x.astype(BF16)
    r1 = x - p1.astype(F32)
    p2 = r1.astype(BF16)
    p3 = (r1 - p2.astype(F32)).astype(BF16)
    return p1, p2, p3


def _dot3(a_hi, a_lo, b_hi, b_lo, dims):
    return _mm(a_hi, b_hi, dims) + _mm(a_hi, b_lo, dims) + _mm(a_lo, b_hi, dims)


def _gelu_erf(x):
    return 0.5 * x * (1.0 + lax.erf(x * (2.0 ** -0.5)))


def _silu(x):
    return x * jax.nn.sigmoid(x)


def _softplus(x):
    return jnp.maximum(x, 0.0) + jnp.log1p(jnp.exp(-jnp.abs(x)))


def _log_sigmoid(x):
    return jnp.minimum(x, 0.0) - jnp.log1p(jnp.exp(-jnp.abs(x)))


def _rms(x, g):
    return x * lax.rsqrt(jnp.mean(x * x, axis=-1, keepdims=True) + NORM_EPS) * g


def _modulated_norm(x, g, shift, scale):
    return _rms(x, g) * (1.0 + scale) + shift


def _causal_mask(n, reverse):
    ri = lax.broadcasted_iota(jnp.int32, (n, n), 0)
    ci = lax.broadcasted_iota(jnp.int32, (n, n), 1)
    return (ci >= ri) if reverse else (ci <= ri)


def _scan_chunk(s, n_lat, n_ctx, reverse):
    if reverse:
        return n_lat + n_ctx - 1 - s
    return jnp.where(s < n_ctx, n_lat + s, s - n_ctx)


def _inproj_kernel(h_ref, g_ref, shift_ref, scale_ref, w_ref, o_ref):
    xn = _modulated_norm(h_ref[0], g_ref[...], shift_ref[0], scale_ref[0])
    o_ref[0] = _mm(xn.astype(BF16), w_ref[...])


def _inproj(hcomb, norm_g, shift_tab, scale_tab, w_pad, n_lat_tiles):
    b, r, d = hcomb.shape
    mod_spec = pl.BlockSpec((1, 1, d), lambda i, t: (jnp.where(t < n_lat_tiles, i, b), 0, 0))
    return pl.pallas_call(
        _inproj_kernel,
        grid=(b, r // ROW_TILE),
        in_specs=[pl.BlockSpec((1, ROW_TILE, d), lambda i, t: (i, t, 0)),
                  pl.BlockSpec((1, d), lambda i, t: (0, 0)), mod_spec, mod_spec,
                  pl.BlockSpec((d, P_COLS), lambda i, t: (0, 0))],
        out_specs=pl.BlockSpec((1, ROW_TILE, P_COLS), lambda i, t: (i, t, 0)),
        out_shape=jax.ShapeDtypeStruct((b, r, P_COLS), F32),
        compiler_params=pltpu.CompilerParams(dimension_semantics=("arbitrary", "arbitrary"),
                                             vmem_limit_bytes=48 * 2 ** 20),
        name="inproj",
    )(hcomb, norm_g.reshape(1, d), shift_tab, scale_tab, w_pad)


def _pack_w_in(w):
    o_ssd, o_gla, o_mla, o_s5 = 0, SSD_IN, SSD_IN + GLA_IN, SSD_IN + GLA_IN + MLA_IN
    out = jnp.zeros((w.shape[0], P_COLS), F32)
    put = lambda out, col, src, width: out.at[:, col:col + width].set(w[:, src:src + width])
    out = put(out, COL_Z, o_ssd, SSD_WIDTH)
    out = put(out, COL_XS, o_ssd + SSD_WIDTH, SSD_CONV_CH)
    out = put(out, COL_DT, o_ssd + SSD_WIDTH + SSD_CONV_CH, 2 * SSD_HEADS)
    out = put(out, COL_GLA_Q, o_gla, GLA_QK)
    out = put(out, COL_GLA_K, o_gla + GLA_QK, GLA_QK)
    out = put(out, COL_GLA_V, o_gla + 2 * GLA_QK, GLA_WIDTH)
    out = put(out, COL_GLA_R, o_gla + 2 * GLA_QK + GLA_WIDTH, GLA_WIDTH)
    out = put(out, COL_GLR, o_gla + 2 * GLA_QK + 2 * GLA_WIDTH, 2 * GLA_GATE_RANK)
    out = put(out, COL_CQ, o_mla, MLA_Q_RANK)
    out = put(out, COL_CKV, o_mla + MLA_Q_RANK, MLA_KV_RANK)
    o_kr = o_mla + MLA_Q_RANK + MLA_KV_RANK
    half = MLA_ROPE // 2
    out = put(out, COL_KR + MLA_NOPE, o_kr, MLA_ROPE)
    out = out.at[:, COL_KRROT + MLA_NOPE:COL_KRROT + MLA_NOPE + half].set(-w[:, o_kr + half:o_kr + MLA_ROPE])
    out = out.at[:, COL_KRROT + MLA_NOPE + half:COL_KRROT + MLA_NOPE + MLA_ROPE].set(w[:, o_kr:o_kr + half])
    out = put(out, COL_S5, o_s5, S5_WIDTH)
    return out.astype(BF16)


def _ssd_prep_kernel(x_ref, prev_ref, next_ref, dt_ref, w_ref, b_ref, bias_ref, xbc_ref, dtc_ref, dtt_ref,
                     *, n_lat_tiles):
    t = pl.program_id(1)
    x = x_ref[0]
    halo = prev_ref.shape[1]
    prev = jnp.where(jnp.logical_and(t > 0, t < n_lat_tiles), prev_ref[0], 0.0)
    nxt = jnp.where(t < n_lat_tiles - 1, next_ref[0], 0.0)
    ext = jnp.concatenate([prev, x, nxt], axis=0)
    rows = ext.shape[0]
    left = SSD_CONV // 2
    acc = jnp.zeros_like(x) + b_ref[...]
    for k in range(SSD_CONV):
        shifted = ext if k == left else pltpu.roll(ext, (left - k) % rows, 0)
        acc = acc + w_ref[k:k + 1, :] * shifted[halo:halo + x.shape[0]]
    xbc_ref[0] = _silu(acc)
    dt = _softplus(dt_ref[0] + bias_ref[...])
    dtc_ref[0] = dt
    dtt_ref[0] = dt.T[:dtt_ref.shape[1]]


def _ssd_scan_kernel(xbc_f_ref, dtc_f_ref, dtt_f_ref, xbc_b_ref, dtc_b_ref, dtt_b_ref, ahr_ref, ahc_ref,
                     yf_ref, yb_ref, state_ref):
    @pl.when(pl.program_id(1) == 0)
    def _():
        state_ref[...] = jnp.zeros_like(state_ref)

    q = SSD_CHUNK
    n_sub = xbc_f_ref.shape[1] // q
    for direction, (xbc_ref, dtc_ref, dtt_ref, y_ref) in enumerate(
            ((xbc_f_ref, dtc_f_ref, dtt_f_ref, yf_ref), (xbc_b_ref, dtc_b_ref, dtt_b_ref, yb_ref))):
        for c in (range(n_sub) if direction == 0 else range(n_sub - 1, -1, -1)):
            rows = slice(c * q, (c + 1) * q)
            y_ref[0, rows] = _ssd_chunk(xbc_ref[0, rows], dtc_ref[0, rows], dtt_ref[0, :, rows], ahr_ref, ahc_ref,
                                        state_ref.at[direction], direction)


def _ssd_chunk(xbc, dtc, dtt, ahr_ref, ahc_ref, state_ref, direction):
    reverse = direction == 1
    q = SSD_CHUNK
    mask = _causal_mask(q, reverse)
    tri = jnp.where(mask, 1.0, 0.0).astype(BF16)
    xs, bm, cm = xbc[:, :SSD_WIDTH], xbc[:, SSD_WIDTH:SSD_WIDTH + 256], xbc[:, SSD_WIDTH + 256:]
    a_col = dtc * ahr_ref[...]
    a_row = dtt * ahc_ref[...]
    acum_col = sum(_mm(tri, part) for part in _split3_bf16(a_col))
    acum_row = sum(_mm(part, tri, _NT_DIMS) for part in _split3_bf16(a_row))
    end = 0 if reverse else q - 1
    bm_bf, cm_bf = bm.astype(BF16), cm.astype(BF16)
    ys = []
    cb = {}
    for h in range(SSD_HEADS):
        g = h // (SSD_HEADS // SSD_GROUPS)
        gs = slice(g * SSD_STATE, (g + 1) * SSD_STATE)
        if g not in cb:
            cb[g] = _mm(cm_bf[:, gs], bm_bf[:, gs], _NT_DIMS)
        ch = direction * SSD_HEADS + h
        ac = acum_col[:, ch:ch + 1]
        ar = acum_row[ch:ch + 1, :]
        decay = jnp.exp(jnp.where(mask, ac - ar, -jnp.inf))
        xd = xs[:, h * SSD_HEAD_DIM:(h + 1) * SSD_HEAD_DIM] * dtc[:, ch:ch + 1]
        y_diag = _mm((cb[g] * decay).astype(BF16), xd.astype(BF16))
        a_end = ac[end:end + 1, :]
        st_local = _mm((xd * jnp.exp(a_end - ac)).astype(BF16), bm_bf[:, gs], _TN_DIMS)
        hs = state_ref[h]
        y_off = jnp.exp(ac) * _mm(cm_bf[:, gs], hs.astype(BF16), _NT_DIMS)
        state_ref[h] = jnp.exp(a_end) * hs + st_local
        ys.append(y_diag + y_off)
    return jnp.concatenate(ys, axis=1)


def _ssd_mixer(p, conv_w, conv_b, a_log, dt_bias, n_lat_tiles):
    b, r, _ = p.shape
    nt = r // ROW_TILE
    halo = 8
    hb = ROW_TILE // halo
    w8 = jnp.zeros((8, SSD_CONV_CH), F32).at[:SSD_CONV].set(conv_w)
    bias = jnp.zeros((1, LANES), F32).at[0, :2 * SSD_HEADS].set(dt_bias.reshape(-1))
    xbc, dtc, dtt = pl.pallas_call(
        functools.partial(_ssd_prep_kernel, n_lat_tiles=n_lat_tiles),
        grid=(b, nt),
        in_specs=[pl.BlockSpec((1, ROW_TILE, SSD_CONV_CH), lambda i, t: (i, t, 0)),
                  pl.BlockSpec((1, halo, SSD_CONV_CH), lambda i, t: (i, jnp.maximum(t * hb - 1, 0), 0)),
                  pl.BlockSpec((1, halo, SSD_CONV_CH), lambda i, t: (i, jnp.minimum((t + 1) * hb, nt * hb - 1), 0)),
                  pl.BlockSpec((1, ROW_TILE, LANES), lambda i, t: (i, t, COL_DT // LANES)),
                  pl.BlockSpec((8, SSD_CONV_CH), lambda i, t: (0, 0)),
                  pl.BlockSpec((1, SSD_CONV_CH), lambda i, t: (0, 0)),
                  pl.BlockSpec((1, LANES), lambda i, t: (0, 0))],
        out_specs=[pl.BlockSpec((1, ROW_TILE, SSD_CONV_CH), lambda i, t: (i, t, 0)),
                   pl.BlockSpec((1, ROW_TILE, LANES), lambda i, t: (i, t, 0)),
                   pl.BlockSpec((1, 8, ROW_TILE), lambda i, t: (i, 0, t))],
        out_shape=[jax.ShapeDtypeStruct((b, r, SSD_CONV_CH), F32),
                   jax.ShapeDtypeStruct((b, r, LANES), F32),
                   jax.ShapeDtypeStruct((b, 8, r), F32)],
        compiler_params=pltpu.CompilerParams(dimension_semantics=("arbitrary", "arbitrary")),
        name="ssd_prep",
    )(p, p, p, p, w8, conv_b.reshape(1, -1), bias)

    a_head = -jnp.exp(a_log.astype(F32)).reshape(-1)
    ahr = jnp.zeros((1, LANES), F32).at[0, :2 * SSD_HEADS].set(a_head)
    ahc = a_head.reshape(2 * SSD_HEADS, 1)
    blk = SCAN_STEP_ROWS
    n_lat = n_lat_tiles * ROW_TILE // blk
    n_ctx = r // blk - n_lat
    in_specs, y_specs = [], []
    for reverse in (False, True):
        cidx = functools.partial(_scan_chunk, n_lat=n_lat, n_ctx=n_ctx, reverse=reverse)
        in_specs += [pl.BlockSpec((1, blk, SSD_CONV_CH), lambda i, s, cidx=cidx: (i, cidx(s), 0)),
                     pl.BlockSpec((1, blk, LANES), lambda i, s, cidx=cidx: (i, cidx(s), 0)),
                     pl.BlockSpec((1, 8, blk), lambda i, s, cidx=cidx: (i, 0, cidx(s)))]
        y_specs.append(pl.BlockSpec((1, blk, SSD_WIDTH), lambda i, s, cidx=cidx: (i, cidx(s), 0)))
    in_specs += [pl.BlockSpec((1, LANES), lambda i, s: (0, 0)), pl.BlockSpec((2 * SSD_HEADS, 1), lambda i, s: (0, 0))]
    y_f, y_b = pl.pallas_call(
        _ssd_scan_kernel,
        grid=(b, n_lat + n_ctx),
        in_specs=in_specs,
        out_specs=y_specs,
        out_shape=[jax.ShapeDtypeStruct((b, r, SSD_WIDTH), F32)] * 2,
        scratch_shapes=[pltpu.VMEM((2, SSD_HEADS, SSD_HEAD_DIM, SSD_STATE), F32)],
        compiler_params=pltpu.CompilerParams(dimension_semantics=("arbitrary", "arbitrary")),
        name="ssd_scan",
    )(xbc, dtc, dtt, xbc, dtc, dtt, ahr, ahc)
    return y_f, y_b, xbc


def _gla_scan_kernel(qf_ref, kf_ref, vf_ref, glrf_ref, qb_ref, kb_ref, vb_ref, glrb_ref, wg_ref, bias_ref,
                     of_ref, ob_ref, st_ref):
    @pl.when(pl.program_id(1) == 0)
    def _():
        st_ref[...] = jnp.zeros_like(st_ref)

    n = GLA_CHUNK
    n_sub = qf_ref.shape[1] // n
    for direction, (q_ref, k_ref, v_ref, glr_ref, o_ref) in enumerate(
            ((qf_ref, kf_ref, vf_ref, glrf_ref, of_ref), (qb_ref, kb_ref, vb_ref, glrb_ref, ob_ref))):
        for c in (range(n_sub) if direction == 0 else range(n_sub - 1, -1, -1)):
            rows = slice(c * n, (c + 1) * n)
            o_ref[0, rows] = _gla_chunk(q_ref[0, rows], k_ref[0, rows], v_ref[0, rows], glr_ref[0, rows],
                                        wg_ref.at[direction], bias_ref.at[direction], st_ref.at[direction],
                                        direction == 1)


def _gla_chunk(q, k, v, glr, wg_ref, bias_ref, st_ref, reverse):
    n = GLA_CHUNK
    mask = _causal_mask(n, reverse)
    tri = jnp.where(mask, 1.0, 0.0).astype(BF16)
    g_hi, g_lo = _split_bf16(glr)
    logits = _dot3(g_hi, g_lo, wg_ref[0], wg_ref[1], _NN_DIMS) + bias_ref[...]
    logg = _log_sigmoid(logits) * (1.0 / GLA_TAU)
    bcum = sum(_mm(tri, part) for part in _split3_bf16(logg))
    end = 0 if reverse else n - 1
    b_end = bcum[end:end + 1, :]
    qe = q * jnp.exp(bcum) * (GLA_DK ** -0.5)
    ke = (k * jnp.exp(-bcum)).astype(BF16)
    kd = k * jnp.exp(b_end - bcum)
    decay_end = jnp.exp(b_end)
    lane_head = lax.broadcasted_iota(jnp.int32, (1, GLA_QK), 1) >> (GLA_DK.bit_length() - 1)
    outs = []
    for h in range(GLA_HEADS):
        hm = lane_head == h
        qh = jnp.where(hm, qe, 0.0).astype(BF16)
        att = jnp.where(mask, _mm(qh, ke, _NT_DIMS), 0.0)
        vh = v[:, h * GLA_DV:(h + 1) * GLA_DV].astype(BF16)
        st = st_ref[h]
        o_h = _mm(att.astype(BF16), vh) + _mm(qh, st.astype(BF16), _NT_DIMS)
        local = _mm(vh, jnp.where(hm, kd, 0.0).astype(BF16), _TN_DIMS)
        st_ref[h] = st * decay_end + local
        outs.append(o_h)
    return jnp.concatenate(outs, axis=1)


def _gla_mixer(p, gate_w, gate_b, n_lat_tiles):
    b, r, _ = p.shape
    rows = SCAN_STEP_ROWS
    n_lat = n_lat_tiles * ROW_TILE // rows
    n_ctx = r // rows - n_lat
    in_specs, o_specs, wgs = [], [], []
    for direction in (0, 1):
        cidx = functools.partial(_scan_chunk, n_lat=n_lat, n_ctx=n_ctx, reverse=direction == 1)
        wg = jnp.zeros((LANES, GLA_QK), F32).at[direction * GLA_GATE_RANK:(direction + 1) * GLA_GATE_RANK].set(
            gate_w[direction])
        wgs.append(jnp.stack(_split_bf16(wg)))
        blk = lambda width, col, cidx=cidx: pl.BlockSpec((1, rows, width), lambda i, s: (i, cidx(s), col // width))
        in_specs += [blk(GLA_QK, COL_GLA_Q), blk(GLA_QK, COL_GLA_K), blk(GLA_WIDTH, COL_GLA_V), blk(LANES, COL_GLR)]
        o_specs.append(pl.BlockSpec((1, rows, GLA_WIDTH), lambda i, s, cidx=cidx: (i, cidx(s), 0)))
    in_specs += [pl.BlockSpec((2, 2, LANES, GLA_QK), lambda i, s: (0, 0, 0, 0)),
                 pl.BlockSpec((2, 1, GLA_QK), lambda i, s: (0, 0, 0))]
    return pl.pallas_call(
        _gla_scan_kernel,
        grid=(b, n_lat + n_ctx),
        in_specs=in_specs,
        out_specs=o_specs,
        out_shape=[jax.ShapeDtypeStruct((b, r, GLA_WIDTH), F32)] * 2,
        scratch_shapes=[pltpu.VMEM((2, GLA_HEADS, GLA_DV, GLA_QK), F32)],
        compiler_params=pltpu.CompilerParams(dimension_semantics=("arbitrary", "arbitrary")),
        name="gla_scan",
    )(p, p, p, p, p, p, p, p, jnp.stack(wgs), gate_b.reshape(2, 1, GLA_QK))


MLA_Q_TILE = 1024
MLA_K_TILES = (768, 256)


def _mla_prep_kernel(cq_ref, ckv_ref, kr_ref, krrot_ref, onec_ref, sinr_ref, gq_ref, gkv_ref,
                     wq_ref, wqr_ref, wk_ref, wv_ref, q_ref, k_ref, v_ref):
    qn = _rms(cq_ref[0], gq_ref[...]).astype(BF16)
    kvn = _rms(ckv_ref[0], gkv_ref[...]).astype(BF16)
    onec, sinr = onec_ref[...], sinr_ref[...]
    k_rope = kr_ref[0] * onec + krrot_ref[0] * sinr
    ones_lane = jnp.where(lax.broadcasted_iota(jnp.int32, (1, LANES), 1) == MLA_V, 1.0, 0.0)
    for h in range(MLA_HEADS):
        qh = _mm(qn, wq_ref[h]) * onec + _mm(qn, wqr_ref[h]) * sinr
        q_ref[0, h] = (qh * MLA_SCALE).astype(BF16)
        k_ref[0, h] = (_mm(kvn, wk_ref[h]) + k_rope).astype(BF16)
        v_ref[0, h] = (_mm(kvn, wv_ref[h]) + ones_lane).astype(BF16)


def _mla_attn_kernel(q_ref, k_ref, v_ref, o_ref, m_ref, acc_ref):
    j = pl.program_id(2)

    @pl.when(j == 0)
    def _():
        m_ref[...] = jnp.full_like(m_ref, -jnp.inf)
        acc_ref[...] = jnp.zeros_like(acc_ref)

    reps = k_ref.shape[2] // LANES
    for h in range(MLA_HEADS):
        s = _mm(q_ref[0, h], k_ref[0, h], _NT_DIMS)
        m_prev = m_ref[h]
        m_new = jnp.maximum(m_prev, jnp.max(s, axis=1, keepdims=True))
        p = jnp.exp((s - jnp.concatenate([m_new] * reps, axis=1)).astype(BF16))
        acc_ref[h] = jnp.exp(m_prev - m_new) * acc_ref[h] + _mm(p, v_ref[0, h])
        m_ref[h] = m_new

    @pl.when(j == pl.num_programs(2) - 1)
    def _():
        outs = []
        for h in range(MLA_HEADS):
            acc = acc_ref[h]
            outs.append(acc[:, :MLA_V] / acc[:, MLA_V:MLA_V + 1])
        o_ref[0] = jnp.concatenate(outs, axis=1)


def _rope_tables(n_lat, n_rows):
    rows = n_lat // GRID_W
    row = jnp.repeat(jnp.arange(rows, dtype=F32), GRID_W)
    col = jnp.tile(jnp.arange(GRID_W, dtype=F32), rows)
    half = MLA_ROPE // 2
    inv = ROPE_BASE ** (-jnp.arange(0, half, 2, dtype=F32) / half)
    ang = jnp.concatenate([row[:, None] * inv, col[:, None] * inv], axis=-1)
    cos = jnp.concatenate([jnp.cos(ang), jnp.ones((n_rows - n_lat, half), F32)], axis=0)
    sin = jnp.concatenate([jnp.sin(ang), jnp.zeros((n_rows - n_lat, half), F32)], axis=0)
    pad = jnp.zeros((n_rows, LANES - MLA_NOPE - MLA_ROPE), F32)
    onec = jnp.concatenate([jnp.ones((n_rows, MLA_NOPE), F32), cos, cos, pad], axis=1)
    sinr = jnp.concatenate([jnp.zeros((n_rows, MLA_NOPE), F32), sin, sin, pad], axis=1)
    return onec, sinr


def _mla_weights(w_uq, w_ukv):
    dqk = MLA_NOPE + MLA_ROPE
    half = MLA_ROPE // 2
    wq = w_uq.reshape(MLA_Q_RANK, MLA_HEADS, dqk).transpose(1, 0, 2)
    rot = jnp.concatenate([jnp.zeros_like(wq[..., :MLA_NOPE]), -wq[..., MLA_NOPE + half:], wq[..., MLA_NOPE:MLA_NOPE + half]],
                          axis=-1)
    padq = lambda w: jnp.pad(w, ((0, 0), (0, 0), (0, LANES - dqk))).astype(BF16)
    wkv = w_ukv.reshape(MLA_KV_RANK, MLA_HEADS, MLA_NOPE + MLA_V).transpose(1, 0, 2)
    padk = lambda w: jnp.pad(w, ((0, 0), (0, 0), (0, LANES - w.shape[-1]))).astype(BF16)
    return padq(wq), padq(rot), padk(wkv[..., :MLA_NOPE]), padk(wkv[..., MLA_NOPE:])


def _mla_attention(q, k, v, q_tile, q_off, n_q, kt, k_off, n_k):
    b = q.shape[0]
    return pl.pallas_call(
        _mla_attn_kernel,
        grid=(b, n_q, n_k),
        in_specs=[pl.BlockSpec((1, MLA_HEADS, q_tile, LANES), lambda i, a, j: (i, 0, q_off + a, 0)),
                  pl.BlockSpec((1, MLA_HEADS, kt, LANES), lambda i, a, j: (i, 0, k_off + j, 0)),
                  pl.BlockSpec((1, MLA_HEADS, kt, LANES), lambda i, a, j: (i, 0, k_off + j, 0))],
        out_specs=pl.BlockSpec((1, q_tile, MLA_WIDTH), lambda i, a, j: (i, a, 0)),
        out_shape=jax.ShapeDtypeStruct((b, n_q * q_tile, MLA_WIDTH), F32),
        scratch_shapes=[pltpu.VMEM((MLA_HEADS, q_tile, LANES), F32), pltpu.VMEM((MLA_HEADS, q_tile, LANES), F32)],
        compiler_params=pltpu.CompilerParams(dimension_semantics=("arbitrary", "arbitrary", "arbitrary")),
        name="mla_attn",
    )(q, k, v)


def _mla_mixer(p, q_norm_g, w_uq, kv_norm_g, w_ukv, n_lat_tiles, ctx_out):
    b, r, _ = p.shape
    nt = r // ROW_TILE
    n_lat = n_lat_tiles * ROW_TILE
    onec, sinr = _rope_tables(n_lat, r)
    wq, wqr, wk, wv = _mla_weights(w_uq, w_ukv)
    blk = lambda width, col: pl.BlockSpec((1, ROW_TILE, width), lambda i, t: (i, t, col // width))
    tab = pl.BlockSpec((ROW_TILE, LANES), lambda i, t: (t, 0))
    full = lambda *shape: pl.BlockSpec(shape, lambda i, t: (0,) * len(shape))
    head_out = pl.BlockSpec((1, MLA_HEADS, ROW_TILE, LANES), lambda i, t: (i, 0, t, 0))
    q, k, v = pl.pallas_call(
        _mla_prep_kernel,
        grid=(b, nt),
        in_specs=[blk(MLA_Q_RANK, COL_CQ), blk(LANES, COL_CKV), blk(LANES, COL_KR), blk(LANES, COL_KRROT), tab, tab,
                  full(1, MLA_Q_RANK), full(1, MLA_KV_RANK),
                  full(MLA_HEADS, MLA_Q_RANK, LANES), full(MLA_HEADS, MLA_Q_RANK, LANES),
                  full(MLA_HEADS, MLA_KV_RANK, LANES), full(MLA_HEADS, MLA_KV_RANK, LANES)],
        out_specs=[head_out] * 3,
        out_shape=[jax.ShapeDtypeStruct((b, MLA_HEADS, r, LANES), BF16)] * 3,
        compiler_params=pltpu.CompilerParams(dimension_semantics=("arbitrary", "arbitrary")),
        name="mla_prep",
    )(p, p, p, p, onec, sinr, q_norm_g.reshape(1, -1), kv_norm_g.reshape(1, -1), wq, wqr, wk, wv)
    q_tile = min(MLA_Q_TILE, n_lat)
    k_tile = next(t for t in MLA_K_TILES if r % t == 0)
    y_lat = _mla_attention(q, k, v, q_tile, 0, n_lat // q_tile, k_tile, 0, r // k_tile)
    y_ctx = None
    if ctx_out:
        n_ctx = r - n_lat
        y_ctx = _mla_attention(q, k, v, n_ctx, n_lat // n_ctx, 1, n_ctx, n_lat // n_ctx, 1)
    return y_lat, y_ctx


def _s5_matrices(a_re, a_im, log_dt, b_re, b_im, c_re, c_im):
    q, ng, ns, nc = S5_CHUNK, S5_NGROUPS, S5_STATE, S5_GROUP
    lam = jnp.minimum(a_re.astype(F32), S5_MAX_RE) + 1j * a_im.astype(F32)
    step = jnp.exp(log_dt.astype(F32))[..., None]
    abar = jnp.exp(lam * step)
    bmat = b_re.astype(F32) + 1j * b_im.astype(F32)
    bbar = ((abar - 1.0) / lam)[..., None] * bmat
    cmat = c_re.astype(F32) + 1j * c_im.astype(F32)
    pw = jnp.exp((lam * step)[..., None] * jnp.arange(q + 1, dtype=F32))
    kern = jnp.einsum('dgcn,dgnl,dgnk->dglck', cmat, pw[..., :q], bbar).real
    ii = jnp.arange(q)
    lag_f = ii[None, :] - ii[:, None]
    gather = lambda kd, lag: jnp.where((lag >= 0)[None, :, :, None, None], kd[:, jnp.clip(lag, 0, q - 1)], 0.0)
    t_f = gather(kern[0], lag_f).transpose(0, 1, 4, 2, 3)
    t_b = gather(kern[1], -lag_f).transpose(0, 1, 4, 2, 3)
    t_sum = (t_f + t_b).reshape(ng, q * nc, q * nc)
    pw_f = pw[0][..., q - 1 - ii]
    pw_b = pw[1][..., ii]
    wst = lambda pwd, bb: jnp.einsum('gnj,gnc->gjcn', pwd, bb).reshape(ng, q * nc, ns)
    wst_f, wst_b = wst(pw_f, bbar[0]), wst(pw_b, bbar[1])
    wout = lambda pwd, cm: jnp.einsum('gcn,gni->gnic', cm, pwd).reshape(ng, ns, q * nc)
    wo_f, wo_b = wout(pw[0][..., ii + 1], cmat[0]), wout(pw[1][..., q - ii], cmat[1])
    aq = pw[..., q]

    def pair_cols(x):
        x = x.reshape(S5_PAIRS, 2, x.shape[1], x.shape[2])
        z = jnp.zeros_like(x[:, 0])
        return jnp.concatenate([jnp.concatenate([x[:, 0], z], axis=2), jnp.concatenate([z, x[:, 1]], axis=2)], axis=1)

    w_local = jnp.concatenate([pair_cols(wst_f.real), pair_cols(wst_f.imag),
                               pair_cols(wst_b.real), pair_cols(wst_b.imag)], axis=2)
    w_out = jnp.concatenate([pair_cols(t_sum), pair_cols(wo_f.real), pair_cols(-wo_f.imag),
                             pair_cols(wo_b.real), pair_cols(-wo_b.imag)], axis=1)
    aq_pair = aq.reshape(2, S5_PAIRS, 2 * ns)
    aq_tab = jnp.concatenate([aq_pair[0].real, aq_pair[0].imag, aq_pair[1].real, aq_pair[1].imag], axis=1)
    return w_local.astype(BF16), w_out.astype(BF16), aq_tab.reshape(S5_PAIRS, 1, 8 * ns).astype(F32)


def _s5_perm():
    cols = S5_CHUNK * S5_WIDTH
    c = jnp.arange(cols, dtype=jnp.int32)
    cc, j = c % S5_GROUP, (c // S5_GROUP) % S5_CHUNK
    g = c // (S5_GROUP * S5_CHUNK)
    per_half = LANES // S5_GROUP
    src = (g // per_half) * (S5_CHUNK * LANES) + j * LANES + (g % per_half) * S5_GROUP + cc
    return jnp.where(c[:, None] == src[None, :], 1.0, 0.0).astype(BF16)


def _s5_pack_kernel(u_ref, o_ref):
    n = o_ref.shape[1]
    for j in range(S5_CHUNK):
        o_ref[0, :, j * LANES:(j + 1) * LANES] = u_ref[0, pl.ds(j, n, stride=S5_CHUNK), :].astype(BF16)


def _s5_unpack_kernel(y_ref, o_ref):
    n = y_ref.shape[1]
    for i in range(S5_CHUNK):
        o_ref[0, pl.ds(i, n, stride=S5_CHUNK), :] = y_ref[0, :, i * LANES:(i + 1) * LANES]


def _s5_local_kernel(u_ref, perm_ref, w_ref, up_ref, s_ref):
    up = _mm(u_ref[0], perm_ref[...]).astype(BF16)
    up_ref[0] = up
    s_ref[0] = _mm(up, w_ref[0])


def _s5_scan_kernel(s3_ref, aq_ref, hs3_ref, *, n_lat, n_ctx, nb):
    s_ref, hs_ref = s3_ref.at[0], hs3_ref.at[0]
    w = 2 * S5_STATE
    aq = aq_ref[0]
    a = [aq[:, i * w:(i + 1) * w] for i in range(4)]
    zero = jnp.zeros((nb, w), F32)
    slab = 8
    cps = slab // nb

    def run_slab(s_re, s_im, a_re, a_im, h_re, h_im, order):
        ent_re, ent_im = [None] * cps, [None] * cps
        for c in order:
            ent_re[c], ent_im[c] = h_re, h_im
            rows = slice(c * nb, (c + 1) * nb)
            h_re, h_im = a_re * h_re - a_im * h_im + s_re[rows], a_re * h_im + a_im * h_re + s_im[rows]
        return jnp.concatenate(ent_re, axis=0), jnp.concatenate(ent_im, axis=0), h_re, h_im

    def body(kk, carry):
        f_re, f_im, b_re, b_im = carry
        rf = pl.multiple_of(_scan_chunk(kk, n_lat // cps, n_ctx // cps, False) * slab, slab)
        rb = pl.multiple_of(_scan_chunk(kk, n_lat // cps, n_ctx // cps, True) * slab, slab)
        e_re, e_im, f_re, f_im = run_slab(s_ref[pl.ds(rf, slab), 0:w], s_ref[pl.ds(rf, slab), w:2 * w],
                                          a[0], a[1], f_re, f_im, range(cps))
        hs_ref[pl.ds(rf, slab), 0:w] = e_re
        hs_ref[pl.ds(rf, slab), w:2 * w] = e_im
        e_re, e_im, b_re, b_im = run_slab(s_ref[pl.ds(rb, slab), 2 * w:3 * w], s_ref[pl.ds(rb, slab), 3 * w:4 * w],
                                          a[2], a[3], b_re, b_im, range(cps - 1, -1, -1))
        hs_ref[pl.ds(rb, slab), 2 * w:3 * w] = e_re
        hs_ref[pl.ds(rb, slab), 3 * w:4 * w] = e_im
        return f_re, f_im, b_re, b_im

    lax.fori_loop(0, (n_lat + n_ctx) // cps, body, (zero, zero, zero, zero))


def _s5_out_kernel(up_ref, hs_ref, w_ref, perm_ref, y_ref):
    @pl.when(pl.program_id(1) == 0)
    def _():
        y_ref[...] = jnp.zeros_like(y_ref)

    kw = up_ref.shape[2]
    y_pair = _mm(up_ref[0], w_ref[0, :kw]) + _mm(hs_ref[0].astype(BF16), w_ref[0, kw:])
    y_hi, y_lo = _split_bf16(y_pair)
    y_ref[0] += _mm(y_hi, perm_ref[...], _NT_DIMS) + _mm(y_lo, perm_ref[...], _NT_DIMS)


def _s5_mixer(p, a_re, a_im, log_dt, b_re, b_im, c_re, c_im, n_lat_tiles):
    b, r, _ = p.shape
    q = S5_CHUNK
    n_chunks = r // q
    cols = q * S5_WIDTH
    kw = 2 * q * S5_GROUP
    w_local, w_out, aq_tab = _s5_matrices(a_re, a_im, log_dt, b_re, b_im, c_re, c_im)
    perm = _s5_perm()
    cp2 = pltpu.CompilerParams(dimension_semantics=("arbitrary", "arbitrary"), vmem_limit_bytes=48 * 2 ** 20)
    pack_rows = next(t for t in S5_PACK_ROWS if r % t == 0)
    cpt = pack_rows // q
    halves = S5_WIDTH // LANES
    cp3 = pltpu.CompilerParams(dimension_semantics=("arbitrary", "arbitrary", "arbitrary"))
    chunk_rows = pl.BlockSpec((1, cpt, q * LANES), lambda i, t, hf: (i, t, hf))
    u_big = pl.pallas_call(
        _s5_pack_kernel,
        grid=(b, r // pack_rows, halves),
        in_specs=[pl.BlockSpec((1, pack_rows, LANES), lambda i, t, hf: (i, t, COL_S5 // LANES + hf))],
        out_specs=chunk_rows,
        out_shape=jax.ShapeDtypeStruct((b, n_chunks, cols), BF16),
        compiler_params=cp3, name="s5_pack",
    )(p)
    all_chunks = pl.BlockSpec((1, n_chunks, cols), lambda i, g: (i, 0, 0))
    col_tile = lambda width: pl.BlockSpec((1, n_chunks, width), lambda i, g: (i, 0, g))
    perm_cols = pl.BlockSpec((cols, kw), lambda i, g: (0, g))
    u_pairs, s_loc = pl.pallas_call(
        _s5_local_kernel,
        grid=(b, S5_PAIRS),
        in_specs=[all_chunks, perm_cols, pl.BlockSpec((1, kw, kw), lambda i, g: (g, 0, 0))],
        out_specs=[col_tile(kw), col_tile(kw)],
        out_shape=[jax.ShapeDtypeStruct((b, n_chunks, cols), BF16), jax.ShapeDtypeStruct((b, n_chunks, cols), F32)],
        compiler_params=cp2, name="s5_local",
    )(u_big, perm, w_local)
    n_lat = n_lat_tiles * ROW_TILE // q
    hs = pl.pallas_call(
        functools.partial(_s5_scan_kernel, n_lat=n_lat, n_ctx=n_chunks - n_lat, nb=1),
        grid=(b, S5_PAIRS),
        in_specs=[col_tile(kw), pl.BlockSpec((1, 1, kw), lambda i, g: (g, 0, 0))],
        out_specs=col_tile(kw),
        out_shape=jax.ShapeDtypeStruct((b, n_chunks, cols), F32),
        compiler_params=cp2, name="s5_scan",
    )(s_loc, aq_tab)
    y_big = pl.pallas_call(
        _s5_out_kernel,
        grid=(b, S5_PAIRS),
        in_specs=[col_tile(kw), col_tile(kw), pl.BlockSpec((1, 2 * kw, kw), lambda i, g: (g, 0, 0)), perm_cols],
        out_specs=all_chunks,
        out_shape=jax.ShapeDtypeStruct((b, n_chunks, cols), F32),
        compiler_params=cp2, name="s5_out",
    )(u_pairs, hs, w_out, perm)
    return pl.pallas_call(
        _s5_unpack_kernel,
        grid=(b, r // pack_rows, halves),
        in_specs=[chunk_rows],
        out_specs=pl.BlockSpec((1, pack_rows, LANES), lambda i, t, hf: (i, t, hf)),
        out_shape=jax.ShapeDtypeStruct((b, r, S5_WIDTH), F32),
        compiler_params=cp3, name="s5_unpack",
    )(y_big)


def _post_kernel(h_ref, xs_ref, z_ref, r_ref, u_ref, ssd_ref, ssd_b_ref, gla_ref, gla_b_ref, mla_ref, s5_ref,
                 ssd_d_ref, ssd_g_ref, gla_g_ref, s5_d_ref, glu_w_ref, glu_b_ref, w_out_ref, mod_ref, o_ref):
    y = ssd_ref[0] + ssd_b_ref[0] + ssd_d_ref[...] * xs_ref[0]
    ssd = _rms(y * _silu(z_ref[0]), ssd_g_ref[...])
    o = gla_ref[0] + gla_b_ref[0]
    lane_head = lax.broadcasted_iota(jnp.int32, (1, GLA_WIDTH), 1) >> (GLA_DV.bit_length() - 1)
    ms = jnp.zeros_like(o)
    for h in range(GLA_HEADS):
        oh = o[:, h * GLA_DV:(h + 1) * GLA_DV]
        ms = jnp.where(lane_head == h, jnp.mean(oh * oh, axis=-1, keepdims=True), ms)
    gla = o * lax.rsqrt(ms + NORM_EPS) * gla_g_ref[...] * _silu(r_ref[0])
    y5 = _gelu_erf(s5_ref[0] + s5_d_ref[...] * u_ref[0])
    s5 = y5 * jax.nn.sigmoid(_mm(y5.astype(BF16), glu_w_ref[...]) + glu_b_ref[...])
    mix_in = jnp.concatenate([ssd, gla, mla_ref[0], s5], axis=1).astype(BF16)
    o_ref[0] = h_ref[0] + mod_ref[0] * _mm(mix_in, w_out_ref[...])


def _post(h, p, ssd_xbc, ssd_y, ssd_yb, gla_o, gla_ob, mla_y, s5_y, ssd_d, ssd_norm_g, gla_norm_g, s5_d, glu_w, glu_b, w_out, mod,
          row_off, mla_off):
    b, rows, d = h.shape
    w = GROUP_WIDTH
    pblk = lambda col: pl.BlockSpec((1, ROW_TILE, w), lambda i, t: (i, row_off + t, col // w))
    yblk = pl.BlockSpec((1, ROW_TILE, w), lambda i, t: (i, row_off + t, 0))
    full = lambda *shape: pl.BlockSpec(shape, lambda i, t: (0,) * len(shape))
    vec = lambda x: x.reshape(1, -1).astype(F32)
    n_mod = mod.shape[0]
    return pl.pallas_call(
        _post_kernel,
        grid=(b, rows // ROW_TILE),
        in_specs=[pl.BlockSpec((1, ROW_TILE, d), lambda i, t: (i, t, 0)),
                  yblk, pblk(COL_Z), pblk(COL_GLA_R), pblk(COL_S5), yblk, yblk, yblk, yblk,
                  pl.BlockSpec((1, ROW_TILE, w), lambda i, t: (i, mla_off + t, 0)), yblk,
                  full(1, w), full(1, w), full(1, w), full(1, w), full(w, w), full(1, w), full(d, d),
                  pl.BlockSpec((1, 1, d), lambda i, t: (jnp.minimum(i, n_mod - 1), 0, 0))],
        out_specs=pl.BlockSpec((1, ROW_TILE, d), lambda i, t: (i, t, 0)),
        out_shape=jax.ShapeDtypeStruct((b, rows, d), F32),
        compiler_params=pltpu.CompilerParams(dimension_semantics=("arbitrary", "arbitrary")),
        name="mix_post",
    )(h, ssd_xbc, p, p, p, ssd_y, ssd_yb, gla_o, gla_ob, mla_y, s5_y,
      vec(jnp.repeat(ssd_d, SSD_HEAD_DIM)), vec(ssd_norm_g), vec(jnp.tile(gla_norm_g, GLA_HEADS)), vec(s5_d),
      glu_w.astype(BF16), vec(glu_b), w_out.astype(BF16), mod)


PEER_ROUTE_TOKENS = 256
PEER_ROUTE_UNROLL = 4
PEER_GATE_TOKENS = 256
PEER_GATE_UNROLL = 16
PEER_GATE_SUBLANES = 8
PEER_DENSE_TOKENS = 512
PEER_DENSE_EXPERTS = 2 * PEER_GATE_SUBLANES * PEER_KEYS
PEER_SLOTS = PEER_HEADS * PEER_TOPK


def _topk_rows(s, k):
    n_rows = s.shape[0]
    rows = lax.broadcasted_iota(jnp.int32, s.shape, 0)
    vals, idxs = [], []
    for _ in range(k):
        m = jnp.max(s, axis=0, keepdims=True)
        idx = jnp.min(jnp.where(s == m, rows, n_rows), axis=0, keepdims=True)
        vals.append(m)
        idxs.append(idx)
        s = jnp.where(rows == idx, -jnp.inf, s)
    return jnp.concatenate(vals, axis=0), jnp.concatenate(idxs, axis=0)


def _select_rows(pos, table):
    out = jnp.zeros(pos.shape, table.dtype)
    for r in range(table.shape[0]):
        out = jnp.where(pos == r, table[r:r + 1, :], out)
    return out


def _peer_route_kernel(h_ref, g_ref, shift_ref, scale_ref, wq_hi_ref, wq_lo_ref, k_hi_ref, k_lo_ref,
                       xn_ref, i1_ref, i2_ref, gate_ref, q_scr, slot_scr):
    xn = _modulated_norm(h_ref[...], g_ref[...], shift_ref[0], scale_ref[0])
    xn_ref[...] = xn.astype(BF16)
    x_hi, x_lo = _split_bf16(xn)
    q_scr[...] = _dot3(wq_hi_ref[...], wq_lo_ref[...], x_hi, x_lo, _NT_DIMS)
    half = PEER_DQ // 2

    def head_body(h, carry):
        base = pl.multiple_of(h * PEER_DQ, PEER_DQ)
        tops = []
        for j in range(2):
            qq = q_scr[pl.ds(base + j * half, half), :]
            q_hi, q_lo = _split_bf16(qq)
            s = _dot3(k_hi_ref[j, h], k_lo_ref[j, h], q_hi, q_lo, _NN_DIMS)
            tops.append(_topk_rows(s, PEER_TOPK))
        (v1, i1), (v2, i2) = tops
        pieces = [v1[a:a + 1, :] + v2[:PEER_TOPK // (a + 1), :] for a in range(PEER_TOPK)]
        n_cand = sum(PEER_TOPK // (a + 1) for a in range(PEER_TOPK))
        pad = -n_cand % 8
        cand = jnp.concatenate(pieces + [jnp.full((pad, v1.shape[1]), -jnp.inf, F32)], axis=0)
        best, pos = _topk_rows(cand, PEER_TOPK)
        e = jnp.exp(best - best[0:1, :])
        gates = e / jnp.sum(e, axis=0, keepdims=True)
        a_idx = jnp.zeros_like(pos)
        start = jnp.zeros_like(pos)
        first = 0
        for a in range(1, PEER_TOPK):
            width = PEER_TOPK // a
            first += width
            reached = pos >= first
            a_idx = a_idx + jnp.where(reached, 1, 0)
            start = start + jnp.where(reached, width, 0)
        row0 = pl.multiple_of(h * PEER_TOPK, PEER_TOPK)
        slot_scr[0, pl.ds(row0, PEER_TOPK), :] = _select_rows(a_idx, i1).astype(F32)
        slot_scr[1, pl.ds(row0, PEER_TOPK), :] = _select_rows(pos - start, i2).astype(F32)
        slot_scr[2, pl.ds(row0, PEER_TOPK), :] = gates
        return carry

    lax.fori_loop(0, PEER_HEADS, head_body, 0, unroll=PEER_ROUTE_UNROLL)
    i1_ref[...] = slot_scr[0].T.astype(jnp.int32)
    i2_ref[...] = slot_scr[1].T.astype(jnp.int32)
    gate_ref[...] = slot_scr[2].T


def _bf16_bits(x):
    return pltpu.bitcast(x.astype(BF16).astype(F32), jnp.uint32)


def _peer_gate_kernel(i1_ref, i2_ref, gate_ref, g_ref):
    rows = lax.broadcasted_iota(jnp.int32, (PEER_KEYS, PEER_SLOTS), 0)
    sub = PEER_GATE_SUBLANES

    def token_body(t, carry):
        a = i1_ref[pl.ds(t, 1), :]
        b = i2_ref[pl.ds(t, 1), :]
        w = gate_ref[pl.ds(t, 1), :]
        lhs = jnp.where(rows == a, w, 0.0).astype(BF16)
        rhs = jnp.where(rows == b, 1.0, 0.0).astype(BF16)
        gt = _mm(lhs, rhs, _NT_DIMS)
        row0 = pl.multiple_of(t * sub, sub)
        for g in range(PEER_KEYS // (2 * sub)):
            lo = gt[2 * sub * g:2 * sub * g + sub]
            hi = gt[2 * sub * g + sub:2 * sub * (g + 1)]
            g_ref[g, pl.ds(row0, sub), :] = (_bf16_bits(lo) >> 16) | _bf16_bits(hi)
        return carry

    lax.fori_loop(0, i1_ref.shape[0], token_body, 0, unroll=PEER_GATE_UNROLL)


def _peer_dense_kernel(xn_ref, u_ref, v_ref, gpk_ref, h_ref, mod_ref, o_ref, acc_ref):
    j = pl.program_id(1)

    @pl.when(j == 0)
    def _():
        acc_ref[...] = jnp.zeros_like(acc_ref)

    sub = PEER_GATE_SUBLANES
    xn = xn_ref[...]
    tokens = xn.shape[0]
    words = [gpk_ref[0, pl.ds(r, tokens, stride=sub), :] for r in range(sub)]
    for half in range(2):
        rows = slice(half * sub * PEER_KEYS, (half + 1) * sub * PEER_KEYS)
        hid = _gelu_erf(_mm(xn, u_ref[rows, :], _NT_DIMS))
        ys = []
        for r in range(sub):
            bits = (words[r] << 16) if half == 0 else (words[r] & jnp.uint32(0xFFFF0000))
            ys.append(pltpu.bitcast(bits, F32) * hid[:, r * PEER_KEYS:(r + 1) * PEER_KEYS])
        y = jnp.concatenate(ys, axis=1).astype(BF16)
        acc_ref[...] += _mm(y, v_ref[rows, :])

    @pl.when(j == pl.num_programs(1) - 1)
    def _():
        o_ref[...] = h_ref[...] + mod_ref[0] * acc_ref[...]


def _peer_layer(h, norm_g, shift, scale, gate_mod, wq_t_hi, wq_t_lo, keys_hi, keys_lo, u_bf, v_bf):
    n, d = h.shape
    nb = shift.shape[0]
    rows_per_batch = n // nb
    tr = min(PEER_ROUTE_TOKENS, rows_per_batch)
    full = lambda *shape: pl.BlockSpec(shape, lambda i: (0,) * len(shape))
    per_batch = lambda t: pl.BlockSpec((1, 1, d), lambda i: (i * t // rows_per_batch, 0, 0))
    xn, i1, i2, gate = pl.pallas_call(
        _peer_route_kernel,
        grid=(n // tr,),
        in_specs=[pl.BlockSpec((tr, d), lambda i: (i, 0)), full(1, d), per_batch(tr), per_batch(tr),
                  full(PEER_HEADS * PEER_DQ, d), full(PEER_HEADS * PEER_DQ, d),
                  full(2, PEER_HEADS, PEER_KEYS, PEER_DQ // 2), full(2, PEER_HEADS, PEER_KEYS, PEER_DQ // 2)],
        out_specs=[pl.BlockSpec((tr, d), lambda i: (i, 0))] + [pl.BlockSpec((tr, PEER_SLOTS), lambda i: (i, 0))] * 3,
        out_shape=[jax.ShapeDtypeStruct((n, d), BF16),
                   jax.ShapeDtypeStruct((n, PEER_SLOTS), jnp.int32),
                   jax.ShapeDtypeStruct((n, PEER_SLOTS), jnp.int32),
                   jax.ShapeDtypeStruct((n, PEER_SLOTS), F32)],
        scratch_shapes=[pltpu.VMEM((PEER_HEADS * PEER_DQ, tr), F32), pltpu.VMEM((3, PEER_SLOTS, tr), F32)],
        compiler_params=pltpu.CompilerParams(dimension_semantics=("arbitrary",)),
        name="peer_route",
    )(h, norm_g.reshape(1, d), shift, scale, wq_t_hi, wq_t_lo, keys_hi, keys_lo)

    tg = min(PEER_GATE_TOKENS, n)
    n_planes = PEER_KEYS // (2 * PEER_GATE_SUBLANES)
    slot_spec = pl.BlockSpec((tg, PEER_SLOTS), lambda i: (i, 0))
    gmat = pl.pallas_call(
        _peer_gate_kernel,
        grid=(n // tg,),
        in_specs=[slot_spec, slot_spec, slot_spec],
        out_specs=pl.BlockSpec((n_planes, tg * PEER_GATE_SUBLANES, PEER_KEYS), lambda i: (0, i, 0)),
        out_shape=jax.ShapeDtypeStruct((n_planes, n * PEER_GATE_SUBLANES, PEER_KEYS), jnp.uint32),
        compiler_params=pltpu.CompilerParams(dimension_semantics=("arbitrary",)),
        name="peer_gate",
    )(i1, i2, gate)

    tm = min(PEER_DENSE_TOKENS, rows_per_batch)
    te = PEER_DENSE_EXPERTS
    return pl.pallas_call(
        _peer_dense_kernel,
        grid=(n // tm, PEER_EXPERTS // te),
        in_specs=[pl.BlockSpec((tm, d), lambda i, j: (i, 0)),
                  pl.BlockSpec((te, d), lambda i, j: (j, 0)),
                  pl.BlockSpec((te, d), lambda i, j: (j, 0)),
                  pl.BlockSpec((1, tm * PEER_GATE_SUBLANES, PEER_KEYS), lambda i, j: (j, i, 0)),
                  pl.BlockSpec((tm, d), lambda i, j: (i, 0)),
                  pl.BlockSpec((1, 1, d), lambda i, j: (i * tm // rows_per_batch, 0, 0))],
        out_specs=pl.BlockSpec((tm, d), lambda i, j: (i, 0)),
        out_shape=jax.ShapeDtypeStruct((n, d), F32),
        scratch_shapes=[pltpu.VMEM((tm, d), F32)],
        compiler_params=pltpu.CompilerParams(dimension_semantics=("arbitrary", "arbitrary"),
                                             vmem_limit_bytes=52 * 2 ** 20),
        name="peer_dense",
    )(xn, u_bf, v_bf, gmat, h, gate_mod)


def _final_norm_kernel(x_ref, g_ref, o_ref):
    o_ref[...] = _rms(x_ref[...], g_ref[...])


def _final_norm(h, g):
    n = h.shape[0] * h.shape[1]
    x2 = h.reshape(n, D_MODEL)
    tm = 512
    out = pl.pallas_call(
        _final_norm_kernel,
        grid=(n // tm,),
        in_specs=[pl.BlockSpec((tm, D_MODEL), lambda i: (i, 0)),
                  pl.BlockSpec((1, D_MODEL), lambda i: (0, 0))],
        out_specs=pl.BlockSpec((tm, D_MODEL), lambda i: (i, 0)),
        out_shape=jax.ShapeDtypeStruct((n, D_MODEL), F32),
        name="final_norm",
    )(x2, g.reshape(1, D_MODEL))
    return out.reshape(h.shape)


def _mix_layer(h_lat, h_ctx, mod_l, mod_c, norm_g, w_in, w_out, ssd, gla, mla, s5, ctx_out):
    b, n_lat, d = h_lat.shape
    n_lat_tiles = n_lat // ROW_TILE
    hcomb = jnp.concatenate([h_lat, h_ctx], axis=1)
    tab = lambda k: jnp.concatenate([mod_l[k], mod_c[k]], axis=0)
    p = _inproj(hcomb, norm_g, tab(0), tab(1), _pack_w_in(w_in), n_lat_tiles)
    ssd_y, ssd_yb, ssd_xbc = _ssd_mixer(p, ssd["conv_w"], ssd["conv_b"], ssd["a_log"], ssd["dt_bias"], n_lat_tiles)
    gla_o, gla_ob = _gla_mixer(p, gla["gate_w"], gla["gate_b"], n_lat_tiles)
    mla_lat, mla_ctx = _mla_mixer(p, mla["q_norm_g"], mla["w_uq"], mla["kv_norm_g"], mla["w_ukv"], n_lat_tiles, ctx_out)
    s5_y = _s5_mixer(p, s5["a_re"], s5["a_im"], s5["log_dt"], s5["b_re"], s5["b_im"], s5["c_re"], s5["c_im"],
                     n_lat_tiles)
    post = functools.partial(_post, p=p, ssd_xbc=ssd_xbc, ssd_y=ssd_y, ssd_yb=ssd_yb, gla_o=gla_o, gla_ob=gla_ob,
                             s5_y=s5_y, ssd_d=ssd["d"],
                             ssd_norm_g=ssd["norm_g"], gla_norm_g=gla["norm_g"], s5_d=s5["d"],
                             glu_w=s5["glu_w"], glu_b=s5["glu_b"], w_out=w_out)
    new_lat = post(h_lat, mla_y=mla_lat, mod=mod_l[2], row_off=0, mla_off=0)
    new_ctx = None
    if ctx_out:
        new_ctx = post(h_ctx, mla_y=mla_ctx, mod=mod_c[2], row_off=n_lat_tiles, mla_off=0)
    return new_lat, new_ctx


def kernel(x, c, ctx, c_ctx, ada_w, ada_b, norm_mix_g, norm_ffn_g, w_in, w_out,
           ssd_conv_w, ssd_conv_b, ssd_a_log, ssd_dt_bias, ssd_d, ssd_norm_g,
           gla_gate_w, gla_gate_b, gla_norm_g, mla_q_norm_g, mla_w_uq, mla_kv_norm_g, mla_w_ukv,
           s5_a_re, s5_a_im, s5_log_dt, s5_b_re, s5_b_im, s5_c_re, s5_c_im, s5_d, s5_glu_w, s5_glu_b,
           peer_w_q, peer_sub_keys, peer_u, peer_v, final_norm_g):
    h_lat, h_ctx = x, ctx
    cond_lat = jax.nn.silu(c)[:, None, :]
    cond_ctx = jax.nn.silu(c_ctx)[None, None, :]
    for i in range(DEPTH):
        ctx_out = i < DEPTH - 1
        mod_l = jnp.split(cond_lat @ ada_w[i] + ada_b[i], N_MOD, axis=-1)
        mod_c = jnp.split(cond_ctx @ ada_w[i] + ada_b[i], N_MOD, axis=-1)
        ssd = dict(conv_w=ssd_conv_w[i], conv_b=ssd_conv_b[i], a_log=ssd_a_log[i], dt_bias=ssd_dt_bias[i],
                   d=ssd_d[i], norm_g=ssd_norm_g[i])
        gla = dict(gate_w=gla_gate_w[i], gate_b=gla_gate_b[i], norm_g=gla_norm_g[i])
        mla = dict(q_norm_g=mla_q_norm_g[i], w_uq=mla_w_uq[i], kv_norm_g=mla_kv_norm_g[i], w_ukv=mla_w_ukv[i])
        s5 = dict(a_re=s5_a_re[i], a_im=s5_a_im[i], log_dt=s5_log_dt[i], b_re=s5_b_re[i], b_im=s5_b_im[i],
                  c_re=s5_c_re[i], c_im=s5_c_im[i], d=s5_d[i], glu_w=s5_glu_w[i], glu_b=s5_glu_b[i])
        h_lat, h_ctx_new = _mix_layer(h_lat, h_ctx, mod_l, mod_c, norm_mix_g[i], w_in[i], w_out[i],
                                      ssd, gla, mla, s5, ctx_out)
        wq_t_hi, wq_t_lo = _split_bf16(peer_w_q[i].T)
        keys_hi, keys_lo = _split_bf16(peer_sub_keys[i])
        u_bf, v_bf = peer_u[i].astype(BF16), peer_v[i].astype(BF16)
        peer = functools.partial(_peer_layer, norm_g=norm_ffn_g[i], wq_t_hi=wq_t_hi, wq_t_lo=wq_t_lo,
                                 keys_hi=keys_hi, keys_lo=keys_lo, u_bf=u_bf, v_bf=v_bf)
        h_lat = peer(h_lat.reshape(-1, D_MODEL), shift=mod_l[3], scale=mod_l[4],
                     gate_mod=mod_l[5]).reshape(h_lat.shape)
        if ctx_out:
            h_ctx = peer(h_ctx_new.reshape(-1, D_MODEL), shift=mod_c[3], scale=mod_c[4],
                         gate_mod=mod_c[5]).reshape(h_ctx.shape)
    return _final_norm(h_lat, final_norm_g)
```

```python
import functools
import jax
import jax.numpy as jnp
from jax import lax
import numpy as np
from jax.experimental import pallas as pl
from jax.experimental.pallas import tpu as pltpu

D_MODEL = 1024
DEPTH = 2
GRID_W = 64
NORM_EPS = 1e-6
N_MOD = 6

GROUP_WIDTH = D_MODEL // 4

SSD_WIDTH = GROUP_WIDTH
SSD_HEAD_DIM = 64
SSD_HEADS = SSD_WIDTH // SSD_HEAD_DIM
SSD_GROUPS = 2
SSD_STATE = 128
SSD_CONV = 5
SSD_CHUNK = 128
SSD_CONV_CH = SSD_WIDTH + 2 * SSD_GROUPS * SSD_STATE
SSD_IN = SSD_WIDTH + SSD_CONV_CH + 2 * SSD_HEADS

GLA_WIDTH = GROUP_WIDTH
GLA_HEADS = 4
GLA_DV = GLA_WIDTH // GLA_HEADS
GLA_DK = GLA_DV // 2
GLA_QK = GLA_HEADS * GLA_DK
GLA_GATE_RANK = 16
GLA_TAU = 16.0
GLA_CHUNK = 64
GLA_IN = 2 * GLA_QK + 2 * GLA_WIDTH + 2 * GLA_GATE_RANK

MLA_WIDTH = GROUP_WIDTH
MLA_HEADS = 4
MLA_V = MLA_WIDTH // MLA_HEADS
MLA_NOPE = 64
MLA_ROPE = 32
MLA_Q_RANK = 256
MLA_KV_RANK = 128
MLA_SCALE = (MLA_NOPE + MLA_ROPE) ** -0.5
ROPE_BASE = 10000.0
MLA_IN = MLA_Q_RANK + MLA_KV_RANK + MLA_ROPE

S5_WIDTH = GROUP_WIDTH
S5_GROUP = 16
S5_NGROUPS = S5_WIDTH // S5_GROUP
S5_STATE = 64
S5_MAX_RE = -1e-4
S5_IN = S5_WIDTH
S5_CHUNK = 16
S5_PAIRS = S5_NGROUPS // 2
S5_PACK_ROWS = (768, 256)

PEER_KEYS = 128
PEER_EXPERTS = PEER_KEYS * PEER_KEYS
PEER_HEADS = 8
PEER_TOPK = 16
PEER_DQ = 128

LANES = 128
ROW_TILE = 256
SCAN_STEP_ROWS = ROW_TILE

F32 = jnp.float32
BF16 = jnp.bfloat16

COL_XS, COL_BM, COL_CM, COL_Z = 0, 256, 512, 768
COL_GLA_V, COL_GLA_R, COL_CQ, COL_S5 = 1024, 1280, 1536, 1792
COL_GLA_Q, COL_GLA_K, COL_CKV, COL_DT, COL_GLR, COL_KR, COL_KRROT = 2048, 2176, 2304, 2432, 2560, 2688, 2816
P_COLS = 2944

_NN_DIMS = (((1,), (0,)), ((), ()))
_NT_DIMS = (((1,), (1,)), ((), ()))
_TN_DIMS = (((0,), (0,)), ((), ()))


def _mm(a, b, dims=_NN_DIMS):
    return lax.dot_general(a, b, dims, preferred_element_type=F32)


def _split_bf16(x):
    hi = x.astype(BF16)
    lo = (x - hi.astype(F32)).astype(BF16)
    return hi, lo


def _split3_bf16(x):
    p1 = x.astype(BF16)
    r1 = x - p1.astype(F32)
    p2 = r1.astype(BF16)
    p3 = (r1 - p2.astype(F32)).astype(BF16)
    return p1, p2, p3


def _dot3(a_hi, a_lo, b_hi, b_lo, dims):
    return _mm(a_hi, b_hi, dims) + _mm(a_hi, b_lo, dims) + _mm(a_lo, b_hi, dims)


def _gelu_erf(x):
    return 0.5 * x * (1.0 + lax.erf(x * (2.0 ** -0.5)))


def _silu(x):
    return x * jax.nn.sigmoid(x)


def _softplus(x):
    return jnp.maximum(x, 0.0) + jnp.log1p(jnp.exp(-jnp.abs(x)))


def _log_sigmoid(x):
    return jnp.minimum(x, 0.0) - jnp.log1p(jnp.exp(-jnp.abs(x)))


def _rms(x, g):
    return x * lax.rsqrt(jnp.mean(x * x, axis=-1, keepdims=True) + NORM_EPS) * g


def _modulated_norm(x, g, shift, scale):
    return _rms(x, g) * (1.0 + scale) + shift


def _causal_mask(n, reverse):
    ri = lax.broadcasted_iota(jnp.int32, (n, n), 0)
    ci = lax.broadcasted_iota(jnp.int32, (n, n), 1)
    return (ci >= ri) if reverse else (ci <= ri)


def _scan_chunk(s, n_lat, n_ctx, reverse):
    if reverse:
        return n_lat + n_ctx - 1 - s
    return jnp.where(s < n_ctx, n_lat + s, s - n_ctx)


def _inproj_kernel(lat_ref, ctx_ref, g_ref, shift_ref, scale_ref, w_ref, o_ref, *, n_lat_tiles):
    t = pl.program_id(1)
    x = jnp.where(t < n_lat_tiles, lat_ref[0], ctx_ref[0])
    xn = _modulated_norm(x, g_ref[...], shift_ref[0], scale_ref[0])
    o_ref[0] = _mm(xn.astype(BF16), w_ref[...])


def _inproj(h_lat, h_ctx, norm_g, shift_tab, scale_tab, w_pad):
    b, n_lat, d = h_lat.shape
    n_lat_tiles = n_lat // ROW_TILE
    n_tiles = n_lat_tiles + h_ctx.shape[1] // ROW_TILE
    mod_spec = pl.BlockSpec((1, 1, d), lambda i, t: (jnp.where(t < n_lat_tiles, i, b), 0, 0))
    return pl.pallas_call(
        functools.partial(_inproj_kernel, n_lat_tiles=n_lat_tiles),
        grid=(b, n_tiles),
        in_specs=[pl.BlockSpec((1, ROW_TILE, d), lambda i, t: (i, jnp.minimum(t, n_lat_tiles - 1), 0)),
                  pl.BlockSpec((1, ROW_TILE, d), lambda i, t: (i, jnp.maximum(t - n_lat_tiles, 0), 0)),
                  pl.BlockSpec((1, d), lambda i, t: (0, 0)), mod_spec, mod_spec,
                  pl.BlockSpec((d, P_COLS), lambda i, t: (0, 0))],
        out_specs=pl.BlockSpec((1, ROW_TILE, P_COLS), lambda i, t: (i, t, 0)),
        out_shape=jax.ShapeDtypeStruct((b, n_tiles * ROW_TILE, P_COLS), F32),
        compiler_params=pltpu.CompilerParams(dimension_semantics=("arbitrary", "arbitrary"),
                                             vmem_limit_bytes=48 * 2 ** 20),
        name="inproj",
    )(h_lat, h_ctx, norm_g.reshape(1, d), shift_tab, scale_tab, w_pad)


def _pack_w_in(w):
    o_ssd, o_gla, o_mla, o_s5 = 0, SSD_IN, SSD_IN + GLA_IN, SSD_IN + GLA_IN + MLA_IN
    out = jnp.zeros((w.shape[0], P_COLS), F32)
    put = lambda out, col, src, width: out.at[:, col:col + width].set(w[:, src:src + width])
    out = put(out, COL_Z, o_ssd, SSD_WIDTH)
    out = put(out, COL_XS, o_ssd + SSD_WIDTH, SSD_CONV_CH)
    out = put(out, COL_DT, o_ssd + SSD_WIDTH + SSD_CONV_CH, 2 * SSD_HEADS)
    out = put(out, COL_GLA_Q, o_gla, GLA_QK)
    out = put(out, COL_GLA_K, o_gla + GLA_QK, GLA_QK)
    out = put(out, COL_GLA_V, o_gla + 2 * GLA_QK, GLA_WIDTH)
    out = put(out, COL_GLA_R, o_gla + 2 * GLA_QK + GLA_WIDTH, GLA_WIDTH)
    out = put(out, COL_GLR, o_gla + 2 * GLA_QK + 2 * GLA_WIDTH, 2 * GLA_GATE_RANK)
    out = put(out, COL_CQ, o_mla, MLA_Q_RANK)
    out = put(out, COL_CKV, o_mla + MLA_Q_RANK, MLA_KV_RANK)
    o_kr = o_mla + MLA_Q_RANK + MLA_KV_RANK
    half = MLA_ROPE // 2
    out = put(out, COL_KR + MLA_NOPE, o_kr, MLA_ROPE)
    out = out.at[:, COL_KRROT + MLA_NOPE:COL_KRROT + MLA_NOPE + half].set(-w[:, o_kr + half:o_kr + MLA_ROPE])
    out = out.at[:, COL_KRROT + MLA_NOPE + half:COL_KRROT + MLA_NOPE + MLA_ROPE].set(w[:, o_kr:o_kr + half])
    out = put(out, COL_S5, o_s5, S5_WIDTH)
    return out.astype(BF16)


def _ssd_prep_kernel(x_ref, prev_ref, next_ref, dt_ref, w_ref, b_ref, bias_ref, xbc_ref, dtc_ref, dtt_ref,
                     *, n_lat_tiles):
    t = pl.program_id(1)
    x = x_ref[0]
    halo = prev_ref.shape[1]
    prev = jnp.where(jnp.logical_and(t > 0, t < n_lat_tiles), prev_ref[0], 0.0)
    nxt = jnp.where(t < n_lat_tiles - 1, next_ref[0], 0.0)
    ext = jnp.concatenate([prev, x, nxt], axis=0)
    rows = ext.shape[0]
    left = SSD_CONV // 2
    acc = jnp.zeros_like(x) + b_ref[...]
    for k in range(SSD_CONV):
        shifted = ext if k == left else pltpu.roll(ext, (left - k) % rows, 0)
        acc = acc + w_ref[k:k + 1, :] * shifted[halo:halo + x.shape[0]]
    xbc_ref[0] = _silu(acc)
    dt = _softplus(dt_ref[0] + bias_ref[...])
    dtc_ref[0] = dt
    dtt_ref[0] = dt.T[:dtt_ref.shape[1]]


def _ssd_scan_kernel(xbc_f_ref, dtc_f_ref, dtt_f_ref, xbc_b_ref, dtc_b_ref, dtt_b_ref, ahr_ref, ahc_ref,
                     yf_ref, yb_ref, state_ref):
    @pl.when(pl.program_id(1) == 0)
    def _():
        state_ref[...] = jnp.zeros_like(state_ref)

    q = SSD_CHUNK
    n_sub = xbc_f_ref.shape[1] // q
    for direction, (xbc_ref, dtc_ref, dtt_ref, y_ref) in enumerate(
            ((xbc_f_ref, dtc_f_ref, dtt_f_ref, yf_ref), (xbc_b_ref, dtc_b_ref, dtt_b_ref, yb_ref))):
        for c in (range(n_sub) if direction == 0 else range(n_sub - 1, -1, -1)):
            rows = slice(c * q, (c + 1) * q)
            y_ref[0, rows] = _ssd_chunk(xbc_ref[0, rows], dtc_ref[0, rows], dtt_ref[0, :, rows], ahr_ref, ahc_ref,
                                        state_ref.at[direction], direction)


def _ssd_chunk(xbc, dtc, dtt, ahr_ref, ahc_ref, state_ref, direction):
    reverse = direction == 1
    q = SSD_CHUNK
    mask = _causal_mask(q, reverse)
    tri = jnp.where(mask, 1.0, 0.0).astype(BF16)
    xs, bm, cm = xbc[:, :SSD_WIDTH], xbc[:, SSD_WIDTH:SSD_WIDTH + 256], xbc[:, SSD_WIDTH + 256:]
    a_col = dtc * ahr_ref[...]
    a_row = dtt * ahc_ref[...]
    acum_col = sum(_mm(tri, part) for part in _split3_bf16(a_col))
    acum_row = sum(_mm(part, tri, _NT_DIMS) for part in _split3_bf16(a_row))
    end = 0 if reverse else q - 1
    bm_bf, cm_bf = bm.astype(BF16), cm.astype(BF16)
    ys = []
    cb = {}
    for h in range(SSD_HEADS):
        g = h // (SSD_HEADS // SSD_GROUPS)
        gs = slice(g * SSD_STATE, (g + 1) * SSD_STATE)
        if g not in cb:
            cb[g] = _mm(cm_bf[:, gs], bm_bf[:, gs], _NT_DIMS)
        ch = direction * SSD_HEADS + h
        ac = acum_col[:, ch:ch + 1]
        ar = acum_row[ch:ch + 1, :]
        decay = jnp.exp(jnp.where(mask, ac - ar, -jnp.inf))
        xd = xs[:, h * SSD_HEAD_DIM:(h + 1) * SSD_HEAD_DIM] * dtc[:, ch:ch + 1]
        y_diag = _mm((cb[g] * decay).astype(BF16), xd.astype(BF16))
        a_end = ac[end:end + 1, :]
        st_local = _mm((xd * jnp.exp(a_end - ac)).astype(BF16), bm_bf[:, gs], _TN_DIMS)
        hs = state_ref[h]
        y_off = jnp.exp(ac) * _mm(cm_bf[:, gs], hs.astype(BF16), _NT_DIMS)
        state_ref[h] = jnp.exp(a_end) * hs + st_local
        ys.append(y_diag + y_off)
    return jnp.concatenate(ys, axis=1)


def _ssd_mixer(p, conv_w, conv_b, a_log, dt_bias, n_lat_tiles):
    b, r, _ = p.shape
    nt = r // ROW_TILE
    halo = 8
    hb = ROW_TILE // halo
    w8 = jnp.zeros((8, SSD_CONV_CH), F32).at[:SSD_CONV].set(conv_w)
    bias = jnp.zeros((1, LANES), F32).at[0, :2 * SSD_HEADS].set(dt_bias.reshape(-1))
    xbc, dtc, dtt = pl.pallas_call(
        functools.partial(_ssd_prep_kernel, n_lat_tiles=n_lat_tiles),
        grid=(b, nt),
        in_specs=[pl.BlockSpec((1, ROW_TILE, SSD_CONV_CH), lambda i, t: (i, t, 0)),
                  pl.BlockSpec((1, halo, SSD_CONV_CH), lambda i, t: (i, jnp.maximum(t * hb - 1, 0), 0)),
                  pl.BlockSpec((1, halo, SSD_CONV_CH), lambda i, t: (i, jnp.minimum((t + 1) * hb, nt * hb - 1), 0)),
                  pl.BlockSpec((1, ROW_TILE, LANES), lambda i, t: (i, t, COL_DT // LANES)),
                  pl.BlockSpec((8, SSD_CONV_CH), lambda i, t: (0, 0)),
                  pl.BlockSpec((1, SSD_CONV_CH), lambda i, t: (0, 0)),
                  pl.BlockSpec((1, LANES), lambda i, t: (0, 0))],
        out_specs=[pl.BlockSpec((1, ROW_TILE, SSD_CONV_CH), lambda i, t: (i, t, 0)),
                   pl.BlockSpec((1, ROW_TILE, LANES), lambda i, t: (i, t, 0)),
                   pl.BlockSpec((1, 8, ROW_TILE), lambda i, t: (i, 0, t))],
        out_shape=[jax.ShapeDtypeStruct((b, r, SSD_CONV_CH), F32),
                   jax.ShapeDtypeStruct((b, r, LANES), F32),
                   jax.ShapeDtypeStruct((b, 8, r), F32)],
        compiler_params=pltpu.CompilerParams(dimension_semantics=("arbitrary", "arbitrary")),
        name="ssd_prep",
    )(p, p, p, p, w8, conv_b.reshape(1, -1), bias)

    a_head = -jnp.exp(a_log.astype(F32)).reshape(-1)
    ahr = jnp.zeros((1, LANES), F32).at[0, :2 * SSD_HEADS].set(a_head)
    ahc = a_head.reshape(2 * SSD_HEADS, 1)
    blk = SCAN_STEP_ROWS
    n_lat = n_lat_tiles * ROW_TILE // blk
    n_ctx = r // blk - n_lat
    in_specs, y_specs = [], []
    for reverse in (False, True):
        cidx = functools.partial(_scan_chunk, n_lat=n_lat, n_ctx=n_ctx, reverse=reverse)
        in_specs += [pl.BlockSpec((1, blk, SSD_CONV_CH), lambda i, s, cidx=cidx: (i, cidx(s), 0)),
                     pl.BlockSpec((1, blk, LANES), lambda i, s, cidx=cidx: (i, cidx(s), 0)),
                     pl.BlockSpec((1, 8, blk), lambda i, s, cidx=cidx: (i, 0, cidx(s)))]
        y_specs.append(pl.BlockSpec((1, blk, SSD_WIDTH), lambda i, s, cidx=cidx: (i, cidx(s), 0)))
    in_specs += [pl.BlockSpec((1, LANES), lambda i, s: (0, 0)), pl.BlockSpec((2 * SSD_HEADS, 1), lambda i, s: (0, 0))]
    y_f, y_b = pl.pallas_call(
        _ssd_scan_kernel,
        grid=(b, n_lat + n_ctx),
        in_specs=in_specs,
        out_specs=y_specs,
        out_shape=[jax.ShapeDtypeStruct((b, r, SSD_WIDTH), F32)] * 2,
        scratch_shapes=[pltpu.VMEM((2, SSD_HEADS, SSD_HEAD_DIM, SSD_STATE), F32)],
        compiler_params=pltpu.CompilerParams(dimension_semantics=("arbitrary", "arbitrary")),
        name="ssd_scan",
    )(xbc, dtc, dtt, xbc, dtc, dtt, ahr, ahc)
    return y_f, y_b, xbc


def _gla_scan_kernel(qf_ref, kf_ref, vf_ref, glrf_ref, qb_ref, kb_ref, vb_ref, glrb_ref, wg_ref, bias_ref,
                     of_ref, ob_ref, st_ref):
    @pl.when(pl.program_id(1) == 0)
    def _():
        st_ref[...] = jnp.zeros_like(st_ref)

    n = GLA_CHUNK
    n_sub = qf_ref.shape[1] // n
    for direction, (q_ref, k_ref, v_ref, glr_ref, o_ref) in enumerate(
            ((qf_ref, kf_ref, vf_ref, glrf_ref, of_ref), (qb_ref, kb_ref, vb_ref, glrb_ref, ob_ref))):
        for c in (range(n_sub) if direction == 0 else range(n_sub - 1, -1, -1)):
            rows = slice(c * n, (c + 1) * n)
            o_ref[0, rows] = _gla_chunk(q_ref[0, rows], k_ref[0, rows], v_ref[0, rows], glr_ref[0, rows],
                                        wg_ref.at[direction], bias_ref.at[direction], st_ref.at[direction],
                                        direction == 1)


def _gla_chunk(q, k, v, glr, wg_ref, bias_ref, st_ref, reverse):
    n = GLA_CHUNK
    mask = _causal_mask(n, reverse)
    tri = jnp.where(mask, 1.0, 0.0).astype(BF16)
    g_hi, g_lo = _split_bf16(glr)
    logits = _dot3(g_hi, g_lo, wg_ref[0], wg_ref[1], _NN_DIMS) + bias_ref[...]
    logg = _log_sigmoid(logits) * (1.0 / GLA_TAU)
    bcum = sum(_mm(tri, part) for part in _split3_bf16(logg))
    end = 0 if reverse else n - 1
    b_end = bcum[end:end + 1, :]
    qe = q * jnp.exp(bcum) * (GLA_DK ** -0.5)
    ke = (k * jnp.exp(-bcum)).astype(BF16)
    kd = k * jnp.exp(b_end - bcum)
    decay_end = jnp.exp(b_end)
    lane_head = lax.broadcasted_iota(jnp.int32, (1, GLA_QK), 1) >> (GLA_DK.bit_length() - 1)
    outs = []
    for h in range(GLA_HEADS):
        hm = lane_head == h
        qh = jnp.where(hm, qe, 0.0).astype(BF16)
        att = jnp.where(mask, _mm(qh, ke, _NT_DIMS), 0.0)
        vh = v[:, h * GLA_DV:(h + 1) * GLA_DV].astype(BF16)
        st = st_ref[h]
        o_h = _mm(att.astype(BF16), vh) + _mm(qh, st.astype(BF16), _NT_DIMS)
        local = _mm(vh, jnp.where(hm, kd, 0.0).astype(BF16), _TN_DIMS)
        st_ref[h] = st * decay_end + local
        outs.append(o_h)
    return jnp.concatenate(outs, axis=1)


def _gla_mixer(p, gate_w, gate_b, n_lat_tiles):
    b, r, _ = p.shape
    rows = SCAN_STEP_ROWS
    n_lat = n_lat_tiles * ROW_TILE // rows
    n_ctx = r // rows - n_lat
    in_specs, o_specs, wgs = [], [], []
    for direction in (0, 1):
        cidx = functools.partial(_scan_chunk, n_lat=n_lat, n_ctx=n_ctx, reverse=direction == 1)
        wg = jnp.zeros((LANES, GLA_QK), F32).at[direction * GLA_GATE_RANK:(direction + 1) * GLA_GATE_RANK].set(
            gate_w[direction])
        wgs.append(jnp.stack(_split_bf16(wg)))
        blk = lambda width, col, cidx=cidx: pl.BlockSpec((1, rows, width), lambda i, s: (i, cidx(s), col // width))
        in_specs += [blk(GLA_QK, COL_GLA_Q), blk(GLA_QK, COL_GLA_K), blk(GLA_WIDTH, COL_GLA_V), blk(LANES, COL_GLR)]
        o_specs.append(pl.BlockSpec((1, rows, GLA_WIDTH), lambda i, s, cidx=cidx: (i, cidx(s), 0)))
    in_specs += [pl.BlockSpec((2, 2, LANES, GLA_QK), lambda i, s: (0, 0, 0, 0)),
                 pl.BlockSpec((2, 1, GLA_QK), lambda i, s: (0, 0, 0))]
    return pl.pallas_call(
        _gla_scan_kernel,
        grid=(b, n_lat + n_ctx),
        in_specs=in_specs,
        out_specs=o_specs,
        out_shape=[jax.ShapeDtypeStruct((b, r, GLA_WIDTH), F32)] * 2,
        scratch_shapes=[pltpu.VMEM((2, GLA_HEADS, GLA_DV, GLA_QK), F32)],
        compiler_params=pltpu.CompilerParams(dimension_semantics=("arbitrary", "arbitrary")),
        name="gla_scan",
    )(p, p, p, p, p, p, p, p, jnp.stack(wgs), gate_b.reshape(2, 1, GLA_QK))


MLA_Q_TILE = 1024
MLA_K_TILES = (768, 256)


def _mla_prep_kernel(cq_ref, ckv_ref, kr_ref, krrot_ref, onec_ref, sinr_ref, gq_ref, gkv_ref,
                     wq_ref, wqr_ref, wk_ref, wv_ref, q_ref, k_ref, v_ref):
    qn = _rms(cq_ref[0], gq_ref[...]).astype(BF16)
    kvn = _rms(ckv_ref[0], gkv_ref[...]).astype(BF16)
    onec, sinr = onec_ref[...], sinr_ref[...]
    k_rope = kr_ref[0] * onec + krrot_ref[0] * sinr
    ones_lane = jnp.where(lax.broadcasted_iota(jnp.int32, (1, LANES), 1) == MLA_V, 1.0, 0.0)
    for h in range(MLA_HEADS):
        qh = _mm(qn, wq_ref[h]) * onec + _mm(qn, wqr_ref[h]) * sinr
        q_ref[0, h] = (qh * MLA_SCALE).astype(BF16)
        k_ref[0, h] = (_mm(kvn, wk_ref[h]) + k_rope).astype(BF16)
        v_ref[0, h] = (_mm(kvn, wv_ref[h]) + ones_lane).astype(BF16)


def _mla_attn_kernel(q_ref, k_ref, v_ref, o_ref, m_ref, acc_ref):
    j = pl.program_id(2)

    @pl.when(j == 0)
    def _():
        m_ref[...] = jnp.full_like(m_ref, -jnp.inf)
        acc_ref[...] = jnp.zeros_like(acc_ref)

    reps = k_ref.shape[2] // LANES
    for h in range(MLA_HEADS):
        s = _mm(q_ref[0, h], k_ref[0, h], _NT_DIMS)
        m_prev = m_ref[h]
        m_new = jnp.maximum(m_prev, jnp.max(s, axis=1, keepdims=True))
        p = jnp.exp((s - jnp.concatenate([m_new] * reps, axis=1)).astype(BF16))
        acc_ref[h] = jnp.exp(m_prev - m_new) * acc_ref[h] + _mm(p, v_ref[0, h])
        m_ref[h] = m_new

    @pl.when(j == pl.num_programs(2) - 1)
    def _():
        outs = []
        for h in range(MLA_HEADS):
            acc = acc_ref[h]
            outs.append(acc[:, :MLA_V] / acc[:, MLA_V:MLA_V + 1])
        o_ref[0] = jnp.concatenate(outs, axis=1)


def _rope_tables(n_lat, n_rows):
    rows = n_lat // GRID_W
    row = jnp.repeat(jnp.arange(rows, dtype=F32), GRID_W)
    col = jnp.tile(jnp.arange(GRID_W, dtype=F32), rows)
    half = MLA_ROPE // 2
    inv = ROPE_BASE ** (-jnp.arange(0, half, 2, dtype=F32) / half)
    ang = jnp.concatenate([row[:, None] * inv, col[:, None] * inv], axis=-1)
    cos = jnp.concatenate([jnp.cos(ang), jnp.ones((n_rows - n_lat, half), F32)], axis=0)
    sin = jnp.concatenate([jnp.sin(ang), jnp.zeros((n_rows - n_lat, half), F32)], axis=0)
    pad = jnp.zeros((n_rows, LANES - MLA_NOPE - MLA_ROPE), F32)
    onec = jnp.concatenate([jnp.ones((n_rows, MLA_NOPE), F32), cos, cos, pad], axis=1)
    sinr = jnp.concatenate([jnp.zeros((n_rows, MLA_NOPE), F32), sin, sin, pad], axis=1)
    return onec, sinr


def _mla_weights(w_uq, w_ukv):
    dqk = MLA_NOPE + MLA_ROPE
    half = MLA_ROPE // 2
    wq = w_uq.reshape(MLA_Q_RANK, MLA_HEADS, dqk).transpose(1, 0, 2)
    rot = jnp.concatenate([jnp.zeros_like(wq[..., :MLA_NOPE]), -wq[..., MLA_NOPE + half:], wq[..., MLA_NOPE:MLA_NOPE + half]],
                          axis=-1)
    padq = lambda w: jnp.pad(w, ((0, 0), (0, 0), (0, LANES - dqk))).astype(BF16)
    wkv = w_ukv.reshape(MLA_KV_RANK, MLA_HEADS, MLA_NOPE + MLA_V).transpose(1, 0, 2)
    padk = lambda w: jnp.pad(w, ((0, 0), (0, 0), (0, LANES - w.shape[-1]))).astype(BF16)
    return padq(wq), padq(rot), padk(wkv[..., :MLA_NOPE]), padk(wkv[..., MLA_NOPE:])


def _mla_attention(q, k, v, q_tile, q_off, n_q, kt, k_off, n_k):
    b = q.shape[0]
    return pl.pallas_call(
        _mla_attn_kernel,
        grid=(b, n_q, n_k),
        in_specs=[pl.BlockSpec((1, MLA_HEADS, q_tile, LANES), lambda i, a, j: (i, 0, q_off + a, 0)),
                  pl.BlockSpec((1, MLA_HEADS, kt, LANES), lambda i, a, j: (i, 0, k_off + j, 0)),
                  pl.BlockSpec((1, MLA_HEADS, kt, LANES), lambda i, a, j: (i, 0, k_off + j, 0))],
        out_specs=pl.BlockSpec((1, q_tile, MLA_WIDTH), lambda i, a, j: (i, a, 0)),
        out_shape=jax.ShapeDtypeStruct((b, n_q * q_tile, MLA_WIDTH), F32),
        scratch_shapes=[pltpu.VMEM((MLA_HEADS, q_tile, LANES), F32), pltpu.VMEM((MLA_HEADS, q_tile, LANES), F32)],
        compiler_params=pltpu.CompilerParams(dimension_semantics=("arbitrary", "arbitrary", "arbitrary")),
        name="mla_attn",
    )(q, k, v)


def _mla_mixer(p, q_norm_g, w_uq, kv_norm_g, w_ukv, n_lat_tiles, ctx_out):
    b, r, _ = p.shape
    nt = r // ROW_TILE
    n_lat = n_lat_tiles * ROW_TILE
    onec, sinr = _rope_tables(n_lat, r)
    wq, wqr, wk, wv = _mla_weights(w_uq, w_ukv)
    blk = lambda width, col: pl.BlockSpec((1, ROW_TILE, width), lambda i, t: (i, t, col // width))
    tab = pl.BlockSpec((ROW_TILE, LANES), lambda i, t: (t, 0))
    full = lambda *shape: pl.BlockSpec(shape, lambda i, t: (0,) * len(shape))
    head_out = pl.BlockSpec((1, MLA_HEADS, ROW_TILE, LANES), lambda i, t: (i, 0, t, 0))
    q, k, v = pl.pallas_call(
        _mla_prep_kernel,
        grid=(b, nt),
        in_specs=[blk(MLA_Q_RANK, COL_CQ), blk(LANES, COL_CKV), blk(LANES, COL_KR), blk(LANES, COL_KRROT), tab, tab,
                  full(1, MLA_Q_RANK), full(1, MLA_KV_RANK),
                  full(MLA_HEADS, MLA_Q_RANK, LANES), full(MLA_HEADS, MLA_Q_RANK, LANES),
                  full(MLA_HEADS, MLA_KV_RANK, LANES), full(MLA_HEADS, MLA_KV_RANK, LANES)],
        out_specs=[head_out] * 3,
        out_shape=[jax.ShapeDtypeStruct((b, MLA_HEADS, r, LANES), BF16)] * 3,
        compiler_params=pltpu.CompilerParams(dimension_semantics=("arbitrary", "arbitrary")),
        name="mla_prep",
    )(p, p, p, p, onec, sinr, q_norm_g.reshape(1, -1), kv_norm_g.reshape(1, -1), wq, wqr, wk, wv)
    q_tile = min(MLA_Q_TILE, n_lat)
    k_tile = next(t for t in MLA_K_TILES if r % t == 0)
    y_lat = _mla_attention(q, k, v, q_tile, 0, n_lat // q_tile, k_tile, 0, r // k_tile)
    y_ctx = None
    if ctx_out:
        n_ctx = r - n_lat
        y_ctx = _mla_attention(q, k, v, n_ctx, n_lat // n_ctx, 1, n_ctx, n_lat // n_ctx, 1)
    return y_lat, y_ctx


def _s5_matrices(a_re, a_im, log_dt, b_re, b_im, c_re, c_im):
    q, ng, ns, nc = S5_CHUNK, S5_NGROUPS, S5_STATE, S5_GROUP
    lam = jnp.minimum(a_re.astype(F32), S5_MAX_RE) + 1j * a_im.astype(F32)
    step = jnp.exp(log_dt.astype(F32))[..., None]
    abar = jnp.exp(lam * step)
    bmat = b_re.astype(F32) + 1j * b_im.astype(F32)
    bbar = ((abar - 1.0) / lam)[..., None] * bmat
    cmat = c_re.astype(F32) + 1j * c_im.astype(F32)
    pw = jnp.exp((lam * step)[..., None] * jnp.arange(q + 1, dtype=F32))
    kern = jnp.einsum('dgcn,dgnl,dgnk->dglck', cmat, pw[..., :q], bbar).real
    ii = jnp.arange(q)
    lag_f = ii[None, :] - ii[:, None]
    gather = lambda kd, lag: jnp.where((lag >= 0)[None, :, :, None, None], kd[:, jnp.clip(lag, 0, q - 1)], 0.0)
    t_f = gather(kern[0], lag_f).transpose(0, 1, 4, 2, 3)
    t_b = gather(kern[1], -lag_f).transpose(0, 1, 4, 2, 3)
    t_sum = (t_f + t_b).reshape(ng, q * nc, q * nc)
    pw_f = pw[0][..., q - 1 - ii]
    pw_b = pw[1][..., ii]
    wst = lambda pwd, bb: jnp.einsum('gnj,gnc->gjcn', pwd, bb).reshape(ng, q * nc, ns)
    wst_f, wst_b = wst(pw_f, bbar[0]), wst(pw_b, bbar[1])
    wout = lambda pwd, cm: jnp.einsum('gcn,gni->gnic', cm, pwd).reshape(ng, ns, q * nc)
    wo_f, wo_b = wout(pw[0][..., ii + 1], cmat[0]), wout(pw[1][..., q - ii], cmat[1])
    aq = pw[..., q]

    def pair_cols(x):
        x = x.reshape(S5_PAIRS, 2, x.shape[1], x.shape[2])
        z = jnp.zeros_like(x[:, 0])
        return jnp.concatenate([jnp.concatenate([x[:, 0], z], axis=2), jnp.concatenate([z, x[:, 1]], axis=2)], axis=1)

    w_local = jnp.concatenate([pair_cols(wst_f.real), pair_cols(wst_f.imag),
                               pair_cols(wst_b.real), pair_cols(wst_b.imag)], axis=2)
    w_out = jnp.concatenate([pair_cols(t_sum), pair_cols(wo_f.real), pair_cols(-wo_f.imag),
                             pair_cols(wo_b.real), pair_cols(-wo_b.imag)], axis=1)
    aq_pair = aq.reshape(2, S5_PAIRS, 2 * ns)
    aq_tab = jnp.concatenate([aq_pair[0].real, aq_pair[0].imag, aq_pair[1].real, aq_pair[1].imag], axis=1)
    return w_local.astype(BF16), w_out.astype(BF16), aq_tab.reshape(S5_PAIRS, 1, 8 * ns).astype(F32)


def _s5_perm():
    cols = S5_CHUNK * S5_WIDTH
    c = jnp.arange(cols, dtype=jnp.int32)
    cc, j = c % S5_GROUP, (c // S5_GROUP) % S5_CHUNK
    g = c // (S5_GROUP * S5_CHUNK)
    per_half = LANES // S5_GROUP
    src = (g // per_half) * (S5_CHUNK * LANES) + j * LANES + (g % per_half) * S5_GROUP + cc
    return jnp.where(c[:, None] == src[None, :], 1.0, 0.0).astype(BF16)


def _s5_pack_kernel(u_ref, o_ref):
    n = o_ref.shape[1]
    for j in range(S5_CHUNK):
        o_ref[0, :, j * LANES:(j + 1) * LANES] = u_ref[0, pl.ds(j, n, stride=S5_CHUNK), :].astype(BF16)


def _s5_unpack_kernel(y_ref, o_ref):
    n = y_ref.shape[1]
    for i in range(S5_CHUNK):
        o_ref[0, pl.ds(i, n, stride=S5_CHUNK), :] = y_ref[0, :, i * LANES:(i + 1) * LANES]


def _s5_local_kernel(u_ref, perm_ref, w_ref, up_ref, s_ref):
    up = _mm(u_ref[0], perm_ref[...]).astype(BF16)
    up_ref[0] = up
    s_ref[0] = _mm(up, w_ref[0])


def _s5_scan_kernel(s3_ref, aq_ref, hs3_ref, *, n_lat, n_ctx, nb):
    s_ref, hs_ref = s3_ref.at[0], hs3_ref.at[0]
    w = 2 * S5_STATE
    aq = aq_ref[0]
    a = [aq[:, i * w:(i + 1) * w] for i in range(4)]
    zero = jnp.zeros((nb, w), F32)
    slab = 8
    cps = slab // nb

    def run_slab(s_re, s_im, a_re, a_im, h_re, h_im, order):
        ent_re, ent_im = [None] * cps, [None] * cps
        for c in order:
            ent_re[c], ent_im[c] = h_re, h_im
            rows = slice(c * nb, (c + 1) * nb)
            h_re, h_im = a_re * h_re - a_im * h_im + s_re[rows], a_re * h_im + a_im * h_re + s_im[rows]
        return jnp.concatenate(ent_re, axis=0), jnp.concatenate(ent_im, axis=0), h_re, h_im

    def body(kk, carry):
        f_re, f_im, b_re, b_im = carry
        rf = pl.multiple_of(_scan_chunk(kk, n_lat // cps, n_ctx // cps, False) * slab, slab)
        rb = pl.multiple_of(_scan_chunk(kk, n_lat // cps, n_ctx // cps, True) * slab, slab)
        e_re, e_im, f_re, f_im = run_slab(s_ref[pl.ds(rf, slab), 0:w], s_ref[pl.ds(rf, slab), w:2 * w],
                                          a[0], a[1], f_re, f_im, range(cps))
        hs_ref[pl.ds(rf, slab), 0:w] = e_re
        hs_ref[pl.ds(rf, slab), w:2 * w] = e_im
        e_re, e_im, b_re, b_im = run_slab(s_ref[pl.ds(rb, slab), 2 * w:3 * w], s_ref[pl.ds(rb, slab), 3 * w:4 * w],
                                          a[2], a[3], b_re, b_im, range(cps - 1, -1, -1))
        hs_ref[pl.ds(rb, slab), 2 * w:3 * w] = e_re
        hs_ref[pl.ds(rb, slab), 3 * w:4 * w] = e_im
        return f_re, f_im, b_re, b_im

    lax.fori_loop(0, (n_lat + n_ctx) // cps, body, (zero, zero, zero, zero))


def _s5_out_kernel(up_ref, hs_ref, w_ref, perm_ref, y_ref):
    @pl.when(pl.program_id(1) == 0)
    def _():
        y_ref[...] = jnp.zeros_like(y_ref)

    kw = up_ref.shape[2]
    y_pair = _mm(up_ref[0], w_ref[0, :kw]) + _mm(hs_ref[0].astype(BF16), w_ref[0, kw:])
    y_hi, y_lo = _split_bf16(y_pair)
    y_ref[0] += _mm(y_hi, perm_ref[...], _NT_DIMS) + _mm(y_lo, perm_ref[...], _NT_DIMS)


def _s5_mixer(p, a_re, a_im, log_dt, b_re, b_im, c_re, c_im, n_lat_tiles):
    b, r, _ = p.shape
    q = S5_CHUNK
    n_chunks = r // q
    cols = q * S5_WIDTH
    kw = 2 * q * S5_GROUP
    w_local, w_out, aq_tab = _s5_matrices(a_re, a_im, log_dt, b_re, b_im, c_re, c_im)
    perm = _s5_perm()
    cp2 = pltpu.CompilerParams(dimension_semantics=("arbitrary", "arbitrary"), vmem_limit_bytes=48 * 2 ** 20)
    pack_rows = next(t for t in S5_PACK_ROWS if r % t == 0)
    cpt = pack_rows // q
    halves = S5_WIDTH // LANES
    cp3 = pltpu.CompilerParams(dimension_semantics=("arbitrary", "arbitrary", "arbitrary"))
    chunk_rows = pl.BlockSpec((1, cpt, q * LANES), lambda i, t, hf: (i, t, hf))
    u_big = pl.pallas_call(
        _s5_pack_kernel,
        grid=(b, r // pack_rows, halves),
        in_specs=[pl.BlockSpec((1, pack_rows, LANES), lambda i, t, hf: (i, t, COL_S5 // LANES + hf))],
        out_specs=chunk_rows,
        out_shape=jax.ShapeDtypeStruct((b, n_chunks, cols), BF16),
        compiler_params=cp3, name="s5_pack",
    )(p)
    all_chunks = pl.BlockSpec((1, n_chunks, cols), lambda i, g: (i, 0, 0))
    col_tile = lambda width: pl.BlockSpec((1, n_chunks, width), lambda i, g: (i, 0, g))
    perm_cols = pl.BlockSpec((cols, kw), lambda i, g: (0, g))
    u_pairs, s_loc = pl.pallas_call(
        _s5_local_kernel,
        grid=(b, S5_PAIRS),
        in_specs=[all_chunks, perm_cols, pl.BlockSpec((1, kw, kw), lambda i, g: (g, 0, 0))],
        out_specs=[col_tile(kw), col_tile(kw)],
        out_shape=[jax.ShapeDtypeStruct((b, n_chunks, cols), BF16), jax.ShapeDtypeStruct((b, n_chunks, cols), F32)],
        compiler_params=cp2, name="s5_local",
    )(u_big, perm, w_local)
    n_lat = n_lat_tiles * ROW_TILE // q
    hs = pl.pallas_call(
        functools.partial(_s5_scan_kernel, n_lat=n_lat, n_ctx=n_chunks - n_lat, nb=1),
        grid=(b, S5_PAIRS),
        in_specs=[col_tile(kw), pl.BlockSpec((1, 1, kw), lambda i, g: (g, 0, 0))],
        out_specs=col_tile(kw),
        out_shape=jax.ShapeDtypeStruct((b, n_chunks, cols), F32),
        compiler_params=cp2, name="s5_scan",
    )(s_loc, aq_tab)
    y_big = pl.pallas_call(
        _s5_out_kernel,
        grid=(b, S5_PAIRS),
        in_specs=[col_tile(kw), col_tile(kw), pl.BlockSpec((1, 2 * kw, kw), lambda i, g: (g, 0, 0)), perm_cols],
        out_specs=all_chunks,
        out_shape=jax.ShapeDtypeStruct((b, n_chunks, cols), F32),
        compiler_params=cp2, name="s5_out",
    )(u_pairs, hs, w_out, perm)
    return pl.pallas_call(
        _s5_unpack_kernel,
        grid=(b, r // pack_rows, halves),
        in_specs=[chunk_rows],
        out_specs=pl.BlockSpec((1, pack_rows, LANES), lambda i, t, hf: (i, t, hf)),
        out_shape=jax.ShapeDtypeStruct((b, r, S5_WIDTH), F32),
        compiler_params=cp3, name="s5_unpack",
    )(y_big)


def _post_kernel(h_ref, xs_ref, z_ref, r_ref, u_ref, ssd_ref, ssd_b_ref, gla_ref, gla_b_ref, mla_ref, s5_ref,
                 ssd_d_ref, ssd_g_ref, gla_g_ref, s5_d_ref, glu_w_ref, glu_b_ref, w_out_ref, mod_ref, o_ref):
    y = ssd_ref[0] + ssd_b_ref[0] + ssd_d_ref[...] * xs_ref[0]
    ssd = _rms(y * _silu(z_ref[0]), ssd_g_ref[...])
    o = gla_ref[0] + gla_b_ref[0]
    lane_head = lax.broadcasted_iota(jnp.int32, (1, GLA_WIDTH), 1) >> (GLA_DV.bit_length() - 1)
    ms = jnp.zeros_like(o)
    for h in range(GLA_HEADS):
        oh = o[:, h * GLA_DV:(h + 1) * GLA_DV]
        ms = jnp.where(lane_head == h, jnp.mean(oh * oh, axis=-1, keepdims=True), ms)
    gla = o * lax.rsqrt(ms + NORM_EPS) * gla_g_ref[...] * _silu(r_ref[0])
    y5 = _gelu_erf(s5_ref[0] + s5_d_ref[...] * u_ref[0])
    s5 = y5 * jax.nn.sigmoid(_mm(y5.astype(BF16), glu_w_ref[...]) + glu_b_ref[...])
    mix_in = jnp.concatenate([ssd, gla, mla_ref[0], s5], axis=1).astype(BF16)
    o_ref[0] = h_ref[0] + mod_ref[0] * _mm(mix_in, w_out_ref[...])


def _post(h, p, ssd_xbc, ssd_y, ssd_yb, gla_o, gla_ob, mla_y, s5_y, ssd_d, ssd_norm_g, gla_norm_g, s5_d, glu_w, glu_b, w_out, mod,
          row_off, mla_off):
    b, rows, d = h.shape
    w = GROUP_WIDTH
    pblk = lambda col: pl.BlockSpec((1, ROW_TILE, w), lambda i, t: (i, row_off + t, col // w))
    yblk = pl.BlockSpec((1, ROW_TILE, w), lambda i, t: (i, row_off + t, 0))
    full = lambda *shape: pl.BlockSpec(shape, lambda i, t: (0,) * len(shape))
    vec = lambda x: x.reshape(1, -1).astype(F32)
    n_mod = mod.shape[0]
    return pl.pallas_call(
        _post_kernel,
        grid=(b, rows // ROW_TILE),
        in_specs=[pl.BlockSpec((1, ROW_TILE, d), lambda i, t: (i, t, 0)),
                  yblk, pblk(COL_Z), pblk(COL_GLA_R), pblk(COL_S5), yblk, yblk, yblk, yblk,
                  pl.BlockSpec((1, ROW_TILE, w), lambda i, t: (i, mla_off + t, 0)), yblk,
                  full(1, w), full(1, w), full(1, w), full(1, w), full(w, w), full(1, w), full(d, d),
                  pl.BlockSpec((1, 1, d), lambda i, t: (jnp.minimum(i, n_mod - 1), 0, 0))],
        out_specs=pl.BlockSpec((1, ROW_TILE, d), lambda i, t: (i, t, 0)),
        out_shape=jax.ShapeDtypeStruct((b, rows, d), F32),
        compiler_params=pltpu.CompilerParams(dimension_semantics=("arbitrary", "arbitrary")),
        name="mix_post",
    )(h, ssd_xbc, p, p, p, ssd_y, ssd_yb, gla_o, gla_ob, mla_y, s5_y,
      vec(jnp.repeat(ssd_d, SSD_HEAD_DIM)), vec(ssd_norm_g), vec(jnp.tile(gla_norm_g, GLA_HEADS)), vec(s5_d),
      glu_w.astype(BF16), vec(glu_b), w_out.astype(BF16), mod)


PEER_ROUTE_TOKENS = 256
PEER_ROUTE_UNROLL = 4
PEER_GATE_TOKENS = 256
PEER_GATE_UNROLL = 32
PEER_GATE_SUBLANES = 8
PEER_DENSE_TOKENS = 512
PEER_DENSE_EXPERTS = 2 * PEER_GATE_SUBLANES * PEER_KEYS
PEER_SLOTS = PEER_HEADS * PEER_TOPK


def _topk_rows(s, k):
    n_rows = s.shape[0]
    rows = lax.broadcasted_iota(jnp.int32, s.shape, 0)
    vals, idxs = [], []
    for _ in range(k):
        m = jnp.max(s, axis=0, keepdims=True)
        idx = jnp.min(jnp.where(s == m, rows, n_rows), axis=0, keepdims=True)
        vals.append(m)
        idxs.append(idx)
        s = jnp.where(rows == idx, -jnp.inf, s)
    return jnp.concatenate(vals, axis=0), jnp.concatenate(idxs, axis=0)


def _select_rows(pos, table):
    out = jnp.zeros(pos.shape, table.dtype)
    for r in range(table.shape[0]):
        out = jnp.where(pos == r, table[r:r + 1, :], out)
    return out


def _peer_route_kernel(h_ref, g_ref, shift_ref, scale_ref, wq_hi_ref, wq_lo_ref, k_hi_ref, k_lo_ref,
                       xn_ref, i1_ref, i2_ref, gate_ref, q_scr, slot_scr):
    xn = _modulated_norm(h_ref[...], g_ref[...], shift_ref[0], scale_ref[0])
    xn_ref[...] = xn.astype(BF16)
    x_hi, x_lo = _split_bf16(xn)
    q_scr[...] = _dot3(wq_hi_ref[...], wq_lo_ref[...], x_hi, x_lo, _NT_DIMS)
    half = PEER_DQ // 2

    def head_body(h, carry):
        base = pl.multiple_of(h * PEER_DQ, PEER_DQ)
        tops = []
        for j in range(2):
            qq = q_scr[pl.ds(base + j * half, half), :]
            q_hi, q_lo = _split_bf16(qq)
            s = _dot3(k_hi_ref[j, h], k_lo_ref[j, h], q_hi, q_lo, _NN_DIMS)
            tops.append(_topk_rows(s, PEER_TOPK))
        (v1, i1), (v2, i2) = tops
        pieces = [v1[a:a + 1, :] + v2[:PEER_TOPK // (a + 1), :] for a in range(PEER_TOPK)]
        n_cand = sum(PEER_TOPK // (a + 1) for a in range(PEER_TOPK))
        pad = -n_cand % 8
        cand = jnp.concatenate(pieces + [jnp.full((pad, v1.shape[1]), -jnp.inf, F32)], axis=0)
        best, pos = _topk_rows(cand, PEER_TOPK)
        e = jnp.exp(best - best[0:1, :])
        gates = e / jnp.sum(e, axis=0, keepdims=True)
        a_idx = jnp.zeros_like(pos)
        start = jnp.zeros_like(pos)
        first = 0
        for a in range(1, PEER_TOPK):
            width = PEER_TOPK // a
            first += width
            reached = pos >= first
            a_idx = a_idx + jnp.where(reached, 1, 0)
            start = start + jnp.where(reached, width, 0)
        row0 = pl.multiple_of(h * PEER_TOPK, PEER_TOPK)
        slot_scr[0, pl.ds(row0, PEER_TOPK), :] = _select_rows(a_idx, i1).astype(F32)
        slot_scr[1, pl.ds(row0, PEER_TOPK), :] = _select_rows(pos - start, i2).astype(F32)
        slot_scr[2, pl.ds(row0, PEER_TOPK), :] = gates
        return carry

    lax.fori_loop(0, PEER_HEADS, head_body, 0, unroll=PEER_ROUTE_UNROLL)
    i1_ref[...] = slot_scr[0].T.astype(jnp.int32)
    i2_ref[...] = slot_scr[1].T.astype(jnp.int32)
    gate_ref[...] = slot_scr[2].T


def _bf16_bits(x):
    return pltpu.bitcast(x.astype(BF16).astype(F32), jnp.uint32)


def _peer_gate_kernel(i1_ref, i2_ref, gate_ref, g_ref):
    rows = lax.broadcasted_iota(jnp.int32, (PEER_KEYS, PEER_SLOTS), 0)
    sub = PEER_GATE_SUBLANES

    def token_body(t, carry):
        a = i1_ref[pl.ds(t, 1), :]
        b = i2_ref[pl.ds(t, 1), :]
        w = gate_ref[pl.ds(t, 1), :]
        lhs = jnp.where(rows == a, w, 0.0).astype(BF16)
        rhs = jnp.where(rows == b, 1.0, 0.0).astype(BF16)
        gt = _mm(lhs, rhs, _NT_DIMS)
        row0 = pl.multiple_of(t * sub, sub)
        for g in range(PEER_KEYS // (2 * sub)):
            lo = gt[2 * sub * g:2 * sub * g + sub]
            hi = gt[2 * sub * g + sub:2 * sub * (g + 1)]
            g_ref[g, pl.ds(row0, sub), :] = (_bf16_bits(lo) >> 16) | _bf16_bits(hi)
        return carry

    lax.fori_loop(0, i1_ref.shape[0], token_body, 0, unroll=PEER_GATE_UNROLL)


def _peer_dense_kernel(xn_ref, u_ref, v_ref, gpk_ref, h_ref, mod_ref, o_ref, acc_ref):
    j = pl.program_id(1)

    @pl.when(j == 0)
    def _():
        acc_ref[...] = jnp.zeros_like(acc_ref)

    sub = PEER_GATE_SUBLANES
    xn = xn_ref[...]
    tokens = xn.shape[0]
    words = [gpk_ref[0, pl.ds(r, tokens, stride=sub), :] for r in range(sub)]
    for half in range(2):
        rows = slice(half * sub * PEER_KEYS, (half + 1) * sub * PEER_KEYS)
        hid = _gelu_erf(_mm(xn, u_ref[rows, :], _NT_DIMS))
        ys = []
        for r in range(sub):
            bits = (words[r] << 16) if half == 0 else (words[r] & jnp.uint32(0xFFFF0000))
            ys.append(pltpu.bitcast(bits, F32) * hid[:, r * PEER_KEYS:(r + 1) * PEER_KEYS])
        y = jnp.concatenate(ys, axis=1).astype(BF16)
        acc_ref[...] += _mm(y, v_ref[rows, :])

    @pl.when(j == pl.num_programs(1) - 1)
    def _():
        o_ref[...] = h_ref[...] + mod_ref[0] * acc_ref[...]


def _peer_layer(h, norm_g, shift, scale, gate_mod, wq_t_hi, wq_t_lo, keys_hi, keys_lo, u_bf, v_bf):
    n, d = h.shape
    nb = shift.shape[0]
    rows_per_batch = n // nb
    tr = min(PEER_ROUTE_TOKENS, rows_per_batch)
    full = lambda *shape: pl.BlockSpec(shape, lambda i: (0,) * len(shape))
    per_batch = lambda t: pl.BlockSpec((1, 1, d), lambda i: (i * t // rows_per_batch, 0, 0))
    xn, i1, i2, gate = pl.pallas_call(
        _peer_route_kernel,
        grid=(n // tr,),
        in_specs=[pl.BlockSpec((tr, d), lambda i: (i, 0)), full(1, d), per_batch(tr), per_batch(tr),
                  full(PEER_HEADS * PEER_DQ, d), full(PEER_HEADS * PEER_DQ, d),
                  full(2, PEER_HEADS, PEER_KEYS, PEER_DQ // 2), full(2, PEER_HEADS, PEER_KEYS, PEER_DQ // 2)],
        out_specs=[pl.BlockSpec((tr, d), lambda i: (i, 0))] + [pl.BlockSpec((tr, PEER_SLOTS), lambda i: (i, 0))] * 3,
        out_shape=[jax.ShapeDtypeStruct((n, d), BF16),
                   jax.ShapeDtypeStruct((n, PEER_SLOTS), jnp.int32),
                   jax.ShapeDtypeStruct((n, PEER_SLOTS), jnp.int32),
                   jax.ShapeDtypeStruct((n, PEER_SLOTS), F32)],
        scratch_shapes=[pltpu.VMEM((PEER_HEADS * PEER_DQ, tr), F32), pltpu.VMEM((3, PEER_SLOTS, tr), F32)],
        compiler_params=pltpu.CompilerParams(dimension_semantics=("arbitrary",)),
        name="peer_route",
    )(h, norm_g.reshape(1, d), shift, scale, wq_t_hi, wq_t_lo, keys_hi, keys_lo)

    tg = min(PEER_GATE_TOKENS, n)
    n_planes = PEER_KEYS // (2 * PEER_GATE_SUBLANES)
    slot_spec = pl.BlockSpec((tg, PEER_SLOTS), lambda i: (i, 0))
    gmat = pl.pallas_call(
        _peer_gate_kernel,
        grid=(n // tg,),
        in_specs=[slot_spec, slot_spec, slot_spec],
        out_specs=pl.BlockSpec((n_planes, tg * PEER_GATE_SUBLANES, PEER_KEYS), lambda i: (0, i, 0)),
        out_shape=jax.ShapeDtypeStruct((n_planes, n * PEER_GATE_SUBLANES, PEER_KEYS), jnp.uint32),
        compiler_params=pltpu.CompilerParams(dimension_semantics=("arbitrary",)),
        name="peer_gate",
    )(i1, i2, gate)

    tm = min(PEER_DENSE_TOKENS, rows_per_batch)
    te = PEER_DENSE_EXPERTS
    return pl.pallas_call(
        _peer_dense_kernel,
        grid=(n // tm, PEER_EXPERTS // te),
        in_specs=[pl.BlockSpec((tm, d), lambda i, j: (i, 0)),
                  pl.BlockSpec((te, d), lambda i, j: (j, 0)),
                  pl.BlockSpec((te, d), lambda i, j: (j, 0)),
                  pl.BlockSpec((1, tm * PEER_GATE_SUBLANES, PEER_KEYS), lambda i, j: (j, i, 0)),
                  pl.BlockSpec((tm, d), lambda i, j: (i, 0)),
                  pl.BlockSpec((1, 1, d), lambda i, j: (i * tm // rows_per_batch, 0, 0))],
        out_specs=pl.BlockSpec((tm, d), lambda i, j: (i, 0)),
        out_shape=jax.ShapeDtypeStruct((n, d), F32),
        scratch_shapes=[pltpu.VMEM((tm, d), F32)],
        compiler_params=pltpu.CompilerParams(dimension_semantics=("arbitrary", "arbitrary"),
                                             vmem_limit_bytes=52 * 2 ** 20),
        name="peer_dense",
    )(xn, u_bf, v_bf, gmat, h, gate_mod)


def _final_norm_kernel(x_ref, g_ref, o_ref):
    o_ref[...] = _rms(x_ref[...], g_ref[...])


def _final_norm(h, g):
    n = h.shape[0] * h.shape[1]
    x2 = h.reshape(n, D_MODEL)
    tm = 512
    out = pl.pallas_call(
        _final_norm_kernel,
        grid=(n // tm,),
        in_specs=[pl.BlockSpec((tm, D_MODEL), lambda i: (i, 0)),
                  pl.BlockSpec((1, D_MODEL), lambda i: (0, 0))],
        out_specs=pl.BlockSpec((tm, D_MODEL), lambda i: (i, 0)),
        out_shape=jax.ShapeDtypeStruct((n, D_MODEL), F32),
        name="final_norm",
    )(x2, g.reshape(1, D_MODEL))
    return out.reshape(h.shape)


def _mix_layer(h_lat, h_ctx, mod_l, mod_c, norm_g, w_in, w_out, ssd, gla, mla, s5, ctx_out):
    b, n_lat, d = h_lat.shape
    n_lat_tiles = n_lat // ROW_TILE
    tab = lambda k: jnp.concatenate([mod_l[k], mod_c[k]], axis=0)
    p = _inproj(h_lat, h_ctx, norm_g, tab(0), tab(1), _pack_w_in(w_in))
    ssd_y, ssd_yb, ssd_xbc = _ssd_mixer(p, ssd["conv_w"], ssd["conv_b"], ssd["a_log"], ssd["dt_bias"], n_lat_tiles)
    gla_o, gla_ob = _gla_mixer(p, gla["gate_w"], gla["gate_b"], n_lat_tiles)
    mla_lat, mla_ctx = _mla_mixer(p, mla["q_norm_g"], mla["w_uq"], mla["kv_norm_g"], mla["w_ukv"], n_lat_tiles, ctx_out)
    s5_y = _s5_mixer(p, s5["a_re"], s5["a_im"], s5["log_dt"], s5["b_re"], s5["b_im"], s5["c_re"], s5["c_im"],
                     n_lat_tiles)
    post = functools.partial(_post, p=p, ssd_xbc=ssd_xbc, ssd_y=ssd_y, ssd_yb=ssd_yb, gla_o=gla_o, gla_ob=gla_ob,
                             s5_y=s5_y, ssd_d=ssd["d"],
                             ssd_norm_g=ssd["norm_g"], gla_norm_g=gla["norm_g"], s5_d=s5["d"],
                             glu_w=s5["glu_w"], glu_b=s5["glu_b"], w_out=w_out)
    new_lat = post(h_lat, mla_y=mla_lat, mod=mod_l[2], row_off=0, mla_off=0)
    new_ctx = None
    if ctx_out:
        new_ctx = post(h_ctx, mla_y=mla_ctx, mod=mod_c[2], row_off=n_lat_tiles, mla_off=0)
    return new_lat, new_ctx


def kernel(x, c, ctx, c_ctx, ada_w, ada_b, norm_mix_g, norm_ffn_g, w_in, w_out,
           ssd_conv_w, ssd_conv_b, ssd_a_log, ssd_dt_bias, ssd_d, ssd_norm_g,
           gla_gate_w, gla_gate_b, gla_norm_g, mla_q_norm_g, mla_w_uq, mla_kv_norm_g, mla_w_ukv,
           s5_a_re, s5_a_im, s5_log_dt, s5_b_re, s5_b_im, s5_c_re, s5_c_im, s5_d, s5_glu_w, s5_glu_b,
           peer_w_q, peer_sub_keys, peer_u, peer_v, final_norm_g):
    h_lat, h_ctx = x, ctx
    cond_lat = jax.nn.silu(c)[:, None, :]
    cond_ctx = jax.nn.silu(c_ctx)[None, None, :]
    for i in range(DEPTH):
        ctx_out = i < DEPTH - 1
        mod_l = jnp.split(cond_lat @ ada_w[i] + ada_b[i], N_MOD, axis=-1)
        mod_c = jnp.split(cond_ctx @ ada_w[i] + ada_b[i], N_MOD, axis=-1)
        ssd = dict(conv_w=ssd_conv_w[i], conv_b=ssd_conv_b[i], a_log=ssd_a_log[i], dt_bias=ssd_dt_bias[i],
                   d=ssd_d[i], norm_g=ssd_norm_g[i])
        gla = dict(gate_w=gla_gate_w[i], gate_b=gla_gate_b[i], norm_g=gla_norm_g[i])
        mla = dict(q_norm_g=mla_q_norm_g[i], w_uq=mla_w_uq[i], kv_norm_g=mla_kv_norm_g[i], w_ukv=mla_w_ukv[i])
        s5 = dict(a_re=s5_a_re[i], a_im=s5_a_im[i], log_dt=s5_log_dt[i], b_re=s5_b_re[i], b_im=s5_b_im[i],
                  c_re=s5_c_re[i], c_im=s5_c_im[i], d=s5_d[i], glu_w=s5_glu_w[i], glu_b=s5_glu_b[i])
        h_lat, h_ctx_new = _mix_layer(h_lat, h_ctx, mod_l, mod_c, norm_mix_g[i], w_in[i], w_out[i],
                                      ssd, gla, mla, s5, ctx_out)
        wq_t_hi, wq_t_lo = _split_bf16(peer_w_q[i].T)
        keys_hi, keys_lo = _split_bf16(peer_sub_keys[i])
        u_bf, v_bf = peer_u[i].astype(BF16), peer_v[i].astype(BF16)
        peer = functools.partial(_peer_layer, norm_g=norm_ffn_g[i], wq_t_hi=wq_t_hi, wq_t_lo=wq_t_lo,
                                 keys_hi=keys_hi, keys_lo=keys_lo, u_bf=u_bf, v_bf=v_bf)
        h_lat = peer(h_lat.reshape(-1, D_MODEL), shift=mod_l[3], scale=mod_l[4],
                     gate_mod=mod_l[5]).reshape(h_lat.shape)
        if ctx_out:
            h_ctx = peer(h_ctx_new.reshape(-1, D_MODEL), shift=mod_c[3], scale=mod_c[4],
                         gate_mod=mod_c[5]).reshape(h_ctx.shape)
    return _final_norm(h_lat, final_norm_g)
```

```python
import functools
import jax
import jax.numpy as jnp
from jax import lax
import numpy as np
from jax.experimental import pallas as pl
from jax.experimental.pallas import tpu as pltpu

D_MODEL = 1024
DEPTH = 2
GRID_W = 64
NORM_EPS = 1e-6
N_MOD = 6

GROUP_WIDTH = D_MODEL // 4

SSD_WIDTH = GROUP_WIDTH
SSD_HEAD_DIM = 64
SSD_HEADS = SSD_WIDTH // SSD_HEAD_DIM
SSD_GROUPS = 2
SSD_STATE = 128
SSD_CONV = 5
SSD_CHUNK = 128
SSD_CONV_CH = SSD_WIDTH + 2 * SSD_GROUPS * SSD_STATE
SSD_IN = SSD_WIDTH + SSD_CONV_CH + 2 * SSD_HEADS

GLA_WIDTH = GROUP_WIDTH
GLA_HEADS = 4
GLA_DV = GLA_WIDTH // GLA_HEADS
GLA_DK = GLA_DV // 2
GLA_QK = GLA_HEADS * GLA_DK
GLA_GATE_RANK = 16
GLA_TAU = 16.0
GLA_CHUNK = 64
GLA_IN = 2 * GLA_QK + 2 * GLA_WIDTH + 2 * GLA_GATE_RANK

MLA_WIDTH = GROUP_WIDTH
MLA_HEADS = 4
MLA_V = MLA_WIDTH // MLA_HEADS
MLA_NOPE = 64
MLA_ROPE = 32
MLA_Q_RANK = 256
MLA_KV_RANK = 128
MLA_SCALE = (MLA_NOPE + MLA_ROPE) ** -0.5
ROPE_BASE = 10000.0
MLA_IN = MLA_Q_RANK + MLA_KV_RANK + MLA_ROPE

S5_WIDTH = GROUP_WIDTH
S5_GROUP = 16
S5_NGROUPS = S5_WIDTH // S5_GROUP
S5_STATE = 64
S5_MAX_RE = -1e-4
S5_IN = S5_WIDTH
S5_CHUNK = 16
S5_PAIRS = S5_NGROUPS // 2
S5_PACK_ROWS = (768, 256)

PEER_KEYS = 128
PEER_EXPERTS = PEER_KEYS * PEER_KEYS
PEER_HEADS = 8
PEER_TOPK = 16
PEER_DQ = 128

LANES = 128
ROW_TILE = 256
SCAN_STEP_ROWS = ROW_TILE

F32 = jnp.float32
BF16 = jnp.bfloat16

COL_XS, COL_BM, COL_CM, COL_Z = 0, 256, 512, 768
COL_GLA_V, COL_GLA_R, COL_CQ, COL_S5 = 1024, 1280, 1536, 1792
COL_GLA_Q, COL_GLA_K, COL_CKV, COL_DT, COL_GLR, COL_KR, COL_KRROT = 2048, 2176, 2304, 2432, 2560, 2688, 2816
P_COLS = 2944

_NN_DIMS = (((1,), (0,)), ((), ()))
_NT_DIMS = (((1,), (1,)), ((), ()))
_TN_DIMS = (((0,), (0,)), ((), ()))


def _mm(a, b, dims=_NN_DIMS):
    return lax.dot_general(a, b, dims, preferred_element_type=F32)


def _split_bf16(x):
    hi = x.astype(BF16)
    lo = (x - hi.astype(F32)).astype(BF16)
    return hi, lo


def _split3_bf16(x):
    p1 = x.astype(BF16)
    r1 = x - p1.astype(F32)
    p2 = r1.astype(BF16)
    p3 = (r1 - p2.astype(F32)).astype(BF16)
    return p1, p2, p3


def _dot3(a_hi, a_lo, b_hi, b_lo, dims):
    return _mm(a_hi, b_hi, dims) + _mm(a_hi, b_lo, dims) + _mm(a_lo, b_hi, dims)


def _gelu_erf(x):
    return 0.5 * x * (1.0 + lax.erf(x * (2.0 ** -0.5)))


def _silu(x):
    return x * jax.nn.sigmoid(x)


def _softplus(x):
    return jnp.maximum(x, 0.0) + jnp.log1p(jnp.exp(-jnp.abs(x)))


def _log_sigmoid(x):
    return jnp.minimum(x, 0.0) - jnp.log1p(jnp.exp(-jnp.abs(x)))


def _rms(x, g):
    return x * lax.rsqrt(jnp.mean(x * x, axis=-1, keepdims=True) + NORM_EPS) * g


def _modulated_norm(x, g, shift, scale):
    return _rms(x, g) * (1.0 + scale) + shift


def _causal_mask(n, reverse):
    ri = lax.broadcasted_iota(jnp.int32, (n, n), 0)
    ci = lax.broadcasted_iota(jnp.int32, (n, n), 1)
    return (ci >= ri) if reverse else (ci <= ri)


def _scan_chunk(s, n_lat, n_ctx, reverse):
    if reverse:
        return n_lat + n_ctx - 1 - s
    return jnp.where(s < n_ctx, n_lat + s, s - n_ctx)


def _inproj_kernel(lat_ref, ctx_ref, g_ref, shift_ref, scale_ref, w_ref, o_ref, *, n_lat_tiles):
    t = pl.program_id(1)
    x = jnp.where(t < n_lat_tiles, lat_ref[0], ctx_ref[0])
    xn = _modulated_norm(x, g_ref[...], shift_ref[0], scale_ref[0])
    o_ref[0] = _mm(xn.astype(BF16), w_ref[...])


def _inproj(h_lat, h_ctx, norm_g, shift_tab, scale_tab, w_pad):
    b, n_lat, d = h_lat.shape
    n_lat_tiles = n_lat // ROW_TILE
    n_tiles = n_lat_tiles + h_ctx.shape[1] // ROW_TILE
    mod_spec = pl.BlockSpec((1, 1, d), lambda i, t: (jnp.where(t < n_lat_tiles, i, b), 0, 0))
    return pl.pallas_call(
        functools.partial(_inproj_kernel, n_lat_tiles=n_lat_tiles),
        grid=(b, n_tiles),
        in_specs=[pl.BlockSpec((1, ROW_TILE, d), lambda i, t: (i, jnp.minimum(t, n_lat_tiles - 1), 0)),
                  pl.BlockSpec((1, ROW_TILE, d), lambda i, t: (i, jnp.maximum(t - n_lat_tiles, 0), 0)),
                  pl.BlockSpec((1, d), lambda i, t: (0, 0)), mod_spec, mod_spec,
                  pl.BlockSpec((d, P_COLS), lambda i, t: (0, 0))],
        out_specs=pl.BlockSpec((1, ROW_TILE, P_COLS), lambda i, t: (i, t, 0)),
        out_shape=jax.ShapeDtypeStruct((b, n_tiles * ROW_TILE, P_COLS), F32),
        compiler_params=pltpu.CompilerParams(dimension_semantics=("arbitrary", "arbitrary"),
                                             vmem_limit_bytes=48 * 2 ** 20),
        name="inproj",
    )(h_lat, h_ctx, norm_g.reshape(1, d), shift_tab, scale_tab, w_pad)


def _pack_w_in(w):
    o_ssd, o_gla, o_mla, o_s5 = 0, SSD_IN, SSD_IN + GLA_IN, SSD_IN + GLA_IN + MLA_IN
    out = jnp.zeros((w.shape[0], P_COLS), F32)
    put = lambda out, col, src, width: out.at[:, col:col + width].set(w[:, src:src + width])
    out = put(out, COL_Z, o_ssd, SSD_WIDTH)
    out = put(out, COL_XS, o_ssd + SSD_WIDTH, SSD_CONV_CH)
    out = put(out, COL_DT, o_ssd + SSD_WIDTH + SSD_CONV_CH, 2 * SSD_HEADS)
    out = put(out, COL_GLA_Q, o_gla, GLA_QK)
    out = put(out, COL_GLA_K, o_gla + GLA_QK, GLA_QK)
    out = put(out, COL_GLA_V, o_gla + 2 * GLA_QK, GLA_WIDTH)
    out = put(out, COL_GLA_R, o_gla + 2 * GLA_QK + GLA_WIDTH, GLA_WIDTH)
    out = put(out, COL_GLR, o_gla + 2 * GLA_QK + 2 * GLA_WIDTH, 2 * GLA_GATE_RANK)
    out = put(out, COL_CQ, o_mla, MLA_Q_RANK)
    out = put(out, COL_CKV, o_mla + MLA_Q_RANK, MLA_KV_RANK)
    o_kr = o_mla + MLA_Q_RANK + MLA_KV_RANK
    half = MLA_ROPE // 2
    out = put(out, COL_KR + MLA_NOPE, o_kr, MLA_ROPE)
    out = out.at[:, COL_KRROT + MLA_NOPE:COL_KRROT + MLA_NOPE + half].set(-w[:, o_kr + half:o_kr + MLA_ROPE])
    out = out.at[:, COL_KRROT + MLA_NOPE + half:COL_KRROT + MLA_NOPE + MLA_ROPE].set(w[:, o_kr:o_kr + half])
    out = put(out, COL_S5, o_s5, S5_WIDTH)
    return out.astype(BF16)


def _ssd_prep_kernel(x_ref, prev_ref, next_ref, dt_ref, w_ref, b_ref, bias_ref, xbc_ref, dtc_ref, dtt_ref,
                     *, n_lat_tiles):
    t = pl.program_id(1)
    x = x_ref[0]
    halo = prev_ref.shape[1]
    prev = jnp.where(jnp.logical_and(t > 0, t < n_lat_tiles), prev_ref[0], 0.0)
    nxt = jnp.where(t < n_lat_tiles - 1, next_ref[0], 0.0)
    ext = jnp.concatenate([prev, x, nxt], axis=0)
    rows = ext.shape[0]
    left = SSD_CONV // 2
    acc = jnp.zeros_like(x) + b_ref[...]
    for k in range(SSD_CONV):
        shifted = ext if k == left else pltpu.roll(ext, (left - k) % rows, 0)
        acc = acc + w_ref[k:k + 1, :] * shifted[halo:halo + x.shape[0]]
    xbc_ref[0] = _silu(acc)
    dt = _softplus(dt_ref[0] + bias_ref[...])
    dtc_ref[0] = dt
    dtt_ref[0] = dt.T[:dtt_ref.shape[1]]


def _ssd_scan_kernel(xbc_f_ref, dtc_f_ref, dtt_f_ref, xbc_b_ref, dtc_b_ref, dtt_b_ref, ahr_ref, ahc_ref,
                     yf_ref, yb_ref, state_ref):
    @pl.when(pl.program_id(0) == 0)
    def _():
        state_ref[...] = jnp.zeros_like(state_ref)

    q = SSD_CHUNK
    n_sub = xbc_f_ref.shape[1] // q
    for k in range(n_sub):
        for direction, (xbc_ref, dtc_ref, dtt_ref, y_ref) in enumerate(
                ((xbc_f_ref, dtc_f_ref, dtt_f_ref, yf_ref), (xbc_b_ref, dtc_b_ref, dtt_b_ref, yb_ref))):
            c = k if direction == 0 else n_sub - 1 - k
            rows = slice(c * q, (c + 1) * q)
            for bi in range(xbc_ref.shape[0]):
                y_ref[bi, rows] = _ssd_chunk(xbc_ref[bi, rows], dtc_ref[bi, rows], dtt_ref[bi, :, rows], ahr_ref,
                                             ahc_ref, state_ref.at[bi, direction], direction)


def _ssd_chunk(xbc, dtc, dtt, ahr_ref, ahc_ref, state_ref, direction):
    reverse = direction == 1
    q = SSD_CHUNK
    mask = _causal_mask(q, reverse)
    tri = jnp.where(mask, 1.0, 0.0).astype(BF16)
    xs, bm, cm = xbc[:, :SSD_WIDTH], xbc[:, SSD_WIDTH:SSD_WIDTH + 256], xbc[:, SSD_WIDTH + 256:]
    a_col = dtc * ahr_ref[...]
    a_row = dtt * ahc_ref[...]
    acum_col = sum(_mm(tri, part) for part in _split3_bf16(a_col))
    acum_row = sum(_mm(part, tri, _NT_DIMS) for part in _split3_bf16(a_row))
    end = 0 if reverse else q - 1
    bm_bf, cm_bf = bm.astype(BF16), cm.astype(BF16)
    ys = []
    cb = {}
    for h in range(SSD_HEADS):
        g = h // (SSD_HEADS // SSD_GROUPS)
        gs = slice(g * SSD_STATE, (g + 1) * SSD_STATE)
        if g not in cb:
            cb[g] = _mm(cm_bf[:, gs], bm_bf[:, gs], _NT_DIMS)
        ch = direction * SSD_HEADS + h
        ac = acum_col[:, ch:ch + 1]
        ar = acum_row[ch:ch + 1, :]
        decay = jnp.exp(jnp.where(mask, ac - ar, -jnp.inf))
        xd = xs[:, h * SSD_HEAD_DIM:(h + 1) * SSD_HEAD_DIM] * dtc[:, ch:ch + 1]
        y_diag = _mm((cb[g] * decay).astype(BF16), xd.astype(BF16))
        a_end = ac[end:end + 1, :]
        st_local = _mm((xd * jnp.exp(a_end - ac)).astype(BF16), bm_bf[:, gs], _TN_DIMS)
        hs = state_ref[h]
        y_off = jnp.exp(ac) * _mm(cm_bf[:, gs], hs.astype(BF16), _NT_DIMS)
        state_ref[h] = jnp.exp(a_end) * hs + st_local
        ys.append(y_diag + y_off)
    return jnp.concatenate(ys, axis=1)


def _ssd_mixer(p, conv_w, conv_b, a_log, dt_bias, n_lat_tiles):
    b, r, _ = p.shape
    nt = r // ROW_TILE
    halo = 8
    hb = ROW_TILE // halo
    w8 = jnp.zeros((8, SSD_CONV_CH), F32).at[:SSD_CONV].set(conv_w)
    bias = jnp.zeros((1, LANES), F32).at[0, :2 * SSD_HEADS].set(dt_bias.reshape(-1))
    xbc, dtc, dtt = pl.pallas_call(
        functools.partial(_ssd_prep_kernel, n_lat_tiles=n_lat_tiles),
        grid=(b, nt),
        in_specs=[pl.BlockSpec((1, ROW_TILE, SSD_CONV_CH), lambda i, t: (i, t, 0)),
                  pl.BlockSpec((1, halo, SSD_CONV_CH), lambda i, t: (i, jnp.maximum(t * hb - 1, 0), 0)),
                  pl.BlockSpec((1, halo, SSD_CONV_CH), lambda i, t: (i, jnp.minimum((t + 1) * hb, nt * hb - 1), 0)),
                  pl.BlockSpec((1, ROW_TILE, LANES), lambda i, t: (i, t, COL_DT // LANES)),
                  pl.BlockSpec((8, SSD_CONV_CH), lambda i, t: (0, 0)),
                  pl.BlockSpec((1, SSD_CONV_CH), lambda i, t: (0, 0)),
                  pl.BlockSpec((1, LANES), lambda i, t: (0, 0))],
        out_specs=[pl.BlockSpec((1, ROW_TILE, SSD_CONV_CH), lambda i, t: (i, t, 0)),
                   pl.BlockSpec((1, ROW_TILE, LANES), lambda i, t: (i, t, 0)),
                   pl.BlockSpec((1, 8, ROW_TILE), lambda i, t: (i, 0, t))],
        out_shape=[jax.ShapeDtypeStruct((b, r, SSD_CONV_CH), F32),
                   jax.ShapeDtypeStruct((b, r, LANES), F32),
                   jax.ShapeDtypeStruct((b, 8, r), F32)],
        compiler_params=pltpu.CompilerParams(dimension_semantics=("arbitrary", "arbitrary")),
        name="ssd_prep",
    )(p, p, p, p, w8, conv_b.reshape(1, -1), bias)

    a_head = -jnp.exp(a_log.astype(F32)).reshape(-1)
    ahr = jnp.zeros((1, LANES), F32).at[0, :2 * SSD_HEADS].set(a_head)
    ahc = a_head.reshape(2 * SSD_HEADS, 1)
    blk = SCAN_STEP_ROWS
    n_lat = n_lat_tiles * ROW_TILE // blk
    n_ctx = r // blk - n_lat
    in_specs, y_specs = [], []
    for reverse in (False, True):
        cidx = functools.partial(_scan_chunk, n_lat=n_lat, n_ctx=n_ctx, reverse=reverse)
        in_specs += [pl.BlockSpec((b, blk, SSD_CONV_CH), lambda s, cidx=cidx: (0, cidx(s), 0)),
                     pl.BlockSpec((b, blk, LANES), lambda s, cidx=cidx: (0, cidx(s), 0)),
                     pl.BlockSpec((b, 8, blk), lambda s, cidx=cidx: (0, 0, cidx(s)))]
        y_specs.append(pl.BlockSpec((b, blk, SSD_WIDTH), lambda s, cidx=cidx: (0, cidx(s), 0)))
    in_specs += [pl.BlockSpec((1, LANES), lambda s: (0, 0)), pl.BlockSpec((2 * SSD_HEADS, 1), lambda s: (0, 0))]
    y_f, y_b = pl.pallas_call(
        _ssd_scan_kernel,
        grid=(n_lat + n_ctx,),
        in_specs=in_specs,
        out_specs=y_specs,
        out_shape=[jax.ShapeDtypeStruct((b, r, SSD_WIDTH), F32)] * 2,
        scratch_shapes=[pltpu.VMEM((b, 2, SSD_HEADS, SSD_HEAD_DIM, SSD_STATE), F32)],
        compiler_params=pltpu.CompilerParams(dimension_semantics=("arbitrary",)),
        name="ssd_scan",
    )(xbc, dtc, dtt, xbc, dtc, dtt, ahr, ahc)
    return y_f, y_b, xbc


def _gla_scan_kernel(qf_ref, kf_ref, vf_ref, glrf_ref, qb_ref, kb_ref, vb_ref, glrb_ref, wg_ref, bias_ref,
                     of_ref, ob_ref, st_ref):
    @pl.when(pl.program_id(0) == 0)
    def _():
        st_ref[...] = jnp.zeros_like(st_ref)

    n = GLA_CHUNK
    n_sub = qf_ref.shape[1] // n
    for kk in range(n_sub):
        for direction, (q_ref, k_ref, v_ref, glr_ref, o_ref) in enumerate(
                ((qf_ref, kf_ref, vf_ref, glrf_ref, of_ref), (qb_ref, kb_ref, vb_ref, glrb_ref, ob_ref))):
            c = kk if direction == 0 else n_sub - 1 - kk
            rows = slice(c * n, (c + 1) * n)
            for bi in range(q_ref.shape[0]):
                o_ref[bi, rows] = _gla_chunk(q_ref[bi, rows], k_ref[bi, rows], v_ref[bi, rows], glr_ref[bi, rows],
                                             wg_ref.at[direction], bias_ref.at[direction],
                                             st_ref.at[bi, direction], direction == 1)


def _gla_chunk(q, k, v, glr, wg_ref, bias_ref, st_ref, reverse):
    n = GLA_CHUNK
    mask = _causal_mask(n, reverse)
    tri = jnp.where(mask, 1.0, 0.0).astype(BF16)
    g_hi, g_lo = _split_bf16(glr)
    logits = _dot3(g_hi, g_lo, wg_ref[0], wg_ref[1], _NN_DIMS) + bias_ref[...]
    logg = _log_sigmoid(logits) * (1.0 / GLA_TAU)
    bcum = sum(_mm(tri, part) for part in _split3_bf16(logg))
    end = 0 if reverse else n - 1
    b_end = bcum[end:end + 1, :]
    qe = q * jnp.exp(bcum) * (GLA_DK ** -0.5)
    ke = (k * jnp.exp(-bcum)).astype(BF16)
    kd = k * jnp.exp(b_end - bcum)
    decay_end = jnp.exp(b_end)
    lane_head = lax.broadcasted_iota(jnp.int32, (1, GLA_QK), 1) >> (GLA_DK.bit_length() - 1)
    outs = []
    for h in range(GLA_HEADS):
        hm = lane_head == h
        qh = jnp.where(hm, qe, 0.0).astype(BF16)
        att = jnp.where(mask, _mm(qh, ke, _NT_DIMS), 0.0)
        vh = v[:, h * GLA_DV:(h + 1) * GLA_DV].astype(BF16)
        st = st_ref[h]
        o_h = _mm(att.astype(BF16), vh) + _mm(qh, st.astype(BF16), _NT_DIMS)
        local = _mm(vh, jnp.where(hm, kd, 0.0).astype(BF16), _TN_DIMS)
        st_ref[h] = st * decay_end + local
        outs.append(o_h)
    return jnp.concatenate(outs, axis=1)


def _gla_mixer(p, gate_w, gate_b, n_lat_tiles):
    b, r, _ = p.shape
    rows = SCAN_STEP_ROWS
    n_lat = n_lat_tiles * ROW_TILE // rows
    n_ctx = r // rows - n_lat
    in_specs, o_specs, wgs = [], [], []
    for direction in (0, 1):
        cidx = functools.partial(_scan_chunk, n_lat=n_lat, n_ctx=n_ctx, reverse=direction == 1)
        wg = jnp.zeros((LANES, GLA_QK), F32).at[direction * GLA_GATE_RANK:(direction + 1) * GLA_GATE_RANK].set(
            gate_w[direction])
        wgs.append(jnp.stack(_split_bf16(wg)))
        blk = lambda width, col, cidx=cidx: pl.BlockSpec((b, rows, width), lambda s: (0, cidx(s), col // width))
        in_specs += [blk(GLA_QK, COL_GLA_Q), blk(GLA_QK, COL_GLA_K), blk(GLA_WIDTH, COL_GLA_V), blk(LANES, COL_GLR)]
        o_specs.append(pl.BlockSpec((b, rows, GLA_WIDTH), lambda s, cidx=cidx: (0, cidx(s), 0)))
    in_specs += [pl.BlockSpec((2, 2, LANES, GLA_QK), lambda s: (0, 0, 0, 0)),
                 pl.BlockSpec((2, 1, GLA_QK), lambda s: (0, 0, 0))]
    return pl.pallas_call(
        _gla_scan_kernel,
        grid=(n_lat + n_ctx,),
        in_specs=in_specs,
        out_specs=o_specs,
        out_shape=[jax.ShapeDtypeStruct((b, r, GLA_WIDTH), F32)] * 2,
        scratch_shapes=[pltpu.VMEM((b, 2, GLA_HEADS, GLA_DV, GLA_QK), F32)],
        compiler_params=pltpu.CompilerParams(dimension_semantics=("arbitrary",)),
        name="gla_scan",
    )(p, p, p, p, p, p, p, p, jnp.stack(wgs), gate_b.reshape(2, 1, GLA_QK))


MLA_Q_TILE = 1024
MLA_K_TILES = (768, 256)


def _mla_prep_kernel(cq_ref, ckv_ref, kr_ref, krrot_ref, onec_ref, sinr_ref, gq_ref, gkv_ref,
                     wq_ref, wqr_ref, wk_ref, wv_ref, q_ref, k_ref, v_ref):
    qn = _rms(cq_ref[0], gq_ref[...]).astype(BF16)
    kvn = _rms(ckv_ref[0], gkv_ref[...]).astype(BF16)
    onec, sinr = onec_ref[...], sinr_ref[...]
    k_rope = kr_ref[0] * onec + krrot_ref[0] * sinr
    ones_lane = jnp.where(lax.broadcasted_iota(jnp.int32, (1, LANES), 1) == MLA_V, 1.0, 0.0)
    for h in range(MLA_HEADS):
        qh = _mm(qn, wq_ref[h]) * onec + _mm(qn, wqr_ref[h]) * sinr
        q_ref[0, h] = (qh * MLA_SCALE).astype(BF16)
        k_ref[0, h] = (_mm(kvn, wk_ref[h]) + k_rope).astype(BF16)
        v_ref[0, h] = (_mm(kvn, wv_ref[h]) + ones_lane).astype(BF16)


def _mla_attn_kernel(q_ref, k_ref, v_ref, o_ref, m_ref, acc_ref):
    j = pl.program_id(2)

    @pl.when(j == 0)
    def _():
        m_ref[...] = jnp.full_like(m_ref, -jnp.inf)
        acc_ref[...] = jnp.zeros_like(acc_ref)

    reps = k_ref.shape[2] // LANES
    for h in range(MLA_HEADS):
        s = _mm(q_ref[0, h], k_ref[0, h], _NT_DIMS)
        m_prev = m_ref[h]
        m_new = jnp.maximum(m_prev, jnp.max(s, axis=1, keepdims=True))
        p = jnp.exp((s - jnp.concatenate([m_new] * reps, axis=1)).astype(BF16))
        acc_ref[h] = jnp.exp(m_prev - m_new) * acc_ref[h] + _mm(p, v_ref[0, h])
        m_ref[h] = m_new

    @pl.when(j == pl.num_programs(2) - 1)
    def _():
        outs = []
        for h in range(MLA_HEADS):
            acc = acc_ref[h]
            outs.append(acc[:, :MLA_V] / acc[:, MLA_V:MLA_V + 1])
        o_ref[0] = jnp.concatenate(outs, axis=1)


def _rope_tables(n_lat, n_rows):
    rows = n_lat // GRID_W
    row = jnp.repeat(jnp.arange(rows, dtype=F32), GRID_W)
    col = jnp.tile(jnp.arange(GRID_W, dtype=F32), rows)
    half = MLA_ROPE // 2
    inv = ROPE_BASE ** (-jnp.arange(0, half, 2, dtype=F32) / half)
    ang = jnp.concatenate([row[:, None] * inv, col[:, None] * inv], axis=-1)
    cos = jnp.concatenate([jnp.cos(ang), jnp.ones((n_rows - n_lat, half), F32)], axis=0)
    sin = jnp.concatenate([jnp.sin(ang), jnp.zeros((n_rows - n_lat, half), F32)], axis=0)
    pad = jnp.zeros((n_rows, LANES - MLA_NOPE - MLA_ROPE), F32)
    onec = jnp.concatenate([jnp.ones((n_rows, MLA_NOPE), F32), cos, cos, pad], axis=1)
    sinr = jnp.concatenate([jnp.zeros((n_rows, MLA_NOPE), F32), sin, sin, pad], axis=1)
    return onec, sinr


def _mla_weights(w_uq, w_ukv):
    dqk = MLA_NOPE + MLA_ROPE
    half = MLA_ROPE // 2
    wq = w_uq.reshape(MLA_Q_RANK, MLA_HEADS, dqk).transpose(1, 0, 2)
    rot = jnp.concatenate([jnp.zeros_like(wq[..., :MLA_NOPE]), -wq[..., MLA_NOPE + half:], wq[..., MLA_NOPE:MLA_NOPE + half]],
                          axis=-1)
    padq = lambda w: jnp.pad(w, ((0, 0), (0, 0), (0, LANES - dqk))).astype(BF16)
    wkv = w_ukv.reshape(MLA_KV_RANK, MLA_HEADS, MLA_NOPE + MLA_V).transpose(1, 0, 2)
    padk = lambda w: jnp.pad(w, ((0, 0), (0, 0), (0, LANES - w.shape[-1]))).astype(BF16)
    return padq(wq), padq(rot), padk(wkv[..., :MLA_NOPE]), padk(wkv[..., MLA_NOPE:])


def _mla_attention(q, k, v, q_tile, q_off, n_q, kt, k_off, n_k):
    b = q.shape[0]
    return pl.pallas_call(
        _mla_attn_kernel,
        grid=(b, n_q, n_k),
        in_specs=[pl.BlockSpec((1, MLA_HEADS, q_tile, LANES), lambda i, a, j: (i, 0, q_off + a, 0)),
                  pl.BlockSpec((1, MLA_HEADS, kt, LANES), lambda i, a, j: (i, 0, k_off + j, 0)),
                  pl.BlockSpec((1, MLA_HEADS, kt, LANES), lambda i, a, j: (i, 0, k_off + j, 0))],
        out_specs=pl.BlockSpec((1, q_tile, MLA_WIDTH), lambda i, a, j: (i, a, 0)),
        out_shape=jax.ShapeDtypeStruct((b, n_q * q_tile, MLA_WIDTH), F32),
        scratch_shapes=[pltpu.VMEM((MLA_HEADS, q_tile, LANES), F32), pltpu.VMEM((MLA_HEADS, q_tile, LANES), F32)],
        compiler_params=pltpu.CompilerParams(dimension_semantics=("arbitrary", "arbitrary", "arbitrary")),
        name="mla_attn",
    )(q, k, v)


def _mla_mixer(p, q_norm_g, w_uq, kv_norm_g, w_ukv, n_lat_tiles, ctx_out):
    b, r, _ = p.shape
    nt = r // ROW_TILE
    n_lat = n_lat_tiles * ROW_TILE
    onec, sinr = _rope_tables(n_lat, r)
    wq, wqr, wk, wv = _mla_weights(w_uq, w_ukv)
    blk = lambda width, col: pl.BlockSpec((1, ROW_TILE, width), lambda i, t: (i, t, col // width))
    tab = pl.BlockSpec((ROW_TILE, LANES), lambda i, t: (t, 0))
    full = lambda *shape: pl.BlockSpec(shape, lambda i, t: (0,) * len(shape))
    head_out = pl.BlockSpec((1, MLA_HEADS, ROW_TILE, LANES), lambda i, t: (i, 0, t, 0))
    q, k, v = pl.pallas_call(
        _mla_prep_kernel,
        grid=(b, nt),
        in_specs=[blk(MLA_Q_RANK, COL_CQ), blk(LANES, COL_CKV), blk(LANES, COL_KR), blk(LANES, COL_KRROT), tab, tab,
                  full(1, MLA_Q_RANK), full(1, MLA_KV_RANK),
                  full(MLA_HEADS, MLA_Q_RANK, LANES), full(MLA_HEADS, MLA_Q_RANK, LANES),
                  full(MLA_HEADS, MLA_KV_RANK, LANES), full(MLA_HEADS, MLA_KV_RANK, LANES)],
        out_specs=[head_out] * 3,
        out_shape=[jax.ShapeDtypeStruct((b, MLA_HEADS, r, LANES), BF16)] * 3,
        compiler_params=pltpu.CompilerParams(dimension_semantics=("arbitrary", "arbitrary")),
        name="mla_prep",
    )(p, p, p, p, onec, sinr, q_norm_g.reshape(1, -1), kv_norm_g.reshape(1, -1), wq, wqr, wk, wv)
    q_tile = min(MLA_Q_TILE, n_lat)
    k_tile = next(t for t in MLA_K_TILES if r % t == 0)
    y_lat = _mla_attention(q, k, v, q_tile, 0, n_lat // q_tile, k_tile, 0, r // k_tile)
    y_ctx = None
    if ctx_out:
        n_ctx = r - n_lat
        y_ctx = _mla_attention(q, k, v, n_ctx, n_lat // n_ctx, 1, n_ctx, n_lat // n_ctx, 1)
    return y_lat, y_ctx


def _s5_matrices(a_re, a_im, log_dt, b_re, b_im, c_re, c_im):
    q, ng, ns, nc = S5_CHUNK, S5_NGROUPS, S5_STATE, S5_GROUP
    lam = jnp.minimum(a_re.astype(F32), S5_MAX_RE) + 1j * a_im.astype(F32)
    step = jnp.exp(log_dt.astype(F32))[..., None]
    abar = jnp.exp(lam * step)
    bmat = b_re.astype(F32) + 1j * b_im.astype(F32)
    bbar = ((abar - 1.0) / lam)[..., None] * bmat
    cmat = c_re.astype(F32) + 1j * c_im.astype(F32)
    pw = jnp.exp((lam * step)[..., None] * jnp.arange(q + 1, dtype=F32))
    kern = jnp.einsum('dgcn,dgnl,dgnk->dglck', cmat, pw[..., :q], bbar).real
    ii = jnp.arange(q)
    lag_f = ii[None, :] - ii[:, None]
    gather = lambda kd, lag: jnp.where((lag >= 0)[None, :, :, None, None], kd[:, jnp.clip(lag, 0, q - 1)], 0.0)
    t_f = gather(kern[0], lag_f).transpose(0, 1, 4, 2, 3)
    t_b = gather(kern[1], -lag_f).transpose(0, 1, 4, 2, 3)
    t_sum = (t_f + t_b).reshape(ng, q * nc, q * nc)
    pw_f = pw[0][..., q - 1 - ii]
    pw_b = pw[1][..., ii]
    wst = lambda pwd, bb: jnp.einsum('gnj,gnc->gjcn', pwd, bb).reshape(ng, q * nc, ns)
    wst_f, wst_b = wst(pw_f, bbar[0]), wst(pw_b, bbar[1])
    wout = lambda pwd, cm: jnp.einsum('gcn,gni->gnic', cm, pwd).reshape(ng, ns, q * nc)
    wo_f, wo_b = wout(pw[0][..., ii + 1], cmat[0]), wout(pw[1][..., q - ii], cmat[1])
    aq = pw[..., q]

    def pair_cols(x):
        x = x.reshape(S5_PAIRS, 2, x.shape[1], x.shape[2])
        z = jnp.zeros_like(x[:, 0])
        return jnp.concatenate([jnp.concatenate([x[:, 0], z], axis=2), jnp.concatenate([z, x[:, 1]], axis=2)], axis=1)

    w_local = jnp.concatenate([pair_cols(wst_f.real), pair_cols(wst_f.imag),
                               pair_cols(wst_b.real), pair_cols(wst_b.imag)], axis=2)
    w_out = jnp.concatenate([pair_cols(t_sum), pair_cols(wo_f.real), pair_cols(-wo_f.imag),
                             pair_cols(wo_b.real), pair_cols(-wo_b.imag)], axis=1)
    aq_pair = aq.reshape(2, S5_PAIRS, 2 * ns)
    aq_tab = jnp.concatenate([aq_pair[0].real, aq_pair[0].imag, aq_pair[1].real, aq_pair[1].imag], axis=1)
    return w_local.astype(BF16), w_out.astype(BF16), aq_tab.reshape(S5_PAIRS, 1, 8 * ns).astype(F32)


def _s5_perm():
    cols = S5_CHUNK * S5_WIDTH
    c = jnp.arange(cols, dtype=jnp.int32)
    cc, j = c % S5_GROUP, (c // S5_GROUP) % S5_CHUNK
    g = c // (S5_GROUP * S5_CHUNK)
    per_half = LANES // S5_GROUP
    src = (g // per_half) * (S5_CHUNK * LANES) + j * LANES + (g % per_half) * S5_GROUP + cc
    return jnp.where(c[:, None] == src[None, :], 1.0, 0.0).astype(BF16)


def _s5_pack_kernel(u_ref, o_ref):
    n = o_ref.shape[1]
    for j in range(S5_CHUNK):
        o_ref[0, :, j * LANES:(j + 1) * LANES] = u_ref[0, pl.ds(j, n, stride=S5_CHUNK), :].astype(BF16)


def _s5_unpack_kernel(y_ref, o_ref):
    n = y_ref.shape[1]
    for i in range(S5_CHUNK):
        o_ref[0, pl.ds(i, n, stride=S5_CHUNK), :] = y_ref[0, :, i * LANES:(i + 1) * LANES]


def _s5_local_kernel(u_ref, perm_ref, w_ref, up_ref, s_ref):
    up = _mm(u_ref[0], perm_ref[...]).astype(BF16)
    up_ref[0] = up
    s_ref[0] = _mm(up, w_ref[0])


def _s5_scan_kernel(s3_ref, aq_ref, hs3_ref, *, n_lat, n_ctx, nb):
    s_ref, hs_ref = s3_ref.at[0], hs3_ref.at[0]
    w = 2 * S5_STATE
    aq = aq_ref[0]
    a = [aq[:, i * w:(i + 1) * w] for i in range(4)]
    zero = jnp.zeros((nb, w), F32)
    slab = 8
    cps = slab // nb

    def run_slab(s_re, s_im, a_re, a_im, h_re, h_im, order):
        ent_re, ent_im = [None] * cps, [None] * cps
        for c in order:
            ent_re[c], ent_im[c] = h_re, h_im
            rows = slice(c * nb, (c + 1) * nb)
            h_re, h_im = a_re * h_re - a_im * h_im + s_re[rows], a_re * h_im + a_im * h_re + s_im[rows]
        return jnp.concatenate(ent_re, axis=0), jnp.concatenate(ent_im, axis=0), h_re, h_im

    def body(kk, carry):
        f_re, f_im, b_re, b_im = carry
        rf = pl.multiple_of(_scan_chunk(kk, n_lat // cps, n_ctx // cps, False) * slab, slab)
        rb = pl.multiple_of(_scan_chunk(kk, n_lat // cps, n_ctx // cps, True) * slab, slab)
        e_re, e_im, f_re, f_im = run_slab(s_ref[pl.ds(rf, slab), 0:w], s_ref[pl.ds(rf, slab), w:2 * w],
                                          a[0], a[1], f_re, f_im, range(cps))
        hs_ref[pl.ds(rf, slab), 0:w] = e_re
        hs_ref[pl.ds(rf, slab), w:2 * w] = e_im
        e_re, e_im, b_re, b_im = run_slab(s_ref[pl.ds(rb, slab), 2 * w:3 * w], s_ref[pl.ds(rb, slab), 3 * w:4 * w],
                                          a[2], a[3], b_re, b_im, range(cps - 1, -1, -1))
        hs_ref[pl.ds(rb, slab), 2 * w:3 * w] = e_re
        hs_ref[pl.ds(rb, slab), 3 * w:4 * w] = e_im
        return f_re, f_im, b_re, b_im

    lax.fori_loop(0, (n_lat + n_ctx) // cps, body, (zero, zero, zero, zero))


def _s5_out_kernel(up_ref, hs_ref, w_ref, perm_ref, y_ref):
    @pl.when(pl.program_id(1) == 0)
    def _():
        y_ref[...] = jnp.zeros_like(y_ref)

    kw = up_ref.shape[2]
    y_pair = _mm(up_ref[0], w_ref[0, :kw]) + _mm(hs_ref[0].astype(BF16), w_ref[0, kw:])
    y_hi, y_lo = _split_bf16(y_pair)
    y_ref[0] += _mm(y_hi, perm_ref[...], _NT_DIMS) + _mm(y_lo, perm_ref[...], _NT_DIMS)


def _s5_mixer(p, a_re, a_im, log_dt, b_re, b_im, c_re, c_im, n_lat_tiles):
    b, r, _ = p.shape
    q = S5_CHUNK
    n_chunks = r // q
    cols = q * S5_WIDTH
    kw = 2 * q * S5_GROUP
    w_local, w_out, aq_tab = _s5_matrices(a_re, a_im, log_dt, b_re, b_im, c_re, c_im)
    perm = _s5_perm()
    cp2 = pltpu.CompilerParams(dimension_semantics=("arbitrary", "arbitrary"), vmem_limit_bytes=48 * 2 ** 20)
    pack_rows = next(t for t in S5_PACK_ROWS if r % t == 0)
    cpt = pack_rows // q
    halves = S5_WIDTH // LANES
    cp3 = pltpu.CompilerParams(dimension_semantics=("arbitrary", "arbitrary", "arbitrary"))
    chunk_rows = pl.BlockSpec((1, cpt, q * LANES), lambda i, t, hf: (i, t, hf))
    u_big = pl.pallas_call(
        _s5_pack_kernel,
        grid=(b, r // pack_rows, halves),
        in_specs=[pl.BlockSpec((1, pack_rows, LANES), lambda i, t, hf: (i, t, COL_S5 // LANES + hf))],
        out_specs=chunk_rows,
        out_shape=jax.ShapeDtypeStruct((b, n_chunks, cols), BF16),
        compiler_params=cp3, name="s5_pack",
    )(p)
    all_chunks = pl.BlockSpec((1, n_chunks, cols), lambda i, g: (i, 0, 0))
    col_tile = lambda width: pl.BlockSpec((1, n_chunks, width), lambda i, g: (i, 0, g))
    perm_cols = pl.BlockSpec((cols, kw), lambda i, g: (0, g))
    u_pairs, s_loc = pl.pallas_call(
        _s5_local_kernel,
        grid=(b, S5_PAIRS),
        in_specs=[all_chunks, perm_cols, pl.BlockSpec((1, kw, kw), lambda i, g: (g, 0, 0))],
        out_specs=[col_tile(kw), col_tile(kw)],
        out_shape=[jax.ShapeDtypeStruct((b, n_chunks, cols), BF16), jax.ShapeDtypeStruct((b, n_chunks, cols), F32)],
        compiler_params=cp2, name="s5_local",
    )(u_big, perm, w_local)
    n_lat = n_lat_tiles * ROW_TILE // q
    hs = pl.pallas_call(
        functools.partial(_s5_scan_kernel, n_lat=n_lat, n_ctx=n_chunks - n_lat, nb=1),
        grid=(b, S5_PAIRS),
        in_specs=[col_tile(kw), pl.BlockSpec((1, 1, kw), lambda i, g: (g, 0, 0))],
        out_specs=col_tile(kw),
        out_shape=jax.ShapeDtypeStruct((b, n_chunks, cols), F32),
        compiler_params=cp2, name="s5_scan",
    )(s_loc, aq_tab)
    y_big = pl.pallas_call(
        _s5_out_kernel,
        grid=(b, S5_PAIRS),
        in_specs=[col_tile(kw), col_tile(kw), pl.BlockSpec((1, 2 * kw, kw), lambda i, g: (g, 0, 0)), perm_cols],
        out_specs=all_chunks,
        out_shape=jax.ShapeDtypeStruct((b, n_chunks, cols), F32),
        compiler_params=cp2, name="s5_out",
    )(u_pairs, hs, w_out, perm)
    return pl.pallas_call(
        _s5_unpack_kernel,
        grid=(b, r // pack_rows, halves),
        in_specs=[chunk_rows],
        out_specs=pl.BlockSpec((1, pack_rows, LANES), lambda i, t, hf: (i, t, hf)),
        out_shape=jax.ShapeDtypeStruct((b, r, S5_WIDTH), F32),
        compiler_params=cp3, name="s5_unpack",
    )(y_big)


def _post_kernel(h_ref, xs_ref, z_ref, r_ref, u_ref, ssd_ref, ssd_b_ref, gla_ref, gla_b_ref, mla_ref, s5_ref,
                 ssd_d_ref, ssd_g_ref, gla_g_ref, s5_d_ref, glu_w_ref, glu_b_ref, w_out_ref, mod_ref, o_ref):
    y = ssd_ref[0] + ssd_b_ref[0] + ssd_d_ref[...] * xs_ref[0]
    ssd = _rms(y * _silu(z_ref[0]), ssd_g_ref[...])
    o = gla_ref[0] + gla_b_ref[0]
    lane_head = lax.broadcasted_iota(jnp.int32, (1, GLA_WIDTH), 1) >> (GLA_DV.bit_length() - 1)
    ms = jnp.zeros_like(o)
    for h in range(GLA_HEADS):
        oh = o[:, h * GLA_DV:(h + 1) * GLA_DV]
        ms = jnp.where(lane_head == h, jnp.mean(oh * oh, axis=-1, keepdims=True), ms)
    gla = o * lax.rsqrt(ms + NORM_EPS) * gla_g_ref[...] * _silu(r_ref[0])
    y5 = _gelu_erf(s5_ref[0] + s5_d_ref[...] * u_ref[0])
    s5 = y5 * jax.nn.sigmoid(_mm(y5.astype(BF16), glu_w_ref[...]) + glu_b_ref[...])
    mix_in = jnp.concatenate([ssd, gla, mla_ref[0], s5], axis=1).astype(BF16)
    o_ref[0] = h_ref[0] + mod_ref[0] * _mm(mix_in, w_out_ref[...])


def _post(h, p, ssd_xbc, ssd_y, ssd_yb, gla_o, gla_ob, mla_y, s5_y, ssd_d, ssd_norm_g, gla_norm_g, s5_d, glu_w, glu_b, w_out, mod,
          row_off, mla_off):
    b, rows, d = h.shape
    w = GROUP_WIDTH
    pblk = lambda col: pl.BlockSpec((1, ROW_TILE, w), lambda i, t: (i, row_off + t, col // w))
    yblk = pl.BlockSpec((1, ROW_TILE, w), lambda i, t: (i, row_off + t, 0))
    full = lambda *shape: pl.BlockSpec(shape, lambda i, t: (0,) * len(shape))
    vec = lambda x: x.reshape(1, -1).astype(F32)
    n_mod = mod.shape[0]
    return pl.pallas_call(
        _post_kernel,
        grid=(b, rows // ROW_TILE),
        in_specs=[pl.BlockSpec((1, ROW_TILE, d), lambda i, t: (i, t, 0)),
                  yblk, pblk(COL_Z), pblk(COL_GLA_R), pblk(COL_S5), yblk, yblk, yblk, yblk,
                  pl.BlockSpec((1, ROW_TILE, w), lambda i, t: (i, mla_off + t, 0)), yblk,
                  full(1, w), full(1, w), full(1, w), full(1, w), full(w, w), full(1, w), full(d, d),
                  pl.BlockSpec((1, 1, d), lambda i, t: (jnp.minimum(i, n_mod - 1), 0, 0))],
        out_specs=pl.BlockSpec((1, ROW_TILE, d), lambda i, t: (i, t, 0)),
        out_shape=jax.ShapeDtypeStruct((b, rows, d), F32),
        compiler_params=pltpu.CompilerParams(dimension_semantics=("arbitrary", "arbitrary")),
        name="mix_post",
    )(h, ssd_xbc, p, p, p, ssd_y, ssd_yb, gla_o, gla_ob, mla_y, s5_y,
      vec(jnp.repeat(ssd_d, SSD_HEAD_DIM)), vec(ssd_norm_g), vec(jnp.tile(gla_norm_g, GLA_HEADS)), vec(s5_d),
      glu_w.astype(BF16), vec(glu_b), w_out.astype(BF16), mod)


PEER_ROUTE_TOKENS = 256
PEER_ROUTE_UNROLL = 4
PEER_GATE_TOKENS = 256
PEER_GATE_UNROLL = 32
PEER_GATE_SUBLANES = 8
PEER_DENSE_TOKENS = 512
PEER_DENSE_EXPERTS = 2 * PEER_GATE_SUBLANES * PEER_KEYS
PEER_SLOTS = PEER_HEADS * PEER_TOPK


def _topk_rows(s, k):
    n_rows = s.shape[0]
    rows = lax.broadcasted_iota(jnp.int32, s.shape, 0)
    vals, idxs = [], []
    for _ in range(k):
        m = jnp.max(s, axis=0, keepdims=True)
        idx = jnp.min(jnp.where(s == m, rows, n_rows), axis=0, keepdims=True)
        vals.append(m)
        idxs.append(idx)
        s = jnp.where(rows == idx, -jnp.inf, s)
    return jnp.concatenate(vals, axis=0), jnp.concatenate(idxs, axis=0)


def _select_rows(pos, table):
    out = jnp.zeros(pos.shape, table.dtype)
    for r in range(table.shape[0]):
        out = jnp.where(pos == r, table[r:r + 1, :], out)
    return out


def _peer_route_kernel(h_ref, g_ref, shift_ref, scale_ref, wq_hi_ref, wq_lo_ref, k_hi_ref, k_lo_ref,
                       xn_ref, i1_ref, i2_ref, gate_ref, q_scr, slot_scr):
    xn = _modulated_norm(h_ref[...], g_ref[...], shift_ref[0], scale_ref[0])
    xn_ref[...] = xn.astype(BF16)
    x_hi, x_lo = _split_bf16(xn)
    q_scr[...] = _dot3(wq_hi_ref[...], wq_lo_ref[...], x_hi, x_lo, _NT_DIMS)
    half = PEER_DQ // 2

    def head_body(h, carry):
        base = pl.multiple_of(h * PEER_DQ, PEER_DQ)
        tops = []
        for j in range(2):
            qq = q_scr[pl.ds(base + j * half, half), :]
            q_hi, q_lo = _split_bf16(qq)
            s = _dot3(k_hi_ref[j, h], k_lo_ref[j, h], q_hi, q_lo, _NN_DIMS)
            tops.append(_topk_rows(s, PEER_TOPK))
        (v1, i1), (v2, i2) = tops
        pieces = [v1[a:a + 1, :] + v2[:PEER_TOPK // (a + 1), :] for a in range(PEER_TOPK)]
        n_cand = sum(PEER_TOPK // (a + 1) for a in range(PEER_TOPK))
        pad = -n_cand % 8
        cand = jnp.concatenate(pieces + [jnp.full((pad, v1.shape[1]), -jnp.inf, F32)], axis=0)
        best, pos = _topk_rows(cand, PEER_TOPK)
        e = jnp.exp(best - best[0:1, :])
        gates = e / jnp.sum(e, axis=0, keepdims=True)
        a_idx = jnp.zeros_like(pos)
        start = jnp.zeros_like(pos)
        first = 0
        for a in range(1, PEER_TOPK):
            width = PEER_TOPK // a
            first += width
            reached = pos >= first
            a_idx = a_idx + jnp.where(reached, 1, 0)
            start = start + jnp.where(reached, width, 0)
        row0 = pl.multiple_of(h * PEER_TOPK, PEER_TOPK)
        slot_scr[0, pl.ds(row0, PEER_TOPK), :] = _select_rows(a_idx, i1).astype(F32)
        slot_scr[1, pl.ds(row0, PEER_TOPK), :] = _select_rows(pos - start, i2).astype(F32)
        slot_scr[2, pl.ds(row0, PEER_TOPK), :] = gates
        return carry

    lax.fori_loop(0, PEER_HEADS, head_body, 0, unroll=PEER_ROUTE_UNROLL)
    i1_ref[...] = slot_scr[0].T.astype(jnp.int32)
    i2_ref[...] = slot_scr[1].T.astype(jnp.int32)
    gate_ref[...] = slot_scr[2].T


def _bf16_bits(x):
    return pltpu.bitcast(x.astype(BF16).astype(F32), jnp.uint32)


def _peer_gate_kernel(i1_ref, i2_ref, gate_ref, g_ref):
    rows = lax.broadcasted_iota(jnp.int32, (PEER_KEYS, PEER_SLOTS), 0)
    sub = PEER_GATE_SUBLANES

    def token_body(t, carry):
        a = i1_ref[pl.ds(t, 1), :]
        b = i2_ref[pl.ds(t, 1), :]
        w = gate_ref[pl.ds(t, 1), :]
        lhs = jnp.where(rows == a, w, 0.0).astype(BF16)
        rhs = jnp.where(rows == b, 1.0, 0.0).astype(BF16)
        gt = _mm(lhs, rhs, _NT_DIMS)
        row0 = pl.multiple_of(t * sub, sub)
        for g in range(PEER_KEYS // (2 * sub)):
            lo = gt[2 * sub * g:2 * sub * g + sub]
            hi = gt[2 * sub * g + sub:2 * sub * (g + 1)]
            g_ref[g, pl.ds(row0, sub), :] = (_bf16_bits(lo) >> 16) | _bf16_bits(hi)
        return carry

    lax.fori_loop(0, i1_ref.shape[0], token_body, 0, unroll=PEER_GATE_UNROLL)


def _peer_dense_kernel(xn_ref, u_ref, v_ref, gpk_ref, h_ref, mod_ref, o_ref, acc_ref):
    j = pl.program_id(1)

    @pl.when(j == 0)
    def _():
        acc_ref[...] = jnp.zeros_like(acc_ref)

    sub = PEER_GATE_SUBLANES
    xn = xn_ref[...]
    tokens = xn.shape[0]
    words = [gpk_ref[0, pl.ds(r, tokens, stride=sub), :] for r in range(sub)]
    for half in range(2):
        rows = slice(half * sub * PEER_KEYS, (half + 1) * sub * PEER_KEYS)
        hid = _gelu_erf(_mm(xn, u_ref[0, rows, :], _NT_DIMS))
        ys = []
        for r in range(sub):
            bits = (words[r] << 16) if half == 0 else (words[r] & jnp.uint32(0xFFFF0000))
            ys.append(pltpu.bitcast(bits, F32) * hid[:, r * PEER_KEYS:(r + 1) * PEER_KEYS])
        y = jnp.concatenate(ys, axis=1).astype(BF16)
        acc_ref[...] += _mm(y, v_ref[0, rows, :])

    @pl.when(j == pl.num_programs(1) - 1)
    def _():
        o_ref[...] = h_ref[...] + mod_ref[0] * acc_ref[...]


def _peer_layer(h, norm_g, shift, scale, gate_mod, wq_t_hi, wq_t_lo, keys_hi, keys_lo, u_bf, v_bf, layer):
    n, d = h.shape
    nb = shift.shape[0]
    rows_per_batch = n // nb
    tr = min(PEER_ROUTE_TOKENS, rows_per_batch)
    full = lambda *shape: pl.BlockSpec(shape, lambda i: (0,) * len(shape))
    per_batch = lambda t: pl.BlockSpec((1, 1, d), lambda i: (i * t // rows_per_batch, 0, 0))
    xn, i1, i2, gate = pl.pallas_call(
        _peer_route_kernel,
        grid=(n // tr,),
        in_specs=[pl.BlockSpec((tr, d), lambda i: (i, 0)), full(1, d), per_batch(tr), per_batch(tr),
                  full(PEER_HEADS * PEER_DQ, d), full(PEER_HEADS * PEER_DQ, d),
                  full(2, PEER_HEADS, PEER_KEYS, PEER_DQ // 2), full(2, PEER_HEADS, PEER_KEYS, PEER_DQ // 2)],
        out_specs=[pl.BlockSpec((tr, d), lambda i: (i, 0))] + [pl.BlockSpec((tr, PEER_SLOTS), lambda i: (i, 0))] * 3,
        out_shape=[jax.ShapeDtypeStruct((n, d), BF16),
                   jax.ShapeDtypeStruct((n, PEER_SLOTS), jnp.int32),
                   jax.ShapeDtypeStruct((n, PEER_SLOTS), jnp.int32),
                   jax.ShapeDtypeStruct((n, PEER_SLOTS), F32)],
        scratch_shapes=[pltpu.VMEM((PEER_HEADS * PEER_DQ, tr), F32), pltpu.VMEM((3, PEER_SLOTS, tr), F32)],
        compiler_params=pltpu.CompilerParams(dimension_semantics=("arbitrary",)),
        name="peer_route",
    )(h, norm_g.reshape(1, d), shift, scale, wq_t_hi, wq_t_lo, keys_hi, keys_lo)

    tg = min(PEER_GATE_TOKENS, n)
    n_planes = PEER_KEYS // (2 * PEER_GATE_SUBLANES)
    slot_spec = pl.BlockSpec((tg, PEER_SLOTS), lambda i: (i, 0))
    gmat = pl.pallas_call(
        _peer_gate_kernel,
        grid=(n // tg,),
        in_specs=[slot_spec, slot_spec, slot_spec],
        out_specs=pl.BlockSpec((n_planes, tg * PEER_GATE_SUBLANES, PEER_KEYS), lambda i: (0, i, 0)),
        out_shape=jax.ShapeDtypeStruct((n_planes, n * PEER_GATE_SUBLANES, PEER_KEYS), jnp.uint32),
        compiler_params=pltpu.CompilerParams(dimension_semantics=("arbitrary",)),
        name="peer_gate",
    )(i1, i2, gate)

    tm = min(PEER_DENSE_TOKENS, rows_per_batch)
    te = PEER_DENSE_EXPERTS
    return pl.pallas_call(
        _peer_dense_kernel,
        grid=(n // tm, PEER_EXPERTS // te),
        in_specs=[pl.BlockSpec((tm, d), lambda i, j: (i, 0)),
                  pl.BlockSpec((1, te, d), lambda i, j: (layer, j, 0)),
                  pl.BlockSpec((1, te, d), lambda i, j: (layer, j, 0)),
                  pl.BlockSpec((1, tm * PEER_GATE_SUBLANES, PEER_KEYS), lambda i, j: (j, i, 0)),
                  pl.BlockSpec((tm, d), lambda i, j: (i, 0)),
                  pl.BlockSpec((1, 1, d), lambda i, j: (i * tm // rows_per_batch, 0, 0))],
        out_specs=pl.BlockSpec((tm, d), lambda i, j: (i, 0)),
        out_shape=jax.ShapeDtypeStruct((n, d), F32),
        scratch_shapes=[pltpu.VMEM((tm, d), F32)],
        compiler_params=pltpu.CompilerParams(dimension_semantics=("arbitrary", "arbitrary"),
                                             vmem_limit_bytes=52 * 2 ** 20),
        name="peer_dense",
    )(xn, u_bf, v_bf, gmat, h, gate_mod)


def _final_norm_kernel(x_ref, g_ref, o_ref):
    o_ref[...] = _rms(x_ref[...], g_ref[...])


def _final_norm(h, g):
    n = h.shape[0] * h.shape[1]
    x2 = h.reshape(n, D_MODEL)
    tm = 512
    out = pl.pallas_call(
        _final_norm_kernel,
        grid=(n // tm,),
        in_specs=[pl.BlockSpec((tm, D_MODEL), lambda i: (i, 0)),
                  pl.BlockSpec((1, D_MODEL), lambda i: (0, 0))],
        out_specs=pl.BlockSpec((tm, D_MODEL), lambda i: (i, 0)),
        out_shape=jax.ShapeDtypeStruct((n, D_MODEL), F32),
        name="final_norm",
    )(x2, g.reshape(1, D_MODEL))
    return out.reshape(h.shape)


def _mix_layer(h_lat, h_ctx, mod_l, mod_c, norm_g, w_in, w_out, ssd, gla, mla, s5, ctx_out):
    b, n_lat, d = h_lat.shape
    n_lat_tiles = n_lat // ROW_TILE
    tab = lambda k: jnp.concatenate([mod_l[k], mod_c[k]], axis=0)
    p = _inproj(h_lat, h_ctx, norm_g, tab(0), tab(1), _pack_w_in(w_in))
    ssd_y, ssd_yb, ssd_xbc = _ssd_mixer(p, ssd["conv_w"], ssd["conv_b"], ssd["a_log"], ssd["dt_bias"], n_lat_tiles)
    gla_o, gla_ob = _gla_mixer(p, gla["gate_w"], gla["gate_b"], n_lat_tiles)
    mla_lat, mla_ctx = _mla_mixer(p, mla["q_norm_g"], mla["w_uq"], mla["kv_norm_g"], mla["w_ukv"], n_lat_tiles, ctx_out)
    s5_y = _s5_mixer(p, s5["a_re"], s5["a_im"], s5["log_dt"], s5["b_re"], s5["b_im"], s5["c_re"], s5["c_im"],
                     n_lat_tiles)
    post = functools.partial(_post, p=p, ssd_xbc=ssd_xbc, ssd_y=ssd_y, ssd_yb=ssd_yb, gla_o=gla_o, gla_ob=gla_ob,
                             s5_y=s5_y, ssd_d=ssd["d"],
                             ssd_norm_g=ssd["norm_g"], gla_norm_g=gla["norm_g"], s5_d=s5["d"],
                             glu_w=s5["glu_w"], glu_b=s5["glu_b"], w_out=w_out)
    new_lat = post(h_lat, mla_y=mla_lat, mod=mod_l[2], row_off=0, mla_off=0)
    new_ctx = None
    if ctx_out:
        new_ctx = post(h_ctx, mla_y=mla_ctx, mod=mod_c[2], row_off=n_lat_tiles, mla_off=0)
    return new_lat, new_ctx


def kernel(x, c, ctx, c_ctx, ada_w, ada_b, norm_mix_g, norm_ffn_g, w_in, w_out,
           ssd_conv_w, ssd_conv_b, ssd_a_log, ssd_dt_bias, ssd_d, ssd_norm_g,
           gla_gate_w, gla_gate_b, gla_norm_g, mla_q_norm_g, mla_w_uq, mla_kv_norm_g, mla_w_ukv,
           s5_a_re, s5_a_im, s5_log_dt, s5_b_re, s5_b_im, s5_c_re, s5_c_im, s5_d, s5_glu_w, s5_glu_b,
           peer_w_q, peer_sub_keys, peer_u, peer_v, final_norm_g):
    h_lat, h_ctx = x, ctx
    cond_lat = jax.nn.silu(c)[:, None, :]
    cond_ctx = jax.nn.silu(c_ctx)[None, None, :]
    u_bf, v_bf = peer_u.astype(BF16), peer_v.astype(BF16)
    for i in range(DEPTH):
        ctx_out = i < DEPTH - 1
        mod_l = jnp.split(cond_lat @ ada_w[i] + ada_b[i], N_MOD, axis=-1)
        mod_c = jnp.split(cond_ctx @ ada_w[i] + ada_b[i], N_MOD, axis=-1)
        ssd = dict(conv_w=ssd_conv_w[i], conv_b=ssd_conv_b[i], a_log=ssd_a_log[i], dt_bias=ssd_dt_bias[i],
                   d=ssd_d[i], norm_g=ssd_norm_g[i])
        gla = dict(gate_w=gla_gate_w[i], gate_b=gla_gate_b[i], norm_g=gla_norm_g[i])
        mla = dict(q_norm_g=mla_q_norm_g[i], w_uq=mla_w_uq[i], kv_norm_g=mla_kv_norm_g[i], w_ukv=mla_w_ukv[i])
        s5 = dict(a_re=s5_a_re[i], a_im=s5_a_im[i], log_dt=s5_log_dt[i], b_re=s5_b_re[i], b_im=s5_b_im[i],
                  c_re=s5_c_re[i], c_im=s5_c_im[i], d=s5_d[i], glu_w=s5_glu_w[i], glu_b=s5_glu_b[i])
        h_lat, h_ctx_new = _mix_layer(h_lat, h_ctx, mod_l, mod_c, norm_mix_g[i], w_in[i], w_out[i],
                                      ssd, gla, mla, s5, ctx_out)
        wq_t_hi, wq_t_lo = _split_bf16(peer_w_q[i].T)
        keys_hi, keys_lo = _split_bf16(peer_sub_keys[i])
        peer = functools.partial(_peer_layer, norm_g=norm_ffn_g[i], wq_t_hi=wq_t_hi, wq_t_lo=wq_t_lo,
                                 keys_hi=keys_hi, keys_lo=keys_lo, u_bf=u_bf, v_bf=v_bf, layer=i)
        h_lat = peer(h_lat.reshape(-1, D_MODEL), shift=mod_l[3], scale=mod_l[4],
                     gate_mod=mod_l[5]).reshape(h_lat.shape)
        if ctx_out:
            h_ctx = peer(h_ctx_new.reshape(-1, D_MODEL), shift=mod_c[3], scale=mod_c[4],
                         gate_mod=mod_c[5]).reshape(h_ctx.shape)
    return _final_norm(h_lat, final_norm_g)
```

```python
import functools
import jax
import jax.numpy as jnp
from jax import lax
import numpy as np
from jax.experimental import pallas as pl
from jax.experimental.pallas import tpu as pltpu

D_MODEL = 1024
DEPTH = 2
GRID_W = 64
NORM_EPS = 1e-6
N_MOD = 6

GROUP_WIDTH = D_MODEL // 4

SSD_WIDTH = GROUP_WIDTH
SSD_HEAD_DIM = 64
SSD_HEADS = SSD_WIDTH // SSD_HEAD_DIM
SSD_GROUPS = 2
SSD_STATE = 128
SSD_CONV = 5
SSD_CHUNK = 128
SSD_CONV_CH = SSD_WIDTH + 2 * SSD_GROUPS * SSD_STATE
SSD_IN = SSD_WIDTH + SSD_CONV_CH + 2 * SSD_HEADS

GLA_WIDTH = GROUP_WIDTH
GLA_HEADS = 4
GLA_DV = GLA_WIDTH // GLA_HEADS
GLA_DK = GLA_DV // 2
GLA_QK = GLA_HEADS * GLA_DK
GLA_GATE_RANK = 16
GLA_TAU = 16.0
GLA_CHUNK = 64
GLA_IN = 2 * GLA_QK + 2 * GLA_WIDTH + 2 * GLA_GATE_RANK

MLA_WIDTH = GROUP_WIDTH
MLA_HEADS = 4
MLA_V = MLA_WIDTH // MLA_HEADS
MLA_NOPE = 64
MLA_ROPE = 32
MLA_Q_RANK = 256
MLA_KV_RANK = 128
MLA_SCALE = (MLA_NOPE + MLA_ROPE) ** -0.5
ROPE_BASE = 10000.0
MLA_IN = MLA_Q_RANK + MLA_KV_RANK + MLA_ROPE

S5_WIDTH = GROUP_WIDTH
S5_GROUP = 16
S5_NGROUPS = S5_WIDTH // S5_GROUP
S5_STATE = 64
S5_MAX_RE = -1e-4
S5_IN = S5_WIDTH
S5_CHUNK = 16
S5_PAIRS = S5_NGROUPS // 2
S5_PACK_ROWS = (768, 256)

PEER_KEYS = 128
PEER_EXPERTS = PEER_KEYS * PEER_KEYS
PEER_HEADS = 8
PEER_TOPK = 16
PEER_DQ = 128

LANES = 128
ROW_TILE = 256
SCAN_STEP_ROWS = ROW_TILE

F32 = jnp.float32
BF16 = jnp.bfloat16

COL_XS, COL_BM, COL_CM, COL_Z = 0, 256, 512, 768
COL_GLA_V, COL_GLA_R, COL_CQ, COL_S5 = 1024, 1280, 1536, 1792
COL_GLA_Q, COL_GLA_K, COL_CKV, COL_DT, COL_GLR, COL_KR, COL_KRROT = 2048, 2176, 2304, 2432, 2560, 2688, 2816
P_COLS = 2944

_NN_DIMS = (((1,), (0,)), ((), ()))
_NT_DIMS = (((1,), (1,)), ((), ()))
_TN_DIMS = (((0,), (0,)), ((), ()))


def _mm(a, b, dims=_NN_DIMS):
    return lax.dot_general(a, b, dims, preferred_element_type=F32)


def _split_bf16(x):
    hi = x.astype(BF16)
    lo = (x - hi.astype(F32)).astype(BF16)
    return hi, lo


def _split3_bf16(x):
    p1 = x.astype(BF16)
    r1 = x - p1.astype(F32)
    p2 = r1.astype(BF16)
    p3 = (r1 - p2.astype(F32)).astype(BF16)
    return p1, p2, p3


def _dot3(a_hi, a_lo, b_hi, b_lo, dims):
    return _mm(a_hi, b_hi, dims) + _mm(a_hi, b_lo, dims) + _mm(a_lo, b_hi, dims)


def _gelu_erf(x):
    return 0.5 * x * (1.0 + lax.erf(x * (2.0 ** -0.5)))


def _silu(x):
    return x * jax.nn.sigmoid(x)


def _softplus(x):
    return jnp.maximum(x, 0.0) + jnp.log1p(jnp.exp(-jnp.abs(x)))


def _log_sigmoid(x):
    return jnp.minimum(x, 0.0) - jnp.log1p(jnp.exp(-jnp.abs(x)))


def _rms(x, g):
    return x * lax.rsqrt(jnp.mean(x * x, axis=-1, keepdims=True) + NORM_EPS) * g


def _modulated_norm(x, g, shift, scale):
    return _rms(x, g) * (1.0 + scale) + shift


def _causal_mask(n, reverse):
    ri = lax.broadcasted_iota(jnp.int32, (n, n), 0)
    ci = lax.broadcasted_iota(jnp.int32, (n, n), 1)
    return (ci >= ri) if reverse else (ci <= ri)


def _scan_chunk(s, n_lat, n_ctx, reverse):
    if reverse:
        return n_lat + n_ctx - 1 - s
    return jnp.where(s < n_ctx, n_lat + s, s - n_ctx)


def _inproj_kernel(lat_ref, ctx_ref, g_ref, shift_ref, scale_ref, w_ref, o_ref, *, n_lat_tiles):
    t = pl.program_id(1)
    x = jnp.where(t < n_lat_tiles, lat_ref[0], ctx_ref[0])
    xn = _modulated_norm(x, g_ref[...], shift_ref[0], scale_ref[0])
    o_ref[0] = _mm(xn.astype(BF16), w_ref[...])


def _inproj(h_lat, h_ctx, norm_g, shift_tab, scale_tab, w_pad):
    b, n_lat, d = h_lat.shape
    n_lat_tiles = n_lat // ROW_TILE
    n_tiles = n_lat_tiles + h_ctx.shape[1] // ROW_TILE
    mod_spec = pl.BlockSpec((1, 1, d), lambda i, t: (jnp.where(t < n_lat_tiles, i, b), 0, 0))
    return pl.pallas_call(
        functools.partial(_inproj_kernel, n_lat_tiles=n_lat_tiles),
        grid=(b, n_tiles),
        in_specs=[pl.BlockSpec((1, ROW_TILE, d), lambda i, t: (i, jnp.minimum(t, n_lat_tiles - 1), 0)),
                  pl.BlockSpec((1, ROW_TILE, d), lambda i, t: (i, jnp.maximum(t - n_lat_tiles, 0), 0)),
                  pl.BlockSpec((1, d), lambda i, t: (0, 0)), mod_spec, mod_spec,
                  pl.BlockSpec((d, P_COLS), lambda i, t: (0, 0))],
        out_specs=pl.BlockSpec((1, ROW_TILE, P_COLS), lambda i, t: (i, t, 0)),
        out_shape=jax.ShapeDtypeStruct((b, n_tiles * ROW_TILE, P_COLS), F32),
        compiler_params=pltpu.CompilerParams(dimension_semantics=("arbitrary", "arbitrary"),
                                             vmem_limit_bytes=48 * 2 ** 20),
        name="inproj",
    )(h_lat, h_ctx, norm_g.reshape(1, d), shift_tab, scale_tab, w_pad)


def _pack_w_in(w):
    o_ssd, o_gla, o_mla, o_s5 = 0, SSD_IN, SSD_IN + GLA_IN, SSD_IN + GLA_IN + MLA_IN
    out = jnp.zeros((w.shape[0], P_COLS), F32)
    put = lambda out, col, src, width: out.at[:, col:col + width].set(w[:, src:src + width])
    out = put(out, COL_Z, o_ssd, SSD_WIDTH)
    out = put(out, COL_XS, o_ssd + SSD_WIDTH, SSD_CONV_CH)
    out = put(out, COL_DT, o_ssd + SSD_WIDTH + SSD_CONV_CH, 2 * SSD_HEADS)
    out = put(out, COL_GLA_Q, o_gla, GLA_QK)
    out = put(out, COL_GLA_K, o_gla + GLA_QK, GLA_QK)
    out = put(out, COL_GLA_V, o_gla + 2 * GLA_QK, GLA_WIDTH)
    out = put(out, COL_GLA_R, o_gla + 2 * GLA_QK + GLA_WIDTH, GLA_WIDTH)
    out = put(out, COL_GLR, o_gla + 2 * GLA_QK + 2 * GLA_WIDTH, 2 * GLA_GATE_RANK)
    out = put(out, COL_CQ, o_mla, MLA_Q_RANK)
    out = put(out, COL_CKV, o_mla + MLA_Q_RANK, MLA_KV_RANK)
    o_kr = o_mla + MLA_Q_RANK + MLA_KV_RANK
    half = MLA_ROPE // 2
    out = put(out, COL_KR + MLA_NOPE, o_kr, MLA_ROPE)
    out = out.at[:, COL_KRROT + MLA_NOPE:COL_KRROT + MLA_NOPE + half].set(-w[:, o_kr + half:o_kr + MLA_ROPE])
    out = out.at[:, COL_KRROT + MLA_NOPE + half:COL_KRROT + MLA_NOPE + MLA_ROPE].set(w[:, o_kr:o_kr + half])
    out = put(out, COL_S5, o_s5, S5_WIDTH)
    return out.astype(BF16)


def _ssd_prep_kernel(x_ref, prev_ref, next_ref, dt_ref, w_ref, b_ref, bias_ref, xbc_ref, dtc_ref, dtt_ref,
                     *, n_lat_tiles):
    t = pl.program_id(1)
    x = x_ref[0]
    halo = prev_ref.shape[1]
    prev = jnp.where(jnp.logical_and(t > 0, t < n_lat_tiles), prev_ref[0], 0.0)
    nxt = jnp.where(t < n_lat_tiles - 1, next_ref[0], 0.0)
    ext = jnp.concatenate([prev, x, nxt], axis=0)
    rows = ext.shape[0]
    left = SSD_CONV // 2
    acc = jnp.zeros_like(x) + b_ref[...]
    for k in range(SSD_CONV):
        shifted = ext if k == left else pltpu.roll(ext, (left - k) % rows, 0)
        acc = acc + w_ref[k:k + 1, :] * shifted[halo:halo + x.shape[0]]
    xbc_ref[0] = _silu(acc)
    dt = _softplus(dt_ref[0] + bias_ref[...])
    dtc_ref[0] = dt
    dtt_ref[0] = dt.T[:dtt_ref.shape[1]]


def _ssd_scan_kernel(xbc_f_ref, dtc_f_ref, dtt_f_ref, xbc_b_ref, dtc_b_ref, dtt_b_ref, ahr_ref, ahc_ref,
                     yf_ref, yb_ref, state_ref):
    @pl.when(pl.program_id(0) == 0)
    def _():
        state_ref[...] = jnp.zeros_like(state_ref)

    q = SSD_CHUNK
    n_sub = xbc_f_ref.shape[1] // q
    for k in range(n_sub):
        for direction, (xbc_ref, dtc_ref, dtt_ref, y_ref) in enumerate(
                ((xbc_f_ref, dtc_f_ref, dtt_f_ref, yf_ref), (xbc_b_ref, dtc_b_ref, dtt_b_ref, yb_ref))):
            c = k if direction == 0 else n_sub - 1 - k
            rows = slice(c * q, (c + 1) * q)
            for bi in range(xbc_ref.shape[0]):
                y_ref[bi, rows] = _ssd_chunk(xbc_ref[bi, rows], dtc_ref[bi, rows], dtt_ref[bi, :, rows], ahr_ref,
                                             ahc_ref, state_ref.at[bi, direction], direction)


def _ssd_chunk(xbc, dtc, dtt, ahr_ref, ahc_ref, state_ref, direction):
    reverse = direction == 1
    q = SSD_CHUNK
    mask = _causal_mask(q, reverse)
    tri = jnp.where(mask, 1.0, 0.0).astype(BF16)
    xs, bm, cm = xbc[:, :SSD_WIDTH], xbc[:, SSD_WIDTH:SSD_WIDTH + 256], xbc[:, SSD_WIDTH + 256:]
    a_col = dtc * ahr_ref[...]
    a_row = dtt * ahc_ref[...]
    acum_col = sum(_mm(tri, part) for part in _split3_bf16(a_col))
    acum_row = sum(_mm(part, tri, _NT_DIMS) for part in _split3_bf16(a_row))
    end = 0 if reverse else q - 1
    bm_bf, cm_bf = bm.astype(BF16), cm.astype(BF16)
    ys = []
    cb = {}
    for h in range(SSD_HEADS):
        g = h // (SSD_HEADS // SSD_GROUPS)
        gs = slice(g * SSD_STATE, (g + 1) * SSD_STATE)
        if g not in cb:
            cb[g] = _mm(cm_bf[:, gs], bm_bf[:, gs], _NT_DIMS)
        ch = direction * SSD_HEADS + h
        ac = acum_col[:, ch:ch + 1]
        ar = acum_row[ch:ch + 1, :]
        decay = jnp.exp(jnp.where(mask, ac - ar, -jnp.inf))
        xd = xs[:, h * SSD_HEAD_DIM:(h + 1) * SSD_HEAD_DIM] * dtc[:, ch:ch + 1]
        y_diag = _mm((cb[g] * decay).astype(BF16), xd.astype(BF16))
        a_end = ac[end:end + 1, :]
        st_local = _mm((xd * jnp.exp(a_end - ac)).astype(BF16), bm_bf[:, gs], _TN_DIMS)
        hs = state_ref[h]
        y_off = jnp.exp(ac) * _mm(cm_bf[:, gs], hs.astype(BF16), _NT_DIMS)
        state_ref[h] = jnp.exp(a_end) * hs + st_local
        ys.append(y_diag + y_off)
    return jnp.concatenate(ys, axis=1)


def _ssd_mixer(p, conv_w, conv_b, a_log, dt_bias, n_lat_tiles):
    b, r, _ = p.shape
    nt = r // ROW_TILE
    halo = 8
    hb = ROW_TILE // halo
    w8 = jnp.zeros((8, SSD_CONV_CH), F32).at[:SSD_CONV].set(conv_w)
    bias = jnp.zeros((1, LANES), F32).at[0, :2 * SSD_HEADS].set(dt_bias.reshape(-1))
    xbc, dtc, dtt = pl.pallas_call(
        functools.partial(_ssd_prep_kernel, n_lat_tiles=n_lat_tiles),
        grid=(b, nt),
        in_specs=[pl.BlockSpec((1, ROW_TILE, SSD_CONV_CH), lambda i, t: (i, t, 0)),
                  pl.BlockSpec((1, halo, SSD_CONV_CH), lambda i, t: (i, jnp.maximum(t * hb - 1, 0), 0)),
                  pl.BlockSpec((1, halo, SSD_CONV_CH), lambda i, t: (i, jnp.minimum((t + 1) * hb, nt * hb - 1), 0)),
                  pl.BlockSpec((1, ROW_TILE, LANES), lambda i, t: (i, t, COL_DT // LANES)),
                  pl.BlockSpec((8, SSD_CONV_CH), lambda i, t: (0, 0)),
                  pl.BlockSpec((1, SSD_CONV_CH), lambda i, t: (0, 0)),
                  pl.BlockSpec((1, LANES), lambda i, t: (0, 0))],
        out_specs=[pl.BlockSpec((1, ROW_TILE, SSD_CONV_CH), lambda i, t: (i, t, 0)),
                   pl.BlockSpec((1, ROW_TILE, LANES), lambda i, t: (i, t, 0)),
                   pl.BlockSpec((1, 8, ROW_TILE), lambda i, t: (i, 0, t))],
        out_shape=[jax.ShapeDtypeStruct((b, r, SSD_CONV_CH), F32),
                   jax.ShapeDtypeStruct((b, r, LANES), F32),
                   jax.ShapeDtypeStruct((b, 8, r), F32)],
        compiler_params=pltpu.CompilerParams(dimension_semantics=("arbitrary", "arbitrary")),
        name="ssd_prep",
    )(p, p, p, p, w8, conv_b.reshape(1, -1), bias)

    a_head = -jnp.exp(a_log.astype(F32)).reshape(-1)
    ahr = jnp.zeros((1, LANES), F32).at[0, :2 * SSD_HEADS].set(a_head)
    ahc = a_head.reshape(2 * SSD_HEADS, 1)
    blk = SCAN_STEP_ROWS
    n_lat = n_lat_tiles * ROW_TILE // blk
    n_ctx = r // blk - n_lat
    in_specs, y_specs = [], []
    for reverse in (False, True):
        cidx = functools.partial(_scan_chunk, n_lat=n_lat, n_ctx=n_ctx, reverse=reverse)
        in_specs += [pl.BlockSpec((b, blk, SSD_CONV_CH), lambda s, cidx=cidx: (0, cidx(s), 0)),
                     pl.BlockSpec((b, blk, LANES), lambda s, cidx=cidx: (0, cidx(s), 0)),
                     pl.BlockSpec((b, 8, blk), lambda s, cidx=cidx: (0, 0, cidx(s)))]
        y_specs.append(pl.BlockSpec((b, blk, SSD_WIDTH), lambda s, cidx=cidx: (0, cidx(s), 0)))
    in_specs += [pl.BlockSpec((1, LANES), lambda s: (0, 0)), pl.BlockSpec((2 * SSD_HEADS, 1), lambda s: (0, 0))]
    y_f, y_b = pl.pallas_call(
        _ssd_scan_kernel,
        grid=(n_lat + n_ctx,),
        in_specs=in_specs,
        out_specs=y_specs,
        out_shape=[jax.ShapeDtypeStruct((b, r, SSD_WIDTH), F32)] * 2,
        scratch_shapes=[pltpu.VMEM((b, 2, SSD_HEADS, SSD_HEAD_DIM, SSD_STATE), F32)],
        compiler_params=pltpu.CompilerParams(dimension_semantics=("arbitrary",)),
        name="ssd_scan",
    )(xbc, dtc, dtt, xbc, dtc, dtt, ahr, ahc)
    return y_f, y_b, xbc


def _gla_scan_kernel(qf_ref, kf_ref, vf_ref, glrf_ref, qb_ref, kb_ref, vb_ref, glrb_ref, wg_ref, bias_ref,
                     of_ref, ob_ref, st_ref):
    @pl.when(pl.program_id(0) == 0)
    def _():
        st_ref[...] = jnp.zeros_like(st_ref)

    n = GLA_CHUNK
    n_sub = qf_ref.shape[1] // n
    for kk in range(n_sub):
        for direction, (q_ref, k_ref, v_ref, glr_ref, o_ref) in enumerate(
                ((qf_ref, kf_ref, vf_ref, glrf_ref, of_ref), (qb_ref, kb_ref, vb_ref, glrb_ref, ob_ref))):
            c = kk if direction == 0 else n_sub - 1 - kk
            rows = slice(c * n, (c + 1) * n)
            for bi in range(q_ref.shape[0]):
                o_ref[bi, rows] = _gla_chunk(q_ref[bi, rows], k_ref[bi, rows], v_ref[bi, rows], glr_ref[bi, rows],
                                             wg_ref.at[direction], bias_ref.at[direction],
                                             st_ref.at[bi, direction], direction == 1)


def _gla_chunk(q, k, v, glr, wg_ref, bias_ref, st_ref, reverse):
    n = GLA_CHUNK
    mask = _causal_mask(n, reverse)
    tri = jnp.where(mask, 1.0, 0.0).astype(BF16)
    g_hi, g_lo = _split_bf16(glr)
    logits = _dot3(g_hi, g_lo, wg_ref[0], wg_ref[1], _NN_DIMS) + bias_ref[...]
    logg = _log_sigmoid(logits) * (1.0 / GLA_TAU)
    bcum = sum(_mm(tri, part) for part in _split3_bf16(logg))
    end = 0 if reverse else n - 1
    b_end = bcum[end:end + 1, :]
    qe = q * jnp.exp(bcum) * (GLA_DK ** -0.5)
    ke = (k * jnp.exp(-bcum)).astype(BF16)
    kd = k * jnp.exp(b_end - bcum)
    decay_end = jnp.exp(b_end)
    lane_head = lax.broadcasted_iota(jnp.int32, (1, GLA_QK), 1) >> (GLA_DK.bit_length() - 1)
    outs = []
    for h in range(GLA_HEADS):
        hm = lane_head == h
        qh = jnp.where(hm, qe, 0.0).astype(BF16)
        att = jnp.where(mask, _mm(qh, ke, _NT_DIMS), 0.0)
        vh = v[:, h * GLA_DV:(h + 1) * GLA_DV].astype(BF16)
        st = st_ref[h]
        o_h = _mm(att.astype(BF16), vh) + _mm(qh, st.astype(BF16), _NT_DIMS)
        local = _mm(vh, jnp.where(hm, kd, 0.0).astype(BF16), _TN_DIMS)
        st_ref[h] = st * decay_end + local
        outs.append(o_h)
    return jnp.concatenate(outs, axis=1)


def _gla_mixer(p, gate_w, gate_b, n_lat_tiles):
    b, r, _ = p.shape
    rows = SCAN_STEP_ROWS
    n_lat = n_lat_tiles * ROW_TILE // rows
    n_ctx = r // rows - n_lat
    in_specs, o_specs, wgs = [], [], []
    for direction in (0, 1):
        cidx = functools.partial(_scan_chunk, n_lat=n_lat, n_ctx=n_ctx, reverse=direction == 1)
        wg = jnp.zeros((LANES, GLA_QK), F32).at[direction * GLA_GATE_RANK:(direction + 1) * GLA_GATE_RANK].set(
            gate_w[direction])
        wgs.append(jnp.stack(_split_bf16(wg)))
        blk = lambda width, col, cidx=cidx: pl.BlockSpec((b, rows, width), lambda s: (0, cidx(s), col // width))
        in_specs += [blk(GLA_QK, COL_GLA_Q), blk(GLA_QK, COL_GLA_K), blk(GLA_WIDTH, COL_GLA_V), blk(LANES, COL_GLR)]
        o_specs.append(pl.BlockSpec((b, rows, GLA_WIDTH), lambda s, cidx=cidx: (0, cidx(s), 0)))
    in_specs += [pl.BlockSpec((2, 2, LANES, GLA_QK), lambda s: (0, 0, 0, 0)),
                 pl.BlockSpec((2, 1, GLA_QK), lambda s: (0, 0, 0))]
    return pl.pallas_call(
        _gla_scan_kernel,
        grid=(n_lat + n_ctx,),
        in_specs=in_specs,
        out_specs=o_specs,
        out_shape=[jax.ShapeDtypeStruct((b, r, GLA_WIDTH), F32)] * 2,
        scratch_shapes=[pltpu.VMEM((b, 2, GLA_HEADS, GLA_DV, GLA_QK), F32)],
        compiler_params=pltpu.CompilerParams(dimension_semantics=("arbitrary",)),
        name="gla_scan",
    )(p, p, p, p, p, p, p, p, jnp.stack(wgs), gate_b.reshape(2, 1, GLA_QK))


MLA_Q_TILE = 1024
MLA_K_TILES = (768, 256)


def _mla_prep_kernel(cq_ref, ckv_ref, kr_ref, krrot_ref, onec_ref, sinr_ref, gq_ref, gkv_ref,
                     wq_ref, wqr_ref, wk_ref, wv_ref, q_ref, k_ref, v_ref):
    qn = _rms(cq_ref[0], gq_ref[...]).astype(BF16)
    kvn = _rms(ckv_ref[0], gkv_ref[...]).astype(BF16)
    onec, sinr = onec_ref[...], sinr_ref[...]
    k_rope = kr_ref[0] * onec + krrot_ref[0] * sinr
    ones_lane = jnp.where(lax.broadcasted_iota(jnp.int32, (1, LANES), 1) == MLA_V, 1.0, 0.0)
    for h in range(MLA_HEADS):
        qh = _mm(qn, wq_ref[h]) * onec + _mm(qn, wqr_ref[h]) * sinr
        q_ref[0, h] = (qh * MLA_SCALE).astype(BF16)
        k_ref[0, h] = (_mm(kvn, wk_ref[h]) + k_rope).astype(BF16)
        v_ref[0, h] = (_mm(kvn, wv_ref[h]) + ones_lane).astype(BF16)


def _mla_attn_kernel(q_ref, k_ref, v_ref, o_ref, m_ref, acc_ref):
    j = pl.program_id(2)

    @pl.when(j == 0)
    def _():
        m_ref[...] = jnp.full_like(m_ref, -jnp.inf)
        acc_ref[...] = jnp.zeros_like(acc_ref)

    reps = k_ref.shape[2] // LANES
    for h in range(MLA_HEADS):
        s = _mm(q_ref[0, h], k_ref[0, h], _NT_DIMS)
        m_prev = m_ref[h]
        m_new = jnp.maximum(m_prev, jnp.max(s, axis=1, keepdims=True))
        p = jnp.exp((s - jnp.concatenate([m_new] * reps, axis=1)).astype(BF16))
        acc_ref[h] = jnp.exp(m_prev - m_new) * acc_ref[h] + _mm(p, v_ref[0, h])
        m_ref[h] = m_new

    @pl.when(j == pl.num_programs(2) - 1)
    def _():
        outs = []
        for h in range(MLA_HEADS):
            acc = acc_ref[h]
            outs.append(acc[:, :MLA_V] / acc[:, MLA_V:MLA_V + 1])
        o_ref[0] = jnp.concatenate(outs, axis=1)


def _rope_tables(n_lat, n_rows):
    rows = n_lat // GRID_W
    row = jnp.repeat(jnp.arange(rows, dtype=F32), GRID_W)
    col = jnp.tile(jnp.arange(GRID_W, dtype=F32), rows)
    half = MLA_ROPE // 2
    inv = ROPE_BASE ** (-jnp.arange(0, half, 2, dtype=F32) / half)
    ang = jnp.concatenate([row[:, None] * inv, col[:, None] * inv], axis=-1)
    cos = jnp.concatenate([jnp.cos(ang), jnp.ones((n_rows - n_lat, half), F32)], axis=0)
    sin = jnp.concatenate([jnp.sin(ang), jnp.zeros((n_rows - n_lat, half), F32)], axis=0)
    pad = jnp.zeros((n_rows, LANES - MLA_NOPE - MLA_ROPE), F32)
    onec = jnp.concatenate([jnp.ones((n_rows, MLA_NOPE), F32), cos, cos, pad], axis=1)
    sinr = jnp.concatenate([jnp.zeros((n_rows, MLA_NOPE), F32), sin, sin, pad], axis=1)
    return onec, sinr


def _mla_weights(w_uq, w_ukv):
    dqk = MLA_NOPE + MLA_ROPE
    half = MLA_ROPE // 2
    wq = w_uq.reshape(MLA_Q_RANK, MLA_HEADS, dqk).transpose(1, 0, 2)
    rot = jnp.concatenate([jnp.zeros_like(wq[..., :MLA_NOPE]), -wq[..., MLA_NOPE + half:], wq[..., MLA_NOPE:MLA_NOPE + half]],
                          axis=-1)
    padq = lambda w: jnp.pad(w, ((0, 0), (0, 0), (0, LANES - dqk))).astype(BF16)
    wkv = w_ukv.reshape(MLA_KV_RANK, MLA_HEADS, MLA_NOPE + MLA_V).transpose(1, 0, 2)
    padk = lambda w: jnp.pad(w, ((0, 0), (0, 0), (0, LANES - w.shape[-1]))).astype(BF16)
    return padq(wq), padq(rot), padk(wkv[..., :MLA_NOPE]), padk(wkv[..., MLA_NOPE:])


def _mla_attention(q, k, v, q_tile, q_off, n_q, kt, k_off, n_k):
    b = q.shape[0]
    return pl.pallas_call(
        _mla_attn_kernel,
        grid=(b, n_q, n_k),
        in_specs=[pl.BlockSpec((1, MLA_HEADS, q_tile, LANES), lambda i, a, j: (i, 0, q_off + a, 0)),
                  pl.BlockSpec((1, MLA_HEADS, kt, LANES), lambda i, a, j: (i, 0, k_off + j, 0)),
                  pl.BlockSpec((1, MLA_HEADS, kt, LANES), lambda i, a, j: (i, 0, k_off + j, 0))],
        out_specs=pl.BlockSpec((1, q_tile, MLA_WIDTH), lambda i, a, j: (i, a, 0)),
        out_shape=jax.ShapeDtypeStruct((b, n_q * q_tile, MLA_WIDTH), F32),
        scratch_shapes=[pltpu.VMEM((MLA_HEADS, q_tile, LANES), F32), pltpu.VMEM((MLA_HEADS, q_tile, LANES), F32)],
        compiler_params=pltpu.CompilerParams(dimension_semantics=("arbitrary", "arbitrary", "arbitrary")),
        name="mla_attn",
    )(q, k, v)


def _mla_mixer(p, q_norm_g, w_uq, kv_norm_g, w_ukv, n_lat_tiles, ctx_out):
    b, r, _ = p.shape
    nt = r // ROW_TILE
    n_lat = n_lat_tiles * ROW_TILE
    onec, sinr = _rope_tables(n_lat, r)
    wq, wqr, wk, wv = _mla_weights(w_uq, w_ukv)
    blk = lambda width, col: pl.BlockSpec((1, ROW_TILE, width), lambda i, t: (i, t, col // width))
    tab = pl.BlockSpec((ROW_TILE, LANES), lambda i, t: (t, 0))
    full = lambda *shape: pl.BlockSpec(shape, lambda i, t: (0,) * len(shape))
    head_out = pl.BlockSpec((1, MLA_HEADS, ROW_TILE, LANES), lambda i, t: (i, 0, t, 0))
    q, k, v = pl.pallas_call(
        _mla_prep_kernel,
        grid=(b, nt),
        in_specs=[blk(MLA_Q_RANK, COL_CQ), blk(LANES, COL_CKV), blk(LANES, COL_KR), blk(LANES, COL_KRROT), tab, tab,
                  full(1, MLA_Q_RANK), full(1, MLA_KV_RANK),
                  full(MLA_HEADS, MLA_Q_RANK, LANES), full(MLA_HEADS, MLA_Q_RANK, LANES),
                  full(MLA_HEADS, MLA_KV_RANK, LANES), full(MLA_HEADS, MLA_KV_RANK, LANES)],
        out_specs=[head_out] * 3,
        out_shape=[jax.ShapeDtypeStruct((b, MLA_HEADS, r, LANES), BF16)] * 3,
        compiler_params=pltpu.CompilerParams(dimension_semantics=("arbitrary", "arbitrary")),
        name="mla_prep",
    )(p, p, p, p, onec, sinr, q_norm_g.reshape(1, -1), kv_norm_g.reshape(1, -1), wq, wqr, wk, wv)
    q_tile = min(MLA_Q_TILE, n_lat)
    k_tile = next(t for t in MLA_K_TILES if r % t == 0)
    y_lat = _mla_attention(q, k, v, q_tile, 0, n_lat // q_tile, k_tile, 0, r // k_tile)
    y_ctx = None
    if ctx_out:
        n_ctx = r - n_lat
        y_ctx = _mla_attention(q, k, v, n_ctx, n_lat // n_ctx, 1, n_ctx, n_lat // n_ctx, 1)
    return y_lat, y_ctx


def _s5_matrices(a_re, a_im, log_dt, b_re, b_im, c_re, c_im):
    q, ng, ns, nc = S5_CHUNK, S5_NGROUPS, S5_STATE, S5_GROUP
    lam = jnp.minimum(a_re.astype(F32), S5_MAX_RE) + 1j * a_im.astype(F32)
    step = jnp.exp(log_dt.astype(F32))[..., None]
    abar = jnp.exp(lam * step)
    bmat = b_re.astype(F32) + 1j * b_im.astype(F32)
    bbar = ((abar - 1.0) / lam)[..., None] * bmat
    cmat = c_re.astype(F32) + 1j * c_im.astype(F32)
    pw = jnp.exp((lam * step)[..., None] * jnp.arange(q + 1, dtype=F32))
    kern = jnp.einsum('dgcn,dgnl,dgnk->dglck', cmat, pw[..., :q], bbar).real
    ii = jnp.arange(q)
    lag_f = ii[None, :] - ii[:, None]
    gather = lambda kd, lag: jnp.where((lag >= 0)[None, :, :, None, None], kd[:, jnp.clip(lag, 0, q - 1)], 0.0)
    t_f = gather(kern[0], lag_f).transpose(0, 1, 4, 2, 3)
    t_b = gather(kern[1], -lag_f).transpose(0, 1, 4, 2, 3)
    t_sum = (t_f + t_b).reshape(ng, q * nc, q * nc)
    pw_f = pw[0][..., q - 1 - ii]
    pw_b = pw[1][..., ii]
    wst = lambda pwd, bb: jnp.einsum('gnj,gnc->gjcn', pwd, bb).reshape(ng, q * nc, ns)
    wst_f, wst_b = wst(pw_f, bbar[0]), wst(pw_b, bbar[1])
    wout = lambda pwd, cm: jnp.einsum('gcn,gni->gnic', cm, pwd).reshape(ng, ns, q * nc)
    wo_f, wo_b = wout(pw[0][..., ii + 1], cmat[0]), wout(pw[1][..., q - ii], cmat[1])
    aq = pw[..., q]

    def pair_cols(x):
        x = x.reshape(S5_PAIRS, 2, x.shape[1], x.shape[2])
        z = jnp.zeros_like(x[:, 0])
        return jnp.concatenate([jnp.concatenate([x[:, 0], z], axis=2), jnp.concatenate([z, x[:, 1]], axis=2)], axis=1)

    w_local = jnp.concatenate([pair_cols(wst_f.real), pair_cols(wst_f.imag),
                               pair_cols(wst_b.real), pair_cols(wst_b.imag)], axis=2)
    w_out = jnp.concatenate([pair_cols(t_sum), pair_cols(wo_f.real), pair_cols(-wo_f.imag),
                             pair_cols(wo_b.real), pair_cols(-wo_b.imag)], axis=1)
    aq_pair = aq.reshape(2, S5_PAIRS, 2 * ns)
    aq_tab = jnp.concatenate([aq_pair[0].real, aq_pair[0].imag, aq_pair[1].real, aq_pair[1].imag], axis=1)
    return w_local.astype(BF16), w_out.astype(BF16), aq_tab.reshape(S5_PAIRS, 1, 8 * ns).astype(F32)


def _s5_perm():
    cols = S5_CHUNK * S5_WIDTH
    c = jnp.arange(cols, dtype=jnp.int32)
    cc, j = c % S5_GROUP, (c // S5_GROUP) % S5_CHUNK
    g = c // (S5_GROUP * S5_CHUNK)
    per_half = LANES // S5_GROUP
    src = (g // per_half) * (S5_CHUNK * LANES) + j * LANES + (g % per_half) * S5_GROUP + cc
    return jnp.where(c[:, None] == src[None, :], 1.0, 0.0).astype(BF16)


def _s5_pack_kernel(u_ref, o_ref):
    n = o_ref.shape[1]
    for j in range(S5_CHUNK):
        o_ref[0, :, j * LANES:(j + 1) * LANES] = u_ref[0, pl.ds(j, n, stride=S5_CHUNK), :].astype(BF16)


def _s5_unpack_kernel(y_ref, o_ref):
    n = y_ref.shape[1]
    for i in range(S5_CHUNK):
        o_ref[0, pl.ds(i, n, stride=S5_CHUNK), :] = y_ref[0, :, i * LANES:(i + 1) * LANES]


def _s5_local_kernel(u_ref, perm_ref, w_ref, up_ref, s_ref):
    up = _mm(u_ref[0], perm_ref[...]).astype(BF16)
    up_ref[0] = up
    s_ref[0] = _mm(up, w_ref[0])


def _s5_scan_kernel(s3_ref, aq_ref, hs3_ref, *, n_lat, n_ctx, nb):
    s_ref, hs_ref = s3_ref.at[0], hs3_ref.at[0]
    w = 2 * S5_STATE
    aq = aq_ref[0]
    a = [aq[:, i * w:(i + 1) * w] for i in range(4)]
    zero = jnp.zeros((nb, w), F32)
    slab = 8
    cps = slab // nb

    def run_slab(s_re, s_im, a_re, a_im, h_re, h_im, order):
        ent_re, ent_im = [None] * cps, [None] * cps
        for c in order:
            ent_re[c], ent_im[c] = h_re, h_im
            rows = slice(c * nb, (c + 1) * nb)
            h_re, h_im = a_re * h_re - a_im * h_im + s_re[rows], a_re * h_im + a_im * h_re + s_im[rows]
        return jnp.concatenate(ent_re, axis=0), jnp.concatenate(ent_im, axis=0), h_re, h_im

    def body(kk, carry):
        f_re, f_im, b_re, b_im = carry
        rf = pl.multiple_of(_scan_chunk(kk, n_lat // cps, n_ctx // cps, False) * slab, slab)
        rb = pl.multiple_of(_scan_chunk(kk, n_lat // cps, n_ctx // cps, True) * slab, slab)
        e_re, e_im, f_re, f_im = run_slab(s_ref[pl.ds(rf, slab), 0:w], s_ref[pl.ds(rf, slab), w:2 * w],
                                          a[0], a[1], f_re, f_im, range(cps))
        hs_ref[pl.ds(rf, slab), 0:w] = e_re
        hs_ref[pl.ds(rf, slab), w:2 * w] = e_im
        e_re, e_im, b_re, b_im = run_slab(s_ref[pl.ds(rb, slab), 2 * w:3 * w], s_ref[pl.ds(rb, slab), 3 * w:4 * w],
                                          a[2], a[3], b_re, b_im, range(cps - 1, -1, -1))
        hs_ref[pl.ds(rb, slab), 2 * w:3 * w] = e_re
        hs_ref[pl.ds(rb, slab), 3 * w:4 * w] = e_im
        return f_re, f_im, b_re, b_im

    lax.fori_loop(0, (n_lat + n_ctx) // cps, body, (zero, zero, zero, zero))


def _s5_out_kernel(up_ref, hs_ref, w_ref, perm_ref, y_ref):
    @pl.when(pl.program_id(1) == 0)
    def _():
        y_ref[...] = jnp.zeros_like(y_ref)

    kw = up_ref.shape[2]
    y_pair = _mm(up_ref[0], w_ref[0, :kw]) + _mm(hs_ref[0].astype(BF16), w_ref[0, kw:])
    y_hi, y_lo = _split_bf16(y_pair)
    y_ref[0] += _mm(y_hi, perm_ref[...], _NT_DIMS) + _mm(y_lo, perm_ref[...], _NT_DIMS)


def _s5_mixer(p, a_re, a_im, log_dt, b_re, b_im, c_re, c_im, n_lat_tiles):
    b, r, _ = p.shape
    q = S5_CHUNK
    n_chunks = r // q
    cols = q * S5_WIDTH
    kw = 2 * q * S5_GROUP
    w_local, w_out, aq_tab = _s5_matrices(a_re, a_im, log_dt, b_re, b_im, c_re, c_im)
    perm = _s5_perm()
    cp2 = pltpu.CompilerParams(dimension_semantics=("arbitrary", "arbitrary"), vmem_limit_bytes=48 * 2 ** 20)
    pack_rows = next(t for t in S5_PACK_ROWS if r % t == 0)
    cpt = pack_rows // q
    halves = S5_WIDTH // LANES
    cp3 = pltpu.CompilerParams(dimension_semantics=("arbitrary", "arbitrary", "arbitrary"))
    chunk_rows = pl.BlockSpec((1, cpt, q * LANES), lambda i, t, hf: (i, t, hf))
    u_big = pl.pallas_call(
        _s5_pack_kernel,
        grid=(b, r // pack_rows, halves),
        in_specs=[pl.BlockSpec((1, pack_rows, LANES), lambda i, t, hf: (i, t, COL_S5 // LANES + hf))],
        out_specs=chunk_rows,
        out_shape=jax.ShapeDtypeStruct((b, n_chunks, cols), BF16),
        compiler_params=cp3, name="s5_pack",
    )(p)
    all_chunks = pl.BlockSpec((1, n_chunks, cols), lambda i, g: (i, 0, 0))
    col_tile = lambda width: pl.BlockSpec((1, n_chunks, width), lambda i, g: (i, 0, g))
    perm_cols = pl.BlockSpec((cols, kw), lambda i, g: (0, g))
    u_pairs, s_loc = pl.pallas_call(
        _s5_local_kernel,
        grid=(b, S5_PAIRS),
        in_specs=[all_chunks, perm_cols, pl.BlockSpec((1, kw, kw), lambda i, g: (g, 0, 0))],
        out_specs=[col_tile(kw), col_tile(kw)],
        out_shape=[jax.ShapeDtypeStruct((b, n_chunks, cols), BF16), jax.ShapeDtypeStruct((b, n_chunks, cols), F32)],
        compiler_params=cp2, name="s5_local",
    )(u_big, perm, w_local)
    n_lat = n_lat_tiles * ROW_TILE // q
    hs = pl.pallas_call(
        functools.partial(_s5_scan_kernel, n_lat=n_lat, n_ctx=n_chunks - n_lat, nb=1),
        grid=(b, S5_PAIRS),
        in_specs=[col_tile(kw), pl.BlockSpec((1, 1, kw), lambda i, g: (g, 0, 0))],
        out_specs=col_tile(kw),
        out_shape=jax.ShapeDtypeStruct((b, n_chunks, cols), F32),
        compiler_params=cp2, name="s5_scan",
    )(s_loc, aq_tab)
    y_big = pl.pallas_call(
        _s5_out_kernel,
        grid=(b, S5_PAIRS),
        in_specs=[col_tile(kw), col_tile(kw), pl.BlockSpec((1, 2 * kw, kw), lambda i, g: (g, 0, 0)), perm_cols],
        out_specs=all_chunks,
        out_shape=jax.ShapeDtypeStruct((b, n_chunks, cols), F32),
        compiler_params=cp2, name="s5_out",
    )(u_pairs, hs, w_out, perm)
    return pl.pallas_call(
        _s5_unpack_kernel,
        grid=(b, r // pack_rows, halves),
        in_specs=[chunk_rows],
        out_specs=pl.BlockSpec((1, pack_rows, LANES), lambda i, t, hf: (i, t, hf)),
        out_shape=jax.ShapeDtypeStruct((b, r, S5_WIDTH), F32),
        compiler_params=cp3, name="s5_unpack",
    )(y_big)


def _post_kernel(h_ref, xs_ref, z_ref, r_ref, u_ref, ssd_ref, ssd_b_ref, gla_ref, gla_b_ref, mla_ref, s5_ref,
                 ssd_d_ref, ssd_g_ref, gla_g_ref, s5_d_ref, glu_w_ref, glu_b_ref, w_out_ref, mod_ref, o_ref):
    y = ssd_ref[0] + ssd_b_ref[0] + ssd_d_ref[...] * xs_ref[0]
    ssd = _rms(y * _silu(z_ref[0]), ssd_g_ref[...])
    o = gla_ref[0] + gla_b_ref[0]
    lane_head = lax.broadcasted_iota(jnp.int32, (1, GLA_WIDTH), 1) >> (GLA_DV.bit_length() - 1)
    ms = jnp.zeros_like(o)
    for h in range(GLA_HEADS):
        oh = o[:, h * GLA_DV:(h + 1) * GLA_DV]
        ms = jnp.where(lane_head == h, jnp.mean(oh * oh, axis=-1, keepdims=True), ms)
    gla = o * lax.rsqrt(ms + NORM_EPS) * gla_g_ref[...] * _silu(r_ref[0])
    y5 = _gelu_erf(s5_ref[0] + s5_d_ref[...] * u_ref[0])
    s5 = y5 * jax.nn.sigmoid(_mm(y5.astype(BF16), glu_w_ref[...]) + glu_b_ref[...])
    mix_in = jnp.concatenate([ssd, gla, mla_ref[0], s5], axis=1).astype(BF16)
    o_ref[0] = h_ref[0] + mod_ref[0] * _mm(mix_in, w_out_ref[...])


def _post(h, p, ssd_xbc, ssd_y, ssd_yb, gla_o, gla_ob, mla_y, s5_y, ssd_d, ssd_norm_g, gla_norm_g, s5_d, glu_w, glu_b, w_out, mod,
          row_off, mla_off):
    b, rows, d = h.shape
    w = GROUP_WIDTH
    pblk = lambda col: pl.BlockSpec((1, ROW_TILE, w), lambda i, t: (i, row_off + t, col // w))
    yblk = pl.BlockSpec((1, ROW_TILE, w), lambda i, t: (i, row_off + t, 0))
    full = lambda *shape: pl.BlockSpec(shape, lambda i, t: (0,) * len(shape))
    vec = lambda x: x.reshape(1, -1).astype(F32)
    n_mod = mod.shape[0]
    return pl.pallas_call(
        _post_kernel,
        grid=(b, rows // ROW_TILE),
        in_specs=[pl.BlockSpec((1, ROW_TILE, d), lambda i, t: (i, t, 0)),
                  yblk, pblk(COL_Z), pblk(COL_GLA_R), pblk(COL_S5), yblk, yblk, yblk, yblk,
                  pl.BlockSpec((1, ROW_TILE, w), lambda i, t: (i, mla_off + t, 0)), yblk,
                  full(1, w), full(1, w), full(1, w), full(1, w), full(w, w), full(1, w), full(d, d),
                  pl.BlockSpec((1, 1, d), lambda i, t: (jnp.minimum(i, n_mod - 1), 0, 0))],
        out_specs=pl.BlockSpec((1, ROW_TILE, d), lambda i, t: (i, t, 0)),
        out_shape=jax.ShapeDtypeStruct((b, rows, d), F32),
        compiler_params=pltpu.CompilerParams(dimension_semantics=("arbitrary", "arbitrary")),
        name="mix_post",
    )(h, ssd_xbc, p, p, p, ssd_y, ssd_yb, gla_o, gla_ob, mla_y, s5_y,
      vec(jnp.repeat(ssd_d, SSD_HEAD_DIM)), vec(ssd_norm_g), vec(jnp.tile(gla_norm_g, GLA_HEADS)), vec(s5_d),
      glu_w.astype(BF16), vec(glu_b), w_out.astype(BF16), mod)


PEER_ROUTE_TOKENS = 256
PEER_ROUTE_UNROLL = 4
PEER_GATE_TOKENS = 256
PEER_GATE_UNROLL = 32
PEER_GATE_SUBLANES = 8
PEER_DENSE_TOKENS = 1024
PEER_DENSE_EXPERTS = PEER_GATE_SUBLANES * PEER_KEYS
PEER_SLOTS = PEER_HEADS * PEER_TOPK


def _topk_rows(s, k):
    n_rows = s.shape[0]
    rows = lax.broadcasted_iota(jnp.int32, s.shape, 0)
    vals, idxs = [], []
    for _ in range(k):
        m = jnp.max(s, axis=0, keepdims=True)
        idx = jnp.min(jnp.where(s == m, rows, n_rows), axis=0, keepdims=True)
        vals.append(m)
        idxs.append(idx)
        s = jnp.where(rows == idx, -jnp.inf, s)
    return jnp.concatenate(vals, axis=0), jnp.concatenate(idxs, axis=0)


def _select_rows(pos, table):
    out = jnp.zeros(pos.shape, table.dtype)
    for r in range(table.shape[0]):
        out = jnp.where(pos == r, table[r:r + 1, :], out)
    return out


def _peer_route_kernel(h_ref, g_ref, shift_ref, scale_ref, wq_hi_ref, wq_lo_ref, k_hi_ref, k_lo_ref,
                       xn_ref, i1_ref, i2_ref, gate_ref, q_scr, slot_scr):
    xn = _modulated_norm(h_ref[...], g_ref[...], shift_ref[0], scale_ref[0])
    xn_ref[...] = xn.astype(BF16)
    x_hi, x_lo = _split_bf16(xn)
    q_scr[...] = _dot3(wq_hi_ref[...], wq_lo_ref[...], x_hi, x_lo, _NT_DIMS)
    half = PEER_DQ // 2

    def head_body(h, carry):
        base = pl.multiple_of(h * PEER_DQ, PEER_DQ)
        tops = []
        for j in range(2):
            qq = q_scr[pl.ds(base + j * half, half), :]
            q_hi, q_lo = _split_bf16(qq)
            s = _dot3(k_hi_ref[j, h], k_lo_ref[j, h], q_hi, q_lo, _NN_DIMS)
            tops.append(_topk_rows(s, PEER_TOPK))
        (v1, i1), (v2, i2) = tops
        pieces = [v1[a:a + 1, :] + v2[:PEER_TOPK // (a + 1), :] for a in range(PEER_TOPK)]
        n_cand = sum(PEER_TOPK // (a + 1) for a in range(PEER_TOPK))
        pad = -n_cand % 8
        cand = jnp.concatenate(pieces + [jnp.full((pad, v1.shape[1]), -jnp.inf, F32)], axis=0)
        best, pos = _topk_rows(cand, PEER_TOPK)
        e = jnp.exp(best - best[0:1, :])
        gates = e / jnp.sum(e, axis=0, keepdims=True)
        a_idx = jnp.zeros_like(pos)
        start = jnp.zeros_like(pos)
        first = 0
        for a in range(1, PEER_TOPK):
            width = PEER_TOPK // a
            first += width
            reached = pos >= first
            a_idx = a_idx + jnp.where(reached, 1, 0)
            start = start + jnp.where(reached, width, 0)
        row0 = pl.multiple_of(h * PEER_TOPK, PEER_TOPK)
        slot_scr[0, pl.ds(row0, PEER_TOPK), :] = _select_rows(a_idx, i1).astype(F32)
        slot_scr[1, pl.ds(row0, PEER_TOPK), :] = _select_rows(pos - start, i2).astype(F32)
        slot_scr[2, pl.ds(row0, PEER_TOPK), :] = gates
        return carry

    lax.fori_loop(0, PEER_HEADS, head_body, 0, unroll=PEER_ROUTE_UNROLL)
    i1_ref[...] = slot_scr[0].T.astype(jnp.int32)
    i2_ref[...] = slot_scr[1].T.astype(jnp.int32)
    gate_ref[...] = slot_scr[2].T


def _bf16_bits(x):
    return pltpu.bitcast(x.astype(BF16).astype(F32), jnp.uint32)


def _peer_gate_kernel(i1_ref, i2_ref, gate_ref, g_ref):
    rows = lax.broadcasted_iota(jnp.int32, (PEER_KEYS, PEER_SLOTS), 0)
    sub = PEER_GATE_SUBLANES

    def token_body(t, carry):
        a = i1_ref[pl.ds(t, 1), :]
        b = i2_ref[pl.ds(t, 1), :]
        w = gate_ref[pl.ds(t, 1), :]
        lhs = jnp.where(rows == a, w, 0.0).astype(BF16)
        rhs = jnp.where(rows == b, 1.0, 0.0).astype(BF16)
        gt = _mm(lhs, rhs, _NT_DIMS)
        row0 = pl.multiple_of(t * sub, sub)
        for g in range(PEER_KEYS // (2 * sub)):
            lo = gt[2 * sub * g:2 * sub * g + sub]
            hi = gt[2 * sub * g + sub:2 * sub * (g + 1)]
            g_ref[g, pl.ds(row0, sub), :] = (_bf16_bits(lo) >> 16) | _bf16_bits(hi)
        return carry

    lax.fori_loop(0, i1_ref.shape[0], token_body, 0, unroll=PEER_GATE_UNROLL)


def _peer_dense_kernel(xn_ref, u_ref, v_ref, gpk_ref, h_ref, mod_ref, o_ref, acc_ref):
    j, half = pl.program_id(1), pl.program_id(2)

    @pl.when(jnp.logical_and(j == 0, half == 0))
    def _():
        acc_ref[...] = jnp.zeros_like(acc_ref)

    sub = PEER_GATE_SUBLANES
    xn = xn_ref[...]
    tokens = xn.shape[0]
    hid = _gelu_erf(_mm(xn, u_ref[0], _NT_DIMS))
    ys = []
    for r in range(sub):
        word = gpk_ref[0, pl.ds(r, tokens, stride=sub), :]
        bits = jnp.where(half == 0, word << 16, word & jnp.uint32(0xFFFF0000))
        ys.append(pltpu.bitcast(bits, F32) * hid[:, r * PEER_KEYS:(r + 1) * PEER_KEYS])
    y = jnp.concatenate(ys, axis=1).astype(BF16)
    acc_ref[...] += _mm(y, v_ref[0])

    @pl.when(jnp.logical_and(j == pl.num_programs(1) - 1, half == 1))
    def _():
        o_ref[...] = h_ref[...] + mod_ref[0] * acc_ref[...]


def _peer_layer(h, norm_g, shift, scale, gate_mod, wq_t_hi, wq_t_lo, keys_hi, keys_lo, u_bf, v_bf, layer):
    n, d = h.shape
    nb = shift.shape[0]
    rows_per_batch = n // nb
    tr = min(PEER_ROUTE_TOKENS, rows_per_batch)
    full = lambda *shape: pl.BlockSpec(shape, lambda i: (0,) * len(shape))
    per_batch = lambda t: pl.BlockSpec((1, 1, d), lambda i: (i * t // rows_per_batch, 0, 0))
    xn, i1, i2, gate = pl.pallas_call(
        _peer_route_kernel,
        grid=(n // tr,),
        in_specs=[pl.BlockSpec((tr, d), lambda i: (i, 0)), full(1, d), per_batch(tr), per_batch(tr),
                  full(PEER_HEADS * PEER_DQ, d), full(PEER_HEADS * PEER_DQ, d),
                  full(2, PEER_HEADS, PEER_KEYS, PEER_DQ // 2), full(2, PEER_HEADS, PEER_KEYS, PEER_DQ // 2)],
        out_specs=[pl.BlockSpec((tr, d), lambda i: (i, 0))] + [pl.BlockSpec((tr, PEER_SLOTS), lambda i: (i, 0))] * 3,
        out_shape=[jax.ShapeDtypeStruct((n, d), BF16),
                   jax.ShapeDtypeStruct((n, PEER_SLOTS), jnp.int32),
                   jax.ShapeDtypeStruct((n, PEER_SLOTS), jnp.int32),
                   jax.ShapeDtypeStruct((n, PEER_SLOTS), F32)],
        scratch_shapes=[pltpu.VMEM((PEER_HEADS * PEER_DQ, tr), F32), pltpu.VMEM((3, PEER_SLOTS, tr), F32)],
        compiler_params=pltpu.CompilerParams(dimension_semantics=("arbitrary",)),
        name="peer_route",
    )(h, norm_g.reshape(1, d), shift, scale, wq_t_hi, wq_t_lo, keys_hi, keys_lo)

    tg = min(PEER_GATE_TOKENS, n)
    n_planes = PEER_KEYS // (2 * PEER_GATE_SUBLANES)
    slot_spec = pl.BlockSpec((tg, PEER_SLOTS), lambda i: (i, 0))
    gmat = pl.pallas_call(
        _peer_gate_kernel,
        grid=(n // tg,),
        in_specs=[slot_spec, slot_spec, slot_spec],
        out_specs=pl.BlockSpec((n_planes, tg * PEER_GATE_SUBLANES, PEER_KEYS), lambda i: (0, i, 0)),
        out_shape=jax.ShapeDtypeStruct((n_planes, n * PEER_GATE_SUBLANES, PEER_KEYS), jnp.uint32),
        compiler_params=pltpu.CompilerParams(dimension_semantics=("arbitrary",)),
        name="peer_gate",
    )(i1, i2, gate)

    tm = min(PEER_DENSE_TOKENS, rows_per_batch)
    te = PEER_DENSE_EXPERTS
    return pl.pallas_call(
        _peer_dense_kernel,
        grid=(n // tm, n_planes, 2),
        in_specs=[pl.BlockSpec((tm, d), lambda i, j, hf: (i, 0)),
                  pl.BlockSpec((1, te, d), lambda i, j, hf: (layer, 2 * j + hf, 0)),
                  pl.BlockSpec((1, te, d), lambda i, j, hf: (layer, 2 * j + hf, 0)),
                  pl.BlockSpec((1, tm * PEER_GATE_SUBLANES, PEER_KEYS), lambda i, j, hf: (j, i, 0)),
                  pl.BlockSpec((tm, d), lambda i, j, hf: (i, 0)),
                  pl.BlockSpec((1, 1, d), lambda i, j, hf: (i * tm // rows_per_batch, 0, 0))],
        out_specs=pl.BlockSpec((tm, d), lambda i, j, hf: (i, 0)),
        out_shape=jax.ShapeDtypeStruct((n, d), F32),
        scratch_shapes=[pltpu.VMEM((tm, d), F32)],
        compiler_params=pltpu.CompilerParams(dimension_semantics=("arbitrary", "arbitrary", "arbitrary"),
                                             vmem_limit_bytes=56 * 2 ** 20),
        name="peer_dense",
    )(xn, u_bf, v_bf, gmat, h, gate_mod)


def _final_norm_kernel(x_ref, g_ref, o_ref):
    o_ref[...] = _rms(x_ref[...], g_ref[...])


def _final_norm(h, g):
    n = h.shape[0] * h.shape[1]
    x2 = h.reshape(n, D_MODEL)
    tm = 512
    out = pl.pallas_call(
        _final_norm_kernel,
        grid=(n // tm,),
        in_specs=[pl.BlockSpec((tm, D_MODEL), lambda i: (i, 0)),
                  pl.BlockSpec((1, D_MODEL), lambda i: (0, 0))],
        out_specs=pl.BlockSpec((tm, D_MODEL), lambda i: (i, 0)),
        out_shape=jax.ShapeDtypeStruct((n, D_MODEL), F32),
        name="final_norm",
    )(x2, g.reshape(1, D_MODEL))
    return out.reshape(h.shape)


def _mix_layer(h_lat, h_ctx, mod_l, mod_c, norm_g, w_in, w_out, ssd, gla, mla, s5, ctx_out):
    b, n_lat, d = h_lat.shape
    n_lat_tiles = n_lat // ROW_TILE
    tab = lambda k: jnp.concatenate([mod_l[k], mod_c[k]], axis=0)
    p = _inproj(h_lat, h_ctx, norm_g, tab(0), tab(1), _pack_w_in(w_in))
    ssd_y, ssd_yb, ssd_xbc = _ssd_mixer(p, ssd["conv_w"], ssd["conv_b"], ssd["a_log"], ssd["dt_bias"], n_lat_tiles)
    gla_o, gla_ob = _gla_mixer(p, gla["gate_w"], gla["gate_b"], n_lat_tiles)
    mla_lat, mla_ctx = _mla_mixer(p, mla["q_norm_g"], mla["w_uq"], mla["kv_norm_g"], mla["w_ukv"], n_lat_tiles, ctx_out)
    s5_y = _s5_mixer(p, s5["a_re"], s5["a_im"], s5["log_dt"], s5["b_re"], s5["b_im"], s5["c_re"], s5["c_im"],
                     n_lat_tiles)
    post = functools.partial(_post, p=p, ssd_xbc=ssd_xbc, ssd_y=ssd_y, ssd_yb=ssd_yb, gla_o=gla_o, gla_ob=gla_ob,
                             s5_y=s5_y, ssd_d=ssd["d"],
                             ssd_norm_g=ssd["norm_g"], gla_norm_g=gla["norm_g"], s5_d=s5["d"],
                             glu_w=s5["glu_w"], glu_b=s5["glu_b"], w_out=w_out)
    new_lat = post(h_lat, mla_y=mla_lat, mod=mod_l[2], row_off=0, mla_off=0)
    new_ctx = None
    if ctx_out:
        new_ctx = post(h_ctx, mla_y=mla_ctx, mod=mod_c[2], row_off=n_lat_tiles, mla_off=0)
    return new_lat, new_ctx


def kernel(x, c, ctx, c_ctx, ada_w, ada_b, norm_mix_g, norm_ffn_g, w_in, w_out,
           ssd_conv_w, ssd_conv_b, ssd_a_log, ssd_dt_bias, ssd_d, ssd_norm_g,
           gla_gate_w, gla_gate_b, gla_norm_g, mla_q_norm_g, mla_w_uq, mla_kv_norm_g, mla_w_ukv,
           s5_a_re, s5_a_im, s5_log_dt, s5_b_re, s5_b_im, s5_c_re, s5_c_im, s5_d, s5_glu_w, s5_glu_b,
           peer_w_q, peer_sub_keys, peer_u, peer_v, final_norm_g):
    h_lat, h_ctx = x, ctx
    cond_lat = jax.nn.silu(c)[:, None, :]
    cond_ctx = jax.nn.silu(c_ctx)[None, None, :]
    u_bf, v_bf = peer_u.astype(BF16), peer_v.astype(BF16)
    for i in range(DEPTH):
        ctx_out = i < DEPTH - 1
        mod_l = jnp.split(cond_lat @ ada_w[i] + ada_b[i], N_MOD, axis=-1)
        mod_c = jnp.split(cond_ctx @ ada_w[i] + ada_b[i], N_MOD, axis=-1)
        ssd = dict(conv_w=ssd_conv_w[i], conv_b=ssd_conv_b[i], a_log=ssd_a_log[i], dt_bias=ssd_dt_bias[i],
                   d=ssd_d[i], norm_g=ssd_norm_g[i])
        gla = dict(gate_w=gla_gate_w[i], gate_b=gla_gate_b[i], norm_g=gla_norm_g[i])
        mla = dict(q_norm_g=mla_q_norm_g[i], w_uq=mla_w_uq[i], kv_norm_g=mla_kv_norm_g[i], w_ukv=mla_w_ukv[i])
        s5 = dict(a_re=s5_a_re[i], a_im=s5_a_im[i], log_dt=s5_log_dt[i], b_re=s5_b_re[i], b_im=s5_b_im[i],
                  c_re=s5_c_re[i], c_im=s5_c_im[i], d=s5_d[i], glu_w=s5_glu_w[i], glu_b=s5_glu_b[i])
        h_lat, h_ctx_new = _mix_layer(h_lat, h_ctx, mod_l, mod_c, norm_mix_g[i], w_in[i], w_out[i],
                                      ssd, gla, mla, s5, ctx_out)
        wq_t_hi, wq_t_lo = _split_bf16(peer_w_q[i].T)
        keys_hi, keys_lo = _split_bf16(peer_sub_keys[i])
        peer = functools.partial(_peer_layer, norm_g=norm_ffn_g[i], wq_t_hi=wq_t_hi, wq_t_lo=wq_t_lo,
                                 keys_hi=keys_hi, keys_lo=keys_lo, u_bf=u_bf, v_bf=v_bf, layer=i)
        h_lat = peer(h_lat.reshape(-1, D_MODEL), shift=mod_l[3], scale=mod_l[4],
                     gate_mod=mod_l[5]).reshape(h_lat.shape)
        if ctx_out:
            h_ctx = peer(h_ctx_new.reshape(-1, D_MODEL), shift=mod_c[3], scale=mod_c[4],
                         gate_mod=mod_c[5]).reshape(h_ctx.shape)
    return _final_norm(h_lat, final_norm_g)
```

```python
import functools
import jax
import jax.numpy as jnp
from jax import lax
import numpy as np
from jax.experimental import pallas as pl
from jax.experimental.pallas import tpu as pltpu

D_MODEL = 1024
DEPTH = 2
GRID_W = 64
NORM_EPS = 1e-6
N_MOD = 6

GROUP_WIDTH = D_MODEL // 4

SSD_WIDTH = GROUP_WIDTH
SSD_HEAD_DIM = 64
SSD_HEADS = SSD_WIDTH // SSD_HEAD_DIM
SSD_GROUPS = 2
SSD_STATE = 128
SSD_CONV = 5
SSD_CHUNK = 128
SSD_CONV_CH = SSD_WIDTH + 2 * SSD_GROUPS * SSD_STATE
SSD_IN = SSD_WIDTH + SSD_CONV_CH + 2 * SSD_HEADS

GLA_WIDTH = GROUP_WIDTH
GLA_HEADS = 4
GLA_DV = GLA_WIDTH // GLA_HEADS
GLA_DK = GLA_DV // 2
GLA_QK = GLA_HEADS * GLA_DK
GLA_GATE_RANK = 16
GLA_TAU = 16.0
GLA_CHUNK = 64
GLA_IN = 2 * GLA_QK + 2 * GLA_WIDTH + 2 * GLA_GATE_RANK

MLA_WIDTH = GROUP_WIDTH
MLA_HEADS = 4
MLA_V = MLA_WIDTH // MLA_HEADS
MLA_NOPE = 64
MLA_ROPE = 32
MLA_Q_RANK = 256
MLA_KV_RANK = 128
MLA_SCALE = (MLA_NOPE + MLA_ROPE) ** -0.5
ROPE_BASE = 10000.0
MLA_IN = MLA_Q_RANK + MLA_KV_RANK + MLA_ROPE

S5_WIDTH = GROUP_WIDTH
S5_GROUP = 16
S5_NGROUPS = S5_WIDTH // S5_GROUP
S5_STATE = 64
S5_MAX_RE = -1e-4
S5_IN = S5_WIDTH
S5_CHUNK = 16
S5_PAIRS = S5_NGROUPS // 2
S5_PACK_ROWS = (768, 256)

PEER_KEYS = 128
PEER_EXPERTS = PEER_KEYS * PEER_KEYS
PEER_HEADS = 8
PEER_TOPK = 16
PEER_DQ = 128

LANES = 128
ROW_TILE = 256
SCAN_STEP_ROWS = ROW_TILE

F32 = jnp.float32
BF16 = jnp.bfloat16

COL_XS, COL_BM, COL_CM, COL_Z = 0, 256, 512, 768
COL_GLA_V, COL_GLA_R, COL_CQ, COL_S5 = 1024, 1280, 1536, 1792
COL_GLA_Q, COL_GLA_K, COL_CKV, COL_DT, COL_GLR, COL_KR, COL_KRROT = 2048, 2176, 2304, 2432, 2560, 2688, 2816
P_COLS = 2944

_NN_DIMS = (((1,), (0,)), ((), ()))
_NT_DIMS = (((1,), (1,)), ((), ()))
_TN_DIMS = (((0,), (0,)), ((), ()))


def _mm(a, b, dims=_NN_DIMS):
    return lax.dot_general(a, b, dims, preferred_element_type=F32)


def _split_bf16(x):
    hi = x.astype(BF16)
    lo = (x - hi.astype(F32)).astype(BF16)
    return hi, lo


def _split3_bf16(x):
    p1 = x.astype(BF16)
    r1 = x - p1.astype(F32)
    p2 = r1.astype(BF16)
    p3 = (r1 - p2.astype(F32)).astype(BF16)
    return p1, p2, p3


def _dot3(a_hi, a_lo, b_hi, b_lo, dims):
    return _mm(a_hi, b_hi, dims) + _mm(a_hi, b_lo, dims) + _mm(a_lo, b_hi, dims)


def _gelu_erf(x):
    return 0.5 * x * (1.0 + lax.erf(x * (2.0 ** -0.5)))


def _silu(x):
    return x * jax.nn.sigmoid(x)


def _softplus(x):
    return jnp.maximum(x, 0.0) + jnp.log1p(jnp.exp(-jnp.abs(x)))


def _log_sigmoid(x):
    return jnp.minimum(x, 0.0) - jnp.log1p(jnp.exp(-jnp.abs(x)))


def _rms(x, g):
    return x * lax.rsqrt(jnp.mean(x * x, axis=-1, keepdims=True) + NORM_EPS) * g


def _modulated_norm(x, g, shift, scale):
    return _rms(x, g) * (1.0 + scale) + shift


def _causal_mask(n, reverse):
    ri = lax.broadcasted_iota(jnp.int32, (n, n), 0)
    ci = lax.broadcasted_iota(jnp.int32, (n, n), 1)
    return (ci >= ri) if reverse else (ci <= ri)


def _scan_chunk(s, n_lat, n_ctx, reverse):
    if reverse:
        return n_lat + n_ctx - 1 - s
    return jnp.where(s < n_ctx, n_lat + s, s - n_ctx)


def _inproj_kernel(lat_ref, ctx_ref, g_ref, shift_ref, scale_ref, w_ref, o_ref, *, n_lat_tiles):
    t = pl.program_id(1)
    x = jnp.where(t < n_lat_tiles, lat_ref[0], ctx_ref[0])
    xn = _modulated_norm(x, g_ref[...], shift_ref[0], scale_ref[0])
    o_ref[0] = _mm(xn.astype(BF16), w_ref[...])


def _inproj(h_lat, h_ctx, norm_g, shift_tab, scale_tab, w_pad):
    b, n_lat, d = h_lat.shape
    n_lat_tiles = n_lat // ROW_TILE
    n_tiles = n_lat_tiles + h_ctx.shape[1] // ROW_TILE
    mod_spec = pl.BlockSpec((1, 1, d), lambda i, t: (jnp.where(t < n_lat_tiles, i, b), 0, 0))
    return pl.pallas_call(
        functools.partial(_inproj_kernel, n_lat_tiles=n_lat_tiles),
        grid=(b, n_tiles),
        in_specs=[pl.BlockSpec((1, ROW_TILE, d), lambda i, t: (i, jnp.minimum(t, n_lat_tiles - 1), 0)),
                  pl.BlockSpec((1, ROW_TILE, d), lambda i, t: (i, jnp.maximum(t - n_lat_tiles, 0), 0)),
                  pl.BlockSpec((1, d), lambda i, t: (0, 0)), mod_spec, mod_spec,
                  pl.BlockSpec((d, P_COLS), lambda i, t: (0, 0))],
        out_specs=pl.BlockSpec((1, ROW_TILE, P_COLS), lambda i, t: (i, t, 0)),
        out_shape=jax.ShapeDtypeStruct((b, n_tiles * ROW_TILE, P_COLS), F32),
        compiler_params=pltpu.CompilerParams(dimension_semantics=("arbitrary", "arbitrary"),
                                             vmem_limit_bytes=48 * 2 ** 20),
        name="inproj",
    )(h_lat, h_ctx, norm_g.reshape(1, d), shift_tab, scale_tab, w_pad)


def _pack_w_in(w):
    o_ssd, o_gla, o_mla, o_s5 = 0, SSD_IN, SSD_IN + GLA_IN, SSD_IN + GLA_IN + MLA_IN
    out = jnp.zeros((w.shape[0], P_COLS), F32)
    put = lambda out, col, src, width: out.at[:, col:col + width].set(w[:, src:src + width])
    out = put(out, COL_Z, o_ssd, SSD_WIDTH)
    out = put(out, COL_XS, o_ssd + SSD_WIDTH, SSD_CONV_CH)
    out = put(out, COL_DT, o_ssd + SSD_WIDTH + SSD_CONV_CH, 2 * SSD_HEADS)
    out = put(out, COL_GLA_Q, o_gla, GLA_QK)
    out = put(out, COL_GLA_K, o_gla + GLA_QK, GLA_QK)
    out = put(out, COL_GLA_V, o_gla + 2 * GLA_QK, GLA_WIDTH)
    out = put(out, COL_GLA_R, o_gla + 2 * GLA_QK + GLA_WIDTH, GLA_WIDTH)
    out = put(out, COL_GLR, o_gla + 2 * GLA_QK + 2 * GLA_WIDTH, 2 * GLA_GATE_RANK)
    out = put(out, COL_CQ, o_mla, MLA_Q_RANK)
    out = put(out, COL_CKV, o_mla + MLA_Q_RANK, MLA_KV_RANK)
    o_kr = o_mla + MLA_Q_RANK + MLA_KV_RANK
    half = MLA_ROPE // 2
    out = put(out, COL_KR + MLA_NOPE, o_kr, MLA_ROPE)
    out = out.at[:, COL_KRROT + MLA_NOPE:COL_KRROT + MLA_NOPE + half].set(-w[:, o_kr + half:o_kr + MLA_ROPE])
    out = out.at[:, COL_KRROT + MLA_NOPE + half:COL_KRROT + MLA_NOPE + MLA_ROPE].set(w[:, o_kr:o_kr + half])
    out = put(out, COL_S5, o_s5, S5_WIDTH)
    return out.astype(BF16)


def _ssd_prep_kernel(x_ref, prev_ref, next_ref, dt_ref, w_ref, b_ref, bias_ref, xbc_ref, dtc_ref, dtt_ref,
                     *, n_lat_tiles):
    t = pl.program_id(1)
    x = x_ref[0]
    halo = prev_ref.shape[1]
    prev = jnp.where(jnp.logical_and(t > 0, t < n_lat_tiles), prev_ref[0], 0.0)
    nxt = jnp.where(t < n_lat_tiles - 1, next_ref[0], 0.0)
    ext = jnp.concatenate([prev, x, nxt], axis=0)
    rows = ext.shape[0]
    left = SSD_CONV // 2
    acc = jnp.zeros_like(x) + b_ref[...]
    for k in range(SSD_CONV):
        shifted = ext if k == left else pltpu.roll(ext, (left - k) % rows, 0)
        acc = acc + w_ref[k:k + 1, :] * shifted[halo:halo + x.shape[0]]
    xbc_ref[0] = _silu(acc)
    dt = _softplus(dt_ref[0] + bias_ref[...])
    dtc_ref[0] = dt
    dtt_ref[0] = dt.T[:dtt_ref.shape[1]]


def _ssd_scan_kernel(xbc_f_ref, dtc_f_ref, dtt_f_ref, xbc_b_ref, dtc_b_ref, dtt_b_ref, ahr_ref, ahc_ref,
                     yf_ref, yb_ref, state_ref):
    @pl.when(pl.program_id(0) == 0)
    def _():
        state_ref[...] = jnp.zeros_like(state_ref)

    q = SSD_CHUNK
    n_sub = xbc_f_ref.shape[1] // q
    for k in range(n_sub):
        for direction, (xbc_ref, dtc_ref, dtt_ref, y_ref) in enumerate(
                ((xbc_f_ref, dtc_f_ref, dtt_f_ref, yf_ref), (xbc_b_ref, dtc_b_ref, dtt_b_ref, yb_ref))):
            c = k if direction == 0 else n_sub - 1 - k
            rows = slice(c * q, (c + 1) * q)
            for bi in range(xbc_ref.shape[0]):
                y_ref[bi, rows] = _ssd_chunk(xbc_ref[bi, rows], dtc_ref[bi, rows], dtt_ref[bi, :, rows], ahr_ref,
                                             ahc_ref, state_ref.at[bi, direction], direction)


def _ssd_chunk(xbc, dtc, dtt, ahr_ref, ahc_ref, state_ref, direction):
    reverse = direction == 1
    q = SSD_CHUNK
    mask = _causal_mask(q, reverse)
    tri = jnp.where(mask, 1.0, 0.0).astype(BF16)
    xs, bm, cm = xbc[:, :SSD_WIDTH], xbc[:, SSD_WIDTH:SSD_WIDTH + 256], xbc[:, SSD_WIDTH + 256:]
    a_col = dtc * ahr_ref[...]
    a_row = dtt * ahc_ref[...]
    acum_col = sum(_mm(tri, part) for part in _split3_bf16(a_col))
    acum_row = sum(_mm(part, tri, _NT_DIMS) for part in _split3_bf16(a_row))
    end = 0 if reverse else q - 1
    bm_bf, cm_bf = bm.astype(BF16), cm.astype(BF16)
    ys = []
    cb = {}
    for h in range(SSD_HEADS):
        g = h // (SSD_HEADS // SSD_GROUPS)
        gs = slice(g * SSD_STATE, (g + 1) * SSD_STATE)
        if g not in cb:
            cb[g] = _mm(cm_bf[:, gs], bm_bf[:, gs], _NT_DIMS)
        ch = direction * SSD_HEADS + h
        ac = acum_col[:, ch:ch + 1]
        ar = acum_row[ch:ch + 1, :]
        decay = jnp.exp(jnp.where(mask, ac - ar, -jnp.inf))
        xd = xs[:, h * SSD_HEAD_DIM:(h + 1) * SSD_HEAD_DIM] * dtc[:, ch:ch + 1]
        y_diag = _mm((cb[g] * decay).astype(BF16), xd.astype(BF16))
        a_end = ac[end:end + 1, :]
        st_local = _mm((xd * jnp.exp(a_end - ac)).astype(BF16), bm_bf[:, gs], _TN_DIMS)
        hs = state_ref[h]
        y_off = jnp.exp(ac) * _mm(cm_bf[:, gs], hs.astype(BF16), _NT_DIMS)
        state_ref[h] = jnp.exp(a_end) * hs + st_local
        ys.append(y_diag + y_off)
    return jnp.concatenate(ys, axis=1)


def _ssd_mixer(p, conv_w, conv_b, a_log, dt_bias, n_lat_tiles):
    b, r, _ = p.shape
    nt = r // ROW_TILE
    halo = 8
    hb = ROW_TILE // halo
    w8 = jnp.zeros((8, SSD_CONV_CH), F32).at[:SSD_CONV].set(conv_w)
    bias = jnp.zeros((1, LANES), F32).at[0, :2 * SSD_HEADS].set(dt_bias.reshape(-1))
    xbc, dtc, dtt = pl.pallas_call(
        functools.partial(_ssd_prep_kernel, n_lat_tiles=n_lat_tiles),
        grid=(b, nt),
        in_specs=[pl.BlockSpec((1, ROW_TILE, SSD_CONV_CH), lambda i, t: (i, t, 0)),
                  pl.BlockSpec((1, halo, SSD_CONV_CH), lambda i, t: (i, jnp.maximum(t * hb - 1, 0), 0)),
                  pl.BlockSpec((1, halo, SSD_CONV_CH), lambda i, t: (i, jnp.minimum((t + 1) * hb, nt * hb - 1), 0)),
                  pl.BlockSpec((1, ROW_TILE, LANES), lambda i, t: (i, t, COL_DT // LANES)),
                  pl.BlockSpec((8, SSD_CONV_CH), lambda i, t: (0, 0)),
                  pl.BlockSpec((1, SSD_CONV_CH), lambda i, t: (0, 0)),
                  pl.BlockSpec((1, LANES), lambda i, t: (0, 0))],
        out_specs=[pl.BlockSpec((1, ROW_TILE, SSD_CONV_CH), lambda i, t: (i, t, 0)),
                   pl.BlockSpec((1, ROW_TILE, LANES), lambda i, t: (i, t, 0)),
                   pl.BlockSpec((1, 8, ROW_TILE), lambda i, t: (i, 0, t))],
        out_shape=[jax.ShapeDtypeStruct((b, r, SSD_CONV_CH), F32),
                   jax.ShapeDtypeStruct((b, r, LANES), F32),
                   jax.ShapeDtypeStruct((b, 8, r), F32)],
        compiler_params=pltpu.CompilerParams(dimension_semantics=("arbitrary", "arbitrary")),
        name="ssd_prep",
    )(p, p, p, p, w8, conv_b.reshape(1, -1), bias)

    a_head = -jnp.exp(a_log.astype(F32)).reshape(-1)
    ahr = jnp.zeros((1, LANES), F32).at[0, :2 * SSD_HEADS].set(a_head)
    ahc = a_head.reshape(2 * SSD_HEADS, 1)
    blk = SCAN_STEP_ROWS
    n_lat = n_lat_tiles * ROW_TILE // blk
    n_ctx = r // blk - n_lat
    in_specs, y_specs = [], []
    for reverse in (False, True):
        cidx = functools.partial(_scan_chunk, n_lat=n_lat, n_ctx=n_ctx, reverse=reverse)
        in_specs += [pl.BlockSpec((b, blk, SSD_CONV_CH), lambda s, cidx=cidx: (0, cidx(s), 0)),
                     pl.BlockSpec((b, blk, LANES), lambda s, cidx=cidx: (0, cidx(s), 0)),
                     pl.BlockSpec((b, 8, blk), lambda s, cidx=cidx: (0, 0, cidx(s)))]
        y_specs.append(pl.BlockSpec((b, blk, SSD_WIDTH), lambda s, cidx=cidx: (0, cidx(s), 0)))
    in_specs += [pl.BlockSpec((1, LANES), lambda s: (0, 0)), pl.BlockSpec((2 * SSD_HEADS, 1), lambda s: (0, 0))]
    y_f, y_b = pl.pallas_call(
        _ssd_scan_kernel,
        grid=(n_lat + n_ctx,),
        in_specs=in_specs,
        out_specs=y_specs,
        out_shape=[jax.ShapeDtypeStruct((b, r, SSD_WIDTH), F32)] * 2,
        scratch_shapes=[pltpu.VMEM((b, 2, SSD_HEADS, SSD_HEAD_DIM, SSD_STATE), F32)],
        compiler_params=pltpu.CompilerParams(dimension_semantics=("arbitrary",)),
        name="ssd_scan",
    )(xbc, dtc, dtt, xbc, dtc, dtt, ahr, ahc)
    return y_f, y_b, xbc


def _gla_scan_kernel(qf_ref, kf_ref, vf_ref, glrf_ref, qb_ref, kb_ref, vb_ref, glrb_ref, wg_ref, bias_ref,
                     of_ref, ob_ref, st_ref):
    @pl.when(pl.program_id(0) == 0)
    def _():
        st_ref[...] = jnp.zeros_like(st_ref)

    n = GLA_CHUNK
    n_sub = qf_ref.shape[1] // n
    for kk in range(n_sub):
        for direction, (q_ref, k_ref, v_ref, glr_ref, o_ref) in enumerate(
                ((qf_ref, kf_ref, vf_ref, glrf_ref, of_ref), (qb_ref, kb_ref, vb_ref, glrb_ref, ob_ref))):
            c = kk if direction == 0 else n_sub - 1 - kk
            rows = slice(c * n, (c + 1) * n)
            for bi in range(q_ref.shape[0]):
                o_ref[bi, rows] = _gla_chunk(q_ref[bi, rows], k_ref[bi, rows], v_ref[bi, rows], glr_ref[bi, rows],
                                             wg_ref.at[direction], bias_ref.at[direction],
                                             st_ref.at[bi, direction], direction == 1)


def _gla_chunk(q, k, v, glr, wg_ref, bias_ref, st_ref, reverse):
    n = GLA_CHUNK
    mask = _causal_mask(n, reverse)
    tri = jnp.where(mask, 1.0, 0.0).astype(BF16)
    g_hi, g_lo = _split_bf16(glr)
    logits = _dot3(g_hi, g_lo, wg_ref[0], wg_ref[1], _NN_DIMS) + bias_ref[...]
    logg = _log_sigmoid(logits) * (1.0 / GLA_TAU)
    bcum = sum(_mm(tri, part) for part in _split3_bf16(logg))
    end = 0 if reverse else n - 1
    b_end = bcum[end:end + 1, :]
    qe = q * jnp.exp(bcum) * (GLA_DK ** -0.5)
    ke = (k * jnp.exp(-bcum)).astype(BF16)
    kd = k * jnp.exp(b_end - bcum)
    decay_end = jnp.exp(b_end)
    lane_head = lax.broadcasted_iota(jnp.int32, (1, GLA_QK), 1) >> (GLA_DK.bit_length() - 1)
    outs = []
    for h in range(GLA_HEADS):
        hm = lane_head == h
        qh = jnp.where(hm, qe, 0.0).astype(BF16)
        att = jnp.where(mask, _mm(qh, ke, _NT_DIMS), 0.0)
        vh = v[:, h * GLA_DV:(h + 1) * GLA_DV].astype(BF16)
        st = st_ref[h]
        o_h = _mm(att.astype(BF16), vh) + _mm(qh, st.astype(BF16), _NT_DIMS)
        local = _mm(vh, jnp.where(hm, kd, 0.0).astype(BF16), _TN_DIMS)
        st_ref[h] = st * decay_end + local
        outs.append(o_h)
    return jnp.concatenate(outs, axis=1)


def _gla_mixer(p, gate_w, gate_b, n_lat_tiles):
    b, r, _ = p.shape
    rows = SCAN_STEP_ROWS
    n_lat = n_lat_tiles * ROW_TILE // rows
    n_ctx = r // rows - n_lat
    in_specs, o_specs, wgs = [], [], []
    for direction in (0, 1):
        cidx = functools.partial(_scan_chunk, n_lat=n_lat, n_ctx=n_ctx, reverse=direction == 1)
        wg = jnp.zeros((LANES, GLA_QK), F32).at[direction * GLA_GATE_RANK:(direction + 1) * GLA_GATE_RANK].set(
            gate_w[direction])
        wgs.append(jnp.stack(_split_bf16(wg)))
        blk = lambda width, col, cidx=cidx: pl.BlockSpec((b, rows, width), lambda s: (0, cidx(s), col // width))
        in_specs += [blk(GLA_QK, COL_GLA_Q), blk(GLA_QK, COL_GLA_K), blk(GLA_WIDTH, COL_GLA_V), blk(LANES, COL_GLR)]
        o_specs.append(pl.BlockSpec((b, rows, GLA_WIDTH), lambda s, cidx=cidx: (0, cidx(s), 0)))
    in_specs += [pl.BlockSpec((2, 2, LANES, GLA_QK), lambda s: (0, 0, 0, 0)),
                 pl.BlockSpec((2, 1, GLA_QK), lambda s: (0, 0, 0))]
    return pl.pallas_call(
        _gla_scan_kernel,
        grid=(n_lat + n_ctx,),
        in_specs=in_specs,
        out_specs=o_specs,
        out_shape=[jax.ShapeDtypeStruct((b, r, GLA_WIDTH), F32)] * 2,
        scratch_shapes=[pltpu.VMEM((b, 2, GLA_HEADS, GLA_DV, GLA_QK), F32)],
        compiler_params=pltpu.CompilerParams(dimension_semantics=("arbitrary",)),
        name="gla_scan",
    )(p, p, p, p, p, p, p, p, jnp.stack(wgs), gate_b.reshape(2, 1, GLA_QK))


MLA_Q_TILE = 1024
MLA_K_TILES = (768, 256)


def _mla_prep_kernel(cq_ref, ckv_ref, kr_ref, krrot_ref, onec_ref, sinr_ref, gq_ref, gkv_ref,
                     wq_ref, wqr_ref, wk_ref, wv_ref, q_ref, k_ref, v_ref):
    qn = _rms(cq_ref[0], gq_ref[...]).astype(BF16)
    kvn = _rms(ckv_ref[0], gkv_ref[...]).astype(BF16)
    onec, sinr = onec_ref[...], sinr_ref[...]
    k_rope = kr_ref[0] * onec + krrot_ref[0] * sinr
    ones_lane = jnp.where(lax.broadcasted_iota(jnp.int32, (1, LANES), 1) == MLA_V, 1.0, 0.0)
    for h in range(MLA_HEADS):
        qh = _mm(qn, wq_ref[h]) * onec + _mm(qn, wqr_ref[h]) * sinr
        q_ref[0, h] = (qh * MLA_SCALE).astype(BF16)
        k_ref[0, h] = (_mm(kvn, wk_ref[h]) + k_rope).astype(BF16)
        v_ref[0, h] = (_mm(kvn, wv_ref[h]) + ones_lane).astype(BF16)


def _mla_attn_kernel(q_ref, k_ref, v_ref, o_ref, m_ref, acc_ref):
    j = pl.program_id(2)

    @pl.when(j == 0)
    def _():
        m_ref[...] = jnp.full_like(m_ref, -jnp.inf)
        acc_ref[...] = jnp.zeros_like(acc_ref)

    reps = k_ref.shape[2] // LANES
    for h in range(MLA_HEADS):
        s = _mm(q_ref[0, h], k_ref[0, h], _NT_DIMS)
        m_prev = m_ref[h]
        m_new = jnp.maximum(m_prev, jnp.max(s, axis=1, keepdims=True))
        p = jnp.exp((s - jnp.concatenate([m_new] * reps, axis=1)).astype(BF16))
        acc_ref[h] = jnp.exp(m_prev - m_new) * acc_ref[h] + _mm(p, v_ref[0, h])
        m_ref[h] = m_new

    @pl.when(j == pl.num_programs(2) - 1)
    def _():
        outs = []
        for h in range(MLA_HEADS):
            acc = acc_ref[h]
            outs.append(acc[:, :MLA_V] / acc[:, MLA_V:MLA_V + 1])
        o_ref[0] = jnp.concatenate(outs, axis=1)


def _rope_tables(n_lat, n_rows):
    rows = n_lat // GRID_W
    row = jnp.repeat(jnp.arange(rows, dtype=F32), GRID_W)
    col = jnp.tile(jnp.arange(GRID_W, dtype=F32), rows)
    half = MLA_ROPE // 2
    inv = ROPE_BASE ** (-jnp.arange(0, half, 2, dtype=F32) / half)
    ang = jnp.concatenate([row[:, None] * inv, col[:, None] * inv], axis=-1)
    cos = jnp.concatenate([jnp.cos(ang), jnp.ones((n_rows - n_lat, half), F32)], axis=0)
    sin = jnp.concatenate([jnp.sin(ang), jnp.zeros((n_rows - n_lat, half), F32)], axis=0)
    pad = jnp.zeros((n_rows, LANES - MLA_NOPE - MLA_ROPE), F32)
    onec = jnp.concatenate([jnp.ones((n_rows, MLA_NOPE), F32), cos, cos, pad], axis=1)
    sinr = jnp.concatenate([jnp.zeros((n_rows, MLA_NOPE), F32), sin, sin, pad], axis=1)
    return onec, sinr


def _mla_weights(w_uq, w_ukv):
    dqk = MLA_NOPE + MLA_ROPE
    half = MLA_ROPE // 2
    wq = w_uq.reshape(MLA_Q_RANK, MLA_HEADS, dqk).transpose(1, 0, 2)
    rot = jnp.concatenate([jnp.zeros_like(wq[..., :MLA_NOPE]), -wq[..., MLA_NOPE + half:], wq[..., MLA_NOPE:MLA_NOPE + half]],
                          axis=-1)
    padq = lambda w: jnp.pad(w, ((0, 0), (0, 0), (0, LANES - dqk))).astype(BF16)
    wkv = w_ukv.reshape(MLA_KV_RANK, MLA_HEADS, MLA_NOPE + MLA_V).transpose(1, 0, 2)
    padk = lambda w: jnp.pad(w, ((0, 0), (0, 0), (0, LANES - w.shape[-1]))).astype(BF16)
    return padq(wq), padq(rot), padk(wkv[..., :MLA_NOPE]), padk(wkv[..., MLA_NOPE:])


def _mla_attention(q, k, v, q_tile, q_off, n_q, kt, k_off, n_k):
    b = q.shape[0]
    return pl.pallas_call(
        _mla_attn_kernel,
        grid=(b, n_q, n_k),
        in_specs=[pl.BlockSpec((1, MLA_HEADS, q_tile, LANES), lambda i, a, j: (i, 0, q_off + a, 0)),
                  pl.BlockSpec((1, MLA_HEADS, kt, LANES), lambda i, a, j: (i, 0, k_off + j, 0)),
                  pl.BlockSpec((1, MLA_HEADS, kt, LANES), lambda i, a, j: (i, 0, k_off + j, 0))],
        out_specs=pl.BlockSpec((1, q_tile, MLA_WIDTH), lambda i, a, j: (i, a, 0)),
        out_shape=jax.ShapeDtypeStruct((b, n_q * q_tile, MLA_WIDTH), F32),
        scratch_shapes=[pltpu.VMEM((MLA_HEADS, q_tile, LANES), F32), pltpu.VMEM((MLA_HEADS, q_tile, LANES), F32)],
        compiler_params=pltpu.CompilerParams(dimension_semantics=("arbitrary", "arbitrary", "arbitrary")),
        name="mla_attn",
    )(q, k, v)


def _mla_mixer(p, q_norm_g, w_uq, kv_norm_g, w_ukv, n_lat_tiles, ctx_out):
    b, r, _ = p.shape
    nt = r // ROW_TILE
    n_lat = n_lat_tiles * ROW_TILE
    onec, sinr = _rope_tables(n_lat, r)
    wq, wqr, wk, wv = _mla_weights(w_uq, w_ukv)
    blk = lambda width, col: pl.BlockSpec((1, ROW_TILE, width), lambda i, t: (i, t, col // width))
    tab = pl.BlockSpec((ROW_TILE, LANES), lambda i, t: (t, 0))
    full = lambda *shape: pl.BlockSpec(shape, lambda i, t: (0,) * len(shape))
    head_out = pl.BlockSpec((1, MLA_HEADS, ROW_TILE, LANES), lambda i, t: (i, 0, t, 0))
    q, k, v = pl.pallas_call(
        _mla_prep_kernel,
        grid=(b, nt),
        in_specs=[blk(MLA_Q_RANK, COL_CQ), blk(LANES, COL_CKV), blk(LANES, COL_KR), blk(LANES, COL_KRROT), tab, tab,
                  full(1, MLA_Q_RANK), full(1, MLA_KV_RANK),
                  full(MLA_HEADS, MLA_Q_RANK, LANES), full(MLA_HEADS, MLA_Q_RANK, LANES),
                  full(MLA_HEADS, MLA_KV_RANK, LANES), full(MLA_HEADS, MLA_KV_RANK, LANES)],
        out_specs=[head_out] * 3,
        out_shape=[jax.ShapeDtypeStruct((b, MLA_HEADS, r, LANES), BF16)] * 3,
        compiler_params=pltpu.CompilerParams(dimension_semantics=("arbitrary", "arbitrary")),
        name="mla_prep",
    )(p, p, p, p, onec, sinr, q_norm_g.reshape(1, -1), kv_norm_g.reshape(1, -1), wq, wqr, wk, wv)
    q_tile = min(MLA_Q_TILE, n_lat)
    k_tile = next(t for t in MLA_K_TILES if r % t == 0)
    y_lat = _mla_attention(q, k, v, q_tile, 0, n_lat // q_tile, k_tile, 0, r // k_tile)
    y_ctx = None
    if ctx_out:
        n_ctx = r - n_lat
        y_ctx = _mla_attention(q, k, v, n_ctx, n_lat // n_ctx, 1, n_ctx, n_lat // n_ctx, 1)
    return y_lat, y_ctx


def _s5_matrices(a_re, a_im, log_dt, b_re, b_im, c_re, c_im):
    q, ng, ns, nc = S5_CHUNK, S5_NGROUPS, S5_STATE, S5_GROUP
    lam = jnp.minimum(a_re.astype(F32), S5_MAX_RE) + 1j * a_im.astype(F32)
    step = jnp.exp(log_dt.astype(F32))[..., None]
    abar = jnp.exp(lam * step)
    bmat = b_re.astype(F32) + 1j * b_im.astype(F32)
    bbar = ((abar - 1.0) / lam)[..., None] * bmat
    cmat = c_re.astype(F32) + 1j * c_im.astype(F32)
    pw = jnp.exp((lam * step)[..., None] * jnp.arange(q + 1, dtype=F32))
    kern = jnp.einsum('dgcn,dgnl,dgnk->dglck', cmat, pw[..., :q], bbar).real
    ii = jnp.arange(q)
    lag_f = ii[None, :] - ii[:, None]
    gather = lambda kd, lag: jnp.where((lag >= 0)[None, :, :, None, None], kd[:, jnp.clip(lag, 0, q - 1)], 0.0)
    t_f = gather(kern[0], lag_f).transpose(0, 1, 4, 2, 3)
    t_b = gather(kern[1], -lag_f).transpose(0, 1, 4, 2, 3)
    t_sum = (t_f + t_b).reshape(ng, q * nc, q * nc)
    pw_f = pw[0][..., q - 1 - ii]
    pw_b = pw[1][..., ii]
    wst = lambda pwd, bb: jnp.einsum('gnj,gnc->gjcn', pwd, bb).reshape(ng, q * nc, ns)
    wst_f, wst_b = wst(pw_f, bbar[0]), wst(pw_b, bbar[1])
    wout = lambda pwd, cm: jnp.einsum('gcn,gni->gnic', cm, pwd).reshape(ng, ns, q * nc)
    wo_f, wo_b = wout(pw[0][..., ii + 1], cmat[0]), wout(pw[1][..., q - ii], cmat[1])
    aq = pw[..., q]

    def pair_cols(x):
        x = x.reshape(S5_PAIRS, 2, x.shape[1], x.shape[2])
        z = jnp.zeros_like(x[:, 0])
        return jnp.concatenate([jnp.concatenate([x[:, 0], z], axis=2), jnp.concatenate([z, x[:, 1]], axis=2)], axis=1)

    w_local = jnp.concatenate([pair_cols(wst_f.real), pair_cols(wst_f.imag),
                               pair_cols(wst_b.real), pair_cols(wst_b.imag)], axis=2)
    w_out = jnp.concatenate([pair_cols(t_sum), pair_cols(wo_f.real), pair_cols(-wo_f.imag),
                             pair_cols(wo_b.real), pair_cols(-wo_b.imag)], axis=1)
    aq_pair = aq.reshape(2, S5_PAIRS, 2 * ns)
    aq_tab = jnp.concatenate([aq_pair[0].real, aq_pair[0].imag, aq_pair[1].real, aq_pair[1].imag], axis=1)
    return w_local.astype(BF16), w_out.astype(BF16), aq_tab.reshape(S5_PAIRS, 1, 8 * ns).astype(F32)


def _s5_perm():
    cols = S5_CHUNK * S5_WIDTH
    c = jnp.arange(cols, dtype=jnp.int32)
    cc, j = c % S5_GROUP, (c // S5_GROUP) % S5_CHUNK
    g = c // (S5_GROUP * S5_CHUNK)
    per_half = LANES // S5_GROUP
    src = (g // per_half) * (S5_CHUNK * LANES) + j * LANES + (g % per_half) * S5_GROUP + cc
    return jnp.where(c[:, None] == src[None, :], 1.0, 0.0).astype(BF16)


def _s5_pack_kernel(u_ref, o_ref):
    n = o_ref.shape[1]
    for j in range(S5_CHUNK):
        o_ref[0, :, j * LANES:(j + 1) * LANES] = u_ref[0, pl.ds(j, n, stride=S5_CHUNK), :].astype(BF16)


def _s5_unpack_kernel(y_ref, o_ref):
    n = y_ref.shape[1]
    for i in range(S5_CHUNK):
        o_ref[0, pl.ds(i, n, stride=S5_CHUNK), :] = y_ref[0, :, i * LANES:(i + 1) * LANES]


def _s5_local_kernel(u_ref, perm_ref, w_ref, up_ref, s_ref):
    up = _mm(u_ref[0], perm_ref[...]).astype(BF16)
    up_ref[0] = up
    s_ref[0] = _mm(up, w_ref[0])


def _s5_scan_kernel(s3_ref, aq_ref, hs3_ref, *, n_lat, n_ctx, nb):
    s_ref, hs_ref = s3_ref.at[0], hs3_ref.at[0]
    w = 2 * S5_STATE
    aq = aq_ref[0]
    a = [aq[:, i * w:(i + 1) * w] for i in range(4)]
    zero = jnp.zeros((nb, w), F32)
    slab = 8
    cps = slab // nb

    def run_slab(s_re, s_im, a_re, a_im, h_re, h_im, order):
        ent_re, ent_im = [None] * cps, [None] * cps
        for c in order:
            ent_re[c], ent_im[c] = h_re, h_im
            rows = slice(c * nb, (c + 1) * nb)
            h_re, h_im = a_re * h_re - a_im * h_im + s_re[rows], a_re * h_im + a_im * h_re + s_im[rows]
        return jnp.concatenate(ent_re, axis=0), jnp.concatenate(ent_im, axis=0), h_re, h_im

    def body(kk, carry):
        f_re, f_im, b_re, b_im = carry
        rf = pl.multiple_of(_scan_chunk(kk, n_lat // cps, n_ctx // cps, False) * slab, slab)
        rb = pl.multiple_of(_scan_chunk(kk, n_lat // cps, n_ctx // cps, True) * slab, slab)
        e_re, e_im, f_re, f_im = run_slab(s_ref[pl.ds(rf, slab), 0:w], s_ref[pl.ds(rf, slab), w:2 * w],
                                          a[0], a[1], f_re, f_im, range(cps))
        hs_ref[pl.ds(rf, slab), 0:w] = e_re
        hs_ref[pl.ds(rf, slab), w:2 * w] = e_im
        e_re, e_im, b_re, b_im = run_slab(s_ref[pl.ds(rb, slab), 2 * w:3 * w], s_ref[pl.ds(rb, slab), 3 * w:4 * w],
                                          a[2], a[3], b_re, b_im, range(cps - 1, -1, -1))
        hs_ref[pl.ds(rb, slab), 2 * w:3 * w] = e_re
        hs_ref[pl.ds(rb, slab), 3 * w:4 * w] = e_im
        return f_re, f_im, b_re, b_im

    lax.fori_loop(0, (n_lat + n_ctx) // cps, body, (zero, zero, zero, zero))


def _s5_out_kernel(up_ref, hs_ref, w_ref, perm_ref, y_ref):
    @pl.when(pl.program_id(1) == 0)
    def _():
        y_ref[...] = jnp.zeros_like(y_ref)

    kw = up_ref.shape[2]
    y_pair = _mm(up_ref[0], w_ref[0, :kw]) + _mm(hs_ref[0].astype(BF16), w_ref[0, kw:])
    y_hi, y_lo = _split_bf16(y_pair)
    y_ref[0] += _mm(y_hi, perm_ref[...], _NT_DIMS) + _mm(y_lo, perm_ref[...], _NT_DIMS)


def _s5_mixer(p, a_re, a_im, log_dt, b_re, b_im, c_re, c_im, n_lat_tiles):
    b, r, _ = p.shape
    q = S5_CHUNK
    n_chunks = r // q
    cols = q * S5_WIDTH
    kw = 2 * q * S5_GROUP
    w_local, w_out, aq_tab = _s5_matrices(a_re, a_im, log_dt, b_re, b_im, c_re, c_im)
    perm = _s5_perm()
    cp2 = pltpu.CompilerParams(dimension_semantics=("arbitrary", "arbitrary"), vmem_limit_bytes=48 * 2 ** 20)
    pack_rows = next(t for t in S5_PACK_ROWS if r % t == 0)
    cpt = pack_rows // q
    halves = S5_WIDTH // LANES
    cp3 = pltpu.CompilerParams(dimension_semantics=("arbitrary", "arbitrary", "arbitrary"))
    chunk_rows = pl.BlockSpec((1, cpt, q * LANES), lambda i, t, hf: (i, t, hf))
    u_big = pl.pallas_call(
        _s5_pack_kernel,
        grid=(b, r // pack_rows, halves),
        in_specs=[pl.BlockSpec((1, pack_rows, LANES), lambda i, t, hf: (i, t, COL_S5 // LANES + hf))],
        out_specs=chunk_rows,
        out_shape=jax.ShapeDtypeStruct((b, n_chunks, cols), BF16),
        compiler_params=cp3, name="s5_pack",
    )(p)
    all_chunks = pl.BlockSpec((1, n_chunks, cols), lambda i, g: (i, 0, 0))
    col_tile = lambda width: pl.BlockSpec((1, n_chunks, width), lambda i, g: (i, 0, g))
    perm_cols = pl.BlockSpec((cols, kw), lambda i, g: (0, g))
    u_pairs, s_loc = pl.pallas_call(
        _s5_local_kernel,
        grid=(b, S5_PAIRS),
        in_specs=[all_chunks, perm_cols, pl.BlockSpec((1, kw, kw), lambda i, g: (g, 0, 0))],
        out_specs=[col_tile(kw), col_tile(kw)],
        out_shape=[jax.ShapeDtypeStruct((b, n_chunks, cols), BF16), jax.ShapeDtypeStruct((b, n_chunks, cols), F32)],
        compiler_params=cp2, name="s5_local",
    )(u_big, perm, w_local)
    n_lat = n_lat_tiles * ROW_TILE // q
    hs = pl.pallas_call(
        functools.partial(_s5_scan_kernel, n_lat=n_lat, n_ctx=n_chunks - n_lat, nb=1),
        grid=(b, S5_PAIRS),
        in_specs=[col_tile(kw), pl.BlockSpec((1, 1, kw), lambda i, g: (g, 0, 0))],
        out_specs=col_tile(kw),
        out_shape=jax.ShapeDtypeStruct((b, n_chunks, cols), F32),
        compiler_params=cp2, name="s5_scan",
    )(s_loc, aq_tab)
    y_big = pl.pallas_call(
        _s5_out_kernel,
        grid=(b, S5_PAIRS),
        in_specs=[col_tile(kw), col_tile(kw), pl.BlockSpec((1, 2 * kw, kw), lambda i, g: (g, 0, 0)), perm_cols],
        out_specs=all_chunks,
        out_shape=jax.ShapeDtypeStruct((b, n_chunks, cols), F32),
        compiler_params=cp2, name="s5_out",
    )(u_pairs, hs, w_out, perm)
    return pl.pallas_call(
        _s5_unpack_kernel,
        grid=(b, r // pack_rows, halves),
        in_specs=[chunk_rows],
        out_specs=pl.BlockSpec((1, pack_rows, LANES), lambda i, t, hf: (i, t, hf)),
        out_shape=jax.ShapeDtypeStruct((b, r, S5_WIDTH), F32),
        compiler_params=cp3, name="s5_unpack",
    )(y_big)


def _post_kernel(h_ref, xs_ref, z_ref, r_ref, u_ref, ssd_ref, ssd_b_ref, gla_ref, gla_b_ref, mla_ref, s5_ref,
                 ssd_d_ref, ssd_g_ref, gla_g_ref, s5_d_ref, glu_w_ref, glu_b_ref, w_out_ref, mod_ref, o_ref):
    y = ssd_ref[0] + ssd_b_ref[0] + ssd_d_ref[...] * xs_ref[0]
    ssd = _rms(y * _silu(z_ref[0]), ssd_g_ref[...])
    o = gla_ref[0] + gla_b_ref[0]
    lane_head = lax.broadcasted_iota(jnp.int32, (1, GLA_WIDTH), 1) >> (GLA_DV.bit_length() - 1)
    ms = jnp.zeros_like(o)
    for h in range(GLA_HEADS):
        oh = o[:, h * GLA_DV:(h + 1) * GLA_DV]
        ms = jnp.where(lane_head == h, jnp.mean(oh * oh, axis=-1, keepdims=True), ms)
    gla = o * lax.rsqrt(ms + NORM_EPS) * gla_g_ref[...] * _silu(r_ref[0])
    y5 = _gelu_erf(s5_ref[0] + s5_d_ref[...] * u_ref[0])
    s5 = y5 * jax.nn.sigmoid(_mm(y5.astype(BF16), glu_w_ref[...]) + glu_b_ref[...])
    mix_in = jnp.concatenate([ssd, gla, mla_ref[0], s5], axis=1).astype(BF16)
    o_ref[0] = h_ref[0] + mod_ref[0] * _mm(mix_in, w_out_ref[...])


def _post(h, p, ssd_xbc, ssd_y, ssd_yb, gla_o, gla_ob, mla_y, s5_y, ssd_d, ssd_norm_g, gla_norm_g, s5_d, glu_w, glu_b, w_out, mod,
          row_off, mla_off):
    b, rows, d = h.shape
    w = GROUP_WIDTH
    pblk = lambda col: pl.BlockSpec((1, ROW_TILE, w), lambda i, t: (i, row_off + t, col // w))
    yblk = pl.BlockSpec((1, ROW_TILE, w), lambda i, t: (i, row_off + t, 0))
    full = lambda *shape: pl.BlockSpec(shape, lambda i, t: (0,) * len(shape))
    vec = lambda x: x.reshape(1, -1).astype(F32)
    n_mod = mod.shape[0]
    return pl.pallas_call(
        _post_kernel,
        grid=(b, rows // ROW_TILE),
        in_specs=[pl.BlockSpec((1, ROW_TILE, d), lambda i, t: (i, t, 0)),
                  yblk, pblk(COL_Z), pblk(COL_GLA_R), pblk(COL_S5), yblk, yblk, yblk, yblk,
                  pl.BlockSpec((1, ROW_TILE, w), lambda i, t: (i, mla_off + t, 0)), yblk,
                  full(1, w), full(1, w), full(1, w), full(1, w), full(w, w), full(1, w), full(d, d),
                  pl.BlockSpec((1, 1, d), lambda i, t: (jnp.minimum(i, n_mod - 1), 0, 0))],
        out_specs=pl.BlockSpec((1, ROW_TILE, d), lambda i, t: (i, t, 0)),
        out_shape=jax.ShapeDtypeStruct((b, rows, d), F32),
        compiler_params=pltpu.CompilerParams(dimension_semantics=("arbitrary", "arbitrary")),
        name="mix_post",
    )(h, ssd_xbc, p, p, p, ssd_y, ssd_yb, gla_o, gla_ob, mla_y, s5_y,
      vec(jnp.repeat(ssd_d, SSD_HEAD_DIM)), vec(ssd_norm_g), vec(jnp.tile(gla_norm_g, GLA_HEADS)), vec(s5_d),
      glu_w.astype(BF16), vec(glu_b), w_out.astype(BF16), mod)


PEER_ROUTE_TOKENS = 256
PEER_ROUTE_UNROLL = 4
PEER_GATE_TOKENS = 256
PEER_GATE_UNROLL = 64
PEER_GATE_SUBLANES = 8
PEER_DENSE_TOKENS = 512
PEER_DENSE_EXPERTS = 2 * PEER_GATE_SUBLANES * PEER_KEYS
PEER_SLOTS = PEER_HEADS * PEER_TOPK


def _topk_rows(s, k):
    n_rows = s.shape[0]
    rows = lax.broadcasted_iota(jnp.int32, s.shape, 0)
    vals, idxs = [], []
    for _ in range(k):
        m = jnp.max(s, axis=0, keepdims=True)
        idx = jnp.min(jnp.where(s == m, rows, n_rows), axis=0, keepdims=True)
        vals.append(m)
        idxs.append(idx)
        s = jnp.where(rows == idx, -jnp.inf, s)
    return jnp.concatenate(vals, axis=0), jnp.concatenate(idxs, axis=0)


def _select_rows(pos, table):
    out = jnp.zeros(pos.shape, table.dtype)
    for r in range(table.shape[0]):
        out = jnp.where(pos == r, table[r:r + 1, :], out)
    return out


def _peer_route_kernel(h_ref, g_ref, shift_ref, scale_ref, wq_hi_ref, wq_lo_ref, k_hi_ref, k_lo_ref,
                       xn_ref, i1_ref, i2_ref, gate_ref, q_scr, slot_scr):
    xn = _modulated_norm(h_ref[...], g_ref[...], shift_ref[0], scale_ref[0])
    xn_ref[...] = xn.astype(BF16)
    x_hi, x_lo = _split_bf16(xn)
    q_scr[...] = _dot3(wq_hi_ref[...], wq_lo_ref[...], x_hi, x_lo, _NT_DIMS)
    half = PEER_DQ // 2

    def head_body(h, carry):
        base = pl.multiple_of(h * PEER_DQ, PEER_DQ)
        tops = []
        for j in range(2):
            qq = q_scr[pl.ds(base + j * half, half), :]
            q_hi, q_lo = _split_bf16(qq)
            s = _dot3(k_hi_ref[j, h], k_lo_ref[j, h], q_hi, q_lo, _NN_DIMS)
            tops.append(_topk_rows(s, PEER_TOPK))
        (v1, i1), (v2, i2) = tops
        pieces = [v1[a:a + 1, :] + v2[:PEER_TOPK // (a + 1), :] for a in range(PEER_TOPK)]
        n_cand = sum(PEER_TOPK // (a + 1) for a in range(PEER_TOPK))
        pad = -n_cand % 8
        cand = jnp.concatenate(pieces + [jnp.full((pad, v1.shape[1]), -jnp.inf, F32)], axis=0)
        best, pos = _topk_rows(cand, PEER_TOPK)
        e = jnp.exp(best - best[0:1, :])
        gates = e / jnp.sum(e, axis=0, keepdims=True)
        a_idx = jnp.zeros_like(pos)
        start = jnp.zeros_like(pos)
        first = 0
        for a in range(1, PEER_TOPK):
            width = PEER_TOPK // a
            first += width
            reached = pos >= first
            a_idx = a_idx + jnp.where(reached, 1, 0)
            start = start + jnp.where(reached, width, 0)
        row0 = pl.multiple_of(h * PEER_TOPK, PEER_TOPK)
        slot_scr[0, pl.ds(row0, PEER_TOPK), :] = _select_rows(a_idx, i1).astype(F32)
        slot_scr[1, pl.ds(row0, PEER_TOPK), :] = _select_rows(pos - start, i2).astype(F32)
        slot_scr[2, pl.ds(row0, PEER_TOPK), :] = gates
        return carry

    lax.fori_loop(0, PEER_HEADS, head_body, 0, unroll=PEER_ROUTE_UNROLL)
    i1_ref[...] = slot_scr[0].T.astype(jnp.int32)
    i2_ref[...] = slot_scr[1].T.astype(jnp.int32)
    gate_ref[...] = slot_scr[2].T


def _bf16_bits(x):
    return pltpu.bitcast(x.astype(BF16).astype(F32), jnp.uint32)


def _peer_gate_kernel(i1_ref, i2_ref, gate_ref, g_ref):
    rows = lax.broadcasted_iota(jnp.int32, (PEER_KEYS, PEER_SLOTS), 0)
    sub = PEER_GATE_SUBLANES

    def token_body(t, carry):
        a = i1_ref[pl.ds(t, 1), :]
        b = i2_ref[pl.ds(t, 1), :]
        w = gate_ref[pl.ds(t, 1), :]
        lhs = jnp.where(rows == a, w, 0.0).astype(BF16)
        rhs = jnp.where(rows == b, 1.0, 0.0).astype(BF16)
        gt = _mm(lhs, rhs, _NT_DIMS)
        row0 = pl.multiple_of(t * sub, sub)
        for g in range(PEER_KEYS // (2 * sub)):
            lo = gt[2 * sub * g:2 * sub * g + sub]
            hi = gt[2 * sub * g + sub:2 * sub * (g + 1)]
            g_ref[g, pl.ds(row0, sub), :] = (_bf16_bits(lo) >> 16) | _bf16_bits(hi)
        return carry

    lax.fori_loop(0, i1_ref.shape[0], token_body, 0, unroll=PEER_GATE_UNROLL)


def _peer_dense_kernel(xn_ref, u_ref, v_ref, gpk_ref, h_ref, mod_ref, out_g_ref, o_ref, acc_ref, *, out_norm):
    j = pl.program_id(1)

    @pl.when(j == 0)
    def _():
        acc_ref[...] = jnp.zeros_like(acc_ref)

    sub = PEER_GATE_SUBLANES
    xn = xn_ref[...]
    tokens = xn.shape[0]
    words = [gpk_ref[0, pl.ds(r, tokens, stride=sub), :] for r in range(sub)]
    for half in range(2):
        rows = slice(half * sub * PEER_KEYS, (half + 1) * sub * PEER_KEYS)
        hid = _gelu_erf(_mm(xn, u_ref[0, rows, :], _NT_DIMS))
        ys = []
        for r in range(sub):
            bits = (words[r] << 16) if half == 0 else (words[r] & jnp.uint32(0xFFFF0000))
            ys.append(pltpu.bitcast(bits, F32) * hid[:, r * PEER_KEYS:(r + 1) * PEER_KEYS])
        y = jnp.concatenate(ys, axis=1).astype(BF16)
        acc_ref[...] += _mm(y, v_ref[0, rows, :])

    @pl.when(j == pl.num_programs(1) - 1)
    def _():
        out = h_ref[...] + mod_ref[0] * acc_ref[...]
        o_ref[...] = _rms(out, out_g_ref[...]) if out_norm else out


def _peer_layer(h, norm_g, shift, scale, gate_mod, wq_t_hi, wq_t_lo, keys_hi, keys_lo, u_bf, v_bf, layer,
                out_norm_g=None):
    n, d = h.shape
    nb = shift.shape[0]
    rows_per_batch = n // nb
    tr = min(PEER_ROUTE_TOKENS, rows_per_batch)
    full = lambda *shape: pl.BlockSpec(shape, lambda i: (0,) * len(shape))
    per_batch = lambda t: pl.BlockSpec((1, 1, d), lambda i: (i * t // rows_per_batch, 0, 0))
    xn, i1, i2, gate = pl.pallas_call(
        _peer_route_kernel,
        grid=(n // tr,),
        in_specs=[pl.BlockSpec((tr, d), lambda i: (i, 0)), full(1, d), per_batch(tr), per_batch(tr),
                  full(PEER_HEADS * PEER_DQ, d), full(PEER_HEADS * PEER_DQ, d),
                  full(2, PEER_HEADS, PEER_KEYS, PEER_DQ // 2), full(2, PEER_HEADS, PEER_KEYS, PEER_DQ // 2)],
        out_specs=[pl.BlockSpec((tr, d), lambda i: (i, 0))] + [pl.BlockSpec((tr, PEER_SLOTS), lambda i: (i, 0))] * 3,
        out_shape=[jax.ShapeDtypeStruct((n, d), BF16),
                   jax.ShapeDtypeStruct((n, PEER_SLOTS), jnp.int32),
                   jax.ShapeDtypeStruct((n, PEER_SLOTS), jnp.int32),
                   jax.ShapeDtypeStruct((n, PEER_SLOTS), F32)],
        scratch_shapes=[pltpu.VMEM((PEER_HEADS * PEER_DQ, tr), F32), pltpu.VMEM((3, PEER_SLOTS, tr), F32)],
        compiler_params=pltpu.CompilerParams(dimension_semantics=("arbitrary",)),
        name="peer_route",
    )(h, norm_g.reshape(1, d), shift, scale, wq_t_hi, wq_t_lo, keys_hi, keys_lo)

    tg = min(PEER_GATE_TOKENS, n)
    n_planes = PEER_KEYS // (2 * PEER_GATE_SUBLANES)
    slot_spec = pl.BlockSpec((tg, PEER_SLOTS), lambda i: (i, 0))
    gmat = pl.pallas_call(
        _peer_gate_kernel,
        grid=(n // tg,),
        in_specs=[slot_spec, slot_spec, slot_spec],
        out_specs=pl.BlockSpec((n_planes, tg * PEER_GATE_SUBLANES, PEER_KEYS), lambda i: (0, i, 0)),
        out_shape=jax.ShapeDtypeStruct((n_planes, n * PEER_GATE_SUBLANES, PEER_KEYS), jnp.uint32),
        compiler_params=pltpu.CompilerParams(dimension_semantics=("arbitrary",)),
        name="peer_gate",
    )(i1, i2, gate)

    tm = min(PEER_DENSE_TOKENS, rows_per_batch)
    te = PEER_DENSE_EXPERTS
    return pl.pallas_call(
        functools.partial(_peer_dense_kernel, out_norm=out_norm_g is not None),
        grid=(n // tm, PEER_EXPERTS // te),
        in_specs=[pl.BlockSpec((tm, d), lambda i, j: (i, 0)),
                  pl.BlockSpec((1, te, d), lambda i, j: (layer, j, 0)),
                  pl.BlockSpec((1, te, d), lambda i, j: (layer, j, 0)),
                  pl.BlockSpec((1, tm * PEER_GATE_SUBLANES, PEER_KEYS), lambda i, j: (j, i, 0)),
                  pl.BlockSpec((tm, d), lambda i, j: (i, 0)),
                  pl.BlockSpec((1, 1, d), lambda i, j: (i * tm // rows_per_batch, 0, 0)),
                  pl.BlockSpec((1, d), lambda i, j: (0, 0))],
        out_specs=pl.BlockSpec((tm, d), lambda i, j: (i, 0)),
        out_shape=jax.ShapeDtypeStruct((n, d), F32),
        scratch_shapes=[pltpu.VMEM((tm, d), F32)],
        compiler_params=pltpu.CompilerParams(dimension_semantics=("arbitrary", "arbitrary"),
                                             vmem_limit_bytes=52 * 2 ** 20),
        name="peer_dense",
    )(xn, u_bf, v_bf, gmat, h, gate_mod, jnp.ones((1, d), F32) if out_norm_g is None else out_norm_g.reshape(1, d))


def _mix_layer(h_lat, h_ctx, mod_l, mod_c, norm_g, w_in, w_out, ssd, gla, mla, s5, ctx_out):
    b, n_lat, d = h_lat.shape
    n_lat_tiles = n_lat // ROW_TILE
    tab = lambda k: jnp.concatenate([mod_l[k], mod_c[k]], axis=0)
    p = _inproj(h_lat, h_ctx, norm_g, tab(0), tab(1), _pack_w_in(w_in))
    ssd_y, ssd_yb, ssd_xbc = _ssd_mixer(p, ssd["conv_w"], ssd["conv_b"], ssd["a_log"], ssd["dt_bias"], n_lat_tiles)
    gla_o, gla_ob = _gla_mixer(p, gla["gate_w"], gla["gate_b"], n_lat_tiles)
    mla_lat, mla_ctx = _mla_mixer(p, mla["q_norm_g"], mla["w_uq"], mla["kv_norm_g"], mla["w_ukv"], n_lat_tiles, ctx_out)
    s5_y = _s5_mixer(p, s5["a_re"], s5["a_im"], s5["log_dt"], s5["b_re"], s5["b_im"], s5["c_re"], s5["c_im"],
                     n_lat_tiles)
    post = functools.partial(_post, p=p, ssd_xbc=ssd_xbc, ssd_y=ssd_y, ssd_yb=ssd_yb, gla_o=gla_o, gla_ob=gla_ob,
                             s5_y=s5_y, ssd_d=ssd["d"],
                             ssd_norm_g=ssd["norm_g"], gla_norm_g=gla["norm_g"], s5_d=s5["d"],
                             glu_w=s5["glu_w"], glu_b=s5["glu_b"], w_out=w_out)
    new_lat = post(h_lat, mla_y=mla_lat, mod=mod_l[2], row_off=0, mla_off=0)
    new_ctx = None
    if ctx_out:
        new_ctx = post(h_ctx, mla_y=mla_ctx, mod=mod_c[2], row_off=n_lat_tiles, mla_off=0)
    return new_lat, new_ctx


def kernel(x, c, ctx, c_ctx, ada_w, ada_b, norm_mix_g, norm_ffn_g, w_in, w_out,
           ssd_conv_w, ssd_conv_b, ssd_a_log, ssd_dt_bias, ssd_d, ssd_norm_g,
           gla_gate_w, gla_gate_b, gla_norm_g, mla_q_norm_g, mla_w_uq, mla_kv_norm_g, mla_w_ukv,
           s5_a_re, s5_a_im, s5_log_dt, s5_b_re, s5_b_im, s5_c_re, s5_c_im, s5_d, s5_glu_w, s5_glu_b,
           peer_w_q, peer_sub_keys, peer_u, peer_v, final_norm_g):
    h_lat, h_ctx = x, ctx
    cond_lat = jax.nn.silu(c)[:, None, :]
    cond_ctx = jax.nn.silu(c_ctx)[None, None, :]
    u_bf, v_bf = peer_u.astype(BF16), peer_v.astype(BF16)
    for i in range(DEPTH):
        ctx_out = i < DEPTH - 1
        mod_l = jnp.split(cond_lat @ ada_w[i] + ada_b[i], N_MOD, axis=-1)
        mod_c = jnp.split(cond_ctx @ ada_w[i] + ada_b[i], N_MOD, axis=-1)
        ssd = dict(conv_w=ssd_conv_w[i], conv_b=ssd_conv_b[i], a_log=ssd_a_log[i], dt_bias=ssd_dt_bias[i],
                   d=ssd_d[i], norm_g=ssd_norm_g[i])
        gla = dict(gate_w=gla_gate_w[i], gate_b=gla_gate_b[i], norm_g=gla_norm_g[i])
        mla = dict(q_norm_g=mla_q_norm_g[i], w_uq=mla_w_uq[i], kv_norm_g=mla_kv_norm_g[i], w_ukv=mla_w_ukv[i])
        s5 = dict(a_re=s5_a_re[i], a_im=s5_a_im[i], log_dt=s5_log_dt[i], b_re=s5_b_re[i], b_im=s5_b_im[i],
                  c_re=s5_c_re[i], c_im=s5_c_im[i], d=s5_d[i], glu_w=s5_glu_w[i], glu_b=s5_glu_b[i])
        h_lat, h_ctx_new = _mix_layer(h_lat, h_ctx, mod_l, mod_c, norm_mix_g[i], w_in[i], w_out[i],
                                      ssd, gla, mla, s5, ctx_out)
        wq_t_hi, wq_t_lo = _split_bf16(peer_w_q[i].T)
        keys_hi, keys_lo = _split_bf16(peer_sub_keys[i])
        peer = functools.partial(_peer_layer, norm_g=norm_ffn_g[i], wq_t_hi=wq_t_hi, wq_t_lo=wq_t_lo,
                                 keys_hi=keys_hi, keys_lo=keys_lo, u_bf=u_bf, v_bf=v_bf, layer=i)
        h_lat = peer(h_lat.reshape(-1, D_MODEL), shift=mod_l[3], scale=mod_l[4], gate_mod=mod_l[5],
                     out_norm_g=final_norm_g if i == DEPTH - 1 else None).reshape(h_lat.shape)
        if ctx_out:
            h_ctx = peer(h_ctx_new.reshape(-1, D_MODEL), shift=mod_c[3], scale=mod_c[4],
                         gate_mod=mod_c[5]).reshape(h_ctx.shape)
    return h_lat
```

```python
import functools
import jax
import jax.numpy as jnp
from jax import lax
import numpy as np
from jax.experimental import pallas as pl
from jax.experimental.pallas import tpu as pltpu

D_MODEL = 1024
DEPTH = 2
GRID_W = 64
NORM_EPS = 1e-6
N_MOD = 6

GROUP_WIDTH = D_MODEL // 4

SSD_WIDTH = GROUP_WIDTH
SSD_HEAD_DIM = 64
SSD_HEADS = SSD_WIDTH // SSD_HEAD_DIM
SSD_GROUPS = 2
SSD_STATE = 128
SSD_CONV = 5
SSD_CHUNK = 128
SSD_CONV_CH = SSD_WIDTH + 2 * SSD_GROUPS * SSD_STATE
SSD_IN = SSD_WIDTH + SSD_CONV_CH + 2 * SSD_HEADS

GLA_WIDTH = GROUP_WIDTH
GLA_HEADS = 4
GLA_DV = GLA_WIDTH // GLA_HEADS
GLA_DK = GLA_DV // 2
GLA_QK = GLA_HEADS * GLA_DK
GLA_GATE_RANK = 16
GLA_TAU = 16.0
GLA_CHUNK = 64
GLA_IN = 2 * GLA_QK + 2 * GLA_WIDTH + 2 * GLA_GATE_RANK

MLA_WIDTH = GROUP_WIDTH
MLA_HEADS = 4
MLA_V = MLA_WIDTH // MLA_HEADS
MLA_NOPE = 64
MLA_ROPE = 32
MLA_Q_RANK = 256
MLA_KV_RANK = 128
MLA_SCALE = (MLA_NOPE + MLA_ROPE) ** -0.5
ROPE_BASE = 10000.0
MLA_IN = MLA_Q_RANK + MLA_KV_RANK + MLA_ROPE

S5_WIDTH = GROUP_WIDTH
S5_GROUP = 16
S5_NGROUPS = S5_WIDTH // S5_GROUP
S5_STATE = 64
S5_MAX_RE = -1e-4
S5_IN = S5_WIDTH
S5_CHUNK = 16
S5_PAIRS = S5_NGROUPS // 2
S5_PACK_ROWS = (768, 256)

PEER_KEYS = 128
PEER_EXPERTS = PEER_KEYS * PEER_KEYS
PEER_HEADS = 8
PEER_TOPK = 16
PEER_DQ = 128

LANES = 128
ROW_TILE = 256
SCAN_STEP_ROWS = ROW_TILE

F32 = jnp.float32
BF16 = jnp.bfloat16

COL_XS, COL_BM, COL_CM, COL_Z = 0, 256, 512, 768
COL_GLA_V, COL_GLA_R, COL_CQ, COL_S5 = 1024, 1280, 1536, 1792
COL_GLA_Q, COL_GLA_K, COL_CKV, COL_DT, COL_GLR, COL_KR, COL_KRROT = 2048, 2176, 2304, 2432, 2560, 2688, 2816
P_COLS = 2944

_NN_DIMS = (((1,), (0,)), ((), ()))
_NT_DIMS = (((1,), (1,)), ((), ()))
_TN_DIMS = (((0,), (0,)), ((), ()))


def _mm(a, b, dims=_NN_DIMS):
    return lax.dot_general(a, b, dims, preferred_element_type=F32)


def _split_bf16(x):
    hi = x.astype(BF16)
    lo = (x - hi.astype(F32)).astype(BF16)
    return hi, lo


def _split3_bf16(x):
    p1 = x.astype(BF16)
    r1 = x - p1.astype(F32)
    p2 = r1.astype(BF16)
    p3 = (r1 - p2.astype(F32)).astype(BF16)
    return p1, p2, p3


def _dot3(a_hi, a_lo, b_hi, b_lo, dims):
    return _mm(a_hi, b_hi, dims) + _mm(a_hi, b_lo, dims) + _mm(a_lo, b_hi, dims)


def _gelu_erf(x):
    return 0.5 * x * (1.0 + lax.erf(x * (2.0 ** -0.5)))


def _silu(x):
    return x * jax.nn.sigmoid(x)


def _softplus(x):
    return jnp.maximum(x, 0.0) + jnp.log1p(jnp.exp(-jnp.abs(x)))


def _log_sigmoid(x):
    return jnp.minimum(x, 0.0) - jnp.log1p(jnp.exp(-jnp.abs(x)))


def _rms(x, g):
    return x * lax.rsqrt(jnp.mean(x * x, axis=-1, keepdims=True) + NORM_EPS) * g


def _modulated_norm(x, g, shift, scale):
    return _rms(x, g) * (1.0 + scale) + shift


def _causal_mask(n, reverse):
    ri = lax.broadcasted_iota(jnp.int32, (n, n), 0)
    ci = lax.broadcasted_iota(jnp.int32, (n, n), 1)
    return (ci >= ri) if reverse else (ci <= ri)


def _scan_chunk(s, n_lat, n_ctx, reverse):
    if reverse:
        return n_lat + n_ctx - 1 - s
    return jnp.where(s < n_ctx, n_lat + s, s - n_ctx)


def _inproj_kernel(lat_ref, ctx_ref, g_ref, shift_ref, scale_ref, w_ref, o_ref, *, n_lat_tiles):
    t = pl.program_id(1)
    x = jnp.where(t < n_lat_tiles, lat_ref[0], ctx_ref[0])
    xn = _modulated_norm(x, g_ref[...], shift_ref[0], scale_ref[0])
    o_ref[0] = _mm(xn.astype(BF16), w_ref[...])


def _inproj(h_lat, h_ctx, norm_g, shift_tab, scale_tab, w_pad):
    b, n_lat, d = h_lat.shape
    n_lat_tiles = n_lat // ROW_TILE
    n_tiles = n_lat_tiles + h_ctx.shape[1] // ROW_TILE
    mod_spec = pl.BlockSpec((1, 1, d), lambda i, t: (jnp.where(t < n_lat_tiles, i, b), 0, 0))
    return pl.pallas_call(
        functools.partial(_inproj_kernel, n_lat_tiles=n_lat_tiles),
        grid=(b, n_tiles),
        in_specs=[pl.BlockSpec((1, ROW_TILE, d), lambda i, t: (i, jnp.minimum(t, n_lat_tiles - 1), 0)),
                  pl.BlockSpec((1, ROW_TILE, d), lambda i, t: (i, jnp.maximum(t - n_lat_tiles, 0), 0)),
                  pl.BlockSpec((1, d), lambda i, t: (0, 0)), mod_spec, mod_spec,
                  pl.BlockSpec((d, P_COLS), lambda i, t: (0, 0))],
        out_specs=pl.BlockSpec((1, ROW_TILE, P_COLS), lambda i, t: (i, t, 0)),
        out_shape=jax.ShapeDtypeStruct((b, n_tiles * ROW_TILE, P_COLS), F32),
        compiler_params=pltpu.CompilerParams(dimension_semantics=("arbitrary", "arbitrary"),
                                             vmem_limit_bytes=48 * 2 ** 20),
        name="inproj",
    )(h_lat, h_ctx, norm_g.reshape(1, d), shift_tab, scale_tab, w_pad)


def _pack_w_in(w):
    o_ssd, o_gla, o_mla, o_s5 = 0, SSD_IN, SSD_IN + GLA_IN, SSD_IN + GLA_IN + MLA_IN
    out = jnp.zeros((w.shape[0], P_COLS), F32)
    put = lambda out, col, src, width: out.at[:, col:col + width].set(w[:, src:src + width])
    out = put(out, COL_Z, o_ssd, SSD_WIDTH)
    out = put(out, COL_XS, o_ssd + SSD_WIDTH, SSD_CONV_CH)
    out = put(out, COL_DT, o_ssd + SSD_WIDTH + SSD_CONV_CH, 2 * SSD_HEADS)
    out = put(out, COL_GLA_Q, o_gla, GLA_QK)
    out = put(out, COL_GLA_K, o_gla + GLA_QK, GLA_QK)
    out = put(out, COL_GLA_V, o_gla + 2 * GLA_QK, GLA_WIDTH)
    out = put(out, COL_GLA_R, o_gla + 2 * GLA_QK + GLA_WIDTH, GLA_WIDTH)
    out = put(out, COL_GLR, o_gla + 2 * GLA_QK + 2 * GLA_WIDTH, 2 * GLA_GATE_RANK)
    out = put(out, COL_CQ, o_mla, MLA_Q_RANK)
    out = put(out, COL_CKV, o_mla + MLA_Q_RANK, MLA_KV_RANK)
    o_kr = o_mla + MLA_Q_RANK + MLA_KV_RANK
    half = MLA_ROPE // 2
    out = put(out, COL_KR + MLA_NOPE, o_kr, MLA_ROPE)
    out = out.at[:, COL_KRROT + MLA_NOPE:COL_KRROT + MLA_NOPE + half].set(-w[:, o_kr + half:o_kr + MLA_ROPE])
    out = out.at[:, COL_KRROT + MLA_NOPE + half:COL_KRROT + MLA_NOPE + MLA_ROPE].set(w[:, o_kr:o_kr + half])
    out = put(out, COL_S5, o_s5, S5_WIDTH)
    return out.astype(BF16)


def _ssd_prep_kernel(x_ref, prev_ref, next_ref, dt_ref, w_ref, b_ref, bias_ref, xbc_ref, dtc_ref, dtt_ref,
                     *, n_lat_tiles):
    t = pl.program_id(1)
    x = x_ref[0]
    halo = prev_ref.shape[1]
    prev = jnp.where(jnp.logical_and(t > 0, t < n_lat_tiles), prev_ref[0], 0.0)
    nxt = jnp.where(t < n_lat_tiles - 1, next_ref[0], 0.0)
    ext = jnp.concatenate([prev, x, nxt], axis=0)
    rows = ext.shape[0]
    left = SSD_CONV // 2
    acc = jnp.zeros_like(x) + b_ref[...]
    for k in range(SSD_CONV):
        shifted = ext if k == left else pltpu.roll(ext, (left - k) % rows, 0)
        acc = acc + w_ref[k:k + 1, :] * shifted[halo:halo + x.shape[0]]
    xbc_ref[0] = _silu(acc)
    dt = _softplus(dt_ref[0] + bias_ref[...])
    dtc_ref[0] = dt
    dtt_ref[0] = dt.T[:dtt_ref.shape[1]]


def _ssd_scan_kernel(xbc_f_ref, dtc_f_ref, dtt_f_ref, xbc_b_ref, dtc_b_ref, dtt_b_ref, ahr_ref, ahc_ref,
                     yf_ref, yb_ref, state_ref):
    @pl.when(pl.program_id(0) == 0)
    def _():
        state_ref[...] = jnp.zeros_like(state_ref)

    q = SSD_CHUNK
    n_sub = xbc_f_ref.shape[1] // q
    for k in range(n_sub):
        for direction, (xbc_ref, dtc_ref, dtt_ref, y_ref) in enumerate(
                ((xbc_f_ref, dtc_f_ref, dtt_f_ref, yf_ref), (xbc_b_ref, dtc_b_ref, dtt_b_ref, yb_ref))):
            c = k if direction == 0 else n_sub - 1 - k
            rows = slice(c * q, (c + 1) * q)
            for bi in range(xbc_ref.shape[0]):
                y_ref[bi, rows] = _ssd_chunk(xbc_ref[bi, rows], dtc_ref[bi, rows], dtt_ref[bi, :, rows], ahr_ref,
                                             ahc_ref, state_ref.at[bi, direction], direction)


def _ssd_chunk(xbc, dtc, dtt, ahr_ref, ahc_ref, state_ref, direction):
    reverse = direction == 1
    q = SSD_CHUNK
    mask = _causal_mask(q, reverse)
    tri = jnp.where(mask, 1.0, 0.0).astype(BF16)
    xs, bm, cm = xbc[:, :SSD_WIDTH], xbc[:, SSD_WIDTH:SSD_WIDTH + 256], xbc[:, SSD_WIDTH + 256:]
    a_col = dtc * ahr_ref[...]
    a_row = dtt * ahc_ref[...]
    acum_col = sum(_mm(tri, part) for part in _split3_bf16(a_col))
    acum_row = sum(_mm(part, tri, _NT_DIMS) for part in _split3_bf16(a_row))
    end = 0 if reverse else q - 1
    bm_bf, cm_bf = bm.astype(BF16), cm.astype(BF16)
    ys = []
    cb = {}
    for h in range(SSD_HEADS):
        g = h // (SSD_HEADS // SSD_GROUPS)
        gs = slice(g * SSD_STATE, (g + 1) * SSD_STATE)
        if g not in cb:
            cb[g] = _mm(cm_bf[:, gs], bm_bf[:, gs], _NT_DIMS)
        ch = direction * SSD_HEADS + h
        ac = acum_col[:, ch:ch + 1]
        ar = acum_row[ch:ch + 1, :]
        decay = jnp.exp(jnp.where(mask, ac - ar, -jnp.inf))
        xd = xs[:, h * SSD_HEAD_DIM:(h + 1) * SSD_HEAD_DIM] * dtc[:, ch:ch + 1]
        y_diag = _mm((cb[g] * decay).astype(BF16), xd.astype(BF16))
        a_end = ac[end:end + 1, :]
        st_local = _mm((xd * jnp.exp(a_end - ac)).astype(BF16), bm_bf[:, gs], _TN_DIMS)
        hs = state_ref[h]
        y_off = jnp.exp(ac) * _mm(cm_bf[:, gs], hs.astype(BF16), _NT_DIMS)
        state_ref[h] = jnp.exp(a_end) * hs + st_local
        ys.append(y_diag + y_off)
    return jnp.concatenate(ys, axis=1)


def _ssd_mixer(p, conv_w, conv_b, a_log, dt_bias, n_lat_tiles):
    b, r, _ = p.shape
    nt = r // ROW_TILE
    halo = 8
    hb = ROW_TILE // halo
    w8 = jnp.zeros((8, SSD_CONV_CH), F32).at[:SSD_CONV].set(conv_w)
    bias = jnp.zeros((1, LANES), F32).at[0, :2 * SSD_HEADS].set(dt_bias.reshape(-1))
    xbc, dtc, dtt = pl.pallas_call(
        functools.partial(_ssd_prep_kernel, n_lat_tiles=n_lat_tiles),
        grid=(b, nt),
        in_specs=[pl.BlockSpec((1, ROW_TILE, SSD_CONV_CH), lambda i, t: (i, t, 0)),
                  pl.BlockSpec((1, halo, SSD_CONV_CH), lambda i, t: (i, jnp.maximum(t * hb - 1, 0), 0)),
                  pl.BlockSpec((1, halo, SSD_CONV_CH), lambda i, t: (i, jnp.minimum((t + 1) * hb, nt * hb - 1), 0)),
                  pl.BlockSpec((1, ROW_TILE, LANES), lambda i, t: (i, t, COL_DT // LANES)),
                  pl.BlockSpec((8, SSD_CONV_CH), lambda i, t: (0, 0)),
                  pl.BlockSpec((1, SSD_CONV_CH), lambda i, t: (0, 0)),
                  pl.BlockSpec((1, LANES), lambda i, t: (0, 0))],
        out_specs=[pl.BlockSpec((1, ROW_TILE, SSD_CONV_CH), lambda i, t: (i, t, 0)),
                   pl.BlockSpec((1, ROW_TILE, LANES), lambda i, t: (i, t, 0)),
                   pl.BlockSpec((1, 8, ROW_TILE), lambda i, t: (i, 0, t))],
        out_shape=[jax.ShapeDtypeStruct((b, r, SSD_CONV_CH), F32),
                   jax.ShapeDtypeStruct((b, r, LANES), F32),
                   jax.ShapeDtypeStruct((b, 8, r), F32)],
        compiler_params=pltpu.CompilerParams(dimension_semantics=("arbitrary", "arbitrary")),
        name="ssd_prep",
    )(p, p, p, p, w8, conv_b.reshape(1, -1), bias)

    a_head = -jnp.exp(a_log.astype(F32)).reshape(-1)
    ahr = jnp.zeros((1, LANES), F32).at[0, :2 * SSD_HEADS].set(a_head)
    ahc = a_head.reshape(2 * SSD_HEADS, 1)
    blk = SCAN_STEP_ROWS
    n_lat = n_lat_tiles * ROW_TILE // blk
    n_ctx = r // blk - n_lat
    in_specs, y_specs = [], []
    for reverse in (False, True):
        cidx = functools.partial(_scan_chunk, n_lat=n_lat, n_ctx=n_ctx, reverse=reverse)
        in_specs += [pl.BlockSpec((b, blk, SSD_CONV_CH), lambda s, cidx=cidx: (0, cidx(s), 0)),
                     pl.BlockSpec((b, blk, LANES), lambda s, cidx=cidx: (0, cidx(s), 0)),
                     pl.BlockSpec((b, 8, blk), lambda s, cidx=cidx: (0, 0, cidx(s)))]
        y_specs.append(pl.BlockSpec((b, blk, SSD_WIDTH), lambda s, cidx=cidx: (0, cidx(s), 0)))
    in_specs += [pl.BlockSpec((1, LANES), lambda s: (0, 0)), pl.BlockSpec((2 * SSD_HEADS, 1), lambda s: (0, 0))]
    y_f, y_b = pl.pallas_call(
        _ssd_scan_kernel,
        grid=(n_lat + n_ctx,),
        in_specs=in_specs,
        out_specs=y_specs,
        out_shape=[jax.ShapeDtypeStruct((b, r, SSD_WIDTH), F32)] * 2,
        scratch_shapes=[pltpu.VMEM((b, 2, SSD_HEADS, SSD_HEAD_DIM, SSD_STATE), F32)],
        compiler_params=pltpu.CompilerParams(dimension_semantics=("arbitrary",)),
        name="ssd_scan",
    )(xbc, dtc, dtt, xbc, dtc, dtt, ahr, ahc)
    return y_f, y_b, xbc


def _gla_scan_kernel(qf_ref, kf_ref, vf_ref, glrf_ref, qb_ref, kb_ref, vb_ref, glrb_ref, wg_ref, bias_ref,
                     of_ref, ob_ref, st_ref):
    @pl.when(pl.program_id(0) == 0)
    def _():
        st_ref[...] = jnp.zeros_like(st_ref)

    n = GLA_CHUNK
    n_sub = qf_ref.shape[1] // n
    for kk in range(n_sub):
        for direction, (q_ref, k_ref, v_ref, glr_ref, o_ref) in enumerate(
                ((qf_ref, kf_ref, vf_ref, glrf_ref, of_ref), (qb_ref, kb_ref, vb_ref, glrb_ref, ob_ref))):
            c = kk if direction == 0 else n_sub - 1 - kk
            rows = slice(c * n, (c + 1) * n)
            for bi in range(q_ref.shape[0]):
                o_ref[bi, rows] = _gla_chunk(q_ref[bi, rows], k_ref[bi, rows], v_ref[bi, rows], glr_ref[bi, rows],
                                             wg_ref.at[direction], bias_ref.at[direction],
                                             st_ref.at[bi, direction], direction == 1)


def _gla_chunk(q, k, v, glr, wg_ref, bias_ref, st_ref, reverse):
    n = GLA_CHUNK
    mask = _causal_mask(n, reverse)
    tri = jnp.where(mask, 1.0, 0.0).astype(BF16)
    g_hi, g_lo = _split_bf16(glr)
    logits = _dot3(g_hi, g_lo, wg_ref[0], wg_ref[1], _NN_DIMS) + bias_ref[...]
    logg = _log_sigmoid(logits) * (1.0 / GLA_TAU)
    bcum = sum(_mm(tri, part) for part in _split3_bf16(logg))
    end = 0 if reverse else n - 1
    b_end = bcum[end:end + 1, :]
    qe = q * jnp.exp(bcum) * (GLA_DK ** -0.5)
    ke = (k * jnp.exp(-bcum)).astype(BF16)
    kd = k * jnp.exp(b_end - bcum)
    decay_end = jnp.exp(b_end)
    lane_head = lax.broadcasted_iota(jnp.int32, (1, GLA_QK), 1) >> (GLA_DK.bit_length() - 1)
    outs = []
    for h in range(GLA_HEADS):
        hm = lane_head == h
        qh = jnp.where(hm, qe, 0.0).astype(BF16)
        att = jnp.where(mask, _mm(qh, ke, _NT_DIMS), 0.0)
        vh = v[:, h * GLA_DV:(h + 1) * GLA_DV].astype(BF16)
        st = st_ref[h]
        o_h = _mm(att.astype(BF16), vh) + _mm(qh, st.astype(BF16), _NT_DIMS)
        local = _mm(vh, jnp.where(hm, kd, 0.0).astype(BF16), _TN_DIMS)
        st_ref[h] = st * decay_end + local
        outs.append(o_h)
    return jnp.concatenate(outs, axis=1)


def _gla_mixer(p, gate_w, gate_b, n_lat_tiles):
    b, r, _ = p.shape
    rows = SCAN_STEP_ROWS
    n_lat = n_lat_tiles * ROW_TILE // rows
    n_ctx = r // rows - n_lat
    in_specs, o_specs, wgs = [], [], []
    for direction in (0, 1):
        cidx = functools.partial(_scan_chunk, n_lat=n_lat, n_ctx=n_ctx, reverse=direction == 1)
        wg = jnp.zeros((LANES, GLA_QK), F32).at[direction * GLA_GATE_RANK:(direction + 1) * GLA_GATE_RANK].set(
            gate_w[direction])
        wgs.append(jnp.stack(_split_bf16(wg)))
        blk = lambda width, col, cidx=cidx: pl.BlockSpec((b, rows, width), lambda s: (0, cidx(s), col // width))
        in_specs += [blk(GLA_QK, COL_GLA_Q), blk(GLA_QK, COL_GLA_K), blk(GLA_WIDTH, COL_GLA_V), blk(LANES, COL_GLR)]
        o_specs.append(pl.BlockSpec((b, rows, GLA_WIDTH), lambda s, cidx=cidx: (0, cidx(s), 0)))
    in_specs += [pl.BlockSpec((2, 2, LANES, GLA_QK), lambda s: (0, 0, 0, 0)),
                 pl.BlockSpec((2, 1, GLA_QK), lambda s: (0, 0, 0))]
    return pl.pallas_call(
        _gla_scan_kernel,
        grid=(n_lat + n_ctx,),
        in_specs=in_specs,
        out_specs=o_specs,
        out_shape=[jax.ShapeDtypeStruct((b, r, GLA_WIDTH), F32)] * 2,
        scratch_shapes=[pltpu.VMEM((b, 2, GLA_HEADS, GLA_DV, GLA_QK), F32)],
        compiler_params=pltpu.CompilerParams(dimension_semantics=("arbitrary",)),
        name="gla_scan",
    )(p, p, p, p, p, p, p, p, jnp.stack(wgs), gate_b.reshape(2, 1, GLA_QK))


MLA_Q_TILE = 2048
MLA_K_TILES = (768, 256)


def _mla_prep_kernel(cq_ref, ckv_ref, kr_ref, krrot_ref, onec_ref, sinr_ref, gq_ref, gkv_ref,
                     wq_ref, wqr_ref, wk_ref, wv_ref, q_ref, k_ref, v_ref):
    qn = _rms(cq_ref[0], gq_ref[...]).astype(BF16)
    kvn = _rms(ckv_ref[0], gkv_ref[...]).astype(BF16)
    onec, sinr = onec_ref[...], sinr_ref[...]
    k_rope = kr_ref[0] * onec + krrot_ref[0] * sinr
    ones_lane = jnp.where(lax.broadcasted_iota(jnp.int32, (1, LANES), 1) == MLA_V, 1.0, 0.0)
    for h in range(MLA_HEADS):
        qh = _mm(qn, wq_ref[h]) * onec + _mm(qn, wqr_ref[h]) * sinr
        q_ref[0, h] = (qh * MLA_SCALE).astype(BF16)
        k_ref[0, h] = (_mm(kvn, wk_ref[h]) + k_rope).astype(BF16)
        v_ref[0, h] = (_mm(kvn, wv_ref[h]) + ones_lane).astype(BF16)


def _mla_attn_kernel(q_ref, k_ref, v_ref, o_ref, m_ref, acc_ref):
    j = pl.program_id(2)

    @pl.when(j == 0)
    def _():
        m_ref[...] = jnp.full_like(m_ref, -jnp.inf)
        acc_ref[...] = jnp.zeros_like(acc_ref)

    reps = k_ref.shape[2] // LANES
    for h in range(MLA_HEADS):
        s = _mm(q_ref[0, h], k_ref[0, h], _NT_DIMS)
        m_prev = m_ref[h]
        m_new = jnp.maximum(m_prev, jnp.max(s, axis=1, keepdims=True))
        p = jnp.exp((s - jnp.concatenate([m_new] * reps, axis=1)).astype(BF16))
        acc_ref[h] = jnp.exp(m_prev - m_new) * acc_ref[h] + _mm(p, v_ref[0, h])
        m_ref[h] = m_new

    @pl.when(j == pl.num_programs(2) - 1)
    def _():
        outs = []
        for h in range(MLA_HEADS):
            acc = acc_ref[h]
            outs.append(acc[:, :MLA_V] / acc[:, MLA_V:MLA_V + 1])
        o_ref[0] = jnp.concatenate(outs, axis=1)


def _rope_tables(n_lat, n_rows):
    rows = n_lat // GRID_W
    row = jnp.repeat(jnp.arange(rows, dtype=F32), GRID_W)
    col = jnp.tile(jnp.arange(GRID_W, dtype=F32), rows)
    half = MLA_ROPE // 2
    inv = ROPE_BASE ** (-jnp.arange(0, half, 2, dtype=F32) / half)
    ang = jnp.concatenate([row[:, None] * inv, col[:, None] * inv], axis=-1)
    cos = jnp.concatenate([jnp.cos(ang), jnp.ones((n_rows - n_lat, half), F32)], axis=0)
    sin = jnp.concatenate([jnp.sin(ang), jnp.zeros((n_rows - n_lat, half), F32)], axis=0)
    pad = jnp.zeros((n_rows, LANES - MLA_NOPE - MLA_ROPE), F32)
    onec = jnp.concatenate([jnp.ones((n_rows, MLA_NOPE), F32), cos, cos, pad], axis=1)
    sinr = jnp.concatenate([jnp.zeros((n_rows, MLA_NOPE), F32), sin, sin, pad], axis=1)
    return onec, sinr


def _mla_weights(w_uq, w_ukv):
    dqk = MLA_NOPE + MLA_ROPE
    half = MLA_ROPE // 2
    wq = w_uq.reshape(MLA_Q_RANK, MLA_HEADS, dqk).transpose(1, 0, 2)
    rot = jnp.concatenate([jnp.zeros_like(wq[..., :MLA_NOPE]), -wq[..., MLA_NOPE + half:], wq[..., MLA_NOPE:MLA_NOPE + half]],
                          axis=-1)
    padq = lambda w: jnp.pad(w, ((0, 0), (0, 0), (0, LANES - dqk))).astype(BF16)
    wkv = w_ukv.reshape(MLA_KV_RANK, MLA_HEADS, MLA_NOPE + MLA_V).transpose(1, 0, 2)
    padk = lambda w: jnp.pad(w, ((0, 0), (0, 0), (0, LANES - w.shape[-1]))).astype(BF16)
    return padq(wq), padq(rot), padk(wkv[..., :MLA_NOPE]), padk(wkv[..., MLA_NOPE:])


def _mla_attention(q, k, v, q_tile, q_off, n_q, kt, k_off, n_k):
    b = q.shape[0]
    return pl.pallas_call(
        _mla_attn_kernel,
        grid=(b, n_q, n_k),
        in_specs=[pl.BlockSpec((1, MLA_HEADS, q_tile, LANES), lambda i, a, j: (i, 0, q_off + a, 0)),
                  pl.BlockSpec((1, MLA_HEADS, kt, LANES), lambda i, a, j: (i, 0, k_off + j, 0)),
                  pl.BlockSpec((1, MLA_HEADS, kt, LANES), lambda i, a, j: (i, 0, k_off + j, 0))],
        out_specs=pl.BlockSpec((1, q_tile, MLA_WIDTH), lambda i, a, j: (i, a, 0)),
        out_shape=jax.ShapeDtypeStruct((b, n_q * q_tile, MLA_WIDTH), F32),
        scratch_shapes=[pltpu.VMEM((MLA_HEADS, q_tile, LANES), F32), pltpu.VMEM((MLA_HEADS, q_tile, LANES), F32)],
        compiler_params=pltpu.CompilerParams(dimension_semantics=("arbitrary", "arbitrary", "arbitrary")),
        name="mla_attn",
    )(q, k, v)


def _mla_mixer(p, q_norm_g, w_uq, kv_norm_g, w_ukv, n_lat_tiles, ctx_out):
    b, r, _ = p.shape
    nt = r // ROW_TILE
    n_lat = n_lat_tiles * ROW_TILE
    onec, sinr = _rope_tables(n_lat, r)
    wq, wqr, wk, wv = _mla_weights(w_uq, w_ukv)
    blk = lambda width, col: pl.BlockSpec((1, ROW_TILE, width), lambda i, t: (i, t, col // width))
    tab = pl.BlockSpec((ROW_TILE, LANES), lambda i, t: (t, 0))
    full = lambda *shape: pl.BlockSpec(shape, lambda i, t: (0,) * len(shape))
    head_out = pl.BlockSpec((1, MLA_HEADS, ROW_TILE, LANES), lambda i, t: (i, 0, t, 0))
    q, k, v = pl.pallas_call(
        _mla_prep_kernel,
        grid=(b, nt),
        in_specs=[blk(MLA_Q_RANK, COL_CQ), blk(LANES, COL_CKV), blk(LANES, COL_KR), blk(LANES, COL_KRROT), tab, tab,
                  full(1, MLA_Q_RANK), full(1, MLA_KV_RANK),
                  full(MLA_HEADS, MLA_Q_RANK, LANES), full(MLA_HEADS, MLA_Q_RANK, LANES),
                  full(MLA_HEADS, MLA_KV_RANK, LANES), full(MLA_HEADS, MLA_KV_RANK, LANES)],
        out_specs=[head_out] * 3,
        out_shape=[jax.ShapeDtypeStruct((b, MLA_HEADS, r, LANES), BF16)] * 3,
        compiler_params=pltpu.CompilerParams(dimension_semantics=("arbitrary", "arbitrary")),
        name="mla_prep",
    )(p, p, p, p, onec, sinr, q_norm_g.reshape(1, -1), kv_norm_g.reshape(1, -1), wq, wqr, wk, wv)
    q_tile = min(MLA_Q_TILE, n_lat)
    k_tile = next(t for t in MLA_K_TILES if r % t == 0)
    y_lat = _mla_attention(q, k, v, q_tile, 0, n_lat // q_tile, k_tile, 0, r // k_tile)
    y_ctx = None
    if ctx_out:
        n_ctx = r - n_lat
        y_ctx = _mla_attention(q, k, v, n_ctx, n_lat // n_ctx, 1, n_ctx, n_lat // n_ctx, 1)
    return y_lat, y_ctx


def _s5_matrices(a_re, a_im, log_dt, b_re, b_im, c_re, c_im):
    q, ng, ns, nc = S5_CHUNK, S5_NGROUPS, S5_STATE, S5_GROUP
    lam = jnp.minimum(a_re.astype(F32), S5_MAX_RE) + 1j * a_im.astype(F32)
    step = jnp.exp(log_dt.astype(F32))[..., None]
    abar = jnp.exp(lam * step)
    bmat = b_re.astype(F32) + 1j * b_im.astype(F32)
    bbar = ((abar - 1.0) / lam)[..., None] * bmat
    cmat = c_re.astype(F32) + 1j * c_im.astype(F32)
    pw = jnp.exp((lam * step)[..., None] * jnp.arange(q + 1, dtype=F32))
    kern = jnp.einsum('dgcn,dgnl,dgnk->dglck', cmat, pw[..., :q], bbar).real
    ii = jnp.arange(q)
    lag_f = ii[None, :] - ii[:, None]
    gather = lambda kd, lag: jnp.where((lag >= 0)[None, :, :, None, None], kd[:, jnp.clip(lag, 0, q - 1)], 0.0)
    t_f = gather(kern[0], lag_f).transpose(0, 1, 4, 2, 3)
    t_b = gather(kern[1], -lag_f).transpose(0, 1, 4, 2, 3)
    t_sum = (t_f + t_b).reshape(ng, q * nc, q * nc)
    pw_f = pw[0][..., q - 1 - ii]
    pw_b = pw[1][..., ii]
    wst = lambda pwd, bb: jnp.einsum('gnj,gnc->gjcn', pwd, bb).reshape(ng, q * nc, ns)
    wst_f, wst_b = wst(pw_f, bbar[0]), wst(pw_b, bbar[1])
    wout = lambda pwd, cm: jnp.einsum('gcn,gni->gnic', cm, pwd).reshape(ng, ns, q * nc)
    wo_f, wo_b = wout(pw[0][..., ii + 1], cmat[0]), wout(pw[1][..., q - ii], cmat[1])
    aq = pw[..., q]

    def pair_cols(x):
        x = x.reshape(S5_PAIRS, 2, x.shape[1], x.shape[2])
        z = jnp.zeros_like(x[:, 0])
        return jnp.concatenate([jnp.concatenate([x[:, 0], z], axis=2), jnp.concatenate([z, x[:, 1]], axis=2)], axis=1)

    w_local = jnp.concatenate([pair_cols(wst_f.real), pair_cols(wst_f.imag),
                               pair_cols(wst_b.real), pair_cols(wst_b.imag)], axis=2)
    w_out = jnp.concatenate([pair_cols(t_sum), pair_cols(wo_f.real), pair_cols(-wo_f.imag),
                             pair_cols(wo_b.real), pair_cols(-wo_b.imag)], axis=1)
    aq_pair = aq.reshape(2, S5_PAIRS, 2 * ns)
    aq_tab = jnp.concatenate([aq_pair[0].real, aq_pair[0].imag, aq_pair[1].real, aq_pair[1].imag], axis=1)
    return w_local.astype(BF16), w_out.astype(BF16), aq_tab.reshape(S5_PAIRS, 1, 8 * ns).astype(F32)


def _s5_perm():
    cols = S5_CHUNK * S5_WIDTH
    c = jnp.arange(cols, dtype=jnp.int32)
    cc, j = c % S5_GROUP, (c // S5_GROUP) % S5_CHUNK
    g = c // (S5_GROUP * S5_CHUNK)
    per_half = LANES // S5_GROUP
    src = (g // per_half) * (S5_CHUNK * LANES) + j * LANES + (g % per_half) * S5_GROUP + cc
    return jnp.where(c[:, None] == src[None, :], 1.0, 0.0).astype(BF16)


def _s5_pack_kernel(u_ref, o_ref):
    n = o_ref.shape[1]
    for j in range(S5_CHUNK):
        o_ref[0, :, j * LANES:(j + 1) * LANES] = u_ref[0, pl.ds(j, n, stride=S5_CHUNK), :].astype(BF16)


def _s5_unpack_kernel(y_ref, o_ref):
    n = y_ref.shape[1]
    for i in range(S5_CHUNK):
        o_ref[0, pl.ds(i, n, stride=S5_CHUNK), :] = y_ref[0, :, i * LANES:(i + 1) * LANES]


def _s5_local_kernel(u_ref, perm_ref, w_ref, up_ref, s_ref):
    up = _mm(u_ref[0], perm_ref[...]).astype(BF16)
    up_ref[0] = up
    s_ref[0] = _mm(up, w_ref[0])


def _s5_scan_kernel(s3_ref, aq_ref, hs3_ref, *, n_lat, n_ctx, nb):
    s_ref, hs_ref = s3_ref.at[0], hs3_ref.at[0]
    w = 2 * S5_STATE
    aq = aq_ref[0]
    a = [aq[:, i * w:(i + 1) * w] for i in range(4)]
    zero = jnp.zeros((nb, w), F32)
    slab = 8
    cps = slab // nb

    def run_slab(s_re, s_im, a_re, a_im, h_re, h_im, order):
        ent_re, ent_im = [None] * cps, [None] * cps
        for c in order:
            ent_re[c], ent_im[c] = h_re, h_im
            rows = slice(c * nb, (c + 1) * nb)
            h_re, h_im = a_re * h_re - a_im * h_im + s_re[rows], a_re * h_im + a_im * h_re + s_im[rows]
        return jnp.concatenate(ent_re, axis=0), jnp.concatenate(ent_im, axis=0), h_re, h_im

    def body(kk, carry):
        f_re, f_im, b_re, b_im = carry
        rf = pl.multiple_of(_scan_chunk(kk, n_lat // cps, n_ctx // cps, False) * slab, slab)
        rb = pl.multiple_of(_scan_chunk(kk, n_lat // cps, n_ctx // cps, True) * slab, slab)
        e_re, e_im, f_re, f_im = run_slab(s_ref[pl.ds(rf, slab), 0:w], s_ref[pl.ds(rf, slab), w:2 * w],
                                          a[0], a[1], f_re, f_im, range(cps))
        hs_ref[pl.ds(rf, slab), 0:w] = e_re
        hs_ref[pl.ds(rf, slab), w:2 * w] = e_im
        e_re, e_im, b_re, b_im = run_slab(s_ref[pl.ds(rb, slab), 2 * w:3 * w], s_ref[pl.ds(rb, slab), 3 * w:4 * w],
                                          a[2], a[3], b_re, b_im, range(cps - 1, -1, -1))
        hs_ref[pl.ds(rb, slab), 2 * w:3 * w] = e_re
        hs_ref[pl.ds(rb, slab), 3 * w:4 * w] = e_im
        return f_re, f_im, b_re, b_im

    lax.fori_loop(0, (n_lat + n_ctx) // cps, body, (zero, zero, zero, zero))


def _s5_out_kernel(up_ref, hs_ref, w_ref, perm_ref, y_ref):
    @pl.when(pl.program_id(1) == 0)
    def _():
        y_ref[...] = jnp.zeros_like(y_ref)

    kw = up_ref.shape[2]
    y_pair = _mm(up_ref[0], w_ref[0, :kw]) + _mm(hs_ref[0].astype(BF16), w_ref[0, kw:])
    y_hi, y_lo = _split_bf16(y_pair)
    y_ref[0] += _mm(y_hi, perm_ref[...], _NT_DIMS) + _mm(y_lo, perm_ref[...], _NT_DIMS)


def _s5_mixer(p, a_re, a_im, log_dt, b_re, b_im, c_re, c_im, n_lat_tiles):
    b, r, _ = p.shape
    q = S5_CHUNK
    n_chunks = r // q
    cols = q * S5_WIDTH
    kw = 2 * q * S5_GROUP
    w_local, w_out, aq_tab = _s5_matrices(a_re, a_im, log_dt, b_re, b_im, c_re, c_im)
    perm = _s5_perm()
    cp2 = pltpu.CompilerParams(dimension_semantics=("arbitrary", "arbitrary"), vmem_limit_bytes=48 * 2 ** 20)
    pack_rows = next(t for t in S5_PACK_ROWS if r % t == 0)
    cpt = pack_rows // q
    halves = S5_WIDTH // LANES
    cp3 = pltpu.CompilerParams(dimension_semantics=("arbitrary", "arbitrary", "arbitrary"))
    chunk_rows = pl.BlockSpec((1, cpt, q * LANES), lambda i, t, hf: (i, t, hf))
    u_big = pl.pallas_call(
        _s5_pack_kernel,
        grid=(b, r // pack_rows, halves),
        in_specs=[pl.BlockSpec((1, pack_rows, LANES), lambda i, t, hf: (i, t, COL_S5 // LANES + hf))],
        out_specs=chunk_rows,
        out_shape=jax.ShapeDtypeStruct((b, n_chunks, cols), BF16),
        compiler_params=cp3, name="s5_pack",
    )(p)
    all_chunks = pl.BlockSpec((1, n_chunks, cols), lambda i, g: (i, 0, 0))
    col_tile = lambda width: pl.BlockSpec((1, n_chunks, width), lambda i, g: (i, 0, g))
    perm_cols = pl.BlockSpec((cols, kw), lambda i, g: (0, g))
    u_pairs, s_loc = pl.pallas_call(
        _s5_local_kernel,
        grid=(b, S5_PAIRS),
        in_specs=[all_chunks, perm_cols, pl.BlockSpec((1, kw, kw), lambda i, g: (g, 0, 0))],
        out_specs=[col_tile(kw), col_tile(kw)],
        out_shape=[jax.ShapeDtypeStruct((b, n_chunks, cols), BF16), jax.ShapeDtypeStruct((b, n_chunks, cols), F32)],
        compiler_params=cp2, name="s5_local",
    )(u_big, perm, w_local)
    n_lat = n_lat_tiles * ROW_TILE // q
    hs = pl.pallas_call(
        functools.partial(_s5_scan_kernel, n_lat=n_lat, n_ctx=n_chunks - n_lat, nb=1),
        grid=(b, S5_PAIRS),
        in_specs=[col_tile(kw), pl.BlockSpec((1, 1, kw), lambda i, g: (g, 0, 0))],
        out_specs=col_tile(kw),
        out_shape=jax.ShapeDtypeStruct((b, n_chunks, cols), F32),
        compiler_params=cp2, name="s5_scan",
    )(s_loc, aq_tab)
    y_big = pl.pallas_call(
        _s5_out_kernel,
        grid=(b, S5_PAIRS),
        in_specs=[col_tile(kw), col_tile(kw), pl.BlockSpec((1, 2 * kw, kw), lambda i, g: (g, 0, 0)), perm_cols],
        out_specs=all_chunks,
        out_shape=jax.ShapeDtypeStruct((b, n_chunks, cols), F32),
        compiler_params=cp2, name="s5_out",
    )(u_pairs, hs, w_out, perm)
    return pl.pallas_call(
        _s5_unpack_kernel,
        grid=(b, r // pack_rows, halves),
        in_specs=[chunk_rows],
        out_specs=pl.BlockSpec((1, pack_rows, LANES), lambda i, t, hf: (i, t, hf)),
        out_shape=jax.ShapeDtypeStruct((b, r, S5_WIDTH), F32),
        compiler_params=cp3, name="s5_unpack",
    )(y_big)


def _post_kernel(h_ref, xs_ref, z_ref, r_ref, u_ref, ssd_ref, ssd_b_ref, gla_ref, gla_b_ref, mla_ref, s5_ref,
                 ssd_d_ref, ssd_g_ref, gla_g_ref, s5_d_ref, glu_w_ref, glu_b_ref, w_out_ref, mod_ref, o_ref):
    y = ssd_ref[0] + ssd_b_ref[0] + ssd_d_ref[...] * xs_ref[0]
    ssd = _rms(y * _silu(z_ref[0]), ssd_g_ref[...])
    o = gla_ref[0] + gla_b_ref[0]
    lane_head = lax.broadcasted_iota(jnp.int32, (1, GLA_WIDTH), 1) >> (GLA_DV.bit_length() - 1)
    ms = jnp.zeros_like(o)
    for h in range(GLA_HEADS):
        oh = o[:, h * GLA_DV:(h + 1) * GLA_DV]
        ms = jnp.where(lane_head == h, jnp.mean(oh * oh, axis=-1, keepdims=True), ms)
    gla = o * lax.rsqrt(ms + NORM_EPS) * gla_g_ref[...] * _silu(r_ref[0])
    y5 = _gelu_erf(s5_ref[0] + s5_d_ref[...] * u_ref[0])
    s5 = y5 * jax.nn.sigmoid(_mm(y5.astype(BF16), glu_w_ref[...]) + glu_b_ref[...])
    mix_in = jnp.concatenate([ssd, gla, mla_ref[0], s5], axis=1).astype(BF16)
    o_ref[0] = h_ref[0] + mod_ref[0] * _mm(mix_in, w_out_ref[...])


def _post(h, p, ssd_xbc, ssd_y, ssd_yb, gla_o, gla_ob, mla_y, s5_y, ssd_d, ssd_norm_g, gla_norm_g, s5_d, glu_w, glu_b, w_out, mod,
          row_off, mla_off):
    b, rows, d = h.shape
    w = GROUP_WIDTH
    pblk = lambda col: pl.BlockSpec((1, ROW_TILE, w), lambda i, t: (i, row_off + t, col // w))
    yblk = pl.BlockSpec((1, ROW_TILE, w), lambda i, t: (i, row_off + t, 0))
    full = lambda *shape: pl.BlockSpec(shape, lambda i, t: (0,) * len(shape))
    vec = lambda x: x.reshape(1, -1).astype(F32)
    n_mod = mod.shape[0]
    return pl.pallas_call(
        _post_kernel,
        grid=(b, rows // ROW_TILE),
        in_specs=[pl.BlockSpec((1, ROW_TILE, d), lambda i, t: (i, t, 0)),
                  yblk, pblk(COL_Z), pblk(COL_GLA_R), pblk(COL_S5), yblk, yblk, yblk, yblk,
                  pl.BlockSpec((1, ROW_TILE, w), lambda i, t: (i, mla_off + t, 0)), yblk,
                  full(1, w), full(1, w), full(1, w), full(1, w), full(w, w), full(1, w), full(d, d),
                  pl.BlockSpec((1, 1, d), lambda i, t: (jnp.minimum(i, n_mod - 1), 0, 0))],
        out_specs=pl.BlockSpec((1, ROW_TILE, d), lambda i, t: (i, t, 0)),
        out_shape=jax.ShapeDtypeStruct((b, rows, d), F32),
        compiler_params=pltpu.CompilerParams(dimension_semantics=("arbitrary", "arbitrary")),
        name="mix_post",
    )(h, ssd_xbc, p, p, p, ssd_y, ssd_yb, gla_o, gla_ob, mla_y, s5_y,
      vec(jnp.repeat(ssd_d, SSD_HEAD_DIM)), vec(ssd_norm_g), vec(jnp.tile(gla_norm_g, GLA_HEADS)), vec(s5_d),
      glu_w.astype(BF16), vec(glu_b), w_out.astype(BF16), mod)


PEER_ROUTE_TOKENS = 256
PEER_ROUTE_UNROLL = 4
PEER_GATE_TOKENS = 256
PEER_GATE_UNROLL = 64
PEER_GATE_SUBLANES = 8
PEER_DENSE_TOKENS = 512
PEER_DENSE_EXPERTS = 2 * PEER_GATE_SUBLANES * PEER_KEYS
PEER_SLOTS = PEER_HEADS * PEER_TOPK


def _topk_rows(s, k):
    n_rows = s.shape[0]
    rows = lax.broadcasted_iota(jnp.int32, s.shape, 0)
    vals, idxs = [], []
    for _ in range(k):
        m = jnp.max(s, axis=0, keepdims=True)
        idx = jnp.min(jnp.where(s == m, rows, n_rows), axis=0, keepdims=True)
        vals.append(m)
        idxs.append(idx)
        s = jnp.where(rows == idx, -jnp.inf, s)
    return jnp.concatenate(vals, axis=0), jnp.concatenate(idxs, axis=0)


def _select_rows(pos, table):
    out = jnp.zeros(pos.shape, table.dtype)
    for r in range(table.shape[0]):
        out = jnp.where(pos == r, table[r:r + 1, :], out)
    return out


def _peer_route_kernel(h_ref, g_ref, shift_ref, scale_ref, wq_hi_ref, wq_lo_ref, k_hi_ref, k_lo_ref,
                       xn_ref, i1_ref, i2_ref, gate_ref, q_scr, slot_scr):
    xn = _modulated_norm(h_ref[...], g_ref[...], shift_ref[0], scale_ref[0])
    xn_ref[...] = xn.astype(BF16)
    x_hi, x_lo = _split_bf16(xn)
    q_scr[...] = _dot3(wq_hi_ref[...], wq_lo_ref[...], x_hi, x_lo, _NT_DIMS)
    half = PEER_DQ // 2

    def head_body(h, carry):
        base = pl.multiple_of(h * PEER_DQ, PEER_DQ)
        tops = []
        for j in range(2):
            qq = q_scr[pl.ds(base + j * half, half), :]
            q_hi, q_lo = _split_bf16(qq)
            s = _dot3(k_hi_ref[j, h], k_lo_ref[j, h], q_hi, q_lo, _NN_DIMS)
            tops.append(_topk_rows(s, PEER_TOPK))
        (v1, i1), (v2, i2) = tops
        pieces = [v1[a:a + 1, :] + v2[:PEER_TOPK // (a + 1), :] for a in range(PEER_TOPK)]
        n_cand = sum(PEER_TOPK // (a + 1) for a in range(PEER_TOPK))
        pad = -n_cand % 8
        cand = jnp.concatenate(pieces + [jnp.full((pad, v1.shape[1]), -jnp.inf, F32)], axis=0)
        best, pos = _topk_rows(cand, PEER_TOPK)
        e = jnp.exp(best - best[0:1, :])
        gates = e / jnp.sum(e, axis=0, keepdims=True)
        a_idx = jnp.zeros_like(pos)
        start = jnp.zeros_like(pos)
        first = 0
        for a in range(1, PEER_TOPK):
            width = PEER_TOPK // a
            first += width
            reached = pos >= first
            a_idx = a_idx + jnp.where(reached, 1, 0)
            start = start + jnp.where(reached, width, 0)
        row0 = pl.multiple_of(h * PEER_TOPK, PEER_TOPK)
        slot_scr[0, pl.ds(row0, PEER_TOPK), :] = _select_rows(a_idx, i1).astype(F32)
        slot_scr[1, pl.ds(row0, PEER_TOPK), :] = _select_rows(pos - start, i2).astype(F32)
        slot_scr[2, pl.ds(row0, PEER_TOPK), :] = gates
        return carry

    lax.fori_loop(0, PEER_HEADS, head_body, 0, unroll=PEER_ROUTE_UNROLL)
    i1_ref[...] = slot_scr[0].T.astype(jnp.int32)
    i2_ref[...] = slot_scr[1].T.astype(jnp.int32)
    gate_ref[...] = slot_scr[2].T


def _bf16_bits(x):
    return pltpu.bitcast(x.astype(BF16).astype(F32), jnp.uint32)


def _peer_gate_kernel(i1_ref, i2_ref, gate_ref, g_ref):
    rows = lax.broadcasted_iota(jnp.int32, (PEER_KEYS, PEER_SLOTS), 0)
    sub = PEER_GATE_SUBLANES

    def token_body(t, carry):
        a = i1_ref[pl.ds(t, 1), :]
        b = i2_ref[pl.ds(t, 1), :]
        w = gate_ref[pl.ds(t, 1), :]
        lhs = jnp.where(rows == a, w, 0.0).astype(BF16)
        rhs = jnp.where(rows == b, 1.0, 0.0).astype(BF16)
        gt = _mm(lhs, rhs, _NT_DIMS)
        row0 = pl.multiple_of(t * sub, sub)
        for g in range(PEER_KEYS // (2 * sub)):
            lo = gt[2 * sub * g:2 * sub * g + sub]
            hi = gt[2 * sub * g + sub:2 * sub * (g + 1)]
            g_ref[g, pl.ds(row0, sub), :] = (_bf16_bits(lo) >> 16) | _bf16_bits(hi)
        return carry

    lax.fori_loop(0, i1_ref.shape[0], token_body, 0, unroll=PEER_GATE_UNROLL)


def _peer_dense_kernel(xn_ref, u_ref, v_ref, gpk_ref, h_ref, mod_ref, out_g_ref, o_ref, acc_ref, *, out_norm):
    j = pl.program_id(1)

    @pl.when(j == 0)
    def _():
        acc_ref[...] = jnp.zeros_like(acc_ref)

    sub = PEER_GATE_SUBLANES
    xn = xn_ref[...]
    tokens = xn.shape[0]
    words = [gpk_ref[0, pl.ds(r, tokens, stride=sub), :] for r in range(sub)]
    for half in range(2):
        rows = slice(half * sub * PEER_KEYS, (half + 1) * sub * PEER_KEYS)
        hid = _gelu_erf(_mm(xn, u_ref[0, rows, :], _NT_DIMS))
        ys = []
        for r in range(sub):
            bits = (words[r] << 16) if half == 0 else (words[r] & jnp.uint32(0xFFFF0000))
            ys.append(pltpu.bitcast(bits, F32) * hid[:, r * PEER_KEYS:(r + 1) * PEER_KEYS])
        y = jnp.concatenate(ys, axis=1).astype(BF16)
        acc_ref[...] += _mm(y, v_ref[0, rows, :])

    @pl.when(j == pl.num_programs(1) - 1)
    def _():
        out = h_ref[...] + mod_ref[0] * acc_ref[...]
        o_ref[...] = _rms(out, out_g_ref[...]) if out_norm else out


def _peer_layer(h, norm_g, shift, scale, gate_mod, wq_t_hi, wq_t_lo, keys_hi, keys_lo, u_bf, v_bf, layer,
                out_norm_g=None):
    n, d = h.shape
    nb = shift.shape[0]
    rows_per_batch = n // nb
    tr = min(PEER_ROUTE_TOKENS, rows_per_batch)
    full = lambda *shape: pl.BlockSpec(shape, lambda i: (0,) * len(shape))
    per_batch = lambda t: pl.BlockSpec((1, 1, d), lambda i: (i * t // rows_per_batch, 0, 0))
    xn, i1, i2, gate = pl.pallas_call(
        _peer_route_kernel,
        grid=(n // tr,),
        in_specs=[pl.BlockSpec((tr, d), lambda i: (i, 0)), full(1, d), per_batch(tr), per_batch(tr),
                  full(PEER_HEADS * PEER_DQ, d), full(PEER_HEADS * PEER_DQ, d),
                  full(2, PEER_HEADS, PEER_KEYS, PEER_DQ // 2), full(2, PEER_HEADS, PEER_KEYS, PEER_DQ // 2)],
        out_specs=[pl.BlockSpec((tr, d), lambda i: (i, 0))] + [pl.BlockSpec((tr, PEER_SLOTS), lambda i: (i, 0))] * 3,
        out_shape=[jax.ShapeDtypeStruct((n, d), BF16),
                   jax.ShapeDtypeStruct((n, PEER_SLOTS), jnp.int32),
                   jax.ShapeDtypeStruct((n, PEER_SLOTS), jnp.int32),
                   jax.ShapeDtypeStruct((n, PEER_SLOTS), F32)],
        scratch_shapes=[pltpu.VMEM((PEER_HEADS * PEER_DQ, tr), F32), pltpu.VMEM((3, PEER_SLOTS, tr), F32)],
        compiler_params=pltpu.CompilerParams(dimension_semantics=("arbitrary",)),
        name="peer_route",
    )(h, norm_g.reshape(1, d), shift, scale, wq_t_hi, wq_t_lo, keys_hi, keys_lo)

    tg = min(PEER_GATE_TOKENS, n)
    n_planes = PEER_KEYS // (2 * PEER_GATE_SUBLANES)
    slot_spec = pl.BlockSpec((tg, PEER_SLOTS), lambda i: (i, 0))
    gmat = pl.pallas_call(
        _peer_gate_kernel,
        grid=(n // tg,),
        in_specs=[slot_spec, slot_spec, slot_spec],
        out_specs=pl.BlockSpec((n_planes, tg * PEER_GATE_SUBLANES, PEER_KEYS), lambda i: (0, i, 0)),
        out_shape=jax.ShapeDtypeStruct((n_planes, n * PEER_GATE_SUBLANES, PEER_KEYS), jnp.uint32),
        compiler_params=pltpu.CompilerParams(dimension_semantics=("arbitrary",)),
        name="peer_gate",
    )(i1, i2, gate)

    tm = min(PEER_DENSE_TOKENS, rows_per_batch)
    te = PEER_DENSE_EXPERTS
    return pl.pallas_call(
        functools.partial(_peer_dense_kernel, out_norm=out_norm_g is not None),
        grid=(n // tm, PEER_EXPERTS // te),
        in_specs=[pl.BlockSpec((tm, d), lambda i, j: (i, 0)),
                  pl.BlockSpec((1, te, d), lambda i, j: (layer, j, 0)),
                  pl.BlockSpec((1, te, d), lambda i, j: (layer, j, 0)),
                  pl.BlockSpec((1, tm * PEER_GATE_SUBLANES, PEER_KEYS), lambda i, j: (j, i, 0)),
                  pl.BlockSpec((tm, d), lambda i, j: (i, 0)),
                  pl.BlockSpec((1, 1, d), lambda i, j: (i * tm // rows_per_batch, 0, 0)),
                  pl.BlockSpec((1, d), lambda i, j: (0, 0))],
        out_specs=pl.BlockSpec((tm, d), lambda i, j: (i, 0)),
        out_shape=jax.ShapeDtypeStruct((n, d), F32),
        scratch_shapes=[pltpu.VMEM((tm, d), F32)],
        compiler_params=pltpu.CompilerParams(dimension_semantics=("arbitrary", "arbitrary"),
                                             vmem_limit_bytes=52 * 2 ** 20),
        name="peer_dense",
    )(xn, u_bf, v_bf, gmat, h, gate_mod, jnp.ones((1, d), F32) if out_norm_g is None else out_norm_g.reshape(1, d))


def _mix_layer(h_lat, h_ctx, mod_l, mod_c, norm_g, w_in, w_out, ssd, gla, mla, s5, ctx_out):
    b, n_lat, d = h_lat.shape
    n_lat_tiles = n_lat // ROW_TILE
    tab = lambda k: jnp.concatenate([mod_l[k], mod_c[k]], axis=0)
    p = _inproj(h_lat, h_ctx, norm_g, tab(0), tab(1), _pack_w_in(w_in))
    ssd_y, ssd_yb, ssd_xbc = _ssd_mixer(p, ssd["conv_w"], ssd["conv_b"], ssd["a_log"], ssd["dt_bias"], n_lat_tiles)
    gla_o, gla_ob = _gla_mixer(p, gla["gate_w"], gla["gate_b"], n_lat_tiles)
    mla_lat, mla_ctx = _mla_mixer(p, mla["q_norm_g"], mla["w_uq"], mla["kv_norm_g"], mla["w_ukv"], n_lat_tiles, ctx_out)
    s5_y = _s5_mixer(p, s5["a_re"], s5["a_im"], s5["log_dt"], s5["b_re"], s5["b_im"], s5["c_re"], s5["c_im"],
                     n_lat_tiles)
    post = functools.partial(_post, p=p, ssd_xbc=ssd_xbc, ssd_y=ssd_y, ssd_yb=ssd_yb, gla_o=gla_o, gla_ob=gla_ob,
                             s5_y=s5_y, ssd_d=ssd["d"],
                             ssd_norm_g=ssd["norm_g"], gla_norm_g=gla["norm_g"], s5_d=s5["d"],
                             glu_w=s5["glu_w"], glu_b=s5["glu_b"], w_out=w_out)
    new_lat = post(h_lat, mla_y=mla_lat, mod=mod_l[2], row_off=0, mla_off=0)
    new_ctx = None
    if ctx_out:
        new_ctx = post(h_ctx, mla_y=mla_ctx, mod=mod_c[2], row_off=n_lat_tiles, mla_off=0)
    return new_lat, new_ctx


def kernel(x, c, ctx, c_ctx, ada_w, ada_b, norm_mix_g, norm_ffn_g, w_in, w_out,
           ssd_conv_w, ssd_conv_b, ssd_a_log, ssd_dt_bias, ssd_d, ssd_norm_g,
           gla_gate_w, gla_gate_b, gla_norm_g, mla_q_norm_g, mla_w_uq, mla_kv_norm_g, mla_w_ukv,
           s5_a_re, s5_a_im, s5_log_dt, s5_b_re, s5_b_im, s5_c_re, s5_c_im, s5_d, s5_glu_w, s5_glu_b,
           peer_w_q, peer_sub_keys, peer_u, peer_v, final_norm_g):
    h_lat, h_ctx = x, ctx
    cond_lat = jax.nn.silu(c)[:, None, :]
    cond_ctx = jax.nn.silu(c_ctx)[None, None, :]
    u_bf, v_bf = peer_u.astype(BF16), peer_v.astype(BF16)
    for i in range(DEPTH):
        ctx_out = i < DEPTH - 1
        mod_l = jnp.split(cond_lat @ ada_w[i] + ada_b[i], N_MOD, axis=-1)
        mod_c = jnp.split(cond_ctx @ ada_w[i] + ada_b[i], N_MOD, axis=-1)
        ssd = dict(conv_w=ssd_conv_w[i], conv_b=ssd_conv_b[i], a_log=ssd_a_log[i], dt_bias=ssd_dt_bias[i],
                   d=ssd_d[i], norm_g=ssd_norm_g[i])
        gla = dict(gate_w=gla_gate_w[i], gate_b=gla_gate_b[i], norm_g=gla_norm_g[i])
        mla = dict(q_norm_g=mla_q_norm_g[i], w_uq=mla_w_uq[i], kv_norm_g=mla_kv_norm_g[i], w_ukv=mla_w_ukv[i])
        s5 = dict(a_re=s5_a_re[i], a_im=s5_a_im[i], log_dt=s5_log_dt[i], b_re=s5_b_re[i], b_im=s5_b_im[i],
                  c_re=s5_c_re[i], c_im=s5_c_im[i], d=s5_d[i], glu_w=s5_glu_w[i], glu_b=s5_glu_b[i])
        h_lat, h_ctx_new = _mix_layer(h_lat, h_ctx, mod_l, mod_c, norm_mix_g[i], w_in[i], w_out[i],
                                      ssd, gla, mla, s5, ctx_out)
        wq_t_hi, wq_t_lo = _split_bf16(peer_w_q[i].T)
        keys_hi, keys_lo = _split_bf16(peer_sub_keys[i])
        peer = functools.partial(_peer_layer, norm_g=norm_ffn_g[i], wq_t_hi=wq_t_hi, wq_t_lo=wq_t_lo,
                                 keys_hi=keys_hi, keys_lo=keys_lo, u_bf=u_bf, v_bf=v_bf, layer=i)
        h_lat = peer(h_lat.reshape(-1, D_MODEL), shift=mod_l[3], scale=mod_l[4], gate_mod=mod_l[5],
                     out_norm_g=final_norm_g if i == DEPTH - 1 else None).reshape(h_lat.shape)
        if ctx_out:
            h_ctx = peer(h_ctx_new.reshape(-1, D_MODEL), shift=mod_c[3], scale=mod_c[4],
                         gate_mod=mod_c[5]).reshape(h_ctx.shape)
    return h_lat
```

```python
import functools
import jax
import jax.numpy as jnp
from jax import lax
import numpy as np
from jax.experimental import pallas as pl
from jax.experimental.pallas import tpu as pltpu

D_MODEL = 1024
DEPTH = 2
GRID_W = 64
NORM_EPS = 1e-6
N_MOD = 6

GROUP_WIDTH = D_MODEL // 4

SSD_WIDTH = GROUP_WIDTH
SSD_HEAD_DIM = 64
SSD_HEADS = SSD_WIDTH // SSD_HEAD_DIM
SSD_GROUPS = 2
SSD_STATE = 128
SSD_CONV = 5
SSD_CHUNK = 128
SSD_CONV_CH = SSD_WIDTH + 2 * SSD_GROUPS * SSD_STATE
SSD_IN = SSD_WIDTH + SSD_CONV_CH + 2 * SSD_HEADS

GLA_WIDTH = GROUP_WIDTH
GLA_HEADS = 4
GLA_DV = GLA_WIDTH // GLA_HEADS
GLA_DK = GLA_DV // 2
GLA_QK = GLA_HEADS * GLA_DK
GLA_GATE_RANK = 16
GLA_TAU = 16.0
GLA_CHUNK = 64
GLA_IN = 2 * GLA_QK + 2 * GLA_WIDTH + 2 * GLA_GATE_RANK

MLA_WIDTH = GROUP_WIDTH
MLA_HEADS = 4
MLA_V = MLA_WIDTH // MLA_HEADS
MLA_NOPE = 64
MLA_ROPE = 32
MLA_Q_RANK = 256
MLA_KV_RANK = 128
MLA_SCALE = (MLA_NOPE + MLA_ROPE) ** -0.5
ROPE_BASE = 10000.0
MLA_IN = MLA_Q_RANK + MLA_KV_RANK + MLA_ROPE

S5_WIDTH = GROUP_WIDTH
S5_GROUP = 16
S5_NGROUPS = S5_WIDTH // S5_GROUP
S5_STATE = 64
S5_MAX_RE = -1e-4
S5_IN = S5_WIDTH
S5_CHUNK = 16
S5_PAIRS = S5_NGROUPS // 2
S5_PACK_ROWS = (768, 256)

PEER_KEYS = 128
PEER_EXPERTS = PEER_KEYS * PEER_KEYS
PEER_HEADS = 8
PEER_TOPK = 16
PEER_DQ = 128

LANES = 128
ROW_TILE = 256
SCAN_STEP_ROWS = ROW_TILE

F32 = jnp.float32
BF16 = jnp.bfloat16

COL_XS, COL_BM, COL_CM, COL_Z = 0, 256, 512, 768
COL_GLA_V, COL_GLA_R, COL_CQ, COL_S5 = 1024, 1280, 1536, 1792
COL_GLA_Q, COL_GLA_K, COL_CKV, COL_DT, COL_GLR, COL_KR, COL_KRROT = 2048, 2176, 2304, 2432, 2560, 2688, 2816
P_COLS = 2944

_NN_DIMS = (((1,), (0,)), ((), ()))
_NT_DIMS = (((1,), (1,)), ((), ()))
_TN_DIMS = (((0,), (0,)), ((), ()))


def _mm(a, b, dims=_NN_DIMS):
    return lax.dot_general(a, b, dims, preferred_element_type=F32)


def _split_bf16(x):
    hi = x.astype(BF16)
    lo = (x - hi.astype(F32)).astype(BF16)
    return hi, lo


def _split3_bf16(x):
    p1 = x.astype(BF16)
    r1 = x - p1.astype(F32)
    p2 = r1.astype(BF16)
    p3 = (r1 - p2.astype(F32)).astype(BF16)
    return p1, p2, p3


def _dot3(a_hi, a_lo, b_hi, b_lo, dims):
    return _mm(a_hi, b_hi, dims) + _mm(a_hi, b_lo, dims) + _mm(a_lo, b_hi, dims)


def _gelu_erf(x):
    return 0.5 * x * (1.0 + lax.erf(x * (2.0 ** -0.5)))


def _silu(x):
    return x * jax.nn.sigmoid(x)


def _softplus(x):
    return jnp.maximum(x, 0.0) + jnp.log1p(jnp.exp(-jnp.abs(x)))


def _log_sigmoid(x):
    return jnp.minimum(x, 0.0) - jnp.log1p(jnp.exp(-jnp.abs(x)))


def _rms(x, g):
    return x * lax.rsqrt(jnp.mean(x * x, axis=-1, keepdims=True) + NORM_EPS) * g


def _modulated_norm(x, g, shift, scale):
    return _rms(x, g) * (1.0 + scale) + shift


def _causal_mask(n, reverse):
    ri = lax.broadcasted_iota(jnp.int32, (n, n), 0)
    ci = lax.broadcasted_iota(jnp.int32, (n, n), 1)
    return (ci >= ri) if reverse else (ci <= ri)


def _scan_chunk(s, n_lat, n_ctx, reverse):
    if reverse:
        return n_lat + n_ctx - 1 - s
    return jnp.where(s < n_ctx, n_lat + s, s - n_ctx)


def _inproj_kernel(lat_ref, ctx_ref, g_ref, shift_ref, scale_ref, w_ref, o_ref, *, n_lat_tiles):
    t = pl.program_id(1)
    x = jnp.where(t < n_lat_tiles, lat_ref[0], ctx_ref[0])
    xn = _modulated_norm(x, g_ref[...], shift_ref[0], scale_ref[0])
    o_ref[0] = _mm(xn.astype(BF16), w_ref[...])


def _inproj(h_lat, h_ctx, norm_g, shift_tab, scale_tab, w_pad):
    b, n_lat, d = h_lat.shape
    n_lat_tiles = n_lat // ROW_TILE
    n_tiles = n_lat_tiles + h_ctx.shape[1] // ROW_TILE
    mod_spec = pl.BlockSpec((1, 1, d), lambda i, t: (jnp.where(t < n_lat_tiles, i, b), 0, 0))
    return pl.pallas_call(
        functools.partial(_inproj_kernel, n_lat_tiles=n_lat_tiles),
        grid=(b, n_tiles),
        in_specs=[pl.BlockSpec((1, ROW_TILE, d), lambda i, t: (i, jnp.minimum(t, n_lat_tiles - 1), 0)),
                  pl.BlockSpec((1, ROW_TILE, d), lambda i, t: (i, jnp.maximum(t - n_lat_tiles, 0), 0)),
                  pl.BlockSpec((1, d), lambda i, t: (0, 0)), mod_spec, mod_spec,
                  pl.BlockSpec((d, P_COLS), lambda i, t: (0, 0))],
        out_specs=pl.BlockSpec((1, ROW_TILE, P_COLS), lambda i, t: (i, t, 0)),
        out_shape=jax.ShapeDtypeStruct((b, n_tiles * ROW_TILE, P_COLS), F32),
        compiler_params=pltpu.CompilerParams(dimension_semantics=("arbitrary", "arbitrary"),
                                             vmem_limit_bytes=48 * 2 ** 20),
        name="inproj",
    )(h_lat, h_ctx, norm_g.reshape(1, d), shift_tab, scale_tab, w_pad)


def _pack_w_in(w):
    o_ssd, o_gla, o_mla, o_s5 = 0, SSD_IN, SSD_IN + GLA_IN, SSD_IN + GLA_IN + MLA_IN
    out = jnp.zeros((w.shape[0], P_COLS), F32)
    put = lambda out, col, src, width: out.at[:, col:col + width].set(w[:, src:src + width])
    out = put(out, COL_Z, o_ssd, SSD_WIDTH)
    out = put(out, COL_XS, o_ssd + SSD_WIDTH, SSD_CONV_CH)
    out = put(out, COL_DT, o_ssd + SSD_WIDTH + SSD_CONV_CH, 2 * SSD_HEADS)
    out = put(out, COL_GLA_Q, o_gla, GLA_QK)
    out = put(out, COL_GLA_K, o_gla + GLA_QK, GLA_QK)
    out = put(out, COL_GLA_V, o_gla + 2 * GLA_QK, GLA_WIDTH)
    out = put(out, COL_GLA_R, o_gla + 2 * GLA_QK + GLA_WIDTH, GLA_WIDTH)
    out = put(out, COL_GLR, o_gla + 2 * GLA_QK + 2 * GLA_WIDTH, 2 * GLA_GATE_RANK)
    out = put(out, COL_CQ, o_mla, MLA_Q_RANK)
    out = put(out, COL_CKV, o_mla + MLA_Q_RANK, MLA_KV_RANK)
    o_kr = o_mla + MLA_Q_RANK + MLA_KV_RANK
    half = MLA_ROPE // 2
    out = put(out, COL_KR + MLA_NOPE, o_kr, MLA_ROPE)
    out = out.at[:, COL_KRROT + MLA_NOPE:COL_KRROT + MLA_NOPE + half].set(-w[:, o_kr + half:o_kr + MLA_ROPE])
    out = out.at[:, COL_KRROT + MLA_NOPE + half:COL_KRROT + MLA_NOPE + MLA_ROPE].set(w[:, o_kr:o_kr + half])
    out = put(out, COL_S5, o_s5, S5_WIDTH)
    return out.astype(BF16)


def _ssd_prep_kernel(x_ref, prev_ref, next_ref, dt_ref, w_ref, b_ref, bias_ref, xbc_ref, dtc_ref, dtt_ref,
                     *, n_lat_tiles):
    t = pl.program_id(1)
    x = x_ref[0]
    halo = prev_ref.shape[1]
    prev = jnp.where(jnp.logical_and(t > 0, t < n_lat_tiles), prev_ref[0], 0.0)
    nxt = jnp.where(t < n_lat_tiles - 1, next_ref[0], 0.0)
    ext = jnp.concatenate([prev, x, nxt], axis=0)
    rows = ext.shape[0]
    left = SSD_CONV // 2
    acc = jnp.zeros_like(x) + b_ref[...]
    for k in range(SSD_CONV):
        shifted = ext if k == left else pltpu.roll(ext, (left - k) % rows, 0)
        acc = acc + w_ref[k:k + 1, :] * shifted[halo:halo + x.shape[0]]
    xbc_ref[0] = _silu(acc)
    dt = _softplus(dt_ref[0] + bias_ref[...])
    dtc_ref[0] = dt
    dtt_ref[0] = dt.T[:dtt_ref.shape[1]]


def _ssd_scan_kernel(xbc_f_ref, dtc_f_ref, dtt_f_ref, xbc_b_ref, dtc_b_ref, dtt_b_ref, ahr_ref, ahc_ref,
                     yf_ref, yb_ref, state_ref):
    @pl.when(pl.program_id(0) == 0)
    def _():
        state_ref[...] = jnp.zeros_like(state_ref)

    q = SSD_CHUNK
    n_sub = xbc_f_ref.shape[1] // q
    for k in range(n_sub):
        for direction, (xbc_ref, dtc_ref, dtt_ref, y_ref) in enumerate(
                ((xbc_f_ref, dtc_f_ref, dtt_f_ref, yf_ref), (xbc_b_ref, dtc_b_ref, dtt_b_ref, yb_ref))):
            c = k if direction == 0 else n_sub - 1 - k
            rows = slice(c * q, (c + 1) * q)
            for bi in range(xbc_ref.shape[0]):
                y_ref[bi, rows] = _ssd_chunk(xbc_ref[bi, rows], dtc_ref[bi, rows], dtt_ref[bi, :, rows], ahr_ref,
                                             ahc_ref, state_ref.at[bi, direction], direction)


def _ssd_chunk(xbc, dtc, dtt, ahr_ref, ahc_ref, state_ref, direction):
    reverse = direction == 1
    q = SSD_CHUNK
    mask = _causal_mask(q, reverse)
    tri = jnp.where(mask, 1.0, 0.0).astype(BF16)
    xs, bm, cm = xbc[:, :SSD_WIDTH], xbc[:, SSD_WIDTH:SSD_WIDTH + 256], xbc[:, SSD_WIDTH + 256:]
    a_col = dtc * ahr_ref[...]
    a_row = dtt * ahc_ref[...]
    acum_col = sum(_mm(tri, part) for part in _split3_bf16(a_col))
    acum_row = sum(_mm(part, tri, _NT_DIMS) for part in _split3_bf16(a_row))
    end = 0 if reverse else q - 1
    bm_bf, cm_bf = bm.astype(BF16), cm.astype(BF16)
    ys = []
    cb = {}
    for h in range(SSD_HEADS):
        g = h // (SSD_HEADS // SSD_GROUPS)
        gs = slice(g * SSD_STATE, (g + 1) * SSD_STATE)
        if g not in cb:
            cb[g] = _mm(cm_bf[:, gs], bm_bf[:, gs], _NT_DIMS)
        ch = direction * SSD_HEADS + h
        ac = acum_col[:, ch:ch + 1]
        ar = acum_row[ch:ch + 1, :]
        decay = jnp.exp(jnp.where(mask, ac - ar, -jnp.inf))
        xd = xs[:, h * SSD_HEAD_DIM:(h + 1) * SSD_HEAD_DIM] * dtc[:, ch:ch + 1]
        y_diag = _mm((cb[g] * decay).astype(BF16), xd.astype(BF16))
        a_end = ac[end:end + 1, :]
        st_local = _mm((xd * jnp.exp(a_end - ac)).astype(BF16), bm_bf[:, gs], _TN_DIMS)
        hs = state_ref[h]
        y_off = jnp.exp(ac) * _mm(cm_bf[:, gs], hs.astype(BF16), _NT_DIMS)
        state_ref[h] = jnp.exp(a_end) * hs + st_local
        ys.append(y_diag + y_off)
    return jnp.concatenate(ys, axis=1)


def _ssd_mixer(p, conv_w, conv_b, a_log, dt_bias, n_lat_tiles):
    b, r, _ = p.shape
    nt = r // ROW_TILE
    halo = 8
    hb = ROW_TILE // halo
    w8 = jnp.zeros((8, SSD_CONV_CH), F32).at[:SSD_CONV].set(conv_w)
    bias = jnp.zeros((1, LANES), F32).at[0, :2 * SSD_HEADS].set(dt_bias.reshape(-1))
    xbc, dtc, dtt = pl.pallas_call(
        functools.partial(_ssd_prep_kernel, n_lat_tiles=n_lat_tiles),
        grid=(b, nt),
        in_specs=[pl.BlockSpec((1, ROW_TILE, SSD_CONV_CH), lambda i, t: (i, t, 0)),
                  pl.BlockSpec((1, halo, SSD_CONV_CH), lambda i, t: (i, jnp.maximum(t * hb - 1, 0), 0)),
                  pl.BlockSpec((1, halo, SSD_CONV_CH), lambda i, t: (i, jnp.minimum((t + 1) * hb, nt * hb - 1), 0)),
                  pl.BlockSpec((1, ROW_TILE, LANES), lambda i, t: (i, t, COL_DT // LANES)),
                  pl.BlockSpec((8, SSD_CONV_CH), lambda i, t: (0, 0)),
                  pl.BlockSpec((1, SSD_CONV_CH), lambda i, t: (0, 0)),
                  pl.BlockSpec((1, LANES), lambda i, t: (0, 0))],
        out_specs=[pl.BlockSpec((1, ROW_TILE, SSD_CONV_CH), lambda i, t: (i, t, 0)),
                   pl.BlockSpec((1, ROW_TILE, LANES), lambda i, t: (i, t, 0)),
                   pl.BlockSpec((1, 8, ROW_TILE), lambda i, t: (i, 0, t))],
        out_shape=[jax.ShapeDtypeStruct((b, r, SSD_CONV_CH), F32),
                   jax.ShapeDtypeStruct((b, r, LANES), F32),
                   jax.ShapeDtypeStruct((b, 8, r), F32)],
        compiler_params=pltpu.CompilerParams(dimension_semantics=("arbitrary", "arbitrary")),
        name="ssd_prep",
    )(p, p, p, p, w8, conv_b.reshape(1, -1), bias)

    a_head = -jnp.exp(a_log.astype(F32)).reshape(-1)
    ahr = jnp.zeros((1, LANES), F32).at[0, :2 * SSD_HEADS].set(a_head)
    ahc = a_head.reshape(2 * SSD_HEADS, 1)
    blk = SCAN_STEP_ROWS
    n_lat = n_lat_tiles * ROW_TILE // blk
    n_ctx = r // blk - n_lat
    in_specs, y_specs = [], []
    for reverse in (False, True):
        cidx = functools.partial(_scan_chunk, n_lat=n_lat, n_ctx=n_ctx, reverse=reverse)
        in_specs += [pl.BlockSpec((b, blk, SSD_CONV_CH), lambda s, cidx=cidx: (0, cidx(s), 0)),
                     pl.BlockSpec((b, blk, LANES), lambda s, cidx=cidx: (0, cidx(s), 0)),
                     pl.BlockSpec((b, 8, blk), lambda s, cidx=cidx: (0, 0, cidx(s)))]
        y_specs.append(pl.BlockSpec((b, blk, SSD_WIDTH), lambda s, cidx=cidx: (0, cidx(s), 0)))
    in_specs += [pl.BlockSpec((1, LANES), lambda s: (0, 0)), pl.BlockSpec((2 * SSD_HEADS, 1), lambda s: (0, 0))]
    y_f, y_b = pl.pallas_call(
        _ssd_scan_kernel,
        grid=(n_lat + n_ctx,),
        in_specs=in_specs,
        out_specs=y_specs,
        out_shape=[jax.ShapeDtypeStruct((b, r, SSD_WIDTH), F32)] * 2,
        scratch_shapes=[pltpu.VMEM((b, 2, SSD_HEADS, SSD_HEAD_DIM, SSD_STATE), F32)],
        compiler_params=pltpu.CompilerParams(dimension_semantics=("arbitrary",)),
        name="ssd_scan",
    )(xbc, dtc, dtt, xbc, dtc, dtt, ahr, ahc)
    return y_f, y_b, xbc


def _gla_scan_kernel(qf_ref, kf_ref, vf_ref, glrf_ref, qb_ref, kb_ref, vb_ref, glrb_ref, wg_ref, bias_ref,
                     of_ref, ob_ref, st_ref):
    @pl.when(pl.program_id(0) == 0)
    def _():
        st_ref[...] = jnp.zeros_like(st_ref)

    n = GLA_CHUNK
    n_sub = qf_ref.shape[1] // n
    for kk in range(n_sub):
        for direction, (q_ref, k_ref, v_ref, glr_ref, o_ref) in enumerate(
                ((qf_ref, kf_ref, vf_ref, glrf_ref, of_ref), (qb_ref, kb_ref, vb_ref, glrb_ref, ob_ref))):
            c = kk if direction == 0 else n_sub - 1 - kk
            rows = slice(c * n, (c + 1) * n)
            for bi in range(q_ref.shape[0]):
                o_ref[bi, rows] = _gla_chunk(q_ref[bi, rows], k_ref[bi, rows], v_ref[bi, rows], glr_ref[bi, rows],
                                             wg_ref.at[direction], bias_ref.at[direction],
                                             st_ref.at[bi, direction], direction == 1)


def _gla_chunk(q, k, v, glr, wg_ref, bias_ref, st_ref, reverse):
    n = GLA_CHUNK
    mask = _causal_mask(n, reverse)
    tri = jnp.where(mask, 1.0, 0.0).astype(BF16)
    g_hi, g_lo = _split_bf16(glr)
    logits = _dot3(g_hi, g_lo, wg_ref[0], wg_ref[1], _NN_DIMS) + bias_ref[...]
    logg = _log_sigmoid(logits) * (1.0 / GLA_TAU)
    bcum = sum(_mm(tri, part) for part in _split3_bf16(logg))
    end = 0 if reverse else n - 1
    b_end = bcum[end:end + 1, :]
    qe = q * jnp.exp(bcum) * (GLA_DK ** -0.5)
    ke = (k * jnp.exp(-bcum)).astype(BF16)
    kd = k * jnp.exp(b_end - bcum)
    decay_end = jnp.exp(b_end)
    lane_head = lax.broadcasted_iota(jnp.int32, (1, GLA_QK), 1) >> (GLA_DK.bit_length() - 1)
    outs = []
    for h in range(GLA_HEADS):
        hm = lane_head == h
        qh = jnp.where(hm, qe, 0.0).astype(BF16)
        att = jnp.where(mask, _mm(qh, ke, _NT_DIMS), 0.0)
        vh = v[:, h * GLA_DV:(h + 1) * GLA_DV].astype(BF16)
        st = st_ref[h]
        o_h = _mm(att.astype(BF16), vh) + _mm(qh, st.astype(BF16), _NT_DIMS)
        local = _mm(vh, jnp.where(hm, kd, 0.0).astype(BF16), _TN_DIMS)
        st_ref[h] = st * decay_end + local
        outs.append(o_h)
    return jnp.concatenate(outs, axis=1)


def _gla_mixer(p, gate_w, gate_b, n_lat_tiles):
    b, r, _ = p.shape
    rows = SCAN_STEP_ROWS
    n_lat = n_lat_tiles * ROW_TILE // rows
    n_ctx = r // rows - n_lat
    in_specs, o_specs, wgs = [], [], []
    for direction in (0, 1):
        cidx = functools.partial(_scan_chunk, n_lat=n_lat, n_ctx=n_ctx, reverse=direction == 1)
        wg = jnp.zeros((LANES, GLA_QK), F32).at[direction * GLA_GATE_RANK:(direction + 1) * GLA_GATE_RANK].set(
            gate_w[direction])
        wgs.append(jnp.stack(_split_bf16(wg)))
        blk = lambda width, col, cidx=cidx: pl.BlockSpec((b, rows, width), lambda s: (0, cidx(s), col // width))
        in_specs += [blk(GLA_QK, COL_GLA_Q), blk(GLA_QK, COL_GLA_K), blk(GLA_WIDTH, COL_GLA_V), blk(LANES, COL_GLR)]
        o_specs.append(pl.BlockSpec((b, rows, GLA_WIDTH), lambda s, cidx=cidx: (0, cidx(s), 0)))
    in_specs += [pl.BlockSpec((2, 2, LANES, GLA_QK), lambda s: (0, 0, 0, 0)),
                 pl.BlockSpec((2, 1, GLA_QK), lambda s: (0, 0, 0))]
    return pl.pallas_call(
        _gla_scan_kernel,
        grid=(n_lat + n_ctx,),
        in_specs=in_specs,
        out_specs=o_specs,
        out_shape=[jax.ShapeDtypeStruct((b, r, GLA_WIDTH), F32)] * 2,
        scratch_shapes=[pltpu.VMEM((b, 2, GLA_HEADS, GLA_DV, GLA_QK), F32)],
        compiler_params=pltpu.CompilerParams(dimension_semantics=("arbitrary",)),
        name="gla_scan",
    )(p, p, p, p, p, p, p, p, jnp.stack(wgs), gate_b.reshape(2, 1, GLA_QK))


MLA_Q_TILE = 4096
MLA_K_TILES = (768, 256)


def _mla_prep_kernel(cq_ref, ckv_ref, kr_ref, krrot_ref, onec_ref, sinr_ref, gq_ref, gkv_ref,
                     wq_ref, wqr_ref, wk_ref, wv_ref, q_ref, k_ref, v_ref):
    qn = _rms(cq_ref[0], gq_ref[...]).astype(BF16)
    kvn = _rms(ckv_ref[0], gkv_ref[...]).astype(BF16)
    onec, sinr = onec_ref[...], sinr_ref[...]
    k_rope = kr_ref[0] * onec + krrot_ref[0] * sinr
    ones_lane = jnp.where(lax.broadcasted_iota(jnp.int32, (1, LANES), 1) == MLA_V, 1.0, 0.0)
    for h in range(MLA_HEADS):
        qh = _mm(qn, wq_ref[h]) * onec + _mm(qn, wqr_ref[h]) * sinr
        q_ref[0, h] = (qh * MLA_SCALE).astype(BF16)
        k_ref[0, h] = (_mm(kvn, wk_ref[h]) + k_rope).astype(BF16)
        v_ref[0, h] = (_mm(kvn, wv_ref[h]) + ones_lane).astype(BF16)


def _mla_attn_kernel(q_ref, k_ref, v_ref, o_ref, m_ref, acc_ref):
    j = pl.program_id(2)

    @pl.when(j == 0)
    def _():
        m_ref[...] = jnp.full_like(m_ref, -jnp.inf)
        acc_ref[...] = jnp.zeros_like(acc_ref)

    reps = k_ref.shape[2] // LANES
    for h in range(MLA_HEADS):
        s = _mm(q_ref[0, h], k_ref[0, h], _NT_DIMS)
        m_prev = m_ref[h]
        m_new = jnp.maximum(m_prev, jnp.max(s, axis=1, keepdims=True))
        p = jnp.exp((s - jnp.concatenate([m_new] * reps, axis=1)).astype(BF16))
        acc_ref[h] = jnp.exp(m_prev - m_new) * acc_ref[h] + _mm(p, v_ref[0, h])
        m_ref[h] = m_new

    @pl.when(j == pl.num_programs(2) - 1)
    def _():
        outs = []
        for h in range(MLA_HEADS):
            acc = acc_ref[h]
            outs.append(acc[:, :MLA_V] / acc[:, MLA_V:MLA_V + 1])
        o_ref[0] = jnp.concatenate(outs, axis=1)


def _rope_tables(n_lat, n_rows):
    rows = n_lat // GRID_W
    row = jnp.repeat(jnp.arange(rows, dtype=F32), GRID_W)
    col = jnp.tile(jnp.arange(GRID_W, dtype=F32), rows)
    half = MLA_ROPE // 2
    inv = ROPE_BASE ** (-jnp.arange(0, half, 2, dtype=F32) / half)
    ang = jnp.concatenate([row[:, None] * inv, col[:, None] * inv], axis=-1)
    cos = jnp.concatenate([jnp.cos(ang), jnp.ones((n_rows - n_lat, half), F32)], axis=0)
    sin = jnp.concatenate([jnp.sin(ang), jnp.zeros((n_rows - n_lat, half), F32)], axis=0)
    pad = jnp.zeros((n_rows, LANES - MLA_NOPE - MLA_ROPE), F32)
    onec = jnp.concatenate([jnp.ones((n_rows, MLA_NOPE), F32), cos, cos, pad], axis=1)
    sinr = jnp.concatenate([jnp.zeros((n_rows, MLA_NOPE), F32), sin, sin, pad], axis=1)
    return onec, sinr


def _mla_weights(w_uq, w_ukv):
    dqk = MLA_NOPE + MLA_ROPE
    half = MLA_ROPE // 2
    wq = w_uq.reshape(MLA_Q_RANK, MLA_HEADS, dqk).transpose(1, 0, 2)
    rot = jnp.concatenate([jnp.zeros_like(wq[..., :MLA_NOPE]), -wq[..., MLA_NOPE + half:], wq[..., MLA_NOPE:MLA_NOPE + half]],
                          axis=-1)
    padq = lambda w: jnp.pad(w, ((0, 0), (0, 0), (0, LANES - dqk))).astype(BF16)
    wkv = w_ukv.reshape(MLA_KV_RANK, MLA_HEADS, MLA_NOPE + MLA_V).transpose(1, 0, 2)
    padk = lambda w: jnp.pad(w, ((0, 0), (0, 0), (0, LANES - w.shape[-1]))).astype(BF16)
    return padq(wq), padq(rot), padk(wkv[..., :MLA_NOPE]), padk(wkv[..., MLA_NOPE:])


def _mla_attention(q, k, v, q_tile, q_off, n_q, kt, k_off, n_k):
    b = q.shape[0]
    return pl.pallas_call(
        _mla_attn_kernel,
        grid=(b, n_q, n_k),
        in_specs=[pl.BlockSpec((1, MLA_HEADS, q_tile, LANES), lambda i, a, j: (i, 0, q_off + a, 0)),
                  pl.BlockSpec((1, MLA_HEADS, kt, LANES), lambda i, a, j: (i, 0, k_off + j, 0)),
                  pl.BlockSpec((1, MLA_HEADS, kt, LANES), lambda i, a, j: (i, 0, k_off + j, 0))],
        out_specs=pl.BlockSpec((1, q_tile, MLA_WIDTH), lambda i, a, j: (i, a, 0)),
        out_shape=jax.ShapeDtypeStruct((b, n_q * q_tile, MLA_WIDTH), F32),
        scratch_shapes=[pltpu.VMEM((MLA_HEADS, q_tile, LANES), F32), pltpu.VMEM((MLA_HEADS, q_tile, LANES), F32)],
        compiler_params=pltpu.CompilerParams(dimension_semantics=("arbitrary", "arbitrary", "arbitrary"),
                                             vmem_limit_bytes=56 * 2 ** 20),
        name="mla_attn",
    )(q, k, v)


def _mla_mixer(p, q_norm_g, w_uq, kv_norm_g, w_ukv, n_lat_tiles, ctx_out):
    b, r, _ = p.shape
    nt = r // ROW_TILE
    n_lat = n_lat_tiles * ROW_TILE
    onec, sinr = _rope_tables(n_lat, r)
    wq, wqr, wk, wv = _mla_weights(w_uq, w_ukv)
    blk = lambda width, col: pl.BlockSpec((1, ROW_TILE, width), lambda i, t: (i, t, col // width))
    tab = pl.BlockSpec((ROW_TILE, LANES), lambda i, t: (t, 0))
    full = lambda *shape: pl.BlockSpec(shape, lambda i, t: (0,) * len(shape))
    head_out = pl.BlockSpec((1, MLA_HEADS, ROW_TILE, LANES), lambda i, t: (i, 0, t, 0))
    q, k, v = pl.pallas_call(
        _mla_prep_kernel,
        grid=(b, nt),
        in_specs=[blk(MLA_Q_RANK, COL_CQ), blk(LANES, COL_CKV), blk(LANES, COL_KR), blk(LANES, COL_KRROT), tab, tab,
                  full(1, MLA_Q_RANK), full(1, MLA_KV_RANK),
                  full(MLA_HEADS, MLA_Q_RANK, LANES), full(MLA_HEADS, MLA_Q_RANK, LANES),
                  full(MLA_HEADS, MLA_KV_RANK, LANES), full(MLA_HEADS, MLA_KV_RANK, LANES)],
        out_specs=[head_out] * 3,
        out_shape=[jax.ShapeDtypeStruct((b, MLA_HEADS, r, LANES), BF16)] * 3,
        compiler_params=pltpu.CompilerParams(dimension_semantics=("arbitrary", "arbitrary")),
        name="mla_prep",
    )(p, p, p, p, onec, sinr, q_norm_g.reshape(1, -1), kv_norm_g.reshape(1, -1), wq, wqr, wk, wv)
    q_tile = min(MLA_Q_TILE, n_lat)
    k_tile = next(t for t in MLA_K_TILES if r % t == 0)
    y_lat = _mla_attention(q, k, v, q_tile, 0, n_lat // q_tile, k_tile, 0, r // k_tile)
    y_ctx = None
    if ctx_out:
        n_ctx = r - n_lat
        y_ctx = _mla_attention(q, k, v, n_ctx, n_lat // n_ctx, 1, n_ctx, n_lat // n_ctx, 1)
    return y_lat, y_ctx


def _s5_matrices(a_re, a_im, log_dt, b_re, b_im, c_re, c_im):
    q, ng, ns, nc = S5_CHUNK, S5_NGROUPS, S5_STATE, S5_GROUP
    lam = jnp.minimum(a_re.astype(F32), S5_MAX_RE) + 1j * a_im.astype(F32)
    step = jnp.exp(log_dt.astype(F32))[..., None]
    abar = jnp.exp(lam * step)
    bmat = b_re.astype(F32) + 1j * b_im.astype(F32)
    bbar = ((abar - 1.0) / lam)[..., None] * bmat
    cmat = c_re.astype(F32) + 1j * c_im.astype(F32)
    pw = jnp.exp((lam * step)[..., None] * jnp.arange(q + 1, dtype=F32))
    kern = jnp.einsum('dgcn,dgnl,dgnk->dglck', cmat, pw[..., :q], bbar).real
    ii = jnp.arange(q)
    lag_f = ii[None, :] - ii[:, None]
    gather = lambda kd, lag: jnp.where((lag >= 0)[None, :, :, None, None], kd[:, jnp.clip(lag, 0, q - 1)], 0.0)
    t_f = gather(kern[0], lag_f).transpose(0, 1, 4, 2, 3)
    t_b = gather(kern[1], -lag_f).transpose(0, 1, 4, 2, 3)
    t_sum = (t_f + t_b).reshape(ng, q * nc, q * nc)
    pw_f = pw[0][..., q - 1 - ii]
    pw_b = pw[1][..., ii]
    wst = lambda pwd, bb: jnp.einsum('gnj,gnc->gjcn', pwd, bb).reshape(ng, q * nc, ns)
    wst_f, wst_b = wst(pw_f, bbar[0]), wst(pw_b, bbar[1])
    wout = lambda pwd, cm: jnp.einsum('gcn,gni->gnic', cm, pwd).reshape(ng, ns, q * nc)
    wo_f, wo_b = wout(pw[0][..., ii + 1], cmat[0]), wout(pw[1][..., q - ii], cmat[1])
    aq = pw[..., q]

    def pair_cols(x):
        x = x.reshape(S5_PAIRS, 2, x.shape[1], x.shape[2])
        z = jnp.zeros_like(x[:, 0])
        return jnp.concatenate([jnp.concatenate([x[:, 0], z], axis=2), jnp.concatenate([z, x[:, 1]], axis=2)], axis=1)

    w_local = jnp.concatenate([pair_cols(wst_f.real), pair_cols(wst_f.imag),
                               pair_cols(wst_b.real), pair_cols(wst_b.imag)], axis=2)
    w_out = jnp.concatenate([pair_cols(t_sum), pair_cols(wo_f.real), pair_cols(-wo_f.imag),
                             pair_cols(wo_b.real), pair_cols(-wo_b.imag)], axis=1)
    aq_pair = aq.reshape(2, S5_PAIRS, 2 * ns)
    aq_tab = jnp.concatenate([aq_pair[0].real, aq_pair[0].imag, aq_pair[1].real, aq_pair[1].imag], axis=1)
    return w_local.astype(BF16), w_out.astype(BF16), aq_tab.reshape(S5_PAIRS, 1, 8 * ns).astype(F32)


def _s5_perm():
    cols = S5_CHUNK * S5_WIDTH
    c = jnp.arange(cols, dtype=jnp.int32)
    cc, j = c % S5_GROUP, (c // S5_GROUP) % S5_CHUNK
    g = c // (S5_GROUP * S5_CHUNK)
    per_half = LANES // S5_GROUP
    src = (g // per_half) * (S5_CHUNK * LANES) + j * LANES + (g % per_half) * S5_GROUP + cc
    return jnp.where(c[:, None] == src[None, :], 1.0, 0.0).astype(BF16)


def _s5_pack_kernel(u_ref, o_ref):
    n = o_ref.shape[1]
    for j in range(S5_CHUNK):
        o_ref[0, :, j * LANES:(j + 1) * LANES] = u_ref[0, pl.ds(j, n, stride=S5_CHUNK), :].astype(BF16)


def _s5_unpack_kernel(y_ref, o_ref):
    n = y_ref.shape[1]
    for i in range(S5_CHUNK):
        o_ref[0, pl.ds(i, n, stride=S5_CHUNK), :] = y_ref[0, :, i * LANES:(i + 1) * LANES]


def _s5_local_kernel(u_ref, perm_ref, w_ref, up_ref, s_ref):
    up = _mm(u_ref[0], perm_ref[...]).astype(BF16)
    up_ref[0] = up
    s_ref[0] = _mm(up, w_ref[0])


def _s5_scan_kernel(s3_ref, aq_ref, hs3_ref, *, n_lat, n_ctx, nb):
    s_ref, hs_ref = s3_ref.at[0], hs3_ref.at[0]
    w = 2 * S5_STATE
    aq = aq_ref[0]
    a = [aq[:, i * w:(i + 1) * w] for i in range(4)]
    zero = jnp.zeros((nb, w), F32)
    slab = 8
    cps = slab // nb

    def run_slab(s_re, s_im, a_re, a_im, h_re, h_im, order):
        ent_re, ent_im = [None] * cps, [None] * cps
        for c in order:
            ent_re[c], ent_im[c] = h_re, h_im
            rows = slice(c * nb, (c + 1) * nb)
            h_re, h_im = a_re * h_re - a_im * h_im + s_re[rows], a_re * h_im + a_im * h_re + s_im[rows]
        return jnp.concatenate(ent_re, axis=0), jnp.concatenate(ent_im, axis=0), h_re, h_im

    def body(kk, carry):
        f_re, f_im, b_re, b_im = carry
        rf = pl.multiple_of(_scan_chunk(kk, n_lat // cps, n_ctx // cps, False) * slab, slab)
        rb = pl.multiple_of(_scan_chunk(kk, n_lat // cps, n_ctx // cps, True) * slab, slab)
        e_re, e_im, f_re, f_im = run_slab(s_ref[pl.ds(rf, slab), 0:w], s_ref[pl.ds(rf, slab), w:2 * w],
                                          a[0], a[1], f_re, f_im, range(cps))
        hs_ref[pl.ds(rf, slab), 0:w] = e_re
        hs_ref[pl.ds(rf, slab), w:2 * w] = e_im
        e_re, e_im, b_re, b_im = run_slab(s_ref[pl.ds(rb, slab), 2 * w:3 * w], s_ref[pl.ds(rb, slab), 3 * w:4 * w],
                                          a[2], a[3], b_re, b_im, range(cps - 1, -1, -1))
        hs_ref[pl.ds(rb, slab), 2 * w:3 * w] = e_re
        hs_ref[pl.ds(rb, slab), 3 * w:4 * w] = e_im
        return f_re, f_im, b_re, b_im

    lax.fori_loop(0, (n_lat + n_ctx) // cps, body, (zero, zero, zero, zero))


def _s5_out_kernel(up_ref, hs_ref, w_ref, perm_ref, y_ref):
    @pl.when(pl.program_id(1) == 0)
    def _():
        y_ref[...] = jnp.zeros_like(y_ref)

    kw = up_ref.shape[2]
    y_pair = _mm(up_ref[0], w_ref[0, :kw]) + _mm(hs_ref[0].astype(BF16), w_ref[0, kw:])
    y_hi, y_lo = _split_bf16(y_pair)
    y_ref[0] += _mm(y_hi, perm_ref[...], _NT_DIMS) + _mm(y_lo, perm_ref[...], _NT_DIMS)


def _s5_mixer(p, a_re, a_im, log_dt, b_re, b_im, c_re, c_im, n_lat_tiles):
    b, r, _ = p.shape
    q = S5_CHUNK
    n_chunks = r // q
    cols = q * S5_WIDTH
    kw = 2 * q * S5_GROUP
    w_local, w_out, aq_tab = _s5_matrices(a_re, a_im, log_dt, b_re, b_im, c_re, c_im)
    perm = _s5_perm()
    cp2 = pltpu.CompilerParams(dimension_semantics=("arbitrary", "arbitrary"), vmem_limit_bytes=48 * 2 ** 20)
    pack_rows = next(t for t in S5_PACK_ROWS if r % t == 0)
    cpt = pack_rows // q
    halves = S5_WIDTH // LANES
    cp3 = pltpu.CompilerParams(dimension_semantics=("arbitrary", "arbitrary", "arbitrary"))
    chunk_rows = pl.BlockSpec((1, cpt, q * LANES), lambda i, t, hf: (i, t, hf))
    u_big = pl.pallas_call(
        _s5_pack_kernel,
        grid=(b, r // pack_rows, halves),
        in_specs=[pl.BlockSpec((1, pack_rows, LANES), lambda i, t, hf: (i, t, COL_S5 // LANES + hf))],
        out_specs=chunk_rows,
        out_shape=jax.ShapeDtypeStruct((b, n_chunks, cols), BF16),
        compiler_params=cp3, name="s5_pack",
    )(p)
    all_chunks = pl.BlockSpec((1, n_chunks, cols), lambda i, g: (i, 0, 0))
    col_tile = lambda width: pl.BlockSpec((1, n_chunks, width), lambda i, g: (i, 0, g))
    perm_cols = pl.BlockSpec((cols, kw), lambda i, g: (0, g))
    u_pairs, s_loc = pl.pallas_call(
        _s5_local_kernel,
        grid=(b, S5_PAIRS),
        in_specs=[all_chunks, perm_cols, pl.BlockSpec((1, kw, kw), lambda i, g: (g, 0, 0))],
        out_specs=[col_tile(kw), col_tile(kw)],
        out_shape=[jax.ShapeDtypeStruct((b, n_chunks, cols), BF16), jax.ShapeDtypeStruct((b, n_chunks, cols), F32)],
        compiler_params=cp2, name="s5_local",
    )(u_big, perm, w_local)
    n_lat = n_lat_tiles * ROW_TILE // q
    hs = pl.pallas_call(
        functools.partial(_s5_scan_kernel, n_lat=n_lat, n_ctx=n_chunks - n_lat, nb=1),
        grid=(b, S5_PAIRS),
        in_specs=[col_tile(kw), pl.BlockSpec((1, 1, kw), lambda i, g: (g, 0, 0))],
        out_specs=col_tile(kw),
        out_shape=jax.ShapeDtypeStruct((b, n_chunks, cols), F32),
        compiler_params=cp2, name="s5_scan",
    )(s_loc, aq_tab)
    y_big = pl.pallas_call(
        _s5_out_kernel,
        grid=(b, S5_PAIRS),
        in_specs=[col_tile(kw), col_tile(kw), pl.BlockSpec((1, 2 * kw, kw), lambda i, g: (g, 0, 0)), perm_cols],
        out_specs=all_chunks,
        out_shape=jax.ShapeDtypeStruct((b, n_chunks, cols), F32),
        compiler_params=cp2, name="s5_out",
    )(u_pairs, hs, w_out, perm)
    return pl.pallas_call(
        _s5_unpack_kernel,
        grid=(b, r // pack_rows, halves),
        in_specs=[chunk_rows],
        out_specs=pl.BlockSpec((1, pack_rows, LANES), lambda i, t, hf: (i, t, hf)),
        out_shape=jax.ShapeDtypeStruct((b, r, S5_WIDTH), F32),
        compiler_params=cp3, name="s5_unpack",
    )(y_big)


def _post_kernel(h_ref, xs_ref, z_ref, r_ref, u_ref, ssd_ref, ssd_b_ref, gla_ref, gla_b_ref, mla_ref, s5_ref,
                 ssd_d_ref, ssd_g_ref, gla_g_ref, s5_d_ref, glu_w_ref, glu_b_ref, w_out_ref, mod_ref, o_ref):
    y = ssd_ref[0] + ssd_b_ref[0] + ssd_d_ref[...] * xs_ref[0]
    ssd = _rms(y * _silu(z_ref[0]), ssd_g_ref[...])
    o = gla_ref[0] + gla_b_ref[0]
    lane_head = lax.broadcasted_iota(jnp.int32, (1, GLA_WIDTH), 1) >> (GLA_DV.bit_length() - 1)
    ms = jnp.zeros_like(o)
    for h in range(GLA_HEADS):
        oh = o[:, h * GLA_DV:(h + 1) * GLA_DV]
        ms = jnp.where(lane_head == h, jnp.mean(oh * oh, axis=-1, keepdims=True), ms)
    gla = o * lax.rsqrt(ms + NORM_EPS) * gla_g_ref[...] * _silu(r_ref[0])
    y5 = _gelu_erf(s5_ref[0] + s5_d_ref[...] * u_ref[0])
    s5 = y5 * jax.nn.sigmoid(_mm(y5.astype(BF16), glu_w_ref[...]) + glu_b_ref[...])
    mix_in = jnp.concatenate([ssd, gla, mla_ref[0], s5], axis=1).astype(BF16)
    o_ref[0] = h_ref[0] + mod_ref[0] * _mm(mix_in, w_out_ref[...])


def _post(h, p, ssd_xbc, ssd_y, ssd_yb, gla_o, gla_ob, mla_y, s5_y, ssd_d, ssd_norm_g, gla_norm_g, s5_d, glu_w, glu_b, w_out, mod,
          row_off, mla_off):
    b, rows, d = h.shape
    w = GROUP_WIDTH
    pblk = lambda col: pl.BlockSpec((1, ROW_TILE, w), lambda i, t: (i, row_off + t, col // w))
    yblk = pl.BlockSpec((1, ROW_TILE, w), lambda i, t: (i, row_off + t, 0))
    full = lambda *shape: pl.BlockSpec(shape, lambda i, t: (0,) * len(shape))
    vec = lambda x: x.reshape(1, -1).astype(F32)
    n_mod = mod.shape[0]
    return pl.pallas_call(
        _post_kernel,
        grid=(b, rows // ROW_TILE),
        in_specs=[pl.BlockSpec((1, ROW_TILE, d), lambda i, t: (i, t, 0)),
                  yblk, pblk(COL_Z), pblk(COL_GLA_R), pblk(COL_S5), yblk, yblk, yblk, yblk,
                  pl.BlockSpec((1, ROW_TILE, w), lambda i, t: (i, mla_off + t, 0)), yblk,
                  full(1, w), full(1, w), full(1, w), full(1, w), full(w, w), full(1, w), full(d, d),
                  pl.BlockSpec((1, 1, d), lambda i, t: (jnp.minimum(i, n_mod - 1), 0, 0))],
        out_specs=pl.BlockSpec((1, ROW_TILE, d), lambda i, t: (i, t, 0)),
        out_shape=jax.ShapeDtypeStruct((b, rows, d), F32),
        compiler_params=pltpu.CompilerParams(dimension_semantics=("arbitrary", "arbitrary")),
        name="mix_post",
    )(h, ssd_xbc, p, p, p, ssd_y, ssd_yb, gla_o, gla_ob, mla_y, s5_y,
      vec(jnp.repeat(ssd_d, SSD_HEAD_DIM)), vec(ssd_norm_g), vec(jnp.tile(gla_norm_g, GLA_HEADS)), vec(s5_d),
      glu_w.astype(BF16), vec(glu_b), w_out.astype(BF16), mod)


PEER_ROUTE_TOKENS = 256
PEER_ROUTE_UNROLL = 4
PEER_GATE_TOKENS = 256
PEER_GATE_UNROLL = 64
PEER_GATE_SUBLANES = 8
PEER_DENSE_TOKENS = 512
PEER_DENSE_EXPERTS = 2 * PEER_GATE_SUBLANES * PEER_KEYS
PEER_SLOTS = PEER_HEADS * PEER_TOPK


def _topk_rows(s, k):
    n_rows = s.shape[0]
    rows = lax.broadcasted_iota(jnp.int32, s.shape, 0)
    vals, idxs = [], []
    for _ in range(k):
        m = jnp.max(s, axis=0, keepdims=True)
        idx = jnp.min(jnp.where(s == m, rows, n_rows), axis=0, keepdims=True)
        vals.append(m)
        idxs.append(idx)
        s = jnp.where(rows == idx, -jnp.inf, s)
    return jnp.concatenate(vals, axis=0), jnp.concatenate(idxs, axis=0)


def _select_rows(pos, table):
    out = jnp.zeros(pos.shape, table.dtype)
    for r in range(table.shape[0]):
        out = jnp.where(pos == r, table[r:r + 1, :], out)
    return out


def _peer_route_kernel(h_ref, g_ref, shift_ref, scale_ref, wq_hi_ref, wq_lo_ref, k_hi_ref, k_lo_ref,
                       xn_ref, i1_ref, i2_ref, gate_ref, q_scr, slot_scr):
    xn = _modulated_norm(h_ref[...], g_ref[...], shift_ref[0], scale_ref[0])
    xn_ref[...] = xn.astype(BF16)
    x_hi, x_lo = _split_bf16(xn)
    q_scr[...] = _dot3(wq_hi_ref[...], wq_lo_ref[...], x_hi, x_lo, _NT_DIMS)
    half = PEER_DQ // 2

    def head_body(h, carry):
        base = pl.multiple_of(h * PEER_DQ, PEER_DQ)
        tops = []
        for j in range(2):
            qq = q_scr[pl.ds(base + j * half, half), :]
            q_hi, q_lo = _split_bf16(qq)
            s = _dot3(k_hi_ref[j, h], k_lo_ref[j, h], q_hi, q_lo, _NN_DIMS)
            tops.append(_topk_rows(s, PEER_TOPK))
        (v1, i1), (v2, i2) = tops
        pieces = [v1[a:a + 1, :] + v2[:PEER_TOPK // (a + 1), :] for a in range(PEER_TOPK)]
        n_cand = sum(PEER_TOPK // (a + 1) for a in range(PEER_TOPK))
        pad = -n_cand % 8
        cand = jnp.concatenate(pieces + [jnp.full((pad, v1.shape[1]), -jnp.inf, F32)], axis=0)
        best, pos = _topk_rows(cand, PEER_TOPK)
        e = jnp.exp(best - best[0:1, :])
        gates = e / jnp.sum(e, axis=0, keepdims=True)
        a_idx = jnp.zeros_like(pos)
        start = jnp.zeros_like(pos)
        first = 0
        for a in range(1, PEER_TOPK):
            width = PEER_TOPK // a
            first += width
            reached = pos >= first
            a_idx = a_idx + jnp.where(reached, 1, 0)
            start = start + jnp.where(reached, width, 0)
        row0 = pl.multiple_of(h * PEER_TOPK, PEER_TOPK)
        slot_scr[0, pl.ds(row0, PEER_TOPK), :] = _select_rows(a_idx, i1).astype(F32)
        slot_scr[1, pl.ds(row0, PEER_TOPK), :] = _select_rows(pos - start, i2).astype(F32)
        slot_scr[2, pl.ds(row0, PEER_TOPK), :] = gates
        return carry

    lax.fori_loop(0, PEER_HEADS, head_body, 0, unroll=PEER_ROUTE_UNROLL)
    i1_ref[...] = slot_scr[0].T.astype(jnp.int32)
    i2_ref[...] = slot_scr[1].T.astype(jnp.int32)
    gate_ref[...] = slot_scr[2].T


def _bf16_bits(x):
    return pltpu.bitcast(x.astype(BF16).astype(F32), jnp.uint32)


def _peer_gate_kernel(i1_ref, i2_ref, gate_ref, g_ref):
    rows = lax.broadcasted_iota(jnp.int32, (PEER_KEYS, PEER_SLOTS), 0)
    sub = PEER_GATE_SUBLANES

    def token_body(t, carry):
        a = i1_ref[pl.ds(t, 1), :]
        b = i2_ref[pl.ds(t, 1), :]
        w = gate_ref[pl.ds(t, 1), :]
        lhs = jnp.where(rows == a, w, 0.0).astype(BF16)
        rhs = jnp.where(rows == b, 1.0, 0.0).astype(BF16)
        gt = _mm(lhs, rhs, _NT_DIMS)
        row0 = pl.multiple_of(t * sub, sub)
        for g in range(PEER_KEYS // (2 * sub)):
            lo = gt[2 * sub * g:2 * sub * g + sub]
            hi = gt[2 * sub * g + sub:2 * sub * (g + 1)]
            g_ref[g, pl.ds(row0, sub), :] = (_bf16_bits(lo) >> 16) | _bf16_bits(hi)
        return carry

    lax.fori_loop(0, i1_ref.shape[0], token_body, 0, unroll=PEER_GATE_UNROLL)


def _peer_dense_kernel(xn_ref, u_ref, v_ref, gpk_ref, h_ref, mod_ref, out_g_ref, o_ref, acc_ref, *, out_norm):
    j = pl.program_id(1)

    @pl.when(j == 0)
    def _():
        acc_ref[...] = jnp.zeros_like(acc_ref)

    sub = PEER_GATE_SUBLANES
    xn = xn_ref[...]
    tokens = xn.shape[0]
    words = [gpk_ref[0, pl.ds(r, tokens, stride=sub), :] for r in range(sub)]
    for half in range(2):
        rows = slice(half * sub * PEER_KEYS, (half + 1) * sub * PEER_KEYS)
        hid = _gelu_erf(_mm(xn, u_ref[0, rows, :], _NT_DIMS))
        ys = []
        for r in range(sub):
            bits = (words[r] << 16) if half == 0 else (words[r] & jnp.uint32(0xFFFF0000))
            ys.append(pltpu.bitcast(bits, F32) * hid[:, r * PEER_KEYS:(r + 1) * PEER_KEYS])
        y = jnp.concatenate(ys, axis=1).astype(BF16)
        acc_ref[...] += _mm(y, v_ref[0, rows, :])

    @pl.when(j == pl.num_programs(1) - 1)
    def _():
        out = h_ref[...] + mod_ref[0] * acc_ref[...]
        o_ref[...] = _rms(out, out_g_ref[...]) if out_norm else out


def _peer_layer(h, norm_g, shift, scale, gate_mod, wq_t_hi, wq_t_lo, keys_hi, keys_lo, u_bf, v_bf, layer,
                out_norm_g=None):
    n, d = h.shape
    nb = shift.shape[0]
    rows_per_batch = n // nb
    tr = min(PEER_ROUTE_TOKENS, rows_per_batch)
    full = lambda *shape: pl.BlockSpec(shape, lambda i: (0,) * len(shape))
    per_batch = lambda t: pl.BlockSpec((1, 1, d), lambda i: (i * t // rows_per_batch, 0, 0))
    xn, i1, i2, gate = pl.pallas_call(
        _peer_route_kernel,
        grid=(n // tr,),
        in_specs=[pl.BlockSpec((tr, d), lambda i: (i, 0)), full(1, d), per_batch(tr), per_batch(tr),
                  full(PEER_HEADS * PEER_DQ, d), full(PEER_HEADS * PEER_DQ, d),
                  full(2, PEER_HEADS, PEER_KEYS, PEER_DQ // 2), full(2, PEER_HEADS, PEER_KEYS, PEER_DQ // 2)],
        out_specs=[pl.BlockSpec((tr, d), lambda i: (i, 0))] + [pl.BlockSpec((tr, PEER_SLOTS), lambda i: (i, 0))] * 3,
        out_shape=[jax.ShapeDtypeStruct((n, d), BF16),
                   jax.ShapeDtypeStruct((n, PEER_SLOTS), jnp.int32),
                   jax.ShapeDtypeStruct((n, PEER_SLOTS), jnp.int32),
                   jax.ShapeDtypeStruct((n, PEER_SLOTS), F32)],
        scratch_shapes=[pltpu.VMEM((PEER_HEADS * PEER_DQ, tr), F32), pltpu.VMEM((3, PEER_SLOTS, tr), F32)],
        compiler_params=pltpu.CompilerParams(dimension_semantics=("arbitrary",)),
        name="peer_route",
    )(h, norm_g.reshape(1, d), shift, scale, wq_t_hi, wq_t_lo, keys_hi, keys_lo)

    tg = min(PEER_GATE_TOKENS, n)
    n_planes = PEER_KEYS // (2 * PEER_GATE_SUBLANES)
    slot_spec = pl.BlockSpec((tg, PEER_SLOTS), lambda i: (i, 0))
    gmat = pl.pallas_call(
        _peer_gate_kernel,
        grid=(n // tg,),
        in_specs=[slot_spec, slot_spec, slot_spec],
        out_specs=pl.BlockSpec((n_planes, tg * PEER_GATE_SUBLANES, PEER_KEYS), lambda i: (0, i, 0)),
        out_shape=jax.ShapeDtypeStruct((n_planes, n * PEER_GATE_SUBLANES, PEER_KEYS), jnp.uint32),
        compiler_params=pltpu.CompilerParams(dimension_semantics=("arbitrary",)),
        name="peer_gate",
    )(i1, i2, gate)

    tm = min(PEER_DENSE_TOKENS, rows_per_batch)
    te = PEER_DENSE_EXPERTS
    return pl.pallas_call(
        functools.partial(_peer_dense_kernel, out_norm=out_norm_g is not None),
        grid=(n // tm, PEER_EXPERTS // te),
        in_specs=[pl.BlockSpec((tm, d), lambda i, j: (i, 0)),
                  pl.BlockSpec((1, te, d), lambda i, j: (layer, j, 0)),
                  pl.BlockSpec((1, te, d), lambda i, j: (layer, j, 0)),
                  pl.BlockSpec((1, tm * PEER_GATE_SUBLANES, PEER_KEYS), lambda i, j: (j, i, 0)),
                  pl.BlockSpec((tm, d), lambda i, j: (i, 0)),
                  pl.BlockSpec((1, 1, d), lambda i, j: (i * tm // rows_per_batch, 0, 0)),
                  pl.BlockSpec((1, d), lambda i, j: (0, 0))],
        out_specs=pl.BlockSpec((tm, d), lambda i, j: (i, 0)),
        out_shape=jax.ShapeDtypeStruct((n, d), F32),
        scratch_shapes=[pltpu.VMEM((tm, d), F32)],
        compiler_params=pltpu.CompilerParams(dimension_semantics=("arbitrary", "arbitrary"),
                                             vmem_limit_bytes=52 * 2 ** 20),
        name="peer_dense",
    )(xn, u_bf, v_bf, gmat, h, gate_mod, jnp.ones((1, d), F32) if out_norm_g is None else out_norm_g.reshape(1, d))


def _mix_layer(h_lat, h_ctx, mod_l, mod_c, norm_g, w_in, w_out, ssd, gla, mla, s5, ctx_out):
    b, n_lat, d = h_lat.shape
    n_lat_tiles = n_lat // ROW_TILE
    tab = lambda k: jnp.concatenate([mod_l[k], mod_c[k]], axis=0)
    p = _inproj(h_lat, h_ctx, norm_g, tab(0), tab(1), _pack_w_in(w_in))
    ssd_y, ssd_yb, ssd_xbc = _ssd_mixer(p, ssd["conv_w"], ssd["conv_b"], ssd["a_log"], ssd["dt_bias"], n_lat_tiles)
    gla_o, gla_ob = _gla_mixer(p, gla["gate_w"], gla["gate_b"], n_lat_tiles)
    mla_lat, mla_ctx = _mla_mixer(p, mla["q_norm_g"], mla["w_uq"], mla["kv_norm_g"], mla["w_ukv"], n_lat_tiles, ctx_out)
    s5_y = _s5_mixer(p, s5["a_re"], s5["a_im"], s5["log_dt"], s5["b_re"], s5["b_im"], s5["c_re"], s5["c_im"],
                     n_lat_tiles)
    post = functools.partial(_post, p=p, ssd_xbc=ssd_xbc, ssd_y=ssd_y, ssd_yb=ssd_yb, gla_o=gla_o, gla_ob=gla_ob,
                             s5_y=s5_y, ssd_d=ssd["d"],
                             ssd_norm_g=ssd["norm_g"], gla_norm_g=gla["norm_g"], s5_d=s5["d"],
                             glu_w=s5["glu_w"], glu_b=s5["glu_b"], w_out=w_out)
    new_lat = post(h_lat, mla_y=mla_lat, mod=mod_l[2], row_off=0, mla_off=0)
    new_ctx = None
    if ctx_out:
        new_ctx = post(h_ctx, mla_y=mla_ctx, mod=mod_c[2], row_off=n_lat_tiles, mla_off=0)
    return new_lat, new_ctx


def kernel(x, c, ctx, c_ctx, ada_w, ada_b, norm_mix_g, norm_ffn_g, w_in, w_out,
           ssd_conv_w, ssd_conv_b, ssd_a_log, ssd_dt_bias, ssd_d, ssd_norm_g,
           gla_gate_w, gla_gate_b, gla_norm_g, mla_q_norm_g, mla_w_uq, mla_kv_norm_g, mla_w_ukv,
           s5_a_re, s5_a_im, s5_log_dt, s5_b_re, s5_b_im, s5_c_re, s5_c_im, s5_d, s5_glu_w, s5_glu_b,
           peer_w_q, peer_sub_keys, peer_u, peer_v, final_norm_g):
    h_lat, h_ctx = x, ctx
    cond_lat = jax.nn.silu(c)[:, None, :]
    cond_ctx = jax.nn.silu(c_ctx)[None, None, :]
    u_bf, v_bf = peer_u.astype(BF16), peer_v.astype(BF16)
    for i in range(DEPTH):
        ctx_out = i < DEPTH - 1
        mod_l = jnp.split(cond_lat @ ada_w[i] + ada_b[i], N_MOD, axis=-1)
        mod_c = jnp.split(cond_ctx @ ada_w[i] + ada_b[i], N_MOD, axis=-1)
        ssd = dict(conv_w=ssd_conv_w[i], conv_b=ssd_conv_b[i], a_log=ssd_a_log[i], dt_bias=ssd_dt_bias[i],
                   d=ssd_d[i], norm_g=ssd_norm_g[i])
        gla = dict(gate_w=gla_gate_w[i], gate_b=gla_gate_b[i], norm_g=gla_norm_g[i])
        mla = dict(q_norm_g=mla_q_norm_g[i], w_uq=mla_w_uq[i], kv_norm_g=mla_kv_norm_g[i], w_ukv=mla_w_ukv[i])
        s5 = dict(a_re=s5_a_re[i], a_im=s5_a_im[i], log_dt=s5_log_dt[i], b_re=s5_b_re[i], b_im=s5_b_im[i],
                  c_re=s5_c_re[i], c_im=s5_c_im[i], d=s5_d[i], glu_w=s5_glu_w[i], glu_b=s5_glu_b[i])
        h_lat, h_ctx_new = _mix_layer(h_lat, h_ctx, mod_l, mod_c, norm_mix_g[i], w_in[i], w_out[i],
                                      ssd, gla, mla, s5, ctx_out)
        wq_t_hi, wq_t_lo = _split_bf16(peer_w_q[i].T)
        keys_hi, keys_lo = _split_bf16(peer_sub_keys[i])
        peer = functools.partial(_peer_layer, norm_g=norm_ffn_g[i], wq_t_hi=wq_t_hi, wq_t_lo=wq_t_lo,
                                 keys_hi=keys_hi, keys_lo=keys_lo, u_bf=u_bf, v_bf=v_bf, layer=i)
        h_lat = peer(h_lat.reshape(-1, D_MODEL), shift=mod_l[3], scale=mod_l[4], gate_mod=mod_l[5],
                     out_norm_g=final_norm_g if i == DEPTH - 1 else None).reshape(h_lat.shape)
        if ctx_out:
            h_ctx = peer(h_ctx_new.reshape(-1, D_MODEL), shift=mod_c[3], scale=mod_c[4],
                         gate_mod=mod_c[5]).reshape(h_ctx.shape)
    return h_lat
```
